```python
import math
import jax
import jax.numpy as jnp
from jax import lax
import numpy as np

D_MODEL = 1024
BATCH = 2
SEQ = 16384
DEPTH = 4

CTX_LEN = 256
GRID_W = 64
D_MIX = D_MODEL
N_MIXERS = 4
GROUP_W = D_MIX // N_MIXERS
HEAD_DIM = 64
N_HEADS_GROUP = GROUP_W // HEAD_DIM
EPS = 1e-6
NEG_INF = -1e30
N_MOD = 6
CONV_W = 4
CONV_PAD_L = CONV_W // 2
LRU_C = 8.0
NA_WIN_ROWS = 8
NA_WIN_COLS = 16
NA_KEY_BAND = 2 * NA_WIN_COLS
SSD_STATE = 128
SSD_GROUPS = 2
SSD_CHUNK = 128
GDN_CHUNK = 64
ROPE_BASE = 10000.0
ROPE_AXIS_DIM = HEAD_DIM // 2
N_EXPERTS = 64
N_EXPERT_GROUPS = 8
TOPK_GROUPS = 4
TOP_K = 8
D_EXPERT = 256
D_SHARED = 256
ROUTED_SCALE = 2.5
MOE_BLOCK = 256

IN_SPLITS = (
    GROUP_W, GROUP_W,
    GROUP_W, GROUP_W, GROUP_W,
    GROUP_W, SSD_GROUPS * SSD_STATE, SSD_GROUPS * SSD_STATE,
    GROUP_W, 2 * N_HEADS_GROUP,
    GROUP_W, GROUP_W, GROUP_W, GROUP_W,
    2 * N_HEADS_GROUP, 2 * N_HEADS_GROUP,
)
D_IN = sum(IN_SPLITS)

kernel_name = 'hybrid_dit_parallel_groups_moe'


def rms_norm(x, g):
    xf = x.astype(jnp.float32)
    y = xf * lax.rsqrt(jnp.mean(xf * xf, axis=-1, keepdims=True) + EPS)
    return (y * g.astype(jnp.float32)).astype(x.dtype)


def l2_norm(x):
    return x * lax.rsqrt(jnp.sum(x * x, axis=-1, keepdims=True) + EPS)


def flip_seq(t):
    return jnp.flip(t, axis=1)


def keep_seq(t):
    return t


def dwconv(x, w, b=None):
    y = lax.conv_general_dilated(
        x, w[:, None, :].astype(x.dtype), window_strides=(1,),
        padding=[(CONV_PAD_L, CONV_W - 1 - CONV_PAD_L)],
        dimension_numbers=('NWC', 'WIO', 'NWC'), feature_group_count=x.shape[-1])
    return y if b is None else y + b.astype(x.dtype)


def split_in(p):
    return jnp.split(p, np.cumsum(IN_SPLITS)[:-1].tolist(), axis=-1)


def axial_rope(seq):
    t = jnp.arange(seq)
    row = (t // GRID_W).astype(jnp.float32)
    col = (t % GRID_W).astype(jnp.float32)
    inv = ROPE_BASE ** (-jnp.arange(0, ROPE_AXIS_DIM, 2, dtype=jnp.float32) / ROPE_AXIS_DIM)
    ang_r = row[:, None] * inv
    ang_c = col[:, None] * inv
    return (jnp.cos(ang_r), jnp.sin(ang_r), jnp.cos(ang_c), jnp.sin(ang_c))


def rope_half(x, cos, sin):
    m = x.shape[-1] // 2
    x1, x2 = x[..., :m], x[..., m:]
    cos, sin = cos[None, :, None, :], sin[None, :, None, :]
    return jnp.concatenate([x1 * cos - x2 * sin, x1 * sin + x2 * cos], axis=-1)


def rope2d(x, tabs):
    cr, sr, cc, sc = tabs
    return jnp.concatenate([rope_half(x[..., :ROPE_AXIS_DIM], cr, sr),
                            rope_half(x[..., ROPE_AXIS_DIM:], cc, sc)], axis=-1)


def linear_scan(a, b, h0):
    def combine(e1, e2):
        a1, b1 = e1
        a2, b2 = e2
        return a1 * a2, a2 * b1 + b2
    a_cum, h_zero = lax.associative_scan(combine, (a, b), axis=1)
    h = h_zero + a_cum * h0[:, None, :]
    return h, h[:, -1]


def lru_coeffs(u, wa, ba, wx, bx, lam):
    uf = u.astype(jnp.float32)
    ub = uf.reshape(*uf.shape[:-1], N_HEADS_GROUP, HEAD_DIM)
    r = jax.nn.sigmoid(jnp.einsum('blhi,hij->blhj', ub, wa.astype(jnp.float32)).reshape(uf.shape)
                       + ba.astype(jnp.float32))
    i = jax.nn.sigmoid(jnp.einsum('blhi,hij->blhj', ub, wx.astype(jnp.float32)).reshape(uf.shape)
                       + bx.astype(jnp.float32))
    log_a = -LRU_C * r * jax.nn.softplus(-lam.astype(jnp.float32))
    a = jnp.exp(log_a)
    b = jnp.sqrt(-jnp.expm1(2.0 * log_a)) * (i * uf)
    return a, b


def rglru_mixer(sides, conv_w, conv_b, wa, ba, wx, bx, lam):
    us = [dwconv(xb, conv_w, conv_b) for xb, _ in sides]
    ys = [None, None]
    for d in range(2):
        flip = flip_seq if d else keep_seq
        h = jnp.zeros((us[0].shape[0], GROUP_W), jnp.float32)
        for i, u in enumerate(us):
            a, b = lru_coeffs(u, wa[d], ba[d], wx[d], bx[d], lam[d])
            hs, h = linear_scan(flip(a), flip(b), h)
            ys[i] = flip(hs) if d == 0 else ys[i] + flip(hs)
    return [(y * jax.nn.gelu(g.astype(jnp.float32))).astype(g.dtype) for y, (_, g) in zip(ys, sides)]


def na_mixer(sides, bias_table):
    (q_c, k_c, v_c), (q_l, k_l, v_l) = sides
    bsz, n_ctx, _ = q_c.shape
    seq = q_l.shape[1]
    rows = seq // GRID_W
    wr = min(NA_WIN_ROWS, rows)
    n_cb = GRID_W // NA_WIN_COLS
    scale = HEAD_DIM ** -0.5

    def heads(t):
        return t.reshape(*t.shape[:-1], N_HEADS_GROUP, HEAD_DIM)

    q_c, k_c, v_c = heads(q_c), heads(k_c), heads(v_c)
    s_cc = jnp.einsum('bqhd,bkhd->bhqk', q_c, k_c).astype(jnp.float32) * scale
    out_c = jnp.einsum('bhqk,bkhd->bqhd', jax.nn.softmax(s_cc, axis=-1).astype(v_c.dtype), v_c)
    out_c = out_c.reshape(bsz, n_ctx, GROUP_W)

    qcol = np.arange(GRID_W).reshape(n_cb, NA_WIN_COLS)
    band0 = np.clip(np.arange(n_cb) * NA_WIN_COLS - NA_WIN_COLS // 2, 0, GRID_W - NA_KEY_BAND)
    kcol = band0[:, None] + np.arange(NA_KEY_BAND)
    win0 = np.clip(qcol - NA_WIN_COLS // 2, 0, GRID_W - NA_WIN_COLS)
    col_mask = ((kcol[:, None, :] >= win0[..., None])
                & (kcol[:, None, :] < win0[..., None] + NA_WIN_COLS))
    dc_idx = np.clip(kcol[:, None, :] - qcol[..., None] + NA_WIN_COLS - 1, 0, 2 * NA_WIN_COLS - 2)
    bias_cols = bias_table.astype(jnp.float32)[:, :, dc_idx]

    qg = heads(q_l).reshape(bsz, rows, n_cb, NA_WIN_COLS, N_HEADS_GROUP, HEAD_DIM)
    kb = heads(k_l).reshape(bsz, rows, GRID_W, N_HEADS_GROUP, HEAD_DIM)[:, :, kcol]
    vb = heads(v_l).reshape(bsz, rows, GRID_W, N_HEADS_GROUP, HEAD_DIM)[:, :, kcol]
    n_loc = wr * NA_KEY_BAND

    def row_block(args):
        r, q_r = args
        r0 = jnp.clip(r - wr // 2, 0, rows - wr)
        k_r = lax.dynamic_slice_in_dim(kb, r0, wr, axis=1)
        v_r = lax.dynamic_slice_in_dim(vb, r0, wr, axis=1)
        dr = r0 + jnp.arange(wr) - r + NA_WIN_ROWS - 1
        bias = jnp.take(bias_cols, dr, axis=1).transpose(0, 2, 3, 1, 4)
        s_loc = jnp.einsum('bjqhd,byjkhd->bhjqyk', q_r, k_r).astype(jnp.float32) * scale + bias
        s_loc = jnp.where(col_mask[:, :, None, :], s_loc, NEG_INF)
        s_ctx = jnp.einsum('bjqhd,bchd->bhjqc', q_r, k_c).astype(jnp.float32) * scale
        s_all = jnp.concatenate([s_loc.reshape(*s_loc.shape[:4], n_loc), s_ctx], axis=-1)
        p = jax.nn.softmax(s_all, axis=-1).astype(v_l.dtype)
        p_loc = p[..., :n_loc].reshape(s_loc.shape)
        o = (jnp.einsum('bhjqyk,byjkhd->bjqhd', p_loc, v_r)
             + jnp.einsum('bhjqc,bchd->bjqhd', p[..., n_loc:], v_c))
        return o.reshape(bsz, GRID_W, N_HEADS_GROUP, HEAD_DIM)

    out = lax.map(row_block, (jnp.arange(rows), jnp.moveaxis(qg, 1, 0)))
    out_l = jnp.moveaxis(out, 0, 1).reshape(bsz, seq, GROUP_W)
    return [out_c, out_l]


def ssd_chunked(x, dt, a, bm, cm, h0):
    bsz, n, nh, hp = x.shape
    ns = bm.shape[-1]
    q = SSD_CHUNK
    nc = n // q
    x = x.reshape(bsz, nc, q, nh, hp)
    dt = dt.reshape(bsz, nc, q, nh)
    bm = bm.reshape(bsz, nc, q, nh, ns)
    cm = cm.reshape(bsz, nc, q, nh, ns)
    acum = jnp.cumsum(dt * a, axis=2)
    tril = jnp.tril(jnp.ones((q, q), bool))
    seg = acum[:, :, :, None, :] - acum[:, :, None, :, :]
    lmat = jnp.exp(jnp.where(tril[None, None, :, :, None], seg, -jnp.inf))
    xdt = x * dt[..., None]
    scores = jnp.einsum('bcihn,bcjhn->bcijh', cm, bm) * lmat
    y_diag = jnp.einsum('bcijh,bcjhp->bcihp', scores, xdt)
    decay_end = jnp.exp(acum[:, :, -1:, :] - acum)
    states = jnp.einsum('bcjhn,bcjhp->bchpn', bm * decay_end[..., None], xdt)
    chunk_decay = jnp.exp(acum[:, :, -1, :])

    def step(h, inp):
        st, dec = inp
        return h * dec[..., None, None] + st, h

    h_last, h_in = lax.scan(step, h0, (jnp.moveaxis(states, 1, 0), jnp.moveaxis(chunk_decay, 1, 0)))
    h_in = jnp.moveaxis(h_in, 0, 1)
    y_off = jnp.einsum('bcihn,bchpn->bcihp', cm * jnp.exp(acum)[..., None], h_in)
    return (y_diag + y_off).reshape(bsz, n, nh, hp), h_last


def ssd_mixer(sides, conv_w, conv_b, a_log, dt_bias, d_skip, norm_w):
    nh = N_HEADS_GROUP

    def prep(x, bm, cm, z, dt_raw):
        xbc = jax.nn.silu(dwconv(jnp.concatenate([x, bm, cm], axis=-1), conv_w, conv_b))
        bsz, n = x.shape[:2]
        x, bm, cm = jnp.split(xbc, [GROUP_W, GROUP_W + SSD_GROUPS * SSD_STATE], axis=-1)
        rep = nh // SSD_GROUPS
        xh = x.reshape(bsz, n, nh, HEAD_DIM).astype(jnp.float32)
        bh = jnp.repeat(bm.reshape(bsz, n, SSD_GROUPS, SSD_STATE), rep, axis=2).astype(jnp.float32)
        ch = jnp.repeat(cm.reshape(bsz, n, SSD_GROUPS, SSD_STATE), rep, axis=2).astype(jnp.float32)
        return xh, bh, ch, z, dt_raw.astype(jnp.float32)

    prepped = [prep(*s) for s in sides]
    ys = [p[0] * d_skip.astype(jnp.float32)[:, None] for p in prepped]
    bsz = prepped[0][0].shape[0]
    for d in range(2):
        flip = flip_seq if d else keep_seq
        a = -jnp.exp(a_log[d].astype(jnp.float32))
        h = jnp.zeros((bsz, nh, HEAD_DIM, SSD_STATE), jnp.float32)
        for i, (xh, bh, ch, _, dt_raw) in enumerate(prepped):
            dt = jax.nn.softplus(dt_raw[..., d * nh:(d + 1) * nh] + dt_bias[d].astype(jnp.float32))
            y, h = ssd_chunked(flip(xh), flip(dt), a, flip(bh), flip(ch), h)
            ys[i] = ys[i] + flip(y)
    outs = []
    for (xh, _, _, z, _), y in zip(prepped, ys):
        y = y.reshape(*xh.shape[:2], GROUP_W) * jax.nn.silu(z.astype(jnp.float32))
        outs.append(rms_norm(y, norm_w).astype(z.dtype))
    return outs


def gdn_chunked(q, k, v, g, beta, s0):
    bsz, n, nh, dk = q.shape
    dv = v.shape[-1]
    cl = GDN_CHUNK
    nc = n // cl

    def chunks(t):
        return t.reshape(bsz, nc, cl, nh, -1).transpose(0, 3, 1, 2, 4)

    q = chunks(q) * (HEAD_DIM ** -0.5)
    k, v = chunks(k), chunks(v)
    g = g.reshape(bsz, nc, cl, nh).transpose(0, 3, 1, 2)
    beta = beta.reshape(bsz, nc, cl, nh).transpose(0, 3, 1, 2)
    gc = jnp.cumsum(g, axis=-1)
    tril = jnp.tril(jnp.ones((cl, cl), bool))
    decay = jnp.exp(jnp.where(tril, gc[..., :, None] - gc[..., None, :], -jnp.inf))
    kb = k * beta[..., None]
    strict = jnp.tril(jnp.ones((cl, cl), bool), -1)
    a_str = jnp.where(strict, jnp.einsum('bhnik,bhnjk->bhnij', kb, k) * decay, 0.0)
    rhs = jnp.concatenate([v * beta[..., None], kb * jnp.exp(gc)[..., None]], axis=-1)
    sol = lax.linalg.triangular_solve(a_str + jnp.eye(cl, dtype=jnp.float32), rhs,
                                      left_side=True, lower=True)
    u, w = sol[..., :dv], sol[..., dv:]
    qk = jnp.einsum('bhnik,bhnjk->bhnij', q, k) * decay
    qg = q * jnp.exp(gc)[..., None]
    kd = k * jnp.exp(gc[..., -1:] - gc)[..., None]
    glast = jnp.exp(gc[..., -1])

    def step(s, inp):
        u_i, w_i, qk_i, qg_i, kd_i, gl_i = inp
        v_new = u_i - jnp.einsum('bhck,bhkv->bhcv', w_i, s)
        o_i = jnp.einsum('bhck,bhkv->bhcv', qg_i, s) + jnp.einsum('bhij,bhjv->bhiv', qk_i, v_new)
        s = s * gl_i[..., None, None] + jnp.einsum('bhck,bhcv->bhkv', kd_i, v_new)
        return s, o_i

    xs = tuple(jnp.moveaxis(t, 2, 0) for t in (u, w, qk, qg, kd, glast))
    s_last, o = lax.scan(step, s0, xs)
    return o.transpose(1, 0, 3, 2, 4).reshape(bsz, n, nh, dv), s_last


def gdn_mixer(sides, conv_w, a_log, dt_bias, norm_w, rope_tabs):
    nh = N_HEADS_GROUP

    def prep(q, k, v, z, b_raw, a_raw, tabs):
        qkv = jax.nn.silu(dwconv(jnp.concatenate([q, k, v], axis=-1), conv_w))
        bsz, n = qkv.shape[:2]
        q, k, v = [t.reshape(bsz, n, nh, HEAD_DIM).astype(jnp.float32) for t in jnp.split(qkv, 3, axis=-1)]
        q, k = l2_norm(q), l2_norm(k)
        if tabs is not None:
            q, k = rope2d(q, tabs), rope2d(k, tabs)
        return q, k, v, z, b_raw.astype(jnp.float32), a_raw.astype(jnp.float32)

    prepped = [prep(*s, t) for s, t in zip(sides, (None, rope_tabs))]
    os = [None, None]
    bsz = prepped[0][0].shape[0]
    for d in range(2):
        flip = flip_seq if d else keep_seq
        a = jnp.exp(a_log[d].astype(jnp.float32))
        s = jnp.zeros((bsz, nh, HEAD_DIM, HEAD_DIM), jnp.float32)
        for i, (q, k, v, _, b_raw, a_raw) in enumerate(prepped):
            beta = jax.nn.sigmoid(b_raw[..., d * nh:(d + 1) * nh])
            g = -a * jax.nn.softplus(a_raw[..., d * nh:(d + 1) * nh] + dt_bias[d].astype(jnp.float32))
            o, s = gdn_chunked(flip(q), flip(k), flip(v), flip(g), flip(beta), s)
            os[i] = flip(o) if d == 0 else os[i] + flip(o)
    outs = []
    for (q, _, _, z, _, _), o in zip(prepped, os):
        zh = z.reshape(o.shape).astype(jnp.float32)
        o = o * lax.rsqrt(jnp.mean(o * o, axis=-1, keepdims=True) + EPS) * norm_w.astype(jnp.float32)
        outs.append((o * jax.nn.silu(zh)).reshape(*o.shape[:2], GROUP_W).astype(z.dtype))
    return outs


def moe_ffn(tok, router_w, router_b, w_gate, w_up, w_down, ws_gate, ws_up, ws_down):
    n_tok, d = tok.shape
    scores = jax.nn.sigmoid((tok @ router_w).astype(jnp.float32))
    grouped = (scores + router_b.astype(jnp.float32)).reshape(n_tok, N_EXPERT_GROUPS, -1)
    group_score = lax.top_k(grouped, 2)[0].sum(-1)
    top_groups = lax.top_k(group_score, TOPK_GROUPS)[1]
    group_ok = (top_groups[:, :, None] == jnp.arange(N_EXPERT_GROUPS)).any(axis=1)
    choice = jnp.where(group_ok[:, :, None], grouped, -jnp.inf).reshape(n_tok, N_EXPERTS)
    expert_idx = lax.top_k(choice, TOP_K)[1]
    gate = jnp.take_along_axis(scores, expert_idx, axis=-1)
    gate = gate / jnp.sum(gate, axis=-1, keepdims=True) * ROUTED_SCALE

    n_assign = n_tok * TOP_K
    e_flat = expert_idx.reshape(-1)
    order = jnp.argsort(e_flat)
    e_sorted = e_flat[order]
    tok_sorted = (order // TOP_K).astype(jnp.int32)
    gate_sorted = gate.reshape(-1)[order]
    counts = jax.ops.segment_sum(jnp.ones_like(e_flat), e_flat, num_segments=N_EXPERTS)
    starts = jnp.cumsum(counts) - counts
    padded = (counts + MOE_BLOCK - 1) // MOE_BLOCK * MOE_BLOCK
    pad_end = jnp.cumsum(padded)
    dest = pad_end[e_sorted] - padded[e_sorted] + jnp.arange(n_assign) - starts[e_sorted]
    n_blocks = (n_assign + N_EXPERTS * (MOE_BLOCK - 1) + MOE_BLOCK - 1) // MOE_BLOCK
    n_rows = n_blocks * MOE_BLOCK
    row_tok = jnp.zeros((n_rows,), jnp.int32).at[dest].set(tok_sorted)
    row_gate = jnp.zeros((n_rows,), jnp.float32).at[dest].set(gate_sorted)
    block_expert = jnp.minimum(jnp.searchsorted(pad_end, jnp.arange(n_blocks) * MOE_BLOCK, side='right'),
                               N_EXPERTS - 1)

    def block(acc, blk):
        toks, gts, e = blk
        xb = tok[toks]
        hb = jax.nn.silu(xb @ w_gate[e]) * (xb @ w_up[e])
        yb = (hb @ w_down[e]).astype(jnp.float32) * gts[:, None]
        return acc.at[toks].add(yb), None

    routed, _ = lax.scan(block, jnp.zeros((n_tok, d), jnp.float32),
                         (row_tok.reshape(n_blocks, MOE_BLOCK), row_gate.reshape(n_blocks, MOE_BLOCK),
                          block_expert))
    shared = (jax.nn.silu(tok @ ws_gate) * (tok @ ws_up)) @ ws_down
    return (shared.astype(jnp.float32) + routed).astype(tok.dtype)


def setup_inputs(seed: int = 0) -> dict:
    key = jax.random.key(seed)
    keys = iter(jax.random.split(key, 64))

    def nrm(shape, scale):
        return jax.random.normal(next(keys), shape, jnp.float32) * scale

    def gain(shape):
        return 1.0 + nrm(shape, 0.05)

    def unif(shape, lo, hi):
        return jax.random.uniform(next(keys), shape, jnp.float32, lo, hi)

    def dt_bias_init(shape):
        dt = jnp.exp(unif(shape, math.log(1e-3), math.log(1e-1)))
        return dt + jnp.log(-jnp.expm1(-dt))

    L, H, D = DEPTH, N_HEADS_GROUP, D_MODEL
    u = unif((L, 2, GROUP_W), 0.9, 0.999)
    s = u ** (1.0 / LRU_C)
    lru_lambda = jnp.log(s) - jnp.log1p(-s)
    ssd_ch = GROUP_W + 2 * SSD_GROUPS * SSD_STATE
    return {
        'x': nrm((BATCH, SEQ, D), 1.0),
        'c': nrm((BATCH, D), 1.0),
        'ctx': nrm((BATCH, CTX_LEN, D), 1.0),
        'c_ctx': nrm((D,), 1.0),
        'w_mod': nrm((L, D, N_MOD * D), 0.3 * D ** -0.5),
        'b_mod': nrm((L, N_MOD * D), 0.02),
        'g_pre_mix': gain((L, D)),
        'g_post_mix': gain((L, D)),
        'g_pre_ffn': gain((L, D)),
        'g_post_ffn': gain((L, D)),
        'w_in': nrm((L, D, D_IN), D ** -0.5),
        'w_out': nrm((L, D_MIX, D), D_MIX ** -0.5),
        'lru_conv_w': nrm((L, CONV_W, GROUP_W), CONV_W ** -0.5),
        'lru_conv_b': nrm((L, GROUP_W), 0.02),
        'lru_wa': nrm((L, 2, H, HEAD_DIM, HEAD_DIM), HEAD_DIM ** -0.5),
        'lru_ba': nrm((L, 2, GROUP_W), 0.02),
        'lru_wx': nrm((L, 2, H, HEAD_DIM, HEAD_DIM), HEAD_DIM ** -0.5),
        'lru_bx': nrm((L, 2, GROUP_W), 0.02),
        'lru_lambda': lru_lambda,
        'na_bias': nrm((L, H, 2 * NA_WIN_ROWS - 1, 2 * NA_WIN_COLS - 1), 0.1),
        'ssd_conv_w': nrm((L, CONV_W, ssd_ch), CONV_W ** -0.5),
        'ssd_conv_b': nrm((L, ssd_ch), 0.02),
        'ssd_a_log': jnp.log(unif((L, 2, H), 1.0, 16.0)),
        'ssd_dt_bias': dt_bias_init((L, 2, H)),
        'ssd_d': gain((L, H)),
        'ssd_norm': gain((L, GROUP_W)),
        'gdn_conv_w': nrm((L, CONV_W, 3 * GROUP_W), CONV_W ** -0.5),
        'gdn_a_log': jnp.log(unif((L, 2, H), 1.0, 16.0)),
        'gdn_dt_bias': dt_bias_init((L, 2, H)),
        'gdn_norm': gain((L, HEAD_DIM)),
        'router_w': nrm((L, D, N_EXPERTS), D ** -0.5),
        'router_b': nrm((L, N_EXPERTS), 0.01),
        'we_gate': nrm((L, N_EXPERTS, D, D_EXPERT), D ** -0.5),
        'we_up': nrm((L, N_EXPERTS, D, D_EXPERT), D ** -0.5),
        'we_down': nrm((L, N_EXPERTS, D_EXPERT, D), D_EXPERT ** -0.5),
        'ws_gate': nrm((L, D, D_SHARED), D ** -0.5),
        'ws_up': nrm((L, D, D_SHARED), D ** -0.5),
        'ws_down': nrm((L, D_SHARED, D), D_SHARED ** -0.5),
    }


def reference(x, c, ctx, c_ctx, w_mod, b_mod, g_pre_mix, g_post_mix, g_pre_ffn, g_post_ffn,
              w_in, w_out, lru_conv_w, lru_conv_b, lru_wa, lru_ba, lru_wx, lru_bx, lru_lambda,
              na_bias, ssd_conv_w, ssd_conv_b, ssd_a_log, ssd_dt_bias, ssd_d, ssd_norm,
              gdn_conv_w, gdn_a_log, gdn_dt_bias, gdn_norm, router_w, router_b,
              we_gate, we_up, we_down, ws_gate, ws_up, ws_down):
    bsz, seq, d = x.shape
    n_ctx = ctx.shape[1]
    rope_tabs = axial_rope(seq)
    xl, xc = x, ctx
    for l in range(DEPTH):
        last = l == DEPTH - 1
        mod_l = jnp.split(jax.nn.silu(c) @ w_mod[l] + b_mod[l], N_MOD, axis=-1)
        sh1l, sc1l, ga1l, sh2l, sc2l, ga2l = [m[:, None, :] for m in mod_l]
        sh1c, sc1c, ga1c, sh2c, sc2c, ga2c = jnp.split(jax.nn.silu(c_ctx) @ w_mod[l] + b_mod[l], N_MOD, axis=-1)

        hc = rms_norm(xc, g_pre_mix[l]) * (1.0 + sc1c) + sh1c
        hl = rms_norm(xl, g_pre_mix[l]) * (1.0 + sc1l) + sh1l
        pc, pl = split_in(hc @ w_in[l]), split_in(hl @ w_in[l])
        a_c, a_l = rglru_mixer((pc[0:2], pl[0:2]), lru_conv_w[l], lru_conv_b[l], lru_wa[l], lru_ba[l],
                               lru_wx[l], lru_bx[l], lru_lambda[l])
        b_c, b_l = na_mixer((pc[2:5], pl[2:5]), na_bias[l])
        s_c, s_l = ssd_mixer((pc[5:10], pl[5:10]), ssd_conv_w[l], ssd_conv_b[l], ssd_a_log[l],
                             ssd_dt_bias[l], ssd_d[l], ssd_norm[l])
        d_c, d_l = gdn_mixer((pc[10:16], pl[10:16]), gdn_conv_w[l], gdn_a_log[l], gdn_dt_bias[l],
                             gdn_norm[l], rope_tabs)
        ml = jnp.concatenate([a_l, b_l, s_l, d_l], axis=-1) @ w_out[l]
        xl = xl + ga1l * rms_norm(ml, g_post_mix[l])

        hl2 = rms_norm(xl, g_pre_ffn[l]) * (1.0 + sc2l) + sh2l
        moe_args = (router_w[l], router_b[l], we_gate[l], we_up[l], we_down[l], ws_gate[l], ws_up[l], ws_down[l])
        if last:
            fl = moe_ffn(hl2.reshape(-1, d), *moe_args).reshape(xl.shape)
        else:
            mc = jnp.concatenate([a_c, b_c, s_c, d_c], axis=-1) @ w_out[l]
            xc = xc + ga1c * rms_norm(mc, g_post_mix[l])
            hc2 = rms_norm(xc, g_pre_ffn[l]) * (1.0 + sc2c) + sh2c
            f = moe_ffn(jnp.concatenate([hc2.reshape(-1, d), hl2.reshape(-1, d)], axis=0), *moe_args)
            fl = f[bsz * n_ctx:].reshape(xl.shape)
            xc = xc + ga2c * rms_norm(f[:bsz * n_ctx].reshape(xc.shape), g_post_ffn[l])
        xl = xl + ga2l * rms_norm(fl, g_post_ffn[l])
    return xl
```

```python
import functools
import math

import jax
import jax.numpy as jnp
import numpy as np
from jax import lax
from jax.experimental import pallas as pl
from jax.experimental.pallas import tpu as pltpu

F32 = jnp.float32
MXU_DTYPE = jnp.bfloat16
HI = lax.Precision.HIGHEST

D_MODEL = 1024
GRID_W = 64
GROUP_W = 256
HEAD_DIM = 64
N_HEADS = 4
EPS = 1e-6
NEG_INF = -1e30
N_MOD = 6
LRU_C = 8.0
NA_WIN_ROWS = 8
NA_WIN_COLS = 16
SSD_STATE = 128
SSD_GROUPS = 2
ROPE_BASE = 10000.0
ROPE_AXIS_DIM = HEAD_DIM // 2
N_EXPERTS = 64
N_EXPERT_GROUPS = 8
TOPK_GROUPS = 4
TOP_K = 8
D_EXPERT = 256
ROUTED_SCALE = 2.5

LANES = 128
SUBLANES = 8
VMEM_LIMIT = 56 * 1024 * 1024

TOKEN_TILE = 512
LRU_CHUNK = 256
SSD_CHUNK = 128
GDN_CHUNK = 64
GDN_BASE = 16
MOE_TILE = 1024
MOE_EB = 2

P_WIDTHS = (256, 256, 256, 256, 256, 768, 256, 768, 256, 128)
(P_AX, P_AG, P_BQ, P_BK, P_BV, P_CX, P_CZ, P_DX, P_DZ, P_SM) = range(10)
SM_DT, SM_BETA, SM_DECAY = 0, 8, 16


def _cp(*sem):
    return pltpu.CompilerParams(dimension_semantics=sem, vmem_limit_bytes=VMEM_LIMIT)


def _mx(x):
    return x.astype(MXU_DTYPE)


def _dot(a, b):
    return jnp.dot(_mx(a), _mx(b), preferred_element_type=F32)


def _dot_nt(a, b):
    return lax.dot_general(_mx(a), _mx(b), (((1,), (1,)), ((), ())), preferred_element_type=F32)


def _dot_tn(a, b):
    return lax.dot_general(_mx(a), _mx(b), (((0,), (0,)), ((), ())), preferred_element_type=F32)


def _dot_hi(a, b):
    return jnp.dot(a, b, preferred_element_type=F32, precision=HI)


def _sigmoid(x):
    return 1.0 / (1.0 + jnp.exp(-x))


def _silu(x):
    return x * _sigmoid(x)


def _softplus(x):
    return jnp.maximum(x, 0.0) + jnp.log1p(jnp.exp(-jnp.abs(x)))


def _gelu_tanh(x):
    return 0.5 * x * (1.0 + jnp.tanh(math.sqrt(2.0 / math.pi) * (x + 0.044715 * (x * x * x))))


def _rms(x, g):
    return x * lax.rsqrt(jnp.mean(x * x, axis=-1, keepdims=True) + EPS) * g


def _full(shape):
    n = len(shape)
    return pl.BlockSpec(shape, lambda *_: (0,) * n)


MOD_COLS = 1536


def _mod_kernel(c_ref, w_ref, b_ref, o_ref):
    o_ref[0] = _dot_hi(_silu(c_ref[...]), w_ref[0]) + b_ref[0]


def modulation(cond, w_mod, b_mod):
    depth, d, n = w_mod.shape
    return pl.pallas_call(
        _mod_kernel,
        grid=(depth, n // MOD_COLS),
        in_specs=[pl.BlockSpec((SUBLANES, d), lambda l, j: (0, 0)),
                  pl.BlockSpec((1, d, MOD_COLS), lambda l, j: (l, 0, j)),
                  pl.BlockSpec((1, 1, MOD_COLS), lambda l, j: (l, 0, j))],
        out_specs=pl.BlockSpec((1, SUBLANES, MOD_COLS), lambda l, j: (l, 0, j)),
        out_shape=jax.ShapeDtypeStruct((depth, SUBLANES, n), F32),
        compiler_params=_cp("parallel", "parallel"),
        name="modulation",
    )(cond, w_mod, b_mod.reshape(depth, 1, n))


def _inproj_kernel(x_ref, g_ref, sc_ref, sh_ref, w_ref, *o_refs):
    h = _rms(x_ref[...], g_ref[...]) * (1.0 + sc_ref[0]) + sh_ref[0]
    p = _dot(h, w_ref[...])
    off = 0
    for o_ref, w in zip(o_refs, P_WIDTHS):
        o_ref[...] = p[:, off:off + w]
        off += w


def in_projection(x, g, sc, sh, w, tiles_per_group):
    t, d = x.shape
    tm = min(TOKEN_TILE, t)
    vec = lambda i: (i // tiles_per_group, 0, 0)
    return pl.pallas_call(
        _inproj_kernel,
        grid=(t // tm,),
        in_specs=[pl.BlockSpec((tm, d), lambda i: (i, 0)),
                  _full((1, d)),
                  pl.BlockSpec((1, 1, d), vec),
                  pl.BlockSpec((1, 1, d), vec),
                  _full(w.shape)],
        out_specs=[pl.BlockSpec((tm, wd), lambda i: (i, 0)) for wd in P_WIDTHS],
        out_shape=[jax.ShapeDtypeStruct((t, wd), F32) for wd in P_WIDTHS],
        compiler_params=_cp("parallel"),
        name="in_projection",
    )(x, g, sc, sh, w)


def _halo(p, bsz, q):
    c = p.shape[-1]
    pr = p.reshape(bsz, -1, q, c)
    nc = pr.shape[1]
    prev = jnp.concatenate([jnp.zeros((bsz, 1, 2, c), p.dtype), pr[:, :-1, q - 2:, :]], axis=1)
    nxt = jnp.concatenate([pr[:, 1:, :1, :], jnp.zeros((bsz, 1, 1, c), p.dtype)], axis=1)
    pad = jnp.zeros((bsz, nc, SUBLANES - 3, c), p.dtype)
    return jnp.concatenate([prev, nxt, pad], axis=2).reshape(bsz * nc, SUBLANES, c)


def _dwconv(x, halo, w, b=None):
    q = x.shape[0]
    row = lax.broadcasted_iota(jnp.int32, x.shape, 0)
    xm2 = jnp.where(row == 0, halo[0:1], jnp.where(row == 1, halo[1:2], pltpu.roll(x, 2, 0)))
    xm1 = jnp.where(row == 0, halo[1:2], pltpu.roll(x, 1, 0))
    xp1 = jnp.where(row == q - 1, halo[2:3], pltpu.roll(x, q - 1, 0))
    y = w[0:1] * xm2 + w[1:2] * xm1 + w[2:3] * x + w[3:4] * xp1
    return y if b is None else y + b


def _pad_rows(a, rows=SUBLANES):
    return jnp.concatenate([a, jnp.zeros((rows - a.shape[0],) + a.shape[1:], a.dtype)], axis=0)


def _chunk_specs(nc, q, c):
    fwd = pl.BlockSpec((q, c), lambda b, i: (b * nc + i, 0))
    bwd = pl.BlockSpec((q, c), lambda b, i: (b * nc + nc - 1 - i, 0))
    return fwd, bwd


def _halo_specs(nc, c):
    fwd = pl.BlockSpec((1, SUBLANES, c), lambda b, i: (b * nc + i, 0, 0))
    bwd = pl.BlockSpec((1, SUBLANES, c), lambda b, i: (b * nc + nc - 1 - i, 0, 0))
    return fwd, bwd


def _lru_kernel(xf_ref, xb_ref, hf_ref, hb_ref, h0_ref, cw_ref, cb_ref, wg_ref, bg_ref, lam_ref,
                yf_ref, yb_ref, hfin_ref, af_s, bf_s, ab_s, bb_s, carry_s):
    i = pl.program_id(1)
    q = xf_ref.shape[0]

    @pl.when(i == 0)
    def _():
        carry_s[...] = h0_ref[0]

    def coeffs(x_ref, halo_ref, d, a_s, b_s):
        u = _dwconv(x_ref[...], halo_ref[0], cw_ref[...], cb_ref[...])
        g = _dot(u, wg_ref[:, 2 * GROUP_W * d:2 * GROUP_W * (d + 1)]) + bg_ref[:, 2 * GROUP_W * d:2 * GROUP_W * (d + 1)]
        r = _sigmoid(g[:, :GROUP_W])
        gate_in = _sigmoid(g[:, GROUP_W:])
        log_a = -LRU_C * r * _softplus(-lam_ref[d:d + 1, :])
        a_s[...] = jnp.exp(log_a)
        b_s[...] = jnp.sqrt(1.0 - jnp.exp(2.0 * log_a)) * (gate_in * u)

    coeffs(xf_ref, hf_ref, 0, af_s, bf_s)
    coeffs(xb_ref, hb_ref, 1, ab_s, bb_s)

    ng = q // SUBLANES
    row = lax.broadcasted_iota(jnp.int32, (SUBLANES, GROUP_W), 0)

    def body(g, hs):
        h_f, h_b = hs
        i0 = pl.multiple_of(g * SUBLANES, SUBLANES)
        a = af_s[pl.ds(i0, SUBLANES), :]
        b = bf_s[pl.ds(i0, SUBLANES), :]
        for s in (1, 2, 4):
            m = row >= s
            b = jnp.where(m, a * pltpu.roll(b, s, 0) + b, b)
            a = jnp.where(m, a * pltpu.roll(a, s, 0), a)
        h = b + a * h_f
        yf_ref[pl.ds(i0, SUBLANES), :] = h
        h_f = h[SUBLANES - 1:SUBLANES, :]
        j0 = pl.multiple_of((ng - 1 - g) * SUBLANES, SUBLANES)
        a = ab_s[pl.ds(j0, SUBLANES), :]
        b = bb_s[pl.ds(j0, SUBLANES), :]
        for s in (1, 2, 4):
            m = row < SUBLANES - s
            b = jnp.where(m, a * pltpu.roll(b, SUBLANES - s, 0) + b, b)
            a = jnp.where(m, a * pltpu.roll(a, SUBLANES - s, 0), a)
        h = b + a * h_b
        yb_ref[pl.ds(j0, SUBLANES), :] = h
        return h_f, h[0:1, :]

    h_f, h_b = lax.fori_loop(0, ng, body, (carry_s[0:1, :], carry_s[1:2, :]))
    carry_s[0:1, :] = h_f
    carry_s[1:2, :] = h_b

    @pl.when(i == pl.num_programs(1) - 1)
    def _():
        hfin_ref[0] = carry_s[...]


def _block_diag(w):
    h, a, b = w.shape
    return jnp.einsum('hij,hg->higj', w, jnp.eye(h, dtype=w.dtype)).reshape(h * a, h * b)


def lru_params(conv_w, conv_b, wa, ba, wx, bx, lam):
    wg = jnp.concatenate([_block_diag(wa[0]), _block_diag(wx[0]), _block_diag(wa[1]), _block_diag(wx[1])], axis=1)
    bg = jnp.concatenate([ba[0], bx[0], ba[1], bx[1]])[None, :]
    return _pad_rows(conv_w), conv_b[None, :], wg.astype(MXU_DTYPE), bg, _pad_rows(lam)


def lru_mixer(x, h0, bsz, cw, cb, wg, bg, lam):
    t, c = x.shape
    s = t // bsz
    q = min(LRU_CHUNK, s)
    nc = s // q
    halo = _halo(x, bsz, q)
    xf, xb = _chunk_specs(nc, q, c)
    hf, hb = _halo_specs(nc, c)
    st = pl.BlockSpec((1, SUBLANES, c), lambda b, i: (b, 0, 0))
    return pl.pallas_call(
        _lru_kernel,
        grid=(bsz, nc),
        in_specs=[xf, xb, hf, hb, st, _full(cw.shape), _full(cb.shape), _full(wg.shape), _full(bg.shape),
                  _full(lam.shape)],
        out_specs=[xf, xb, st],
        out_shape=[jax.ShapeDtypeStruct((t, c), F32), jax.ShapeDtypeStruct((t, c), F32),
                   jax.ShapeDtypeStruct((bsz, SUBLANES, c), F32)],
        scratch_shapes=[pltpu.VMEM((q, c), F32)] * 4 + [pltpu.VMEM((SUBLANES, c), F32)],
        compiler_params=_cp("parallel", "arbitrary"),
        name="lru_mixer",
    )(x, x, halo, halo, h0, cw, cb, wg, bg, lam)


NA_KEYS = NA_WIN_ROWS * GRID_W


def na_bias_slabs(table):
    qc = np.arange(GRID_W)[:, None]
    kc = np.arange(GRID_W)[None, :]
    win0 = np.clip(qc - NA_WIN_COLS // 2, 0, GRID_W - NA_WIN_COLS)
    ok = (kc >= win0) & (kc < win0 + NA_WIN_COLS)
    dc = np.clip(kc - qc + NA_WIN_COLS - 1, 0, 2 * NA_WIN_COLS - 2)
    dr = np.arange(NA_WIN_ROWS)[:, None] + np.arange(NA_WIN_ROWS)[None, :]
    b = table.astype(F32)[:, dr][:, :, :, dc]
    b = jnp.where(ok[None, None, None], b, NEG_INF)
    h = table.shape[0]
    return b.transpose(0, 1, 3, 2, 4).reshape(h, NA_WIN_ROWS, GRID_W, NA_KEYS)


def _na_kernel(q_ref, kw_ref, vw_ref, kc_ref, vc_ref, slab_ref, o_ref, *, rows):
    r = pl.program_id(1)
    r0 = jnp.clip(r - NA_WIN_ROWS // 2, 0, rows - NA_WIN_ROWS)
    o = r0 - r + NA_WIN_ROWS - 1
    q = q_ref[...] * (HEAD_DIM ** -0.5)
    kw, vw, kc, vc = kw_ref[...], vw_ref[...], kc_ref[0], vc_ref[0]
    outs = []
    for h in range(N_HEADS):
        sl = slice(h * HEAD_DIM, (h + 1) * HEAD_DIM)
        s_loc = _dot_nt(q[:, sl], kw[:, sl]) + slab_ref[h, o]
        s_ctx = _dot_nt(q[:, sl], kc[:, sl])
        m = jnp.maximum(jnp.max(s_loc, axis=-1, keepdims=True), jnp.max(s_ctx, axis=-1, keepdims=True))
        p_loc = jnp.exp(s_loc - m)
        p_ctx = jnp.exp(s_ctx - m)
        den = jnp.sum(p_loc, axis=-1, keepdims=True) + jnp.sum(p_ctx, axis=-1, keepdims=True)
        outs.append((_dot(p_loc, vw[:, sl]) + _dot(p_ctx, vc[:, sl])) / den)
    o_ref[...] = jnp.concatenate(outs, axis=1)


def na_mixer(q, k, v, kc, vc, slabs, bsz):
    t, c = q.shape
    s = t // bsz
    rows = s // GRID_W
    n_ctx = kc.shape[1]

    def win(b, r):
        r0 = jnp.clip(r - NA_WIN_ROWS // 2, 0, rows - NA_WIN_ROWS)
        return ((b * rows + r0) * GRID_W, 0)

    wspec = pl.BlockSpec((pl.Element(NA_KEYS), pl.Element(c)), win)
    cspec = pl.BlockSpec((1, n_ctx, c), lambda b, r: (b, 0, 0))
    qspec = pl.BlockSpec((GRID_W, c), lambda b, r: (b * rows + r, 0))
    return pl.pallas_call(
        functools.partial(_na_kernel, rows=rows),
        grid=(bsz, rows),
        in_specs=[qspec, wspec, wspec, cspec, cspec, _full(slabs.shape)],
        out_specs=qspec,
        out_shape=jax.ShapeDtypeStruct((t, c), F32),
        compiler_params=_cp("parallel", "arbitrary"),
        name="na_mixer",
    )(q, k, v, kc, vc, slabs)


def _ctx_attn_kernel(q_ref, k_ref, v_ref, o_ref):
    q = q_ref[0] * (HEAD_DIM ** -0.5)
    k, v = k_ref[0], v_ref[0]
    outs = []
    for h in range(N_HEADS):
        sl = slice(h * HEAD_DIM, (h + 1) * HEAD_DIM)
        s = _dot_nt(q[:, sl], k[:, sl])
        p = jnp.exp(s - jnp.max(s, axis=-1, keepdims=True))
        outs.append(_dot(p, v[:, sl]) / jnp.sum(p, axis=-1, keepdims=True))
    o_ref[0] = jnp.concatenate(outs, axis=1)


def ctx_attention(q, k, v):
    spec = pl.BlockSpec((1,) + q.shape[1:], lambda b: (b, 0, 0))
    return pl.pallas_call(
        _ctx_attn_kernel,
        grid=(q.shape[0],),
        in_specs=[spec, spec, spec],
        out_specs=spec,
        out_shape=jax.ShapeDtypeStruct(q.shape, F32),
        compiler_params=_cp("parallel"),
        name="ctx_attention",
    )(q, k, v)


def _small_vec(vals, off):
    v = jnp.zeros((LANES,), F32).at[off:off + 2 * N_HEADS].set(vals.reshape(-1).astype(F32))
    return v[None, :]


def _lane_mask(off):
    lane = lax.broadcasted_iota(jnp.int32, (1, LANES), 1)
    return (lane >= off) & (lane < off + 2 * N_HEADS)


def _tri_masks(q):
    rowi = lax.broadcasted_iota(jnp.int32, (q, q), 0)
    coli = lax.broadcasted_iota(jnp.int32, (q, q), 1)
    return rowi, coli


def _ssd_kernel(xf_ref, xb_ref, hf_ref, hb_ref, sf_ref, sb_ref, h0_ref, cw_ref, cb_ref, dtb_ref, alog_ref,
                yf_ref, yb_ref, xc_ref, hfin_ref, state_s):
    i = pl.program_id(1)
    q = xf_ref.shape[0]

    @pl.when(i == 0)
    def _():
        state_s[...] = h0_ref[0]

    rowi, coli = _tri_masks(q)
    a_neg = jnp.where(_lane_mask(SM_DT), -jnp.exp(alog_ref[...]), 0.0)

    def direction(x_ref, halo_ref, sm_ref, d, y_ref):
        xbc = _silu(_dwconv(x_ref[...], halo_ref[0], cw_ref[...], cb_ref[...]))
        if d == 0:
            xc_ref[...] = xbc[:, :GROUP_W]
        dt = _softplus(sm_ref[...] + dtb_ref[...])
        keep = (rowi >= coli) if d == 0 else (rowi <= coli)
        acum = _dot_hi(keep.astype(F32), dt * a_neg)
        acum_t = acum.T
        last = acum[q - 1:q, :] if d == 0 else acum[0:1, :]
        dec_end = jnp.exp(last - acum)
        e_acum = jnp.exp(acum)
        e_last = jnp.exp(last)
        ys = []
        for g in range(SSD_GROUPS):
            bg = xbc[:, GROUP_W + SSD_STATE * g:GROUP_W + SSD_STATE * (g + 1)]
            cg = xbc[:, GROUP_W + SSD_STATE * (SSD_GROUPS + g):GROUP_W + SSD_STATE * (SSD_GROUPS + g + 1)]
            cbt = _dot_nt(cg, bg)
            for hh in range(N_HEADS // SSD_GROUPS):
                h = g * (N_HEADS // SSD_GROUPS) + hh
                ln = SM_DT + N_HEADS * d + h
                lmat = jnp.exp(jnp.where(keep, acum[:, ln:ln + 1] - acum_t[ln:ln + 1, :], NEG_INF))
                xdt = xbc[:, h * HEAD_DIM:(h + 1) * HEAD_DIM] * dt[:, ln:ln + 1]
                st = state_s[d, h]
                ys.append(_dot(cbt * lmat, xdt) + _dot(cg * e_acum[:, ln:ln + 1], st))
                state_s[d, h] = st * e_last[:, ln:ln + 1] + _dot_tn(bg * dec_end[:, ln:ln + 1], xdt)
        y_ref[...] = jnp.concatenate(ys, axis=1)

    direction(xf_ref, hf_ref, sf_ref, 0, yf_ref)
    direction(xb_ref, hb_ref, sb_ref, 1, yb_ref)

    @pl.when(i == pl.num_programs(1) - 1)
    def _():
        hfin_ref[0] = state_s[...]


def ssd_params(conv_w, conv_b, a_log, dt_bias):
    return _pad_rows(conv_w), conv_b[None, :], _small_vec(dt_bias, SM_DT), _small_vec(a_log, SM_DT)


def ssd_mixer(xbc, sm, h0, bsz, cw, cb, dtb, alog):
    t, c = xbc.shape
    s = t // bsz
    q = min(SSD_CHUNK, s)
    nc = s // q
    halo = _halo(xbc, bsz, q)
    xf, xb = _chunk_specs(nc, q, c)
    hf, hb = _halo_specs(nc, c)
    sf, sb = _chunk_specs(nc, q, LANES)
    yf, yb = _chunk_specs(nc, q, GROUP_W)
    st = pl.BlockSpec((1,) + h0.shape[1:], lambda b, i: (b, 0, 0, 0, 0))
    y_shape = jax.ShapeDtypeStruct((t, GROUP_W), F32)
    return pl.pallas_call(
        _ssd_kernel,
        grid=(bsz, nc),
        in_specs=[xf, xb, hf, hb, sf, sb, st, _full(cw.shape), _full(cb.shape), _full(dtb.shape), _full(alog.shape)],
        out_specs=[yf, yb, yf, st],
        out_shape=[y_shape, y_shape, y_shape, jax.ShapeDtypeStruct(h0.shape, F32)],
        scratch_shapes=[pltpu.VMEM(h0.shape[1:], F32)],
        compiler_params=_cp("parallel", "arbitrary"),
        name="ssd_mixer",
    )(xbc, xbc, halo, halo, sm, sm, h0, cw, cb, dtb, alog)


def rope_tables(seq):
    t = jnp.arange(seq)
    row = (t // GRID_W).astype(F32)
    col = (t % GRID_W).astype(F32)
    inv = ROPE_BASE ** (-jnp.arange(0, ROPE_AXIS_DIM, 2, dtype=F32) / ROPE_AXIS_DIM)
    ar, ac = row[:, None] * inv, col[:, None] * inv
    cos = jnp.concatenate([jnp.cos(ar), jnp.cos(ar), jnp.cos(ac), jnp.cos(ac)], axis=1)
    sin = jnp.concatenate([-jnp.sin(ar), jnp.sin(ar), -jnp.sin(ac), jnp.sin(ac)], axis=1)
    return jnp.tile(cos, (1, N_HEADS)), jnp.tile(sin, (1, N_HEADS))


def _swap16(x):
    lane = lax.broadcasted_iota(jnp.int32, x.shape, 1)
    half = ROPE_AXIS_DIM // 2
    return jnp.where((lane & (ROPE_AXIS_DIM - 1)) < half,
                     pltpu.roll(x, x.shape[1] - half, 1), pltpu.roll(x, half, 1))


def _l2norm_heads(x):
    outs = []
    for h in range(N_HEADS):
        xh = x[:, h * HEAD_DIM:(h + 1) * HEAD_DIM]
        outs.append(xh * lax.rsqrt(jnp.sum(xh * xh, axis=-1, keepdims=True) + EPS))
    return jnp.concatenate(outs, axis=1)


def _unit_tri_inverse(a, rowi, coli):
    q = a.shape[0]
    same = lambda n: (rowi >> (n.bit_length() - 1)) == (coli >> (n.bit_length() - 1))
    ab = jnp.where(same(GDN_BASE), a, 0.0)
    t = (rowi == coli).astype(F32) - ab
    p = _dot(ab, ab)
    t = t + _dot(t, p)
    n = 4
    while n < GDN_BASE:
        p = _dot(p, p)
        t = t + _dot(t, p)
        n *= 2
    n = GDN_BASE
    while n < q:
        off = jnp.where(same(2 * n) & jnp.logical_not(same(n)), a, 0.0)
        t = t - _dot(_dot(t, off), t)
        n *= 2
    return t


def _gdn_kernel(*refs, rope):
    if rope:
        (xf_ref, xb_ref, hf_ref, hb_ref, sf_ref, sb_ref, cf_ref, cb_ref, nf_ref, nb_ref,
         s0_ref, cw_ref, alog_ref, dtb_ref, of_ref, ob_ref, sfin_ref, state_s) = refs
    else:
        (xf_ref, xb_ref, hf_ref, hb_ref, sf_ref, sb_ref,
         s0_ref, cw_ref, alog_ref, dtb_ref, of_ref, ob_ref, sfin_ref, state_s) = refs
        cf_ref = cb_ref = nf_ref = nb_ref = None
    i = pl.program_id(1)
    q = xf_ref.shape[0]

    @pl.when(i == 0)
    def _():
        state_s[...] = s0_ref[0]

    rowi, coli = _tri_masks(q)
    a_neg = jnp.where(_lane_mask(SM_DECAY), -jnp.exp(alog_ref[...]), 0.0)

    def direction(x_ref, halo_ref, sm_ref, cos_ref, sin_ref, d, o_ref):
        qkv = _silu(_dwconv(x_ref[...], halo_ref[0], cw_ref[...]))
        qn = _l2norm_heads(qkv[:, :GROUP_W])
        kn = _l2norm_heads(qkv[:, GROUP_W:2 * GROUP_W])
        v = qkv[:, 2 * GROUP_W:]
        if rope:
            cos, sin = cos_ref[...], sin_ref[...]
            qn = qn * cos + _swap16(qn) * sin
            kn = kn * cos + _swap16(kn) * sin
        qn = qn * (HEAD_DIM ** -0.5)
        sm = sm_ref[...]
        beta = _sigmoid(sm)
        keep = (rowi >= coli) if d == 0 else (rowi <= coli)
        strict = (rowi > coli) if d == 0 else (rowi < coli)
        gc = _dot_hi(keep.astype(F32), _softplus(sm + dtb_ref[...]) * a_neg)
        gc_t = gc.T
        last = gc[q - 1:q, :] if d == 0 else gc[0:1, :]
        e_gc = jnp.exp(gc)
        e_end = jnp.exp(last - gc)
        e_last = jnp.exp(last)
        outs = []
        for h in range(N_HEADS):
            sl = slice(h * HEAD_DIM, (h + 1) * HEAD_DIM)
            lg = SM_DECAY + N_HEADS * d + h
            lb = SM_BETA + N_HEADS * d + h
            decay = jnp.exp(jnp.where(keep, gc[:, lg:lg + 1] - gc_t[lg:lg + 1, :], NEG_INF))
            qh, kh, bcol = qn[:, sl], kn[:, sl], beta[:, lb:lb + 1]
            kb = kh * bcol
            a = jnp.where(strict, _dot_nt(kb, kh) * decay, 0.0)
            tinv = _unit_tri_inverse(a, rowi, coli)
            sol = _dot(tinv, jnp.concatenate([v[:, sl] * bcol, kb * e_gc[:, lg:lg + 1]], axis=1))
            u, w = sol[:, :HEAD_DIM], sol[:, HEAD_DIM:]
            st = state_s[d, h]
            v_new = u - _dot(w, st)
            outs.append(_dot(qh * e_gc[:, lg:lg + 1], st) + _dot(_dot_nt(qh, kh) * decay, v_new))
            state_s[d, h] = st * e_last[:, lg:lg + 1] + _dot_tn(kh * e_end[:, lg:lg + 1], v_new)
        o_ref[...] = jnp.concatenate(outs, axis=1)

    direction(xf_ref, hf_ref, sf_ref, cf_ref, nf_ref, 0, of_ref)
    direction(xb_ref, hb_ref, sb_ref, cb_ref, nb_ref, 1, ob_ref)

    @pl.when(i == pl.num_programs(1) - 1)
    def _():
        sfin_ref[0] = state_s[...]


def gdn_params(conv_w, a_log, dt_bias):
    return _pad_rows(conv_w), _small_vec(a_log, SM_DECAY), _small_vec(dt_bias, SM_DECAY)


def gdn_mixer(qkv, sm, s0, bsz, cw, alog, dtb, rope=None):
    t, c = qkv.shape
    s = t // bsz
    q = min(GDN_CHUNK, s)
    nc = s // q
    halo = _halo(qkv, bsz, q)
    xf, xb = _chunk_specs(nc, q, c)
    hf, hb = _halo_specs(nc, c)
    sf, sb = _chunk_specs(nc, q, LANES)
    of, ob = _chunk_specs(nc, q, GROUP_W)
    st = pl.BlockSpec((1,) + s0.shape[1:], lambda b, i: (b, 0, 0, 0, 0))
    ins = [qkv, qkv, halo, halo, sm, sm]
    specs = [xf, xb, hf, hb, sf, sb]
    if rope is not None:
        tf = pl.BlockSpec((q, GROUP_W), lambda b, i: (i, 0))
        tb = pl.BlockSpec((q, GROUP_W), lambda b, i: (nc - 1 - i, 0))
        ins += [rope[0], rope[0], rope[1], rope[1]]
        specs += [tf, tb, tf, tb]
    ins += [s0, cw, alog, dtb]
    specs += [st, _full(cw.shape), _full(alog.shape), _full(dtb.shape)]
    o_shape = jax.ShapeDtypeStruct((t, GROUP_W), F32)
    return pl.pallas_call(
        functools.partial(_gdn_kernel, rope=rope is not None),
        grid=(bsz, nc),
        in_specs=specs,
        out_specs=[of, ob, st],
        out_shape=[o_shape, o_shape, jax.ShapeDtypeStruct(s0.shape, F32)],
        scratch_shapes=[pltpu.VMEM(s0.shape[1:], F32)],
        compiler_params=_cp("parallel", "arbitrary"),
        name="gdn_mixer",
    )(*ins)


def _split_hi_lo(a):
    hi = _mx(a)
    return hi, _mx(a - hi.astype(F32))


def _outproj_kernel(x_ref, ahf_ref, ahb_ref, ag_ref, bo_ref, cyf_ref, cyb_ref, cxc_ref, cz_ref,
                    dof_ref, dob_ref, dz_ref, wout_ref, gpost_ref, ga1_ref, gpre_ref, sc2_ref, sh2_ref,
                    dskip_ref, cnorm_ref, dnorm_ref, rhi_ref, rlo_ref, xo_ref, h2_ref, lg_ref):
    m_a = (ahf_ref[...] + ahb_ref[...]) * _gelu_tanh(ag_ref[...])
    y_c = (cyf_ref[...] + cyb_ref[...] + cxc_ref[...] * dskip_ref[...]) * _silu(cz_ref[...])
    m_c = _rms(y_c, cnorm_ref[...])
    o_d = dof_ref[...] + dob_ref[...]
    heads = []
    for h in range(N_HEADS):
        oh = o_d[:, h * HEAD_DIM:(h + 1) * HEAD_DIM]
        heads.append(oh * lax.rsqrt(jnp.mean(oh * oh, axis=-1, keepdims=True) + EPS))
    m_d = jnp.concatenate(heads, axis=1) * dnorm_ref[...] * _silu(dz_ref[...])
    mix = jnp.concatenate([_mx(m_a), _mx(bo_ref[...]), _mx(m_c), _mx(m_d)], axis=1)
    ml = jnp.dot(mix, wout_ref[...], preferred_element_type=F32)
    x_new = x_ref[...] + ga1_ref[0] * _rms(ml, gpost_ref[...])
    xo_ref[...] = x_new
    h2 = _rms(x_new, gpre_ref[...]) * (1.0 + sc2_ref[0]) + sh2_ref[0]
    hi, lo = _split_hi_lo(h2)
    h2_ref[...] = hi
    rhi = rhi_ref[...]
    lg_ref[...] = (jnp.dot(hi, rhi, preferred_element_type=F32) + jnp.dot(lo, rhi, preferred_element_type=F32)
                   + jnp.dot(hi, rlo_ref[...], preferred_element_type=F32))


def out_projection(x, mixers, w_out, gpost, ga1, gpre, sc2, sh2, dskip, cnorm, dnorm, router_w, tiles_per_group):
    t, d = x.shape
    tm = min(TOKEN_TILE, t)
    vec = lambda i: (i // tiles_per_group, 0, 0)
    row = lambda w: pl.BlockSpec((tm, w), lambda i: (i, 0))
    ne = LANES
    rhi, rlo = _split_hi_lo(jnp.pad(router_w.astype(F32), ((0, 0), (0, ne - router_w.shape[1]))))
    return pl.pallas_call(
        _outproj_kernel,
        grid=(t // tm,),
        in_specs=[row(d)] + [row(GROUP_W)] * 11
                 + [_full(w_out.shape), _full((1, d)), pl.BlockSpec((1, 1, d), vec), _full((1, d)),
                    pl.BlockSpec((1, 1, d), vec), pl.BlockSpec((1, 1, d), vec),
                    _full((1, GROUP_W)), _full((1, GROUP_W)), _full((1, GROUP_W)), _full(rhi.shape), _full(rlo.shape)],
        out_specs=[row(d), row(d), row(ne)],
        out_shape=[jax.ShapeDtypeStruct((t, d), F32), jax.ShapeDtypeStruct((t, d), MXU_DTYPE),
                   jax.ShapeDtypeStruct((t, ne), F32)],
        compiler_params=_cp("parallel"),
        name="out_projection",
    )(x, *mixers, w_out, gpost, ga1, gpre, sc2, sh2, dskip, cnorm, dnorm, rhi, rlo)


def _rank_before(vals, idx, count, stride):
    rank = jnp.zeros(vals.shape, jnp.int32)
    for j in range(count):
        other = vals[j * stride:j * stride + 1, :]
        ahead = (other > vals) | ((other == vals) & (idx > j))
        rank = rank + ahead.astype(jnp.int32)
    return rank


def _xor_partner(x, row, s):
    n = x.shape[0]
    return jnp.where((row & s) == 0, pltpu.roll(x, n - s, 0), pltpu.roll(x, s, 0))


def _router_kernel(lg_ref, rb_ref, gate_ref):
    ne = N_EXPERTS
    gsz = ne // N_EXPERT_GROUPS
    scores = _sigmoid(lg_ref[...].T[:ne, :])
    tm = scores.shape[1]
    biased = scores + rb_ref[...]
    row = lax.broadcasted_iota(jnp.int32, (ne, tm), 0)
    m1, m2 = biased, jnp.full((ne, tm), -jnp.inf, F32)
    s = 1
    while s < gsz:
        o1, o2 = _xor_partner(m1, row, s), _xor_partner(m2, row, s)
        m2 = jnp.maximum(jnp.minimum(m1, o1), jnp.maximum(m2, o2))
        m1 = jnp.maximum(m1, o1)
        s *= 2
    gidx = row >> (gsz.bit_length() - 1)
    group_ok = _rank_before(m1 + m2, gidx, N_EXPERT_GROUPS, gsz) < TOPK_GROUPS
    choice = jnp.where(group_ok, biased, -jnp.inf)
    picked = _rank_before(choice, row, ne, 1) < TOP_K
    gate = jnp.where(picked, scores, 0.0)
    gate = gate / jnp.sum(gate, axis=0, keepdims=True) * ROUTED_SCALE
    gate_ref[...] = jnp.concatenate([gate, jnp.zeros((LANES - ne, tm), F32)], axis=0).T


def router_gates(logits, router_b):
    t, w = logits.shape
    tm = min(TOKEN_TILE, t)
    return pl.pallas_call(
        _router_kernel,
        grid=(t // tm,),
        in_specs=[pl.BlockSpec((tm, w), lambda i: (i, 0)), _full((N_EXPERTS, 1))],
        out_specs=pl.BlockSpec((tm, w), lambda i: (i, 0)),
        out_shape=jax.ShapeDtypeStruct((t, w), F32),
        compiler_params=_cp("parallel"),
        name="router_gates",
    )(logits, router_b.reshape(N_EXPERTS, 1).astype(F32))


def _moe_kernel(h_ref, gate_ref, x_ref, wg_ref, wu_ref, wd_ref, sg_ref, su_ref, sd_ref, gpost_ref, ga2_ref,
                o_ref, acc_s):
    e = pl.program_id(1)
    h = h_ref[...]

    @pl.when(e == 0)
    def _():
        hs = _silu(jnp.dot(h, sg_ref[...], preferred_element_type=F32)) * jnp.dot(h, su_ref[...], preferred_element_type=F32)
        acc_s[...] = jnp.dot(_mx(hs), sd_ref[...], preferred_element_type=F32)

    gates = gate_ref[...]
    lane = lax.broadcasted_iota(jnp.int32, gates.shape, 1)
    hid = []
    for j in range(MOE_EB):
        gcol = jnp.sum(jnp.where(lane == e * MOE_EB + j, gates, 0.0), axis=1, keepdims=True)
        g = jnp.dot(h, wg_ref[j], preferred_element_type=F32)
        u = jnp.dot(h, wu_ref[j], preferred_element_type=F32)
        hid.append(_mx(_silu(g) * u * gcol))
    wd = wd_ref[...].reshape(MOE_EB * D_EXPERT, -1)
    acc_s[...] += jnp.dot(jnp.concatenate(hid, axis=1), wd, preferred_element_type=F32)

    @pl.when(e == pl.num_programs(1) - 1)
    def _():
        o_ref[...] = x_ref[...] + ga2_ref[0] * _rms(acc_s[...], gpost_ref[...])


def moe_ffn(h, gates, x, wg, wu, wd, sg, su, sd, gpost, ga2, tiles_per_group):
    t, d = x.shape
    tm = min(MOE_TILE, t)
    ne, _, f = wg.shape
    row = lambda w: pl.BlockSpec((tm, w), lambda i, e: (i, 0))
    return pl.pallas_call(
        _moe_kernel,
        grid=(t // tm, ne // MOE_EB),
        in_specs=[row(d), row(gates.shape[1]), row(d),
                  pl.BlockSpec((MOE_EB, d, f), lambda i, e: (e, 0, 0)),
                  pl.BlockSpec((MOE_EB, d, f), lambda i, e: (e, 0, 0)),
                  pl.BlockSpec((MOE_EB, f, d), lambda i, e: (e, 0, 0)),
                  _full(sg.shape), _full(su.shape), _full(sd.shape), _full((1, d)),
                  pl.BlockSpec((1, 1, d), lambda i, e: (i // tiles_per_group, 0, 0))],
        out_specs=row(d),
        out_shape=jax.ShapeDtypeStruct((t, d), F32),
        scratch_shapes=[pltpu.VMEM((tm, d), F32)],
        compiler_params=_cp("parallel", "arbitrary"),
        name="moe_ffn",
    )(h, gates, x, wg, wu, wd, sg, su, sd, gpost, ga2)


def _reorder_w_in(w_in):
    c = np.cumsum((0,) + (GROUP_W, GROUP_W, GROUP_W, GROUP_W, GROUP_W, GROUP_W, 2 * SSD_STATE, 2 * SSD_STATE,
                          GROUP_W, 2 * N_HEADS, GROUP_W, GROUP_W, GROUP_W, GROUP_W, 2 * N_HEADS, 2 * N_HEADS))
    seg = lambda a, b: w_in[:, c[a]:c[b]]
    small = jnp.concatenate([seg(9, 10), seg(14, 15), seg(15, 16),
                             jnp.zeros((w_in.shape[0], LANES - 6 * N_HEADS), w_in.dtype)], axis=1)
    return jnp.concatenate([seg(0, 5), seg(5, 8), seg(8, 9), seg(10, 13), seg(13, 14), small], axis=1)


def kernel(x, c, ctx, c_ctx, w_mod, b_mod, g_pre_mix, g_post_mix, g_pre_ffn, g_post_ffn, w_in, w_out, lru_conv_w, lru_conv_b, lru_wa, lru_ba, lru_wx, lru_bx, lru_lambda, na_bias, ssd_conv_w, ssd_conv_b, ssd_a_log, ssd_dt_bias, ssd_d, ssd_norm, gdn_conv_w, gdn_a_log, gdn_dt_bias, gdn_norm, router_w, router_b, we_gate, we_up, we_down, ws_gate, ws_up, ws_down):
    bsz, seq, d = x.shape
    n_ctx = ctx.shape[1]
    depth = w_mod.shape[0]
    lat_tpg = seq // min(TOKEN_TILE, seq)
    lat_mpg = seq // min(MOE_TILE, seq)
    ctx_tpg = max(bsz * n_ctx // TOKEN_TILE, 1)
    ctx_mpg = max(bsz * n_ctx // MOE_TILE, 1)

    cond = _pad_rows(jnp.concatenate([c, c_ctx[None, :]], axis=0))
    mod = modulation(cond, w_mod, b_mod).reshape(depth, SUBLANES, N_MOD, d)
    rope = rope_tables(seq)
    row = lambda v: v[None, :].astype(F32)

    xl = x.reshape(bsz * seq, d)
    xc = ctx.reshape(bsz * n_ctx, d)
    for l in range(depth):
        last = l == depth - 1
        m_lat = [mod[l, :bsz, k][:, None, :] for k in range(N_MOD)]
        m_ctx = [mod[l, bsz:bsz + 1, k][:, None, :] for k in range(N_MOD)]
        w_in_l = _reorder_w_in(w_in[l]).astype(MXU_DTYPE)
        pc = in_projection(xc, row(g_pre_mix[l]), m_ctx[1], m_ctx[0], w_in_l, ctx_tpg)
        pl_ = in_projection(xl, row(g_pre_mix[l]), m_lat[1], m_lat[0], w_in_l, lat_tpg)

        lru_p = lru_params(lru_conv_w[l], lru_conv_b[l], lru_wa[l], lru_ba[l], lru_wx[l], lru_bx[l], lru_lambda[l])
        a_cf, a_cb, a_st = lru_mixer(pc[P_AX], jnp.zeros((bsz, SUBLANES, GROUP_W), F32), bsz, *lru_p)
        a_lf, a_lb, _ = lru_mixer(pl_[P_AX], a_st, bsz, *lru_p)

        kc = pc[P_BK].reshape(bsz, n_ctx, GROUP_W)
        vc = pc[P_BV].reshape(bsz, n_ctx, GROUP_W)
        b_c = ctx_attention(pc[P_BQ].reshape(bsz, n_ctx, GROUP_W), kc, vc).reshape(bsz * n_ctx, GROUP_W)
        b_l = na_mixer(pl_[P_BQ], pl_[P_BK], pl_[P_BV], kc, vc, na_bias_slabs(na_bias[l]), bsz)

        ssd_p = ssd_params(ssd_conv_w[l], ssd_conv_b[l], ssd_a_log[l], ssd_dt_bias[l])
        c_cf, c_cb, c_cx, c_st = ssd_mixer(pc[P_CX], pc[P_SM], jnp.zeros((bsz, 2, N_HEADS, SSD_STATE, HEAD_DIM), F32),
                                           bsz, *ssd_p)
        c_lf, c_lb, c_lx, _ = ssd_mixer(pl_[P_CX], pl_[P_SM], c_st, bsz, *ssd_p)

        gdn_p = gdn_params(gdn_conv_w[l], gdn_a_log[l], gdn_dt_bias[l])
        d_cf, d_cb, d_st = gdn_mixer(pc[P_DX], pc[P_SM], jnp.zeros((bsz, 2, N_HEADS, HEAD_DIM, HEAD_DIM), F32),
                                     bsz, *gdn_p)
        d_lf, d_lb, _ = gdn_mixer(pl_[P_DX], pl_[P_SM], d_st, bsz, *gdn_p, rope=rope)

        epi = (w_out[l].astype(MXU_DTYPE), row(g_post_mix[l]))
        epi_tail = (row(jnp.repeat(ssd_d[l], HEAD_DIM)), row(ssd_norm[l]), row(jnp.tile(gdn_norm[l], N_HEADS)), router_w[l])
        moe_w = (we_gate[l].astype(MXU_DTYPE), we_up[l].astype(MXU_DTYPE), we_down[l].astype(MXU_DTYPE),
                 ws_gate[l].astype(MXU_DTYPE), ws_up[l].astype(MXU_DTYPE), ws_down[l].astype(MXU_DTYPE),
                 row(g_post_ffn[l]))

        mix_l = (a_lf, a_lb, pl_[P_AG], b_l, c_lf, c_lb, c_lx, pl_[P_CZ], d_lf, d_lb, pl_[P_DZ])
        xl, h2, lg = out_projection(xl, mix_l, *epi, m_lat[2], row(g_pre_ffn[l]), m_lat[4], m_lat[3], *epi_tail, lat_tpg)
        xl = moe_ffn(h2, router_gates(lg, router_b[l]), xl, *moe_w, m_lat[5], lat_mpg)
        if not last:
            mix_c = (a_cf, a_cb, pc[P_AG], b_c, c_cf, c_cb, c_cx, pc[P_CZ], d_cf, d_cb, pc[P_DZ])
            xc, h2, lg = out_projection(xc, mix_c, *epi, m_ctx[2], row(g_pre_ffn[l]), m_ctx[4], m_ctx[3], *epi_tail, ctx_tpg)
            xc = moe_ffn(h2, router_gates(lg, router_b[l]), xc, *moe_w, m_ctx[5], ctx_mpg)
    return xl.reshape(bsz, seq, d)
```

```python
import functools
import math

import jax
import jax.numpy as jnp
import numpy as np
from jax import lax
from jax.experimental import pallas as pl
from jax.experimental.pallas import tpu as pltpu

F32 = jnp.float32
MXU_DTYPE = jnp.bfloat16
HI = lax.Precision.HIGHEST

D_MODEL = 1024
GRID_W = 64
GROUP_W = 256
HEAD_DIM = 64
N_HEADS = 4
EPS = 1e-6
NEG_INF = -1e30
N_MOD = 6
LRU_C = 8.0
NA_WIN_ROWS = 8
NA_WIN_COLS = 16
SSD_STATE = 128
SSD_GROUPS = 2
ROPE_BASE = 10000.0
ROPE_AXIS_DIM = HEAD_DIM // 2
N_EXPERTS = 64
N_EXPERT_GROUPS = 8
TOPK_GROUPS = 4
TOP_K = 8
D_EXPERT = 256
ROUTED_SCALE = 2.5

LANES = 128
SUBLANES = 8
VMEM_LIMIT = 56 * 1024 * 1024

TOKEN_TILE = 512
LRU_CHUNK = 256
SSD_CHUNK = 128
GDN_CHUNK = 64
GDN_TILE = 256
GDN_BASE = 16
MOE_TILE = 1024
MOE_EB = 2

P_WIDTHS = (256, 256, 256, 256, 256, 768, 256, 768, 256, 128)
(P_AX, P_AG, P_BQ, P_BK, P_BV, P_CX, P_CZ, P_DX, P_DZ, P_SM) = range(10)
SM_DT, SM_BETA, SM_DECAY = 0, 8, 16


def _cp(*sem):
    return pltpu.CompilerParams(dimension_semantics=sem, vmem_limit_bytes=VMEM_LIMIT)


def _mx(x):
    return x.astype(MXU_DTYPE)


def _dot(a, b):
    return jnp.dot(_mx(a), _mx(b), preferred_element_type=F32)


def _dot_nt(a, b):
    return lax.dot_general(_mx(a), _mx(b), (((1,), (1,)), ((), ())), preferred_element_type=F32)


def _dot_tn(a, b):
    return lax.dot_general(_mx(a), _mx(b), (((0,), (0,)), ((), ())), preferred_element_type=F32)


def _dot_hi(a, b):
    return jnp.dot(a, b, preferred_element_type=F32, precision=HI)


def _sigmoid(x):
    return 1.0 / (1.0 + jnp.exp(-x))


def _silu(x):
    return x * _sigmoid(x)


def _softplus(x):
    return jnp.maximum(x, 0.0) + jnp.log1p(jnp.exp(-jnp.abs(x)))


def _gelu_tanh(x):
    return 0.5 * x * (1.0 + jnp.tanh(math.sqrt(2.0 / math.pi) * (x + 0.044715 * (x * x * x))))


def _rms(x, g):
    return x * lax.rsqrt(jnp.mean(x * x, axis=-1, keepdims=True) + EPS) * g


def _full(shape):
    n = len(shape)
    return pl.BlockSpec(shape, lambda *_: (0,) * n)


MOD_COLS = 1536


def _mod_kernel(c_ref, w_ref, b_ref, o_ref):
    o_ref[0] = _dot_hi(_silu(c_ref[...]), w_ref[0]) + b_ref[0]


def modulation(cond, w_mod, b_mod):
    depth, d, n = w_mod.shape
    return pl.pallas_call(
        _mod_kernel,
        grid=(depth, n // MOD_COLS),
        in_specs=[pl.BlockSpec((SUBLANES, d), lambda l, j: (0, 0)),
                  pl.BlockSpec((1, d, MOD_COLS), lambda l, j: (l, 0, j)),
                  pl.BlockSpec((1, 1, MOD_COLS), lambda l, j: (l, 0, j))],
        out_specs=pl.BlockSpec((1, SUBLANES, MOD_COLS), lambda l, j: (l, 0, j)),
        out_shape=jax.ShapeDtypeStruct((depth, SUBLANES, n), F32),
        compiler_params=_cp("parallel", "parallel"),
        name="modulation",
    )(cond, w_mod, b_mod.reshape(depth, 1, n))


def _inproj_kernel(x_ref, g_ref, sc_ref, sh_ref, w_ref, *o_refs):
    h = _rms(x_ref[...], g_ref[...]) * (1.0 + sc_ref[0]) + sh_ref[0]
    p = _dot(h, w_ref[...])
    off = 0
    for o_ref, w in zip(o_refs, P_WIDTHS):
        o_ref[...] = p[:, off:off + w]
        off += w


def in_projection(x, g, sc, sh, w, tiles_per_group):
    t, d = x.shape
    tm = min(TOKEN_TILE, t)
    vec = lambda i: (i // tiles_per_group, 0, 0)
    return pl.pallas_call(
        _inproj_kernel,
        grid=(t // tm,),
        in_specs=[pl.BlockSpec((tm, d), lambda i: (i, 0)),
                  _full((1, d)),
                  pl.BlockSpec((1, 1, d), vec),
                  pl.BlockSpec((1, 1, d), vec),
                  _full(w.shape)],
        out_specs=[pl.BlockSpec((tm, wd), lambda i: (i, 0)) for wd in P_WIDTHS],
        out_shape=[jax.ShapeDtypeStruct((t, wd), F32) for wd in P_WIDTHS],
        compiler_params=_cp("parallel"),
        name="in_projection",
    )(x, g, sc, sh, w)


def _halo(p, bsz, q):
    c = p.shape[-1]
    pr = p.reshape(bsz, -1, q, c)
    nc = pr.shape[1]
    prev = jnp.concatenate([jnp.zeros((bsz, 1, 2, c), p.dtype), pr[:, :-1, q - 2:, :]], axis=1)
    nxt = jnp.concatenate([pr[:, 1:, :1, :], jnp.zeros((bsz, 1, 1, c), p.dtype)], axis=1)
    pad = jnp.zeros((bsz, nc, SUBLANES - 3, c), p.dtype)
    return jnp.concatenate([prev, nxt, pad], axis=2).reshape(bsz * nc, SUBLANES, c)


def _dwconv(x, halo, w, b=None):
    q = x.shape[0]
    row = lax.broadcasted_iota(jnp.int32, x.shape, 0)
    xm2 = jnp.where(row == 0, halo[0:1], jnp.where(row == 1, halo[1:2], pltpu.roll(x, 2, 0)))
    xm1 = jnp.where(row == 0, halo[1:2], pltpu.roll(x, 1, 0))
    xp1 = jnp.where(row == q - 1, halo[2:3], pltpu.roll(x, q - 1, 0))
    y = w[0:1] * xm2 + w[1:2] * xm1 + w[2:3] * x + w[3:4] * xp1
    return y if b is None else y + b


def _pad_rows(a, rows=SUBLANES):
    return jnp.concatenate([a, jnp.zeros((rows - a.shape[0],) + a.shape[1:], a.dtype)], axis=0)


def _chunk_specs(nc, q, c):
    fwd = pl.BlockSpec((q, c), lambda b, i: (b * nc + i, 0))
    bwd = pl.BlockSpec((q, c), lambda b, i: (b * nc + nc - 1 - i, 0))
    return fwd, bwd


def _halo_specs(nc, c):
    fwd = pl.BlockSpec((1, SUBLANES, c), lambda b, i: (b * nc + i, 0, 0))
    bwd = pl.BlockSpec((1, SUBLANES, c), lambda b, i: (b * nc + nc - 1 - i, 0, 0))
    return fwd, bwd


def _lru_kernel(xf_ref, xb_ref, hf_ref, hb_ref, h0_ref, cw_ref, cb_ref, wg_ref, bg_ref, lam_ref,
                yf_ref, yb_ref, hfin_ref, af_s, bf_s, ab_s, bb_s, carry_s):
    i = pl.program_id(1)
    q = xf_ref.shape[0]

    @pl.when(i == 0)
    def _():
        carry_s[...] = h0_ref[0]

    def coeffs(x_ref, halo_ref, d, a_s, b_s):
        u = _dwconv(x_ref[...], halo_ref[0], cw_ref[...], cb_ref[...])
        g = _dot(u, wg_ref[:, 2 * GROUP_W * d:2 * GROUP_W * (d + 1)]) + bg_ref[:, 2 * GROUP_W * d:2 * GROUP_W * (d + 1)]
        r = _sigmoid(g[:, :GROUP_W])
        gate_in = _sigmoid(g[:, GROUP_W:])
        log_a = -LRU_C * r * _softplus(-lam_ref[d:d + 1, :])
        a_s[...] = jnp.exp(log_a)
        b_s[...] = jnp.sqrt(1.0 - jnp.exp(2.0 * log_a)) * (gate_in * u)

    coeffs(xf_ref, hf_ref, 0, af_s, bf_s)
    coeffs(xb_ref, hb_ref, 1, ab_s, bb_s)

    ng = q // SUBLANES
    row = lax.broadcasted_iota(jnp.int32, (SUBLANES, GROUP_W), 0)

    def body(g, hs):
        h_f, h_b = hs
        i0 = pl.multiple_of(g * SUBLANES, SUBLANES)
        a = af_s[pl.ds(i0, SUBLANES), :]
        b = bf_s[pl.ds(i0, SUBLANES), :]
        for s in (1, 2, 4):
            m = row >= s
            b = jnp.where(m, a * pltpu.roll(b, s, 0) + b, b)
            a = jnp.where(m, a * pltpu.roll(a, s, 0), a)
        h = b + a * h_f
        yf_ref[pl.ds(i0, SUBLANES), :] = h
        h_f = h[SUBLANES - 1:SUBLANES, :]
        j0 = pl.multiple_of((ng - 1 - g) * SUBLANES, SUBLANES)
        a = ab_s[pl.ds(j0, SUBLANES), :]
        b = bb_s[pl.ds(j0, SUBLANES), :]
        for s in (1, 2, 4):
            m = row < SUBLANES - s
            b = jnp.where(m, a * pltpu.roll(b, SUBLANES - s, 0) + b, b)
            a = jnp.where(m, a * pltpu.roll(a, SUBLANES - s, 0), a)
        h = b + a * h_b
        yb_ref[pl.ds(j0, SUBLANES), :] = h
        return h_f, h[0:1, :]

    h_f, h_b = lax.fori_loop(0, ng, body, (carry_s[0:1, :], carry_s[1:2, :]))
    carry_s[0:1, :] = h_f
    carry_s[1:2, :] = h_b

    @pl.when(i == pl.num_programs(1) - 1)
    def _():
        hfin_ref[0] = carry_s[...]


def _block_diag(w):
    h, a, b = w.shape
    return jnp.einsum('hij,hg->higj', w, jnp.eye(h, dtype=w.dtype)).reshape(h * a, h * b)


def lru_params(conv_w, conv_b, wa, ba, wx, bx, lam):
    wg = jnp.concatenate([_block_diag(wa[0]), _block_diag(wx[0]), _block_diag(wa[1]), _block_diag(wx[1])], axis=1)
    bg = jnp.concatenate([ba[0], bx[0], ba[1], bx[1]])[None, :]
    return _pad_rows(conv_w), conv_b[None, :], wg.astype(MXU_DTYPE), bg, _pad_rows(lam)


def lru_mixer(x, h0, bsz, cw, cb, wg, bg, lam):
    t, c = x.shape
    s = t // bsz
    q = min(LRU_CHUNK, s)
    nc = s // q
    halo = _halo(x, bsz, q)
    xf, xb = _chunk_specs(nc, q, c)
    hf, hb = _halo_specs(nc, c)
    st = pl.BlockSpec((1, SUBLANES, c), lambda b, i: (b, 0, 0))
    return pl.pallas_call(
        _lru_kernel,
        grid=(bsz, nc),
        in_specs=[xf, xb, hf, hb, st, _full(cw.shape), _full(cb.shape), _full(wg.shape), _full(bg.shape),
                  _full(lam.shape)],
        out_specs=[xf, xb, st],
        out_shape=[jax.ShapeDtypeStruct((t, c), F32), jax.ShapeDtypeStruct((t, c), F32),
                   jax.ShapeDtypeStruct((bsz, SUBLANES, c), F32)],
        scratch_shapes=[pltpu.VMEM((q, c), F32)] * 4 + [pltpu.VMEM((SUBLANES, c), F32)],
        compiler_params=_cp("parallel", "arbitrary"),
        name="lru_mixer",
    )(x, x, halo, halo, h0, cw, cb, wg, bg, lam)


NA_KEYS = NA_WIN_ROWS * GRID_W


def na_bias_slabs(table):
    qc = np.arange(GRID_W)[:, None]
    kc = np.arange(GRID_W)[None, :]
    win0 = np.clip(qc - NA_WIN_COLS // 2, 0, GRID_W - NA_WIN_COLS)
    ok = (kc >= win0) & (kc < win0 + NA_WIN_COLS)
    dc = np.clip(kc - qc + NA_WIN_COLS - 1, 0, 2 * NA_WIN_COLS - 2)
    dr = np.arange(NA_WIN_ROWS)[:, None] + np.arange(NA_WIN_ROWS)[None, :]
    b = table.astype(F32)[:, dr][:, :, :, dc]
    b = jnp.where(ok[None, None, None], b, NEG_INF)
    h = table.shape[0]
    return b.transpose(0, 1, 3, 2, 4).reshape(h, NA_WIN_ROWS, GRID_W, NA_KEYS)


def _na_kernel(q_ref, kw_ref, vw_ref, kc_ref, vc_ref, slab_ref, o_ref, *, rows):
    r = pl.program_id(1)
    r0 = jnp.clip(r - NA_WIN_ROWS // 2, 0, rows - NA_WIN_ROWS)
    o = r0 - r + NA_WIN_ROWS - 1
    q = q_ref[...] * (HEAD_DIM ** -0.5)
    kw, vw, kc, vc = kw_ref[...], vw_ref[...], kc_ref[0], vc_ref[0]
    outs = []
    for h in range(N_HEADS):
        sl = slice(h * HEAD_DIM, (h + 1) * HEAD_DIM)
        s_loc = _dot_nt(q[:, sl], kw[:, sl]) + slab_ref[h, o]
        s_ctx = _dot_nt(q[:, sl], kc[:, sl])
        m = jnp.maximum(jnp.max(s_loc, axis=-1, keepdims=True), jnp.max(s_ctx, axis=-1, keepdims=True))
        p_loc = jnp.exp(s_loc - m)
        p_ctx = jnp.exp(s_ctx - m)
        den = jnp.sum(p_loc, axis=-1, keepdims=True) + jnp.sum(p_ctx, axis=-1, keepdims=True)
        outs.append((_dot(p_loc, vw[:, sl]) + _dot(p_ctx, vc[:, sl])) / den)
    o_ref[...] = jnp.concatenate(outs, axis=1)


def na_mixer(q, k, v, kc, vc, slabs, bsz):
    t, c = q.shape
    s = t // bsz
    rows = s // GRID_W
    n_ctx = kc.shape[1]

    def win(b, r):
        r0 = jnp.clip(r - NA_WIN_ROWS // 2, 0, rows - NA_WIN_ROWS)
        return ((b * rows + r0) * GRID_W, 0)

    wspec = pl.BlockSpec((pl.Element(NA_KEYS), pl.Element(c)), win)
    cspec = pl.BlockSpec((1, n_ctx, c), lambda b, r: (b, 0, 0))
    qspec = pl.BlockSpec((GRID_W, c), lambda b, r: (b * rows + r, 0))
    return pl.pallas_call(
        functools.partial(_na_kernel, rows=rows),
        grid=(bsz, rows),
        in_specs=[qspec, wspec, wspec, cspec, cspec, _full(slabs.shape)],
        out_specs=qspec,
        out_shape=jax.ShapeDtypeStruct((t, c), F32),
        compiler_params=_cp("parallel", "arbitrary"),
        name="na_mixer",
    )(q, k, v, kc, vc, slabs)


def _ctx_attn_kernel(q_ref, k_ref, v_ref, o_ref):
    q = q_ref[0] * (HEAD_DIM ** -0.5)
    k, v = k_ref[0], v_ref[0]
    outs = []
    for h in range(N_HEADS):
        sl = slice(h * HEAD_DIM, (h + 1) * HEAD_DIM)
        s = _dot_nt(q[:, sl], k[:, sl])
        p = jnp.exp(s - jnp.max(s, axis=-1, keepdims=True))
        outs.append(_dot(p, v[:, sl]) / jnp.sum(p, axis=-1, keepdims=True))
    o_ref[0] = jnp.concatenate(outs, axis=1)


def ctx_attention(q, k, v):
    spec = pl.BlockSpec((1,) + q.shape[1:], lambda b: (b, 0, 0))
    return pl.pallas_call(
        _ctx_attn_kernel,
        grid=(q.shape[0],),
        in_specs=[spec, spec, spec],
        out_specs=spec,
        out_shape=jax.ShapeDtypeStruct(q.shape, F32),
        compiler_params=_cp("parallel"),
        name="ctx_attention",
    )(q, k, v)


def _small_vec(vals, off):
    v = jnp.zeros((LANES,), F32).at[off:off + 2 * N_HEADS].set(vals.reshape(-1).astype(F32))
    return v[None, :]


def _lane_mask(off):
    lane = lax.broadcasted_iota(jnp.int32, (1, LANES), 1)
    return (lane >= off) & (lane < off + 2 * N_HEADS)


def _tri_masks(q):
    rowi = lax.broadcasted_iota(jnp.int32, (q, q), 0)
    coli = lax.broadcasted_iota(jnp.int32, (q, q), 1)
    return rowi, coli


def _ssd_kernel(xf_ref, xb_ref, hf_ref, hb_ref, sf_ref, sb_ref, h0_ref, cw_ref, cb_ref, dtb_ref, alog_ref,
                yf_ref, yb_ref, xc_ref, hfin_ref, state_s):
    i = pl.program_id(1)
    q = xf_ref.shape[0]

    @pl.when(i == 0)
    def _():
        state_s[...] = h0_ref[0]

    rowi, coli = _tri_masks(q)
    a_neg = jnp.where(_lane_mask(SM_DT), -jnp.exp(alog_ref[...]), 0.0)

    def direction(x_ref, halo_ref, sm_ref, d, y_ref):
        xbc = _silu(_dwconv(x_ref[...], halo_ref[0], cw_ref[...], cb_ref[...]))
        if d == 0:
            xc_ref[...] = xbc[:, :GROUP_W]
        dt = _softplus(sm_ref[...] + dtb_ref[...])
        keep = (rowi >= coli) if d == 0 else (rowi <= coli)
        acum = _dot_hi(keep.astype(F32), dt * a_neg)
        acum_t = acum.T
        last = acum[q - 1:q, :] if d == 0 else acum[0:1, :]
        dec_end = jnp.exp(last - acum)
        e_acum = jnp.exp(acum)
        e_last = jnp.exp(last)
        ys = []
        for g in range(SSD_GROUPS):
            bg = xbc[:, GROUP_W + SSD_STATE * g:GROUP_W + SSD_STATE * (g + 1)]
            cg = xbc[:, GROUP_W + SSD_STATE * (SSD_GROUPS + g):GROUP_W + SSD_STATE * (SSD_GROUPS + g + 1)]
            cbt = _dot_nt(cg, bg)
            for hh in range(N_HEADS // SSD_GROUPS):
                h = g * (N_HEADS // SSD_GROUPS) + hh
                ln = SM_DT + N_HEADS * d + h
                lmat = jnp.exp(jnp.where(keep, acum[:, ln:ln + 1] - acum_t[ln:ln + 1, :], NEG_INF))
                xdt = xbc[:, h * HEAD_DIM:(h + 1) * HEAD_DIM] * dt[:, ln:ln + 1]
                st = state_s[d, h]
                ys.append(_dot(cbt * lmat, xdt) + _dot(cg * e_acum[:, ln:ln + 1], st))
                state_s[d, h] = st * e_last[:, ln:ln + 1] + _dot_tn(bg * dec_end[:, ln:ln + 1], xdt)
        y_ref[...] = jnp.concatenate(ys, axis=1)

    direction(xf_ref, hf_ref, sf_ref, 0, yf_ref)
    direction(xb_ref, hb_ref, sb_ref, 1, yb_ref)

    @pl.when(i == pl.num_programs(1) - 1)
    def _():
        hfin_ref[0] = state_s[...]


def ssd_params(conv_w, conv_b, a_log, dt_bias):
    return _pad_rows(conv_w), conv_b[None, :], _small_vec(dt_bias, SM_DT), _small_vec(a_log, SM_DT)


def ssd_mixer(xbc, sm, h0, bsz, cw, cb, dtb, alog):
    t, c = xbc.shape
    s = t // bsz
    q = min(SSD_CHUNK, s)
    nc = s // q
    halo = _halo(xbc, bsz, q)
    xf, xb = _chunk_specs(nc, q, c)
    hf, hb = _halo_specs(nc, c)
    sf, sb = _chunk_specs(nc, q, LANES)
    yf, yb = _chunk_specs(nc, q, GROUP_W)
    st = pl.BlockSpec((1,) + h0.shape[1:], lambda b, i: (b, 0, 0, 0, 0))
    y_shape = jax.ShapeDtypeStruct((t, GROUP_W), F32)
    return pl.pallas_call(
        _ssd_kernel,
        grid=(bsz, nc),
        in_specs=[xf, xb, hf, hb, sf, sb, st, _full(cw.shape), _full(cb.shape), _full(dtb.shape), _full(alog.shape)],
        out_specs=[yf, yb, yf, st],
        out_shape=[y_shape, y_shape, y_shape, jax.ShapeDtypeStruct(h0.shape, F32)],
        scratch_shapes=[pltpu.VMEM(h0.shape[1:], F32)],
        compiler_params=_cp("parallel", "arbitrary"),
        name="ssd_mixer",
    )(xbc, xbc, halo, halo, sm, sm, h0, cw, cb, dtb, alog)


def rope_tables(seq):
    t = jnp.arange(seq)
    row = (t // GRID_W).astype(F32)
    col = (t % GRID_W).astype(F32)
    inv = ROPE_BASE ** (-jnp.arange(0, ROPE_AXIS_DIM, 2, dtype=F32) / ROPE_AXIS_DIM)
    ar, ac = row[:, None] * inv, col[:, None] * inv
    cos = jnp.concatenate([jnp.cos(ar), jnp.cos(ar), jnp.cos(ac), jnp.cos(ac)], axis=1)
    sin = jnp.concatenate([-jnp.sin(ar), jnp.sin(ar), -jnp.sin(ac), jnp.sin(ac)], axis=1)
    return jnp.tile(cos, (1, N_HEADS)), jnp.tile(sin, (1, N_HEADS))


def _swap16(x):
    lane = lax.broadcasted_iota(jnp.int32, x.shape, 1)
    half = ROPE_AXIS_DIM // 2
    return jnp.where((lane & (ROPE_AXIS_DIM - 1)) < half,
                     pltpu.roll(x, x.shape[1] - half, 1), pltpu.roll(x, half, 1))


def _l2norm_heads(x):
    outs = []
    for h in range(N_HEADS):
        xh = x[:, h * HEAD_DIM:(h + 1) * HEAD_DIM]
        outs.append(xh * lax.rsqrt(jnp.sum(xh * xh, axis=-1, keepdims=True) + EPS))
    return jnp.concatenate(outs, axis=1)


def _dot_tri(mask, x):
    m = _mx(mask.astype(F32))
    x1 = _mx(x)
    r1 = x - x1.astype(F32)
    x2 = _mx(r1)
    x3 = _mx(r1 - x2.astype(F32))
    return (jnp.dot(m, x1, preferred_element_type=F32) + jnp.dot(m, x2, preferred_element_type=F32)
            + jnp.dot(m, x3, preferred_element_type=F32))


def _same_block(rowi, coli, n):
    sh = n.bit_length() - 1
    return (rowi >> sh) == (coli >> sh)


def _solve_unit_tri(a_list, rhs_list, rowi, coli, chunk):
    eye = (rowi == coli).astype(F32)
    base = [jnp.where(_same_block(rowi, coli, GDN_BASE), a, 0.0) for a in a_list]
    ts = [eye - ab for ab in base]
    ps = [_dot(ab, ab) for ab in base]
    ts = [t + _dot(t, p) for t, p in zip(ts, ps)]
    n = 4
    while n < GDN_BASE:
        ps = [_dot(p, p) for p in ps]
        ts = [t + _dot(t, p) for t, p in zip(ts, ps)]
        n *= 2
    n = GDN_BASE
    while 2 * n < chunk:
        inner = _same_block(rowi, coli, 2 * n) & jnp.logical_not(_same_block(rowi, coli, n))
        offs = [jnp.where(inner, a, 0.0) for a in a_list]
        ts = [t - _dot(_dot(t, off), t) for t, off in zip(ts, offs)]
        n *= 2
    outer = jnp.logical_not(_same_block(rowi, coli, n))
    offs = [jnp.where(outer, a, 0.0) for a in a_list]
    ys = [_dot(t, r) for t, r in zip(ts, rhs_list)]
    zs = [_dot(off, y) for off, y in zip(offs, ys)]
    return [y - _dot(t, z) for y, t, z in zip(ys, ts, zs)]


def _gdn_kernel(*refs, rope):
    if rope:
        (xf_ref, xb_ref, hf_ref, hb_ref, sf_ref, sb_ref, cf_ref, cb_ref, nf_ref, nb_ref,
         s0_ref, cw_ref, alog_ref, dtb_ref, of_ref, ob_ref, sfin_ref, state_s) = refs
    else:
        (xf_ref, xb_ref, hf_ref, hb_ref, sf_ref, sb_ref,
         s0_ref, cw_ref, alog_ref, dtb_ref, of_ref, ob_ref, sfin_ref, state_s) = refs
        cf_ref = cb_ref = nf_ref = nb_ref = None
    i = pl.program_id(1)
    tq = xf_ref.shape[0]
    ck = min(GDN_CHUNK, tq)
    nck = tq // ck

    @pl.when(i == 0)
    def _():
        state_s[...] = s0_ref[0]

    rowi, coli = _tri_masks(tq)
    in_chunk = _same_block(rowi, coli, ck)
    a_neg = jnp.where(_lane_mask(SM_DECAY), -jnp.exp(alog_ref[...]), 0.0)

    a_list, rhs_list, qkm, qg, kd, e_last = [], [], [], [], [], []
    for d, (x_ref, halo_ref, sm_ref, cos_ref, sin_ref) in enumerate(
            ((xf_ref, hf_ref, sf_ref, cf_ref, nf_ref), (xb_ref, hb_ref, sb_ref, cb_ref, nb_ref))):
        qkv = _silu(_dwconv(x_ref[...], halo_ref[0], cw_ref[...]))
        qn = _l2norm_heads(qkv[:, :GROUP_W])
        kn = _l2norm_heads(qkv[:, GROUP_W:2 * GROUP_W])
        v = qkv[:, 2 * GROUP_W:]
        if rope:
            cos, sin = cos_ref[...], sin_ref[...]
            qn = qn * cos + _swap16(qn) * sin
            kn = kn * cos + _swap16(kn) * sin
        qn = qn * (HEAD_DIM ** -0.5)
        sm = sm_ref[...]
        beta = _sigmoid(sm)
        keep = in_chunk & ((rowi >= coli) if d == 0 else (rowi <= coli))
        strict = in_chunk & ((rowi > coli) if d == 0 else (rowi < coli))
        gc = _dot_tri(keep, _softplus(sm + dtb_ref[...]) * a_neg)
        gc_t = gc.T
        edge = ck - 1 if d == 0 else 0
        last = jnp.concatenate([jnp.broadcast_to(gc[c * ck + edge:c * ck + edge + 1, :], (ck, LANES))
                                for c in range(nck)], axis=0)
        e_gc = jnp.exp(gc)
        e_end = jnp.exp(last - gc)
        e_last.append(jnp.exp(last))
        for h in range(N_HEADS):
            sl = slice(h * HEAD_DIM, (h + 1) * HEAD_DIM)
            lg = SM_DECAY + N_HEADS * d + h
            lb = SM_BETA + N_HEADS * d + h
            decay = jnp.exp(jnp.where(keep, gc[:, lg:lg + 1] - gc_t[lg:lg + 1, :], NEG_INF))
            qh, kh, bcol = qn[:, sl], kn[:, sl], beta[:, lb:lb + 1]
            kb = kh * bcol
            a_list.append(jnp.where(strict, _dot_nt(kb, kh) * decay, 0.0))
            rhs_list.append(jnp.concatenate([v[:, sl] * bcol, kb * e_gc[:, lg:lg + 1]], axis=1))
            qkm.append(_dot_nt(qh, kh) * decay)
            qg.append(qh * e_gc[:, lg:lg + 1])
            kd.append(kh * e_end[:, lg:lg + 1])
    sols = _solve_unit_tri(a_list, rhs_list, rowi, coli, ck)

    chains = [(d, h) for d in range(2) for h in range(N_HEADS)]
    states = [state_s[d, h] for d, h in chains]
    v_new = [[None] * nck for _ in chains]
    o_st = [[None] * nck for _ in chains]
    for step in range(nck):
        rows = [slice((step if d == 0 else nck - 1 - step) * ck, (step if d == 0 else nck - 1 - step) * ck + ck)
                for d, _ in chains]
        ms = [_dot(jnp.concatenate([sols[n][r, HEAD_DIM:], qg[n][r]], axis=0), states[n])
              for n, r in enumerate(rows)]
        for n, (d, _) in enumerate(chains):
            c = step if d == 0 else nck - 1 - step
            v_new[n][c] = sols[n][rows[n], :HEAD_DIM] - ms[n][:ck]
            o_st[n][c] = ms[n][ck:]
        ups = [_dot_tn(kd[n][r], v_new[n][step if chains[n][0] == 0 else nck - 1 - step])
               for n, r in enumerate(rows)]
        for n, (d, h) in enumerate(chains):
            lg = SM_DECAY + N_HEADS * d + h
            states[n] = states[n] * e_last[d][rows[n].start:rows[n].start + 1, lg:lg + 1] + ups[n]
    outs = [jnp.concatenate(o_st[n], axis=0) + _dot(qkm[n], jnp.concatenate(v_new[n], axis=0))
            for n in range(len(chains))]
    of_ref[...] = jnp.concatenate(outs[:N_HEADS], axis=1)
    ob_ref[...] = jnp.concatenate(outs[N_HEADS:], axis=1)
    for n, (d, h) in enumerate(chains):
        state_s[d, h] = states[n]

    @pl.when(i == pl.num_programs(1) - 1)
    def _():
        sfin_ref[0] = state_s[...]


def gdn_params(conv_w, a_log, dt_bias):
    return _pad_rows(conv_w), _small_vec(a_log, SM_DECAY), _small_vec(dt_bias, SM_DECAY)


def gdn_mixer(qkv, sm, s0, bsz, cw, alog, dtb, rope=None):
    t, c = qkv.shape
    s = t // bsz
    q = min(GDN_TILE, s)
    nc = s // q
    halo = _halo(qkv, bsz, q)
    xf, xb = _chunk_specs(nc, q, c)
    hf, hb = _halo_specs(nc, c)
    sf, sb = _chunk_specs(nc, q, LANES)
    of, ob = _chunk_specs(nc, q, GROUP_W)
    st = pl.BlockSpec((1,) + s0.shape[1:], lambda b, i: (b, 0, 0, 0, 0))
    ins = [qkv, qkv, halo, halo, sm, sm]
    specs = [xf, xb, hf, hb, sf, sb]
    if rope is not None:
        tf = pl.BlockSpec((q, GROUP_W), lambda b, i: (i, 0))
        tb = pl.BlockSpec((q, GROUP_W), lambda b, i: (nc - 1 - i, 0))
        ins += [rope[0], rope[0], rope[1], rope[1]]
        specs += [tf, tb, tf, tb]
    ins += [s0, cw, alog, dtb]
    specs += [st, _full(cw.shape), _full(alog.shape), _full(dtb.shape)]
    o_shape = jax.ShapeDtypeStruct((t, GROUP_W), F32)
    return pl.pallas_call(
        functools.partial(_gdn_kernel, rope=rope is not None),
        grid=(bsz, nc),
        in_specs=specs,
        out_specs=[of, ob, st],
        out_shape=[o_shape, o_shape, jax.ShapeDtypeStruct(s0.shape, F32)],
        scratch_shapes=[pltpu.VMEM(s0.shape[1:], F32)],
        compiler_params=_cp("parallel", "arbitrary"),
        name="gdn_mixer",
    )(*ins)


def _split_hi_lo(a):
    hi = _mx(a)
    return hi, _mx(a - hi.astype(F32))


def _outproj_kernel(x_ref, ahf_ref, ahb_ref, ag_ref, bo_ref, cyf_ref, cyb_ref, cxc_ref, cz_ref,
                    dof_ref, dob_ref, dz_ref, wout_ref, gpost_ref, ga1_ref, gpre_ref, sc2_ref, sh2_ref,
                    dskip_ref, cnorm_ref, dnorm_ref, rhi_ref, rlo_ref, xo_ref, h2_ref, lg_ref):
    m_a = (ahf_ref[...] + ahb_ref[...]) * _gelu_tanh(ag_ref[...])
    y_c = (cyf_ref[...] + cyb_ref[...] + cxc_ref[...] * dskip_ref[...]) * _silu(cz_ref[...])
    m_c = _rms(y_c, cnorm_ref[...])
    o_d = dof_ref[...] + dob_ref[...]
    heads = []
    for h in range(N_HEADS):
        oh = o_d[:, h * HEAD_DIM:(h + 1) * HEAD_DIM]
        heads.append(oh * lax.rsqrt(jnp.mean(oh * oh, axis=-1, keepdims=True) + EPS))
    m_d = jnp.concatenate(heads, axis=1) * dnorm_ref[...] * _silu(dz_ref[...])
    mix = jnp.concatenate([_mx(m_a), _mx(bo_ref[...]), _mx(m_c), _mx(m_d)], axis=1)
    ml = jnp.dot(mix, wout_ref[...], preferred_element_type=F32)
    x_new = x_ref[...] + ga1_ref[0] * _rms(ml, gpost_ref[...])
    xo_ref[...] = x_new
    h2 = _rms(x_new, gpre_ref[...]) * (1.0 + sc2_ref[0]) + sh2_ref[0]
    hi, lo = _split_hi_lo(h2)
    h2_ref[...] = hi
    rhi = rhi_ref[...]
    lg_ref[...] = (jnp.dot(hi, rhi, preferred_element_type=F32) + jnp.dot(lo, rhi, preferred_element_type=F32)
                   + jnp.dot(hi, rlo_ref[...], preferred_element_type=F32))


def out_projection(x, mixers, w_out, gpost, ga1, gpre, sc2, sh2, dskip, cnorm, dnorm, router_w, tiles_per_group):
    t, d = x.shape
    tm = min(TOKEN_TILE, t)
    vec = lambda i: (i // tiles_per_group, 0, 0)
    row = lambda w: pl.BlockSpec((tm, w), lambda i: (i, 0))
    ne = LANES
    rhi, rlo = _split_hi_lo(jnp.pad(router_w.astype(F32), ((0, 0), (0, ne - router_w.shape[1]))))
    return pl.pallas_call(
        _outproj_kernel,
        grid=(t // tm,),
        in_specs=[row(d)] + [row(GROUP_W)] * 11
                 + [_full(w_out.shape), _full((1, d)), pl.BlockSpec((1, 1, d), vec), _full((1, d)),
                    pl.BlockSpec((1, 1, d), vec), pl.BlockSpec((1, 1, d), vec),
                    _full((1, GROUP_W)), _full((1, GROUP_W)), _full((1, GROUP_W)), _full(rhi.shape), _full(rlo.shape)],
        out_specs=[row(d), row(d), row(ne)],
        out_shape=[jax.ShapeDtypeStruct((t, d), F32), jax.ShapeDtypeStruct((t, d), MXU_DTYPE),
                   jax.ShapeDtypeStruct((t, ne), F32)],
        compiler_params=_cp("parallel"),
        name="out_projection",
    )(x, *mixers, w_out, gpost, ga1, gpre, sc2, sh2, dskip, cnorm, dnorm, rhi, rlo)


def _rank_before(vals, idx, count, stride):
    rank = jnp.zeros(vals.shape, jnp.int32)
    for j in range(count):
        other = vals[j * stride:j * stride + 1, :]
        ahead = (other > vals) | ((other == vals) & (idx > j))
        rank = rank + ahead.astype(jnp.int32)
    return rank


def _xor_partner(x, row, s):
    n = x.shape[0]
    return jnp.where((row & s) == 0, pltpu.roll(x, n - s, 0), pltpu.roll(x, s, 0))


def _router_kernel(lg_ref, rb_ref, gate_ref):
    ne = N_EXPERTS
    gsz = ne // N_EXPERT_GROUPS
    scores = _sigmoid(lg_ref[...].T[:ne, :])
    tm = scores.shape[1]
    biased = scores + rb_ref[...]
    row = lax.broadcasted_iota(jnp.int32, (ne, tm), 0)
    m1, m2 = biased, jnp.full((ne, tm), -jnp.inf, F32)
    s = 1
    while s < gsz:
        o1, o2 = _xor_partner(m1, row, s), _xor_partner(m2, row, s)
        m2 = jnp.maximum(jnp.minimum(m1, o1), jnp.maximum(m2, o2))
        m1 = jnp.maximum(m1, o1)
        s *= 2
    gidx = row >> (gsz.bit_length() - 1)
    group_ok = _rank_before(m1 + m2, gidx, N_EXPERT_GROUPS, gsz) < TOPK_GROUPS
    choice = jnp.where(group_ok, biased, -jnp.inf)
    picked = _rank_before(choice, row, ne, 1) < TOP_K
    gate = jnp.where(picked, scores, 0.0)
    gate = gate / jnp.sum(gate, axis=0, keepdims=True) * ROUTED_SCALE
    gate_ref[...] = jnp.concatenate([gate, jnp.zeros((LANES - ne, tm), F32)], axis=0).T


def router_gates(logits, router_b):
    t, w = logits.shape
    tm = min(TOKEN_TILE, t)
    return pl.pallas_call(
        _router_kernel,
        grid=(t // tm,),
        in_specs=[pl.BlockSpec((tm, w), lambda i: (i, 0)), _full((N_EXPERTS, 1))],
        out_specs=pl.BlockSpec((tm, w), lambda i: (i, 0)),
        out_shape=jax.ShapeDtypeStruct((t, w), F32),
        compiler_params=_cp("parallel"),
        name="router_gates",
    )(logits, router_b.reshape(N_EXPERTS, 1).astype(F32))


def _moe_kernel(h_ref, gate_ref, x_ref, wg_ref, wu_ref, wd_ref, sg_ref, su_ref, sd_ref, gpost_ref, ga2_ref,
                o_ref, acc_s):
    e = pl.program_id(1)
    h = h_ref[...]

    @pl.when(e == 0)
    def _():
        hs = _silu(jnp.dot(h, sg_ref[...], preferred_element_type=F32)) * jnp.dot(h, su_ref[...], preferred_element_type=F32)
        acc_s[...] = jnp.dot(_mx(hs), sd_ref[...], preferred_element_type=F32)

    gates = gate_ref[...]
    lane = lax.broadcasted_iota(jnp.int32, gates.shape, 1)
    hid = []
    for j in range(MOE_EB):
        gcol = jnp.sum(jnp.where(lane == e * MOE_EB + j, gates, 0.0), axis=1, keepdims=True)
        g = jnp.dot(h, wg_ref[j], preferred_element_type=F32)
        u = jnp.dot(h, wu_ref[j], preferred_element_type=F32)
        hid.append(_mx(_silu(g) * u * gcol))
    wd = wd_ref[...].reshape(MOE_EB * D_EXPERT, -1)
    acc_s[...] += jnp.dot(jnp.concatenate(hid, axis=1), wd, preferred_element_type=F32)

    @pl.when(e == pl.num_programs(1) - 1)
    def _():
        o_ref[...] = x_ref[...] + ga2_ref[0] * _rms(acc_s[...], gpost_ref[...])


def moe_ffn(h, gates, x, wg, wu, wd, sg, su, sd, gpost, ga2, tiles_per_group):
    t, d = x.shape
    tm = min(MOE_TILE, t)
    ne, _, f = wg.shape
    row = lambda w: pl.BlockSpec((tm, w), lambda i, e: (i, 0))
    return pl.pallas_call(
        _moe_kernel,
        grid=(t // tm, ne // MOE_EB),
        in_specs=[row(d), row(gates.shape[1]), row(d),
                  pl.BlockSpec((MOE_EB, d, f), lambda i, e: (e, 0, 0)),
                  pl.BlockSpec((MOE_EB, d, f), lambda i, e: (e, 0, 0)),
                  pl.BlockSpec((MOE_EB, f, d), lambda i, e: (e, 0, 0)),
                  _full(sg.shape), _full(su.shape), _full(sd.shape), _full((1, d)),
                  pl.BlockSpec((1, 1, d), lambda i, e: (i // tiles_per_group, 0, 0))],
        out_specs=row(d),
        out_shape=jax.ShapeDtypeStruct((t, d), F32),
        scratch_shapes=[pltpu.VMEM((tm, d), F32)],
        compiler_params=_cp("parallel", "arbitrary"),
        name="moe_ffn",
    )(h, gates, x, wg, wu, wd, sg, su, sd, gpost, ga2)


def _reorder_w_in(w_in):
    c = np.cumsum((0,) + (GROUP_W, GROUP_W, GROUP_W, GROUP_W, GROUP_W, GROUP_W, 2 * SSD_STATE, 2 * SSD_STATE,
                          GROUP_W, 2 * N_HEADS, GROUP_W, GROUP_W, GROUP_W, GROUP_W, 2 * N_HEADS, 2 * N_HEADS))
    seg = lambda a, b: w_in[:, c[a]:c[b]]
    small = jnp.concatenate([seg(9, 10), seg(14, 15), seg(15, 16),
                             jnp.zeros((w_in.shape[0], LANES - 6 * N_HEADS), w_in.dtype)], axis=1)
    return jnp.concatenate([seg(0, 5), seg(5, 8), seg(8, 9), seg(10, 13), seg(13, 14), small], axis=1)


def kernel(x, c, ctx, c_ctx, w_mod, b_mod, g_pre_mix, g_post_mix, g_pre_ffn, g_post_ffn, w_in, w_out, lru_conv_w, lru_conv_b, lru_wa, lru_ba, lru_wx, lru_bx, lru_lambda, na_bias, ssd_conv_w, ssd_conv_b, ssd_a_log, ssd_dt_bias, ssd_d, ssd_norm, gdn_conv_w, gdn_a_log, gdn_dt_bias, gdn_norm, router_w, router_b, we_gate, we_up, we_down, ws_gate, ws_up, ws_down):
    bsz, seq, d = x.shape
    n_ctx = ctx.shape[1]
    depth = w_mod.shape[0]
    lat_tpg = seq // min(TOKEN_TILE, seq)
    lat_mpg = seq // min(MOE_TILE, seq)
    ctx_tpg = max(bsz * n_ctx // TOKEN_TILE, 1)
    ctx_mpg = max(bsz * n_ctx // MOE_TILE, 1)

    cond = _pad_rows(jnp.concatenate([c, c_ctx[None, :]], axis=0))
    mod = modulation(cond, w_mod, b_mod).reshape(depth, SUBLANES, N_MOD, d)
    rope = rope_tables(seq)
    row = lambda v: v[None, :].astype(F32)

    xl = x.reshape(bsz * seq, d)
    xc = ctx.reshape(bsz * n_ctx, d)
    for l in range(depth):
        last = l == depth - 1
        m_lat = [mod[l, :bsz, k][:, None, :] for k in range(N_MOD)]
        m_ctx = [mod[l, bsz:bsz + 1, k][:, None, :] for k in range(N_MOD)]
        w_in_l = _reorder_w_in(w_in[l]).astype(MXU_DTYPE)
        pc = in_projection(xc, row(g_pre_mix[l]), m_ctx[1], m_ctx[0], w_in_l, ctx_tpg)
        pl_ = in_projection(xl, row(g_pre_mix[l]), m_lat[1], m_lat[0], w_in_l, lat_tpg)

        lru_p = lru_params(lru_conv_w[l], lru_conv_b[l], lru_wa[l], lru_ba[l], lru_wx[l], lru_bx[l], lru_lambda[l])
        a_cf, a_cb, a_st = lru_mixer(pc[P_AX], jnp.zeros((bsz, SUBLANES, GROUP_W), F32), bsz, *lru_p)
        a_lf, a_lb, _ = lru_mixer(pl_[P_AX], a_st, bsz, *lru_p)

        kc = pc[P_BK].reshape(bsz, n_ctx, GROUP_W)
        vc = pc[P_BV].reshape(bsz, n_ctx, GROUP_W)
        b_c = ctx_attention(pc[P_BQ].reshape(bsz, n_ctx, GROUP_W), kc, vc).reshape(bsz * n_ctx, GROUP_W)
        b_l = na_mixer(pl_[P_BQ], pl_[P_BK], pl_[P_BV], kc, vc, na_bias_slabs(na_bias[l]), bsz)

        ssd_p = ssd_params(ssd_conv_w[l], ssd_conv_b[l], ssd_a_log[l], ssd_dt_bias[l])
        c_cf, c_cb, c_cx, c_st = ssd_mixer(pc[P_CX], pc[P_SM], jnp.zeros((bsz, 2, N_HEADS, SSD_STATE, HEAD_DIM), F32),
                                           bsz, *ssd_p)
        c_lf, c_lb, c_lx, _ = ssd_mixer(pl_[P_CX], pl_[P_SM], c_st, bsz, *ssd_p)

        gdn_p = gdn_params(gdn_conv_w[l], gdn_a_log[l], gdn_dt_bias[l])
        d_cf, d_cb, d_st = gdn_mixer(pc[P_DX], pc[P_SM], jnp.zeros((bsz, 2, N_HEADS, HEAD_DIM, HEAD_DIM), F32),
                                     bsz, *gdn_p)
        d_lf, d_lb, _ = gdn_mixer(pl_[P_DX], pl_[P_SM], d_st, bsz, *gdn_p, rope=rope)

        epi = (w_out[l].astype(MXU_DTYPE), row(g_post_mix[l]))
        epi_tail = (row(jnp.repeat(ssd_d[l], HEAD_DIM)), row(ssd_norm[l]), row(jnp.tile(gdn_norm[l], N_HEADS)), router_w[l])
        moe_w = (we_gate[l].astype(MXU_DTYPE), we_up[l].astype(MXU_DTYPE), we_down[l].astype(MXU_DTYPE),
                 ws_gate[l].astype(MXU_DTYPE), ws_up[l].astype(MXU_DTYPE), ws_down[l].astype(MXU_DTYPE),
                 row(g_post_ffn[l]))

        mix_l = (a_lf, a_lb, pl_[P_AG], b_l, c_lf, c_lb, c_lx, pl_[P_CZ], d_lf, d_lb, pl_[P_DZ])
        xl, h2, lg = out_projection(xl, mix_l, *epi, m_lat[2], row(g_pre_ffn[l]), m_lat[4], m_lat[3], *epi_tail, lat_tpg)
        xl = moe_ffn(h2, router_gates(lg, router_b[l]), xl, *moe_w, m_lat[5], lat_mpg)
        if not last:
            mix_c = (a_cf, a_cb, pc[P_AG], b_c, c_cf, c_cb, c_cx, pc[P_CZ], d_cf, d_cb, pc[P_DZ])
            xc, h2, lg = out_projection(xc, mix_c, *epi, m_ctx[2], row(g_pre_ffn[l]), m_ctx[4], m_ctx[3], *epi_tail, ctx_tpg)
            xc = moe_ffn(h2, router_gates(lg, router_b[l]), xc, *moe_w, m_ctx[5], ctx_mpg)
    return xl.reshape(bsz, seq, d)
```

```python
import functools
import math

import jax
import jax.numpy as jnp
import numpy as np
from jax import lax
from jax.experimental import pallas as pl
from jax.experimental.pallas import tpu as pltpu

F32 = jnp.float32
MXU_DTYPE = jnp.bfloat16
HI = lax.Precision.HIGHEST

D_MODEL = 1024
GRID_W = 64
GROUP_W = 256
HEAD_DIM = 64
N_HEADS = 4
EPS = 1e-6
NEG_INF = -1e30
N_MOD = 6
LRU_C = 8.0
NA_WIN_ROWS = 8
NA_WIN_COLS = 16
SSD_STATE = 128
SSD_GROUPS = 2
ROPE_BASE = 10000.0
ROPE_AXIS_DIM = HEAD_DIM // 2
N_EXPERTS = 64
N_EXPERT_GROUPS = 8
TOPK_GROUPS = 4
TOP_K = 8
D_EXPERT = 256
ROUTED_SCALE = 2.5

LANES = 128
SUBLANES = 8
VMEM_LIMIT = 56 * 1024 * 1024

TOKEN_TILE = 512
LRU_CHUNK = 256
SSD_CHUNK = 128
GDN_CHUNK = 64
GDN_TILE = 256
GDN_SUB = 128
GDN_BASE = 16
MOE_TILE = 1024
MOE_EB = 4

P_WIDTHS = (256, 256, 256, 256, 256, 768, 256, 768, 256, 128)
(P_AX, P_AG, P_BQ, P_BK, P_BV, P_CX, P_CZ, P_DX, P_DZ, P_SM) = range(10)
SM_DT, SM_BETA, SM_DECAY = 0, 8, 16


def _cp(*sem):
    return pltpu.CompilerParams(dimension_semantics=sem, vmem_limit_bytes=VMEM_LIMIT)


def _mx(x):
    return x.astype(MXU_DTYPE)


def _dot(a, b):
    return jnp.dot(_mx(a), _mx(b), preferred_element_type=F32)


def _dot_nt(a, b):
    return lax.dot_general(_mx(a), _mx(b), (((1,), (1,)), ((), ())), preferred_element_type=F32)


def _dot_tn(a, b):
    return lax.dot_general(_mx(a), _mx(b), (((0,), (0,)), ((), ())), preferred_element_type=F32)


def _dot_hi(a, b):
    return jnp.dot(a, b, preferred_element_type=F32, precision=HI)


def _sigmoid(x):
    return 1.0 / (1.0 + jnp.exp(-x))


def _silu(x):
    return x * _sigmoid(x)


def _softplus(x):
    return jnp.maximum(x, 0.0) + jnp.log1p(jnp.exp(-jnp.abs(x)))


def _gelu_tanh(x):
    return 0.5 * x * (1.0 + jnp.tanh(math.sqrt(2.0 / math.pi) * (x + 0.044715 * (x * x * x))))


def _rms(x, g):
    return x * lax.rsqrt(jnp.mean(x * x, axis=-1, keepdims=True) + EPS) * g


def _full(shape):
    n = len(shape)
    return pl.BlockSpec(shape, lambda *_: (0,) * n)


MOD_COLS = 1536


def _mod_kernel(c_ref, w_ref, b_ref, o_ref):
    o_ref[0] = _dot_hi(_silu(c_ref[...]), w_ref[0]) + b_ref[0]


def modulation(cond, w_mod, b_mod):
    depth, d, n = w_mod.shape
    return pl.pallas_call(
        _mod_kernel,
        grid=(depth, n // MOD_COLS),
        in_specs=[pl.BlockSpec((SUBLANES, d), lambda l, j: (0, 0)),
                  pl.BlockSpec((1, d, MOD_COLS), lambda l, j: (l, 0, j)),
                  pl.BlockSpec((1, 1, MOD_COLS), lambda l, j: (l, 0, j))],
        out_specs=pl.BlockSpec((1, SUBLANES, MOD_COLS), lambda l, j: (l, 0, j)),
        out_shape=jax.ShapeDtypeStruct((depth, SUBLANES, n), F32),
        compiler_params=_cp("parallel", "parallel"),
        name="modulation",
    )(cond, w_mod, b_mod.reshape(depth, 1, n))


def _inproj_kernel(x_ref, g_ref, sc_ref, sh_ref, w_ref, *o_refs):
    h = _rms(x_ref[...], g_ref[...]) * (1.0 + sc_ref[0]) + sh_ref[0]
    p = _dot(h, w_ref[...])
    off = 0
    for o_ref, w in zip(o_refs, P_WIDTHS):
        o_ref[...] = p[:, off:off + w]
        off += w


def in_projection(x, g, sc, sh, w, tiles_per_group):
    t, d = x.shape
    tm = min(TOKEN_TILE, t)
    vec = lambda i: (i // tiles_per_group, 0, 0)
    return pl.pallas_call(
        _inproj_kernel,
        grid=(t // tm,),
        in_specs=[pl.BlockSpec((tm, d), lambda i: (i, 0)),
                  _full((1, d)),
                  pl.BlockSpec((1, 1, d), vec),
                  pl.BlockSpec((1, 1, d), vec),
                  _full(w.shape)],
        out_specs=[pl.BlockSpec((tm, wd), lambda i: (i, 0)) for wd in P_WIDTHS],
        out_shape=[jax.ShapeDtypeStruct((t, wd), F32) for wd in P_WIDTHS],
        compiler_params=_cp("parallel"),
        name="in_projection",
    )(x, g, sc, sh, w)


def _halo(p, bsz, q):
    c = p.shape[-1]
    pr = p.reshape(bsz, -1, q, c)
    nc = pr.shape[1]
    prev = jnp.concatenate([jnp.zeros((bsz, 1, 2, c), p.dtype), pr[:, :-1, q - 2:, :]], axis=1)
    nxt = jnp.concatenate([pr[:, 1:, :1, :], jnp.zeros((bsz, 1, 1, c), p.dtype)], axis=1)
    pad = jnp.zeros((bsz, nc, SUBLANES - 3, c), p.dtype)
    return jnp.concatenate([prev, nxt, pad], axis=2).reshape(bsz * nc, SUBLANES, c)


def _dwconv(x, halo, w, b=None):
    q = x.shape[0]
    row = lax.broadcasted_iota(jnp.int32, x.shape, 0)
    xm2 = jnp.where(row == 0, halo[0:1], jnp.where(row == 1, halo[1:2], pltpu.roll(x, 2, 0)))
    xm1 = jnp.where(row == 0, halo[1:2], pltpu.roll(x, 1, 0))
    xp1 = jnp.where(row == q - 1, halo[2:3], pltpu.roll(x, q - 1, 0))
    y = w[0:1] * xm2 + w[1:2] * xm1 + w[2:3] * x + w[3:4] * xp1
    return y if b is None else y + b


def _pad_rows(a, rows=SUBLANES):
    return jnp.concatenate([a, jnp.zeros((rows - a.shape[0],) + a.shape[1:], a.dtype)], axis=0)


def _chunk_specs(nc, q, c):
    fwd = pl.BlockSpec((q, c), lambda b, i: (b * nc + i, 0))
    bwd = pl.BlockSpec((q, c), lambda b, i: (b * nc + nc - 1 - i, 0))
    return fwd, bwd


def _halo_specs(nc, c):
    fwd = pl.BlockSpec((1, SUBLANES, c), lambda b, i: (b * nc + i, 0, 0))
    bwd = pl.BlockSpec((1, SUBLANES, c), lambda b, i: (b * nc + nc - 1 - i, 0, 0))
    return fwd, bwd


def _lru_kernel(xf_ref, xb_ref, hf_ref, hb_ref, h0_ref, cw_ref, cb_ref, wg_ref, bg_ref, lam_ref,
                yf_ref, yb_ref, hfin_ref, af_s, bf_s, ab_s, bb_s, carry_s):
    i = pl.program_id(1)
    q = xf_ref.shape[0]

    @pl.when(i == 0)
    def _():
        carry_s[...] = h0_ref[0]

    def coeffs(x_ref, halo_ref, d, a_s, b_s):
        u = _dwconv(x_ref[...], halo_ref[0], cw_ref[...], cb_ref[...])
        g = _dot(u, wg_ref[:, 2 * GROUP_W * d:2 * GROUP_W * (d + 1)]) + bg_ref[:, 2 * GROUP_W * d:2 * GROUP_W * (d + 1)]
        r = _sigmoid(g[:, :GROUP_W])
        gate_in = _sigmoid(g[:, GROUP_W:])
        log_a = -LRU_C * r * _softplus(-lam_ref[d:d + 1, :])
        a_s[...] = jnp.exp(log_a)
        b_s[...] = jnp.sqrt(1.0 - jnp.exp(2.0 * log_a)) * (gate_in * u)

    coeffs(xf_ref, hf_ref, 0, af_s, bf_s)
    coeffs(xb_ref, hb_ref, 1, ab_s, bb_s)

    ng = q // SUBLANES
    row = lax.broadcasted_iota(jnp.int32, (SUBLANES, GROUP_W), 0)

    def body(g, hs):
        h_f, h_b = hs
        i0 = pl.multiple_of(g * SUBLANES, SUBLANES)
        a = af_s[pl.ds(i0, SUBLANES), :]
        b = bf_s[pl.ds(i0, SUBLANES), :]
        for s in (1, 2, 4):
            m = row >= s
            b = jnp.where(m, a * pltpu.roll(b, s, 0) + b, b)
            a = jnp.where(m, a * pltpu.roll(a, s, 0), a)
        h = b + a * h_f
        yf_ref[pl.ds(i0, SUBLANES), :] = h
        h_f = h[SUBLANES - 1:SUBLANES, :]
        j0 = pl.multiple_of((ng - 1 - g) * SUBLANES, SUBLANES)
        a = ab_s[pl.ds(j0, SUBLANES), :]
        b = bb_s[pl.ds(j0, SUBLANES), :]
        for s in (1, 2, 4):
            m = row < SUBLANES - s
            b = jnp.where(m, a * pltpu.roll(b, SUBLANES - s, 0) + b, b)
            a = jnp.where(m, a * pltpu.roll(a, SUBLANES - s, 0), a)
        h = b + a * h_b
        yb_ref[pl.ds(j0, SUBLANES), :] = h
        return h_f, h[0:1, :]

    h_f, h_b = lax.fori_loop(0, ng, body, (carry_s[0:1, :], carry_s[1:2, :]))
    carry_s[0:1, :] = h_f
    carry_s[1:2, :] = h_b

    @pl.when(i == pl.num_programs(1) - 1)
    def _():
        hfin_ref[0] = carry_s[...]


def _block_diag(w):
    h, a, b = w.shape
    return jnp.einsum('hij,hg->higj', w, jnp.eye(h, dtype=w.dtype)).reshape(h * a, h * b)


def lru_params(conv_w, conv_b, wa, ba, wx, bx, lam):
    wg = jnp.concatenate([_block_diag(wa[0]), _block_diag(wx[0]), _block_diag(wa[1]), _block_diag(wx[1])], axis=1)
    bg = jnp.concatenate([ba[0], bx[0], ba[1], bx[1]])[None, :]
    return _pad_rows(conv_w), conv_b[None, :], wg.astype(MXU_DTYPE), bg, _pad_rows(lam)


def lru_mixer(x, h0, bsz, cw, cb, wg, bg, lam):
    t, c = x.shape
    s = t // bsz
    q = min(LRU_CHUNK, s)
    nc = s // q
    halo = _halo(x, bsz, q)
    xf, xb = _chunk_specs(nc, q, c)
    hf, hb = _halo_specs(nc, c)
    st = pl.BlockSpec((1, SUBLANES, c), lambda b, i: (b, 0, 0))
    return pl.pallas_call(
        _lru_kernel,
        grid=(bsz, nc),
        in_specs=[xf, xb, hf, hb, st, _full(cw.shape), _full(cb.shape), _full(wg.shape), _full(bg.shape),
                  _full(lam.shape)],
        out_specs=[xf, xb, st],
        out_shape=[jax.ShapeDtypeStruct((t, c), F32), jax.ShapeDtypeStruct((t, c), F32),
                   jax.ShapeDtypeStruct((bsz, SUBLANES, c), F32)],
        scratch_shapes=[pltpu.VMEM((q, c), F32)] * 4 + [pltpu.VMEM((SUBLANES, c), F32)],
        compiler_params=_cp("parallel", "arbitrary"),
        name="lru_mixer",
    )(x, x, halo, halo, h0, cw, cb, wg, bg, lam)


NA_KEYS = NA_WIN_ROWS * GRID_W
NA_ROW_BLOCK = 4


def na_bias_slabs(table):
    qc = np.arange(GRID_W)[:, None]
    kc = np.arange(GRID_W)[None, :]
    win0 = np.clip(qc - NA_WIN_COLS // 2, 0, GRID_W - NA_WIN_COLS)
    ok = (kc >= win0) & (kc < win0 + NA_WIN_COLS)
    dc = np.clip(kc - qc + NA_WIN_COLS - 1, 0, 2 * NA_WIN_COLS - 2)
    dr = np.arange(NA_WIN_ROWS)[:, None] + np.arange(NA_WIN_ROWS)[None, :]
    b = table.astype(F32)[:, dr][:, :, :, dc]
    b = jnp.where(ok[None, None, None], b, NEG_INF)
    h = table.shape[0]
    return b.transpose(0, 1, 3, 2, 4).reshape(h, NA_WIN_ROWS, GRID_W, NA_KEYS)


def _na_span_start(j, rows):
    return jnp.clip(j * NA_ROW_BLOCK - NA_WIN_ROWS // 2, 0, rows - (NA_ROW_BLOCK + NA_WIN_ROWS - 1))


def _na_kernel(q_ref, kw_ref, vw_ref, kc_ref, vc_ref, slab_ref, o_ref, *, rows):
    j = pl.program_id(1)
    ustart = _na_span_start(j, rows)
    q = q_ref[...] * (HEAD_DIM ** -0.5)
    kc, vc = kc_ref[0], vc_ref[0]
    heads = [slice(h * HEAD_DIM, (h + 1) * HEAD_DIM) for h in range(N_HEADS)]
    qrows = [slice(i * GRID_W, (i + 1) * GRID_W) for i in range(NA_ROW_BLOCK)]
    kws, vws, offs = [], [], []
    for i in range(NA_ROW_BLOCK):
        r = j * NA_ROW_BLOCK + i
        r0 = jnp.clip(r - NA_WIN_ROWS // 2, 0, rows - NA_WIN_ROWS)
        start = pl.multiple_of((r0 - ustart) * GRID_W, GRID_W)
        kws.append(kw_ref[pl.ds(start, NA_KEYS), :])
        vws.append(vw_ref[pl.ds(start, NA_KEYS), :])
        offs.append(r0 - r + NA_WIN_ROWS - 1)
    s_ctx = [_dot_nt(q[:, sl], kc[:, sl]) for sl in heads]
    s_loc = [[_dot_nt(q[qr, sl], kws[i][:, sl]) + slab_ref[h, offs[i]] for h, sl in enumerate(heads)]
             for i, qr in enumerate(qrows)]
    m = [[jnp.maximum(jnp.max(s_loc[i][h], axis=-1, keepdims=True), jnp.max(s_ctx[h][qr], axis=-1, keepdims=True))
          for h in range(N_HEADS)] for i, qr in enumerate(qrows)]
    p_loc = [[jnp.exp(s_loc[i][h] - m[i][h]) for h in range(N_HEADS)] for i in range(NA_ROW_BLOCK)]
    p_ctx = [jnp.exp(s_ctx[h] - jnp.concatenate([m[i][h] for i in range(NA_ROW_BLOCK)], axis=0))
             for h in range(N_HEADS)]
    o_ctx = [_dot(p_ctx[h], vc[:, sl]) for h, sl in enumerate(heads)]
    rows_out = []
    for i, qr in enumerate(qrows):
        outs = []
        for h, sl in enumerate(heads):
            den = jnp.sum(p_loc[i][h], axis=-1, keepdims=True) + jnp.sum(p_ctx[h][qr], axis=-1, keepdims=True)
            outs.append((_dot(p_loc[i][h], vws[i][:, sl]) + o_ctx[h][qr]) / den)
        rows_out.append(jnp.concatenate(outs, axis=1))
    o_ref[...] = jnp.concatenate(rows_out, axis=0)


def na_mixer(q, k, v, kc, vc, slabs, bsz):
    t, c = q.shape
    s = t // bsz
    rows = s // GRID_W
    n_ctx = kc.shape[1]
    span = (NA_ROW_BLOCK + NA_WIN_ROWS - 1) * GRID_W

    def win(b, j):
        return ((b * rows + _na_span_start(j, rows)) * GRID_W, 0)

    wspec = pl.BlockSpec((pl.Element(span), pl.Element(c)), win)
    cspec = pl.BlockSpec((1, n_ctx, c), lambda b, j: (b, 0, 0))
    qspec = pl.BlockSpec((NA_ROW_BLOCK * GRID_W, c), lambda b, j: (b * (rows // NA_ROW_BLOCK) + j, 0))
    return pl.pallas_call(
        functools.partial(_na_kernel, rows=rows),
        grid=(bsz, rows // NA_ROW_BLOCK),
        in_specs=[qspec, wspec, wspec, cspec, cspec, _full(slabs.shape)],
        out_specs=qspec,
        out_shape=jax.ShapeDtypeStruct((t, c), F32),
        compiler_params=_cp("parallel", "arbitrary"),
        name="na_mixer",
    )(q, k, v, kc, vc, slabs)


def _ctx_attn_kernel(q_ref, k_ref, v_ref, o_ref):
    q = q_ref[0] * (HEAD_DIM ** -0.5)
    k, v = k_ref[0], v_ref[0]
    outs = []
    for h in range(N_HEADS):
        sl = slice(h * HEAD_DIM, (h + 1) * HEAD_DIM)
        s = _dot_nt(q[:, sl], k[:, sl])
        p = jnp.exp(s - jnp.max(s, axis=-1, keepdims=True))
        outs.append(_dot(p, v[:, sl]) / jnp.sum(p, axis=-1, keepdims=True))
    o_ref[0] = jnp.concatenate(outs, axis=1)


def ctx_attention(q, k, v):
    spec = pl.BlockSpec((1,) + q.shape[1:], lambda b: (b, 0, 0))
    return pl.pallas_call(
        _ctx_attn_kernel,
        grid=(q.shape[0],),
        in_specs=[spec, spec, spec],
        out_specs=spec,
        out_shape=jax.ShapeDtypeStruct(q.shape, F32),
        compiler_params=_cp("parallel"),
        name="ctx_attention",
    )(q, k, v)


def _small_vec(vals, off):
    v = jnp.zeros((LANES,), F32).at[off:off + 2 * N_HEADS].set(vals.reshape(-1).astype(F32))
    return v[None, :]


def _lane_mask(off):
    lane = lax.broadcasted_iota(jnp.int32, (1, LANES), 1)
    return (lane >= off) & (lane < off + 2 * N_HEADS)


def _tri_masks(q):
    rowi = lax.broadcasted_iota(jnp.int32, (q, q), 0)
    coli = lax.broadcasted_iota(jnp.int32, (q, q), 1)
    return rowi, coli


def _ssd_kernel(xf_ref, xb_ref, hf_ref, hb_ref, sf_ref, sb_ref, h0_ref, cw_ref, cb_ref, dtb_ref, alog_ref,
                yf_ref, yb_ref, xc_ref, hfin_ref, state_s):
    i = pl.program_id(1)
    q = xf_ref.shape[0]

    @pl.when(i == 0)
    def _():
        state_s[...] = h0_ref[0]

    rowi, coli = _tri_masks(q)
    a_neg = jnp.where(_lane_mask(SM_DT), -jnp.exp(alog_ref[...]), 0.0)

    def direction(x_ref, halo_ref, sm_ref, d, y_ref):
        xbc = _silu(_dwconv(x_ref[...], halo_ref[0], cw_ref[...], cb_ref[...]))
        if d == 0:
            xc_ref[...] = xbc[:, :GROUP_W]
        dt = _softplus(sm_ref[...] + dtb_ref[...])
        keep = (rowi >= coli) if d == 0 else (rowi <= coli)
        acum = _dot_hi(keep.astype(F32), dt * a_neg)
        acum_t = acum.T
        last = acum[q - 1:q, :] if d == 0 else acum[0:1, :]
        dec_end = jnp.exp(last - acum)
        e_acum = jnp.exp(acum)
        e_last = jnp.exp(last)
        ys = []
        for g in range(SSD_GROUPS):
            bg = xbc[:, GROUP_W + SSD_STATE * g:GROUP_W + SSD_STATE * (g + 1)]
            cg = xbc[:, GROUP_W + SSD_STATE * (SSD_GROUPS + g):GROUP_W + SSD_STATE * (SSD_GROUPS + g + 1)]
            cbt = _dot_nt(cg, bg)
            for hh in range(N_HEADS // SSD_GROUPS):
                h = g * (N_HEADS // SSD_GROUPS) + hh
                ln = SM_DT + N_HEADS * d + h
                lmat = jnp.exp(jnp.where(keep, acum[:, ln:ln + 1] - acum_t[ln:ln + 1, :], NEG_INF))
                xdt = xbc[:, h * HEAD_DIM:(h + 1) * HEAD_DIM] * dt[:, ln:ln + 1]
                st = state_s[d, h]
                ys.append(_dot(cbt * lmat, xdt) + _dot(cg * e_acum[:, ln:ln + 1], st))
                state_s[d, h] = st * e_last[:, ln:ln + 1] + _dot_tn(bg * dec_end[:, ln:ln + 1], xdt)
        y_ref[...] = jnp.concatenate(ys, axis=1)

    direction(xf_ref, hf_ref, sf_ref, 0, yf_ref)
    direction(xb_ref, hb_ref, sb_ref, 1, yb_ref)

    @pl.when(i == pl.num_programs(1) - 1)
    def _():
        hfin_ref[0] = state_s[...]


def ssd_params(conv_w, conv_b, a_log, dt_bias):
    return _pad_rows(conv_w), conv_b[None, :], _small_vec(dt_bias, SM_DT), _small_vec(a_log, SM_DT)


def ssd_mixer(xbc, sm, h0, bsz, cw, cb, dtb, alog):
    t, c = xbc.shape
    s = t // bsz
    q = min(SSD_CHUNK, s)
    nc = s // q
    halo = _halo(xbc, bsz, q)
    xf, xb = _chunk_specs(nc, q, c)
    hf, hb = _halo_specs(nc, c)
    sf, sb = _chunk_specs(nc, q, LANES)
    yf, yb = _chunk_specs(nc, q, GROUP_W)
    st = pl.BlockSpec((1,) + h0.shape[1:], lambda b, i: (b, 0, 0, 0, 0))
    y_shape = jax.ShapeDtypeStruct((t, GROUP_W), F32)
    return pl.pallas_call(
        _ssd_kernel,
        grid=(bsz, nc),
        in_specs=[xf, xb, hf, hb, sf, sb, st, _full(cw.shape), _full(cb.shape), _full(dtb.shape), _full(alog.shape)],
        out_specs=[yf, yb, yf, st],
        out_shape=[y_shape, y_shape, y_shape, jax.ShapeDtypeStruct(h0.shape, F32)],
        scratch_shapes=[pltpu.VMEM(h0.shape[1:], F32)],
        compiler_params=_cp("parallel", "arbitrary"),
        name="ssd_mixer",
    )(xbc, xbc, halo, halo, sm, sm, h0, cw, cb, dtb, alog)


def rope_tables(seq):
    t = jnp.arange(seq)
    row = (t // GRID_W).astype(F32)
    col = (t % GRID_W).astype(F32)
    inv = ROPE_BASE ** (-jnp.arange(0, ROPE_AXIS_DIM, 2, dtype=F32) / ROPE_AXIS_DIM)
    ar, ac = row[:, None] * inv, col[:, None] * inv
    cos = jnp.concatenate([jnp.cos(ar), jnp.cos(ar), jnp.cos(ac), jnp.cos(ac)], axis=1)
    sin = jnp.concatenate([-jnp.sin(ar), jnp.sin(ar), -jnp.sin(ac), jnp.sin(ac)], axis=1)
    return jnp.tile(cos, (1, N_HEADS)), jnp.tile(sin, (1, N_HEADS))


def _swap16(x):
    lane = lax.broadcasted_iota(jnp.int32, x.shape, 1)
    half = ROPE_AXIS_DIM // 2
    return jnp.where((lane & (ROPE_AXIS_DIM - 1)) < half,
                     pltpu.roll(x, x.shape[1] - half, 1), pltpu.roll(x, half, 1))


def _l2norm_heads(x):
    outs = []
    for h in range(N_HEADS):
        xh = x[:, h * HEAD_DIM:(h + 1) * HEAD_DIM]
        outs.append(xh * lax.rsqrt(jnp.sum(xh * xh, axis=-1, keepdims=True) + EPS))
    return jnp.concatenate(outs, axis=1)


def _dot_tri(mask, x):
    m = _mx(mask.astype(F32))
    x1 = _mx(x)
    r1 = x - x1.astype(F32)
    x2 = _mx(r1)
    x3 = _mx(r1 - x2.astype(F32))
    return (jnp.dot(m, x1, preferred_element_type=F32) + jnp.dot(m, x2, preferred_element_type=F32)
            + jnp.dot(m, x3, preferred_element_type=F32))


def _same_block(rowi, coli, n):
    sh = n.bit_length() - 1
    return (rowi >> sh) == (coli >> sh)


def _solve_unit_tri(a_list, rhs_list, rowi, coli, chunk):
    mm = lambda x, y: jnp.dot(x, y, preferred_element_type=F32)
    eye = (rowi == coli).astype(F32)
    in_base = _same_block(rowi, coli, GDN_BASE)
    base = [_mx(jnp.where(in_base, a, 0.0)) for a in a_list]
    ts = [jnp.where(in_base, eye - a, 0.0) for a in a_list]
    ps = [_mx(mm(b, b)) for b in base]
    ts = [t + mm(_mx(t), p) for t, p in zip(ts, ps)]
    n = 4
    while n < GDN_BASE:
        ps = [_mx(mm(p, p)) for p in ps]
        ts = [t + mm(_mx(t), p) for t, p in zip(ts, ps)]
        n *= 2
    n = GDN_BASE
    while 2 * n < chunk:
        inner = _same_block(rowi, coli, 2 * n) & jnp.logical_not(_same_block(rowi, coli, n))
        offs = [_mx(jnp.where(inner, a, 0.0)) for a in a_list]
        tb = [_mx(t) for t in ts]
        ms = [_mx(mm(t, off)) for t, off in zip(tb, offs)]
        ts = [t - mm(m, t_b) for t, m, t_b in zip(ts, ms, tb)]
        n *= 2
    outer = jnp.logical_not(_same_block(rowi, coli, n))
    offs = [_mx(jnp.where(outer, a, 0.0)) for a in a_list]
    tb = [_mx(t) for t in ts]
    ys = [mm(t, _mx(r)) for t, r in zip(tb, rhs_list)]
    zs = [_mx(mm(off, _mx(y))) for off, y in zip(offs, ys)]
    return [y - mm(t, z) for y, t, z in zip(ys, tb, zs)]


def _gdn_kernel(*refs, rope):
    if rope:
        (xf_ref, xb_ref, hf_ref, hb_ref, sf_ref, sb_ref, cf_ref, cb_ref, nf_ref, nb_ref,
         s0_ref, cw_ref, alog_ref, dtb_ref, of_ref, ob_ref, sfin_ref, state_s) = refs
    else:
        (xf_ref, xb_ref, hf_ref, hb_ref, sf_ref, sb_ref,
         s0_ref, cw_ref, alog_ref, dtb_ref, of_ref, ob_ref, sfin_ref, state_s) = refs
        cf_ref = cb_ref = nf_ref = nb_ref = None
    i = pl.program_id(1)
    tq = xf_ref.shape[0]
    ck = min(GDN_CHUNK, tq)
    nck = tq // ck

    @pl.when(i == 0)
    def _():
        state_s[...] = s0_ref[0]

    sub = min(GDN_SUB, tq)
    nsub = tq // sub
    rowt, colt = _tri_masks(tq)
    in_chunk_t = _same_block(rowt, colt, ck)
    rowi, coli = _tri_masks(sub)
    in_chunk = _same_block(rowi, coli, ck)
    a_neg = jnp.where(_lane_mask(SM_DECAY), -jnp.exp(alog_ref[...]), 0.0)

    a_list, rhs_list, qkm, qg, kd, e_last = [], [], [], [], [], []
    for d, (x_ref, halo_ref, sm_ref, cos_ref, sin_ref) in enumerate(
            ((xf_ref, hf_ref, sf_ref, cf_ref, nf_ref), (xb_ref, hb_ref, sb_ref, cb_ref, nb_ref))):
        qkv = _silu(_dwconv(x_ref[...], halo_ref[0], cw_ref[...]))
        qn = _l2norm_heads(qkv[:, :GROUP_W])
        kn = _l2norm_heads(qkv[:, GROUP_W:2 * GROUP_W])
        v = qkv[:, 2 * GROUP_W:]
        if rope:
            cos, sin = cos_ref[...], sin_ref[...]
            qn = qn * cos + _swap16(qn) * sin
            kn = kn * cos + _swap16(kn) * sin
        qn = qn * (HEAD_DIM ** -0.5)
        sm = sm_ref[...]
        beta = _sigmoid(sm)
        keep_t = in_chunk_t & ((rowt >= colt) if d == 0 else (rowt <= colt))
        keep = in_chunk & ((rowi >= coli) if d == 0 else (rowi <= coli))
        strict = in_chunk & ((rowi > coli) if d == 0 else (rowi < coli))
        gc = _dot_tri(keep_t, _softplus(sm + dtb_ref[...]) * a_neg)
        gc_t = gc.T
        edge = ck - 1 if d == 0 else 0
        last = jnp.concatenate([jnp.broadcast_to(gc[c * ck + edge:c * ck + edge + 1, :], (ck, LANES))
                                for c in range(nck)], axis=0)
        e_gc = jnp.exp(gc)
        e_end = jnp.exp(last - gc)
        e_last.append(jnp.exp(last))
        for h in range(N_HEADS):
            sl = slice(h * HEAD_DIM, (h + 1) * HEAD_DIM)
            lg = SM_DECAY + N_HEADS * d + h
            lb = SM_BETA + N_HEADS * d + h
            qh, kh, bcol = qn[:, sl], kn[:, sl], beta[:, lb:lb + 1]
            kb = kh * bcol
            rhs = jnp.concatenate([v[:, sl] * bcol, kb * e_gc[:, lg:lg + 1]], axis=1)
            qg.append(qh * e_gc[:, lg:lg + 1])
            kd.append(kh * e_end[:, lg:lg + 1])
            for s in range(nsub):
                rs = slice(s * sub, (s + 1) * sub)
                decay = jnp.exp(jnp.where(keep, gc[rs, lg:lg + 1] - gc_t[lg:lg + 1, rs], NEG_INF))
                a_list.append(jnp.where(strict, _dot_nt(kb[rs], kh[rs]) * decay, 0.0))
                rhs_list.append(rhs[rs])
                qkm.append(_dot_nt(qh[rs], kh[rs]) * decay)
    sols = _solve_unit_tri(a_list, rhs_list, rowi, coli, ck)
    sols = [jnp.concatenate(sols[n * nsub:(n + 1) * nsub], axis=0) for n in range(2 * N_HEADS)]

    chains = [(d, h) for d in range(2) for h in range(N_HEADS)]
    states = [state_s[d, h] for d, h in chains]
    v_new = [[None] * nck for _ in chains]
    o_st = [[None] * nck for _ in chains]
    for step in range(nck):
        rows = [slice((step if d == 0 else nck - 1 - step) * ck, (step if d == 0 else nck - 1 - step) * ck + ck)
                for d, _ in chains]
        ms = [_dot(jnp.concatenate([sols[n][r, HEAD_DIM:], qg[n][r]], axis=0), states[n])
              for n, r in enumerate(rows)]
        for n, (d, _) in enumerate(chains):
            c = step if d == 0 else nck - 1 - step
            v_new[n][c] = sols[n][rows[n], :HEAD_DIM] - ms[n][:ck]
            o_st[n][c] = ms[n][ck:]
        ups = [_dot_tn(kd[n][r], v_new[n][step if chains[n][0] == 0 else nck - 1 - step])
               for n, r in enumerate(rows)]
        for n, (d, h) in enumerate(chains):
            lg = SM_DECAY + N_HEADS * d + h
            states[n] = states[n] * e_last[d][rows[n].start:rows[n].start + 1, lg:lg + 1] + ups[n]
    cps = sub // ck
    outs = [jnp.concatenate(o_st[n], axis=0)
            + jnp.concatenate([_dot(qkm[n * nsub + s], jnp.concatenate(v_new[n][s * cps:(s + 1) * cps], axis=0))
                               for s in range(nsub)], axis=0)
            for n in range(len(chains))]
    of_ref[...] = jnp.concatenate(outs[:N_HEADS], axis=1)
    ob_ref[...] = jnp.concatenate(outs[N_HEADS:], axis=1)
    for n, (d, h) in enumerate(chains):
        state_s[d, h] = states[n]

    @pl.when(i == pl.num_programs(1) - 1)
    def _():
        sfin_ref[0] = state_s[...]


def gdn_params(conv_w, a_log, dt_bias):
    return _pad_rows(conv_w), _small_vec(a_log, SM_DECAY), _small_vec(dt_bias, SM_DECAY)


def gdn_mixer(qkv, sm, s0, bsz, cw, alog, dtb, rope=None):
    t, c = qkv.shape
    s = t // bsz
    q = min(GDN_TILE, s)
    nc = s // q
    halo = _halo(qkv, bsz, q)
    xf, xb = _chunk_specs(nc, q, c)
    hf, hb = _halo_specs(nc, c)
    sf, sb = _chunk_specs(nc, q, LANES)
    of, ob = _chunk_specs(nc, q, GROUP_W)
    st = pl.BlockSpec((1,) + s0.shape[1:], lambda b, i: (b, 0, 0, 0, 0))
    ins = [qkv, qkv, halo, halo, sm, sm]
    specs = [xf, xb, hf, hb, sf, sb]
    if rope is not None:
        tf = pl.BlockSpec((q, GROUP_W), lambda b, i: (i, 0))
        tb = pl.BlockSpec((q, GROUP_W), lambda b, i: (nc - 1 - i, 0))
        ins += [rope[0], rope[0], rope[1], rope[1]]
        specs += [tf, tb, tf, tb]
    ins += [s0, cw, alog, dtb]
    specs += [st, _full(cw.shape), _full(alog.shape), _full(dtb.shape)]
    o_shape = jax.ShapeDtypeStruct((t, GROUP_W), F32)
    return pl.pallas_call(
        functools.partial(_gdn_kernel, rope=rope is not None),
        grid=(bsz, nc),
        in_specs=specs,
        out_specs=[of, ob, st],
        out_shape=[o_shape, o_shape, jax.ShapeDtypeStruct(s0.shape, F32)],
        scratch_shapes=[pltpu.VMEM(s0.shape[1:], F32)],
        compiler_params=_cp("parallel", "arbitrary"),
        name="gdn_mixer",
    )(*ins)


def _split_hi_lo(a):
    hi = _mx(a)
    return hi, _mx(a - hi.astype(F32))


def _outproj_kernel(x_ref, ahf_ref, ahb_ref, ag_ref, bo_ref, cyf_ref, cyb_ref, cxc_ref, cz_ref,
                    dof_ref, dob_ref, dz_ref, wout_ref, gpost_ref, ga1_ref, gpre_ref, sc2_ref, sh2_ref,
                    dskip_ref, cnorm_ref, dnorm_ref, rhi_ref, rlo_ref, xo_ref, h2_ref, lg_ref):
    m_a = (ahf_ref[...] + ahb_ref[...]) * _gelu_tanh(ag_ref[...])
    y_c = (cyf_ref[...] + cyb_ref[...] + cxc_ref[...] * dskip_ref[...]) * _silu(cz_ref[...])
    m_c = _rms(y_c, cnorm_ref[...])
    o_d = dof_ref[...] + dob_ref[...]
    heads = []
    for h in range(N_HEADS):
        oh = o_d[:, h * HEAD_DIM:(h + 1) * HEAD_DIM]
        heads.append(oh * lax.rsqrt(jnp.mean(oh * oh, axis=-1, keepdims=True) + EPS))
    m_d = jnp.concatenate(heads, axis=1) * dnorm_ref[...] * _silu(dz_ref[...])
    mix = jnp.concatenate([_mx(m_a), _mx(bo_ref[...]), _mx(m_c), _mx(m_d)], axis=1)
    ml = jnp.dot(mix, wout_ref[...], preferred_element_type=F32)
    x_new = x_ref[...] + ga1_ref[0] * _rms(ml, gpost_ref[...])
    xo_ref[...] = x_new
    h2 = _rms(x_new, gpre_ref[...]) * (1.0 + sc2_ref[0]) + sh2_ref[0]
    hi, lo = _split_hi_lo(h2)
    h2_ref[...] = hi
    rhi = rhi_ref[...]
    lg_ref[...] = (jnp.dot(hi, rhi, preferred_element_type=F32) + jnp.dot(lo, rhi, preferred_element_type=F32)
                   + jnp.dot(hi, rlo_ref[...], preferred_element_type=F32))


def out_projection(x, mixers, w_out, gpost, ga1, gpre, sc2, sh2, dskip, cnorm, dnorm, router_w, tiles_per_group):
    t, d = x.shape
    tm = min(TOKEN_TILE, t)
    vec = lambda i: (i // tiles_per_group, 0, 0)
    row = lambda w: pl.BlockSpec((tm, w), lambda i: (i, 0))
    ne = LANES
    rhi, rlo = _split_hi_lo(jnp.pad(router_w.astype(F32), ((0, 0), (0, ne - router_w.shape[1]))))
    return pl.pallas_call(
        _outproj_kernel,
        grid=(t // tm,),
        in_specs=[row(d)] + [row(GROUP_W)] * 11
                 + [_full(w_out.shape), _full((1, d)), pl.BlockSpec((1, 1, d), vec), _full((1, d)),
                    pl.BlockSpec((1, 1, d), vec), pl.BlockSpec((1, 1, d), vec),
                    _full((1, GROUP_W)), _full((1, GROUP_W)), _full((1, GROUP_W)), _full(rhi.shape), _full(rlo.shape)],
        out_specs=[row(d), row(d), row(ne)],
        out_shape=[jax.ShapeDtypeStruct((t, d), F32), jax.ShapeDtypeStruct((t, d), MXU_DTYPE),
                   jax.ShapeDtypeStruct((t, ne), F32)],
        compiler_params=_cp("parallel"),
        name="out_projection",
    )(x, *mixers, w_out, gpost, ga1, gpre, sc2, sh2, dskip, cnorm, dnorm, rhi, rlo)


def _rank_before(vals, idx, count, stride):
    rank = jnp.zeros(vals.shape, jnp.int32)
    for j in range(count):
        other = vals[j * stride:j * stride + 1, :]
        ahead = (other > vals) | ((other == vals) & (idx > j))
        rank = rank + ahead.astype(jnp.int32)
    return rank


def _xor_partner(x, row, s):
    n = x.shape[0]
    return jnp.where((row & s) == 0, pltpu.roll(x, n - s, 0), pltpu.roll(x, s, 0))


def _router_kernel(lg_ref, rb_ref, gate_ref):
    ne = N_EXPERTS
    gsz = ne // N_EXPERT_GROUPS
    scores = _sigmoid(lg_ref[...].T[:ne, :])
    tm = scores.shape[1]
    biased = scores + rb_ref[...]
    row = lax.broadcasted_iota(jnp.int32, (ne, tm), 0)
    m1, m2 = biased, jnp.full((ne, tm), -jnp.inf, F32)
    s = 1
    while s < gsz:
        o1, o2 = _xor_partner(m1, row, s), _xor_partner(m2, row, s)
        m2 = jnp.maximum(jnp.minimum(m1, o1), jnp.maximum(m2, o2))
        m1 = jnp.maximum(m1, o1)
        s *= 2
    gidx = row >> (gsz.bit_length() - 1)
    group_ok = _rank_before(m1 + m2, gidx, N_EXPERT_GROUPS, gsz) < TOPK_GROUPS
    choice = jnp.where(group_ok, biased, -jnp.inf)
    picked = _rank_before(choice, row, ne, 1) < TOP_K
    gate = jnp.where(picked, scores, 0.0)
    gate = gate / jnp.sum(gate, axis=0, keepdims=True) * ROUTED_SCALE
    gate_ref[...] = jnp.concatenate([gate, jnp.zeros((LANES - ne, tm), F32)], axis=0).T


def router_gates(logits, router_b):
    t, w = logits.shape
    tm = min(TOKEN_TILE, t)
    return pl.pallas_call(
        _router_kernel,
        grid=(t // tm,),
        in_specs=[pl.BlockSpec((tm, w), lambda i: (i, 0)), _full((N_EXPERTS, 1))],
        out_specs=pl.BlockSpec((tm, w), lambda i: (i, 0)),
        out_shape=jax.ShapeDtypeStruct((t, w), F32),
        compiler_params=_cp("parallel"),
        name="router_gates",
    )(logits, router_b.reshape(N_EXPERTS, 1).astype(F32))


def _moe_kernel(h_ref, gate_ref, x_ref, wg_ref, wu_ref, wd_ref, sg_ref, su_ref, sd_ref, gpost_ref, ga2_ref,
                o_ref, acc_s):
    e = pl.program_id(1)
    h = h_ref[...]

    @pl.when(e == 0)
    def _():
        hs = _silu(jnp.dot(h, sg_ref[...], preferred_element_type=F32)) * jnp.dot(h, su_ref[...], preferred_element_type=F32)
        acc_s[...] = jnp.dot(_mx(hs), sd_ref[...], preferred_element_type=F32)

    gates = gate_ref[...]
    lane = lax.broadcasted_iota(jnp.int32, gates.shape, 1)
    hid = []
    for j in range(MOE_EB):
        gcol = jnp.sum(jnp.where(lane == e * MOE_EB + j, gates, 0.0), axis=1, keepdims=True)
        g = jnp.dot(h, wg_ref[j], preferred_element_type=F32)
        u = jnp.dot(h, wu_ref[j], preferred_element_type=F32)
        hid.append(_mx(_silu(g) * u * gcol))
    wd = wd_ref[...].reshape(MOE_EB * D_EXPERT, -1)
    acc_s[...] += jnp.dot(jnp.concatenate(hid, axis=1), wd, preferred_element_type=F32)

    @pl.when(e == pl.num_programs(1) - 1)
    def _():
        o_ref[...] = x_ref[...] + ga2_ref[0] * _rms(acc_s[...], gpost_ref[...])


def moe_ffn(h, gates, x, wg, wu, wd, sg, su, sd, gpost, ga2, tiles_per_group):
    t, d = x.shape
    tm = min(MOE_TILE, t)
    ne, _, f = wg.shape
    row = lambda w: pl.BlockSpec((tm, w), lambda i, e: (i, 0))
    return pl.pallas_call(
        _moe_kernel,
        grid=(t // tm, ne // MOE_EB),
        in_specs=[row(d), row(gates.shape[1]), row(d),
                  pl.BlockSpec((MOE_EB, d, f), lambda i, e: (e, 0, 0)),
                  pl.BlockSpec((MOE_EB, d, f), lambda i, e: (e, 0, 0)),
                  pl.BlockSpec((MOE_EB, f, d), lambda i, e: (e, 0, 0)),
                  _full(sg.shape), _full(su.shape), _full(sd.shape), _full((1, d)),
                  pl.BlockSpec((1, 1, d), lambda i, e: (i // tiles_per_group, 0, 0))],
        out_specs=row(d),
        out_shape=jax.ShapeDtypeStruct((t, d), F32),
        scratch_shapes=[pltpu.VMEM((tm, d), F32)],
        compiler_params=_cp("parallel", "arbitrary"),
        name="moe_ffn",
    )(h, gates, x, wg, wu, wd, sg, su, sd, gpost, ga2)


def _reorder_w_in(w_in):
    c = np.cumsum((0,) + (GROUP_W, GROUP_W, GROUP_W, GROUP_W, GROUP_W, GROUP_W, 2 * SSD_STATE, 2 * SSD_STATE,
                          GROUP_W, 2 * N_HEADS, GROUP_W, GROUP_W, GROUP_W, GROUP_W, 2 * N_HEADS, 2 * N_HEADS))
    seg = lambda a, b: w_in[:, c[a]:c[b]]
    small = jnp.concatenate([seg(9, 10), seg(14, 15), seg(15, 16),
                             jnp.zeros((w_in.shape[0], LANES - 6 * N_HEADS), w_in.dtype)], axis=1)
    return jnp.concatenate([seg(0, 5), seg(5, 8), seg(8, 9), seg(10, 13), seg(13, 14), small], axis=1)


def kernel(x, c, ctx, c_ctx, w_mod, b_mod, g_pre_mix, g_post_mix, g_pre_ffn, g_post_ffn, w_in, w_out, lru_conv_w, lru_conv_b, lru_wa, lru_ba, lru_wx, lru_bx, lru_lambda, na_bias, ssd_conv_w, ssd_conv_b, ssd_a_log, ssd_dt_bias, ssd_d, ssd_norm, gdn_conv_w, gdn_a_log, gdn_dt_bias, gdn_norm, router_w, router_b, we_gate, we_up, we_down, ws_gate, ws_up, ws_down):
    bsz, seq, d = x.shape
    n_ctx = ctx.shape[1]
    depth = w_mod.shape[0]
    lat_tpg = seq // min(TOKEN_TILE, seq)
    lat_mpg = seq // min(MOE_TILE, seq)
    ctx_tpg = max(bsz * n_ctx // TOKEN_TILE, 1)
    ctx_mpg = max(bsz * n_ctx // MOE_TILE, 1)

    cond = _pad_rows(jnp.concatenate([c, c_ctx[None, :]], axis=0))
    mod = modulation(cond, w_mod, b_mod).reshape(depth, SUBLANES, N_MOD, d)
    rope = rope_tables(seq)
    row = lambda v: v[None, :].astype(F32)

    xl = x.reshape(bsz * seq, d)
    xc = ctx.reshape(bsz * n_ctx, d)
    for l in range(depth):
        last = l == depth - 1
        m_lat = [mod[l, :bsz, k][:, None, :] for k in range(N_MOD)]
        m_ctx = [mod[l, bsz:bsz + 1, k][:, None, :] for k in range(N_MOD)]
        w_in_l = _reorder_w_in(w_in[l]).astype(MXU_DTYPE)
        pc = in_projection(xc, row(g_pre_mix[l]), m_ctx[1], m_ctx[0], w_in_l, ctx_tpg)
        pl_ = in_projection(xl, row(g_pre_mix[l]), m_lat[1], m_lat[0], w_in_l, lat_tpg)

        lru_p = lru_params(lru_conv_w[l], lru_conv_b[l], lru_wa[l], lru_ba[l], lru_wx[l], lru_bx[l], lru_lambda[l])
        a_cf, a_cb, a_st = lru_mixer(pc[P_AX], jnp.zeros((bsz, SUBLANES, GROUP_W), F32), bsz, *lru_p)
        a_lf, a_lb, _ = lru_mixer(pl_[P_AX], a_st, bsz, *lru_p)

        kc = pc[P_BK].reshape(bsz, n_ctx, GROUP_W)
        vc = pc[P_BV].reshape(bsz, n_ctx, GROUP_W)
        b_c = ctx_attention(pc[P_BQ].reshape(bsz, n_ctx, GROUP_W), kc, vc).reshape(bsz * n_ctx, GROUP_W)
        b_l = na_mixer(pl_[P_BQ], pl_[P_BK], pl_[P_BV], kc, vc, na_bias_slabs(na_bias[l]), bsz)

        ssd_p = ssd_params(ssd_conv_w[l], ssd_conv_b[l], ssd_a_log[l], ssd_dt_bias[l])
        c_cf, c_cb, c_cx, c_st = ssd_mixer(pc[P_CX], pc[P_SM], jnp.zeros((bsz, 2, N_HEADS, SSD_STATE, HEAD_DIM), F32),
                                           bsz, *ssd_p)
        c_lf, c_lb, c_lx, _ = ssd_mixer(pl_[P_CX], pl_[P_SM], c_st, bsz, *ssd_p)

        gdn_p = gdn_params(gdn_conv_w[l], gdn_a_log[l], gdn_dt_bias[l])
        d_cf, d_cb, d_st = gdn_mixer(pc[P_DX], pc[P_SM], jnp.zeros((bsz, 2, N_HEADS, HEAD_DIM, HEAD_DIM), F32),
                                     bsz, *gdn_p)
        d_lf, d_lb, _ = gdn_mixer(pl_[P_DX], pl_[P_SM], d_st, bsz, *gdn_p, rope=rope)

        epi = (w_out[l].astype(MXU_DTYPE), row(g_post_mix[l]))
        epi_tail = (row(jnp.repeat(ssd_d[l], HEAD_DIM)), row(ssd_norm[l]), row(jnp.tile(gdn_norm[l], N_HEADS)), router_w[l])
        moe_w = (we_gate[l].astype(MXU_DTYPE), we_up[l].astype(MXU_DTYPE), we_down[l].astype(MXU_DTYPE),
                 ws_gate[l].astype(MXU_DTYPE), ws_up[l].astype(MXU_DTYPE), ws_down[l].astype(MXU_DTYPE),
                 row(g_post_ffn[l]))

        mix_l = (a_lf, a_lb, pl_[P_AG], b_l, c_lf, c_lb, c_lx, pl_[P_CZ], d_lf, d_lb, pl_[P_DZ])
        xl, h2, lg = out_projection(xl, mix_l, *epi, m_lat[2], row(g_pre_ffn[l]), m_lat[4], m_lat[3], *epi_tail, lat_tpg)
        xl = moe_ffn(h2, router_gates(lg, router_b[l]), xl, *moe_w, m_lat[5], lat_mpg)
        if not last:
            mix_c = (a_cf, a_cb, pc[P_AG], b_c, c_cf, c_cb, c_cx, pc[P_CZ], d_cf, d_cb, pc[P_DZ])
            xc, h2, lg = out_projection(xc, mix_c, *epi, m_ctx[2], row(g_pre_ffn[l]), m_ctx[4], m_ctx[3], *epi_tail, ctx_tpg)
            xc = moe_ffn(h2, router_gates(lg, router_b[l]), xc, *moe_w, m_ctx[5], ctx_mpg)
    return xl.reshape(bsz, seq, d)
```

```python
import functools
import math

import jax
import jax.numpy as jnp
import numpy as np
from jax import lax
from jax.experimental import pallas as pl
from jax.experimental.pallas import tpu as pltpu

F32 = jnp.float32
MXU_DTYPE = jnp.bfloat16
HI = lax.Precision.HIGHEST

D_MODEL = 1024
GRID_W = 64
GROUP_W = 256
HEAD_DIM = 64
N_HEADS = 4
EPS = 1e-6
NEG_INF = -1e30
N_MOD = 6
LRU_C = 8.0
NA_WIN_ROWS = 8
NA_WIN_COLS = 16
SSD_STATE = 128
SSD_GROUPS = 2
ROPE_BASE = 10000.0
ROPE_AXIS_DIM = HEAD_DIM // 2
N_EXPERTS = 64
N_EXPERT_GROUPS = 8
TOPK_GROUPS = 4
TOP_K = 8
D_EXPERT = 256
ROUTED_SCALE = 2.5

LANES = 128
SUBLANES = 8
VMEM_LIMIT = 56 * 1024 * 1024

TOKEN_TILE = 512
LRU_CHUNK = 256
SSD_CHUNK = 128
GDN_CHUNK = 64
GDN_TILE = 256
GDN_SUB = 128
GDN_BASE = 16
MOE_TILE = 1024
MOE_EB = 4
MOE_BLOCK = 512
MOE_ROW_TILE = 256

P_WIDTHS = (256, 256, 256, 256, 256, 768, 256, 768, 256, 128)
(P_AX, P_AG, P_BQ, P_BK, P_BV, P_CX, P_CZ, P_DX, P_DZ, P_SM) = range(10)
SM_DT, SM_BETA, SM_DECAY = 0, 8, 16


def _cp(*sem):
    return pltpu.CompilerParams(dimension_semantics=sem, vmem_limit_bytes=VMEM_LIMIT)


def _mx(x):
    return x.astype(MXU_DTYPE)


def _dot(a, b):
    return jnp.dot(_mx(a), _mx(b), preferred_element_type=F32)


def _dot_nt(a, b):
    return lax.dot_general(_mx(a), _mx(b), (((1,), (1,)), ((), ())), preferred_element_type=F32)


def _dot_tn(a, b):
    return lax.dot_general(_mx(a), _mx(b), (((0,), (0,)), ((), ())), preferred_element_type=F32)


def _dot_hi(a, b):
    return jnp.dot(a, b, preferred_element_type=F32, precision=HI)


def _sigmoid(x):
    return 1.0 / (1.0 + jnp.exp(-x))


def _silu(x):
    return x * _sigmoid(x)


def _softplus(x):
    return jnp.maximum(x, 0.0) + jnp.log1p(jnp.exp(-jnp.abs(x)))


def _gelu_tanh(x):
    return 0.5 * x * (1.0 + jnp.tanh(math.sqrt(2.0 / math.pi) * (x + 0.044715 * (x * x * x))))


def _rms(x, g):
    return x * lax.rsqrt(jnp.mean(x * x, axis=-1, keepdims=True) + EPS) * g


def _full(shape):
    n = len(shape)
    return pl.BlockSpec(shape, lambda *_: (0,) * n)


MOD_COLS = 1536


def _mod_kernel(c_ref, w_ref, b_ref, o_ref):
    o_ref[0] = _dot_hi(_silu(c_ref[...]), w_ref[0]) + b_ref[0]


def modulation(cond, w_mod, b_mod):
    depth, d, n = w_mod.shape
    return pl.pallas_call(
        _mod_kernel,
        grid=(depth, n // MOD_COLS),
        in_specs=[pl.BlockSpec((SUBLANES, d), lambda l, j: (0, 0)),
                  pl.BlockSpec((1, d, MOD_COLS), lambda l, j: (l, 0, j)),
                  pl.BlockSpec((1, 1, MOD_COLS), lambda l, j: (l, 0, j))],
        out_specs=pl.BlockSpec((1, SUBLANES, MOD_COLS), lambda l, j: (l, 0, j)),
        out_shape=jax.ShapeDtypeStruct((depth, SUBLANES, n), F32),
        compiler_params=_cp("parallel", "parallel"),
        name="modulation",
    )(cond, w_mod, b_mod.reshape(depth, 1, n))


def _inproj_kernel(x_ref, g_ref, sc_ref, sh_ref, w_ref, *o_refs):
    h = _rms(x_ref[...], g_ref[...]) * (1.0 + sc_ref[0]) + sh_ref[0]
    p = _dot(h, w_ref[...])
    off = 0
    for o_ref, w in zip(o_refs, P_WIDTHS):
        o_ref[...] = p[:, off:off + w]
        off += w


def in_projection(x, g, sc, sh, w, tiles_per_group):
    t, d = x.shape
    tm = min(TOKEN_TILE, t)
    vec = lambda i: (i // tiles_per_group, 0, 0)
    return pl.pallas_call(
        _inproj_kernel,
        grid=(t // tm,),
        in_specs=[pl.BlockSpec((tm, d), lambda i: (i, 0)),
                  _full((1, d)),
                  pl.BlockSpec((1, 1, d), vec),
                  pl.BlockSpec((1, 1, d), vec),
                  _full(w.shape)],
        out_specs=[pl.BlockSpec((tm, wd), lambda i: (i, 0)) for wd in P_WIDTHS],
        out_shape=[jax.ShapeDtypeStruct((t, wd), F32) for wd in P_WIDTHS],
        compiler_params=_cp("parallel"),
        name="in_projection",
    )(x, g, sc, sh, w)


def _halo(p, bsz, q):
    c = p.shape[-1]
    pr = p.reshape(bsz, -1, q, c)
    nc = pr.shape[1]
    prev = jnp.concatenate([jnp.zeros((bsz, 1, 2, c), p.dtype), pr[:, :-1, q - 2:, :]], axis=1)
    nxt = jnp.concatenate([pr[:, 1:, :1, :], jnp.zeros((bsz, 1, 1, c), p.dtype)], axis=1)
    pad = jnp.zeros((bsz, nc, SUBLANES - 3, c), p.dtype)
    return jnp.concatenate([prev, nxt, pad], axis=2).reshape(bsz * nc, SUBLANES, c)


def _dwconv(x, halo, w, b=None):
    q = x.shape[0]
    row = lax.broadcasted_iota(jnp.int32, x.shape, 0)
    xm2 = jnp.where(row == 0, halo[0:1], jnp.where(row == 1, halo[1:2], pltpu.roll(x, 2, 0)))
    xm1 = jnp.where(row == 0, halo[1:2], pltpu.roll(x, 1, 0))
    xp1 = jnp.where(row == q - 1, halo[2:3], pltpu.roll(x, q - 1, 0))
    y = w[0:1] * xm2 + w[1:2] * xm1 + w[2:3] * x + w[3:4] * xp1
    return y if b is None else y + b


def _pad_rows(a, rows=SUBLANES):
    return jnp.concatenate([a, jnp.zeros((rows - a.shape[0],) + a.shape[1:], a.dtype)], axis=0)


def _chunk_specs(nc, q, c):
    fwd = pl.BlockSpec((q, c), lambda b, i: (b * nc + i, 0))
    bwd = pl.BlockSpec((q, c), lambda b, i: (b * nc + nc - 1 - i, 0))
    return fwd, bwd


def _halo_specs(nc, c):
    fwd = pl.BlockSpec((1, SUBLANES, c), lambda b, i: (b * nc + i, 0, 0))
    bwd = pl.BlockSpec((1, SUBLANES, c), lambda b, i: (b * nc + nc - 1 - i, 0, 0))
    return fwd, bwd


def _lru_kernel(xf_ref, xb_ref, hf_ref, hb_ref, h0_ref, cw_ref, cb_ref, wg_ref, bg_ref, lam_ref,
                yf_ref, yb_ref, hfin_ref, af_s, bf_s, ab_s, bb_s, carry_s):
    i = pl.program_id(1)
    q = xf_ref.shape[0]

    @pl.when(i == 0)
    def _():
        carry_s[...] = h0_ref[0]

    def coeffs(x_ref, halo_ref, d, a_s, b_s):
        u = _dwconv(x_ref[...], halo_ref[0], cw_ref[...], cb_ref[...])
        g = _dot(u, wg_ref[:, 2 * GROUP_W * d:2 * GROUP_W * (d + 1)]) + bg_ref[:, 2 * GROUP_W * d:2 * GROUP_W * (d + 1)]
        r = _sigmoid(g[:, :GROUP_W])
        gate_in = _sigmoid(g[:, GROUP_W:])
        log_a = -LRU_C * r * _softplus(-lam_ref[d:d + 1, :])
        a_s[...] = jnp.exp(log_a)
        b_s[...] = jnp.sqrt(1.0 - jnp.exp(2.0 * log_a)) * (gate_in * u)

    coeffs(xf_ref, hf_ref, 0, af_s, bf_s)
    coeffs(xb_ref, hb_ref, 1, ab_s, bb_s)

    ng = q // SUBLANES
    row = lax.broadcasted_iota(jnp.int32, (SUBLANES, GROUP_W), 0)

    def body(g, hs):
        h_f, h_b = hs
        i0 = pl.multiple_of(g * SUBLANES, SUBLANES)
        a = af_s[pl.ds(i0, SUBLANES), :]
        b = bf_s[pl.ds(i0, SUBLANES), :]
        for s in (1, 2, 4):
            m = row >= s
            b = jnp.where(m, a * pltpu.roll(b, s, 0) + b, b)
            a = jnp.where(m, a * pltpu.roll(a, s, 0), a)
        h = b + a * h_f
        yf_ref[pl.ds(i0, SUBLANES), :] = h
        h_f = h[SUBLANES - 1:SUBLANES, :]
        j0 = pl.multiple_of((ng - 1 - g) * SUBLANES, SUBLANES)
        a = ab_s[pl.ds(j0, SUBLANES), :]
        b = bb_s[pl.ds(j0, SUBLANES), :]
        for s in (1, 2, 4):
            m = row < SUBLANES - s
            b = jnp.where(m, a * pltpu.roll(b, SUBLANES - s, 0) + b, b)
            a = jnp.where(m, a * pltpu.roll(a, SUBLANES - s, 0), a)
        h = b + a * h_b
        yb_ref[pl.ds(j0, SUBLANES), :] = h
        return h_f, h[0:1, :]

    h_f, h_b = lax.fori_loop(0, ng, body, (carry_s[0:1, :], carry_s[1:2, :]))
    carry_s[0:1, :] = h_f
    carry_s[1:2, :] = h_b

    @pl.when(i == pl.num_programs(1) - 1)
    def _():
        hfin_ref[0] = carry_s[...]


def _block_diag(w):
    h, a, b = w.shape
    return jnp.einsum('hij,hg->higj', w, jnp.eye(h, dtype=w.dtype)).reshape(h * a, h * b)


def lru_params(conv_w, conv_b, wa, ba, wx, bx, lam):
    wg = jnp.concatenate([_block_diag(wa[0]), _block_diag(wx[0]), _block_diag(wa[1]), _block_diag(wx[1])], axis=1)
    bg = jnp.concatenate([ba[0], bx[0], ba[1], bx[1]])[None, :]
    return _pad_rows(conv_w), conv_b[None, :], wg.astype(MXU_DTYPE), bg, _pad_rows(lam)


def lru_mixer(x, h0, bsz, cw, cb, wg, bg, lam):
    t, c = x.shape
    s = t // bsz
    q = min(LRU_CHUNK, s)
    nc = s // q
    halo = _halo(x, bsz, q)
    xf, xb = _chunk_specs(nc, q, c)
    hf, hb = _halo_specs(nc, c)
    st = pl.BlockSpec((1, SUBLANES, c), lambda b, i: (b, 0, 0))
    return pl.pallas_call(
        _lru_kernel,
        grid=(bsz, nc),
        in_specs=[xf, xb, hf, hb, st, _full(cw.shape), _full(cb.shape), _full(wg.shape), _full(bg.shape),
                  _full(lam.shape)],
        out_specs=[xf, xb, st],
        out_shape=[jax.ShapeDtypeStruct((t, c), F32), jax.ShapeDtypeStruct((t, c), F32),
                   jax.ShapeDtypeStruct((bsz, SUBLANES, c), F32)],
        scratch_shapes=[pltpu.VMEM((q, c), F32)] * 4 + [pltpu.VMEM((SUBLANES, c), F32)],
        compiler_params=_cp("parallel", "arbitrary"),
        name="lru_mixer",
    )(x, x, halo, halo, h0, cw, cb, wg, bg, lam)


NA_KEYS = NA_WIN_ROWS * GRID_W
NA_ROW_BLOCK = 4


def na_bias_slabs(table):
    qc = np.arange(GRID_W)[:, None]
    kc = np.arange(GRID_W)[None, :]
    win0 = np.clip(qc - NA_WIN_COLS // 2, 0, GRID_W - NA_WIN_COLS)
    ok = (kc >= win0) & (kc < win0 + NA_WIN_COLS)
    dc = np.clip(kc - qc + NA_WIN_COLS - 1, 0, 2 * NA_WIN_COLS - 2)
    dr = np.arange(NA_WIN_ROWS)[:, None] + np.arange(NA_WIN_ROWS)[None, :]
    b = table.astype(F32)[:, dr][:, :, :, dc]
    b = jnp.where(ok[None, None, None], b, NEG_INF)
    h = table.shape[0]
    return b.transpose(0, 1, 3, 2, 4).reshape(h, NA_WIN_ROWS, GRID_W, NA_KEYS)


def _na_span_start(j, rows):
    return jnp.clip(j * NA_ROW_BLOCK - NA_WIN_ROWS // 2, 0, rows - (NA_ROW_BLOCK + NA_WIN_ROWS - 1))


def _na_kernel(q_ref, kw_ref, vw_ref, kc_ref, vc_ref, slab_ref, o_ref, *, rows):
    j = pl.program_id(1)
    ustart = _na_span_start(j, rows)
    q = q_ref[...] * (HEAD_DIM ** -0.5)
    kc, vc = kc_ref[0], vc_ref[0]
    heads = [slice(h * HEAD_DIM, (h + 1) * HEAD_DIM) for h in range(N_HEADS)]
    qrows = [slice(i * GRID_W, (i + 1) * GRID_W) for i in range(NA_ROW_BLOCK)]
    kws, vws, offs = [], [], []
    for i in range(NA_ROW_BLOCK):
        r = j * NA_ROW_BLOCK + i
        r0 = jnp.clip(r - NA_WIN_ROWS // 2, 0, rows - NA_WIN_ROWS)
        start = pl.multiple_of((r0 - ustart) * GRID_W, GRID_W)
        kws.append(kw_ref[pl.ds(start, NA_KEYS), :])
        vws.append(vw_ref[pl.ds(start, NA_KEYS), :])
        offs.append(r0 - r + NA_WIN_ROWS - 1)
    s_ctx = [_dot_nt(q[:, sl], kc[:, sl]) for sl in heads]
    s_loc = [[_dot_nt(q[qr, sl], kws[i][:, sl]) + slab_ref[h, offs[i]] for h, sl in enumerate(heads)]
             for i, qr in enumerate(qrows)]
    m = [[jnp.maximum(jnp.max(s_loc[i][h], axis=-1, keepdims=True), jnp.max(s_ctx[h][qr], axis=-1, keepdims=True))
          for h in range(N_HEADS)] for i, qr in enumerate(qrows)]
    p_loc = [[jnp.exp(s_loc[i][h] - m[i][h]) for h in range(N_HEADS)] for i in range(NA_ROW_BLOCK)]
    p_ctx = [jnp.exp(s_ctx[h] - jnp.concatenate([m[i][h] for i in range(NA_ROW_BLOCK)], axis=0))
             for h in range(N_HEADS)]
    o_ctx = [_dot(p_ctx[h], vc[:, sl]) for h, sl in enumerate(heads)]
    rows_out = []
    for i, qr in enumerate(qrows):
        outs = []
        for h, sl in enumerate(heads):
            den = jnp.sum(p_loc[i][h], axis=-1, keepdims=True) + jnp.sum(p_ctx[h][qr], axis=-1, keepdims=True)
            outs.append((_dot(p_loc[i][h], vws[i][:, sl]) + o_ctx[h][qr]) / den)
        rows_out.append(jnp.concatenate(outs, axis=1))
    o_ref[...] = jnp.concatenate(rows_out, axis=0)


def na_mixer(q, k, v, kc, vc, slabs, bsz):
    t, c = q.shape
    s = t // bsz
    rows = s // GRID_W
    n_ctx = kc.shape[1]
    span = (NA_ROW_BLOCK + NA_WIN_ROWS - 1) * GRID_W

    def win(b, j):
        return ((b * rows + _na_span_start(j, rows)) * GRID_W, 0)

    wspec = pl.BlockSpec((pl.Element(span), pl.Element(c)), win)
    cspec = pl.BlockSpec((1, n_ctx, c), lambda b, j: (b, 0, 0))
    qspec = pl.BlockSpec((NA_ROW_BLOCK * GRID_W, c), lambda b, j: (b * (rows // NA_ROW_BLOCK) + j, 0))
    return pl.pallas_call(
        functools.partial(_na_kernel, rows=rows),
        grid=(bsz, rows // NA_ROW_BLOCK),
        in_specs=[qspec, wspec, wspec, cspec, cspec, _full(slabs.shape)],
        out_specs=qspec,
        out_shape=jax.ShapeDtypeStruct((t, c), F32),
        compiler_params=_cp("parallel", "arbitrary"),
        name="na_mixer",
    )(q, k, v, kc, vc, slabs)


def _ctx_attn_kernel(q_ref, k_ref, v_ref, o_ref):
    q = q_ref[0] * (HEAD_DIM ** -0.5)
    k, v = k_ref[0], v_ref[0]
    outs = []
    for h in range(N_HEADS):
        sl = slice(h * HEAD_DIM, (h + 1) * HEAD_DIM)
        s = _dot_nt(q[:, sl], k[:, sl])
        p = jnp.exp(s - jnp.max(s, axis=-1, keepdims=True))
        outs.append(_dot(p, v[:, sl]) / jnp.sum(p, axis=-1, keepdims=True))
    o_ref[0] = jnp.concatenate(outs, axis=1)


def ctx_attention(q, k, v):
    spec = pl.BlockSpec((1,) + q.shape[1:], lambda b: (b, 0, 0))
    return pl.pallas_call(
        _ctx_attn_kernel,
        grid=(q.shape[0],),
        in_specs=[spec, spec, spec],
        out_specs=spec,
        out_shape=jax.ShapeDtypeStruct(q.shape, F32),
        compiler_params=_cp("parallel"),
        name="ctx_attention",
    )(q, k, v)


def _small_vec(vals, off):
    v = jnp.zeros((LANES,), F32).at[off:off + 2 * N_HEADS].set(vals.reshape(-1).astype(F32))
    return v[None, :]


def _lane_mask(off):
    lane = lax.broadcasted_iota(jnp.int32, (1, LANES), 1)
    return (lane >= off) & (lane < off + 2 * N_HEADS)


def _tri_masks(q):
    rowi = lax.broadcasted_iota(jnp.int32, (q, q), 0)
    coli = lax.broadcasted_iota(jnp.int32, (q, q), 1)
    return rowi, coli


def _ssd_kernel(xf_ref, xb_ref, hf_ref, hb_ref, sf_ref, sb_ref, h0_ref, cw_ref, cb_ref, dtb_ref, alog_ref,
                yf_ref, yb_ref, xc_ref, hfin_ref, state_s):
    i = pl.program_id(1)
    q = xf_ref.shape[0]

    @pl.when(i == 0)
    def _():
        state_s[...] = h0_ref[0]

    rowi, coli = _tri_masks(q)
    a_neg = jnp.where(_lane_mask(SM_DT), -jnp.exp(alog_ref[...]), 0.0)

    def direction(x_ref, halo_ref, sm_ref, d, y_ref):
        xbc = _silu(_dwconv(x_ref[...], halo_ref[0], cw_ref[...], cb_ref[...]))
        if d == 0:
            xc_ref[...] = xbc[:, :GROUP_W]
        dt = _softplus(sm_ref[...] + dtb_ref[...])
        keep = (rowi >= coli) if d == 0 else (rowi <= coli)
        acum = _dot_hi(keep.astype(F32), dt * a_neg)
        acum_t = acum.T
        last = acum[q - 1:q, :] if d == 0 else acum[0:1, :]
        dec_end = jnp.exp(last - acum)
        e_acum = jnp.exp(acum)
        e_last = jnp.exp(last)
        ys = []
        for g in range(SSD_GROUPS):
            bg = xbc[:, GROUP_W + SSD_STATE * g:GROUP_W + SSD_STATE * (g + 1)]
            cg = xbc[:, GROUP_W + SSD_STATE * (SSD_GROUPS + g):GROUP_W + SSD_STATE * (SSD_GROUPS + g + 1)]
            cbt = _dot_nt(cg, bg)
            for hh in range(N_HEADS // SSD_GROUPS):
                h = g * (N_HEADS // SSD_GROUPS) + hh
                ln = SM_DT + N_HEADS * d + h
                lmat = jnp.exp(jnp.where(keep, acum[:, ln:ln + 1] - acum_t[ln:ln + 1, :], NEG_INF))
                xdt = xbc[:, h * HEAD_DIM:(h + 1) * HEAD_DIM] * dt[:, ln:ln + 1]
                st = state_s[d, h]
                ys.append(_dot(cbt * lmat, xdt) + _dot(cg * e_acum[:, ln:ln + 1], st))
                state_s[d, h] = st * e_last[:, ln:ln + 1] + _dot_tn(bg * dec_end[:, ln:ln + 1], xdt)
        y_ref[...] = jnp.concatenate(ys, axis=1)

    direction(xf_ref, hf_ref, sf_ref, 0, yf_ref)
    direction(xb_ref, hb_ref, sb_ref, 1, yb_ref)

    @pl.when(i == pl.num_programs(1) - 1)
    def _():
        hfin_ref[0] = state_s[...]


def ssd_params(conv_w, conv_b, a_log, dt_bias):
    return _pad_rows(conv_w), conv_b[None, :], _small_vec(dt_bias, SM_DT), _small_vec(a_log, SM_DT)


def ssd_mixer(xbc, sm, h0, bsz, cw, cb, dtb, alog):
    t, c = xbc.shape
    s = t // bsz
    q = min(SSD_CHUNK, s)
    nc = s // q
    halo = _halo(xbc, bsz, q)
    xf, xb = _chunk_specs(nc, q, c)
    hf, hb = _halo_specs(nc, c)
    sf, sb = _chunk_specs(nc, q, LANES)
    yf, yb = _chunk_specs(nc, q, GROUP_W)
    st = pl.BlockSpec((1,) + h0.shape[1:], lambda b, i: (b, 0, 0, 0, 0))
    y_shape = jax.ShapeDtypeStruct((t, GROUP_W), F32)
    return pl.pallas_call(
        _ssd_kernel,
        grid=(bsz, nc),
        in_specs=[xf, xb, hf, hb, sf, sb, st, _full(cw.shape), _full(cb.shape), _full(dtb.shape), _full(alog.shape)],
        out_specs=[yf, yb, yf, st],
        out_shape=[y_shape, y_shape, y_shape, jax.ShapeDtypeStruct(h0.shape, F32)],
        scratch_shapes=[pltpu.VMEM(h0.shape[1:], F32)],
        compiler_params=_cp("parallel", "arbitrary"),
        name="ssd_mixer",
    )(xbc, xbc, halo, halo, sm, sm, h0, cw, cb, dtb, alog)


def rope_tables(seq):
    t = jnp.arange(seq)
    row = (t // GRID_W).astype(F32)
    col = (t % GRID_W).astype(F32)
    inv = ROPE_BASE ** (-jnp.arange(0, ROPE_AXIS_DIM, 2, dtype=F32) / ROPE_AXIS_DIM)
    ar, ac = row[:, None] * inv, col[:, None] * inv
    cos = jnp.concatenate([jnp.cos(ar), jnp.cos(ar), jnp.cos(ac), jnp.cos(ac)], axis=1)
    sin = jnp.concatenate([-jnp.sin(ar), jnp.sin(ar), -jnp.sin(ac), jnp.sin(ac)], axis=1)
    return jnp.tile(cos, (1, N_HEADS)), jnp.tile(sin, (1, N_HEADS))


def _swap16(x):
    lane = lax.broadcasted_iota(jnp.int32, x.shape, 1)
    half = ROPE_AXIS_DIM // 2
    return jnp.where((lane & (ROPE_AXIS_DIM - 1)) < half,
                     pltpu.roll(x, x.shape[1] - half, 1), pltpu.roll(x, half, 1))


def _l2norm_heads(x):
    outs = []
    for h in range(N_HEADS):
        xh = x[:, h * HEAD_DIM:(h + 1) * HEAD_DIM]
        outs.append(xh * lax.rsqrt(jnp.sum(xh * xh, axis=-1, keepdims=True) + EPS))
    return jnp.concatenate(outs, axis=1)


def _dot_tri(mask, x):
    m = _mx(mask.astype(F32))
    x1 = _mx(x)
    r1 = x - x1.astype(F32)
    x2 = _mx(r1)
    x3 = _mx(r1 - x2.astype(F32))
    return (jnp.dot(m, x1, preferred_element_type=F32) + jnp.dot(m, x2, preferred_element_type=F32)
            + jnp.dot(m, x3, preferred_element_type=F32))


def _same_block(rowi, coli, n):
    sh = n.bit_length() - 1
    return (rowi >> sh) == (coli >> sh)


def _solve_unit_tri(a_list, rhs_list, rowi, coli, chunk):
    mm = lambda x, y: jnp.dot(x, y, preferred_element_type=F32)
    eye = (rowi == coli).astype(F32)
    in_base = _same_block(rowi, coli, GDN_BASE)
    base = [_mx(jnp.where(in_base, a, 0.0)) for a in a_list]
    ts = [jnp.where(in_base, eye - a, 0.0) for a in a_list]
    ps = [_mx(mm(b, b)) for b in base]
    ts = [t + mm(_mx(t), p) for t, p in zip(ts, ps)]
    n = 4
    while n < GDN_BASE:
        ps = [_mx(mm(p, p)) for p in ps]
        ts = [t + mm(_mx(t), p) for t, p in zip(ts, ps)]
        n *= 2
    n = GDN_BASE
    while 2 * n < chunk:
        inner = _same_block(rowi, coli, 2 * n) & jnp.logical_not(_same_block(rowi, coli, n))
        offs = [_mx(jnp.where(inner, a, 0.0)) for a in a_list]
        tb = [_mx(t) for t in ts]
        ms = [_mx(mm(t, off)) for t, off in zip(tb, offs)]
        ts = [t - mm(m, t_b) for t, m, t_b in zip(ts, ms, tb)]
        n *= 2
    outer = jnp.logical_not(_same_block(rowi, coli, n))
    offs = [_mx(jnp.where(outer, a, 0.0)) for a in a_list]
    tb = [_mx(t) for t in ts]
    ys = [mm(t, _mx(r)) for t, r in zip(tb, rhs_list)]
    zs = [_mx(mm(off, _mx(y))) for off, y in zip(offs, ys)]
    return [y - mm(t, z) for y, t, z in zip(ys, tb, zs)]


def _gdn_kernel(*refs, rope):
    if rope:
        (xf_ref, xb_ref, hf_ref, hb_ref, sf_ref, sb_ref, cf_ref, cb_ref, nf_ref, nb_ref,
         s0_ref, cw_ref, alog_ref, dtb_ref, of_ref, ob_ref, sfin_ref, state_s) = refs
    else:
        (xf_ref, xb_ref, hf_ref, hb_ref, sf_ref, sb_ref,
         s0_ref, cw_ref, alog_ref, dtb_ref, of_ref, ob_ref, sfin_ref, state_s) = refs
        cf_ref = cb_ref = nf_ref = nb_ref = None
    i = pl.program_id(1)
    tq = xf_ref.shape[0]
    ck = min(GDN_CHUNK, tq)
    nck = tq // ck

    @pl.when(i == 0)
    def _():
        state_s[...] = s0_ref[0]

    sub = min(GDN_SUB, tq)
    nsub = tq // sub
    rowt, colt = _tri_masks(tq)
    in_chunk_t = _same_block(rowt, colt, ck)
    rowi, coli = _tri_masks(sub)
    in_chunk = _same_block(rowi, coli, ck)
    a_neg = jnp.where(_lane_mask(SM_DECAY), -jnp.exp(alog_ref[...]), 0.0)

    a_list, rhs_list, qkm, qg, kd, e_last = [], [], [], [], [], []
    for d, (x_ref, halo_ref, sm_ref, cos_ref, sin_ref) in enumerate(
            ((xf_ref, hf_ref, sf_ref, cf_ref, nf_ref), (xb_ref, hb_ref, sb_ref, cb_ref, nb_ref))):
        qkv = _silu(_dwconv(x_ref[...], halo_ref[0], cw_ref[...]))
        qn = _l2norm_heads(qkv[:, :GROUP_W])
        kn = _l2norm_heads(qkv[:, GROUP_W:2 * GROUP_W])
        v = qkv[:, 2 * GROUP_W:]
        if rope:
            cos, sin = cos_ref[...], sin_ref[...]
            qn = qn * cos + _swap16(qn) * sin
            kn = kn * cos + _swap16(kn) * sin
        qn = qn * (HEAD_DIM ** -0.5)
        sm = sm_ref[...]
        beta = _sigmoid(sm)
        keep_t = in_chunk_t & ((rowt >= colt) if d == 0 else (rowt <= colt))
        keep = in_chunk & ((rowi >= coli) if d == 0 else (rowi <= coli))
        strict = in_chunk & ((rowi > coli) if d == 0 else (rowi < coli))
        gc = _dot_tri(keep_t, _softplus(sm + dtb_ref[...]) * a_neg)
        gc_t = gc.T
        edge = ck - 1 if d == 0 else 0
        last = jnp.concatenate([jnp.broadcast_to(gc[c * ck + edge:c * ck + edge + 1, :], (ck, LANES))
                                for c in range(nck)], axis=0)
        e_gc = jnp.exp(gc)
        e_end = jnp.exp(last - gc)
        e_last.append(jnp.exp(last))
        for h in range(N_HEADS):
            sl = slice(h * HEAD_DIM, (h + 1) * HEAD_DIM)
            lg = SM_DECAY + N_HEADS * d + h
            lb = SM_BETA + N_HEADS * d + h
            qh, kh, bcol = qn[:, sl], kn[:, sl], beta[:, lb:lb + 1]
            kb = kh * bcol
            rhs = jnp.concatenate([v[:, sl] * bcol, kb * e_gc[:, lg:lg + 1]], axis=1)
            qg.append(qh * e_gc[:, lg:lg + 1])
            kd.append(kh * e_end[:, lg:lg + 1])
            for s in range(nsub):
                rs = slice(s * sub, (s + 1) * sub)
                decay = jnp.exp(jnp.where(keep, gc[rs, lg:lg + 1] - gc_t[lg:lg + 1, rs], NEG_INF))
                a_list.append(jnp.where(strict, _dot_nt(kb[rs], kh[rs]) * decay, 0.0))
                rhs_list.append(rhs[rs])
                qkm.append(_dot_nt(qh[rs], kh[rs]) * decay)
    sols = _solve_unit_tri(a_list, rhs_list, rowi, coli, ck)
    sols = [jnp.concatenate(sols[n * nsub:(n + 1) * nsub], axis=0) for n in range(2 * N_HEADS)]

    chains = [(d, h) for d in range(2) for h in range(N_HEADS)]
    states = [state_s[d, h] for d, h in chains]
    v_new = [[None] * nck for _ in chains]
    o_st = [[None] * nck for _ in chains]
    for step in range(nck):
        rows = [slice((step if d == 0 else nck - 1 - step) * ck, (step if d == 0 else nck - 1 - step) * ck + ck)
                for d, _ in chains]
        ms = [_dot(jnp.concatenate([sols[n][r, HEAD_DIM:], qg[n][r]], axis=0), states[n])
              for n, r in enumerate(rows)]
        for n, (d, _) in enumerate(chains):
            c = step if d == 0 else nck - 1 - step
            v_new[n][c] = sols[n][rows[n], :HEAD_DIM] - ms[n][:ck]
            o_st[n][c] = ms[n][ck:]
        ups = [_dot_tn(kd[n][r], v_new[n][step if chains[n][0] == 0 else nck - 1 - step])
               for n, r in enumerate(rows)]
        for n, (d, h) in enumerate(chains):
            lg = SM_DECAY + N_HEADS * d + h
            states[n] = states[n] * e_last[d][rows[n].start:rows[n].start + 1, lg:lg + 1] + ups[n]
    cps = sub // ck
    outs = [jnp.concatenate(o_st[n], axis=0)
            + jnp.concatenate([_dot(qkm[n * nsub + s], jnp.concatenate(v_new[n][s * cps:(s + 1) * cps], axis=0))
                               for s in range(nsub)], axis=0)
            for n in range(len(chains))]
    of_ref[...] = jnp.concatenate(outs[:N_HEADS], axis=1)
    ob_ref[...] = jnp.concatenate(outs[N_HEADS:], axis=1)
    for n, (d, h) in enumerate(chains):
        state_s[d, h] = states[n]

    @pl.when(i == pl.num_programs(1) - 1)
    def _():
        sfin_ref[0] = state_s[...]


def gdn_params(conv_w, a_log, dt_bias):
    return _pad_rows(conv_w), _small_vec(a_log, SM_DECAY), _small_vec(dt_bias, SM_DECAY)


def gdn_mixer(qkv, sm, s0, bsz, cw, alog, dtb, rope=None):
    t, c = qkv.shape
    s = t // bsz
    q = min(GDN_TILE, s)
    nc = s // q
    halo = _halo(qkv, bsz, q)
    xf, xb = _chunk_specs(nc, q, c)
    hf, hb = _halo_specs(nc, c)
    sf, sb = _chunk_specs(nc, q, LANES)
    of, ob = _chunk_specs(nc, q, GROUP_W)
    st = pl.BlockSpec((1,) + s0.shape[1:], lambda b, i: (b, 0, 0, 0, 0))
    ins = [qkv, qkv, halo, halo, sm, sm]
    specs = [xf, xb, hf, hb, sf, sb]
    if rope is not None:
        tf = pl.BlockSpec((q, GROUP_W), lambda b, i: (i, 0))
        tb = pl.BlockSpec((q, GROUP_W), lambda b, i: (nc - 1 - i, 0))
        ins += [rope[0], rope[0], rope[1], rope[1]]
        specs += [tf, tb, tf, tb]
    ins += [s0, cw, alog, dtb]
    specs += [st, _full(cw.shape), _full(alog.shape), _full(dtb.shape)]
    o_shape = jax.ShapeDtypeStruct((t, GROUP_W), F32)
    return pl.pallas_call(
        functools.partial(_gdn_kernel, rope=rope is not None),
        grid=(bsz, nc),
        in_specs=specs,
        out_specs=[of, ob, st],
        out_shape=[o_shape, o_shape, jax.ShapeDtypeStruct(s0.shape, F32)],
        scratch_shapes=[pltpu.VMEM(s0.shape[1:], F32)],
        compiler_params=_cp("parallel", "arbitrary"),
        name="gdn_mixer",
    )(*ins)


def _split_hi_lo(a):
    hi = _mx(a)
    return hi, _mx(a - hi.astype(F32))


def _outproj_kernel(x_ref, ahf_ref, ahb_ref, ag_ref, bo_ref, cyf_ref, cyb_ref, cxc_ref, cz_ref,
                    dof_ref, dob_ref, dz_ref, wout_ref, gpost_ref, ga1_ref, gpre_ref, sc2_ref, sh2_ref,
                    dskip_ref, cnorm_ref, dnorm_ref, rhi_ref, rlo_ref, xo_ref, h2_ref, lg_ref):
    m_a = (ahf_ref[...] + ahb_ref[...]) * _gelu_tanh(ag_ref[...])
    y_c = (cyf_ref[...] + cyb_ref[...] + cxc_ref[...] * dskip_ref[...]) * _silu(cz_ref[...])
    m_c = _rms(y_c, cnorm_ref[...])
    o_d = dof_ref[...] + dob_ref[...]
    heads = []
    for h in range(N_HEADS):
        oh = o_d[:, h * HEAD_DIM:(h + 1) * HEAD_DIM]
        heads.append(oh * lax.rsqrt(jnp.mean(oh * oh, axis=-1, keepdims=True) + EPS))
    m_d = jnp.concatenate(heads, axis=1) * dnorm_ref[...] * _silu(dz_ref[...])
    mix = jnp.concatenate([_mx(m_a), _mx(bo_ref[...]), _mx(m_c), _mx(m_d)], axis=1)
    ml = jnp.dot(mix, wout_ref[...], preferred_element_type=F32)
    x_new = x_ref[...] + ga1_ref[0] * _rms(ml, gpost_ref[...])
    xo_ref[...] = x_new
    h2 = _rms(x_new, gpre_ref[...]) * (1.0 + sc2_ref[0]) + sh2_ref[0]
    hi, lo = _split_hi_lo(h2)
    h2_ref[...] = h2
    rhi = rhi_ref[...]
    lg_ref[...] = (jnp.dot(hi, rhi, preferred_element_type=F32) + jnp.dot(lo, rhi, preferred_element_type=F32)
                   + jnp.dot(hi, rlo_ref[...], preferred_element_type=F32))


def out_projection(x, mixers, w_out, gpost, ga1, gpre, sc2, sh2, dskip, cnorm, dnorm, router_w, tiles_per_group):
    t, d = x.shape
    tm = min(TOKEN_TILE, t)
    vec = lambda i: (i // tiles_per_group, 0, 0)
    row = lambda w: pl.BlockSpec((tm, w), lambda i: (i, 0))
    ne = LANES
    rhi, rlo = _split_hi_lo(jnp.pad(router_w.astype(F32), ((0, 0), (0, ne - router_w.shape[1]))))
    return pl.pallas_call(
        _outproj_kernel,
        grid=(t // tm,),
        in_specs=[row(d)] + [row(GROUP_W)] * 11
                 + [_full(w_out.shape), _full((1, d)), pl.BlockSpec((1, 1, d), vec), _full((1, d)),
                    pl.BlockSpec((1, 1, d), vec), pl.BlockSpec((1, 1, d), vec),
                    _full((1, GROUP_W)), _full((1, GROUP_W)), _full((1, GROUP_W)), _full(rhi.shape), _full(rlo.shape)],
        out_specs=[row(d), row(d), row(ne)],
        out_shape=[jax.ShapeDtypeStruct((t, d), F32), jax.ShapeDtypeStruct((t, d), F32),
                   jax.ShapeDtypeStruct((t, ne), F32)],
        compiler_params=_cp("parallel"),
        name="out_projection",
    )(x, *mixers, w_out, gpost, ga1, gpre, sc2, sh2, dskip, cnorm, dnorm, rhi, rlo)


def _rank_before(vals, idx, count, stride):
    rank = jnp.zeros(vals.shape, jnp.int32)
    for j in range(count):
        other = vals[j * stride:j * stride + 1, :]
        ahead = (other > vals) | ((other == vals) & (idx > j))
        rank = rank + ahead.astype(jnp.int32)
    return rank


def _xor_partner(x, row, s):
    n = x.shape[0]
    return jnp.where((row & s) == 0, pltpu.roll(x, n - s, 0), pltpu.roll(x, s, 0))


def _route(logits, router_b):
    ne = N_EXPERTS
    gsz = ne // N_EXPERT_GROUPS
    scores = _sigmoid(logits.T[:ne, :])
    tm = scores.shape[1]
    biased = scores + router_b
    row = lax.broadcasted_iota(jnp.int32, (ne, tm), 0)
    m1, m2 = biased, jnp.full((ne, tm), -jnp.inf, F32)
    s = 1
    while s < gsz:
        o1, o2 = _xor_partner(m1, row, s), _xor_partner(m2, row, s)
        m2 = jnp.maximum(jnp.minimum(m1, o1), jnp.maximum(m2, o2))
        m1 = jnp.maximum(m1, o1)
        s *= 2
    gidx = row >> (gsz.bit_length() - 1)
    group_ok = _rank_before(m1 + m2, gidx, N_EXPERT_GROUPS, gsz) < TOPK_GROUPS
    choice = jnp.where(group_ok, biased, -jnp.inf)
    rank = _rank_before(choice, row, ne, 1)
    gate = jnp.where(rank < TOP_K, scores, 0.0)
    gate = gate / jnp.sum(gate, axis=0, keepdims=True) * ROUTED_SCALE
    return gate, rank, row


def _to_token_major(x):
    n, tm = x.shape
    return jnp.concatenate([x, jnp.zeros((LANES - n, tm), x.dtype)], axis=0).T


def _router_kernel(lg_ref, rb_ref, gate_ref):
    gate, _, _ = _route(lg_ref[...], rb_ref[...])
    gate_ref[...] = _to_token_major(gate)


def _router_dispatch_kernel(lg_ref, rb_ref, gk_ref, ek_ref, pk_ref, cnt_ref, carry_s):
    i = pl.program_id(0)

    @pl.when(i == 0)
    def _():
        carry_s[...] = jnp.zeros(carry_s.shape, F32)

    gate, rank, row = _route(lg_ref[...], rb_ref[...])
    tm = gate.shape[1]
    picked = (rank < TOP_K).astype(F32)
    before = lax.broadcasted_iota(jnp.int32, (tm, tm), 0) < lax.broadcasted_iota(jnp.int32, (tm, tm), 1)
    pos = _dot(picked, before.astype(F32)) + carry_s[:, 0:1]
    carry_s[...] = carry_s[...] + jnp.sum(picked, axis=1, keepdims=True)
    gk, ek, pk = [], [], []
    for k in range(TOP_K):
        sel = rank == k
        gk.append(jnp.sum(jnp.where(sel, gate, 0.0), axis=0, keepdims=True))
        ek.append(jnp.sum(jnp.where(sel, row, 0), axis=0, keepdims=True))
        pk.append(jnp.sum(jnp.where(sel, pos, 0.0), axis=0, keepdims=True))
    gk_ref[...] = _to_token_major(jnp.concatenate(gk, axis=0))
    ek_ref[...] = jnp.concatenate(ek, axis=0)
    pk_ref[...] = jnp.concatenate(pk, axis=0).astype(jnp.int32)

    @pl.when(i == pl.num_programs(0) - 1)
    def _():
        cnt_ref[...] = carry_s[...].astype(jnp.int32)


def router_dispatch(logits, router_b):
    t, w = logits.shape
    tm = min(TOKEN_TILE, t)
    return pl.pallas_call(
        _router_dispatch_kernel,
        grid=(t // tm,),
        in_specs=[pl.BlockSpec((tm, w), lambda i: (i, 0)), _full((N_EXPERTS, 1))],
        out_specs=[pl.BlockSpec((tm, w), lambda i: (i, 0)),
                   pl.BlockSpec((TOP_K, tm), lambda i: (0, i)),
                   pl.BlockSpec((TOP_K, tm), lambda i: (0, i)),
                   _full((N_EXPERTS, LANES))],
        out_shape=[jax.ShapeDtypeStruct((t, w), F32), jax.ShapeDtypeStruct((TOP_K, t), jnp.int32),
                   jax.ShapeDtypeStruct((TOP_K, t), jnp.int32), jax.ShapeDtypeStruct((N_EXPERTS, LANES), jnp.int32)],
        scratch_shapes=[pltpu.VMEM((N_EXPERTS, LANES), F32)],
        compiler_params=_cp("arbitrary"),
        name="router_dispatch",
    )(logits, router_b.reshape(N_EXPERTS, 1).astype(F32))


def router_gates(logits, router_b):
    t, w = logits.shape
    tm = min(TOKEN_TILE, t)
    return pl.pallas_call(
        _router_kernel,
        grid=(t // tm,),
        in_specs=[pl.BlockSpec((tm, w), lambda i: (i, 0)), _full((N_EXPERTS, 1))],
        out_specs=pl.BlockSpec((tm, w), lambda i: (i, 0)),
        out_shape=jax.ShapeDtypeStruct((t, w), F32),
        compiler_params=_cp("parallel"),
        name="router_gates",
    )(logits, router_b.reshape(N_EXPERTS, 1).astype(F32))


def _moe_kernel(h_ref, gate_ref, x_ref, wg_ref, wu_ref, wd_ref, sg_ref, su_ref, sd_ref, gpost_ref, ga2_ref,
                o_ref, acc_s):
    e = pl.program_id(1)
    h = _mx(h_ref[...])

    @pl.when(e == 0)
    def _():
        hs = _silu(jnp.dot(h, sg_ref[...], preferred_element_type=F32)) * jnp.dot(h, su_ref[...], preferred_element_type=F32)
        acc_s[...] = jnp.dot(_mx(hs), sd_ref[...], preferred_element_type=F32)

    gates = gate_ref[...]
    lane = lax.broadcasted_iota(jnp.int32, gates.shape, 1)
    hid = []
    for j in range(MOE_EB):
        gcol = jnp.sum(jnp.where(lane == e * MOE_EB + j, gates, 0.0), axis=1, keepdims=True)
        g = jnp.dot(h, wg_ref[j], preferred_element_type=F32)
        u = jnp.dot(h, wu_ref[j], preferred_element_type=F32)
        hid.append(_mx(_silu(g) * u * gcol))
    wd = wd_ref[...].reshape(MOE_EB * D_EXPERT, -1)
    acc_s[...] += jnp.dot(jnp.concatenate(hid, axis=1), wd, preferred_element_type=F32)

    @pl.when(e == pl.num_programs(1) - 1)
    def _():
        o_ref[...] = x_ref[...] + ga2_ref[0] * _rms(acc_s[...], gpost_ref[...])


def moe_ffn(h, gates, x, wg, wu, wd, sg, su, sd, gpost, ga2, tiles_per_group):
    t, d = x.shape
    tm = min(MOE_TILE, t)
    ne, _, f = wg.shape
    row = lambda w: pl.BlockSpec((tm, w), lambda i, e: (i, 0))
    return pl.pallas_call(
        _moe_kernel,
        grid=(t // tm, ne // MOE_EB),
        in_specs=[row(d), row(gates.shape[1]), row(d),
                  pl.BlockSpec((MOE_EB, d, f), lambda i, e: (e, 0, 0)),
                  pl.BlockSpec((MOE_EB, d, f), lambda i, e: (e, 0, 0)),
                  pl.BlockSpec((MOE_EB, f, d), lambda i, e: (e, 0, 0)),
                  _full(sg.shape), _full(su.shape), _full(sd.shape), _full((1, d)),
                  pl.BlockSpec((1, 1, d), lambda i, e: (i // tiles_per_group, 0, 0))],
        out_specs=row(d),
        out_shape=jax.ShapeDtypeStruct((t, d), F32),
        scratch_shapes=[pltpu.VMEM((tm, d), F32)],
        compiler_params=_cp("parallel", "arbitrary"),
        name="moe_ffn",
    )(h, gates, x, wg, wu, wd, sg, su, sd, gpost, ga2)


def moe_plan(counts, ek, pk):
    t = ek.shape[1]
    n_blocks = (t * TOP_K + N_EXPERTS * (MOE_BLOCK - 1) + MOE_BLOCK - 1) // MOE_BLOCK
    cnt = counts[:, 0]
    padded = (cnt + MOE_BLOCK - 1) // MOE_BLOCK * MOE_BLOCK
    pad_end = jnp.cumsum(padded)
    off = pad_end - padded
    start = jnp.arange(n_blocks, dtype=jnp.int32) * MOE_BLOCK
    be = jnp.minimum(jnp.searchsorted(pad_end, start, side='right'), N_EXPERTS - 1).astype(jnp.int32)
    nv = jnp.clip(off[be] + cnt[be] - start, 0, MOE_BLOCK).astype(jnp.int32)
    dest = (off[ek] + pk).astype(jnp.int32)
    return dest, be, nv


def _row_copy(src_ref, s, dst_ref, d, sem):
    return pltpu.make_async_copy(src_ref.at[pl.ds(s, 1), :], dst_ref.at[pl.ds(d, 1), :], sem)


def _dispatch_kernel(dest_ref, h_ref, xs_ref, sem):
    tm = h_ref.shape[0]

    def start(t, c):
        for k in range(TOP_K):
            _row_copy(h_ref, t, xs_ref, dest_ref[k, t], sem).start()
        return c

    def wait(t, c):
        for k in range(TOP_K):
            _row_copy(h_ref, t, xs_ref, dest_ref[k, t], sem).wait()
        return c

    lax.fori_loop(0, tm, start, 0)
    lax.fori_loop(0, tm, wait, 0)


def moe_dispatch(h, dest, n_rows):
    t, d = h.shape
    tm = min(MOE_ROW_TILE, t)
    return pl.pallas_call(
        _dispatch_kernel,
        grid=(t // tm,),
        in_specs=[pl.BlockSpec((TOP_K, tm), lambda i: (0, i), memory_space=pltpu.SMEM),
                  pl.BlockSpec((tm, d), lambda i: (i, 0))],
        out_specs=pl.BlockSpec(memory_space=pl.ANY),
        out_shape=jax.ShapeDtypeStruct((n_rows, d), h.dtype),
        scratch_shapes=[pltpu.SemaphoreType.DMA],
        compiler_params=_cp("arbitrary"),
        name="moe_dispatch",
    )(dest, h)


def _expert_kernel(be_ref, nv_ref, xs_ref, wg_ref, wu_ref, wd_ref, ys_ref):
    nv = nv_ref[pl.program_id(0)]

    @pl.when(nv > 0)
    def _():
        rows = lax.broadcasted_iota(jnp.int32, xs_ref.shape, 0)
        x = _mx(jnp.where(rows < nv, xs_ref[...], 0.0))
        hid = _silu(jnp.dot(x, wg_ref[0], preferred_element_type=F32)) * jnp.dot(x, wu_ref[0], preferred_element_type=F32)
        ys_ref[...] = jnp.dot(_mx(hid), wd_ref[0], preferred_element_type=F32)

    @pl.when(nv == 0)
    def _():
        ys_ref[...] = jnp.zeros(ys_ref.shape, F32)


def moe_experts(xs, be, nv, wg, wu, wd):
    n_rows, d = xs.shape
    f = wg.shape[2]
    return pl.pallas_call(
        _expert_kernel,
        grid_spec=pltpu.PrefetchScalarGridSpec(
            num_scalar_prefetch=2,
            grid=(n_rows // MOE_BLOCK,),
            in_specs=[pl.BlockSpec((MOE_BLOCK, d), lambda b, be, nv: (b, 0)),
                      pl.BlockSpec((1, d, f), lambda b, be, nv: (be[b], 0, 0)),
                      pl.BlockSpec((1, d, f), lambda b, be, nv: (be[b], 0, 0)),
                      pl.BlockSpec((1, f, d), lambda b, be, nv: (be[b], 0, 0))],
            out_specs=pl.BlockSpec((MOE_BLOCK, d), lambda b, be, nv: (b, 0))),
        out_shape=jax.ShapeDtypeStruct((n_rows, d), F32),
        compiler_params=_cp("arbitrary"),
        name="moe_experts",
    )(be, nv, xs, wg, wu, wd)


def _combine_kernel(dest_ref, gk_ref, h_ref, x_ref, sg_ref, su_ref, sd_ref, gpost_ref, ga2_ref, ys_ref,
                    o_ref, yg_s, sem):
    tm = h_ref.shape[0]

    def start(t, c):
        for k in range(TOP_K):
            _row_copy(ys_ref, dest_ref[k, t], yg_s.at[k], t, sem).start()
        return c

    def wait(t, c):
        for k in range(TOP_K):
            _row_copy(ys_ref, dest_ref[k, t], yg_s.at[k], t, sem).wait()
        return c

    lax.fori_loop(0, tm, start, 0)
    h = _mx(h_ref[...])
    hs = _silu(jnp.dot(h, sg_ref[...], preferred_element_type=F32)) * jnp.dot(h, su_ref[...], preferred_element_type=F32)
    f = jnp.dot(_mx(hs), sd_ref[...], preferred_element_type=F32)
    lax.fori_loop(0, tm, wait, 0)
    gk = gk_ref[...]
    for k in range(TOP_K):
        f = f + gk[:, k:k + 1] * yg_s[k]
    o_ref[...] = x_ref[...] + ga2_ref[0] * _rms(f, gpost_ref[...])


def moe_combine(ys, dest, gk, h, x, sg, su, sd, gpost, ga2, tiles_per_group):
    t, d = x.shape
    tm = min(MOE_ROW_TILE, t)
    row = lambda w: pl.BlockSpec((tm, w), lambda i: (i, 0))
    return pl.pallas_call(
        _combine_kernel,
        grid=(t // tm,),
        in_specs=[pl.BlockSpec((TOP_K, tm), lambda i: (0, i), memory_space=pltpu.SMEM),
                  row(gk.shape[1]), row(d), row(d), _full(sg.shape), _full(su.shape), _full(sd.shape), _full((1, d)),
                  pl.BlockSpec((1, 1, d), lambda i: (i // tiles_per_group, 0, 0)),
                  pl.BlockSpec(memory_space=pl.ANY)],
        out_specs=row(d),
        out_shape=jax.ShapeDtypeStruct((t, d), F32),
        scratch_shapes=[pltpu.VMEM((TOP_K, tm, d), F32), pltpu.SemaphoreType.DMA],
        compiler_params=_cp("arbitrary"),
        name="moe_combine",
    )(dest, gk, h, x, sg, su, sd, gpost, ga2, ys)


def _reorder_w_in(w_in):
    c = np.cumsum((0,) + (GROUP_W, GROUP_W, GROUP_W, GROUP_W, GROUP_W, GROUP_W, 2 * SSD_STATE, 2 * SSD_STATE,
                          GROUP_W, 2 * N_HEADS, GROUP_W, GROUP_W, GROUP_W, GROUP_W, 2 * N_HEADS, 2 * N_HEADS))
    seg = lambda a, b: w_in[:, c[a]:c[b]]
    small = jnp.concatenate([seg(9, 10), seg(14, 15), seg(15, 16),
                             jnp.zeros((w_in.shape[0], LANES - 6 * N_HEADS), w_in.dtype)], axis=1)
    return jnp.concatenate([seg(0, 5), seg(5, 8), seg(8, 9), seg(10, 13), seg(13, 14), small], axis=1)


def kernel(x, c, ctx, c_ctx, w_mod, b_mod, g_pre_mix, g_post_mix, g_pre_ffn, g_post_ffn, w_in, w_out, lru_conv_w, lru_conv_b, lru_wa, lru_ba, lru_wx, lru_bx, lru_lambda, na_bias, ssd_conv_w, ssd_conv_b, ssd_a_log, ssd_dt_bias, ssd_d, ssd_norm, gdn_conv_w, gdn_a_log, gdn_dt_bias, gdn_norm, router_w, router_b, we_gate, we_up, we_down, ws_gate, ws_up, ws_down):
    bsz, seq, d = x.shape
    n_ctx = ctx.shape[1]
    depth = w_mod.shape[0]
    lat_tpg = seq // min(TOKEN_TILE, seq)
    lat_mpg = seq // min(MOE_TILE, seq)
    ctx_tpg = max(bsz * n_ctx // TOKEN_TILE, 1)
    ctx_mpg = max(bsz * n_ctx // MOE_TILE, 1)

    cond = _pad_rows(jnp.concatenate([c, c_ctx[None, :]], axis=0))
    mod = modulation(cond, w_mod, b_mod).reshape(depth, SUBLANES, N_MOD, d)
    rope = rope_tables(seq)
    row = lambda v: v[None, :].astype(F32)

    xl = x.reshape(bsz * seq, d)
    xc = ctx.reshape(bsz * n_ctx, d)
    for l in range(depth):
        last = l == depth - 1
        m_lat = [mod[l, :bsz, k][:, None, :] for k in range(N_MOD)]
        m_ctx = [mod[l, bsz:bsz + 1, k][:, None, :] for k in range(N_MOD)]
        w_in_l = _reorder_w_in(w_in[l]).astype(MXU_DTYPE)
        pc = in_projection(xc, row(g_pre_mix[l]), m_ctx[1], m_ctx[0], w_in_l, ctx_tpg)
        pl_ = in_projection(xl, row(g_pre_mix[l]), m_lat[1], m_lat[0], w_in_l, lat_tpg)

        lru_p = lru_params(lru_conv_w[l], lru_conv_b[l], lru_wa[l], lru_ba[l], lru_wx[l], lru_bx[l], lru_lambda[l])
        a_cf, a_cb, a_st = lru_mixer(pc[P_AX], jnp.zeros((bsz, SUBLANES, GROUP_W), F32), bsz, *lru_p)
        a_lf, a_lb, _ = lru_mixer(pl_[P_AX], a_st, bsz, *lru_p)

        kc = pc[P_BK].reshape(bsz, n_ctx, GROUP_W)
        vc = pc[P_BV].reshape(bsz, n_ctx, GROUP_W)
        b_c = ctx_attention(pc[P_BQ].reshape(bsz, n_ctx, GROUP_W), kc, vc).reshape(bsz * n_ctx, GROUP_W)
        b_l = na_mixer(pl_[P_BQ], pl_[P_BK], pl_[P_BV], kc, vc, na_bias_slabs(na_bias[l]), bsz)

        ssd_p = ssd_params(ssd_conv_w[l], ssd_conv_b[l], ssd_a_log[l], ssd_dt_bias[l])
        c_cf, c_cb, c_cx, c_st = ssd_mixer(pc[P_CX], pc[P_SM], jnp.zeros((bsz, 2, N_HEADS, SSD_STATE, HEAD_DIM), F32),
                                           bsz, *ssd_p)
        c_lf, c_lb, c_lx, _ = ssd_mixer(pl_[P_CX], pl_[P_SM], c_st, bsz, *ssd_p)

        gdn_p = gdn_params(gdn_conv_w[l], gdn_a_log[l], gdn_dt_bias[l])
        d_cf, d_cb, d_st = gdn_mixer(pc[P_DX], pc[P_SM], jnp.zeros((bsz, 2, N_HEADS, HEAD_DIM, HEAD_DIM), F32),
                                     bsz, *gdn_p)
        d_lf, d_lb, _ = gdn_mixer(pl_[P_DX], pl_[P_SM], d_st, bsz, *gdn_p, rope=rope)

        epi = (w_out[l].astype(MXU_DTYPE), row(g_post_mix[l]))
        epi_tail = (row(jnp.repeat(ssd_d[l], HEAD_DIM)), row(ssd_norm[l]), row(jnp.tile(gdn_norm[l], N_HEADS)), router_w[l])
        moe_w = (we_gate[l].astype(MXU_DTYPE), we_up[l].astype(MXU_DTYPE), we_down[l].astype(MXU_DTYPE),
                 ws_gate[l].astype(MXU_DTYPE), ws_up[l].astype(MXU_DTYPE), ws_down[l].astype(MXU_DTYPE),
                 row(g_post_ffn[l]))

        mix_l = (a_lf, a_lb, pl_[P_AG], b_l, c_lf, c_lb, c_lx, pl_[P_CZ], d_lf, d_lb, pl_[P_DZ])
        xl, h2, lg = out_projection(xl, mix_l, *epi, m_lat[2], row(g_pre_ffn[l]), m_lat[4], m_lat[3], *epi_tail, lat_tpg)
        gk, ek, pk, cnt = router_dispatch(lg, router_b[l])
        dest, be, nv = moe_plan(cnt, ek, pk)
        xs = moe_dispatch(h2, dest, be.shape[0] * MOE_BLOCK)
        ys = moe_experts(xs, be, nv, *moe_w[:3])
        xl = moe_combine(ys, dest, gk, h2, xl, *moe_w[3:], m_lat[5], seq // min(MOE_ROW_TILE, seq))
        if not last:
            mix_c = (a_cf, a_cb, pc[P_AG], b_c, c_cf, c_cb, c_cx, pc[P_CZ], d_cf, d_cb, pc[P_DZ])
            xc, h2, lg = out_projection(xc, mix_c, *epi, m_ctx[2], row(g_pre_ffn[l]), m_ctx[4], m_ctx[3], *epi_tail, ctx_tpg)
            xc = moe_ffn(h2, router_gates(lg, router_b[l]), xc, *moe_w, m_ctx[5], ctx_mpg)
    return xl.reshape(bsz, seq, d)
```

```python
import functools
import math

import jax
import jax.numpy as jnp
import numpy as np
from jax import lax
from jax.experimental import pallas as pl
from jax.experimental.pallas import tpu as pltpu

F32 = jnp.float32
MXU_DTYPE = jnp.bfloat16
HI = lax.Precision.HIGHEST

D_MODEL = 1024
GRID_W = 64
GROUP_W = 256
HEAD_DIM = 64
N_HEADS = 4
EPS = 1e-6
NEG_INF = -1e30
N_MOD = 6
LRU_C = 8.0
NA_WIN_ROWS = 8
NA_WIN_COLS = 16
SSD_STATE = 128
SSD_GROUPS = 2
ROPE_BASE = 10000.0
ROPE_AXIS_DIM = HEAD_DIM // 2
N_EXPERTS = 64
N_EXPERT_GROUPS = 8
TOPK_GROUPS = 4
TOP_K = 8
D_EXPERT = 256
ROUTED_SCALE = 2.5

LANES = 128
SUBLANES = 8
VMEM_LIMIT = 56 * 1024 * 1024

TOKEN_TILE = 512
LRU_CHUNK = 256
SSD_CHUNK = 128
GDN_CHUNK = 64
GDN_TILE = 256
GDN_SUB = 128
GDN_BASE = 16
MOE_TILE = 1024
MOE_EB = 4
MOE_BLOCK = 512
MOE_ROW_TILE = 256
MOE_PLAN_TILE = 2048
DMA_PRIORITIES = 2

P_WIDTHS = (256, 256, 256, 256, 256, 768, 256, 768, 256, 128)
(P_AX, P_AG, P_BQ, P_BK, P_BV, P_CX, P_CZ, P_DX, P_DZ, P_SM) = range(10)
SM_DT, SM_BETA, SM_DECAY = 0, 8, 16


def _cp(*sem):
    return pltpu.CompilerParams(dimension_semantics=sem, vmem_limit_bytes=VMEM_LIMIT)


def _mx(x):
    return x.astype(MXU_DTYPE)


def _dot(a, b):
    return jnp.dot(_mx(a), _mx(b), preferred_element_type=F32)


def _dot_nt(a, b):
    return lax.dot_general(_mx(a), _mx(b), (((1,), (1,)), ((), ())), preferred_element_type=F32)


def _dot_tn(a, b):
    return lax.dot_general(_mx(a), _mx(b), (((0,), (0,)), ((), ())), preferred_element_type=F32)


def _dot_hi(a, b):
    return jnp.dot(a, b, preferred_element_type=F32, precision=HI)


def _sigmoid(x):
    return 1.0 / (1.0 + jnp.exp(-x))


def _silu(x):
    return x * _sigmoid(x)


def _softplus(x):
    return jnp.maximum(x, 0.0) + jnp.log1p(jnp.exp(-jnp.abs(x)))


def _gelu_tanh(x):
    return 0.5 * x * (1.0 + jnp.tanh(math.sqrt(2.0 / math.pi) * (x + 0.044715 * (x * x * x))))


def _rms(x, g):
    return x * lax.rsqrt(jnp.mean(x * x, axis=-1, keepdims=True) + EPS) * g


def _full(shape):
    n = len(shape)
    return pl.BlockSpec(shape, lambda *_: (0,) * n)


MOD_COLS = 1536


def _mod_kernel(c_ref, w_ref, b_ref, o_ref):
    o_ref[0] = _dot_hi(_silu(c_ref[...]), w_ref[0]) + b_ref[0]


def modulation(cond, w_mod, b_mod):
    depth, d, n = w_mod.shape
    return pl.pallas_call(
        _mod_kernel,
        grid=(depth, n // MOD_COLS),
        in_specs=[pl.BlockSpec((SUBLANES, d), lambda l, j: (0, 0)),
                  pl.BlockSpec((1, d, MOD_COLS), lambda l, j: (l, 0, j)),
                  pl.BlockSpec((1, 1, MOD_COLS), lambda l, j: (l, 0, j))],
        out_specs=pl.BlockSpec((1, SUBLANES, MOD_COLS), lambda l, j: (l, 0, j)),
        out_shape=jax.ShapeDtypeStruct((depth, SUBLANES, n), F32),
        compiler_params=_cp("parallel", "parallel"),
        name="modulation",
    )(cond, w_mod, b_mod.reshape(depth, 1, n))


def _inproj_kernel(x_ref, g_ref, sc_ref, sh_ref, w_ref, *o_refs):
    h = _rms(x_ref[...], g_ref[...]) * (1.0 + sc_ref[0]) + sh_ref[0]
    p = _dot(h, w_ref[...])
    off = 0
    for o_ref, w in zip(o_refs, P_WIDTHS):
        o_ref[...] = p[:, off:off + w]
        off += w


def in_projection(x, g, sc, sh, w, tiles_per_group):
    t, d = x.shape
    tm = min(TOKEN_TILE, t)
    vec = lambda i: (i // tiles_per_group, 0, 0)
    return pl.pallas_call(
        _inproj_kernel,
        grid=(t // tm,),
        in_specs=[pl.BlockSpec((tm, d), lambda i: (i, 0)),
                  _full((1, d)),
                  pl.BlockSpec((1, 1, d), vec),
                  pl.BlockSpec((1, 1, d), vec),
                  _full(w.shape)],
        out_specs=[pl.BlockSpec((tm, wd), lambda i: (i, 0)) for wd in P_WIDTHS],
        out_shape=[jax.ShapeDtypeStruct((t, wd), F32) for wd in P_WIDTHS],
        compiler_params=_cp("parallel"),
        name="in_projection",
    )(x, g, sc, sh, w)


def _halo(p, bsz, q):
    c = p.shape[-1]
    pr = p.reshape(bsz, -1, q, c)
    nc = pr.shape[1]
    prev = jnp.concatenate([jnp.zeros((bsz, 1, 2, c), p.dtype), pr[:, :-1, q - 2:, :]], axis=1)
    nxt = jnp.concatenate([pr[:, 1:, :1, :], jnp.zeros((bsz, 1, 1, c), p.dtype)], axis=1)
    pad = jnp.zeros((bsz, nc, SUBLANES - 3, c), p.dtype)
    return jnp.concatenate([prev, nxt, pad], axis=2).reshape(bsz * nc, SUBLANES, c)


def _dwconv(x, halo, w, b=None):
    q = x.shape[0]
    row = lax.broadcasted_iota(jnp.int32, x.shape, 0)
    xm2 = jnp.where(row == 0, halo[0:1], jnp.where(row == 1, halo[1:2], pltpu.roll(x, 2, 0)))
    xm1 = jnp.where(row == 0, halo[1:2], pltpu.roll(x, 1, 0))
    xp1 = jnp.where(row == q - 1, halo[2:3], pltpu.roll(x, q - 1, 0))
    y = w[0:1] * xm2 + w[1:2] * xm1 + w[2:3] * x + w[3:4] * xp1
    return y if b is None else y + b


def _pad_rows(a, rows=SUBLANES):
    return jnp.concatenate([a, jnp.zeros((rows - a.shape[0],) + a.shape[1:], a.dtype)], axis=0)


def _chunk_specs(nc, q, c):
    fwd = pl.BlockSpec((q, c), lambda b, i: (b * nc + i, 0))
    bwd = pl.BlockSpec((q, c), lambda b, i: (b * nc + nc - 1 - i, 0))
    return fwd, bwd


def _halo_specs(nc, c):
    fwd = pl.BlockSpec((1, SUBLANES, c), lambda b, i: (b * nc + i, 0, 0))
    bwd = pl.BlockSpec((1, SUBLANES, c), lambda b, i: (b * nc + nc - 1 - i, 0, 0))
    return fwd, bwd


def _lru_kernel(xf_ref, xb_ref, hf_ref, hb_ref, h0_ref, cw_ref, cb_ref, wg_ref, bg_ref, lam_ref,
                yf_ref, yb_ref, hfin_ref, af_s, bf_s, ab_s, bb_s, carry_s):
    i = pl.program_id(1)
    q = xf_ref.shape[0]

    @pl.when(i == 0)
    def _():
        carry_s[...] = h0_ref[0]

    def coeffs(x_ref, halo_ref, d, a_s, b_s):
        u = _dwconv(x_ref[...], halo_ref[0], cw_ref[...], cb_ref[...])
        g = _dot(u, wg_ref[:, 2 * GROUP_W * d:2 * GROUP_W * (d + 1)]) + bg_ref[:, 2 * GROUP_W * d:2 * GROUP_W * (d + 1)]
        r = _sigmoid(g[:, :GROUP_W])
        gate_in = _sigmoid(g[:, GROUP_W:])
        log_a = -LRU_C * r * _softplus(-lam_ref[d:d + 1, :])
        a_s[...] = jnp.exp(log_a)
        b_s[...] = jnp.sqrt(1.0 - jnp.exp(2.0 * log_a)) * (gate_in * u)

    coeffs(xf_ref, hf_ref, 0, af_s, bf_s)
    coeffs(xb_ref, hb_ref, 1, ab_s, bb_s)

    ng = q // SUBLANES
    row = lax.broadcasted_iota(jnp.int32, (SUBLANES, GROUP_W), 0)

    def body(g, hs):
        h_f, h_b = hs
        i0 = pl.multiple_of(g * SUBLANES, SUBLANES)
        a = af_s[pl.ds(i0, SUBLANES), :]
        b = bf_s[pl.ds(i0, SUBLANES), :]
        for s in (1, 2, 4):
            m = row >= s
            b = jnp.where(m, a * pltpu.roll(b, s, 0) + b, b)
            a = jnp.where(m, a * pltpu.roll(a, s, 0), a)
        h = b + a * h_f
        yf_ref[pl.ds(i0, SUBLANES), :] = h
        h_f = h[SUBLANES - 1:SUBLANES, :]
        j0 = pl.multiple_of((ng - 1 - g) * SUBLANES, SUBLANES)
        a = ab_s[pl.ds(j0, SUBLANES), :]
        b = bb_s[pl.ds(j0, SUBLANES), :]
        for s in (1, 2, 4):
            m = row < SUBLANES - s
            b = jnp.where(m, a * pltpu.roll(b, SUBLANES - s, 0) + b, b)
            a = jnp.where(m, a * pltpu.roll(a, SUBLANES - s, 0), a)
        h = b + a * h_b
        yb_ref[pl.ds(j0, SUBLANES), :] = h
        return h_f, h[0:1, :]

    h_f, h_b = lax.fori_loop(0, ng, body, (carry_s[0:1, :], carry_s[1:2, :]))
    carry_s[0:1, :] = h_f
    carry_s[1:2, :] = h_b

    @pl.when(i == pl.num_programs(1) - 1)
    def _():
        hfin_ref[0] = carry_s[...]


def _block_diag(w):
    h, a, b = w.shape
    return jnp.einsum('hij,hg->higj', w, jnp.eye(h, dtype=w.dtype)).reshape(h * a, h * b)


def lru_params(conv_w, conv_b, wa, ba, wx, bx, lam):
    wg = jnp.concatenate([_block_diag(wa[0]), _block_diag(wx[0]), _block_diag(wa[1]), _block_diag(wx[1])], axis=1)
    bg = jnp.concatenate([ba[0], bx[0], ba[1], bx[1]])[None, :]
    return _pad_rows(conv_w), conv_b[None, :], wg.astype(MXU_DTYPE), bg, _pad_rows(lam)


def lru_mixer(x, h0, bsz, cw, cb, wg, bg, lam):
    t, c = x.shape
    s = t // bsz
    q = min(LRU_CHUNK, s)
    nc = s // q
    halo = _halo(x, bsz, q)
    xf, xb = _chunk_specs(nc, q, c)
    hf, hb = _halo_specs(nc, c)
    st = pl.BlockSpec((1, SUBLANES, c), lambda b, i: (b, 0, 0))
    return pl.pallas_call(
        _lru_kernel,
        grid=(bsz, nc),
        in_specs=[xf, xb, hf, hb, st, _full(cw.shape), _full(cb.shape), _full(wg.shape), _full(bg.shape),
                  _full(lam.shape)],
        out_specs=[xf, xb, st],
        out_shape=[jax.ShapeDtypeStruct((t, c), F32), jax.ShapeDtypeStruct((t, c), F32),
                   jax.ShapeDtypeStruct((bsz, SUBLANES, c), F32)],
        scratch_shapes=[pltpu.VMEM((q, c), F32)] * 4 + [pltpu.VMEM((SUBLANES, c), F32)],
        compiler_params=_cp("parallel", "arbitrary"),
        name="lru_mixer",
    )(x, x, halo, halo, h0, cw, cb, wg, bg, lam)


NA_KEYS = NA_WIN_ROWS * GRID_W
NA_ROW_BLOCK = 4


def na_bias_slabs(table):
    qc = np.arange(GRID_W)[:, None]
    kc = np.arange(GRID_W)[None, :]
    win0 = np.clip(qc - NA_WIN_COLS // 2, 0, GRID_W - NA_WIN_COLS)
    ok = (kc >= win0) & (kc < win0 + NA_WIN_COLS)
    dc = np.clip(kc - qc + NA_WIN_COLS - 1, 0, 2 * NA_WIN_COLS - 2)
    dr = np.arange(NA_WIN_ROWS)[:, None] + np.arange(NA_WIN_ROWS)[None, :]
    b = table.astype(F32)[:, dr][:, :, :, dc]
    b = jnp.where(ok[None, None, None], b, NEG_INF)
    h = table.shape[0]
    return b.transpose(0, 1, 3, 2, 4).reshape(h, NA_WIN_ROWS, GRID_W, NA_KEYS)


def _na_span_start(j, rows):
    return jnp.clip(j * NA_ROW_BLOCK - NA_WIN_ROWS // 2, 0, rows - (NA_ROW_BLOCK + NA_WIN_ROWS - 1))


def _na_kernel(q_ref, kw_ref, vw_ref, kc_ref, vc_ref, slab_ref, o_ref, *, rows):
    j = pl.program_id(1)
    ustart = _na_span_start(j, rows)
    q = q_ref[...] * (HEAD_DIM ** -0.5)
    kc, vc = kc_ref[0], vc_ref[0]
    heads = [slice(h * HEAD_DIM, (h + 1) * HEAD_DIM) for h in range(N_HEADS)]
    qrows = [slice(i * GRID_W, (i + 1) * GRID_W) for i in range(NA_ROW_BLOCK)]
    kws, vws, offs = [], [], []
    for i in range(NA_ROW_BLOCK):
        r = j * NA_ROW_BLOCK + i
        r0 = jnp.clip(r - NA_WIN_ROWS // 2, 0, rows - NA_WIN_ROWS)
        start = pl.multiple_of((r0 - ustart) * GRID_W, GRID_W)
        kws.append(kw_ref[pl.ds(start, NA_KEYS), :])
        vws.append(vw_ref[pl.ds(start, NA_KEYS), :])
        offs.append(r0 - r + NA_WIN_ROWS - 1)
    s_ctx = [_dot_nt(q[:, sl], kc[:, sl]) for sl in heads]
    s_loc = [[_dot_nt(q[qr, sl], kws[i][:, sl]) + slab_ref[h, offs[i]] for h, sl in enumerate(heads)]
             for i, qr in enumerate(qrows)]
    m = [[jnp.maximum(jnp.max(s_loc[i][h], axis=-1, keepdims=True), jnp.max(s_ctx[h][qr], axis=-1, keepdims=True))
          for h in range(N_HEADS)] for i, qr in enumerate(qrows)]
    p_loc = [[jnp.exp(s_loc[i][h] - m[i][h]) for h in range(N_HEADS)] for i in range(NA_ROW_BLOCK)]
    p_ctx = [jnp.exp(s_ctx[h] - jnp.concatenate([m[i][h] for i in range(NA_ROW_BLOCK)], axis=0))
             for h in range(N_HEADS)]
    o_ctx = [_dot(p_ctx[h], vc[:, sl]) for h, sl in enumerate(heads)]
    rows_out = []
    for i, qr in enumerate(qrows):
        outs = []
        for h, sl in enumerate(heads):
            den = jnp.sum(p_loc[i][h], axis=-1, keepdims=True) + jnp.sum(p_ctx[h][qr], axis=-1, keepdims=True)
            outs.append((_dot(p_loc[i][h], vws[i][:, sl]) + o_ctx[h][qr]) / den)
        rows_out.append(jnp.concatenate(outs, axis=1))
    o_ref[...] = jnp.concatenate(rows_out, axis=0)


def na_mixer(q, k, v, kc, vc, slabs, bsz):
    t, c = q.shape
    s = t // bsz
    rows = s // GRID_W
    n_ctx = kc.shape[1]
    span = (NA_ROW_BLOCK + NA_WIN_ROWS - 1) * GRID_W

    def win(b, j):
        return ((b * rows + _na_span_start(j, rows)) * GRID_W, 0)

    wspec = pl.BlockSpec((pl.Element(span), pl.Element(c)), win)
    cspec = pl.BlockSpec((1, n_ctx, c), lambda b, j: (b, 0, 0))
    qspec = pl.BlockSpec((NA_ROW_BLOCK * GRID_W, c), lambda b, j: (b * (rows // NA_ROW_BLOCK) + j, 0))
    return pl.pallas_call(
        functools.partial(_na_kernel, rows=rows),
        grid=(bsz, rows // NA_ROW_BLOCK),
        in_specs=[qspec, wspec, wspec, cspec, cspec, _full(slabs.shape)],
        out_specs=qspec,
        out_shape=jax.ShapeDtypeStruct((t, c), F32),
        compiler_params=_cp("parallel", "arbitrary"),
        name="na_mixer",
    )(q, k, v, kc, vc, slabs)


def _ctx_attn_kernel(q_ref, k_ref, v_ref, o_ref):
    q = q_ref[0] * (HEAD_DIM ** -0.5)
    k, v = k_ref[0], v_ref[0]
    outs = []
    for h in range(N_HEADS):
        sl = slice(h * HEAD_DIM, (h + 1) * HEAD_DIM)
        s = _dot_nt(q[:, sl], k[:, sl])
        p = jnp.exp(s - jnp.max(s, axis=-1, keepdims=True))
        outs.append(_dot(p, v[:, sl]) / jnp.sum(p, axis=-1, keepdims=True))
    o_ref[0] = jnp.concatenate(outs, axis=1)


def ctx_attention(q, k, v):
    spec = pl.BlockSpec((1,) + q.shape[1:], lambda b: (b, 0, 0))
    return pl.pallas_call(
        _ctx_attn_kernel,
        grid=(q.shape[0],),
        in_specs=[spec, spec, spec],
        out_specs=spec,
        out_shape=jax.ShapeDtypeStruct(q.shape, F32),
        compiler_params=_cp("parallel"),
        name="ctx_attention",
    )(q, k, v)


def _small_vec(vals, off):
    v = jnp.zeros((LANES,), F32).at[off:off + 2 * N_HEADS].set(vals.reshape(-1).astype(F32))
    return v[None, :]


def _lane_mask(off):
    lane = lax.broadcasted_iota(jnp.int32, (1, LANES), 1)
    return (lane >= off) & (lane < off + 2 * N_HEADS)


def _tri_masks(q):
    rowi = lax.broadcasted_iota(jnp.int32, (q, q), 0)
    coli = lax.broadcasted_iota(jnp.int32, (q, q), 1)
    return rowi, coli


def _ssd_kernel(xf_ref, xb_ref, hf_ref, hb_ref, sf_ref, sb_ref, h0_ref, cw_ref, cb_ref, dtb_ref, alog_ref,
                yf_ref, yb_ref, xc_ref, hfin_ref, state_s):
    i = pl.program_id(1)
    q = xf_ref.shape[0]

    @pl.when(i == 0)
    def _():
        state_s[...] = h0_ref[0]

    rowi, coli = _tri_masks(q)
    a_neg = jnp.where(_lane_mask(SM_DT), -jnp.exp(alog_ref[...]), 0.0)

    def direction(x_ref, halo_ref, sm_ref, d, y_ref):
        xbc = _silu(_dwconv(x_ref[...], halo_ref[0], cw_ref[...], cb_ref[...]))
        if d == 0:
            xc_ref[...] = xbc[:, :GROUP_W]
        dt = _softplus(sm_ref[...] + dtb_ref[...])
        keep = (rowi >= coli) if d == 0 else (rowi <= coli)
        acum = _dot_hi(keep.astype(F32), dt * a_neg)
        acum_t = acum.T
        last = acum[q - 1:q, :] if d == 0 else acum[0:1, :]
        dec_end = jnp.exp(last - acum)
        e_acum = jnp.exp(acum)
        e_last = jnp.exp(last)
        ys = []
        for g in range(SSD_GROUPS):
            bg = xbc[:, GROUP_W + SSD_STATE * g:GROUP_W + SSD_STATE * (g + 1)]
            cg = xbc[:, GROUP_W + SSD_STATE * (SSD_GROUPS + g):GROUP_W + SSD_STATE * (SSD_GROUPS + g + 1)]
            cbt = _dot_nt(cg, bg)
            for hh in range(N_HEADS // SSD_GROUPS):
                h = g * (N_HEADS // SSD_GROUPS) + hh
                ln = SM_DT + N_HEADS * d + h
                lmat = jnp.exp(jnp.where(keep, acum[:, ln:ln + 1] - acum_t[ln:ln + 1, :], NEG_INF))
                xdt = xbc[:, h * HEAD_DIM:(h + 1) * HEAD_DIM] * dt[:, ln:ln + 1]
                st = state_s[d, h]
                ys.append(_dot(cbt * lmat, xdt) + _dot(cg * e_acum[:, ln:ln + 1], st))
                state_s[d, h] = st * e_last[:, ln:ln + 1] + _dot_tn(bg * dec_end[:, ln:ln + 1], xdt)
        y_ref[...] = jnp.concatenate(ys, axis=1)

    direction(xf_ref, hf_ref, sf_ref, 0, yf_ref)
    direction(xb_ref, hb_ref, sb_ref, 1, yb_ref)

    @pl.when(i == pl.num_programs(1) - 1)
    def _():
        hfin_ref[0] = state_s[...]


def ssd_params(conv_w, conv_b, a_log, dt_bias):
    return _pad_rows(conv_w), conv_b[None, :], _small_vec(dt_bias, SM_DT), _small_vec(a_log, SM_DT)


def ssd_mixer(xbc, sm, h0, bsz, cw, cb, dtb, alog):
    t, c = xbc.shape
    s = t // bsz
    q = min(SSD_CHUNK, s)
    nc = s // q
    halo = _halo(xbc, bsz, q)
    xf, xb = _chunk_specs(nc, q, c)
    hf, hb = _halo_specs(nc, c)
    sf, sb = _chunk_specs(nc, q, LANES)
    yf, yb = _chunk_specs(nc, q, GROUP_W)
    st = pl.BlockSpec((1,) + h0.shape[1:], lambda b, i: (b, 0, 0, 0, 0))
    y_shape = jax.ShapeDtypeStruct((t, GROUP_W), F32)
    return pl.pallas_call(
        _ssd_kernel,
        grid=(bsz, nc),
        in_specs=[xf, xb, hf, hb, sf, sb, st, _full(cw.shape), _full(cb.shape), _full(dtb.shape), _full(alog.shape)],
        out_specs=[yf, yb, yf, st],
        out_shape=[y_shape, y_shape, y_shape, jax.ShapeDtypeStruct(h0.shape, F32)],
        scratch_shapes=[pltpu.VMEM(h0.shape[1:], F32)],
        compiler_params=_cp("parallel", "arbitrary"),
        name="ssd_mixer",
    )(xbc, xbc, halo, halo, sm, sm, h0, cw, cb, dtb, alog)


def rope_tables(seq):
    t = jnp.arange(seq)
    row = (t // GRID_W).astype(F32)
    col = (t % GRID_W).astype(F32)
    inv = ROPE_BASE ** (-jnp.arange(0, ROPE_AXIS_DIM, 2, dtype=F32) / ROPE_AXIS_DIM)
    ar, ac = row[:, None] * inv, col[:, None] * inv
    cos = jnp.concatenate([jnp.cos(ar), jnp.cos(ar), jnp.cos(ac), jnp.cos(ac)], axis=1)
    sin = jnp.concatenate([-jnp.sin(ar), jnp.sin(ar), -jnp.sin(ac), jnp.sin(ac)], axis=1)
    return jnp.tile(cos, (1, N_HEADS)), jnp.tile(sin, (1, N_HEADS))


def _swap16(x):
    lane = lax.broadcasted_iota(jnp.int32, x.shape, 1)
    half = ROPE_AXIS_DIM // 2
    return jnp.where((lane & (ROPE_AXIS_DIM - 1)) < half,
                     pltpu.roll(x, x.shape[1] - half, 1), pltpu.roll(x, half, 1))


def _l2norm_heads(x):
    outs = []
    for h in range(N_HEADS):
        xh = x[:, h * HEAD_DIM:(h + 1) * HEAD_DIM]
        outs.append(xh * lax.rsqrt(jnp.sum(xh * xh, axis=-1, keepdims=True) + EPS))
    return jnp.concatenate(outs, axis=1)


def _dot_tri(mask, x):
    m = _mx(mask.astype(F32))
    x1 = _mx(x)
    r1 = x - x1.astype(F32)
    x2 = _mx(r1)
    x3 = _mx(r1 - x2.astype(F32))
    return (jnp.dot(m, x1, preferred_element_type=F32) + jnp.dot(m, x2, preferred_element_type=F32)
            + jnp.dot(m, x3, preferred_element_type=F32))


def _same_block(rowi, coli, n):
    sh = n.bit_length() - 1
    return (rowi >> sh) == (coli >> sh)


def _solve_unit_tri(a_list, rhs_list, rowi, coli, chunk):
    mm = lambda x, y: jnp.dot(x, y, preferred_element_type=F32)
    eye = (rowi == coli).astype(F32)
    in_base = _same_block(rowi, coli, GDN_BASE)
    base = [_mx(jnp.where(in_base, a, 0.0)) for a in a_list]
    ts = [jnp.where(in_base, eye - a, 0.0) for a in a_list]
    ps = [_mx(mm(b, b)) for b in base]
    ts = [t + mm(_mx(t), p) for t, p in zip(ts, ps)]
    n = 4
    while n < GDN_BASE:
        ps = [_mx(mm(p, p)) for p in ps]
        ts = [t + mm(_mx(t), p) for t, p in zip(ts, ps)]
        n *= 2
    n = GDN_BASE
    while 2 * n < chunk:
        inner = _same_block(rowi, coli, 2 * n) & jnp.logical_not(_same_block(rowi, coli, n))
        offs = [_mx(jnp.where(inner, a, 0.0)) for a in a_list]
        tb = [_mx(t) for t in ts]
        ms = [_mx(mm(t, off)) for t, off in zip(tb, offs)]
        ts = [t - mm(m, t_b) for t, m, t_b in zip(ts, ms, tb)]
        n *= 2
    outer = jnp.logical_not(_same_block(rowi, coli, n))
    offs = [_mx(jnp.where(outer, a, 0.0)) for a in a_list]
    tb = [_mx(t) for t in ts]
    ys = [mm(t, _mx(r)) for t, r in zip(tb, rhs_list)]
    zs = [_mx(mm(off, _mx(y))) for off, y in zip(offs, ys)]
    return [y - mm(t, z) for y, t, z in zip(ys, tb, zs)]


def _gdn_kernel(*refs, rope):
    if rope:
        (xf_ref, xb_ref, hf_ref, hb_ref, sf_ref, sb_ref, cf_ref, cb_ref, nf_ref, nb_ref,
         s0_ref, cw_ref, alog_ref, dtb_ref, of_ref, ob_ref, sfin_ref, state_s) = refs
    else:
        (xf_ref, xb_ref, hf_ref, hb_ref, sf_ref, sb_ref,
         s0_ref, cw_ref, alog_ref, dtb_ref, of_ref, ob_ref, sfin_ref, state_s) = refs
        cf_ref = cb_ref = nf_ref = nb_ref = None
    i = pl.program_id(1)
    tq = xf_ref.shape[0]
    ck = min(GDN_CHUNK, tq)
    nck = tq // ck

    @pl.when(i == 0)
    def _():
        state_s[...] = s0_ref[0]

    sub = min(GDN_SUB, tq)
    nsub = tq // sub
    rowt, colt = _tri_masks(tq)
    in_chunk_t = _same_block(rowt, colt, ck)
    rowi, coli = _tri_masks(sub)
    in_chunk = _same_block(rowi, coli, ck)
    a_neg = jnp.where(_lane_mask(SM_DECAY), -jnp.exp(alog_ref[...]), 0.0)

    a_list, rhs_list, qkm, qg, kd, e_last = [], [], [], [], [], []
    for d, (x_ref, halo_ref, sm_ref, cos_ref, sin_ref) in enumerate(
            ((xf_ref, hf_ref, sf_ref, cf_ref, nf_ref), (xb_ref, hb_ref, sb_ref, cb_ref, nb_ref))):
        qkv = _silu(_dwconv(x_ref[...], halo_ref[0], cw_ref[...]))
        qn = _l2norm_heads(qkv[:, :GROUP_W])
        kn = _l2norm_heads(qkv[:, GROUP_W:2 * GROUP_W])
        v = qkv[:, 2 * GROUP_W:]
        if rope:
            cos, sin = cos_ref[...], sin_ref[...]
            qn = qn * cos + _swap16(qn) * sin
            kn = kn * cos + _swap16(kn) * sin
        qn = qn * (HEAD_DIM ** -0.5)
        sm = sm_ref[...]
        beta = _sigmoid(sm)
        keep_t = in_chunk_t & ((rowt >= colt) if d == 0 else (rowt <= colt))
        keep = in_chunk & ((rowi >= coli) if d == 0 else (rowi <= coli))
        strict = in_chunk & ((rowi > coli) if d == 0 else (rowi < coli))
        gc = _dot_tri(keep_t, _softplus(sm + dtb_ref[...]) * a_neg)
        gc_t = gc.T
        edge = ck - 1 if d == 0 else 0
        last = jnp.concatenate([jnp.broadcast_to(gc[c * ck + edge:c * ck + edge + 1, :], (ck, LANES))
                                for c in range(nck)], axis=0)
        e_gc = jnp.exp(gc)
        e_end = jnp.exp(last - gc)
        e_last.append(jnp.exp(last))
        for h in range(N_HEADS):
            sl = slice(h * HEAD_DIM, (h + 1) * HEAD_DIM)
            lg = SM_DECAY + N_HEADS * d + h
            lb = SM_BETA + N_HEADS * d + h
            qh, kh, bcol = qn[:, sl], kn[:, sl], beta[:, lb:lb + 1]
            kb = kh * bcol
            rhs = jnp.concatenate([v[:, sl] * bcol, kb * e_gc[:, lg:lg + 1]], axis=1)
            qg.append(qh * e_gc[:, lg:lg + 1])
            kd.append(kh * e_end[:, lg:lg + 1])
            for s in range(nsub):
                rs = slice(s * sub, (s + 1) * sub)
                decay = jnp.exp(jnp.where(keep, gc[rs, lg:lg + 1] - gc_t[lg:lg + 1, rs], NEG_INF))
                a_list.append(jnp.where(strict, _dot_nt(kb[rs], kh[rs]) * decay, 0.0))
                rhs_list.append(rhs[rs])
                qkm.append(_dot_nt(qh[rs], kh[rs]) * decay)
    sols = _solve_unit_tri(a_list, rhs_list, rowi, coli, ck)
    sols = [jnp.concatenate(sols[n * nsub:(n + 1) * nsub], axis=0) for n in range(2 * N_HEADS)]

    chains = [(d, h) for d in range(2) for h in range(N_HEADS)]
    states = [state_s[d, h] for d, h in chains]
    v_new = [[None] * nck for _ in chains]
    o_st = [[None] * nck for _ in chains]
    for step in range(nck):
        rows = [slice((step if d == 0 else nck - 1 - step) * ck, (step if d == 0 else nck - 1 - step) * ck + ck)
                for d, _ in chains]
        ms = [_dot(jnp.concatenate([sols[n][r, HEAD_DIM:], qg[n][r]], axis=0), states[n])
              for n, r in enumerate(rows)]
        for n, (d, _) in enumerate(chains):
            c = step if d == 0 else nck - 1 - step
            v_new[n][c] = sols[n][rows[n], :HEAD_DIM] - ms[n][:ck]
            o_st[n][c] = ms[n][ck:]
        ups = [_dot_tn(kd[n][r], v_new[n][step if chains[n][0] == 0 else nck - 1 - step])
               for n, r in enumerate(rows)]
        for n, (d, h) in enumerate(chains):
            lg = SM_DECAY + N_HEADS * d + h
            states[n] = states[n] * e_last[d][rows[n].start:rows[n].start + 1, lg:lg + 1] + ups[n]
    cps = sub // ck
    outs = [jnp.concatenate(o_st[n], axis=0)
            + jnp.concatenate([_dot(qkm[n * nsub + s], jnp.concatenate(v_new[n][s * cps:(s + 1) * cps], axis=0))
                               for s in range(nsub)], axis=0)
            for n in range(len(chains))]
    of_ref[...] = jnp.concatenate(outs[:N_HEADS], axis=1)
    ob_ref[...] = jnp.concatenate(outs[N_HEADS:], axis=1)
    for n, (d, h) in enumerate(chains):
        state_s[d, h] = states[n]

    @pl.when(i == pl.num_programs(1) - 1)
    def _():
        sfin_ref[0] = state_s[...]


def gdn_params(conv_w, a_log, dt_bias):
    return _pad_rows(conv_w), _small_vec(a_log, SM_DECAY), _small_vec(dt_bias, SM_DECAY)


def gdn_mixer(qkv, sm, s0, bsz, cw, alog, dtb, rope=None):
    t, c = qkv.shape
    s = t // bsz
    q = min(GDN_TILE, s)
    nc = s // q
    halo = _halo(qkv, bsz, q)
    xf, xb = _chunk_specs(nc, q, c)
    hf, hb = _halo_specs(nc, c)
    sf, sb = _chunk_specs(nc, q, LANES)
    of, ob = _chunk_specs(nc, q, GROUP_W)
    st = pl.BlockSpec((1,) + s0.shape[1:], lambda b, i: (b, 0, 0, 0, 0))
    ins = [qkv, qkv, halo, halo, sm, sm]
    specs = [xf, xb, hf, hb, sf, sb]
    if rope is not None:
        tf = pl.BlockSpec((q, GROUP_W), lambda b, i: (i, 0))
        tb = pl.BlockSpec((q, GROUP_W), lambda b, i: (nc - 1 - i, 0))
        ins += [rope[0], rope[0], rope[1], rope[1]]
        specs += [tf, tb, tf, tb]
    ins += [s0, cw, alog, dtb]
    specs += [st, _full(cw.shape), _full(alog.shape), _full(dtb.shape)]
    o_shape = jax.ShapeDtypeStruct((t, GROUP_W), F32)
    return pl.pallas_call(
        functools.partial(_gdn_kernel, rope=rope is not None),
        grid=(bsz, nc),
        in_specs=specs,
        out_specs=[of, ob, st],
        out_shape=[o_shape, o_shape, jax.ShapeDtypeStruct(s0.shape, F32)],
        scratch_shapes=[pltpu.VMEM(s0.shape[1:], F32)],
        compiler_params=_cp("parallel", "arbitrary"),
        name="gdn_mixer",
    )(*ins)


def _split_hi_lo(a):
    hi = _mx(a)
    return hi, _mx(a - hi.astype(F32))


def _outproj_kernel(x_ref, ahf_ref, ahb_ref, ag_ref, bo_ref, cyf_ref, cyb_ref, cxc_ref, cz_ref,
                    dof_ref, dob_ref, dz_ref, wout_ref, gpost_ref, ga1_ref, gpre_ref, sc2_ref, sh2_ref,
                    dskip_ref, cnorm_ref, dnorm_ref, rhi_ref, rlo_ref, xo_ref, h2_ref, lg_ref):
    m_a = (ahf_ref[...] + ahb_ref[...]) * _gelu_tanh(ag_ref[...])
    y_c = (cyf_ref[...] + cyb_ref[...] + cxc_ref[...] * dskip_ref[...]) * _silu(cz_ref[...])
    m_c = _rms(y_c, cnorm_ref[...])
    o_d = dof_ref[...] + dob_ref[...]
    heads = []
    for h in range(N_HEADS):
        oh = o_d[:, h * HEAD_DIM:(h + 1) * HEAD_DIM]
        heads.append(oh * lax.rsqrt(jnp.mean(oh * oh, axis=-1, keepdims=True) + EPS))
    m_d = jnp.concatenate(heads, axis=1) * dnorm_ref[...] * _silu(dz_ref[...])
    mix = jnp.concatenate([_mx(m_a), _mx(bo_ref[...]), _mx(m_c), _mx(m_d)], axis=1)
    ml = jnp.dot(mix, wout_ref[...], preferred_element_type=F32)
    x_new = x_ref[...] + ga1_ref[0] * _rms(ml, gpost_ref[...])
    xo_ref[...] = x_new
    h2 = _rms(x_new, gpre_ref[...]) * (1.0 + sc2_ref[0]) + sh2_ref[0]
    hi, lo = _split_hi_lo(h2)
    h2_ref[...] = h2
    rhi = rhi_ref[...]
    lg_ref[...] = (jnp.dot(hi, rhi, preferred_element_type=F32) + jnp.dot(lo, rhi, preferred_element_type=F32)
                   + jnp.dot(hi, rlo_ref[...], preferred_element_type=F32))


def out_projection(x, mixers, w_out, gpost, ga1, gpre, sc2, sh2, dskip, cnorm, dnorm, router_w, tiles_per_group):
    t, d = x.shape
    tm = min(TOKEN_TILE, t)
    vec = lambda i: (i // tiles_per_group, 0, 0)
    row = lambda w: pl.BlockSpec((tm, w), lambda i: (i, 0))
    ne = LANES
    rhi, rlo = _split_hi_lo(jnp.pad(router_w.astype(F32), ((0, 0), (0, ne - router_w.shape[1]))))
    return pl.pallas_call(
        _outproj_kernel,
        grid=(t // tm,),
        in_specs=[row(d)] + [row(GROUP_W)] * 11
                 + [_full(w_out.shape), _full((1, d)), pl.BlockSpec((1, 1, d), vec), _full((1, d)),
                    pl.BlockSpec((1, 1, d), vec), pl.BlockSpec((1, 1, d), vec),
                    _full((1, GROUP_W)), _full((1, GROUP_W)), _full((1, GROUP_W)), _full(rhi.shape), _full(rlo.shape)],
        out_specs=[row(d), row(d), row(ne)],
        out_shape=[jax.ShapeDtypeStruct((t, d), F32), jax.ShapeDtypeStruct((t, d), F32),
                   jax.ShapeDtypeStruct((t, ne), F32)],
        compiler_params=_cp("parallel"),
        name="out_projection",
    )(x, *mixers, w_out, gpost, ga1, gpre, sc2, sh2, dskip, cnorm, dnorm, rhi, rlo)


def _rank_before(vals, idx, count, stride):
    rank = jnp.zeros(vals.shape, jnp.int32)
    for j in range(count):
        other = vals[j * stride:j * stride + 1, :]
        ahead = (other > vals) | ((other == vals) & (idx > j))
        rank = rank + ahead.astype(jnp.int32)
    return rank


def _xor_partner(x, row, s):
    n = x.shape[0]
    return jnp.where((row & s) == 0, pltpu.roll(x, n - s, 0), pltpu.roll(x, s, 0))


def _route(logits, router_b):
    ne = N_EXPERTS
    gsz = ne // N_EXPERT_GROUPS
    scores = _sigmoid(logits.T[:ne, :])
    tm = scores.shape[1]
    biased = scores + router_b
    row = lax.broadcasted_iota(jnp.int32, (ne, tm), 0)
    m1, m2 = biased, jnp.full((ne, tm), -jnp.inf, F32)
    s = 1
    while s < gsz:
        o1, o2 = _xor_partner(m1, row, s), _xor_partner(m2, row, s)
        m2 = jnp.maximum(jnp.minimum(m1, o1), jnp.maximum(m2, o2))
        m1 = jnp.maximum(m1, o1)
        s *= 2
    gidx = row >> (gsz.bit_length() - 1)
    group_ok = _rank_before(m1 + m2, gidx, N_EXPERT_GROUPS, gsz) < TOPK_GROUPS
    choice = jnp.where(group_ok, biased, -jnp.inf)
    rank = _rank_before(choice, row, ne, 1)
    gate = jnp.where(rank < TOP_K, scores, 0.0)
    gate = gate / jnp.sum(gate, axis=0, keepdims=True) * ROUTED_SCALE
    return gate, rank, row


def _to_token_major(x):
    n, tm = x.shape
    return jnp.concatenate([x, jnp.zeros((LANES - n, tm), x.dtype)], axis=0).T


def _router_kernel(lg_ref, rb_ref, gate_ref):
    gate, _, _ = _route(lg_ref[...], rb_ref[...])
    gate_ref[...] = _to_token_major(gate)


def _router_dispatch_kernel(lg_ref, rb_ref, gk_ref, ek_ref, pk_ref, cnt_ref, carry_s):
    i = pl.program_id(0)

    @pl.when(i == 0)
    def _():
        carry_s[...] = jnp.zeros(carry_s.shape, F32)

    gate, rank, row = _route(lg_ref[...], rb_ref[...])
    tm = gate.shape[1]
    picked = (rank < TOP_K).astype(F32)
    before = lax.broadcasted_iota(jnp.int32, (tm, tm), 0) < lax.broadcasted_iota(jnp.int32, (tm, tm), 1)
    pos = _dot(picked, before.astype(F32)) + carry_s[:, 0:1]
    carry_s[...] = carry_s[...] + jnp.sum(picked, axis=1, keepdims=True)
    gk, ek, pk = [], [], []
    for k in range(TOP_K):
        sel = rank == k
        gk.append(jnp.sum(jnp.where(sel, gate, 0.0), axis=0, keepdims=True))
        ek.append(jnp.sum(jnp.where(sel, row, 0), axis=0, keepdims=True))
        pk.append(jnp.sum(jnp.where(sel, pos, 0.0), axis=0, keepdims=True))
    gk_ref[...] = _to_token_major(jnp.concatenate(gk, axis=0))
    ek_ref[...] = jnp.concatenate(ek, axis=0)
    pk_ref[...] = jnp.concatenate(pk, axis=0).astype(jnp.int32)

    @pl.when(i == pl.num_programs(0) - 1)
    def _():
        cnt_ref[...] = carry_s[...].astype(jnp.int32)


def router_dispatch(logits, router_b):
    t, w = logits.shape
    tm = min(TOKEN_TILE, t)
    return pl.pallas_call(
        _router_dispatch_kernel,
        grid=(t // tm,),
        in_specs=[pl.BlockSpec((tm, w), lambda i: (i, 0)), _full((N_EXPERTS, 1))],
        out_specs=[pl.BlockSpec((tm, w), lambda i: (i, 0)),
                   pl.BlockSpec((TOP_K, tm), lambda i: (0, i)),
                   pl.BlockSpec((TOP_K, tm), lambda i: (0, i)),
                   _full((N_EXPERTS, LANES))],
        out_shape=[jax.ShapeDtypeStruct((t, w), F32), jax.ShapeDtypeStruct((TOP_K, t), jnp.int32),
                   jax.ShapeDtypeStruct((TOP_K, t), jnp.int32), jax.ShapeDtypeStruct((N_EXPERTS, LANES), jnp.int32)],
        scratch_shapes=[pltpu.VMEM((N_EXPERTS, LANES), F32)],
        compiler_params=_cp("arbitrary"),
        name="router_dispatch",
    )(logits, router_b.reshape(N_EXPERTS, 1).astype(F32))


def router_gates(logits, router_b):
    t, w = logits.shape
    tm = min(TOKEN_TILE, t)
    return pl.pallas_call(
        _router_kernel,
        grid=(t // tm,),
        in_specs=[pl.BlockSpec((tm, w), lambda i: (i, 0)), _full((N_EXPERTS, 1))],
        out_specs=pl.BlockSpec((tm, w), lambda i: (i, 0)),
        out_shape=jax.ShapeDtypeStruct((t, w), F32),
        compiler_params=_cp("parallel"),
        name="router_gates",
    )(logits, router_b.reshape(N_EXPERTS, 1).astype(F32))


def _moe_kernel(h_ref, gate_ref, x_ref, wg_ref, wu_ref, wd_ref, sg_ref, su_ref, sd_ref, gpost_ref, ga2_ref,
                o_ref, acc_s):
    e = pl.program_id(1)
    h = _mx(h_ref[...])

    @pl.when(e == 0)
    def _():
        hs = _silu(jnp.dot(h, sg_ref[...], preferred_element_type=F32)) * jnp.dot(h, su_ref[...], preferred_element_type=F32)
        acc_s[...] = jnp.dot(_mx(hs), sd_ref[...], preferred_element_type=F32)

    gates = gate_ref[...]
    lane = lax.broadcasted_iota(jnp.int32, gates.shape, 1)
    hid = []
    for j in range(MOE_EB):
        gcol = jnp.sum(jnp.where(lane == e * MOE_EB + j, gates, 0.0), axis=1, keepdims=True)
        g = jnp.dot(h, _mx(wg_ref[j]), preferred_element_type=F32)
        u = jnp.dot(h, _mx(wu_ref[j]), preferred_element_type=F32)
        hid.append(_mx(_silu(g) * u * gcol))
    wd = _mx(wd_ref[...]).reshape(MOE_EB * D_EXPERT, -1)
    acc_s[...] += jnp.dot(jnp.concatenate(hid, axis=1), wd, preferred_element_type=F32)

    @pl.when(e == pl.num_programs(1) - 1)
    def _():
        o_ref[...] = x_ref[...] + ga2_ref[0] * _rms(acc_s[...], gpost_ref[...])


def moe_ffn(h, gates, x, layer, wg, wu, wd, sg, su, sd, gpost, ga2, tiles_per_group):
    t, d = x.shape
    tm = min(MOE_TILE, t)
    _, ne, _, f = wg.shape
    row = lambda w: pl.BlockSpec((tm, w), lambda i, e: (i, 0))
    return pl.pallas_call(
        _moe_kernel,
        grid=(t // tm, ne // MOE_EB),
        in_specs=[row(d), row(gates.shape[1]), row(d),
                  pl.BlockSpec((None, MOE_EB, d, f), lambda i, e: (layer, e, 0, 0)),
                  pl.BlockSpec((None, MOE_EB, d, f), lambda i, e: (layer, e, 0, 0)),
                  pl.BlockSpec((None, MOE_EB, f, d), lambda i, e: (layer, e, 0, 0)),
                  _full(sg.shape), _full(su.shape), _full(sd.shape), _full((1, d)),
                  pl.BlockSpec((1, 1, d), lambda i, e: (i // tiles_per_group, 0, 0))],
        out_specs=row(d),
        out_shape=jax.ShapeDtypeStruct((t, d), F32),
        scratch_shapes=[pltpu.VMEM((tm, d), F32)],
        compiler_params=_cp("parallel", "arbitrary"),
        name="moe_ffn",
    )(h, gates, x, wg, wu, wd, sg, su, sd, gpost, ga2)


def moe_plan(counts, n_tokens):
    n_blocks = (n_tokens * TOP_K + N_EXPERTS * (MOE_BLOCK - 1) + MOE_BLOCK - 1) // MOE_BLOCK
    cnt = counts[:, 0]
    padded = (cnt + MOE_BLOCK - 1) // MOE_BLOCK * MOE_BLOCK
    pad_end = jnp.cumsum(padded)
    off = pad_end - padded
    start = jnp.arange(n_blocks, dtype=jnp.int32) * MOE_BLOCK
    be = jnp.minimum(jnp.sum(pad_end[None, :] <= start[:, None], axis=1), N_EXPERTS - 1).astype(jnp.int32)
    mine = be[:, None] == jnp.arange(N_EXPERTS, dtype=jnp.int32)[None, :]
    end = jnp.sum(jnp.where(mine, (off + cnt)[None, :], 0), axis=1)
    nv = jnp.clip(end - start, 0, MOE_BLOCK).astype(jnp.int32)
    return off.astype(jnp.int32), be, nv


def _rows_kernel(off_ref, ek_ref, pk_ref, dest_ref):
    ek = ek_ref[...]
    dest = pk_ref[...]
    for e in range(N_EXPERTS):
        dest = dest + jnp.where(ek == e, off_ref[e], 0)
    dest_ref[...] = dest


def moe_rows(off, ek, pk):
    k, t = ek.shape
    tm = min(MOE_PLAN_TILE, t)
    spec = pl.BlockSpec((k, tm), lambda i, off: (0, i))
    return pl.pallas_call(
        _rows_kernel,
        grid_spec=pltpu.PrefetchScalarGridSpec(num_scalar_prefetch=1, grid=(t // tm,),
                                               in_specs=[spec, spec], out_specs=spec),
        out_shape=jax.ShapeDtypeStruct((k, t), jnp.int32),
        compiler_params=_cp("arbitrary"),
        name="moe_rows",
    )(off, ek, pk)


def _row_copy(src_ref, s, dst_ref, d, sem):
    return pltpu.make_async_copy(src_ref.at[pl.ds(s, 1), :], dst_ref.at[pl.ds(d, 1), :], sem)


def _dispatch_kernel(dest_ref, h_ref, xs_ref, sem):
    tm = h_ref.shape[0]

    def start(t, c):
        for k in range(TOP_K):
            _row_copy(h_ref, t, xs_ref, dest_ref[k, t], sem).start(priority=k % DMA_PRIORITIES)
        return c

    def wait(t, c):
        for k in range(TOP_K):
            _row_copy(h_ref, t, xs_ref, dest_ref[k, t], sem).wait()
        return c

    lax.fori_loop(0, tm, start, 0)
    lax.fori_loop(0, tm, wait, 0)


def moe_dispatch(h, dest, n_rows):
    t, d = h.shape
    tm = min(MOE_ROW_TILE, t)
    return pl.pallas_call(
        _dispatch_kernel,
        grid=(t // tm,),
        in_specs=[pl.BlockSpec((TOP_K, tm), lambda i: (0, i), memory_space=pltpu.SMEM),
                  pl.BlockSpec((tm, d), lambda i: (i, 0))],
        out_specs=pl.BlockSpec(memory_space=pl.ANY),
        out_shape=jax.ShapeDtypeStruct((n_rows, d), h.dtype),
        scratch_shapes=[pltpu.SemaphoreType.DMA],
        compiler_params=_cp("arbitrary"),
        name="moe_dispatch",
    )(dest, h)


def _expert_kernel(be_ref, nv_ref, xs_ref, wg_ref, wu_ref, wd_ref, ys_ref):
    nv = nv_ref[pl.program_id(0)]

    @pl.when(nv > 0)
    def _():
        rows = lax.broadcasted_iota(jnp.int32, xs_ref.shape, 0)
        x = _mx(jnp.where(rows < nv, xs_ref[...], 0.0))
        hid = (_silu(jnp.dot(x, _mx(wg_ref[0]), preferred_element_type=F32))
               * jnp.dot(x, _mx(wu_ref[0]), preferred_element_type=F32))
        ys_ref[...] = jnp.dot(_mx(hid), _mx(wd_ref[0]), preferred_element_type=F32)

    @pl.when(nv == 0)
    def _():
        ys_ref[...] = jnp.zeros(ys_ref.shape, F32)


def moe_experts(xs, be, nv, layer, wg, wu, wd):
    n_rows, d = xs.shape
    f = wg.shape[3]
    return pl.pallas_call(
        _expert_kernel,
        grid_spec=pltpu.PrefetchScalarGridSpec(
            num_scalar_prefetch=2,
            grid=(n_rows // MOE_BLOCK,),
            in_specs=[pl.BlockSpec((MOE_BLOCK, d), lambda b, be, nv: (b, 0)),
                      pl.BlockSpec((None, 1, d, f), lambda b, be, nv: (layer, be[b], 0, 0)),
                      pl.BlockSpec((None, 1, d, f), lambda b, be, nv: (layer, be[b], 0, 0)),
                      pl.BlockSpec((None, 1, f, d), lambda b, be, nv: (layer, be[b], 0, 0))],
            out_specs=pl.BlockSpec((MOE_BLOCK, d), lambda b, be, nv: (b, 0))),
        out_shape=jax.ShapeDtypeStruct((n_rows, d), F32),
        compiler_params=_cp("arbitrary"),
        name="moe_experts",
    )(be, nv, xs, wg, wu, wd)


def _combine_kernel(dest_ref, gk_ref, h_ref, x_ref, sg_ref, su_ref, sd_ref, gpost_ref, ga2_ref, ys_ref,
                    o_ref, yg_s, sem):
    tm = h_ref.shape[0]

    def start(t, c):
        for k in range(TOP_K):
            _row_copy(ys_ref, dest_ref[k, t], yg_s.at[k], t, sem).start(priority=k % DMA_PRIORITIES)
        return c

    def wait(t, c):
        for k in range(TOP_K):
            _row_copy(ys_ref, dest_ref[k, t], yg_s.at[k], t, sem).wait()
        return c

    lax.fori_loop(0, tm, start, 0)
    h = _mx(h_ref[...])
    hs = _silu(jnp.dot(h, sg_ref[...], preferred_element_type=F32)) * jnp.dot(h, su_ref[...], preferred_element_type=F32)
    f = jnp.dot(_mx(hs), sd_ref[...], preferred_element_type=F32)
    lax.fori_loop(0, tm, wait, 0)
    gk = gk_ref[...]
    for k in range(TOP_K):
        f = f + gk[:, k:k + 1] * yg_s[k]
    o_ref[...] = x_ref[...] + ga2_ref[0] * _rms(f, gpost_ref[...])


def moe_combine(ys, dest, gk, h, x, sg, su, sd, gpost, ga2, tiles_per_group):
    t, d = x.shape
    tm = min(MOE_ROW_TILE, t)
    row = lambda w: pl.BlockSpec((tm, w), lambda i: (i, 0))
    return pl.pallas_call(
        _combine_kernel,
        grid=(t // tm,),
        in_specs=[pl.BlockSpec((TOP_K, tm), lambda i: (0, i), memory_space=pltpu.SMEM),
                  row(gk.shape[1]), row(d), row(d), _full(sg.shape), _full(su.shape), _full(sd.shape), _full((1, d)),
                  pl.BlockSpec((1, 1, d), lambda i: (i // tiles_per_group, 0, 0)),
                  pl.BlockSpec(memory_space=pl.ANY)],
        out_specs=row(d),
        out_shape=jax.ShapeDtypeStruct((t, d), F32),
        scratch_shapes=[pltpu.VMEM((TOP_K, tm, d), F32), pltpu.SemaphoreType.DMA],
        compiler_params=_cp("arbitrary"),
        name="moe_combine",
    )(dest, gk, h, x, sg, su, sd, gpost, ga2, ys)


def _reorder_w_in(w_in):
    c = np.cumsum((0,) + (GROUP_W, GROUP_W, GROUP_W, GROUP_W, GROUP_W, GROUP_W, 2 * SSD_STATE, 2 * SSD_STATE,
                          GROUP_W, 2 * N_HEADS, GROUP_W, GROUP_W, GROUP_W, GROUP_W, 2 * N_HEADS, 2 * N_HEADS))
    seg = lambda a, b: w_in[:, c[a]:c[b]]
    small = jnp.concatenate([seg(9, 10), seg(14, 15), seg(15, 16),
                             jnp.zeros((w_in.shape[0], LANES - 6 * N_HEADS), w_in.dtype)], axis=1)
    return jnp.concatenate([seg(0, 5), seg(5, 8), seg(8, 9), seg(10, 13), seg(13, 14), small], axis=1)


def kernel(x, c, ctx, c_ctx, w_mod, b_mod, g_pre_mix, g_post_mix, g_pre_ffn, g_post_ffn, w_in, w_out, lru_conv_w, lru_conv_b, lru_wa, lru_ba, lru_wx, lru_bx, lru_lambda, na_bias, ssd_conv_w, ssd_conv_b, ssd_a_log, ssd_dt_bias, ssd_d, ssd_norm, gdn_conv_w, gdn_a_log, gdn_dt_bias, gdn_norm, router_w, router_b, we_gate, we_up, we_down, ws_gate, ws_up, ws_down):
    bsz, seq, d = x.shape
    n_ctx = ctx.shape[1]
    depth = w_mod.shape[0]
    lat_tpg = seq // min(TOKEN_TILE, seq)
    ctx_tpg = max(bsz * n_ctx // TOKEN_TILE, 1)
    ctx_mpg = max(bsz * n_ctx // MOE_TILE, 1)

    cond = _pad_rows(jnp.concatenate([c, c_ctx[None, :]], axis=0))
    mod = modulation(cond, w_mod, b_mod).reshape(depth, SUBLANES, N_MOD, d)
    rope = rope_tables(seq)
    row = lambda v: v[None, :].astype(F32)

    xl = x.reshape(bsz * seq, d)
    xc = ctx.reshape(bsz * n_ctx, d)
    for l in range(depth):
        last = l == depth - 1
        m_lat = [mod[l, :bsz, k][:, None, :] for k in range(N_MOD)]
        m_ctx = [mod[l, bsz:bsz + 1, k][:, None, :] for k in range(N_MOD)]
        w_in_l = _reorder_w_in(w_in[l]).astype(MXU_DTYPE)
        pc = in_projection(xc, row(g_pre_mix[l]), m_ctx[1], m_ctx[0], w_in_l, ctx_tpg)
        pl_ = in_projection(xl, row(g_pre_mix[l]), m_lat[1], m_lat[0], w_in_l, lat_tpg)

        lru_p = lru_params(lru_conv_w[l], lru_conv_b[l], lru_wa[l], lru_ba[l], lru_wx[l], lru_bx[l], lru_lambda[l])
        a_cf, a_cb, a_st = lru_mixer(pc[P_AX], jnp.zeros((bsz, SUBLANES, GROUP_W), F32), bsz, *lru_p)
        a_lf, a_lb, _ = lru_mixer(pl_[P_AX], a_st, bsz, *lru_p)

        kc = pc[P_BK].reshape(bsz, n_ctx, GROUP_W)
        vc = pc[P_BV].reshape(bsz, n_ctx, GROUP_W)
        b_c = ctx_attention(pc[P_BQ].reshape(bsz, n_ctx, GROUP_W), kc, vc).reshape(bsz * n_ctx, GROUP_W)
        b_l = na_mixer(pl_[P_BQ], pl_[P_BK], pl_[P_BV], kc, vc, na_bias_slabs(na_bias[l]), bsz)

        ssd_p = ssd_params(ssd_conv_w[l], ssd_conv_b[l], ssd_a_log[l], ssd_dt_bias[l])
        c_cf, c_cb, c_cx, c_st = ssd_mixer(pc[P_CX], pc[P_SM], jnp.zeros((bsz, 2, N_HEADS, SSD_STATE, HEAD_DIM), F32),
                                           bsz, *ssd_p)
        c_lf, c_lb, c_lx, _ = ssd_mixer(pl_[P_CX], pl_[P_SM], c_st, bsz, *ssd_p)

        gdn_p = gdn_params(gdn_conv_w[l], gdn_a_log[l], gdn_dt_bias[l])
        d_cf, d_cb, d_st = gdn_mixer(pc[P_DX], pc[P_SM], jnp.zeros((bsz, 2, N_HEADS, HEAD_DIM, HEAD_DIM), F32),
                                     bsz, *gdn_p)
        d_lf, d_lb, _ = gdn_mixer(pl_[P_DX], pl_[P_SM], d_st, bsz, *gdn_p, rope=rope)

        epi = (w_out[l].astype(MXU_DTYPE), row(g_post_mix[l]))
        epi_tail = (row(jnp.repeat(ssd_d[l], HEAD_DIM)), row(ssd_norm[l]), row(jnp.tile(gdn_norm[l], N_HEADS)), router_w[l])
        routed_w = (l, we_gate, we_up, we_down)
        shared_w = (ws_gate[l].astype(MXU_DTYPE), ws_up[l].astype(MXU_DTYPE), ws_down[l].astype(MXU_DTYPE),
                    row(g_post_ffn[l]))

        mix_l = (a_lf, a_lb, pl_[P_AG], b_l, c_lf, c_lb, c_lx, pl_[P_CZ], d_lf, d_lb, pl_[P_DZ])
        xl, h2, lg = out_projection(xl, mix_l, *epi, m_lat[2], row(g_pre_ffn[l]), m_lat[4], m_lat[3], *epi_tail, lat_tpg)
        gk, ek, pk, cnt = router_dispatch(lg, router_b[l])
        off, be, nv = moe_plan(cnt, bsz * seq)
        dest = moe_rows(off, ek, pk)
        xs = moe_dispatch(h2, dest, be.shape[0] * MOE_BLOCK)
        ys = moe_experts(xs, be, nv, *routed_w)
        xl = moe_combine(ys, dest, gk, h2, xl, *shared_w, m_lat[5], seq // min(MOE_ROW_TILE, seq))
        if not last:
            mix_c = (a_cf, a_cb, pc[P_AG], b_c, c_cf, c_cb, c_cx, pc[P_CZ], d_cf, d_cb, pc[P_DZ])
            xc, h2, lg = out_projection(xc, mix_c, *epi, m_ctx[2], row(g_pre_ffn[l]), m_ctx[4], m_ctx[3], *epi_tail, ctx_tpg)
            xc = moe_ffn(h2, router_gates(lg, router_b[l]), xc, *routed_w, *shared_w, m_ctx[5], ctx_mpg)
    return xl.reshape(bsz, seq, d)
```

```python
import functools
import math

import jax
import jax.numpy as jnp
import numpy as np
from jax import lax
from jax.experimental import pallas as pl
from jax.experimental.pallas import tpu as pltpu
from jax.experimental.pallas import tpu_sc as plsc

F32 = jnp.float32
MXU_DTYPE = jnp.bfloat16
HI = lax.Precision.HIGHEST

D_MODEL = 1024
GRID_W = 64
GROUP_W = 256
HEAD_DIM = 64
N_HEADS = 4
EPS = 1e-6
NEG_INF = -1e30
N_MOD = 6
LRU_C = 8.0
NA_WIN_ROWS = 8
NA_WIN_COLS = 16
SSD_STATE = 128
SSD_GROUPS = 2
ROPE_BASE = 10000.0
ROPE_AXIS_DIM = HEAD_DIM // 2
N_EXPERTS = 64
N_EXPERT_GROUPS = 8
TOPK_GROUPS = 4
TOP_K = 8
D_EXPERT = 256
ROUTED_SCALE = 2.5

LANES = 128
SUBLANES = 8
VMEM_LIMIT = 56 * 1024 * 1024

TOKEN_TILE = 512
LRU_CHUNK = 256
SSD_CHUNK = 128
GDN_CHUNK = 64
GDN_TILE = 256
GDN_SUB = 128
GDN_BASE = 16
MOE_TILE = 1024
MOE_EB = 4
MOE_BLOCK = 512
MOE_ROW_TILE = 256
MOE_PLAN_TILE = 2048
SC_WINDOW = 128

P_WIDTHS = (256, 256, 256, 256, 256, 768, 256, 768, 256, 128)
(P_AX, P_AG, P_BQ, P_BK, P_BV, P_CX, P_CZ, P_DX, P_DZ, P_SM) = range(10)
SM_DT, SM_BETA, SM_DECAY = 0, 8, 16


def _cp(*sem):
    return pltpu.CompilerParams(dimension_semantics=sem, vmem_limit_bytes=VMEM_LIMIT)


def _mx(x):
    return x.astype(MXU_DTYPE)


def _dot(a, b):
    return jnp.dot(_mx(a), _mx(b), preferred_element_type=F32)


def _dot_nt(a, b):
    return lax.dot_general(_mx(a), _mx(b), (((1,), (1,)), ((), ())), preferred_element_type=F32)


def _dot_tn(a, b):
    return lax.dot_general(_mx(a), _mx(b), (((0,), (0,)), ((), ())), preferred_element_type=F32)


def _dot_hi(a, b):
    return jnp.dot(a, b, preferred_element_type=F32, precision=HI)


def _sigmoid(x):
    return 1.0 / (1.0 + jnp.exp(-x))


def _silu(x):
    return x * _sigmoid(x)


def _softplus(x):
    return jnp.maximum(x, 0.0) + jnp.log1p(jnp.exp(-jnp.abs(x)))


def _gelu_tanh(x):
    return 0.5 * x * (1.0 + jnp.tanh(math.sqrt(2.0 / math.pi) * (x + 0.044715 * (x * x * x))))


def _rms(x, g):
    return x * lax.rsqrt(jnp.mean(x * x, axis=-1, keepdims=True) + EPS) * g


def _full(shape):
    n = len(shape)
    return pl.BlockSpec(shape, lambda *_: (0,) * n)


MOD_COLS = 1536


def _mod_kernel(c_ref, w_ref, b_ref, o_ref):
    o_ref[0] = _dot_hi(_silu(c_ref[...]), w_ref[0]) + b_ref[0]


def modulation(cond, w_mod, b_mod):
    depth, d, n = w_mod.shape
    return pl.pallas_call(
        _mod_kernel,
        grid=(depth, n // MOD_COLS),
        in_specs=[pl.BlockSpec((SUBLANES, d), lambda l, j: (0, 0)),
                  pl.BlockSpec((1, d, MOD_COLS), lambda l, j: (l, 0, j)),
                  pl.BlockSpec((1, 1, MOD_COLS), lambda l, j: (l, 0, j))],
        out_specs=pl.BlockSpec((1, SUBLANES, MOD_COLS), lambda l, j: (l, 0, j)),
        out_shape=jax.ShapeDtypeStruct((depth, SUBLANES, n), F32),
        compiler_params=_cp("parallel", "parallel"),
        name="modulation",
    )(cond, w_mod, b_mod.reshape(depth, 1, n))


def _inproj_kernel(x_ref, g_ref, sc_ref, sh_ref, w_ref, *o_refs):
    h = _rms(x_ref[...], g_ref[...]) * (1.0 + sc_ref[0]) + sh_ref[0]
    p = _dot(h, w_ref[...])
    off = 0
    for o_ref, w in zip(o_refs, P_WIDTHS):
        o_ref[...] = p[:, off:off + w]
        off += w


def in_projection(x, g, sc, sh, w, tiles_per_group):
    t, d = x.shape
    tm = min(TOKEN_TILE, t)
    vec = lambda i: (i // tiles_per_group, 0, 0)
    return pl.pallas_call(
        _inproj_kernel,
        grid=(t // tm,),
        in_specs=[pl.BlockSpec((tm, d), lambda i: (i, 0)),
                  _full((1, d)),
                  pl.BlockSpec((1, 1, d), vec),
                  pl.BlockSpec((1, 1, d), vec),
                  _full(w.shape)],
        out_specs=[pl.BlockSpec((tm, wd), lambda i: (i, 0)) for wd in P_WIDTHS],
        out_shape=[jax.ShapeDtypeStruct((t, wd), F32) for wd in P_WIDTHS],
        compiler_params=_cp("parallel"),
        name="in_projection",
    )(x, g, sc, sh, w)


def _halo(p, bsz, q):
    c = p.shape[-1]
    pr = p.reshape(bsz, -1, q, c)
    nc = pr.shape[1]
    prev = jnp.concatenate([jnp.zeros((bsz, 1, 2, c), p.dtype), pr[:, :-1, q - 2:, :]], axis=1)
    nxt = jnp.concatenate([pr[:, 1:, :1, :], jnp.zeros((bsz, 1, 1, c), p.dtype)], axis=1)
    pad = jnp.zeros((bsz, nc, SUBLANES - 3, c), p.dtype)
    return jnp.concatenate([prev, nxt, pad], axis=2).reshape(bsz * nc, SUBLANES, c)


def _dwconv(x, halo, w, b=None):
    q = x.shape[0]
    row = lax.broadcasted_iota(jnp.int32, x.shape, 0)
    xm2 = jnp.where(row == 0, halo[0:1], jnp.where(row == 1, halo[1:2], pltpu.roll(x, 2, 0)))
    xm1 = jnp.where(row == 0, halo[1:2], pltpu.roll(x, 1, 0))
    xp1 = jnp.where(row == q - 1, halo[2:3], pltpu.roll(x, q - 1, 0))
    y = w[0:1] * xm2 + w[1:2] * xm1 + w[2:3] * x + w[3:4] * xp1
    return y if b is None else y + b


def _pad_rows(a, rows=SUBLANES):
    return jnp.concatenate([a, jnp.zeros((rows - a.shape[0],) + a.shape[1:], a.dtype)], axis=0)


def _chunk_specs(nc, q, c):
    fwd = pl.BlockSpec((q, c), lambda b, i: (b * nc + i, 0))
    bwd = pl.BlockSpec((q, c), lambda b, i: (b * nc + nc - 1 - i, 0))
    return fwd, bwd


def _halo_specs(nc, c):
    fwd = pl.BlockSpec((1, SUBLANES, c), lambda b, i: (b * nc + i, 0, 0))
    bwd = pl.BlockSpec((1, SUBLANES, c), lambda b, i: (b * nc + nc - 1 - i, 0, 0))
    return fwd, bwd


def _lru_kernel(xf_ref, xb_ref, hf_ref, hb_ref, h0_ref, cw_ref, cb_ref, wg_ref, bg_ref, lam_ref,
                yf_ref, yb_ref, hfin_ref, af_s, bf_s, ab_s, bb_s, carry_s):
    i = pl.program_id(1)
    q = xf_ref.shape[0]

    @pl.when(i == 0)
    def _():
        carry_s[...] = h0_ref[0]

    def coeffs(x_ref, halo_ref, d, a_s, b_s):
        u = _dwconv(x_ref[...], halo_ref[0], cw_ref[...], cb_ref[...])
        g = _dot(u, wg_ref[:, 2 * GROUP_W * d:2 * GROUP_W * (d + 1)]) + bg_ref[:, 2 * GROUP_W * d:2 * GROUP_W * (d + 1)]
        r = _sigmoid(g[:, :GROUP_W])
        gate_in = _sigmoid(g[:, GROUP_W:])
        log_a = -LRU_C * r * _softplus(-lam_ref[d:d + 1, :])
        a_s[...] = jnp.exp(log_a)
        b_s[...] = jnp.sqrt(1.0 - jnp.exp(2.0 * log_a)) * (gate_in * u)

    coeffs(xf_ref, hf_ref, 0, af_s, bf_s)
    coeffs(xb_ref, hb_ref, 1, ab_s, bb_s)

    ng = q // SUBLANES
    row = lax.broadcasted_iota(jnp.int32, (SUBLANES, GROUP_W), 0)

    def body(g, hs):
        h_f, h_b = hs
        i0 = pl.multiple_of(g * SUBLANES, SUBLANES)
        a = af_s[pl.ds(i0, SUBLANES), :]
        b = bf_s[pl.ds(i0, SUBLANES), :]
        for s in (1, 2, 4):
            m = row >= s
            b = jnp.where(m, a * pltpu.roll(b, s, 0) + b, b)
            a = jnp.where(m, a * pltpu.roll(a, s, 0), a)
        h = b + a * h_f
        yf_ref[pl.ds(i0, SUBLANES), :] = h
        h_f = h[SUBLANES - 1:SUBLANES, :]
        j0 = pl.multiple_of((ng - 1 - g) * SUBLANES, SUBLANES)
        a = ab_s[pl.ds(j0, SUBLANES), :]
        b = bb_s[pl.ds(j0, SUBLANES), :]
        for s in (1, 2, 4):
            m = row < SUBLANES - s
            b = jnp.where(m, a * pltpu.roll(b, SUBLANES - s, 0) + b, b)
            a = jnp.where(m, a * pltpu.roll(a, SUBLANES - s, 0), a)
        h = b + a * h_b
        yb_ref[pl.ds(j0, SUBLANES), :] = h
        return h_f, h[0:1, :]

    h_f, h_b = lax.fori_loop(0, ng, body, (carry_s[0:1, :], carry_s[1:2, :]))
    carry_s[0:1, :] = h_f
    carry_s[1:2, :] = h_b

    @pl.when(i == pl.num_programs(1) - 1)
    def _():
        hfin_ref[0] = carry_s[...]


def _block_diag(w):
    h, a, b = w.shape
    return jnp.einsum('hij,hg->higj', w, jnp.eye(h, dtype=w.dtype)).reshape(h * a, h * b)


def lru_params(conv_w, conv_b, wa, ba, wx, bx, lam):
    wg = jnp.concatenate([_block_diag(wa[0]), _block_diag(wx[0]), _block_diag(wa[1]), _block_diag(wx[1])], axis=1)
    bg = jnp.concatenate([ba[0], bx[0], ba[1], bx[1]])[None, :]
    return _pad_rows(conv_w), conv_b[None, :], wg.astype(MXU_DTYPE), bg, _pad_rows(lam)


def lru_mixer(x, h0, bsz, cw, cb, wg, bg, lam):
    t, c = x.shape
    s = t // bsz
    q = min(LRU_CHUNK, s)
    nc = s // q
    halo = _halo(x, bsz, q)
    xf, xb = _chunk_specs(nc, q, c)
    hf, hb = _halo_specs(nc, c)
    st = pl.BlockSpec((1, SUBLANES, c), lambda b, i: (b, 0, 0))
    return pl.pallas_call(
        _lru_kernel,
        grid=(bsz, nc),
        in_specs=[xf, xb, hf, hb, st, _full(cw.shape), _full(cb.shape), _full(wg.shape), _full(bg.shape),
                  _full(lam.shape)],
        out_specs=[xf, xb, st],
        out_shape=[jax.ShapeDtypeStruct((t, c), F32), jax.ShapeDtypeStruct((t, c), F32),
                   jax.ShapeDtypeStruct((bsz, SUBLANES, c), F32)],
        scratch_shapes=[pltpu.VMEM((q, c), F32)] * 4 + [pltpu.VMEM((SUBLANES, c), F32)],
        compiler_params=_cp("parallel", "arbitrary"),
        name="lru_mixer",
    )(x, x, halo, halo, h0, cw, cb, wg, bg, lam)


NA_KEYS = NA_WIN_ROWS * GRID_W
NA_ROW_BLOCK = 4


def na_bias_slabs(table):
    qc = np.arange(GRID_W)[:, None]
    kc = np.arange(GRID_W)[None, :]
    win0 = np.clip(qc - NA_WIN_COLS // 2, 0, GRID_W - NA_WIN_COLS)
    ok = (kc >= win0) & (kc < win0 + NA_WIN_COLS)
    dc = np.clip(kc - qc + NA_WIN_COLS - 1, 0, 2 * NA_WIN_COLS - 2)
    dr = np.arange(NA_WIN_ROWS)[:, None] + np.arange(NA_WIN_ROWS)[None, :]
    b = table.astype(F32)[:, dr][:, :, :, dc]
    b = jnp.where(ok[None, None, None], b, NEG_INF)
    h = table.shape[0]
    return b.transpose(0, 1, 3, 2, 4).reshape(h, NA_WIN_ROWS, GRID_W, NA_KEYS)


def _na_span_start(j, rows):
    return jnp.clip(j * NA_ROW_BLOCK - NA_WIN_ROWS // 2, 0, rows - (NA_ROW_BLOCK + NA_WIN_ROWS - 1))


def _na_kernel(q_ref, kw_ref, vw_ref, kc_ref, vc_ref, slab_ref, o_ref, *, rows):
    j = pl.program_id(1)
    ustart = _na_span_start(j, rows)
    q = q_ref[...] * (HEAD_DIM ** -0.5)
    kc, vc = kc_ref[0], vc_ref[0]
    heads = [slice(h * HEAD_DIM, (h + 1) * HEAD_DIM) for h in range(N_HEADS)]
    qrows = [slice(i * GRID_W, (i + 1) * GRID_W) for i in range(NA_ROW_BLOCK)]
    kws, vws, offs = [], [], []
    for i in range(NA_ROW_BLOCK):
        r = j * NA_ROW_BLOCK + i
        r0 = jnp.clip(r - NA_WIN_ROWS // 2, 0, rows - NA_WIN_ROWS)
        start = pl.multiple_of((r0 - ustart) * GRID_W, GRID_W)
        kws.append(kw_ref[pl.ds(start, NA_KEYS), :])
        vws.append(vw_ref[pl.ds(start, NA_KEYS), :])
        offs.append(r0 - r + NA_WIN_ROWS - 1)
    s_ctx = [_dot_nt(q[:, sl], kc[:, sl]) for sl in heads]
    s_loc = [[_dot_nt(q[qr, sl], kws[i][:, sl]) + slab_ref[h, offs[i]] for h, sl in enumerate(heads)]
             for i, qr in enumerate(qrows)]
    m = [[jnp.maximum(jnp.max(s_loc[i][h], axis=-1, keepdims=True), jnp.max(s_ctx[h][qr], axis=-1, keepdims=True))
          for h in range(N_HEADS)] for i, qr in enumerate(qrows)]
    p_loc = [[jnp.exp(s_loc[i][h] - m[i][h]) for h in range(N_HEADS)] for i in range(NA_ROW_BLOCK)]
    p_ctx = [jnp.exp(s_ctx[h] - jnp.concatenate([m[i][h] for i in range(NA_ROW_BLOCK)], axis=0))
             for h in range(N_HEADS)]
    o_ctx = [_dot(p_ctx[h], vc[:, sl]) for h, sl in enumerate(heads)]
    rows_out = []
    for i, qr in enumerate(qrows):
        outs = []
        for h, sl in enumerate(heads):
            den = jnp.sum(p_loc[i][h], axis=-1, keepdims=True) + jnp.sum(p_ctx[h][qr], axis=-1, keepdims=True)
            outs.append((_dot(p_loc[i][h], vws[i][:, sl]) + o_ctx[h][qr]) / den)
        rows_out.append(jnp.concatenate(outs, axis=1))
    o_ref[...] = jnp.concatenate(rows_out, axis=0)


def na_mixer(q, k, v, kc, vc, slabs, bsz):
    t, c = q.shape
    s = t // bsz
    rows = s // GRID_W
    n_ctx = kc.shape[1]
    span = (NA_ROW_BLOCK + NA_WIN_ROWS - 1) * GRID_W

    def win(b, j):
        return ((b * rows + _na_span_start(j, rows)) * GRID_W, 0)

    wspec = pl.BlockSpec((pl.Element(span), pl.Element(c)), win)
    cspec = pl.BlockSpec((1, n_ctx, c), lambda b, j: (b, 0, 0))
    qspec = pl.BlockSpec((NA_ROW_BLOCK * GRID_W, c), lambda b, j: (b * (rows // NA_ROW_BLOCK) + j, 0))
    return pl.pallas_call(
        functools.partial(_na_kernel, rows=rows),
        grid=(bsz, rows // NA_ROW_BLOCK),
        in_specs=[qspec, wspec, wspec, cspec, cspec, _full(slabs.shape)],
        out_specs=qspec,
        out_shape=jax.ShapeDtypeStruct((t, c), F32),
        compiler_params=_cp("parallel", "arbitrary"),
        name="na_mixer",
    )(q, k, v, kc, vc, slabs)


def _ctx_attn_kernel(q_ref, k_ref, v_ref, o_ref):
    q = q_ref[0] * (HEAD_DIM ** -0.5)
    k, v = k_ref[0], v_ref[0]
    outs = []
    for h in range(N_HEADS):
        sl = slice(h * HEAD_DIM, (h + 1) * HEAD_DIM)
        s = _dot_nt(q[:, sl], k[:, sl])
        p = jnp.exp(s - jnp.max(s, axis=-1, keepdims=True))
        outs.append(_dot(p, v[:, sl]) / jnp.sum(p, axis=-1, keepdims=True))
    o_ref[0] = jnp.concatenate(outs, axis=1)


def ctx_attention(q, k, v):
    spec = pl.BlockSpec((1,) + q.shape[1:], lambda b: (b, 0, 0))
    return pl.pallas_call(
        _ctx_attn_kernel,
        grid=(q.shape[0],),
        in_specs=[spec, spec, spec],
        out_specs=spec,
        out_shape=jax.ShapeDtypeStruct(q.shape, F32),
        compiler_params=_cp("parallel"),
        name="ctx_attention",
    )(q, k, v)


def _small_vec(vals, off):
    v = jnp.zeros((LANES,), F32).at[off:off + 2 * N_HEADS].set(vals.reshape(-1).astype(F32))
    return v[None, :]


def _lane_mask(off):
    lane = lax.broadcasted_iota(jnp.int32, (1, LANES), 1)
    return (lane >= off) & (lane < off + 2 * N_HEADS)


def _tri_masks(q):
    rowi = lax.broadcasted_iota(jnp.int32, (q, q), 0)
    coli = lax.broadcasted_iota(jnp.int32, (q, q), 1)
    return rowi, coli


def _ssd_kernel(xf_ref, xb_ref, hf_ref, hb_ref, sf_ref, sb_ref, h0_ref, cw_ref, cb_ref, dtb_ref, alog_ref,
                yf_ref, yb_ref, xc_ref, hfin_ref, state_s):
    i = pl.program_id(1)
    q = xf_ref.shape[0]

    @pl.when(i == 0)
    def _():
        state_s[...] = h0_ref[0]

    rowi, coli = _tri_masks(q)
    a_neg = jnp.where(_lane_mask(SM_DT), -jnp.exp(alog_ref[...]), 0.0)

    def direction(x_ref, halo_ref, sm_ref, d, y_ref):
        xbc = _silu(_dwconv(x_ref[...], halo_ref[0], cw_ref[...], cb_ref[...]))
        if d == 0:
            xc_ref[...] = xbc[:, :GROUP_W]
        dt = _softplus(sm_ref[...] + dtb_ref[...])
        keep = (rowi >= coli) if d == 0 else (rowi <= coli)
        acum = _dot_hi(keep.astype(F32), dt * a_neg)
        acum_t = acum.T
        last = acum[q - 1:q, :] if d == 0 else acum[0:1, :]
        dec_end = jnp.exp(last - acum)
        e_acum = jnp.exp(acum)
        e_last = jnp.exp(last)
        ys = []
        for g in range(SSD_GROUPS):
            bg = xbc[:, GROUP_W + SSD_STATE * g:GROUP_W + SSD_STATE * (g + 1)]
            cg = xbc[:, GROUP_W + SSD_STATE * (SSD_GROUPS + g):GROUP_W + SSD_STATE * (SSD_GROUPS + g + 1)]
            cbt = _dot_nt(cg, bg)
            for hh in range(N_HEADS // SSD_GROUPS):
                h = g * (N_HEADS // SSD_GROUPS) + hh
                ln = SM_DT + N_HEADS * d + h
                lmat = jnp.exp(jnp.where(keep, acum[:, ln:ln + 1] - acum_t[ln:ln + 1, :], NEG_INF))
                xdt = xbc[:, h * HEAD_DIM:(h + 1) * HEAD_DIM] * dt[:, ln:ln + 1]
                st = state_s[d, h]
                ys.append(_dot(cbt * lmat, xdt) + _dot(cg * e_acum[:, ln:ln + 1], st))
                state_s[d, h] = st * e_last[:, ln:ln + 1] + _dot_tn(bg * dec_end[:, ln:ln + 1], xdt)
        y_ref[...] = jnp.concatenate(ys, axis=1)

    direction(xf_ref, hf_ref, sf_ref, 0, yf_ref)
    direction(xb_ref, hb_ref, sb_ref, 1, yb_ref)

    @pl.when(i == pl.num_programs(1) - 1)
    def _():
        hfin_ref[0] = state_s[...]


def ssd_params(conv_w, conv_b, a_log, dt_bias):
    return _pad_rows(conv_w), conv_b[None, :], _small_vec(dt_bias, SM_DT), _small_vec(a_log, SM_DT)


def ssd_mixer(xbc, sm, h0, bsz, cw, cb, dtb, alog):
    t, c = xbc.shape
    s = t // bsz
    q = min(SSD_CHUNK, s)
    nc = s // q
    halo = _halo(xbc, bsz, q)
    xf, xb = _chunk_specs(nc, q, c)
    hf, hb = _halo_specs(nc, c)
    sf, sb = _chunk_specs(nc, q, LANES)
    yf, yb = _chunk_specs(nc, q, GROUP_W)
    st = pl.BlockSpec((1,) + h0.shape[1:], lambda b, i: (b, 0, 0, 0, 0))
    y_shape = jax.ShapeDtypeStruct((t, GROUP_W), F32)
    return pl.pallas_call(
        _ssd_kernel,
        grid=(bsz, nc),
        in_specs=[xf, xb, hf, hb, sf, sb, st, _full(cw.shape), _full(cb.shape), _full(dtb.shape), _full(alog.shape)],
        out_specs=[yf, yb, yf, st],
        out_shape=[y_shape, y_shape, y_shape, jax.ShapeDtypeStruct(h0.shape, F32)],
        scratch_shapes=[pltpu.VMEM(h0.shape[1:], F32)],
        compiler_params=_cp("parallel", "arbitrary"),
        name="ssd_mixer",
    )(xbc, xbc, halo, halo, sm, sm, h0, cw, cb, dtb, alog)


def rope_tables(seq):
    t = jnp.arange(seq)
    row = (t // GRID_W).astype(F32)
    col = (t % GRID_W).astype(F32)
    inv = ROPE_BASE ** (-jnp.arange(0, ROPE_AXIS_DIM, 2, dtype=F32) / ROPE_AXIS_DIM)
    ar, ac = row[:, None] * inv, col[:, None] * inv
    cos = jnp.concatenate([jnp.cos(ar), jnp.cos(ar), jnp.cos(ac), jnp.cos(ac)], axis=1)
    sin = jnp.concatenate([-jnp.sin(ar), jnp.sin(ar), -jnp.sin(ac), jnp.sin(ac)], axis=1)
    return jnp.tile(cos, (1, N_HEADS)), jnp.tile(sin, (1, N_HEADS))


def _swap16(x):
    lane = lax.broadcasted_iota(jnp.int32, x.shape, 1)
    half = ROPE_AXIS_DIM // 2
    return jnp.where((lane & (ROPE_AXIS_DIM - 1)) < half,
                     pltpu.roll(x, x.shape[1] - half, 1), pltpu.roll(x, half, 1))


def _l2norm_heads(x):
    outs = []
    for h in range(N_HEADS):
        xh = x[:, h * HEAD_DIM:(h + 1) * HEAD_DIM]
        outs.append(xh * lax.rsqrt(jnp.sum(xh * xh, axis=-1, keepdims=True) + EPS))
    return jnp.concatenate(outs, axis=1)


def _dot_tri(mask, x):
    m = _mx(mask.astype(F32))
    x1 = _mx(x)
    r1 = x - x1.astype(F32)
    x2 = _mx(r1)
    x3 = _mx(r1 - x2.astype(F32))
    return (jnp.dot(m, x1, preferred_element_type=F32) + jnp.dot(m, x2, preferred_element_type=F32)
            + jnp.dot(m, x3, preferred_element_type=F32))


def _same_block(rowi, coli, n):
    sh = n.bit_length() - 1
    return (rowi >> sh) == (coli >> sh)


def _solve_unit_tri(a_list, rhs_list, rowi, coli, chunk):
    mm = lambda x, y: jnp.dot(x, y, preferred_element_type=F32)
    eye = (rowi == coli).astype(F32)
    in_base = _same_block(rowi, coli, GDN_BASE)
    base = [_mx(jnp.where(in_base, a, 0.0)) for a in a_list]
    ts = [jnp.where(in_base, eye - a, 0.0) for a in a_list]
    ps = [_mx(mm(b, b)) for b in base]
    ts = [t + mm(_mx(t), p) for t, p in zip(ts, ps)]
    n = 4
    while n < GDN_BASE:
        ps = [_mx(mm(p, p)) for p in ps]
        ts = [t + mm(_mx(t), p) for t, p in zip(ts, ps)]
        n *= 2
    n = GDN_BASE
    while 2 * n < chunk:
        inner = _same_block(rowi, coli, 2 * n) & jnp.logical_not(_same_block(rowi, coli, n))
        offs = [_mx(jnp.where(inner, a, 0.0)) for a in a_list]
        tb = [_mx(t) for t in ts]
        ms = [_mx(mm(t, off)) for t, off in zip(tb, offs)]
        ts = [t - mm(m, t_b) for t, m, t_b in zip(ts, ms, tb)]
        n *= 2
    outer = jnp.logical_not(_same_block(rowi, coli, n))
    offs = [_mx(jnp.where(outer, a, 0.0)) for a in a_list]
    tb = [_mx(t) for t in ts]
    ys = [mm(t, _mx(r)) for t, r in zip(tb, rhs_list)]
    zs = [_mx(mm(off, _mx(y))) for off, y in zip(offs, ys)]
    return [y - mm(t, z) for y, t, z in zip(ys, tb, zs)]


def _gdn_kernel(*refs, rope):
    if rope:
        (xf_ref, xb_ref, hf_ref, hb_ref, sf_ref, sb_ref, cf_ref, cb_ref, nf_ref, nb_ref,
         s0_ref, cw_ref, alog_ref, dtb_ref, of_ref, ob_ref, sfin_ref, state_s) = refs
    else:
        (xf_ref, xb_ref, hf_ref, hb_ref, sf_ref, sb_ref,
         s0_ref, cw_ref, alog_ref, dtb_ref, of_ref, ob_ref, sfin_ref, state_s) = refs
        cf_ref = cb_ref = nf_ref = nb_ref = None
    i = pl.program_id(1)
    tq = xf_ref.shape[0]
    ck = min(GDN_CHUNK, tq)
    nck = tq // ck

    @pl.when(i == 0)
    def _():
        state_s[...] = s0_ref[0]

    sub = min(GDN_SUB, tq)
    nsub = tq // sub
    rowt, colt = _tri_masks(tq)
    in_chunk_t = _same_block(rowt, colt, ck)
    rowi, coli = _tri_masks(sub)
    in_chunk = _same_block(rowi, coli, ck)
    a_neg = jnp.where(_lane_mask(SM_DECAY), -jnp.exp(alog_ref[...]), 0.0)

    a_list, rhs_list, qkm, qg, kd, e_last = [], [], [], [], [], []
    for d, (x_ref, halo_ref, sm_ref, cos_ref, sin_ref) in enumerate(
            ((xf_ref, hf_ref, sf_ref, cf_ref, nf_ref), (xb_ref, hb_ref, sb_ref, cb_ref, nb_ref))):
        qkv = _silu(_dwconv(x_ref[...], halo_ref[0], cw_ref[...]))
        qn = _l2norm_heads(qkv[:, :GROUP_W])
        kn = _l2norm_heads(qkv[:, GROUP_W:2 * GROUP_W])
        v = qkv[:, 2 * GROUP_W:]
        if rope:
            cos, sin = cos_ref[...], sin_ref[...]
            qn = qn * cos + _swap16(qn) * sin
            kn = kn * cos + _swap16(kn) * sin
        qn = qn * (HEAD_DIM ** -0.5)
        sm = sm_ref[...]
        beta = _sigmoid(sm)
        keep_t = in_chunk_t & ((rowt >= colt) if d == 0 else (rowt <= colt))
        keep = in_chunk & ((rowi >= coli) if d == 0 else (rowi <= coli))
        strict = in_chunk & ((rowi > coli) if d == 0 else (rowi < coli))
        gc = _dot_tri(keep_t, _softplus(sm + dtb_ref[...]) * a_neg)
        gc_t = gc.T
        edge = ck - 1 if d == 0 else 0
        last = jnp.concatenate([jnp.broadcast_to(gc[c * ck + edge:c * ck + edge + 1, :], (ck, LANES))
                                for c in range(nck)], axis=0)
        e_gc = jnp.exp(gc)
        e_end = jnp.exp(last - gc)
        e_last.append(jnp.exp(last))
        for h in range(N_HEADS):
            sl = slice(h * HEAD_DIM, (h + 1) * HEAD_DIM)
            lg = SM_DECAY + N_HEADS * d + h
            lb = SM_BETA + N_HEADS * d + h
            qh, kh, bcol = qn[:, sl], kn[:, sl], beta[:, lb:lb + 1]
            kb = kh * bcol
            rhs = jnp.concatenate([v[:, sl] * bcol, kb * e_gc[:, lg:lg + 1]], axis=1)
            qg.append(qh * e_gc[:, lg:lg + 1])
            kd.append(kh * e_end[:, lg:lg + 1])
            for s in range(nsub):
                rs = slice(s * sub, (s + 1) * sub)
                decay = jnp.exp(jnp.where(keep, gc[rs, lg:lg + 1] - gc_t[lg:lg + 1, rs], NEG_INF))
                a_list.append(jnp.where(strict, _dot_nt(kb[rs], kh[rs]) * decay, 0.0))
                rhs_list.append(rhs[rs])
                qkm.append(_dot_nt(qh[rs], kh[rs]) * decay)
    sols = _solve_unit_tri(a_list, rhs_list, rowi, coli, ck)
    sols = [jnp.concatenate(sols[n * nsub:(n + 1) * nsub], axis=0) for n in range(2 * N_HEADS)]

    chains = [(d, h) for d in range(2) for h in range(N_HEADS)]
    states = [state_s[d, h] for d, h in chains]
    v_new = [[None] * nck for _ in chains]
    o_st = [[None] * nck for _ in chains]
    for step in range(nck):
        rows = [slice((step if d == 0 else nck - 1 - step) * ck, (step if d == 0 else nck - 1 - step) * ck + ck)
                for d, _ in chains]
        ms = [_dot(jnp.concatenate([sols[n][r, HEAD_DIM:], qg[n][r]], axis=0), states[n])
              for n, r in enumerate(rows)]
        for n, (d, _) in enumerate(chains):
            c = step if d == 0 else nck - 1 - step
            v_new[n][c] = sols[n][rows[n], :HEAD_DIM] - ms[n][:ck]
            o_st[n][c] = ms[n][ck:]
        ups = [_dot_tn(kd[n][r], v_new[n][step if chains[n][0] == 0 else nck - 1 - step])
               for n, r in enumerate(rows)]
        for n, (d, h) in enumerate(chains):
            lg = SM_DECAY + N_HEADS * d + h
            states[n] = states[n] * e_last[d][rows[n].start:rows[n].start + 1, lg:lg + 1] + ups[n]
    cps = sub // ck
    outs = [jnp.concatenate(o_st[n], axis=0)
            + jnp.concatenate([_dot(qkm[n * nsub + s], jnp.concatenate(v_new[n][s * cps:(s + 1) * cps], axis=0))
                               for s in range(nsub)], axis=0)
            for n in range(len(chains))]
    of_ref[...] = jnp.concatenate(outs[:N_HEADS], axis=1)
    ob_ref[...] = jnp.concatenate(outs[N_HEADS:], axis=1)
    for n, (d, h) in enumerate(chains):
        state_s[d, h] = states[n]

    @pl.when(i == pl.num_programs(1) - 1)
    def _():
        sfin_ref[0] = state_s[...]


def gdn_params(conv_w, a_log, dt_bias):
    return _pad_rows(conv_w), _small_vec(a_log, SM_DECAY), _small_vec(dt_bias, SM_DECAY)


def gdn_mixer(qkv, sm, s0, bsz, cw, alog, dtb, rope=None):
    t, c = qkv.shape
    s = t // bsz
    q = min(GDN_TILE, s)
    nc = s // q
    halo = _halo(qkv, bsz, q)
    xf, xb = _chunk_specs(nc, q, c)
    hf, hb = _halo_specs(nc, c)
    sf, sb = _chunk_specs(nc, q, LANES)
    of, ob = _chunk_specs(nc, q, GROUP_W)
    st = pl.BlockSpec((1,) + s0.shape[1:], lambda b, i: (b, 0, 0, 0, 0))
    ins = [qkv, qkv, halo, halo, sm, sm]
    specs = [xf, xb, hf, hb, sf, sb]
    if rope is not None:
        tf = pl.BlockSpec((q, GROUP_W), lambda b, i: (i, 0))
        tb = pl.BlockSpec((q, GROUP_W), lambda b, i: (nc - 1 - i, 0))
        ins += [rope[0], rope[0], rope[1], rope[1]]
        specs += [tf, tb, tf, tb]
    ins += [s0, cw, alog, dtb]
    specs += [st, _full(cw.shape), _full(alog.shape), _full(dtb.shape)]
    o_shape = jax.ShapeDtypeStruct((t, GROUP_W), F32)
    return pl.pallas_call(
        functools.partial(_gdn_kernel, rope=rope is not None),
        grid=(bsz, nc),
        in_specs=specs,
        out_specs=[of, ob, st],
        out_shape=[o_shape, o_shape, jax.ShapeDtypeStruct(s0.shape, F32)],
        scratch_shapes=[pltpu.VMEM(s0.shape[1:], F32)],
        compiler_params=_cp("parallel", "arbitrary"),
        name="gdn_mixer",
    )(*ins)


def _split_hi_lo(a):
    hi = _mx(a)
    return hi, _mx(a - hi.astype(F32))


def _outproj_kernel(x_ref, ahf_ref, ahb_ref, ag_ref, bo_ref, cyf_ref, cyb_ref, cxc_ref, cz_ref,
                    dof_ref, dob_ref, dz_ref, wout_ref, gpost_ref, ga1_ref, gpre_ref, sc2_ref, sh2_ref,
                    dskip_ref, cnorm_ref, dnorm_ref, rhi_ref, rlo_ref, xo_ref, h2_ref, hp_ref, lg_ref):
    m_a = (ahf_ref[...] + ahb_ref[...]) * _gelu_tanh(ag_ref[...])
    y_c = (cyf_ref[...] + cyb_ref[...] + cxc_ref[...] * dskip_ref[...]) * _silu(cz_ref[...])
    m_c = _rms(y_c, cnorm_ref[...])
    o_d = dof_ref[...] + dob_ref[...]
    heads = []
    for h in range(N_HEADS):
        oh = o_d[:, h * HEAD_DIM:(h + 1) * HEAD_DIM]
        heads.append(oh * lax.rsqrt(jnp.mean(oh * oh, axis=-1, keepdims=True) + EPS))
    m_d = jnp.concatenate(heads, axis=1) * dnorm_ref[...] * _silu(dz_ref[...])
    mix = jnp.concatenate([_mx(m_a), _mx(bo_ref[...]), _mx(m_c), _mx(m_d)], axis=1)
    ml = jnp.dot(mix, wout_ref[...], preferred_element_type=F32)
    x_new = x_ref[...] + ga1_ref[0] * _rms(ml, gpost_ref[...])
    xo_ref[...] = x_new
    h2 = _rms(x_new, gpre_ref[...]) * (1.0 + sc2_ref[0]) + sh2_ref[0]
    hi, lo = _split_hi_lo(h2)
    h2_ref[...] = h2
    hp_ref[...] = _pack_pairs(h2)
    rhi = rhi_ref[...]
    lg_ref[...] = (jnp.dot(hi, rhi, preferred_element_type=F32) + jnp.dot(lo, rhi, preferred_element_type=F32)
                   + jnp.dot(hi, rlo_ref[...], preferred_element_type=F32))


def out_projection(x, mixers, w_out, gpost, ga1, gpre, sc2, sh2, dskip, cnorm, dnorm, router_w, tiles_per_group):
    t, d = x.shape
    tm = min(TOKEN_TILE, t)
    vec = lambda i: (i // tiles_per_group, 0, 0)
    row = lambda w: pl.BlockSpec((tm, w), lambda i: (i, 0))
    ne = LANES
    rhi, rlo = _split_hi_lo(jnp.pad(router_w.astype(F32), ((0, 0), (0, ne - router_w.shape[1]))))
    return pl.pallas_call(
        _outproj_kernel,
        grid=(t // tm,),
        in_specs=[row(d)] + [row(GROUP_W)] * 11
                 + [_full(w_out.shape), _full((1, d)), pl.BlockSpec((1, 1, d), vec), _full((1, d)),
                    pl.BlockSpec((1, 1, d), vec), pl.BlockSpec((1, 1, d), vec),
                    _full((1, GROUP_W)), _full((1, GROUP_W)), _full((1, GROUP_W)), _full(rhi.shape), _full(rlo.shape)],
        out_specs=[row(d), row(d), row(d // 2), row(ne)],
        out_shape=[jax.ShapeDtypeStruct((t, d), F32), jax.ShapeDtypeStruct((t, d), F32),
                   jax.ShapeDtypeStruct((t, d // 2), jnp.uint32), jax.ShapeDtypeStruct((t, ne), F32)],
        compiler_params=_cp("parallel"),
        name="out_projection",
    )(x, *mixers, w_out, gpost, ga1, gpre, sc2, sh2, dskip, cnorm, dnorm, rhi, rlo)


def _rank_before(vals, idx, count, stride):
    rank = jnp.zeros(vals.shape, jnp.int32)
    for j in range(count):
        other = vals[j * stride:j * stride + 1, :]
        ahead = (other > vals) | ((other == vals) & (idx > j))
        rank = rank + ahead.astype(jnp.int32)
    return rank


def _xor_partner(x, row, s):
    n = x.shape[0]
    return jnp.where((row & s) == 0, pltpu.roll(x, n - s, 0), pltpu.roll(x, s, 0))


def _route(logits, router_b):
    ne = N_EXPERTS
    gsz = ne // N_EXPERT_GROUPS
    scores = _sigmoid(logits.T[:ne, :])
    tm = scores.shape[1]
    biased = scores + router_b
    row = lax.broadcasted_iota(jnp.int32, (ne, tm), 0)
    m1, m2 = biased, jnp.full((ne, tm), -jnp.inf, F32)
    s = 1
    while s < gsz:
        o1, o2 = _xor_partner(m1, row, s), _xor_partner(m2, row, s)
        m2 = jnp.maximum(jnp.minimum(m1, o1), jnp.maximum(m2, o2))
        m1 = jnp.maximum(m1, o1)
        s *= 2
    gidx = row >> (gsz.bit_length() - 1)
    group_ok = _rank_before(m1 + m2, gidx, N_EXPERT_GROUPS, gsz) < TOPK_GROUPS
    choice = jnp.where(group_ok, biased, -jnp.inf)
    rank = _rank_before(choice, row, ne, 1)
    gate = jnp.where(rank < TOP_K, scores, 0.0)
    gate = gate / jnp.sum(gate, axis=0, keepdims=True) * ROUTED_SCALE
    return gate, rank, row


def _to_token_major(x):
    n, tm = x.shape
    return jnp.concatenate([x, jnp.zeros((LANES - n, tm), x.dtype)], axis=0).T


def _router_kernel(lg_ref, rb_ref, gate_ref):
    gate, _, _ = _route(lg_ref[...], rb_ref[...])
    gate_ref[...] = _to_token_major(gate)


def _router_dispatch_kernel(lg_ref, rb_ref, gk_ref, ek_ref, pk_ref, cnt_ref, carry_s):
    i = pl.program_id(0)

    @pl.when(i == 0)
    def _():
        carry_s[...] = jnp.zeros(carry_s.shape, F32)

    gate, rank, row = _route(lg_ref[...], rb_ref[...])
    tm = gate.shape[1]
    picked = (rank < TOP_K).astype(F32)
    before = lax.broadcasted_iota(jnp.int32, (tm, tm), 0) < lax.broadcasted_iota(jnp.int32, (tm, tm), 1)
    pos = _dot(picked, before.astype(F32)) + carry_s[:, 0:1]
    carry_s[...] = carry_s[...] + jnp.sum(picked, axis=1, keepdims=True)
    gk, ek, pk = [], [], []
    for k in range(TOP_K):
        sel = rank == k
        gk.append(jnp.sum(jnp.where(sel, gate, 0.0), axis=0, keepdims=True))
        ek.append(jnp.sum(jnp.where(sel, row, 0), axis=0, keepdims=True))
        pk.append(jnp.sum(jnp.where(sel, pos, 0.0), axis=0, keepdims=True))
    gk_ref[...] = _to_token_major(jnp.concatenate(gk, axis=0))
    ek_ref[...] = jnp.concatenate(ek, axis=0)
    pk_ref[...] = jnp.concatenate(pk, axis=0).astype(jnp.int32)

    @pl.when(i == pl.num_programs(0) - 1)
    def _():
        cnt_ref[...] = carry_s[...].astype(jnp.int32)


def router_dispatch(logits, router_b):
    t, w = logits.shape
    tm = min(TOKEN_TILE, t)
    return pl.pallas_call(
        _router_dispatch_kernel,
        grid=(t // tm,),
        in_specs=[pl.BlockSpec((tm, w), lambda i: (i, 0)), _full((N_EXPERTS, 1))],
        out_specs=[pl.BlockSpec((tm, w), lambda i: (i, 0)),
                   pl.BlockSpec((TOP_K, tm), lambda i: (0, i)),
                   pl.BlockSpec((TOP_K, tm), lambda i: (0, i)),
                   _full((N_EXPERTS, LANES))],
        out_shape=[jax.ShapeDtypeStruct((t, w), F32), jax.ShapeDtypeStruct((TOP_K, t), jnp.int32),
                   jax.ShapeDtypeStruct((TOP_K, t), jnp.int32), jax.ShapeDtypeStruct((N_EXPERTS, LANES), jnp.int32)],
        scratch_shapes=[pltpu.VMEM((N_EXPERTS, LANES), F32)],
        compiler_params=_cp("arbitrary"),
        name="router_dispatch",
    )(logits, router_b.reshape(N_EXPERTS, 1).astype(F32))


def router_gates(logits, router_b):
    t, w = logits.shape
    tm = min(TOKEN_TILE, t)
    return pl.pallas_call(
        _router_kernel,
        grid=(t // tm,),
        in_specs=[pl.BlockSpec((tm, w), lambda i: (i, 0)), _full((N_EXPERTS, 1))],
        out_specs=pl.BlockSpec((tm, w), lambda i: (i, 0)),
        out_shape=jax.ShapeDtypeStruct((t, w), F32),
        compiler_params=_cp("parallel"),
        name="router_gates",
    )(logits, router_b.reshape(N_EXPERTS, 1).astype(F32))


def _moe_kernel(h_ref, gate_ref, x_ref, wg_ref, wu_ref, wd_ref, sg_ref, su_ref, sd_ref, gpost_ref, ga2_ref,
                o_ref, acc_s):
    e = pl.program_id(1)
    h = _mx(h_ref[...])

    @pl.when(e == 0)
    def _():
        hs = _silu(jnp.dot(h, sg_ref[...], preferred_element_type=F32)) * jnp.dot(h, su_ref[...], preferred_element_type=F32)
        acc_s[...] = jnp.dot(_mx(hs), sd_ref[...], preferred_element_type=F32)

    gates = gate_ref[...]
    lane = lax.broadcasted_iota(jnp.int32, gates.shape, 1)
    hid = []
    for j in range(MOE_EB):
        gcol = jnp.sum(jnp.where(lane == e * MOE_EB + j, gates, 0.0), axis=1, keepdims=True)
        g = jnp.dot(h, _mx(wg_ref[j]), preferred_element_type=F32)
        u = jnp.dot(h, _mx(wu_ref[j]), preferred_element_type=F32)
        hid.append(_mx(_silu(g) * u * gcol))
    wd = _mx(wd_ref[...]).reshape(MOE_EB * D_EXPERT, -1)
    acc_s[...] += jnp.dot(jnp.concatenate(hid, axis=1), wd, preferred_element_type=F32)

    @pl.when(e == pl.num_programs(1) - 1)
    def _():
        o_ref[...] = x_ref[...] + ga2_ref[0] * _rms(acc_s[...], gpost_ref[...])


def moe_ffn(h, gates, x, layer, wg, wu, wd, sg, su, sd, gpost, ga2, tiles_per_group):
    t, d = x.shape
    tm = min(MOE_TILE, t)
    _, ne, _, f = wg.shape
    row = lambda w: pl.BlockSpec((tm, w), lambda i, e: (i, 0))
    return pl.pallas_call(
        _moe_kernel,
        grid=(t // tm, ne // MOE_EB),
        in_specs=[row(d), row(gates.shape[1]), row(d),
                  pl.BlockSpec((None, MOE_EB, d, f), lambda i, e: (layer, e, 0, 0)),
                  pl.BlockSpec((None, MOE_EB, d, f), lambda i, e: (layer, e, 0, 0)),
                  pl.BlockSpec((None, MOE_EB, f, d), lambda i, e: (layer, e, 0, 0)),
                  _full(sg.shape), _full(su.shape), _full(sd.shape), _full((1, d)),
                  pl.BlockSpec((1, 1, d), lambda i, e: (i // tiles_per_group, 0, 0))],
        out_specs=row(d),
        out_shape=jax.ShapeDtypeStruct((t, d), F32),
        scratch_shapes=[pltpu.VMEM((tm, d), F32)],
        compiler_params=_cp("parallel", "arbitrary"),
        name="moe_ffn",
    )(h, gates, x, wg, wu, wd, sg, su, sd, gpost, ga2)


def moe_plan(counts, n_tokens):
    n_blocks = (n_tokens * TOP_K + N_EXPERTS * (MOE_BLOCK - 1) + MOE_BLOCK - 1) // MOE_BLOCK
    cnt = counts[:, 0]
    padded = (cnt + MOE_BLOCK - 1) // MOE_BLOCK * MOE_BLOCK
    pad_end = jnp.cumsum(padded)
    off = pad_end - padded
    start = jnp.arange(n_blocks, dtype=jnp.int32) * MOE_BLOCK
    be = jnp.minimum(jnp.sum(pad_end[None, :] <= start[:, None], axis=1), N_EXPERTS - 1).astype(jnp.int32)
    mine = be[:, None] == jnp.arange(N_EXPERTS, dtype=jnp.int32)[None, :]
    end = jnp.sum(jnp.where(mine, (off + cnt)[None, :], 0), axis=1)
    nv = jnp.clip(end - start, 0, MOE_BLOCK).astype(jnp.int32)
    return off.astype(jnp.int32), be, nv


def _rows_kernel(off_ref, ek_ref, pk_ref, dest_ref):
    ek = ek_ref[...]
    dest = pk_ref[...]
    for e in range(N_EXPERTS):
        dest = dest + jnp.where(ek == e, off_ref[e], 0)
    dest_ref[...] = dest


def moe_rows(off, ek, pk):
    k, t = ek.shape
    tm = min(MOE_PLAN_TILE, t)
    spec = pl.BlockSpec((k, tm), lambda i, off: (0, i))
    return pl.pallas_call(
        _rows_kernel,
        grid_spec=pltpu.PrefetchScalarGridSpec(num_scalar_prefetch=1, grid=(t // tm,),
                                               in_specs=[spec, spec], out_specs=spec),
        out_shape=jax.ShapeDtypeStruct((k, t), jnp.int32),
        compiler_params=_cp("arbitrary"),
        name="moe_rows",
    )(off, ek, pk)


U32 = jnp.uint32
HIGH_HALF = 0xFFFF0000


def _pack_pairs(x):
    w = x.shape[1] // 2
    bits = lax.bitcast_convert_type(x.astype(jnp.bfloat16).astype(F32), U32)
    return (bits[:, w:] & jnp.uint32(HIGH_HALF)) | (bits[:, :w] >> 16)


def _unpack_pairs(p):
    lo = lax.bitcast_convert_type(p << 16, F32)
    hi = lax.bitcast_convert_type(p & jnp.uint32(HIGH_HALF), F32)
    return jnp.concatenate([lo, hi], axis=1)


def _sc_workers():
    info = plsc.get_sparse_core_info()
    return info.num_cores, info.num_cores * info.num_subcores


def sc_scatter_rows(src, idx, n_rows):
    k, t = idx.shape
    w = src.shape[1]
    n_cores, n_workers = _sc_workers()
    per_worker = t // n_workers
    mesh = plsc.VectorSubcoreMesh(core_axis_name="c", subcore_axis_name="s")

    @functools.partial(
        pl.kernel, mesh=mesh, out_type=jax.ShapeDtypeStruct((n_rows, w), src.dtype),
        scratch_types=[pltpu.VMEM((k, SC_WINDOW), jnp.int32), pltpu.VMEM((SC_WINDOW, w), src.dtype),
                       pltpu.SemaphoreType.DMA])
    def scatter(s_hbm, i_hbm, o_hbm, idx_v, rows_v, sem):
        base = (lax.axis_index("s") * n_cores + lax.axis_index("c")) * per_worker

        @pl.loop(0, per_worker // SC_WINDOW)
        def _(j):
            off = base + j * SC_WINDOW
            pltpu.sync_copy(i_hbm.at[:, pl.ds(off, SC_WINDOW)], idx_v)
            pltpu.sync_copy(s_hbm.at[pl.ds(off, SC_WINDOW)], rows_v)
            for kk in range(k):
                pltpu.async_copy(rows_v, o_hbm.at[idx_v.at[kk]], sem).wait()

    return scatter(src, idx)


def sc_gather_rows(table, idx):
    n = idx.shape[0]
    w = table.shape[1]
    n_cores, n_workers = _sc_workers()
    per_worker = n // n_workers
    mesh = plsc.VectorSubcoreMesh(core_axis_name="c", subcore_axis_name="s")

    @functools.partial(
        pl.kernel, mesh=mesh, out_type=jax.ShapeDtypeStruct((n, w), table.dtype),
        scratch_types=[pltpu.VMEM((SC_WINDOW,), jnp.int32), pltpu.VMEM((SC_WINDOW, w), table.dtype),
                       pltpu.SemaphoreType.DMA])
    def gather(t_hbm, i_hbm, o_hbm, idx_v, rows_v, sem):
        base = (lax.axis_index("s") * n_cores + lax.axis_index("c")) * per_worker

        @pl.loop(0, per_worker // SC_WINDOW)
        def _(j):
            off = base + j * SC_WINDOW
            pltpu.sync_copy(i_hbm.at[pl.ds(off, SC_WINDOW)], idx_v)
            pltpu.async_copy(t_hbm.at[idx_v], rows_v, sem).wait()
            pltpu.sync_copy(rows_v, o_hbm.at[pl.ds(off, SC_WINDOW)])

    return gather(table, idx)


def _expert_kernel(be_ref, nv_ref, xs_ref, wg_ref, wu_ref, wd_ref, ys_ref):
    nv = nv_ref[pl.program_id(0)]

    @pl.when(nv > 0)
    def _():
        x = _unpack_pairs(xs_ref[...])
        rows = lax.broadcasted_iota(jnp.int32, x.shape, 0)
        x = _mx(jnp.where(rows < nv, x, 0.0))
        hid = (_silu(jnp.dot(x, _mx(wg_ref[0]), preferred_element_type=F32))
               * jnp.dot(x, _mx(wu_ref[0]), preferred_element_type=F32))
        ys_ref[...] = _pack_pairs(jnp.dot(_mx(hid), _mx(wd_ref[0]), preferred_element_type=F32))

    @pl.when(nv == 0)
    def _():
        ys_ref[...] = jnp.zeros(ys_ref.shape, U32)


def moe_experts(xs, be, nv, layer, wg, wu, wd):
    n_rows, w = xs.shape
    _, _, d, f = wg.shape
    return pl.pallas_call(
        _expert_kernel,
        grid_spec=pltpu.PrefetchScalarGridSpec(
            num_scalar_prefetch=2,
            grid=(n_rows // MOE_BLOCK,),
            in_specs=[pl.BlockSpec((MOE_BLOCK, w), lambda b, be, nv: (b, 0)),
                      pl.BlockSpec((None, 1, d, f), lambda b, be, nv: (layer, be[b], 0, 0)),
                      pl.BlockSpec((None, 1, d, f), lambda b, be, nv: (layer, be[b], 0, 0)),
                      pl.BlockSpec((None, 1, f, d), lambda b, be, nv: (layer, be[b], 0, 0))],
            out_specs=pl.BlockSpec((MOE_BLOCK, w), lambda b, be, nv: (b, 0))),
        out_shape=jax.ShapeDtypeStruct((n_rows, w), U32),
        compiler_params=_cp("arbitrary"),
        name="moe_experts",
    )(be, nv, xs, wg, wu, wd)


def _combine_kernel(yg_ref, gk_ref, hp_ref, x_ref, sg_ref, su_ref, sd_ref, gpost_ref, ga2_ref, o_ref):
    h = _mx(_unpack_pairs(hp_ref[...]))
    hs = _silu(jnp.dot(h, sg_ref[...], preferred_element_type=F32)) * jnp.dot(h, su_ref[...], preferred_element_type=F32)
    f = jnp.dot(_mx(hs), sd_ref[...], preferred_element_type=F32)
    gk = gk_ref[...]
    for k in range(TOP_K):
        f = f + gk[:, k:k + 1] * _unpack_pairs(yg_ref[k])
    o_ref[...] = x_ref[...] + ga2_ref[0] * _rms(f, gpost_ref[...])


def moe_combine(yg, gk, hp, x, sg, su, sd, gpost, ga2, tiles_per_group):
    t, d = x.shape
    tm = min(MOE_ROW_TILE, t)
    w = hp.shape[1]
    row = lambda n: pl.BlockSpec((tm, n), lambda i: (i, 0))
    return pl.pallas_call(
        _combine_kernel,
        grid=(t // tm,),
        in_specs=[pl.BlockSpec((TOP_K, tm, w), lambda i: (0, i, 0)),
                  row(gk.shape[1]), row(w), row(d), _full(sg.shape), _full(su.shape), _full(sd.shape), _full((1, d)),
                  pl.BlockSpec((1, 1, d), lambda i: (i // tiles_per_group, 0, 0))],
        out_specs=row(d),
        out_shape=jax.ShapeDtypeStruct((t, d), F32),
        compiler_params=_cp("parallel"),
        name="moe_combine",
    )(yg, gk, hp, x, sg, su, sd, gpost, ga2)


def _reorder_w_in(w_in):
    c = np.cumsum((0,) + (GROUP_W, GROUP_W, GROUP_W, GROUP_W, GROUP_W, GROUP_W, 2 * SSD_STATE, 2 * SSD_STATE,
                          GROUP_W, 2 * N_HEADS, GROUP_W, GROUP_W, GROUP_W, GROUP_W, 2 * N_HEADS, 2 * N_HEADS))
    seg = lambda a, b: w_in[:, c[a]:c[b]]
    small = jnp.concatenate([seg(9, 10), seg(14, 15), seg(15, 16),
                             jnp.zeros((w_in.shape[0], LANES - 6 * N_HEADS), w_in.dtype)], axis=1)
    return jnp.concatenate([seg(0, 5), seg(5, 8), seg(8, 9), seg(10, 13), seg(13, 14), small], axis=1)


def kernel(x, c, ctx, c_ctx, w_mod, b_mod, g_pre_mix, g_post_mix, g_pre_ffn, g_post_ffn, w_in, w_out, lru_conv_w, lru_conv_b, lru_wa, lru_ba, lru_wx, lru_bx, lru_lambda, na_bias, ssd_conv_w, ssd_conv_b, ssd_a_log, ssd_dt_bias, ssd_d, ssd_norm, gdn_conv_w, gdn_a_log, gdn_dt_bias, gdn_norm, router_w, router_b, we_gate, we_up, we_down, ws_gate, ws_up, ws_down):
    bsz, seq, d = x.shape
    n_ctx = ctx.shape[1]
    depth = w_mod.shape[0]
    lat_tpg = seq // min(TOKEN_TILE, seq)
    ctx_tpg = max(bsz * n_ctx // TOKEN_TILE, 1)
    ctx_mpg = max(bsz * n_ctx // MOE_TILE, 1)

    cond = _pad_rows(jnp.concatenate([c, c_ctx[None, :]], axis=0))
    mod = modulation(cond, w_mod, b_mod).reshape(depth, SUBLANES, N_MOD, d)
    rope = rope_tables(seq)
    row = lambda v: v[None, :].astype(F32)

    xl = x.reshape(bsz * seq, d)
    xc = ctx.reshape(bsz * n_ctx, d)
    for l in range(depth):
        last = l == depth - 1
        m_lat = [mod[l, :bsz, k][:, None, :] for k in range(N_MOD)]
        m_ctx = [mod[l, bsz:bsz + 1, k][:, None, :] for k in range(N_MOD)]
        w_in_l = _reorder_w_in(w_in[l]).astype(MXU_DTYPE)
        pc = in_projection(xc, row(g_pre_mix[l]), m_ctx[1], m_ctx[0], w_in_l, ctx_tpg)
        pl_ = in_projection(xl, row(g_pre_mix[l]), m_lat[1], m_lat[0], w_in_l, lat_tpg)

        lru_p = lru_params(lru_conv_w[l], lru_conv_b[l], lru_wa[l], lru_ba[l], lru_wx[l], lru_bx[l], lru_lambda[l])
        a_cf, a_cb, a_st = lru_mixer(pc[P_AX], jnp.zeros((bsz, SUBLANES, GROUP_W), F32), bsz, *lru_p)
        a_lf, a_lb, _ = lru_mixer(pl_[P_AX], a_st, bsz, *lru_p)

        kc = pc[P_BK].reshape(bsz, n_ctx, GROUP_W)
        vc = pc[P_BV].reshape(bsz, n_ctx, GROUP_W)
        b_c = ctx_attention(pc[P_BQ].reshape(bsz, n_ctx, GROUP_W), kc, vc).reshape(bsz * n_ctx, GROUP_W)
        b_l = na_mixer(pl_[P_BQ], pl_[P_BK], pl_[P_BV], kc, vc, na_bias_slabs(na_bias[l]), bsz)

        ssd_p = ssd_params(ssd_conv_w[l], ssd_conv_b[l], ssd_a_log[l], ssd_dt_bias[l])
        c_cf, c_cb, c_cx, c_st = ssd_mixer(pc[P_CX], pc[P_SM], jnp.zeros((bsz, 2, N_HEADS, SSD_STATE, HEAD_DIM), F32),
                                           bsz, *ssd_p)
        c_lf, c_lb, c_lx, _ = ssd_mixer(pl_[P_CX], pl_[P_SM], c_st, bsz, *ssd_p)

        gdn_p = gdn_params(gdn_conv_w[l], gdn_a_log[l], gdn_dt_bias[l])
        d_cf, d_cb, d_st = gdn_mixer(pc[P_DX], pc[P_SM], jnp.zeros((bsz, 2, N_HEADS, HEAD_DIM, HEAD_DIM), F32),
                                     bsz, *gdn_p)
        d_lf, d_lb, _ = gdn_mixer(pl_[P_DX], pl_[P_SM], d_st, bsz, *gdn_p, rope=rope)

        epi = (w_out[l].astype(MXU_DTYPE), row(g_post_mix[l]))
        epi_tail = (row(jnp.repeat(ssd_d[l], HEAD_DIM)), row(ssd_norm[l]), row(jnp.tile(gdn_norm[l], N_HEADS)), router_w[l])
        routed_w = (l, we_gate, we_up, we_down)
        shared_w = (ws_gate[l].astype(MXU_DTYPE), ws_up[l].astype(MXU_DTYPE), ws_down[l].astype(MXU_DTYPE),
                    row(g_post_ffn[l]))

        mix_l = (a_lf, a_lb, pl_[P_AG], b_l, c_lf, c_lb, c_lx, pl_[P_CZ], d_lf, d_lb, pl_[P_DZ])
        xl, _, hp, lg = out_projection(xl, mix_l, *epi, m_lat[2], row(g_pre_ffn[l]), m_lat[4], m_lat[3], *epi_tail, lat_tpg)
        gk, ek, pk, cnt = router_dispatch(lg, router_b[l])
        off, be, nv = moe_plan(cnt, bsz * seq)
        dest = moe_rows(off, ek, pk)
        xs = sc_scatter_rows(hp, dest, be.shape[0] * MOE_BLOCK)
        ys = moe_experts(xs, be, nv, *routed_w)
        yg = sc_gather_rows(ys, dest.reshape(-1)).reshape(TOP_K, bsz * seq, d // 2)
        xl = moe_combine(yg, gk, hp, xl, *shared_w, m_lat[5], seq // min(MOE_ROW_TILE, seq))
        if not last:
            mix_c = (a_cf, a_cb, pc[P_AG], b_c, c_cf, c_cb, c_cx, pc[P_CZ], d_cf, d_cb, pc[P_DZ])
            xc, h2, _, lg = out_projection(xc, mix_c, *epi, m_ctx[2], row(g_pre_ffn[l]), m_ctx[4], m_ctx[3], *epi_tail, ctx_tpg)
            xc = moe_ffn(h2, router_gates(lg, router_b[l]), xc, *routed_w, *shared_w, m_ctx[5], ctx_mpg)
    return xl.reshape(bsz, seq, d)
```

```python
import functools
import math

import jax
import jax.numpy as jnp
import numpy as np
from jax import lax
from jax.experimental import pallas as pl
from jax.experimental.pallas import tpu as pltpu
from jax.experimental.pallas import tpu_sc as plsc

F32 = jnp.float32
MXU_DTYPE = jnp.bfloat16
HI = lax.Precision.HIGHEST

D_MODEL = 1024
GRID_W = 64
GROUP_W = 256
HEAD_DIM = 64
N_HEADS = 4
EPS = 1e-6
NEG_INF = -1e30
N_MOD = 6
LRU_C = 8.0
NA_WIN_ROWS = 8
NA_WIN_COLS = 16
SSD_STATE = 128
SSD_GROUPS = 2
ROPE_BASE = 10000.0
ROPE_AXIS_DIM = HEAD_DIM // 2
N_EXPERTS = 64
N_EXPERT_GROUPS = 8
TOPK_GROUPS = 4
TOP_K = 8
D_EXPERT = 256
ROUTED_SCALE = 2.5

LANES = 128
SUBLANES = 8
VMEM_LIMIT = 56 * 1024 * 1024

TOKEN_TILE = 512
LRU_CHUNK = 256
SSD_CHUNK = 128
GDN_CHUNK = 64
GDN_TILE = 256
GDN_SUB = 128
GDN_BASE = 16
MOE_TILE = 1024
MOE_EB = 4
MOE_BLOCK = 1024
MOE_ROW_TILE = 256
MOE_PLAN_TILE = 2048
SC_WINDOW = 128

P_WIDTHS = (256, 256, 256, 256, 256, 768, 256, 768, 256, 128)
(P_AX, P_AG, P_BQ, P_BK, P_BV, P_CX, P_CZ, P_DX, P_DZ, P_SM) = range(10)
SM_DT, SM_BETA, SM_DECAY = 0, 8, 16


def _cp(*sem):
    return pltpu.CompilerParams(dimension_semantics=sem, vmem_limit_bytes=VMEM_LIMIT)


def _mx(x):
    return x.astype(MXU_DTYPE)


def _dot(a, b):
    return jnp.dot(_mx(a), _mx(b), preferred_element_type=F32)


def _dot_nt(a, b):
    return lax.dot_general(_mx(a), _mx(b), (((1,), (1,)), ((), ())), preferred_element_type=F32)


def _dot_tn(a, b):
    return lax.dot_general(_mx(a), _mx(b), (((0,), (0,)), ((), ())), preferred_element_type=F32)


def _dot_hi(a, b):
    return jnp.dot(a, b, preferred_element_type=F32, precision=HI)


def _sigmoid(x):
    return 1.0 / (1.0 + jnp.exp(-x))


def _silu(x):
    return x * _sigmoid(x)


def _softplus(x):
    return jnp.maximum(x, 0.0) + jnp.log1p(jnp.exp(-jnp.abs(x)))


def _gelu_tanh(x):
    return 0.5 * x * (1.0 + jnp.tanh(math.sqrt(2.0 / math.pi) * (x + 0.044715 * (x * x * x))))


def _rms(x, g):
    return x * lax.rsqrt(jnp.mean(x * x, axis=-1, keepdims=True) + EPS) * g


def _full(shape):
    n = len(shape)
    return pl.BlockSpec(shape, lambda *_: (0,) * n)


MOD_COLS = 1536


def _mod_kernel(c_ref, w_ref, b_ref, o_ref):
    o_ref[0] = _dot_hi(_silu(c_ref[...]), w_ref[0]) + b_ref[0]


def modulation(cond, w_mod, b_mod):
    depth, d, n = w_mod.shape
    return pl.pallas_call(
        _mod_kernel,
        grid=(depth, n // MOD_COLS),
        in_specs=[pl.BlockSpec((SUBLANES, d), lambda l, j: (0, 0)),
                  pl.BlockSpec((1, d, MOD_COLS), lambda l, j: (l, 0, j)),
                  pl.BlockSpec((1, 1, MOD_COLS), lambda l, j: (l, 0, j))],
        out_specs=pl.BlockSpec((1, SUBLANES, MOD_COLS), lambda l, j: (l, 0, j)),
        out_shape=jax.ShapeDtypeStruct((depth, SUBLANES, n), F32),
        compiler_params=_cp("parallel", "parallel"),
        name="modulation",
    )(cond, w_mod, b_mod.reshape(depth, 1, n))


def _inproj_kernel(x_ref, g_ref, sc_ref, sh_ref, w_ref, *o_refs):
    h = _rms(x_ref[...], g_ref[...]) * (1.0 + sc_ref[0]) + sh_ref[0]
    p = _dot(h, w_ref[...])
    off = 0
    for o_ref, w in zip(o_refs, P_WIDTHS):
        o_ref[...] = p[:, off:off + w]
        off += w


def in_projection(x, g, sc, sh, w, tiles_per_group):
    t, d = x.shape
    tm = min(TOKEN_TILE, t)
    vec = lambda i: (i // tiles_per_group, 0, 0)
    return pl.pallas_call(
        _inproj_kernel,
        grid=(t // tm,),
        in_specs=[pl.BlockSpec((tm, d), lambda i: (i, 0)),
                  _full((1, d)),
                  pl.BlockSpec((1, 1, d), vec),
                  pl.BlockSpec((1, 1, d), vec),
                  _full(w.shape)],
        out_specs=[pl.BlockSpec((tm, wd), lambda i: (i, 0)) for wd in P_WIDTHS],
        out_shape=[jax.ShapeDtypeStruct((t, wd), F32) for wd in P_WIDTHS],
        compiler_params=_cp("parallel"),
        name="in_projection",
    )(x, g, sc, sh, w)


def _halo(p, bsz, q):
    c = p.shape[-1]
    pr = p.reshape(bsz, -1, q, c)
    nc = pr.shape[1]
    prev = jnp.concatenate([jnp.zeros((bsz, 1, 2, c), p.dtype), pr[:, :-1, q - 2:, :]], axis=1)
    nxt = jnp.concatenate([pr[:, 1:, :1, :], jnp.zeros((bsz, 1, 1, c), p.dtype)], axis=1)
    pad = jnp.zeros((bsz, nc, SUBLANES - 3, c), p.dtype)
    return jnp.concatenate([prev, nxt, pad], axis=2).reshape(bsz * nc, SUBLANES, c)


def _dwconv(x, halo, w, b=None):
    q = x.shape[0]
    row = lax.broadcasted_iota(jnp.int32, x.shape, 0)
    xm2 = jnp.where(row == 0, halo[0:1], jnp.where(row == 1, halo[1:2], pltpu.roll(x, 2, 0)))
    xm1 = jnp.where(row == 0, halo[1:2], pltpu.roll(x, 1, 0))
    xp1 = jnp.where(row == q - 1, halo[2:3], pltpu.roll(x, q - 1, 0))
    y = w[0:1] * xm2 + w[1:2] * xm1 + w[2:3] * x + w[3:4] * xp1
    return y if b is None else y + b


def _pad_rows(a, rows=SUBLANES):
    return jnp.concatenate([a, jnp.zeros((rows - a.shape[0],) + a.shape[1:], a.dtype)], axis=0)


def _chunk_specs(nc, q, c):
    fwd = pl.BlockSpec((q, c), lambda b, i: (b * nc + i, 0))
    bwd = pl.BlockSpec((q, c), lambda b, i: (b * nc + nc - 1 - i, 0))
    return fwd, bwd


def _halo_specs(nc, c):
    fwd = pl.BlockSpec((1, SUBLANES, c), lambda b, i: (b * nc + i, 0, 0))
    bwd = pl.BlockSpec((1, SUBLANES, c), lambda b, i: (b * nc + nc - 1 - i, 0, 0))
    return fwd, bwd


def _lru_kernel(xf_ref, xb_ref, hf_ref, hb_ref, h0_ref, cw_ref, cb_ref, wg_ref, bg_ref, lam_ref,
                yf_ref, yb_ref, hfin_ref, af_s, bf_s, ab_s, bb_s, carry_s):
    i = pl.program_id(1)
    q = xf_ref.shape[0]

    @pl.when(i == 0)
    def _():
        carry_s[...] = h0_ref[0]

    def coeffs(x_ref, halo_ref, d, a_s, b_s):
        u = _dwconv(x_ref[...], halo_ref[0], cw_ref[...], cb_ref[...])
        g = _dot(u, wg_ref[:, 2 * GROUP_W * d:2 * GROUP_W * (d + 1)]) + bg_ref[:, 2 * GROUP_W * d:2 * GROUP_W * (d + 1)]
        r = _sigmoid(g[:, :GROUP_W])
        gate_in = _sigmoid(g[:, GROUP_W:])
        log_a = -LRU_C * r * _softplus(-lam_ref[d:d + 1, :])
        a_s[...] = jnp.exp(log_a)
        b_s[...] = jnp.sqrt(1.0 - jnp.exp(2.0 * log_a)) * (gate_in * u)

    coeffs(xf_ref, hf_ref, 0, af_s, bf_s)
    coeffs(xb_ref, hb_ref, 1, ab_s, bb_s)

    ng = q // SUBLANES
    row = lax.broadcasted_iota(jnp.int32, (SUBLANES, GROUP_W), 0)

    def body(g, hs):
        h_f, h_b = hs
        i0 = pl.multiple_of(g * SUBLANES, SUBLANES)
        a = af_s[pl.ds(i0, SUBLANES), :]
        b = bf_s[pl.ds(i0, SUBLANES), :]
        for s in (1, 2, 4):
            m = row >= s
            b = jnp.where(m, a * pltpu.roll(b, s, 0) + b, b)
            a = jnp.where(m, a * pltpu.roll(a, s, 0), a)
        h = b + a * h_f
        yf_ref[pl.ds(i0, SUBLANES), :] = h
        h_f = h[SUBLANES - 1:SUBLANES, :]
        j0 = pl.multiple_of((ng - 1 - g) * SUBLANES, SUBLANES)
        a = ab_s[pl.ds(j0, SUBLANES), :]
        b = bb_s[pl.ds(j0, SUBLANES), :]
        for s in (1, 2, 4):
            m = row < SUBLANES - s
            b = jnp.where(m, a * pltpu.roll(b, SUBLANES - s, 0) + b, b)
            a = jnp.where(m, a * pltpu.roll(a, SUBLANES - s, 0), a)
        h = b + a * h_b
        yb_ref[pl.ds(j0, SUBLANES), :] = h
        return h_f, h[0:1, :]

    h_f, h_b = lax.fori_loop(0, ng, body, (carry_s[0:1, :], carry_s[1:2, :]))
    carry_s[0:1, :] = h_f
    carry_s[1:2, :] = h_b

    @pl.when(i == pl.num_programs(1) - 1)
    def _():
        hfin_ref[0] = carry_s[...]


def _block_diag(w):
    h, a, b = w.shape
    return jnp.einsum('hij,hg->higj', w, jnp.eye(h, dtype=w.dtype)).reshape(h * a, h * b)


def lru_params(conv_w, conv_b, wa, ba, wx, bx, lam):
    wg = jnp.concatenate([_block_diag(wa[0]), _block_diag(wx[0]), _block_diag(wa[1]), _block_diag(wx[1])], axis=1)
    bg = jnp.concatenate([ba[0], bx[0], ba[1], bx[1]])[None, :]
    return _pad_rows(conv_w), conv_b[None, :], wg.astype(MXU_DTYPE), bg, _pad_rows(lam)


def lru_mixer(x, h0, bsz, cw, cb, wg, bg, lam):
    t, c = x.shape
    s = t // bsz
    q = min(LRU_CHUNK, s)
    nc = s // q
    halo = _halo(x, bsz, q)
    xf, xb = _chunk_specs(nc, q, c)
    hf, hb = _halo_specs(nc, c)
    st = pl.BlockSpec((1, SUBLANES, c), lambda b, i: (b, 0, 0))
    return pl.pallas_call(
        _lru_kernel,
        grid=(bsz, nc),
        in_specs=[xf, xb, hf, hb, st, _full(cw.shape), _full(cb.shape), _full(wg.shape), _full(bg.shape),
                  _full(lam.shape)],
        out_specs=[xf, xb, st],
        out_shape=[jax.ShapeDtypeStruct((t, c), F32), jax.ShapeDtypeStruct((t, c), F32),
                   jax.ShapeDtypeStruct((bsz, SUBLANES, c), F32)],
        scratch_shapes=[pltpu.VMEM((q, c), F32)] * 4 + [pltpu.VMEM((SUBLANES, c), F32)],
        compiler_params=_cp("parallel", "arbitrary"),
        name="lru_mixer",
    )(x, x, halo, halo, h0, cw, cb, wg, bg, lam)


NA_KEYS = NA_WIN_ROWS * GRID_W
NA_ROW_BLOCK = 8


def na_bias_slabs(table):
    qc = np.arange(GRID_W)[:, None]
    kc = np.arange(GRID_W)[None, :]
    win0 = np.clip(qc - NA_WIN_COLS // 2, 0, GRID_W - NA_WIN_COLS)
    ok = (kc >= win0) & (kc < win0 + NA_WIN_COLS)
    dc = np.clip(kc - qc + NA_WIN_COLS - 1, 0, 2 * NA_WIN_COLS - 2)
    dr = np.arange(NA_WIN_ROWS)[:, None] + np.arange(NA_WIN_ROWS)[None, :]
    b = table.astype(F32)[:, dr][:, :, :, dc]
    b = jnp.where(ok[None, None, None], b, NEG_INF)
    h = table.shape[0]
    return b.transpose(0, 1, 3, 2, 4).reshape(h, NA_WIN_ROWS, GRID_W, NA_KEYS)


def _na_span_start(j, rows):
    return jnp.clip(j * NA_ROW_BLOCK - NA_WIN_ROWS // 2, 0, rows - (NA_ROW_BLOCK + NA_WIN_ROWS - 1))


def _na_kernel(q_ref, kw_ref, vw_ref, kc_ref, vc_ref, slab_ref, o_ref, *, rows):
    j = pl.program_id(1)
    ustart = _na_span_start(j, rows)
    q = q_ref[...] * (HEAD_DIM ** -0.5)
    kc, vc = kc_ref[0], vc_ref[0]
    heads = [slice(h * HEAD_DIM, (h + 1) * HEAD_DIM) for h in range(N_HEADS)]
    qrows = [slice(i * GRID_W, (i + 1) * GRID_W) for i in range(NA_ROW_BLOCK)]
    kws, vws, offs = [], [], []
    for i in range(NA_ROW_BLOCK):
        r = j * NA_ROW_BLOCK + i
        r0 = jnp.clip(r - NA_WIN_ROWS // 2, 0, rows - NA_WIN_ROWS)
        start = pl.multiple_of((r0 - ustart) * GRID_W, GRID_W)
        kws.append(kw_ref[pl.ds(start, NA_KEYS), :])
        vws.append(vw_ref[pl.ds(start, NA_KEYS), :])
        offs.append(r0 - r + NA_WIN_ROWS - 1)
    s_ctx = [_dot_nt(q[:, sl], kc[:, sl]) for sl in heads]
    s_loc = [[_dot_nt(q[qr, sl], kws[i][:, sl]) + slab_ref[h, offs[i]] for h, sl in enumerate(heads)]
             for i, qr in enumerate(qrows)]
    m = [[jnp.maximum(jnp.max(s_loc[i][h], axis=-1, keepdims=True), jnp.max(s_ctx[h][qr], axis=-1, keepdims=True))
          for h in range(N_HEADS)] for i, qr in enumerate(qrows)]
    p_loc = [[jnp.exp(s_loc[i][h] - m[i][h]) for h in range(N_HEADS)] for i in range(NA_ROW_BLOCK)]
    p_ctx = [jnp.exp(s_ctx[h] - jnp.concatenate([m[i][h] for i in range(NA_ROW_BLOCK)], axis=0))
             for h in range(N_HEADS)]
    o_ctx = [_dot(p_ctx[h], vc[:, sl]) for h, sl in enumerate(heads)]
    rows_out = []
    for i, qr in enumerate(qrows):
        outs = []
        for h, sl in enumerate(heads):
            den = jnp.sum(p_loc[i][h], axis=-1, keepdims=True) + jnp.sum(p_ctx[h][qr], axis=-1, keepdims=True)
            outs.append((_dot(p_loc[i][h], vws[i][:, sl]) + o_ctx[h][qr]) / den)
        rows_out.append(jnp.concatenate(outs, axis=1))
    o_ref[...] = jnp.concatenate(rows_out, axis=0)


def na_mixer(q, k, v, kc, vc, slabs, bsz):
    t, c = q.shape
    s = t // bsz
    rows = s // GRID_W
    n_ctx = kc.shape[1]
    span = (NA_ROW_BLOCK + NA_WIN_ROWS - 1) * GRID_W

    def win(b, j):
        return ((b * rows + _na_span_start(j, rows)) * GRID_W, 0)

    wspec = pl.BlockSpec((pl.Element(span), pl.Element(c)), win)
    cspec = pl.BlockSpec((1, n_ctx, c), lambda b, j: (b, 0, 0))
    qspec = pl.BlockSpec((NA_ROW_BLOCK * GRID_W, c), lambda b, j: (b * (rows // NA_ROW_BLOCK) + j, 0))
    return pl.pallas_call(
        functools.partial(_na_kernel, rows=rows),
        grid=(bsz, rows // NA_ROW_BLOCK),
        in_specs=[qspec, wspec, wspec, cspec, cspec, _full(slabs.shape)],
        out_specs=qspec,
        out_shape=jax.ShapeDtypeStruct((t, c), F32),
        compiler_params=_cp("parallel", "arbitrary"),
        name="na_mixer",
    )(q, k, v, kc, vc, slabs)


def _ctx_attn_kernel(q_ref, k_ref, v_ref, o_ref):
    q = q_ref[0] * (HEAD_DIM ** -0.5)
    k, v = k_ref[0], v_ref[0]
    outs = []
    for h in range(N_HEADS):
        sl = slice(h * HEAD_DIM, (h + 1) * HEAD_DIM)
        s = _dot_nt(q[:, sl], k[:, sl])
        p = jnp.exp(s - jnp.max(s, axis=-1, keepdims=True))
        outs.append(_dot(p, v[:, sl]) / jnp.sum(p, axis=-1, keepdims=True))
    o_ref[0] = jnp.concatenate(outs, axis=1)


def ctx_attention(q, k, v):
    spec = pl.BlockSpec((1,) + q.shape[1:], lambda b: (b, 0, 0))
    return pl.pallas_call(
        _ctx_attn_kernel,
        grid=(q.shape[0],),
        in_specs=[spec, spec, spec],
        out_specs=spec,
        out_shape=jax.ShapeDtypeStruct(q.shape, F32),
        compiler_params=_cp("parallel"),
        name="ctx_attention",
    )(q, k, v)


def _small_vec(vals, off):
    v = jnp.zeros((LANES,), F32).at[off:off + 2 * N_HEADS].set(vals.reshape(-1).astype(F32))
    return v[None, :]


def _lane_mask(off):
    lane = lax.broadcasted_iota(jnp.int32, (1, LANES), 1)
    return (lane >= off) & (lane < off + 2 * N_HEADS)


def _tri_masks(q):
    rowi = lax.broadcasted_iota(jnp.int32, (q, q), 0)
    coli = lax.broadcasted_iota(jnp.int32, (q, q), 1)
    return rowi, coli


def _ssd_kernel(xf_ref, xb_ref, hf_ref, hb_ref, sf_ref, sb_ref, h0_ref, cw_ref, cb_ref, dtb_ref, alog_ref,
                yf_ref, yb_ref, xc_ref, hfin_ref, state_s):
    i = pl.program_id(1)
    q = xf_ref.shape[0]

    @pl.when(i == 0)
    def _():
        state_s[...] = h0_ref[0]

    rowi, coli = _tri_masks(q)
    a_neg = jnp.where(_lane_mask(SM_DT), -jnp.exp(alog_ref[...]), 0.0)

    def direction(x_ref, halo_ref, sm_ref, d, y_ref):
        xbc = _silu(_dwconv(x_ref[...], halo_ref[0], cw_ref[...], cb_ref[...]))
        if d == 0:
            xc_ref[...] = xbc[:, :GROUP_W]
        dt = _softplus(sm_ref[...] + dtb_ref[...])
        keep = (rowi >= coli) if d == 0 else (rowi <= coli)
        acum = _dot_hi(keep.astype(F32), dt * a_neg)
        acum_t = acum.T
        last = acum[q - 1:q, :] if d == 0 else acum[0:1, :]
        dec_end = jnp.exp(last - acum)
        e_acum = jnp.exp(acum)
        e_last = jnp.exp(last)
        ys = []
        for g in range(SSD_GROUPS):
            bg = xbc[:, GROUP_W + SSD_STATE * g:GROUP_W + SSD_STATE * (g + 1)]
            cg = xbc[:, GROUP_W + SSD_STATE * (SSD_GROUPS + g):GROUP_W + SSD_STATE * (SSD_GROUPS + g + 1)]
            cbt = _dot_nt(cg, bg)
            for hh in range(N_HEADS // SSD_GROUPS):
                h = g * (N_HEADS // SSD_GROUPS) + hh
                ln = SM_DT + N_HEADS * d + h
                lmat = jnp.exp(jnp.where(keep, acum[:, ln:ln + 1] - acum_t[ln:ln + 1, :], NEG_INF))
                xdt = xbc[:, h * HEAD_DIM:(h + 1) * HEAD_DIM] * dt[:, ln:ln + 1]
                st = state_s[d, h]
                ys.append(_dot(cbt * lmat, xdt) + _dot(cg * e_acum[:, ln:ln + 1], st))
                state_s[d, h] = st * e_last[:, ln:ln + 1] + _dot_tn(bg * dec_end[:, ln:ln + 1], xdt)
        y_ref[...] = jnp.concatenate(ys, axis=1)

    direction(xf_ref, hf_ref, sf_ref, 0, yf_ref)
    direction(xb_ref, hb_ref, sb_ref, 1, yb_ref)

    @pl.when(i == pl.num_programs(1) - 1)
    def _():
        hfin_ref[0] = state_s[...]


def ssd_params(conv_w, conv_b, a_log, dt_bias):
    return _pad_rows(conv_w), conv_b[None, :], _small_vec(dt_bias, SM_DT), _small_vec(a_log, SM_DT)


def ssd_mixer(xbc, sm, h0, bsz, cw, cb, dtb, alog):
    t, c = xbc.shape
    s = t // bsz
    q = min(SSD_CHUNK, s)
    nc = s // q
    halo = _halo(xbc, bsz, q)
    xf, xb = _chunk_specs(nc, q, c)
    hf, hb = _halo_specs(nc, c)
    sf, sb = _chunk_specs(nc, q, LANES)
    yf, yb = _chunk_specs(nc, q, GROUP_W)
    st = pl.BlockSpec((1,) + h0.shape[1:], lambda b, i: (b, 0, 0, 0, 0))
    y_shape = jax.ShapeDtypeStruct((t, GROUP_W), F32)
    return pl.pallas_call(
        _ssd_kernel,
        grid=(bsz, nc),
        in_specs=[xf, xb, hf, hb, sf, sb, st, _full(cw.shape), _full(cb.shape), _full(dtb.shape), _full(alog.shape)],
        out_specs=[yf, yb, yf, st],
        out_shape=[y_shape, y_shape, y_shape, jax.ShapeDtypeStruct(h0.shape, F32)],
        scratch_shapes=[pltpu.VMEM(h0.shape[1:], F32)],
        compiler_params=_cp("parallel", "arbitrary"),
        name="ssd_mixer",
    )(xbc, xbc, halo, halo, sm, sm, h0, cw, cb, dtb, alog)


def rope_tables(seq):
    t = jnp.arange(seq)
    row = (t // GRID_W).astype(F32)
    col = (t % GRID_W).astype(F32)
    inv = ROPE_BASE ** (-jnp.arange(0, ROPE_AXIS_DIM, 2, dtype=F32) / ROPE_AXIS_DIM)
    ar, ac = row[:, None] * inv, col[:, None] * inv
    cos = jnp.concatenate([jnp.cos(ar), jnp.cos(ar), jnp.cos(ac), jnp.cos(ac)], axis=1)
    sin = jnp.concatenate([-jnp.sin(ar), jnp.sin(ar), -jnp.sin(ac), jnp.sin(ac)], axis=1)
    return jnp.tile(cos, (1, N_HEADS)), jnp.tile(sin, (1, N_HEADS))


def _swap16(x):
    lane = lax.broadcasted_iota(jnp.int32, x.shape, 1)
    half = ROPE_AXIS_DIM // 2
    return jnp.where((lane & (ROPE_AXIS_DIM - 1)) < half,
                     pltpu.roll(x, x.shape[1] - half, 1), pltpu.roll(x, half, 1))


def _l2norm_heads(x):
    outs = []
    for h in range(N_HEADS):
        xh = x[:, h * HEAD_DIM:(h + 1) * HEAD_DIM]
        outs.append(xh * lax.rsqrt(jnp.sum(xh * xh, axis=-1, keepdims=True) + EPS))
    return jnp.concatenate(outs, axis=1)


def _dot_tri(mask, x):
    m = _mx(mask.astype(F32))
    x1 = _mx(x)
    r1 = x - x1.astype(F32)
    x2 = _mx(r1)
    x3 = _mx(r1 - x2.astype(F32))
    return (jnp.dot(m, x1, preferred_element_type=F32) + jnp.dot(m, x2, preferred_element_type=F32)
            + jnp.dot(m, x3, preferred_element_type=F32))


def _same_block(rowi, coli, n):
    sh = n.bit_length() - 1
    return (rowi >> sh) == (coli >> sh)


def _solve_unit_tri(a_list, rhs_list, rowi, coli, chunk):
    mm = lambda x, y: jnp.dot(x, y, preferred_element_type=F32)
    eye = (rowi == coli).astype(F32)
    in_base = _same_block(rowi, coli, GDN_BASE)
    base = [_mx(jnp.where(in_base, a, 0.0)) for a in a_list]
    ts = [jnp.where(in_base, eye - a, 0.0) for a in a_list]
    ps = [_mx(mm(b, b)) for b in base]
    ts = [t + mm(_mx(t), p) for t, p in zip(ts, ps)]
    n = 4
    while n < GDN_BASE:
        ps = [_mx(mm(p, p)) for p in ps]
        ts = [t + mm(_mx(t), p) for t, p in zip(ts, ps)]
        n *= 2
    n = GDN_BASE
    while 2 * n < chunk:
        inner = _same_block(rowi, coli, 2 * n) & jnp.logical_not(_same_block(rowi, coli, n))
        offs = [_mx(jnp.where(inner, a, 0.0)) for a in a_list]
        tb = [_mx(t) for t in ts]
        ms = [_mx(mm(t, off)) for t, off in zip(tb, offs)]
        ts = [t - mm(m, t_b) for t, m, t_b in zip(ts, ms, tb)]
        n *= 2
    outer = jnp.logical_not(_same_block(rowi, coli, n))
    offs = [_mx(jnp.where(outer, a, 0.0)) for a in a_list]
    tb = [_mx(t) for t in ts]
    ys = [mm(t, _mx(r)) for t, r in zip(tb, rhs_list)]
    zs = [_mx(mm(off, _mx(y))) for off, y in zip(offs, ys)]
    return [y - mm(t, z) for y, t, z in zip(ys, tb, zs)]


def _gdn_kernel(*refs, rope):
    if rope:
        (xf_ref, xb_ref, hf_ref, hb_ref, sf_ref, sb_ref, cf_ref, cb_ref, nf_ref, nb_ref,
         s0_ref, cw_ref, alog_ref, dtb_ref, of_ref, ob_ref, sfin_ref, state_s) = refs
    else:
        (xf_ref, xb_ref, hf_ref, hb_ref, sf_ref, sb_ref,
         s0_ref, cw_ref, alog_ref, dtb_ref, of_ref, ob_ref, sfin_ref, state_s) = refs
        cf_ref = cb_ref = nf_ref = nb_ref = None
    i = pl.program_id(1)
    tq = xf_ref.shape[0]
    ck = min(GDN_CHUNK, tq)
    nck = tq // ck

    @pl.when(i == 0)
    def _():
        state_s[...] = s0_ref[0]

    sub = min(GDN_SUB, tq)
    nsub = tq // sub
    rowt, colt = _tri_masks(tq)
    in_chunk_t = _same_block(rowt, colt, ck)
    rowi, coli = _tri_masks(sub)
    in_chunk = _same_block(rowi, coli, ck)
    a_neg = jnp.where(_lane_mask(SM_DECAY), -jnp.exp(alog_ref[...]), 0.0)

    a_list, rhs_list, qkm, qg, kd, e_last = [], [], [], [], [], []
    for d, (x_ref, halo_ref, sm_ref, cos_ref, sin_ref) in enumerate(
            ((xf_ref, hf_ref, sf_ref, cf_ref, nf_ref), (xb_ref, hb_ref, sb_ref, cb_ref, nb_ref))):
        qkv = _silu(_dwconv(x_ref[...], halo_ref[0], cw_ref[...]))
        qn = _l2norm_heads(qkv[:, :GROUP_W])
        kn = _l2norm_heads(qkv[:, GROUP_W:2 * GROUP_W])
        v = qkv[:, 2 * GROUP_W:]
        if rope:
            cos, sin = cos_ref[...], sin_ref[...]
            qn = qn * cos + _swap16(qn) * sin
            kn = kn * cos + _swap16(kn) * sin
        qn = qn * (HEAD_DIM ** -0.5)
        sm = sm_ref[...]
        beta = _sigmoid(sm)
        keep_t = in_chunk_t & ((rowt >= colt) if d == 0 else (rowt <= colt))
        keep = in_chunk & ((rowi >= coli) if d == 0 else (rowi <= coli))
        strict = in_chunk & ((rowi > coli) if d == 0 else (rowi < coli))
        gc = _dot_tri(keep_t, _softplus(sm + dtb_ref[...]) * a_neg)
        gc_t = gc.T
        edge = ck - 1 if d == 0 else 0
        last = jnp.concatenate([jnp.broadcast_to(gc[c * ck + edge:c * ck + edge + 1, :], (ck, LANES))
                                for c in range(nck)], axis=0)
        e_gc = jnp.exp(gc)
        e_end = jnp.exp(last - gc)
        e_last.append(jnp.exp(last))
        for h in range(N_HEADS):
            sl = slice(h * HEAD_DIM, (h + 1) * HEAD_DIM)
            lg = SM_DECAY + N_HEADS * d + h
            lb = SM_BETA + N_HEADS * d + h
            qh, kh, bcol = qn[:, sl], kn[:, sl], beta[:, lb:lb + 1]
            kb = kh * bcol
            rhs = jnp.concatenate([v[:, sl] * bcol, kb * e_gc[:, lg:lg + 1]], axis=1)
            qg.append(qh * e_gc[:, lg:lg + 1])
            kd.append(kh * e_end[:, lg:lg + 1])
            for s in range(nsub):
                rs = slice(s * sub, (s + 1) * sub)
                decay = jnp.exp(jnp.where(keep, gc[rs, lg:lg + 1] - gc_t[lg:lg + 1, rs], NEG_INF))
                a_list.append(jnp.where(strict, _dot_nt(kb[rs], kh[rs]) * decay, 0.0))
                rhs_list.append(rhs[rs])
                qkm.append(_dot_nt(qh[rs], kh[rs]) * decay)
    sols = _solve_unit_tri(a_list, rhs_list, rowi, coli, ck)
    sols = [jnp.concatenate(sols[n * nsub:(n + 1) * nsub], axis=0) for n in range(2 * N_HEADS)]

    chains = [(d, h) for d in range(2) for h in range(N_HEADS)]
    states = [state_s[d, h] for d, h in chains]
    v_new = [[None] * nck for _ in chains]
    o_st = [[None] * nck for _ in chains]
    for step in range(nck):
        rows = [slice((step if d == 0 else nck - 1 - step) * ck, (step if d == 0 else nck - 1 - step) * ck + ck)
                for d, _ in chains]
        ms = [_dot(jnp.concatenate([sols[n][r, HEAD_DIM:], qg[n][r]], axis=0), states[n])
              for n, r in enumerate(rows)]
        for n, (d, _) in enumerate(chains):
            c = step if d == 0 else nck - 1 - step
            v_new[n][c] = sols[n][rows[n], :HEAD_DIM] - ms[n][:ck]
            o_st[n][c] = ms[n][ck:]
        ups = [_dot_tn(kd[n][r], v_new[n][step if chains[n][0] == 0 else nck - 1 - step])
               for n, r in enumerate(rows)]
        for n, (d, h) in enumerate(chains):
            lg = SM_DECAY + N_HEADS * d + h
            states[n] = states[n] * e_last[d][rows[n].start:rows[n].start + 1, lg:lg + 1] + ups[n]
    cps = sub // ck
    outs = [jnp.concatenate(o_st[n], axis=0)
            + jnp.concatenate([_dot(qkm[n * nsub + s], jnp.concatenate(v_new[n][s * cps:(s + 1) * cps], axis=0))
                               for s in range(nsub)], axis=0)
            for n in range(len(chains))]
    of_ref[...] = jnp.concatenate(outs[:N_HEADS], axis=1)
    ob_ref[...] = jnp.concatenate(outs[N_HEADS:], axis=1)
    for n, (d, h) in enumerate(chains):
        state_s[d, h] = states[n]

    @pl.when(i == pl.num_programs(1) - 1)
    def _():
        sfin_ref[0] = state_s[...]


def gdn_params(conv_w, a_log, dt_bias):
    return _pad_rows(conv_w), _small_vec(a_log, SM_DECAY), _small_vec(dt_bias, SM_DECAY)


def gdn_mixer(qkv, sm, s0, bsz, cw, alog, dtb, rope=None):
    t, c = qkv.shape
    s = t // bsz
    q = min(GDN_TILE, s)
    nc = s // q
    halo = _halo(qkv, bsz, q)
    xf, xb = _chunk_specs(nc, q, c)
    hf, hb = _halo_specs(nc, c)
    sf, sb = _chunk_specs(nc, q, LANES)
    of, ob = _chunk_specs(nc, q, GROUP_W)
    st = pl.BlockSpec((1,) + s0.shape[1:], lambda b, i: (b, 0, 0, 0, 0))
    ins = [qkv, qkv, halo, halo, sm, sm]
    specs = [xf, xb, hf, hb, sf, sb]
    if rope is not None:
        tf = pl.BlockSpec((q, GROUP_W), lambda b, i: (i, 0))
        tb = pl.BlockSpec((q, GROUP_W), lambda b, i: (nc - 1 - i, 0))
        ins += [rope[0], rope[0], rope[1], rope[1]]
        specs += [tf, tb, tf, tb]
    ins += [s0, cw, alog, dtb]
    specs += [st, _full(cw.shape), _full(alog.shape), _full(dtb.shape)]
    o_shape = jax.ShapeDtypeStruct((t, GROUP_W), F32)
    return pl.pallas_call(
        functools.partial(_gdn_kernel, rope=rope is not None),
        grid=(bsz, nc),
        in_specs=specs,
        out_specs=[of, ob, st],
        out_shape=[o_shape, o_shape, jax.ShapeDtypeStruct(s0.shape, F32)],
        scratch_shapes=[pltpu.VMEM(s0.shape[1:], F32)],
        compiler_params=_cp("parallel", "arbitrary"),
        name="gdn_mixer",
    )(*ins)


def _split_hi_lo(a):
    hi = _mx(a)
    return hi, _mx(a - hi.astype(F32))


def _outproj_kernel(x_ref, ahf_ref, ahb_ref, ag_ref, bo_ref, cyf_ref, cyb_ref, cxc_ref, cz_ref,
                    dof_ref, dob_ref, dz_ref, wout_ref, gpost_ref, ga1_ref, gpre_ref, sc2_ref, sh2_ref,
                    dskip_ref, cnorm_ref, dnorm_ref, rhi_ref, rlo_ref, xo_ref, hp_ref, lg_ref):
    m_a = (ahf_ref[...] + ahb_ref[...]) * _gelu_tanh(ag_ref[...])
    y_c = (cyf_ref[...] + cyb_ref[...] + cxc_ref[...] * dskip_ref[...]) * _silu(cz_ref[...])
    m_c = _rms(y_c, cnorm_ref[...])
    o_d = dof_ref[...] + dob_ref[...]
    heads = []
    for h in range(N_HEADS):
        oh = o_d[:, h * HEAD_DIM:(h + 1) * HEAD_DIM]
        heads.append(oh * lax.rsqrt(jnp.mean(oh * oh, axis=-1, keepdims=True) + EPS))
    m_d = jnp.concatenate(heads, axis=1) * dnorm_ref[...] * _silu(dz_ref[...])
    mix = jnp.concatenate([_mx(m_a), _mx(bo_ref[...]), _mx(m_c), _mx(m_d)], axis=1)
    ml = jnp.dot(mix, wout_ref[...], preferred_element_type=F32)
    x_new = x_ref[...] + ga1_ref[0] * _rms(ml, gpost_ref[...])
    xo_ref[...] = x_new
    h2 = _rms(x_new, gpre_ref[...]) * (1.0 + sc2_ref[0]) + sh2_ref[0]
    hi, lo = _split_hi_lo(h2)
    hp_ref[...] = _pack_pairs(h2)
    rhi = rhi_ref[...]
    lg_ref[...] = (jnp.dot(hi, rhi, preferred_element_type=F32) + jnp.dot(lo, rhi, preferred_element_type=F32)
                   + jnp.dot(hi, rlo_ref[...], preferred_element_type=F32))


def out_projection(x, mixers, w_out, gpost, ga1, gpre, sc2, sh2, dskip, cnorm, dnorm, router_w, tiles_per_group):
    t, d = x.shape
    tm = min(TOKEN_TILE, t)
    vec = lambda i: (i // tiles_per_group, 0, 0)
    row = lambda w: pl.BlockSpec((tm, w), lambda i: (i, 0))
    ne = LANES
    rhi, rlo = _split_hi_lo(jnp.pad(router_w.astype(F32), ((0, 0), (0, ne - router_w.shape[1]))))
    return pl.pallas_call(
        _outproj_kernel,
        grid=(t // tm,),
        in_specs=[row(d)] + [row(GROUP_W)] * 11
                 + [_full(w_out.shape), _full((1, d)), pl.BlockSpec((1, 1, d), vec), _full((1, d)),
                    pl.BlockSpec((1, 1, d), vec), pl.BlockSpec((1, 1, d), vec),
                    _full((1, GROUP_W)), _full((1, GROUP_W)), _full((1, GROUP_W)), _full(rhi.shape), _full(rlo.shape)],
        out_specs=[row(d), row(d // 2), row(ne)],
        out_shape=[jax.ShapeDtypeStruct((t, d), F32), jax.ShapeDtypeStruct((t, d // 2), jnp.uint32),
                   jax.ShapeDtypeStruct((t, ne), F32)],
        compiler_params=_cp("parallel"),
        name="out_projection",
    )(x, *mixers, w_out, gpost, ga1, gpre, sc2, sh2, dskip, cnorm, dnorm, rhi, rlo)


def _rank_before(vals, idx, count, stride):
    rank = jnp.zeros(vals.shape, jnp.int32)
    for j in range(count):
        other = vals[j * stride:j * stride + 1, :]
        ahead = (other > vals) | ((other == vals) & (idx > j))
        rank = rank + ahead.astype(jnp.int32)
    return rank


def _xor_partner(x, row, s):
    n = x.shape[0]
    return jnp.where((row & s) == 0, pltpu.roll(x, n - s, 0), pltpu.roll(x, s, 0))


def _route(logits, router_b):
    ne = N_EXPERTS
    gsz = ne // N_EXPERT_GROUPS
    scores = _sigmoid(logits.T[:ne, :])
    tm = scores.shape[1]
    biased = scores + router_b
    row = lax.broadcasted_iota(jnp.int32, (ne, tm), 0)
    m1, m2 = biased, jnp.full((ne, tm), -jnp.inf, F32)
    s = 1
    while s < gsz:
        o1, o2 = _xor_partner(m1, row, s), _xor_partner(m2, row, s)
        m2 = jnp.maximum(jnp.minimum(m1, o1), jnp.maximum(m2, o2))
        m1 = jnp.maximum(m1, o1)
        s *= 2
    gidx = row >> (gsz.bit_length() - 1)
    group_ok = _rank_before(m1 + m2, gidx, N_EXPERT_GROUPS, gsz) < TOPK_GROUPS
    choice = jnp.where(group_ok, biased, -jnp.inf)
    rank = _rank_before(choice, row, ne, 1)
    gate = jnp.where(rank < TOP_K, scores, 0.0)
    gate = gate / jnp.sum(gate, axis=0, keepdims=True) * ROUTED_SCALE
    return gate, rank, row


def _to_token_major(x):
    n, tm = x.shape
    return jnp.concatenate([x, jnp.zeros((LANES - n, tm), x.dtype)], axis=0).T


def _router_kernel(lg_ref, rb_ref, gate_ref):
    gate, _, _ = _route(lg_ref[...], rb_ref[...])
    gate_ref[...] = _to_token_major(gate)


def _router_dispatch_kernel(lg_ref, rb_ref, gk_ref, ek_ref, pk_ref, cnt_ref, carry_s):
    i = pl.program_id(0)

    @pl.when(i == 0)
    def _():
        carry_s[...] = jnp.zeros(carry_s.shape, F32)

    gate, rank, row = _route(lg_ref[...], rb_ref[...])
    tm = gate.shape[1]
    picked = (rank < TOP_K).astype(F32)
    before = lax.broadcasted_iota(jnp.int32, (tm, tm), 0) < lax.broadcasted_iota(jnp.int32, (tm, tm), 1)
    pos = _dot(picked, before.astype(F32)) + carry_s[:, 0:1]
    carry_s[...] = carry_s[...] + jnp.sum(picked, axis=1, keepdims=True)
    gk, ek, pk = [], [], []
    for k in range(TOP_K):
        sel = rank == k
        gk.append(jnp.sum(jnp.where(sel, gate, 0.0), axis=0, keepdims=True))
        ek.append(jnp.sum(jnp.where(sel, row, 0), axis=0, keepdims=True))
        pk.append(jnp.sum(jnp.where(sel, pos, 0.0), axis=0, keepdims=True))
    gk_ref[...] = _to_token_major(jnp.concatenate(gk, axis=0))
    ek_ref[...] = jnp.concatenate(ek, axis=0)
    pk_ref[...] = jnp.concatenate(pk, axis=0).astype(jnp.int32)

    @pl.when(i == pl.num_programs(0) - 1)
    def _():
        cnt_ref[...] = carry_s[...].astype(jnp.int32)


def router_dispatch(logits, router_b):
    t, w = logits.shape
    tm = min(TOKEN_TILE, t)
    return pl.pallas_call(
        _router_dispatch_kernel,
        grid=(t // tm,),
        in_specs=[pl.BlockSpec((tm, w), lambda i: (i, 0)), _full((N_EXPERTS, 1))],
        out_specs=[pl.BlockSpec((tm, w), lambda i: (i, 0)),
                   pl.BlockSpec((TOP_K, tm), lambda i: (0, i)),
                   pl.BlockSpec((TOP_K, tm), lambda i: (0, i)),
                   _full((N_EXPERTS, LANES))],
        out_shape=[jax.ShapeDtypeStruct((t, w), F32), jax.ShapeDtypeStruct((TOP_K, t), jnp.int32),
                   jax.ShapeDtypeStruct((TOP_K, t), jnp.int32), jax.ShapeDtypeStruct((N_EXPERTS, LANES), jnp.int32)],
        scratch_shapes=[pltpu.VMEM((N_EXPERTS, LANES), F32)],
        compiler_params=_cp("arbitrary"),
        name="router_dispatch",
    )(logits, router_b.reshape(N_EXPERTS, 1).astype(F32))


def router_gates(logits, router_b):
    t, w = logits.shape
    tm = min(TOKEN_TILE, t)
    return pl.pallas_call(
        _router_kernel,
        grid=(t // tm,),
        in_specs=[pl.BlockSpec((tm, w), lambda i: (i, 0)), _full((N_EXPERTS, 1))],
        out_specs=pl.BlockSpec((tm, w), lambda i: (i, 0)),
        out_shape=jax.ShapeDtypeStruct((t, w), F32),
        compiler_params=_cp("parallel"),
        name="router_gates",
    )(logits, router_b.reshape(N_EXPERTS, 1).astype(F32))


def _moe_kernel(h_ref, gate_ref, x_ref, wg_ref, wu_ref, wd_ref, sg_ref, su_ref, sd_ref, gpost_ref, ga2_ref,
                o_ref, acc_s):
    e = pl.program_id(1)
    h = _mx(_unpack_pairs(h_ref[...]))

    @pl.when(e == 0)
    def _():
        hs = _silu(jnp.dot(h, sg_ref[...], preferred_element_type=F32)) * jnp.dot(h, su_ref[...], preferred_element_type=F32)
        acc_s[...] = jnp.dot(_mx(hs), sd_ref[...], preferred_element_type=F32)

    gates = gate_ref[...]
    lane = lax.broadcasted_iota(jnp.int32, gates.shape, 1)
    hid = []
    for j in range(MOE_EB):
        gcol = jnp.sum(jnp.where(lane == e * MOE_EB + j, gates, 0.0), axis=1, keepdims=True)
        g = jnp.dot(h, _mx(wg_ref[j]), preferred_element_type=F32)
        u = jnp.dot(h, _mx(wu_ref[j]), preferred_element_type=F32)
        hid.append(_mx(_silu(g) * u * gcol))
    wd = _mx(wd_ref[...]).reshape(MOE_EB * D_EXPERT, -1)
    acc_s[...] += jnp.dot(jnp.concatenate(hid, axis=1), wd, preferred_element_type=F32)

    @pl.when(e == pl.num_programs(1) - 1)
    def _():
        o_ref[...] = x_ref[...] + ga2_ref[0] * _rms(acc_s[...], gpost_ref[...])


def moe_ffn(h, gates, x, layer, wg, wu, wd, sg, su, sd, gpost, ga2, tiles_per_group):
    t, d = x.shape
    tm = min(MOE_TILE, t)
    _, ne, _, f = wg.shape
    row = lambda w: pl.BlockSpec((tm, w), lambda i, e: (i, 0))
    return pl.pallas_call(
        _moe_kernel,
        grid=(t // tm, ne // MOE_EB),
        in_specs=[row(h.shape[1]), row(gates.shape[1]), row(d),
                  pl.BlockSpec((None, MOE_EB, d, f), lambda i, e: (layer, e, 0, 0)),
                  pl.BlockSpec((None, MOE_EB, d, f), lambda i, e: (layer, e, 0, 0)),
                  pl.BlockSpec((None, MOE_EB, f, d), lambda i, e: (layer, e, 0, 0)),
                  _full(sg.shape), _full(su.shape), _full(sd.shape), _full((1, d)),
                  pl.BlockSpec((1, 1, d), lambda i, e: (i // tiles_per_group, 0, 0))],
        out_specs=row(d),
        out_shape=jax.ShapeDtypeStruct((t, d), F32),
        scratch_shapes=[pltpu.VMEM((tm, d), F32)],
        compiler_params=_cp("parallel", "arbitrary"),
        name="moe_ffn",
    )(h, gates, x, wg, wu, wd, sg, su, sd, gpost, ga2)


def moe_plan(counts, n_tokens):
    n_blocks = (n_tokens * TOP_K + N_EXPERTS * (MOE_BLOCK - 1) + MOE_BLOCK - 1) // MOE_BLOCK
    cnt = counts[:, 0]
    padded = (cnt + MOE_BLOCK - 1) // MOE_BLOCK * MOE_BLOCK
    pad_end = jnp.cumsum(padded)
    off = pad_end - padded
    start = jnp.arange(n_blocks, dtype=jnp.int32) * MOE_BLOCK
    be = jnp.minimum(jnp.sum(pad_end[None, :] <= start[:, None], axis=1), N_EXPERTS - 1).astype(jnp.int32)
    mine = be[:, None] == jnp.arange(N_EXPERTS, dtype=jnp.int32)[None, :]
    end = jnp.sum(jnp.where(mine, (off + cnt)[None, :], 0), axis=1)
    nv = jnp.clip(end - start, 0, MOE_BLOCK).astype(jnp.int32)
    return off.astype(jnp.int32), be, nv


def _rows_kernel(off_ref, ek_ref, pk_ref, dest_ref):
    ek = ek_ref[...]
    dest = pk_ref[...]
    for e in range(N_EXPERTS):
        dest = dest + jnp.where(ek == e, off_ref[e], 0)
    dest_ref[...] = dest


def moe_rows(off, ek, pk):
    k, t = ek.shape
    tm = min(MOE_PLAN_TILE, t)
    spec = pl.BlockSpec((k, tm), lambda i, off: (0, i))
    return pl.pallas_call(
        _rows_kernel,
        grid_spec=pltpu.PrefetchScalarGridSpec(num_scalar_prefetch=1, grid=(t // tm,),
                                               in_specs=[spec, spec], out_specs=spec),
        out_shape=jax.ShapeDtypeStruct((k, t), jnp.int32),
        compiler_params=_cp("arbitrary"),
        name="moe_rows",
    )(off, ek, pk)


U32 = jnp.uint32
HIGH_HALF = 0xFFFF0000


def _pack_pairs(x):
    w = x.shape[1] // 2
    bits = lax.bitcast_convert_type(x.astype(jnp.bfloat16).astype(F32), U32)
    return (bits[:, w:] & jnp.uint32(HIGH_HALF)) | (bits[:, :w] >> 16)


def _unpack_pairs(p):
    lo = lax.bitcast_convert_type(p << 16, F32)
    hi = lax.bitcast_convert_type(p & jnp.uint32(HIGH_HALF), F32)
    return jnp.concatenate([lo, hi], axis=1)


def _sc_workers():
    info = plsc.get_sparse_core_info()
    return info.num_cores, info.num_cores * info.num_subcores


def sc_scatter_rows(src, idx, n_rows):
    k, t = idx.shape
    w = src.shape[1]
    n_cores, n_workers = _sc_workers()
    per_worker = t // n_workers
    mesh = plsc.VectorSubcoreMesh(core_axis_name="c", subcore_axis_name="s")

    @functools.partial(
        pl.kernel, mesh=mesh, out_type=jax.ShapeDtypeStruct((n_rows, w), src.dtype),
        scratch_types=[pltpu.VMEM((k, SC_WINDOW), jnp.int32), pltpu.VMEM((SC_WINDOW, w), src.dtype),
                       pltpu.SemaphoreType.DMA])
    def scatter(s_hbm, i_hbm, o_hbm, idx_v, rows_v, sem):
        base = (lax.axis_index("s") * n_cores + lax.axis_index("c")) * per_worker

        @pl.loop(0, per_worker // SC_WINDOW)
        def _(j):
            off = base + j * SC_WINDOW
            pltpu.sync_copy(i_hbm.at[:, pl.ds(off, SC_WINDOW)], idx_v)
            pltpu.sync_copy(s_hbm.at[pl.ds(off, SC_WINDOW)], rows_v)
            for kk in range(k):
                pltpu.async_copy(rows_v, o_hbm.at[idx_v.at[kk]], sem).wait()

    return scatter(src, idx)


def sc_gather_rows(table, idx):
    n = idx.shape[0]
    w = table.shape[1]
    n_cores, n_workers = _sc_workers()
    per_worker = n // n_workers
    mesh = plsc.VectorSubcoreMesh(core_axis_name="c", subcore_axis_name="s")

    @functools.partial(
        pl.kernel, mesh=mesh, out_type=jax.ShapeDtypeStruct((n, w), table.dtype),
        scratch_types=[pltpu.VMEM((SC_WINDOW,), jnp.int32), pltpu.VMEM((SC_WINDOW, w), table.dtype),
                       pltpu.SemaphoreType.DMA])
    def gather(t_hbm, i_hbm, o_hbm, idx_v, rows_v, sem):
        base = (lax.axis_index("s") * n_cores + lax.axis_index("c")) * per_worker

        @pl.loop(0, per_worker // SC_WINDOW)
        def _(j):
            off = base + j * SC_WINDOW
            pltpu.sync_copy(i_hbm.at[pl.ds(off, SC_WINDOW)], idx_v)
            pltpu.async_copy(t_hbm.at[idx_v], rows_v, sem).wait()
            pltpu.sync_copy(rows_v, o_hbm.at[pl.ds(off, SC_WINDOW)])

    return gather(table, idx)


def _expert_kernel(be_ref, nv_ref, xs_ref, wg_ref, wu_ref, wd_ref, ys_ref):
    nv = nv_ref[pl.program_id(0)]

    @pl.when(nv > 0)
    def _():
        x = _unpack_pairs(xs_ref[...])
        rows = lax.broadcasted_iota(jnp.int32, x.shape, 0)
        x = _mx(jnp.where(rows < nv, x, 0.0))
        hid = (_silu(jnp.dot(x, _mx(wg_ref[0]), preferred_element_type=F32))
               * jnp.dot(x, _mx(wu_ref[0]), preferred_element_type=F32))
        ys_ref[...] = _pack_pairs(jnp.dot(_mx(hid), _mx(wd_ref[0]), preferred_element_type=F32))

    @pl.when(nv == 0)
    def _():
        ys_ref[...] = jnp.zeros(ys_ref.shape, U32)


def moe_experts(xs, be, nv, layer, wg, wu, wd):
    n_rows, w = xs.shape
    _, _, d, f = wg.shape
    return pl.pallas_call(
        _expert_kernel,
        grid_spec=pltpu.PrefetchScalarGridSpec(
            num_scalar_prefetch=2,
            grid=(n_rows // MOE_BLOCK,),
            in_specs=[pl.BlockSpec((MOE_BLOCK, w), lambda b, be, nv: (b, 0)),
                      pl.BlockSpec((None, 1, d, f), lambda b, be, nv: (layer, be[b], 0, 0)),
                      pl.BlockSpec((None, 1, d, f), lambda b, be, nv: (layer, be[b], 0, 0)),
                      pl.BlockSpec((None, 1, f, d), lambda b, be, nv: (layer, be[b], 0, 0))],
            out_specs=pl.BlockSpec((MOE_BLOCK, w), lambda b, be, nv: (b, 0))),
        out_shape=jax.ShapeDtypeStruct((n_rows, w), U32),
        compiler_params=_cp("arbitrary"),
        name="moe_experts",
    )(be, nv, xs, wg, wu, wd)


def _combine_kernel(yg_ref, gk_ref, hp_ref, x_ref, sg_ref, su_ref, sd_ref, gpost_ref, ga2_ref, o_ref):
    h = _mx(_unpack_pairs(hp_ref[...]))
    hs = _silu(jnp.dot(h, sg_ref[...], preferred_element_type=F32)) * jnp.dot(h, su_ref[...], preferred_element_type=F32)
    f = jnp.dot(_mx(hs), sd_ref[...], preferred_element_type=F32)
    gk = gk_ref[...]
    for k in range(TOP_K):
        f = f + gk[:, k:k + 1] * _unpack_pairs(yg_ref[k])
    o_ref[...] = x_ref[...] + ga2_ref[0] * _rms(f, gpost_ref[...])


def moe_combine(yg, gk, hp, x, sg, su, sd, gpost, ga2, tiles_per_group):
    t, d = x.shape
    tm = min(MOE_ROW_TILE, t)
    w = hp.shape[1]
    row = lambda n: pl.BlockSpec((tm, n), lambda i: (i, 0))
    return pl.pallas_call(
        _combine_kernel,
        grid=(t // tm,),
        in_specs=[pl.BlockSpec((TOP_K, tm, w), lambda i: (0, i, 0)),
                  row(gk.shape[1]), row(w), row(d), _full(sg.shape), _full(su.shape), _full(sd.shape), _full((1, d)),
                  pl.BlockSpec((1, 1, d), lambda i: (i // tiles_per_group, 0, 0))],
        out_specs=row(d),
        out_shape=jax.ShapeDtypeStruct((t, d), F32),
        compiler_params=_cp("parallel"),
        name="moe_combine",
    )(yg, gk, hp, x, sg, su, sd, gpost, ga2)


def _reorder_w_in(w_in):
    c = np.cumsum((0,) + (GROUP_W, GROUP_W, GROUP_W, GROUP_W, GROUP_W, GROUP_W, 2 * SSD_STATE, 2 * SSD_STATE,
                          GROUP_W, 2 * N_HEADS, GROUP_W, GROUP_W, GROUP_W, GROUP_W, 2 * N_HEADS, 2 * N_HEADS))
    seg = lambda a, b: w_in[:, c[a]:c[b]]
    small = jnp.concatenate([seg(9, 10), seg(14, 15), seg(15, 16),
                             jnp.zeros((w_in.shape[0], LANES - 6 * N_HEADS), w_in.dtype)], axis=1)
    return jnp.concatenate([seg(0, 5), seg(5, 8), seg(8, 9), seg(10, 13), seg(13, 14), small], axis=1)


def kernel(x, c, ctx, c_ctx, w_mod, b_mod, g_pre_mix, g_post_mix, g_pre_ffn, g_post_ffn, w_in, w_out, lru_conv_w, lru_conv_b, lru_wa, lru_ba, lru_wx, lru_bx, lru_lambda, na_bias, ssd_conv_w, ssd_conv_b, ssd_a_log, ssd_dt_bias, ssd_d, ssd_norm, gdn_conv_w, gdn_a_log, gdn_dt_bias, gdn_norm, router_w, router_b, we_gate, we_up, we_down, ws_gate, ws_up, ws_down):
    bsz, seq, d = x.shape
    n_ctx = ctx.shape[1]
    depth = w_mod.shape[0]
    lat_tpg = seq // min(TOKEN_TILE, seq)
    ctx_tpg = max(bsz * n_ctx // TOKEN_TILE, 1)
    ctx_mpg = max(bsz * n_ctx // MOE_TILE, 1)

    cond = _pad_rows(jnp.concatenate([c, c_ctx[None, :]], axis=0))
    mod = modulation(cond, w_mod, b_mod).reshape(depth, SUBLANES, N_MOD, d)
    rope = rope_tables(seq)
    row = lambda v: v[None, :].astype(F32)

    xl = x.reshape(bsz * seq, d)
    xc = ctx.reshape(bsz * n_ctx, d)
    for l in range(depth):
        last = l == depth - 1
        m_lat = [mod[l, :bsz, k][:, None, :] for k in range(N_MOD)]
        m_ctx = [mod[l, bsz:bsz + 1, k][:, None, :] for k in range(N_MOD)]
        w_in_l = _reorder_w_in(w_in[l]).astype(MXU_DTYPE)
        pc = in_projection(xc, row(g_pre_mix[l]), m_ctx[1], m_ctx[0], w_in_l, ctx_tpg)
        pl_ = in_projection(xl, row(g_pre_mix[l]), m_lat[1], m_lat[0], w_in_l, lat_tpg)

        lru_p = lru_params(lru_conv_w[l], lru_conv_b[l], lru_wa[l], lru_ba[l], lru_wx[l], lru_bx[l], lru_lambda[l])
        a_cf, a_cb, a_st = lru_mixer(pc[P_AX], jnp.zeros((bsz, SUBLANES, GROUP_W), F32), bsz, *lru_p)
        a_lf, a_lb, _ = lru_mixer(pl_[P_AX], a_st, bsz, *lru_p)

        kc = pc[P_BK].reshape(bsz, n_ctx, GROUP_W)
        vc = pc[P_BV].reshape(bsz, n_ctx, GROUP_W)
        b_c = ctx_attention(pc[P_BQ].reshape(bsz, n_ctx, GROUP_W), kc, vc).reshape(bsz * n_ctx, GROUP_W)
        b_l = na_mixer(pl_[P_BQ], pl_[P_BK], pl_[P_BV], kc, vc, na_bias_slabs(na_bias[l]), bsz)

        ssd_p = ssd_params(ssd_conv_w[l], ssd_conv_b[l], ssd_a_log[l], ssd_dt_bias[l])
        c_cf, c_cb, c_cx, c_st = ssd_mixer(pc[P_CX], pc[P_SM], jnp.zeros((bsz, 2, N_HEADS, SSD_STATE, HEAD_DIM), F32),
                                           bsz, *ssd_p)
        c_lf, c_lb, c_lx, _ = ssd_mixer(pl_[P_CX], pl_[P_SM], c_st, bsz, *ssd_p)

        gdn_p = gdn_params(gdn_conv_w[l], gdn_a_log[l], gdn_dt_bias[l])
        d_cf, d_cb, d_st = gdn_mixer(pc[P_DX], pc[P_SM], jnp.zeros((bsz, 2, N_HEADS, HEAD_DIM, HEAD_DIM), F32),
                                     bsz, *gdn_p)
        d_lf, d_lb, _ = gdn_mixer(pl_[P_DX], pl_[P_SM], d_st, bsz, *gdn_p, rope=rope)

        epi = (w_out[l].astype(MXU_DTYPE), row(g_post_mix[l]))
        epi_tail = (row(jnp.repeat(ssd_d[l], HEAD_DIM)), row(ssd_norm[l]), row(jnp.tile(gdn_norm[l], N_HEADS)), router_w[l])
        routed_w = (l, we_gate, we_up, we_down)
        shared_w = (ws_gate[l].astype(MXU_DTYPE), ws_up[l].astype(MXU_DTYPE), ws_down[l].astype(MXU_DTYPE),
                    row(g_post_ffn[l]))

        mix_l = (a_lf, a_lb, pl_[P_AG], b_l, c_lf, c_lb, c_lx, pl_[P_CZ], d_lf, d_lb, pl_[P_DZ])
        xl, hp, lg = out_projection(xl, mix_l, *epi, m_lat[2], row(g_pre_ffn[l]), m_lat[4], m_lat[3], *epi_tail, lat_tpg)
        gk, ek, pk, cnt = router_dispatch(lg, router_b[l])
        off, be, nv = moe_plan(cnt, bsz * seq)
        dest = moe_rows(off, ek, pk)
        xs = sc_scatter_rows(hp, dest, be.shape[0] * MOE_BLOCK)
        ys = moe_experts(xs, be, nv, *routed_w)
        yg = sc_gather_rows(ys, dest.reshape(-1)).reshape(TOP_K, bsz * seq, d // 2)
        xl = moe_combine(yg, gk, hp, xl, *shared_w, m_lat[5], seq // min(MOE_ROW_TILE, seq))
        if not last:
            mix_c = (a_cf, a_cb, pc[P_AG], b_c, c_cf, c_cb, c_cx, pc[P_CZ], d_cf, d_cb, pc[P_DZ])
            xc, hp, lg = out_projection(xc, mix_c, *epi, m_ctx[2], row(g_pre_ffn[l]), m_ctx[4], m_ctx[3], *epi_tail, ctx_tpg)
            xc = moe_ffn(hp, router_gates(lg, router_b[l]), xc, *routed_w, *shared_w, m_ctx[5], ctx_mpg)
    return xl.reshape(bsz, seq, d)
```

```python
import functools
import math

import jax
import jax.numpy as jnp
import numpy as np
from jax import lax
from jax.experimental import pallas as pl
from jax.experimental.pallas import tpu as pltpu
from jax.experimental.pallas import tpu_sc as plsc

F32 = jnp.float32
MXU_DTYPE = jnp.bfloat16
HI = lax.Precision.HIGHEST

D_MODEL = 1024
GRID_W = 64
GROUP_W = 256
HEAD_DIM = 64
N_HEADS = 4
EPS = 1e-6
NEG_INF = -1e30
N_MOD = 6
LRU_C = 8.0
NA_WIN_ROWS = 8
NA_WIN_COLS = 16
SSD_STATE = 128
SSD_GROUPS = 2
ROPE_BASE = 10000.0
ROPE_AXIS_DIM = HEAD_DIM // 2
N_EXPERTS = 64
N_EXPERT_GROUPS = 8
TOPK_GROUPS = 4
TOP_K = 8
D_EXPERT = 256
ROUTED_SCALE = 2.5

LANES = 128
SUBLANES = 8
VMEM_LIMIT = 56 * 1024 * 1024

TOKEN_TILE = 512
LRU_CHUNK = 256
SSD_CHUNK = 128
GDN_CHUNK = 64
GDN_TILE = 256
GDN_SUB = 128
GDN_BASE = 16
MOE_TILE = 1024
MOE_EB = 4
MOE_BLOCK = 1024
MOE_ROW_TILE = 256
MOE_PLAN_TILE = 2048
SC_WINDOW = 128

P_WIDTHS = (256, 256, 256, 256, 256, 768, 256, 768, 256, 128)
(P_AX, P_AG, P_BQ, P_BK, P_BV, P_CX, P_CZ, P_DX, P_DZ, P_SM) = range(10)
SM_DT, SM_BETA, SM_DECAY = 0, 8, 16


def _cp(*sem):
    return pltpu.CompilerParams(dimension_semantics=sem, vmem_limit_bytes=VMEM_LIMIT)


def _mx(x):
    return x.astype(MXU_DTYPE)


def _dot(a, b):
    return jnp.dot(_mx(a), _mx(b), preferred_element_type=F32)


def _dot_nt(a, b):
    return lax.dot_general(_mx(a), _mx(b), (((1,), (1,)), ((), ())), preferred_element_type=F32)


def _dot_tn(a, b):
    return lax.dot_general(_mx(a), _mx(b), (((0,), (0,)), ((), ())), preferred_element_type=F32)


def _dot_hi(a, b):
    return jnp.dot(a, b, preferred_element_type=F32, precision=HI)


def _sigmoid(x):
    return 1.0 / (1.0 + jnp.exp(-x))


def _silu(x):
    return x * _sigmoid(x)


def _softplus(x):
    return jnp.maximum(x, 0.0) + jnp.log1p(jnp.exp(-jnp.abs(x)))


def _gelu_tanh(x):
    return 0.5 * x * (1.0 + jnp.tanh(math.sqrt(2.0 / math.pi) * (x + 0.044715 * (x * x * x))))


def _rms(x, g):
    return x * lax.rsqrt(jnp.mean(x * x, axis=-1, keepdims=True) + EPS) * g


def _full(shape):
    n = len(shape)
    return pl.BlockSpec(shape, lambda *_: (0,) * n)


MOD_COLS = 1536


def _mod_kernel(c_ref, w_ref, b_ref, o_ref):
    o_ref[0] = _dot_hi(_silu(c_ref[...]), w_ref[0]) + b_ref[0]


def modulation(cond, w_mod, b_mod):
    depth, d, n = w_mod.shape
    return pl.pallas_call(
        _mod_kernel,
        grid=(depth, n // MOD_COLS),
        in_specs=[pl.BlockSpec((SUBLANES, d), lambda l, j: (0, 0)),
                  pl.BlockSpec((1, d, MOD_COLS), lambda l, j: (l, 0, j)),
                  pl.BlockSpec((1, 1, MOD_COLS), lambda l, j: (l, 0, j))],
        out_specs=pl.BlockSpec((1, SUBLANES, MOD_COLS), lambda l, j: (l, 0, j)),
        out_shape=jax.ShapeDtypeStruct((depth, SUBLANES, n), F32),
        compiler_params=_cp("parallel", "parallel"),
        name="modulation",
    )(cond, w_mod, b_mod.reshape(depth, 1, n))


def _inproj_kernel(x_ref, g_ref, sc_ref, sh_ref, w_ref, *o_refs):
    h = _rms(x_ref[...], g_ref[...]) * (1.0 + sc_ref[0]) + sh_ref[0]
    p = _dot(h, w_ref[...])
    off = 0
    for o_ref, w in zip(o_refs, P_WIDTHS):
        o_ref[...] = p[:, off:off + w]
        off += w


def in_projection(x, g, sc, sh, w, tiles_per_group):
    t, d = x.shape
    tm = min(TOKEN_TILE, t)
    vec = lambda i: (i // tiles_per_group, 0, 0)
    return pl.pallas_call(
        _inproj_kernel,
        grid=(t // tm,),
        in_specs=[pl.BlockSpec((tm, d), lambda i: (i, 0)),
                  _full((1, d)),
                  pl.BlockSpec((1, 1, d), vec),
                  pl.BlockSpec((1, 1, d), vec),
                  _full(w.shape)],
        out_specs=[pl.BlockSpec((tm, wd), lambda i: (i, 0)) for wd in P_WIDTHS],
        out_shape=[jax.ShapeDtypeStruct((t, wd), F32) for wd in P_WIDTHS],
        compiler_params=_cp("parallel"),
        name="in_projection",
    )(x, g, sc, sh, w)


def _halo(p, bsz, q):
    c = p.shape[-1]
    pr = p.reshape(bsz, -1, q, c)
    nc = pr.shape[1]
    prev = jnp.concatenate([jnp.zeros((bsz, 1, 2, c), p.dtype), pr[:, :-1, q - 2:, :]], axis=1)
    nxt = jnp.concatenate([pr[:, 1:, :1, :], jnp.zeros((bsz, 1, 1, c), p.dtype)], axis=1)
    pad = jnp.zeros((bsz, nc, SUBLANES - 3, c), p.dtype)
    return jnp.concatenate([prev, nxt, pad], axis=2).reshape(bsz * nc, SUBLANES, c)


def _dwconv(x, halo, w, b=None):
    q = x.shape[0]
    row = lax.broadcasted_iota(jnp.int32, (SUBLANES, x.shape[1]), 0)

    def shifted(s, keep_rolled, edge):
        r = pltpu.roll(x, s % q, 0)
        if s > 0:
            return jnp.concatenate([jnp.where(keep_rolled, r[:SUBLANES], edge), r[SUBLANES:]], axis=0)
        return jnp.concatenate([r[:q - SUBLANES], jnp.where(keep_rolled, r[q - SUBLANES:], edge)], axis=0)

    xm2 = shifted(2, row >= 2, halo)
    xm1 = shifted(1, row >= 1, pltpu.roll(halo, SUBLANES - 1, 0))
    xp1 = shifted(-1, row < SUBLANES - 1, pltpu.roll(halo, SUBLANES - 3, 0))
    y = w[0:1] * xm2 + w[1:2] * xm1 + w[2:3] * x + w[3:4] * xp1
    return y if b is None else y + b


def _pad_rows(a, rows=SUBLANES):
    return jnp.concatenate([a, jnp.zeros((rows - a.shape[0],) + a.shape[1:], a.dtype)], axis=0)


def _chunk_specs(nc, q, c):
    fwd = pl.BlockSpec((q, c), lambda b, i: (b * nc + i, 0))
    bwd = pl.BlockSpec((q, c), lambda b, i: (b * nc + nc - 1 - i, 0))
    return fwd, bwd


def _halo_specs(nc, c):
    fwd = pl.BlockSpec((1, SUBLANES, c), lambda b, i: (b * nc + i, 0, 0))
    bwd = pl.BlockSpec((1, SUBLANES, c), lambda b, i: (b * nc + nc - 1 - i, 0, 0))
    return fwd, bwd


def _lru_kernel(xf_ref, xb_ref, hf_ref, hb_ref, h0_ref, cw_ref, cb_ref, wg_ref, bg_ref, lam_ref,
                yf_ref, yb_ref, hfin_ref, af_s, bf_s, ab_s, bb_s, carry_s):
    i = pl.program_id(1)
    q = xf_ref.shape[0]

    @pl.when(i == 0)
    def _():
        carry_s[...] = h0_ref[0]

    def coeffs(x_ref, halo_ref, d, a_s, b_s):
        u = _dwconv(x_ref[...], halo_ref[0], cw_ref[...], cb_ref[...])
        g = _dot(u, wg_ref[:, 2 * GROUP_W * d:2 * GROUP_W * (d + 1)]) + bg_ref[:, 2 * GROUP_W * d:2 * GROUP_W * (d + 1)]
        r = _sigmoid(g[:, :GROUP_W])
        gate_in = _sigmoid(g[:, GROUP_W:])
        log_a = -LRU_C * r * _softplus(-lam_ref[d:d + 1, :])
        a_s[...] = jnp.exp(log_a)
        b_s[...] = jnp.sqrt(1.0 - jnp.exp(2.0 * log_a)) * (gate_in * u)

    coeffs(xf_ref, hf_ref, 0, af_s, bf_s)
    coeffs(xb_ref, hb_ref, 1, ab_s, bb_s)

    ng = q // SUBLANES
    row = lax.broadcasted_iota(jnp.int32, (SUBLANES, GROUP_W), 0)

    def body(g, hs):
        h_f, h_b = hs
        i0 = pl.multiple_of(g * SUBLANES, SUBLANES)
        a = af_s[pl.ds(i0, SUBLANES), :]
        b = bf_s[pl.ds(i0, SUBLANES), :]
        for s in (1, 2, 4):
            m = row >= s
            b = jnp.where(m, a * pltpu.roll(b, s, 0) + b, b)
            a = jnp.where(m, a * pltpu.roll(a, s, 0), a)
        h = b + a * h_f
        yf_ref[pl.ds(i0, SUBLANES), :] = h
        h_f = h[SUBLANES - 1:SUBLANES, :]
        j0 = pl.multiple_of((ng - 1 - g) * SUBLANES, SUBLANES)
        a = ab_s[pl.ds(j0, SUBLANES), :]
        b = bb_s[pl.ds(j0, SUBLANES), :]
        for s in (1, 2, 4):
            m = row < SUBLANES - s
            b = jnp.where(m, a * pltpu.roll(b, SUBLANES - s, 0) + b, b)
            a = jnp.where(m, a * pltpu.roll(a, SUBLANES - s, 0), a)
        h = b + a * h_b
        yb_ref[pl.ds(j0, SUBLANES), :] = h
        return h_f, h[0:1, :]

    h_f, h_b = lax.fori_loop(0, ng, body, (carry_s[0:1, :], carry_s[1:2, :]))
    carry_s[0:1, :] = h_f
    carry_s[1:2, :] = h_b

    @pl.when(i == pl.num_programs(1) - 1)
    def _():
        hfin_ref[0] = carry_s[...]


def _block_diag(w):
    h, a, b = w.shape
    return jnp.einsum('hij,hg->higj', w, jnp.eye(h, dtype=w.dtype)).reshape(h * a, h * b)


def lru_params(conv_w, conv_b, wa, ba, wx, bx, lam):
    wg = jnp.concatenate([_block_diag(wa[0]), _block_diag(wx[0]), _block_diag(wa[1]), _block_diag(wx[1])], axis=1)
    bg = jnp.concatenate([ba[0], bx[0], ba[1], bx[1]])[None, :]
    return _pad_rows(conv_w), conv_b[None, :], wg.astype(MXU_DTYPE), bg, _pad_rows(lam)


def lru_mixer(x, h0, bsz, cw, cb, wg, bg, lam):
    t, c = x.shape
    s = t // bsz
    q = min(LRU_CHUNK, s)
    nc = s // q
    halo = _halo(x, bsz, q)
    xf, xb = _chunk_specs(nc, q, c)
    hf, hb = _halo_specs(nc, c)
    st = pl.BlockSpec((1, SUBLANES, c), lambda b, i: (b, 0, 0))
    return pl.pallas_call(
        _lru_kernel,
        grid=(bsz, nc),
        in_specs=[xf, xb, hf, hb, st, _full(cw.shape), _full(cb.shape), _full(wg.shape), _full(bg.shape),
                  _full(lam.shape)],
        out_specs=[xf, xb, st],
        out_shape=[jax.ShapeDtypeStruct((t, c), F32), jax.ShapeDtypeStruct((t, c), F32),
                   jax.ShapeDtypeStruct((bsz, SUBLANES, c), F32)],
        scratch_shapes=[pltpu.VMEM((q, c), F32)] * 4 + [pltpu.VMEM((SUBLANES, c), F32)],
        compiler_params=_cp("parallel", "arbitrary"),
        name="lru_mixer",
    )(x, x, halo, halo, h0, cw, cb, wg, bg, lam)


NA_KEYS = NA_WIN_ROWS * GRID_W
NA_ROW_BLOCK = 8


def na_bias_slabs(table):
    qc = np.arange(GRID_W)[:, None]
    kc = np.arange(GRID_W)[None, :]
    win0 = np.clip(qc - NA_WIN_COLS // 2, 0, GRID_W - NA_WIN_COLS)
    ok = (kc >= win0) & (kc < win0 + NA_WIN_COLS)
    dc = np.clip(kc - qc + NA_WIN_COLS - 1, 0, 2 * NA_WIN_COLS - 2)
    dr = np.arange(NA_WIN_ROWS)[:, None] + np.arange(NA_WIN_ROWS)[None, :]
    b = table.astype(F32)[:, dr][:, :, :, dc]
    b = jnp.where(ok[None, None, None], b, NEG_INF)
    h = table.shape[0]
    return b.transpose(0, 1, 3, 2, 4).reshape(h, NA_WIN_ROWS, GRID_W, NA_KEYS)


def _na_span_start(j, rows):
    return jnp.clip(j * NA_ROW_BLOCK - NA_WIN_ROWS // 2, 0, rows - (NA_ROW_BLOCK + NA_WIN_ROWS - 1))


def _na_kernel(q_ref, kw_ref, vw_ref, kc_ref, vc_ref, slab_ref, o_ref, *, rows):
    j = pl.program_id(1)
    ustart = _na_span_start(j, rows)
    q = q_ref[...] * (HEAD_DIM ** -0.5)
    kc, vc = kc_ref[0], vc_ref[0]
    heads = [slice(h * HEAD_DIM, (h + 1) * HEAD_DIM) for h in range(N_HEADS)]
    qrows = [slice(i * GRID_W, (i + 1) * GRID_W) for i in range(NA_ROW_BLOCK)]
    kws, vws, offs = [], [], []
    for i in range(NA_ROW_BLOCK):
        r = j * NA_ROW_BLOCK + i
        r0 = jnp.clip(r - NA_WIN_ROWS // 2, 0, rows - NA_WIN_ROWS)
        start = pl.multiple_of((r0 - ustart) * GRID_W, GRID_W)
        kws.append(kw_ref[pl.ds(start, NA_KEYS), :])
        vws.append(vw_ref[pl.ds(start, NA_KEYS), :])
        offs.append(r0 - r + NA_WIN_ROWS - 1)
    s_ctx = [_dot_nt(q[:, sl], kc[:, sl]) for sl in heads]
    s_loc = [[_dot_nt(q[qr, sl], kws[i][:, sl]) + slab_ref[h, offs[i]] for h, sl in enumerate(heads)]
             for i, qr in enumerate(qrows)]
    m = [[jnp.maximum(jnp.max(s_loc[i][h], axis=-1, keepdims=True), jnp.max(s_ctx[h][qr], axis=-1, keepdims=True))
          for h in range(N_HEADS)] for i, qr in enumerate(qrows)]
    p_loc = [[jnp.exp(s_loc[i][h] - m[i][h]) for h in range(N_HEADS)] for i in range(NA_ROW_BLOCK)]
    p_ctx = [jnp.exp(s_ctx[h] - jnp.concatenate([m[i][h] for i in range(NA_ROW_BLOCK)], axis=0))
             for h in range(N_HEADS)]
    o_ctx = [_dot(p_ctx[h], vc[:, sl]) for h, sl in enumerate(heads)]
    rows_out = []
    for i, qr in enumerate(qrows):
        outs = []
        for h, sl in enumerate(heads):
            den = jnp.sum(p_loc[i][h], axis=-1, keepdims=True) + jnp.sum(p_ctx[h][qr], axis=-1, keepdims=True)
            outs.append((_dot(p_loc[i][h], vws[i][:, sl]) + o_ctx[h][qr]) / den)
        rows_out.append(jnp.concatenate(outs, axis=1))
    o_ref[...] = jnp.concatenate(rows_out, axis=0)


def na_mixer(q, k, v, kc, vc, slabs, bsz):
    t, c = q.shape
    s = t // bsz
    rows = s // GRID_W
    n_ctx = kc.shape[1]
    span = (NA_ROW_BLOCK + NA_WIN_ROWS - 1) * GRID_W

    def win(b, j):
        return ((b * rows + _na_span_start(j, rows)) * GRID_W, 0)

    wspec = pl.BlockSpec((pl.Element(span), pl.Element(c)), win)
    cspec = pl.BlockSpec((1, n_ctx, c), lambda b, j: (b, 0, 0))
    qspec = pl.BlockSpec((NA_ROW_BLOCK * GRID_W, c), lambda b, j: (b * (rows // NA_ROW_BLOCK) + j, 0))
    return pl.pallas_call(
        functools.partial(_na_kernel, rows=rows),
        grid=(bsz, rows // NA_ROW_BLOCK),
        in_specs=[qspec, wspec, wspec, cspec, cspec, _full(slabs.shape)],
        out_specs=qspec,
        out_shape=jax.ShapeDtypeStruct((t, c), F32),
        compiler_params=_cp("parallel", "arbitrary"),
        name="na_mixer",
    )(q, k, v, kc, vc, slabs)


def _ctx_attn_kernel(q_ref, k_ref, v_ref, o_ref):
    q = q_ref[0] * (HEAD_DIM ** -0.5)
    k, v = k_ref[0], v_ref[0]
    outs = []
    for h in range(N_HEADS):
        sl = slice(h * HEAD_DIM, (h + 1) * HEAD_DIM)
        s = _dot_nt(q[:, sl], k[:, sl])
        p = jnp.exp(s - jnp.max(s, axis=-1, keepdims=True))
        outs.append(_dot(p, v[:, sl]) / jnp.sum(p, axis=-1, keepdims=True))
    o_ref[0] = jnp.concatenate(outs, axis=1)


def ctx_attention(q, k, v):
    spec = pl.BlockSpec((1,) + q.shape[1:], lambda b: (b, 0, 0))
    return pl.pallas_call(
        _ctx_attn_kernel,
        grid=(q.shape[0],),
        in_specs=[spec, spec, spec],
        out_specs=spec,
        out_shape=jax.ShapeDtypeStruct(q.shape, F32),
        compiler_params=_cp("parallel"),
        name="ctx_attention",
    )(q, k, v)


def _small_vec(vals, off):
    v = jnp.zeros((LANES,), F32).at[off:off + 2 * N_HEADS].set(vals.reshape(-1).astype(F32))
    return v[None, :]


def _lane_mask(off):
    lane = lax.broadcasted_iota(jnp.int32, (1, LANES), 1)
    return (lane >= off) & (lane < off + 2 * N_HEADS)


def _tri_masks(q):
    rowi = lax.broadcasted_iota(jnp.int32, (q, q), 0)
    coli = lax.broadcasted_iota(jnp.int32, (q, q), 1)
    return rowi, coli


def _ssd_kernel(xf_ref, xb_ref, hf_ref, hb_ref, sf_ref, sb_ref, h0_ref, cw_ref, cb_ref, dtb_ref, alog_ref,
                yf_ref, yb_ref, xc_ref, hfin_ref, state_s):
    i = pl.program_id(1)
    q = xf_ref.shape[0]

    @pl.when(i == 0)
    def _():
        state_s[...] = h0_ref[0]

    rowi, coli = _tri_masks(q)
    a_neg = jnp.where(_lane_mask(SM_DT), -jnp.exp(alog_ref[...]), 0.0)

    def direction(x_ref, halo_ref, sm_ref, d, y_ref):
        xbc = _silu(_dwconv(x_ref[...], halo_ref[0], cw_ref[...], cb_ref[...]))
        if d == 0:
            xc_ref[...] = xbc[:, :GROUP_W]
        dt = _softplus(sm_ref[...] + dtb_ref[...])
        keep = (rowi >= coli) if d == 0 else (rowi <= coli)
        acum = _dot_hi(keep.astype(F32), dt * a_neg)
        acum_t = acum.T
        last = acum[q - 1:q, :] if d == 0 else acum[0:1, :]
        dec_end = jnp.exp(last - acum)
        e_acum = jnp.exp(acum)
        e_last = jnp.exp(last)
        ys = []
        for g in range(SSD_GROUPS):
            bg = xbc[:, GROUP_W + SSD_STATE * g:GROUP_W + SSD_STATE * (g + 1)]
            cg = xbc[:, GROUP_W + SSD_STATE * (SSD_GROUPS + g):GROUP_W + SSD_STATE * (SSD_GROUPS + g + 1)]
            cbt = _dot_nt(cg, bg)
            for hh in range(N_HEADS // SSD_GROUPS):
                h = g * (N_HEADS // SSD_GROUPS) + hh
                ln = SM_DT + N_HEADS * d + h
                lmat = jnp.exp(jnp.where(keep, acum[:, ln:ln + 1] - acum_t[ln:ln + 1, :], NEG_INF))
                xdt = xbc[:, h * HEAD_DIM:(h + 1) * HEAD_DIM] * dt[:, ln:ln + 1]
                st = state_s[d, h]
                ys.append(_dot(cbt * lmat, xdt) + _dot(cg * e_acum[:, ln:ln + 1], st))
                state_s[d, h] = st * e_last[:, ln:ln + 1] + _dot_tn(bg * dec_end[:, ln:ln + 1], xdt)
        y_ref[...] = jnp.concatenate(ys, axis=1)

    direction(xf_ref, hf_ref, sf_ref, 0, yf_ref)
    direction(xb_ref, hb_ref, sb_ref, 1, yb_ref)

    @pl.when(i == pl.num_programs(1) - 1)
    def _():
        hfin_ref[0] = state_s[...]


def ssd_params(conv_w, conv_b, a_log, dt_bias):
    return _pad_rows(conv_w), conv_b[None, :], _small_vec(dt_bias, SM_DT), _small_vec(a_log, SM_DT)


def ssd_mixer(xbc, sm, h0, bsz, cw, cb, dtb, alog):
    t, c = xbc.shape
    s = t // bsz
    q = min(SSD_CHUNK, s)
    nc = s // q
    halo = _halo(xbc, bsz, q)
    xf, xb = _chunk_specs(nc, q, c)
    hf, hb = _halo_specs(nc, c)
    sf, sb = _chunk_specs(nc, q, LANES)
    yf, yb = _chunk_specs(nc, q, GROUP_W)
    st = pl.BlockSpec((1,) + h0.shape[1:], lambda b, i: (b, 0, 0, 0, 0))
    y_shape = jax.ShapeDtypeStruct((t, GROUP_W), F32)
    return pl.pallas_call(
        _ssd_kernel,
        grid=(bsz, nc),
        in_specs=[xf, xb, hf, hb, sf, sb, st, _full(cw.shape), _full(cb.shape), _full(dtb.shape), _full(alog.shape)],
        out_specs=[yf, yb, yf, st],
        out_shape=[y_shape, y_shape, y_shape, jax.ShapeDtypeStruct(h0.shape, F32)],
        scratch_shapes=[pltpu.VMEM(h0.shape[1:], F32)],
        compiler_params=_cp("parallel", "arbitrary"),
        name="ssd_mixer",
    )(xbc, xbc, halo, halo, sm, sm, h0, cw, cb, dtb, alog)


def rope_tables(seq):
    t = jnp.arange(seq)
    row = (t // GRID_W).astype(F32)
    col = (t % GRID_W).astype(F32)
    inv = ROPE_BASE ** (-jnp.arange(0, ROPE_AXIS_DIM, 2, dtype=F32) / ROPE_AXIS_DIM)
    ar, ac = row[:, None] * inv, col[:, None] * inv
    cos = jnp.concatenate([jnp.cos(ar), jnp.cos(ar), jnp.cos(ac), jnp.cos(ac)], axis=1)
    sin = jnp.concatenate([-jnp.sin(ar), jnp.sin(ar), -jnp.sin(ac), jnp.sin(ac)], axis=1)
    return jnp.tile(cos, (1, N_HEADS)), jnp.tile(sin, (1, N_HEADS))


def _swap16(x):
    lane = lax.broadcasted_iota(jnp.int32, x.shape, 1)
    half = ROPE_AXIS_DIM // 2
    return jnp.where((lane & (ROPE_AXIS_DIM - 1)) < half,
                     pltpu.roll(x, x.shape[1] - half, 1), pltpu.roll(x, half, 1))


def _head_sums(sq):
    c = sq.shape[1]
    li = lax.broadcasted_iota(jnp.int32, (c, c), 0)
    lj = lax.broadcasted_iota(jnp.int32, (c, c), 1)
    sh = HEAD_DIM.bit_length() - 1
    ones = _mx(((li >> sh) == (lj >> sh)).astype(F32))
    hi = _mx(sq)
    lo = _mx(sq - hi.astype(F32))
    return jnp.dot(hi, ones, preferred_element_type=F32) + jnp.dot(lo, ones, preferred_element_type=F32)


def _l2norm_heads(x):
    return x * lax.rsqrt(_head_sums(x * x) + EPS)


def _dot_tri(mask, x):
    m = _mx(mask.astype(F32))
    x1 = _mx(x)
    r1 = x - x1.astype(F32)
    x2 = _mx(r1)
    x3 = _mx(r1 - x2.astype(F32))
    return (jnp.dot(m, x1, preferred_element_type=F32) + jnp.dot(m, x2, preferred_element_type=F32)
            + jnp.dot(m, x3, preferred_element_type=F32))


def _same_block(rowi, coli, n):
    sh = n.bit_length() - 1
    return (rowi >> sh) == (coli >> sh)


def _solve_unit_tri(a_list, rhs_list, rowi, coli, chunk):
    mm = lambda x, y: jnp.dot(x, y, preferred_element_type=F32)
    eye = (rowi == coli).astype(F32)
    in_base = _same_block(rowi, coli, GDN_BASE)
    base = [_mx(jnp.where(in_base, a, 0.0)) for a in a_list]
    ts = [jnp.where(in_base, eye - a, 0.0) for a in a_list]
    ps = [_mx(mm(b, b)) for b in base]
    ts = [t + mm(_mx(t), p) for t, p in zip(ts, ps)]
    n = 4
    while n < GDN_BASE:
        ps = [_mx(mm(p, p)) for p in ps]
        ts = [t + mm(_mx(t), p) for t, p in zip(ts, ps)]
        n *= 2
    n = GDN_BASE
    while 2 * n < chunk:
        inner = _same_block(rowi, coli, 2 * n) & jnp.logical_not(_same_block(rowi, coli, n))
        offs = [_mx(jnp.where(inner, a, 0.0)) for a in a_list]
        tb = [_mx(t) for t in ts]
        ms = [_mx(mm(t, off)) for t, off in zip(tb, offs)]
        ts = [t - mm(m, t_b) for t, m, t_b in zip(ts, ms, tb)]
        n *= 2
    outer = jnp.logical_not(_same_block(rowi, coli, n))
    offs = [_mx(jnp.where(outer, a, 0.0)) for a in a_list]
    tb = [_mx(t) for t in ts]
    ys = [mm(t, _mx(r)) for t, r in zip(tb, rhs_list)]
    zs = [_mx(mm(off, _mx(y))) for off, y in zip(offs, ys)]
    return [y - mm(t, z) for y, t, z in zip(ys, tb, zs)]


def _gdn_kernel(*refs, rope):
    if rope:
        (xf_ref, xb_ref, hf_ref, hb_ref, sf_ref, sb_ref, cf_ref, cb_ref, nf_ref, nb_ref,
         s0_ref, cw_ref, alog_ref, dtb_ref, of_ref, ob_ref, sfin_ref, state_s) = refs
    else:
        (xf_ref, xb_ref, hf_ref, hb_ref, sf_ref, sb_ref,
         s0_ref, cw_ref, alog_ref, dtb_ref, of_ref, ob_ref, sfin_ref, state_s) = refs
        cf_ref = cb_ref = nf_ref = nb_ref = None
    i = pl.program_id(1)
    tq = xf_ref.shape[0]
    ck = min(GDN_CHUNK, tq)
    nck = tq // ck

    @pl.when(i == 0)
    def _():
        state_s[...] = s0_ref[0]

    sub = min(GDN_SUB, tq)
    nsub = tq // sub
    rowt, colt = _tri_masks(tq)
    in_chunk_t = _same_block(rowt, colt, ck)
    rowi, coli = _tri_masks(sub)
    in_chunk = _same_block(rowi, coli, ck)
    a_neg = jnp.where(_lane_mask(SM_DECAY), -jnp.exp(alog_ref[...]), 0.0)

    a_list, rhs_list, qkm, qg, kd, e_last = [], [], [], [], [], []
    for d, (x_ref, halo_ref, sm_ref, cos_ref, sin_ref) in enumerate(
            ((xf_ref, hf_ref, sf_ref, cf_ref, nf_ref), (xb_ref, hb_ref, sb_ref, cb_ref, nb_ref))):
        qkv = _silu(_dwconv(x_ref[...], halo_ref[0], cw_ref[...]))
        qn = _l2norm_heads(qkv[:, :GROUP_W])
        kn = _l2norm_heads(qkv[:, GROUP_W:2 * GROUP_W])
        v = qkv[:, 2 * GROUP_W:]
        if rope:
            cos, sin = cos_ref[...], sin_ref[...]
            qn = qn * cos + _swap16(qn) * sin
            kn = kn * cos + _swap16(kn) * sin
        qn = qn * (HEAD_DIM ** -0.5)
        sm = sm_ref[...]
        beta = _sigmoid(sm)
        keep_t = in_chunk_t & ((rowt >= colt) if d == 0 else (rowt <= colt))
        keep = in_chunk & ((rowi >= coli) if d == 0 else (rowi <= coli))
        strict = in_chunk & ((rowi > coli) if d == 0 else (rowi < coli))
        gc = _dot_tri(keep_t, _softplus(sm + dtb_ref[...]) * a_neg)
        gc_t = gc.T
        edge = ck - 1 if d == 0 else 0
        last = jnp.concatenate([jnp.broadcast_to(gc[c * ck + edge:c * ck + edge + 1, :], (ck, LANES))
                                for c in range(nck)], axis=0)
        e_gc = jnp.exp(gc)
        e_end = jnp.exp(last - gc)
        e_last.append(jnp.exp(last))
        for h in range(N_HEADS):
            sl = slice(h * HEAD_DIM, (h + 1) * HEAD_DIM)
            lg = SM_DECAY + N_HEADS * d + h
            lb = SM_BETA + N_HEADS * d + h
            qh, kh, bcol = qn[:, sl], kn[:, sl], beta[:, lb:lb + 1]
            kb = kh * bcol
            rhs = jnp.concatenate([v[:, sl] * bcol, kb * e_gc[:, lg:lg + 1]], axis=1)
            qg.append(qh * e_gc[:, lg:lg + 1])
            kd.append(kh * e_end[:, lg:lg + 1])
            qh_m, kh_m, kb_m = _mx(qh), _mx(kh), _mx(kb)
            for s in range(nsub):
                rs = slice(s * sub, (s + 1) * sub)
                decay = jnp.exp(jnp.where(keep, gc[rs, lg:lg + 1] - gc_t[lg:lg + 1, rs], NEG_INF))
                a_list.append(jnp.where(strict, _dot_nt(kb_m[rs], kh_m[rs]) * decay, 0.0))
                rhs_list.append(rhs[rs])
                qkm.append(_dot_nt(qh_m[rs], kh_m[rs]) * decay)
    sols = _solve_unit_tri(a_list, rhs_list, rowi, coli, ck)
    sols = [jnp.concatenate(sols[n * nsub:(n + 1) * nsub], axis=0) for n in range(2 * N_HEADS)]

    chains = [(d, h) for d in range(2) for h in range(N_HEADS)]
    states = [state_s[d, h] for d, h in chains]
    v_new = [[None] * nck for _ in chains]
    o_st = [[None] * nck for _ in chains]
    for step in range(nck):
        rows = [slice((step if d == 0 else nck - 1 - step) * ck, (step if d == 0 else nck - 1 - step) * ck + ck)
                for d, _ in chains]
        ms = [_dot(jnp.concatenate([sols[n][r, HEAD_DIM:], qg[n][r]], axis=0), states[n])
              for n, r in enumerate(rows)]
        for n, (d, _) in enumerate(chains):
            c = step if d == 0 else nck - 1 - step
            v_new[n][c] = sols[n][rows[n], :HEAD_DIM] - ms[n][:ck]
            o_st[n][c] = ms[n][ck:]
        ups = [_dot_tn(kd[n][r], v_new[n][step if chains[n][0] == 0 else nck - 1 - step])
               for n, r in enumerate(rows)]
        for n, (d, h) in enumerate(chains):
            lg = SM_DECAY + N_HEADS * d + h
            states[n] = states[n] * e_last[d][rows[n].start:rows[n].start + 1, lg:lg + 1] + ups[n]
    cps = sub // ck
    outs = [jnp.concatenate(o_st[n], axis=0)
            + jnp.concatenate([_dot(qkm[n * nsub + s], jnp.concatenate(v_new[n][s * cps:(s + 1) * cps], axis=0))
                               for s in range(nsub)], axis=0)
            for n in range(len(chains))]
    of_ref[...] = jnp.concatenate(outs[:N_HEADS], axis=1)
    ob_ref[...] = jnp.concatenate(outs[N_HEADS:], axis=1)
    for n, (d, h) in enumerate(chains):
        state_s[d, h] = states[n]

    @pl.when(i == pl.num_programs(1) - 1)
    def _():
        sfin_ref[0] = state_s[...]


def gdn_params(conv_w, a_log, dt_bias):
    return _pad_rows(conv_w), _small_vec(a_log, SM_DECAY), _small_vec(dt_bias, SM_DECAY)


def gdn_mixer(qkv, sm, s0, bsz, cw, alog, dtb, rope=None):
    t, c = qkv.shape
    s = t // bsz
    q = min(GDN_TILE, s)
    nc = s // q
    halo = _halo(qkv, bsz, q)
    xf, xb = _chunk_specs(nc, q, c)
    hf, hb = _halo_specs(nc, c)
    sf, sb = _chunk_specs(nc, q, LANES)
    of, ob = _chunk_specs(nc, q, GROUP_W)
    st = pl.BlockSpec((1,) + s0.shape[1:], lambda b, i: (b, 0, 0, 0, 0))
    ins = [qkv, qkv, halo, halo, sm, sm]
    specs = [xf, xb, hf, hb, sf, sb]
    if rope is not None:
        tf = pl.BlockSpec((q, GROUP_W), lambda b, i: (i, 0))
        tb = pl.BlockSpec((q, GROUP_W), lambda b, i: (nc - 1 - i, 0))
        ins += [rope[0], rope[0], rope[1], rope[1]]
        specs += [tf, tb, tf, tb]
    ins += [s0, cw, alog, dtb]
    specs += [st, _full(cw.shape), _full(alog.shape), _full(dtb.shape)]
    o_shape = jax.ShapeDtypeStruct((t, GROUP_W), F32)
    return pl.pallas_call(
        functools.partial(_gdn_kernel, rope=rope is not None),
        grid=(bsz, nc),
        in_specs=specs,
        out_specs=[of, ob, st],
        out_shape=[o_shape, o_shape, jax.ShapeDtypeStruct(s0.shape, F32)],
        scratch_shapes=[pltpu.VMEM(s0.shape[1:], F32)],
        compiler_params=_cp("parallel", "arbitrary"),
        name="gdn_mixer",
    )(*ins)


def _split_hi_lo(a):
    hi = _mx(a)
    return hi, _mx(a - hi.astype(F32))


def _outproj_kernel(x_ref, ahf_ref, ahb_ref, ag_ref, bo_ref, cyf_ref, cyb_ref, cxc_ref, cz_ref,
                    dof_ref, dob_ref, dz_ref, wout_ref, gpost_ref, ga1_ref, gpre_ref, sc2_ref, sh2_ref,
                    dskip_ref, cnorm_ref, dnorm_ref, rhi_ref, rlo_ref, xo_ref, hp_ref, lg_ref):
    m_a = (ahf_ref[...] + ahb_ref[...]) * _gelu_tanh(ag_ref[...])
    y_c = (cyf_ref[...] + cyb_ref[...] + cxc_ref[...] * dskip_ref[...]) * _silu(cz_ref[...])
    m_c = _rms(y_c, cnorm_ref[...])
    o_d = dof_ref[...] + dob_ref[...]
    m_d = o_d * lax.rsqrt(_head_sums(o_d * o_d) * (1.0 / HEAD_DIM) + EPS) * dnorm_ref[...] * _silu(dz_ref[...])
    mix = jnp.concatenate([_mx(m_a), _mx(bo_ref[...]), _mx(m_c), _mx(m_d)], axis=1)
    ml = jnp.dot(mix, wout_ref[...], preferred_element_type=F32)
    x_new = x_ref[...] + ga1_ref[0] * _rms(ml, gpost_ref[...])
    xo_ref[...] = x_new
    h2 = _rms(x_new, gpre_ref[...]) * (1.0 + sc2_ref[0]) + sh2_ref[0]
    hi, lo = _split_hi_lo(h2)
    hp_ref[...] = _pack_pairs(h2)
    rhi = rhi_ref[...]
    lg_ref[...] = (jnp.dot(hi, rhi, preferred_element_type=F32) + jnp.dot(lo, rhi, preferred_element_type=F32)
                   + jnp.dot(hi, rlo_ref[...], preferred_element_type=F32))


def out_projection(x, mixers, w_out, gpost, ga1, gpre, sc2, sh2, dskip, cnorm, dnorm, router_w, tiles_per_group):
    t, d = x.shape
    tm = min(TOKEN_TILE, t)
    vec = lambda i: (i // tiles_per_group, 0, 0)
    row = lambda w: pl.BlockSpec((tm, w), lambda i: (i, 0))
    ne = LANES
    rhi, rlo = _split_hi_lo(jnp.pad(router_w.astype(F32), ((0, 0), (0, ne - router_w.shape[1]))))
    return pl.pallas_call(
        _outproj_kernel,
        grid=(t // tm,),
        in_specs=[row(d)] + [row(GROUP_W)] * 11
                 + [_full(w_out.shape), _full((1, d)), pl.BlockSpec((1, 1, d), vec), _full((1, d)),
                    pl.BlockSpec((1, 1, d), vec), pl.BlockSpec((1, 1, d), vec),
                    _full((1, GROUP_W)), _full((1, GROUP_W)), _full((1, GROUP_W)), _full(rhi.shape), _full(rlo.shape)],
        out_specs=[row(d), row(d // 2), row(ne)],
        out_shape=[jax.ShapeDtypeStruct((t, d), F32), jax.ShapeDtypeStruct((t, d // 2), jnp.uint32),
                   jax.ShapeDtypeStruct((t, ne), F32)],
        compiler_params=_cp("parallel"),
        name="out_projection",
    )(x, *mixers, w_out, gpost, ga1, gpre, sc2, sh2, dskip, cnorm, dnorm, rhi, rlo)


def _rank_before(vals, idx, count, stride):
    rank = jnp.zeros(vals.shape, jnp.int32)
    for j in range(count):
        other = vals[j * stride:j * stride + 1, :]
        ahead = (other > vals) | ((other == vals) & (idx > j))
        rank = rank + ahead.astype(jnp.int32)
    return rank


def _xor_partner(x, row, s):
    n = x.shape[0]
    return jnp.where((row & s) == 0, pltpu.roll(x, n - s, 0), pltpu.roll(x, s, 0))


def _route(logits, router_b):
    ne = N_EXPERTS
    gsz = ne // N_EXPERT_GROUPS
    scores = _sigmoid(logits.T[:ne, :])
    tm = scores.shape[1]
    biased = scores + router_b
    row = lax.broadcasted_iota(jnp.int32, (ne, tm), 0)
    m1, m2 = biased, jnp.full((ne, tm), -jnp.inf, F32)
    s = 1
    while s < gsz:
        o1, o2 = _xor_partner(m1, row, s), _xor_partner(m2, row, s)
        m2 = jnp.maximum(jnp.minimum(m1, o1), jnp.maximum(m2, o2))
        m1 = jnp.maximum(m1, o1)
        s *= 2
    gidx = row >> (gsz.bit_length() - 1)
    group_ok = _rank_before(m1 + m2, gidx, N_EXPERT_GROUPS, gsz) < TOPK_GROUPS
    choice = jnp.where(group_ok, biased, -jnp.inf)
    rank = _rank_before(choice, row, ne, 1)
    gate = jnp.where(rank < TOP_K, scores, 0.0)
    gate = gate / jnp.sum(gate, axis=0, keepdims=True) * ROUTED_SCALE
    return gate, rank, row


def _to_token_major(x):
    n, tm = x.shape
    return jnp.concatenate([x, jnp.zeros((LANES - n, tm), x.dtype)], axis=0).T


def _router_kernel(lg_ref, rb_ref, gate_ref):
    gate, _, _ = _route(lg_ref[...], rb_ref[...])
    gate_ref[...] = _to_token_major(gate)


def _router_dispatch_kernel(lg_ref, rb_ref, gk_ref, ek_ref, pk_ref, cnt_ref, carry_s):
    i = pl.program_id(0)

    @pl.when(i == 0)
    def _():
        carry_s[...] = jnp.zeros(carry_s.shape, F32)

    gate, rank, row = _route(lg_ref[...], rb_ref[...])
    tm = gate.shape[1]
    picked = (rank < TOP_K).astype(F32)
    before = lax.broadcasted_iota(jnp.int32, (tm, tm), 0) < lax.broadcasted_iota(jnp.int32, (tm, tm), 1)
    pos = _dot(picked, before.astype(F32)) + carry_s[:, 0:1]
    carry_s[...] = carry_s[...] + jnp.sum(picked, axis=1, keepdims=True)
    gk, ek, pk = [], [], []
    for k in range(TOP_K):
        sel = rank == k
        gk.append(jnp.sum(jnp.where(sel, gate, 0.0), axis=0, keepdims=True))
        ek.append(jnp.sum(jnp.where(sel, row, 0), axis=0, keepdims=True))
        pk.append(jnp.sum(jnp.where(sel, pos, 0.0), axis=0, keepdims=True))
    gk_ref[...] = _to_token_major(jnp.concatenate(gk, axis=0))
    ek_ref[...] = jnp.concatenate(ek, axis=0)
    pk_ref[...] = jnp.concatenate(pk, axis=0).astype(jnp.int32)

    @pl.when(i == pl.num_programs(0) - 1)
    def _():
        cnt_ref[...] = carry_s[...].astype(jnp.int32)


def router_dispatch(logits, router_b):
    t, w = logits.shape
    tm = min(TOKEN_TILE, t)
    return pl.pallas_call(
        _router_dispatch_kernel,
        grid=(t // tm,),
        in_specs=[pl.BlockSpec((tm, w), lambda i: (i, 0)), _full((N_EXPERTS, 1))],
        out_specs=[pl.BlockSpec((tm, w), lambda i: (i, 0)),
                   pl.BlockSpec((TOP_K, tm), lambda i: (0, i)),
                   pl.BlockSpec((TOP_K, tm), lambda i: (0, i)),
                   _full((N_EXPERTS, LANES))],
        out_shape=[jax.ShapeDtypeStruct((t, w), F32), jax.ShapeDtypeStruct((TOP_K, t), jnp.int32),
                   jax.ShapeDtypeStruct((TOP_K, t), jnp.int32), jax.ShapeDtypeStruct((N_EXPERTS, LANES), jnp.int32)],
        scratch_shapes=[pltpu.VMEM((N_EXPERTS, LANES), F32)],
        compiler_params=_cp("arbitrary"),
        name="router_dispatch",
    )(logits, router_b.reshape(N_EXPERTS, 1).astype(F32))


def router_gates(logits, router_b):
    t, w = logits.shape
    tm = min(TOKEN_TILE, t)
    return pl.pallas_call(
        _router_kernel,
        grid=(t // tm,),
        in_specs=[pl.BlockSpec((tm, w), lambda i: (i, 0)), _full((N_EXPERTS, 1))],
        out_specs=pl.BlockSpec((tm, w), lambda i: (i, 0)),
        out_shape=jax.ShapeDtypeStruct((t, w), F32),
        compiler_params=_cp("parallel"),
        name="router_gates",
    )(logits, router_b.reshape(N_EXPERTS, 1).astype(F32))


def _moe_kernel(h_ref, gate_ref, x_ref, wg_ref, wu_ref, wd_ref, sg_ref, su_ref, sd_ref, gpost_ref, ga2_ref,
                o_ref, acc_s):
    e = pl.program_id(1)
    h = _mx(_unpack_pairs(h_ref[...]))

    @pl.when(e == 0)
    def _():
        hs = _silu(jnp.dot(h, sg_ref[...], preferred_element_type=F32)) * jnp.dot(h, su_ref[...], preferred_element_type=F32)
        acc_s[...] = jnp.dot(_mx(hs), sd_ref[...], preferred_element_type=F32)

    gates = gate_ref[...]
    lane = lax.broadcasted_iota(jnp.int32, gates.shape, 1)
    hid = []
    for j in range(MOE_EB):
        gcol = jnp.sum(jnp.where(lane == e * MOE_EB + j, gates, 0.0), axis=1, keepdims=True)
        g = jnp.dot(h, _mx(wg_ref[j]), preferred_element_type=F32)
        u = jnp.dot(h, _mx(wu_ref[j]), preferred_element_type=F32)
        hid.append(_mx(_silu(g) * u * gcol))
    wd = _mx(wd_ref[...]).reshape(MOE_EB * D_EXPERT, -1)
    acc_s[...] += jnp.dot(jnp.concatenate(hid, axis=1), wd, preferred_element_type=F32)

    @pl.when(e == pl.num_programs(1) - 1)
    def _():
        o_ref[...] = x_ref[...] + ga2_ref[0] * _rms(acc_s[...], gpost_ref[...])


def moe_ffn(h, gates, x, layer, wg, wu, wd, sg, su, sd, gpost, ga2, tiles_per_group):
    t, d = x.shape
    tm = min(MOE_TILE, t)
    _, ne, _, f = wg.shape
    row = lambda w: pl.BlockSpec((tm, w), lambda i, e: (i, 0))
    return pl.pallas_call(
        _moe_kernel,
        grid=(t // tm, ne // MOE_EB),
        in_specs=[row(h.shape[1]), row(gates.shape[1]), row(d),
                  pl.BlockSpec((None, MOE_EB, d, f), lambda i, e: (layer, e, 0, 0)),
                  pl.BlockSpec((None, MOE_EB, d, f), lambda i, e: (layer, e, 0, 0)),
                  pl.BlockSpec((None, MOE_EB, f, d), lambda i, e: (layer, e, 0, 0)),
                  _full(sg.shape), _full(su.shape), _full(sd.shape), _full((1, d)),
                  pl.BlockSpec((1, 1, d), lambda i, e: (i // tiles_per_group, 0, 0))],
        out_specs=row(d),
        out_shape=jax.ShapeDtypeStruct((t, d), F32),
        scratch_shapes=[pltpu.VMEM((tm, d), F32)],
        compiler_params=_cp("parallel", "arbitrary"),
        name="moe_ffn",
    )(h, gates, x, wg, wu, wd, sg, su, sd, gpost, ga2)


def moe_plan(counts, n_tokens):
    n_blocks = (n_tokens * TOP_K + N_EXPERTS * (MOE_BLOCK - 1) + MOE_BLOCK - 1) // MOE_BLOCK
    cnt = counts[:, 0]
    padded = (cnt + MOE_BLOCK - 1) // MOE_BLOCK * MOE_BLOCK
    pad_end = jnp.cumsum(padded)
    off = pad_end - padded
    start = jnp.arange(n_blocks, dtype=jnp.int32) * MOE_BLOCK
    be = jnp.minimum(jnp.sum(pad_end[None, :] <= start[:, None], axis=1), N_EXPERTS - 1).astype(jnp.int32)
    mine = be[:, None] == jnp.arange(N_EXPERTS, dtype=jnp.int32)[None, :]
    end = jnp.sum(jnp.where(mine, (off + cnt)[None, :], 0), axis=1)
    nv = jnp.clip(end - start, 0, MOE_BLOCK).astype(jnp.int32)
    return off.astype(jnp.int32), be, nv


def _rows_kernel(off_ref, ek_ref, pk_ref, dest_ref):
    ek = ek_ref[...]
    dest = pk_ref[...]
    for e in range(N_EXPERTS):
        dest = dest + jnp.where(ek == e, off_ref[e], 0)
    dest_ref[...] = dest


def moe_rows(off, ek, pk):
    k, t = ek.shape
    tm = min(MOE_PLAN_TILE, t)
    spec = pl.BlockSpec((k, tm), lambda i, off: (0, i))
    return pl.pallas_call(
        _rows_kernel,
        grid_spec=pltpu.PrefetchScalarGridSpec(num_scalar_prefetch=1, grid=(t // tm,),
                                               in_specs=[spec, spec], out_specs=spec),
        out_shape=jax.ShapeDtypeStruct((k, t), jnp.int32),
        compiler_params=_cp("arbitrary"),
        name="moe_rows",
    )(off, ek, pk)


U32 = jnp.uint32
HIGH_HALF = 0xFFFF0000


def _pack_pairs(x):
    w = x.shape[1] // 2
    bits = lax.bitcast_convert_type(x.astype(jnp.bfloat16).astype(F32), U32)
    return (bits[:, w:] & jnp.uint32(HIGH_HALF)) | (bits[:, :w] >> 16)


def _unpack_pairs(p):
    lo = lax.bitcast_convert_type(p << 16, F32)
    hi = lax.bitcast_convert_type(p & jnp.uint32(HIGH_HALF), F32)
    return jnp.concatenate([lo, hi], axis=1)


def _sc_workers():
    info = plsc.get_sparse_core_info()
    return info.num_cores, info.num_cores * info.num_subcores


def sc_scatter_rows(src, idx, n_rows):
    k, t = idx.shape
    w = src.shape[1]
    n_cores, n_workers = _sc_workers()
    per_worker = t // n_workers
    mesh = plsc.VectorSubcoreMesh(core_axis_name="c", subcore_axis_name="s")

    @functools.partial(
        pl.kernel, mesh=mesh, out_type=jax.ShapeDtypeStruct((n_rows, w), src.dtype),
        scratch_types=[pltpu.VMEM((k, SC_WINDOW), jnp.int32), pltpu.VMEM((SC_WINDOW, w), src.dtype),
                       pltpu.SemaphoreType.DMA])
    def scatter(s_hbm, i_hbm, o_hbm, idx_v, rows_v, sem):
        base = (lax.axis_index("s") * n_cores + lax.axis_index("c")) * per_worker

        @pl.loop(0, per_worker // SC_WINDOW)
        def _(j):
            off = base + j * SC_WINDOW
            pltpu.sync_copy(i_hbm.at[:, pl.ds(off, SC_WINDOW)], idx_v)
            pltpu.sync_copy(s_hbm.at[pl.ds(off, SC_WINDOW)], rows_v)
            for kk in range(k):
                pltpu.async_copy(rows_v, o_hbm.at[idx_v.at[kk]], sem).wait()

    return scatter(src, idx)


def sc_gather_rows(table, idx):
    n = idx.shape[0]
    w = table.shape[1]
    n_cores, n_workers = _sc_workers()
    per_worker = n // n_workers
    mesh = plsc.VectorSubcoreMesh(core_axis_name="c", subcore_axis_name="s")

    @functools.partial(
        pl.kernel, mesh=mesh, out_type=jax.ShapeDtypeStruct((n, w), table.dtype),
        scratch_types=[pltpu.VMEM((SC_WINDOW,), jnp.int32), pltpu.VMEM((SC_WINDOW, w), table.dtype),
                       pltpu.SemaphoreType.DMA])
    def gather(t_hbm, i_hbm, o_hbm, idx_v, rows_v, sem):
        base = (lax.axis_index("s") * n_cores + lax.axis_index("c")) * per_worker

        @pl.loop(0, per_worker // SC_WINDOW)
        def _(j):
            off = base + j * SC_WINDOW
            pltpu.sync_copy(i_hbm.at[pl.ds(off, SC_WINDOW)], idx_v)
            pltpu.async_copy(t_hbm.at[idx_v], rows_v, sem).wait()
            pltpu.sync_copy(rows_v, o_hbm.at[pl.ds(off, SC_WINDOW)])

    return gather(table, idx)


def _expert_kernel(be_ref, nv_ref, xs_ref, wg_ref, wu_ref, wd_ref, ys_ref):
    nv = nv_ref[pl.program_id(0)]

    @pl.when(nv > 0)
    def _():
        x = _unpack_pairs(xs_ref[...])
        rows = lax.broadcasted_iota(jnp.int32, x.shape, 0)
        x = _mx(jnp.where(rows < nv, x, 0.0))
        hid = (_silu(jnp.dot(x, _mx(wg_ref[0]), preferred_element_type=F32))
               * jnp.dot(x, _mx(wu_ref[0]), preferred_element_type=F32))
        ys_ref[...] = _pack_pairs(jnp.dot(_mx(hid), _mx(wd_ref[0]), preferred_element_type=F32))

    @pl.when(nv == 0)
    def _():
        ys_ref[...] = jnp.zeros(ys_ref.shape, U32)


def moe_experts(xs, be, nv, layer, wg, wu, wd):
    n_rows, w = xs.shape
    _, _, d, f = wg.shape
    return pl.pallas_call(
        _expert_kernel,
        grid_spec=pltpu.PrefetchScalarGridSpec(
            num_scalar_prefetch=2,
            grid=(n_rows // MOE_BLOCK,),
            in_specs=[pl.BlockSpec((MOE_BLOCK, w), lambda b, be, nv: (b, 0)),
                      pl.BlockSpec((None, 1, d, f), lambda b, be, nv: (layer, be[b], 0, 0)),
                      pl.BlockSpec((None, 1, d, f), lambda b, be, nv: (layer, be[b], 0, 0)),
                      pl.BlockSpec((None, 1, f, d), lambda b, be, nv: (layer, be[b], 0, 0))],
            out_specs=pl.BlockSpec((MOE_BLOCK, w), lambda b, be, nv: (b, 0))),
        out_shape=jax.ShapeDtypeStruct((n_rows, w), U32),
        compiler_params=_cp("arbitrary"),
        name="moe_experts",
    )(be, nv, xs, wg, wu, wd)


def _combine_kernel(yg_ref, gk_ref, hp_ref, x_ref, sg_ref, su_ref, sd_ref, gpost_ref, ga2_ref, o_ref):
    h = _mx(_unpack_pairs(hp_ref[...]))
    hs = _silu(jnp.dot(h, sg_ref[...], preferred_element_type=F32)) * jnp.dot(h, su_ref[...], preferred_element_type=F32)
    f = jnp.dot(_mx(hs), sd_ref[...], preferred_element_type=F32)
    gk = gk_ref[...]
    for k in range(TOP_K):
        f = f + gk[:, k:k + 1] * _unpack_pairs(yg_ref[k])
    o_ref[...] = x_ref[...] + ga2_ref[0] * _rms(f, gpost_ref[...])


def moe_combine(yg, gk, hp, x, sg, su, sd, gpost, ga2, tiles_per_group):
    t, d = x.shape
    tm = min(MOE_ROW_TILE, t)
    w = hp.shape[1]
    row = lambda n: pl.BlockSpec((tm, n), lambda i: (i, 0))
    return pl.pallas_call(
        _combine_kernel,
        grid=(t // tm,),
        in_specs=[pl.BlockSpec((TOP_K, tm, w), lambda i: (0, i, 0)),
                  row(gk.shape[1]), row(w), row(d), _full(sg.shape), _full(su.shape), _full(sd.shape), _full((1, d)),
                  pl.BlockSpec((1, 1, d), lambda i: (i // tiles_per_group, 0, 0))],
        out_specs=row(d),
        out_shape=jax.ShapeDtypeStruct((t, d), F32),
        compiler_params=_cp("parallel"),
        name="moe_combine",
    )(yg, gk, hp, x, sg, su, sd, gpost, ga2)


def _reorder_w_in(w_in):
    c = np.cumsum((0,) + (GROUP_W, GROUP_W, GROUP_W, GROUP_W, GROUP_W, GROUP_W, 2 * SSD_STATE, 2 * SSD_STATE,
                          GROUP_W, 2 * N_HEADS, GROUP_W, GROUP_W, GROUP_W, GROUP_W, 2 * N_HEADS, 2 * N_HEADS))
    seg = lambda a, b: w_in[:, c[a]:c[b]]
    small = jnp.concatenate([seg(9, 10), seg(14, 15), seg(15, 16),
                             jnp.zeros((w_in.shape[0], LANES - 6 * N_HEADS), w_in.dtype)], axis=1)
    return jnp.concatenate([seg(0, 5), seg(5, 8), seg(8, 9), seg(10, 13), seg(13, 14), small], axis=1)


def kernel(x, c, ctx, c_ctx, w_mod, b_mod, g_pre_mix, g_post_mix, g_pre_ffn, g_post_ffn, w_in, w_out, lru_conv_w, lru_conv_b, lru_wa, lru_ba, lru_wx, lru_bx, lru_lambda, na_bias, ssd_conv_w, ssd_conv_b, ssd_a_log, ssd_dt_bias, ssd_d, ssd_norm, gdn_conv_w, gdn_a_log, gdn_dt_bias, gdn_norm, router_w, router_b, we_gate, we_up, we_down, ws_gate, ws_up, ws_down):
    bsz, seq, d = x.shape
    n_ctx = ctx.shape[1]
    depth = w_mod.shape[0]
    lat_tpg = seq // min(TOKEN_TILE, seq)
    ctx_tpg = max(bsz * n_ctx // TOKEN_TILE, 1)
    ctx_mpg = max(bsz * n_ctx // MOE_TILE, 1)

    cond = _pad_rows(jnp.concatenate([c, c_ctx[None, :]], axis=0))
    mod = modulation(cond, w_mod, b_mod).reshape(depth, SUBLANES, N_MOD, d)
    rope = rope_tables(seq)
    row = lambda v: v[None, :].astype(F32)

    xl = x.reshape(bsz * seq, d)
    xc = ctx.reshape(bsz * n_ctx, d)
    for l in range(depth):
        last = l == depth - 1
        m_lat = [mod[l, :bsz, k][:, None, :] for k in range(N_MOD)]
        m_ctx = [mod[l, bsz:bsz + 1, k][:, None, :] for k in range(N_MOD)]
        w_in_l = _reorder_w_in(w_in[l]).astype(MXU_DTYPE)
        pc = in_projection(xc, row(g_pre_mix[l]), m_ctx[1], m_ctx[0], w_in_l, ctx_tpg)
        pl_ = in_projection(xl, row(g_pre_mix[l]), m_lat[1], m_lat[0], w_in_l, lat_tpg)

        lru_p = lru_params(lru_conv_w[l], lru_conv_b[l], lru_wa[l], lru_ba[l], lru_wx[l], lru_bx[l], lru_lambda[l])
        a_cf, a_cb, a_st = lru_mixer(pc[P_AX], jnp.zeros((bsz, SUBLANES, GROUP_W), F32), bsz, *lru_p)
        a_lf, a_lb, _ = lru_mixer(pl_[P_AX], a_st, bsz, *lru_p)

        kc = pc[P_BK].reshape(bsz, n_ctx, GROUP_W)
        vc = pc[P_BV].reshape(bsz, n_ctx, GROUP_W)
        b_c = ctx_attention(pc[P_BQ].reshape(bsz, n_ctx, GROUP_W), kc, vc).reshape(bsz * n_ctx, GROUP_W)
        b_l = na_mixer(pl_[P_BQ], pl_[P_BK], pl_[P_BV], kc, vc, na_bias_slabs(na_bias[l]), bsz)

        ssd_p = ssd_params(ssd_conv_w[l], ssd_conv_b[l], ssd_a_log[l], ssd_dt_bias[l])
        c_cf, c_cb, c_cx, c_st = ssd_mixer(pc[P_CX], pc[P_SM], jnp.zeros((bsz, 2, N_HEADS, SSD_STATE, HEAD_DIM), F32),
                                           bsz, *ssd_p)
        c_lf, c_lb, c_lx, _ = ssd_mixer(pl_[P_CX], pl_[P_SM], c_st, bsz, *ssd_p)

        gdn_p = gdn_params(gdn_conv_w[l], gdn_a_log[l], gdn_dt_bias[l])
        d_cf, d_cb, d_st = gdn_mixer(pc[P_DX], pc[P_SM], jnp.zeros((bsz, 2, N_HEADS, HEAD_DIM, HEAD_DIM), F32),
                                     bsz, *gdn_p)
        d_lf, d_lb, _ = gdn_mixer(pl_[P_DX], pl_[P_SM], d_st, bsz, *gdn_p, rope=rope)

        epi = (w_out[l].astype(MXU_DTYPE), row(g_post_mix[l]))
        epi_tail = (row(jnp.repeat(ssd_d[l], HEAD_DIM)), row(ssd_norm[l]), row(jnp.tile(gdn_norm[l], N_HEADS)), router_w[l])
        routed_w = (l, we_gate, we_up, we_down)
        shared_w = (ws_gate[l].astype(MXU_DTYPE), ws_up[l].astype(MXU_DTYPE), ws_down[l].astype(MXU_DTYPE),
                    row(g_post_ffn[l]))

        mix_l = (a_lf, a_lb, pl_[P_AG], b_l, c_lf, c_lb, c_lx, pl_[P_CZ], d_lf, d_lb, pl_[P_DZ])
        xl, hp, lg = out_projection(xl, mix_l, *epi, m_lat[2], row(g_pre_ffn[l]), m_lat[4], m_lat[3], *epi_tail, lat_tpg)
        gk, ek, pk, cnt = router_dispatch(lg, router_b[l])
        off, be, nv = moe_plan(cnt, bsz * seq)
        dest = moe_rows(off, ek, pk)
        xs = sc_scatter_rows(hp, dest, be.shape[0] * MOE_BLOCK)
        ys = moe_experts(xs, be, nv, *routed_w)
        yg = sc_gather_rows(ys, dest.reshape(-1)).reshape(TOP_K, bsz * seq, d // 2)
        xl = moe_combine(yg, gk, hp, xl, *shared_w, m_lat[5], seq // min(MOE_ROW_TILE, seq))
        if not last:
            mix_c = (a_cf, a_cb, pc[P_AG], b_c, c_cf, c_cb, c_cx, pc[P_CZ], d_cf, d_cb, pc[P_DZ])
            xc, hp, lg = out_projection(xc, mix_c, *epi, m_ctx[2], row(g_pre_ffn[l]), m_ctx[4], m_ctx[3], *epi_tail, ctx_tpg)
            xc = moe_ffn(hp, router_gates(lg, router_b[l]), xc, *routed_w, *shared_w, m_ctx[5], ctx_mpg)
    return xl.reshape(bsz, seq, d)
```

```python
import functools
import math

import jax
import jax.numpy as jnp
import numpy as np
from jax import lax
from jax.experimental import pallas as pl
from jax.experimental.pallas import tpu as pltpu
from jax.experimental.pallas import tpu_sc as plsc

F32 = jnp.float32
MXU_DTYPE = jnp.bfloat16
HI = lax.Precision.HIGHEST

D_MODEL = 1024
GRID_W = 64
GROUP_W = 256
HEAD_DIM = 64
N_HEADS = 4
EPS = 1e-6
NEG_INF = -1e30
N_MOD = 6
LRU_C = 8.0
NA_WIN_ROWS = 8
NA_WIN_COLS = 16
SSD_STATE = 128
SSD_GROUPS = 2
ROPE_BASE = 10000.0
ROPE_AXIS_DIM = HEAD_DIM // 2
N_EXPERTS = 64
N_EXPERT_GROUPS = 8
TOPK_GROUPS = 4
TOP_K = 8
D_EXPERT = 256
ROUTED_SCALE = 2.5

LANES = 128
SUBLANES = 8
VMEM_LIMIT = 56 * 1024 * 1024

TOKEN_TILE = 512
LRU_CHUNK = 256
SSD_CHUNK = 128
GDN_CHUNK = 64
GDN_TILE = 256
GDN_SUB = 128
GDN_BASE = 16
MOE_TILE = 1024
MOE_EB = 4
MOE_BLOCK = 1024
MOE_ROW_TILE = 256
MOE_PLAN_TILE = 2048
SC_WINDOW = 128

P_WIDTHS = (256, 768, 768, 256, 256, 256, 256, 256, 256, 128)
(P_AX, P_CX, P_DX, P_AG, P_BQ, P_BK, P_BV, P_CZ, P_DZ, P_SM) = range(10)
P_CONV_GROUPS = 3
SM_DT, SM_BETA, SM_DECAY = 0, 8, 16


def _cp(*sem):
    return pltpu.CompilerParams(dimension_semantics=sem, vmem_limit_bytes=VMEM_LIMIT)


def _mx(x):
    return x.astype(MXU_DTYPE)


def _dot(a, b):
    return jnp.dot(_mx(a), _mx(b), preferred_element_type=F32)


def _dot_nt(a, b):
    return lax.dot_general(_mx(a), _mx(b), (((1,), (1,)), ((), ())), preferred_element_type=F32)


def _dot_tn(a, b):
    return lax.dot_general(_mx(a), _mx(b), (((0,), (0,)), ((), ())), preferred_element_type=F32)


def _dot_hi(a, b):
    return jnp.dot(a, b, preferred_element_type=F32, precision=HI)


def _sigmoid(x):
    return 1.0 / (1.0 + jnp.exp(-x))


def _silu(x):
    return x * _sigmoid(x)


def _softplus(x):
    return jnp.maximum(x, 0.0) + jnp.log1p(jnp.exp(-jnp.abs(x)))


def _gelu_tanh(x):
    return 0.5 * x * (1.0 + jnp.tanh(math.sqrt(2.0 / math.pi) * (x + 0.044715 * (x * x * x))))


def _rms(x, g):
    return x * lax.rsqrt(jnp.mean(x * x, axis=-1, keepdims=True) + EPS) * g


def _full(shape):
    n = len(shape)
    return pl.BlockSpec(shape, lambda *_: (0,) * n)


MOD_COLS = 1536


def _mod_kernel(c_ref, w_ref, b_ref, o_ref):
    o_ref[0] = _dot_hi(_silu(c_ref[...]), w_ref[0]) + b_ref[0]


def modulation(cond, w_mod, b_mod):
    depth, d, n = w_mod.shape
    return pl.pallas_call(
        _mod_kernel,
        grid=(depth, n // MOD_COLS),
        in_specs=[pl.BlockSpec((SUBLANES, d), lambda l, j: (0, 0)),
                  pl.BlockSpec((1, d, MOD_COLS), lambda l, j: (l, 0, j)),
                  pl.BlockSpec((1, 1, MOD_COLS), lambda l, j: (l, 0, j))],
        out_specs=pl.BlockSpec((1, SUBLANES, MOD_COLS), lambda l, j: (l, 0, j)),
        out_shape=jax.ShapeDtypeStruct((depth, SUBLANES, n), F32),
        compiler_params=_cp("parallel", "parallel"),
        name="modulation",
    )(cond, w_mod, b_mod.reshape(depth, 1, n))


def _inproj_kernel(*refs, tiles_per_seq, rope):
    (x_ref, xp_ref, xn_ref, g_ref, sc_ref, sh_ref, w_ref, lcw_ref, lcb_ref, scw_ref, scb_ref, gcw_ref) = refs[:12]
    cos_ref, sin_ref = (refs[12], refs[13]) if rope else (None, None)
    o_refs = refs[14:] if rope else refs[12:]
    i = pl.program_id(0)
    norm = lambda v: _rms(v, g_ref[...]) * (1.0 + sc_ref[0]) + sh_ref[0]
    p = _dot(norm(x_ref[...]), w_ref[...])
    n_conv = sum(P_WIDTHS[:P_CONV_GROUPS])
    ph = _dot(norm(jnp.concatenate([xp_ref[...], xn_ref[...]], axis=0)), w_ref[:, :n_conv])
    pos = i % tiles_per_seq
    prev = jnp.where(pos == 0, 0.0, ph[:SUBLANES])
    nxt = jnp.where(pos == tiles_per_seq - 1, 0.0, ph[SUBLANES:])
    row = lax.broadcasted_iota(jnp.int32, (SUBLANES, n_conv), 0)
    halo = jnp.where(row < 2, pltpu.roll(prev, 2, 0), jnp.where(row == 2, pltpu.roll(nxt, 2, 0), 0.0))
    c0, c1, c2 = GROUP_W, GROUP_W + 3 * GROUP_W, n_conv
    lru_u = _dwconv(p[:, :c0], halo[:, :c0], lcw_ref[...], lcb_ref[...])
    ssd_x = _silu(_dwconv(p[:, c0:c1], halo[:, c0:c1], scw_ref[...], scb_ref[...]))
    qkv = _silu(_dwconv(p[:, c1:c2], halo[:, c1:c2], gcw_ref[...]))
    qn = _l2norm_heads(qkv[:, :GROUP_W])
    kn = _l2norm_heads(qkv[:, GROUP_W:2 * GROUP_W])
    if rope:
        cos, sin = cos_ref[...], sin_ref[...]
        qn = qn * cos + _swap16(qn) * sin
        kn = kn * cos + _swap16(kn) * sin
    outs = [lru_u, ssd_x, jnp.concatenate([qn * (HEAD_DIM ** -0.5), kn, qkv[:, 2 * GROUP_W:]], axis=1)]
    off = n_conv
    for o_ref, w in zip(o_refs, P_WIDTHS):
        if outs:
            o_ref[...] = outs.pop(0)
        else:
            o_ref[...] = p[:, off:off + w]
            off += w


def in_projection(x, g, sc, sh, w, conv, rope, seq_len, tiles_per_group):
    t, d = x.shape
    tm = min(TOKEN_TILE, seq_len)
    tps = seq_len // tm
    hb = tm // SUBLANES
    vec = lambda i: (i // tiles_per_group, 0, 0)
    ins = [x, x, x, g, sc, sh, w, *conv]
    specs = [pl.BlockSpec((tm, d), lambda i: (i, 0)),
             pl.BlockSpec((SUBLANES, d), lambda i: (jnp.maximum(i * hb - 1, 0), 0)),
             pl.BlockSpec((SUBLANES, d), lambda i: (jnp.minimum((i + 1) * hb, t // SUBLANES - 1), 0)),
             _full((1, d)), pl.BlockSpec((1, 1, d), vec), pl.BlockSpec((1, 1, d), vec), _full(w.shape)]
    specs += [_full(a.shape) for a in conv]
    if rope is not None:
        ins += list(rope)
        specs += [pl.BlockSpec((tm, GROUP_W), lambda i: (i % tps, 0))] * 2
    return pl.pallas_call(
        functools.partial(_inproj_kernel, tiles_per_seq=tps, rope=rope is not None),
        grid=(t // tm,),
        in_specs=specs,
        out_specs=[pl.BlockSpec((tm, wd), lambda i: (i, 0)) for wd in P_WIDTHS],
        out_shape=[jax.ShapeDtypeStruct((t, wd), F32) for wd in P_WIDTHS],
        compiler_params=_cp("parallel"),
        name="in_projection",
    )(*ins)


def _dwconv(x, halo, w, b=None):
    q = x.shape[0]
    row = lax.broadcasted_iota(jnp.int32, (SUBLANES, x.shape[1]), 0)

    def shifted(s, keep_rolled, edge):
        r = pltpu.roll(x, s % q, 0)
        if s > 0:
            return jnp.concatenate([jnp.where(keep_rolled, r[:SUBLANES], edge), r[SUBLANES:]], axis=0)
        return jnp.concatenate([r[:q - SUBLANES], jnp.where(keep_rolled, r[q - SUBLANES:], edge)], axis=0)

    xm2 = shifted(2, row >= 2, halo)
    xm1 = shifted(1, row >= 1, pltpu.roll(halo, SUBLANES - 1, 0))
    xp1 = shifted(-1, row < SUBLANES - 1, pltpu.roll(halo, SUBLANES - 3, 0))
    y = w[0:1] * xm2 + w[1:2] * xm1 + w[2:3] * x + w[3:4] * xp1
    return y if b is None else y + b


def _pad_rows(a, rows=SUBLANES):
    return jnp.concatenate([a, jnp.zeros((rows - a.shape[0],) + a.shape[1:], a.dtype)], axis=0)


def _chunk_specs(nc, q, c):
    fwd = pl.BlockSpec((q, c), lambda b, i: (b * nc + i, 0))
    bwd = pl.BlockSpec((q, c), lambda b, i: (b * nc + nc - 1 - i, 0))
    return fwd, bwd


def _lru_kernel(xf_ref, xb_ref, h0_ref, wg_ref, bg_ref, lam_ref,
                yf_ref, yb_ref, hfin_ref, af_s, bf_s, ab_s, bb_s, carry_s):
    i = pl.program_id(1)
    q = xf_ref.shape[0]

    @pl.when(i == 0)
    def _():
        carry_s[...] = h0_ref[0]

    def coeffs(x_ref, d, a_s, b_s):
        u = x_ref[...]
        g = _dot(u, wg_ref[:, 2 * GROUP_W * d:2 * GROUP_W * (d + 1)]) + bg_ref[:, 2 * GROUP_W * d:2 * GROUP_W * (d + 1)]
        r = _sigmoid(g[:, :GROUP_W])
        gate_in = _sigmoid(g[:, GROUP_W:])
        log_a = -LRU_C * r * _softplus(-lam_ref[d:d + 1, :])
        a_s[...] = jnp.exp(log_a)
        b_s[...] = jnp.sqrt(1.0 - jnp.exp(2.0 * log_a)) * (gate_in * u)

    coeffs(xf_ref, 0, af_s, bf_s)
    coeffs(xb_ref, 1, ab_s, bb_s)

    ng = q // SUBLANES
    row = lax.broadcasted_iota(jnp.int32, (SUBLANES, GROUP_W), 0)

    def body(g, hs):
        h_f, h_b = hs
        i0 = pl.multiple_of(g * SUBLANES, SUBLANES)
        a = af_s[pl.ds(i0, SUBLANES), :]
        b = bf_s[pl.ds(i0, SUBLANES), :]
        for s in (1, 2, 4):
            m = row >= s
            b = jnp.where(m, a * pltpu.roll(b, s, 0) + b, b)
            a = jnp.where(m, a * pltpu.roll(a, s, 0), a)
        h = b + a * h_f
        yf_ref[pl.ds(i0, SUBLANES), :] = h
        h_f = h[SUBLANES - 1:SUBLANES, :]
        j0 = pl.multiple_of((ng - 1 - g) * SUBLANES, SUBLANES)
        a = ab_s[pl.ds(j0, SUBLANES), :]
        b = bb_s[pl.ds(j0, SUBLANES), :]
        for s in (1, 2, 4):
            m = row < SUBLANES - s
            b = jnp.where(m, a * pltpu.roll(b, SUBLANES - s, 0) + b, b)
            a = jnp.where(m, a * pltpu.roll(a, SUBLANES - s, 0), a)
        h = b + a * h_b
        yb_ref[pl.ds(j0, SUBLANES), :] = h
        return h_f, h[0:1, :]

    h_f, h_b = lax.fori_loop(0, ng, body, (carry_s[0:1, :], carry_s[1:2, :]))
    carry_s[0:1, :] = h_f
    carry_s[1:2, :] = h_b

    @pl.when(i == pl.num_programs(1) - 1)
    def _():
        hfin_ref[0] = carry_s[...]


def _block_diag(w):
    h, a, b = w.shape
    return jnp.einsum('hij,hg->higj', w, jnp.eye(h, dtype=w.dtype)).reshape(h * a, h * b)


def lru_params(wa, ba, wx, bx, lam):
    wg = jnp.concatenate([_block_diag(wa[0]), _block_diag(wx[0]), _block_diag(wa[1]), _block_diag(wx[1])], axis=1)
    bg = jnp.concatenate([ba[0], bx[0], ba[1], bx[1]])[None, :]
    return wg.astype(MXU_DTYPE), bg, _pad_rows(lam)


def lru_mixer(x, h0, bsz, wg, bg, lam):
    t, c = x.shape
    s = t // bsz
    q = min(LRU_CHUNK, s)
    nc = s // q
    xf, xb = _chunk_specs(nc, q, c)
    st = pl.BlockSpec((1, SUBLANES, c), lambda b, i: (b, 0, 0))
    return pl.pallas_call(
        _lru_kernel,
        grid=(bsz, nc),
        in_specs=[xf, xb, st, _full(wg.shape), _full(bg.shape), _full(lam.shape)],
        out_specs=[xf, xb, st],
        out_shape=[jax.ShapeDtypeStruct((t, c), F32), jax.ShapeDtypeStruct((t, c), F32),
                   jax.ShapeDtypeStruct((bsz, SUBLANES, c), F32)],
        scratch_shapes=[pltpu.VMEM((q, c), F32)] * 4 + [pltpu.VMEM((SUBLANES, c), F32)],
        compiler_params=_cp("parallel", "arbitrary"),
        name="lru_mixer",
    )(x, x, h0, wg, bg, lam)


NA_KEYS = NA_WIN_ROWS * GRID_W
NA_ROW_BLOCK = 8


def na_bias_slabs(table):
    qc = np.arange(GRID_W)[:, None]
    kc = np.arange(GRID_W)[None, :]
    win0 = np.clip(qc - NA_WIN_COLS // 2, 0, GRID_W - NA_WIN_COLS)
    ok = (kc >= win0) & (kc < win0 + NA_WIN_COLS)
    dc = np.clip(kc - qc + NA_WIN_COLS - 1, 0, 2 * NA_WIN_COLS - 2)
    dr = np.arange(NA_WIN_ROWS)[:, None] + np.arange(NA_WIN_ROWS)[None, :]
    b = table.astype(F32)[:, dr][:, :, :, dc]
    b = jnp.where(ok[None, None, None], b, NEG_INF)
    h = table.shape[0]
    return b.transpose(0, 1, 3, 2, 4).reshape(h, NA_WIN_ROWS, GRID_W, NA_KEYS)


def _na_span_start(j, rows):
    return jnp.clip(j * NA_ROW_BLOCK - NA_WIN_ROWS // 2, 0, rows - (NA_ROW_BLOCK + NA_WIN_ROWS - 1))


def _na_kernel(q_ref, kw_ref, vw_ref, kc_ref, vc_ref, slab_ref, o_ref, *, rows):
    j = pl.program_id(1)
    ustart = _na_span_start(j, rows)
    q = q_ref[...] * (HEAD_DIM ** -0.5)
    kc, vc = kc_ref[0], vc_ref[0]
    heads = [slice(h * HEAD_DIM, (h + 1) * HEAD_DIM) for h in range(N_HEADS)]
    qrows = [slice(i * GRID_W, (i + 1) * GRID_W) for i in range(NA_ROW_BLOCK)]
    kws, vws, offs = [], [], []
    for i in range(NA_ROW_BLOCK):
        r = j * NA_ROW_BLOCK + i
        r0 = jnp.clip(r - NA_WIN_ROWS // 2, 0, rows - NA_WIN_ROWS)
        start = pl.multiple_of((r0 - ustart) * GRID_W, GRID_W)
        kws.append(kw_ref[pl.ds(start, NA_KEYS), :])
        vws.append(vw_ref[pl.ds(start, NA_KEYS), :])
        offs.append(r0 - r + NA_WIN_ROWS - 1)
    s_ctx = [_dot_nt(q[:, sl], kc[:, sl]) for sl in heads]
    s_loc = [[_dot_nt(q[qr, sl], kws[i][:, sl]) + slab_ref[h, offs[i]] for h, sl in enumerate(heads)]
             for i, qr in enumerate(qrows)]
    m = [[jnp.maximum(jnp.max(s_loc[i][h], axis=-1, keepdims=True), jnp.max(s_ctx[h][qr], axis=-1, keepdims=True))
          for h in range(N_HEADS)] for i, qr in enumerate(qrows)]
    p_loc = [[jnp.exp(s_loc[i][h] - m[i][h]) for h in range(N_HEADS)] for i in range(NA_ROW_BLOCK)]
    p_ctx = [jnp.exp(s_ctx[h] - jnp.concatenate([m[i][h] for i in range(NA_ROW_BLOCK)], axis=0))
             for h in range(N_HEADS)]
    o_ctx = [_dot(p_ctx[h], vc[:, sl]) for h, sl in enumerate(heads)]
    rows_out = []
    for i, qr in enumerate(qrows):
        outs = []
        for h, sl in enumerate(heads):
            den = jnp.sum(p_loc[i][h], axis=-1, keepdims=True) + jnp.sum(p_ctx[h][qr], axis=-1, keepdims=True)
            outs.append((_dot(p_loc[i][h], vws[i][:, sl]) + o_ctx[h][qr]) / den)
        rows_out.append(jnp.concatenate(outs, axis=1))
    o_ref[...] = jnp.concatenate(rows_out, axis=0)


def na_mixer(q, k, v, kc, vc, slabs, bsz):
    t, c = q.shape
    s = t // bsz
    rows = s // GRID_W
    n_ctx = kc.shape[1]
    span = (NA_ROW_BLOCK + NA_WIN_ROWS - 1) * GRID_W

    def win(b, j):
        return ((b * rows + _na_span_start(j, rows)) * GRID_W, 0)

    wspec = pl.BlockSpec((pl.Element(span), pl.Element(c)), win)
    cspec = pl.BlockSpec((1, n_ctx, c), lambda b, j: (b, 0, 0))
    qspec = pl.BlockSpec((NA_ROW_BLOCK * GRID_W, c), lambda b, j: (b * (rows // NA_ROW_BLOCK) + j, 0))
    return pl.pallas_call(
        functools.partial(_na_kernel, rows=rows),
        grid=(bsz, rows // NA_ROW_BLOCK),
        in_specs=[qspec, wspec, wspec, cspec, cspec, _full(slabs.shape)],
        out_specs=qspec,
        out_shape=jax.ShapeDtypeStruct((t, c), F32),
        compiler_params=_cp("parallel", "arbitrary"),
        name="na_mixer",
    )(q, k, v, kc, vc, slabs)


def _ctx_attn_kernel(q_ref, k_ref, v_ref, o_ref):
    q = q_ref[0] * (HEAD_DIM ** -0.5)
    k, v = k_ref[0], v_ref[0]
    outs = []
    for h in range(N_HEADS):
        sl = slice(h * HEAD_DIM, (h + 1) * HEAD_DIM)
        s = _dot_nt(q[:, sl], k[:, sl])
        p = jnp.exp(s - jnp.max(s, axis=-1, keepdims=True))
        outs.append(_dot(p, v[:, sl]) / jnp.sum(p, axis=-1, keepdims=True))
    o_ref[0] = jnp.concatenate(outs, axis=1)


def ctx_attention(q, k, v):
    spec = pl.BlockSpec((1,) + q.shape[1:], lambda b: (b, 0, 0))
    return pl.pallas_call(
        _ctx_attn_kernel,
        grid=(q.shape[0],),
        in_specs=[spec, spec, spec],
        out_specs=spec,
        out_shape=jax.ShapeDtypeStruct(q.shape, F32),
        compiler_params=_cp("parallel"),
        name="ctx_attention",
    )(q, k, v)


def _small_vec(vals, off):
    v = jnp.zeros((LANES,), F32).at[off:off + 2 * N_HEADS].set(vals.reshape(-1).astype(F32))
    return v[None, :]


def _lane_mask(off):
    lane = lax.broadcasted_iota(jnp.int32, (1, LANES), 1)
    return (lane >= off) & (lane < off + 2 * N_HEADS)


def _tri_masks(q):
    rowi = lax.broadcasted_iota(jnp.int32, (q, q), 0)
    coli = lax.broadcasted_iota(jnp.int32, (q, q), 1)
    return rowi, coli


def _ssd_kernel(xf_ref, xb_ref, sf_ref, sb_ref, h0_ref, dtb_ref, alog_ref,
                yf_ref, yb_ref, hfin_ref, state_s):
    i = pl.program_id(1)
    q = xf_ref.shape[0]

    @pl.when(i == 0)
    def _():
        state_s[...] = h0_ref[0]

    rowi, coli = _tri_masks(q)
    a_neg = jnp.where(_lane_mask(SM_DT), -jnp.exp(alog_ref[...]), 0.0)

    def direction(x_ref, sm_ref, d, y_ref):
        xbc = x_ref[...]
        dt = _softplus(sm_ref[...] + dtb_ref[...])
        keep = (rowi >= coli) if d == 0 else (rowi <= coli)
        acum = _dot_hi(keep.astype(F32), dt * a_neg)
        acum_t = acum.T
        last = acum[q - 1:q, :] if d == 0 else acum[0:1, :]
        dec_end = jnp.exp(last - acum)
        e_acum = jnp.exp(acum)
        e_last = jnp.exp(last)
        ys = []
        for g in range(SSD_GROUPS):
            bg = xbc[:, GROUP_W + SSD_STATE * g:GROUP_W + SSD_STATE * (g + 1)]
            cg = xbc[:, GROUP_W + SSD_STATE * (SSD_GROUPS + g):GROUP_W + SSD_STATE * (SSD_GROUPS + g + 1)]
            cbt = _dot_nt(cg, bg)
            for hh in range(N_HEADS // SSD_GROUPS):
                h = g * (N_HEADS // SSD_GROUPS) + hh
                ln = SM_DT + N_HEADS * d + h
                lmat = jnp.exp(jnp.where(keep, acum[:, ln:ln + 1] - acum_t[ln:ln + 1, :], NEG_INF))
                xdt = xbc[:, h * HEAD_DIM:(h + 1) * HEAD_DIM] * dt[:, ln:ln + 1]
                st = state_s[d, h]
                ys.append(_dot(cbt * lmat, xdt) + _dot(cg * e_acum[:, ln:ln + 1], st))
                state_s[d, h] = st * e_last[:, ln:ln + 1] + _dot_tn(bg * dec_end[:, ln:ln + 1], xdt)
        y_ref[...] = jnp.concatenate(ys, axis=1)

    direction(xf_ref, sf_ref, 0, yf_ref)
    direction(xb_ref, sb_ref, 1, yb_ref)

    @pl.when(i == pl.num_programs(1) - 1)
    def _():
        hfin_ref[0] = state_s[...]


def ssd_params(a_log, dt_bias):
    return _small_vec(dt_bias, SM_DT), _small_vec(a_log, SM_DT)


def ssd_mixer(xbc, sm, h0, bsz, dtb, alog):
    t, c = xbc.shape
    s = t // bsz
    q = min(SSD_CHUNK, s)
    nc = s // q
    xf, xb = _chunk_specs(nc, q, c)
    sf, sb = _chunk_specs(nc, q, LANES)
    yf, yb = _chunk_specs(nc, q, GROUP_W)
    st = pl.BlockSpec((1,) + h0.shape[1:], lambda b, i: (b, 0, 0, 0, 0))
    y_shape = jax.ShapeDtypeStruct((t, GROUP_W), F32)
    return pl.pallas_call(
        _ssd_kernel,
        grid=(bsz, nc),
        in_specs=[xf, xb, sf, sb, st, _full(dtb.shape), _full(alog.shape)],
        out_specs=[yf, yb, st],
        out_shape=[y_shape, y_shape, jax.ShapeDtypeStruct(h0.shape, F32)],
        scratch_shapes=[pltpu.VMEM(h0.shape[1:], F32)],
        compiler_params=_cp("parallel", "arbitrary"),
        name="ssd_mixer",
    )(xbc, xbc, sm, sm, h0, dtb, alog)


def rope_tables(seq):
    t = jnp.arange(seq)
    row = (t // GRID_W).astype(F32)
    col = (t % GRID_W).astype(F32)
    inv = ROPE_BASE ** (-jnp.arange(0, ROPE_AXIS_DIM, 2, dtype=F32) / ROPE_AXIS_DIM)
    ar, ac = row[:, None] * inv, col[:, None] * inv
    cos = jnp.concatenate([jnp.cos(ar), jnp.cos(ar), jnp.cos(ac), jnp.cos(ac)], axis=1)
    sin = jnp.concatenate([-jnp.sin(ar), jnp.sin(ar), -jnp.sin(ac), jnp.sin(ac)], axis=1)
    return jnp.tile(cos, (1, N_HEADS)), jnp.tile(sin, (1, N_HEADS))


def _swap16(x):
    lane = lax.broadcasted_iota(jnp.int32, x.shape, 1)
    half = ROPE_AXIS_DIM // 2
    return jnp.where((lane & (ROPE_AXIS_DIM - 1)) < half,
                     pltpu.roll(x, x.shape[1] - half, 1), pltpu.roll(x, half, 1))


def _head_sums(sq):
    c = sq.shape[1]
    li = lax.broadcasted_iota(jnp.int32, (c, c), 0)
    lj = lax.broadcasted_iota(jnp.int32, (c, c), 1)
    sh = HEAD_DIM.bit_length() - 1
    ones = _mx(((li >> sh) == (lj >> sh)).astype(F32))
    hi = _mx(sq)
    lo = _mx(sq - hi.astype(F32))
    return jnp.dot(hi, ones, preferred_element_type=F32) + jnp.dot(lo, ones, preferred_element_type=F32)


def _l2norm_heads(x):
    return x * lax.rsqrt(_head_sums(x * x) + EPS)


def _dot_tri(mask, x):
    m = _mx(mask.astype(F32))
    x1 = _mx(x)
    r1 = x - x1.astype(F32)
    x2 = _mx(r1)
    x3 = _mx(r1 - x2.astype(F32))
    return (jnp.dot(m, x1, preferred_element_type=F32) + jnp.dot(m, x2, preferred_element_type=F32)
            + jnp.dot(m, x3, preferred_element_type=F32))


def _same_block(rowi, coli, n):
    sh = n.bit_length() - 1
    return (rowi >> sh) == (coli >> sh)


def _solve_unit_tri(a_list, rhs_list, rowi, coli, chunk):
    mm = lambda x, y: jnp.dot(x, y, preferred_element_type=F32)
    eye = (rowi == coli).astype(F32)
    in_base = _same_block(rowi, coli, GDN_BASE)
    base = [_mx(jnp.where(in_base, a, 0.0)) for a in a_list]
    ts = [jnp.where(in_base, eye - a, 0.0) for a in a_list]
    ps = [_mx(mm(b, b)) for b in base]
    ts = [t + mm(_mx(t), p) for t, p in zip(ts, ps)]
    n = 4
    while n < GDN_BASE:
        ps = [_mx(mm(p, p)) for p in ps]
        ts = [t + mm(_mx(t), p) for t, p in zip(ts, ps)]
        n *= 2
    n = GDN_BASE
    while 2 * n < chunk:
        inner = _same_block(rowi, coli, 2 * n) & jnp.logical_not(_same_block(rowi, coli, n))
        offs = [_mx(jnp.where(inner, a, 0.0)) for a in a_list]
        tb = [_mx(t) for t in ts]
        ms = [_mx(mm(t, off)) for t, off in zip(tb, offs)]
        ts = [t - mm(m, t_b) for t, m, t_b in zip(ts, ms, tb)]
        n *= 2
    outer = jnp.logical_not(_same_block(rowi, coli, n))
    offs = [_mx(jnp.where(outer, a, 0.0)) for a in a_list]
    tb = [_mx(t) for t in ts]
    ys = [mm(t, _mx(r)) for t, r in zip(tb, rhs_list)]
    zs = [_mx(mm(off, _mx(y))) for off, y in zip(offs, ys)]
    return [y - mm(t, z) for y, t, z in zip(ys, tb, zs)]


def _gdn_kernel(xf_ref, xb_ref, sf_ref, sb_ref, s0_ref, alog_ref, dtb_ref, of_ref, ob_ref, sfin_ref, state_s):
    i = pl.program_id(1)
    tq = xf_ref.shape[0]
    ck = min(GDN_CHUNK, tq)
    nck = tq // ck

    @pl.when(i == 0)
    def _():
        state_s[...] = s0_ref[0]

    sub = min(GDN_SUB, tq)
    nsub = tq // sub
    rowt, colt = _tri_masks(tq)
    in_chunk_t = _same_block(rowt, colt, ck)
    rowi, coli = _tri_masks(sub)
    in_chunk = _same_block(rowi, coli, ck)
    a_neg = jnp.where(_lane_mask(SM_DECAY), -jnp.exp(alog_ref[...]), 0.0)

    a_list, rhs_list, qkm, qg, kd, e_last = [], [], [], [], [], []
    for d, (x_ref, sm_ref) in enumerate(((xf_ref, sf_ref), (xb_ref, sb_ref))):
        qkv = x_ref[...]
        qn, kn, v = qkv[:, :GROUP_W], qkv[:, GROUP_W:2 * GROUP_W], qkv[:, 2 * GROUP_W:]
        sm = sm_ref[...]
        beta = _sigmoid(sm)
        keep_t = in_chunk_t & ((rowt >= colt) if d == 0 else (rowt <= colt))
        keep = in_chunk & ((rowi >= coli) if d == 0 else (rowi <= coli))
        strict = in_chunk & ((rowi > coli) if d == 0 else (rowi < coli))
        gc = _dot_tri(keep_t, _softplus(sm + dtb_ref[...]) * a_neg)
        gc_t = gc.T
        edge = ck - 1 if d == 0 else 0
        last = jnp.concatenate([jnp.broadcast_to(gc[c * ck + edge:c * ck + edge + 1, :], (ck, LANES))
                                for c in range(nck)], axis=0)
        e_gc = jnp.exp(gc)
        e_end = jnp.exp(last - gc)
        e_last.append(jnp.exp(last))
        for h in range(N_HEADS):
            sl = slice(h * HEAD_DIM, (h + 1) * HEAD_DIM)
            lg = SM_DECAY + N_HEADS * d + h
            lb = SM_BETA + N_HEADS * d + h
            qh, kh, bcol = qn[:, sl], kn[:, sl], beta[:, lb:lb + 1]
            kb = kh * bcol
            rhs = jnp.concatenate([v[:, sl] * bcol, kb * e_gc[:, lg:lg + 1]], axis=1)
            qg.append(qh * e_gc[:, lg:lg + 1])
            kd.append(kh * e_end[:, lg:lg + 1])
            qh_m, kh_m, kb_m = _mx(qh), _mx(kh), _mx(kb)
            for s in range(nsub):
                rs = slice(s * sub, (s + 1) * sub)
                decay = jnp.exp(jnp.where(keep, gc[rs, lg:lg + 1] - gc_t[lg:lg + 1, rs], NEG_INF))
                a_list.append(jnp.where(strict, _dot_nt(kb_m[rs], kh_m[rs]) * decay, 0.0))
                rhs_list.append(rhs[rs])
                qkm.append(_dot_nt(qh_m[rs], kh_m[rs]) * decay)
    sols = _solve_unit_tri(a_list, rhs_list, rowi, coli, ck)
    sols = [jnp.concatenate(sols[n * nsub:(n + 1) * nsub], axis=0) for n in range(2 * N_HEADS)]

    chains = [(d, h) for d in range(2) for h in range(N_HEADS)]
    states = [state_s[d, h] for d, h in chains]
    v_new = [[None] * nck for _ in chains]
    o_st = [[None] * nck for _ in chains]
    for step in range(nck):
        rows = [slice((step if d == 0 else nck - 1 - step) * ck, (step if d == 0 else nck - 1 - step) * ck + ck)
                for d, _ in chains]
        ms = [_dot(jnp.concatenate([sols[n][r, HEAD_DIM:], qg[n][r]], axis=0), states[n])
              for n, r in enumerate(rows)]
        for n, (d, _) in enumerate(chains):
            c = step if d == 0 else nck - 1 - step
            v_new[n][c] = sols[n][rows[n], :HEAD_DIM] - ms[n][:ck]
            o_st[n][c] = ms[n][ck:]
        ups = [_dot_tn(kd[n][r], v_new[n][step if chains[n][0] == 0 else nck - 1 - step])
               for n, r in enumerate(rows)]
        for n, (d, h) in enumerate(chains):
            lg = SM_DECAY + N_HEADS * d + h
            states[n] = states[n] * e_last[d][rows[n].start:rows[n].start + 1, lg:lg + 1] + ups[n]
    cps = sub // ck
    outs = [jnp.concatenate(o_st[n], axis=0)
            + jnp.concatenate([_dot(qkm[n * nsub + s], jnp.concatenate(v_new[n][s * cps:(s + 1) * cps], axis=0))
                               for s in range(nsub)], axis=0)
            for n in range(len(chains))]
    of_ref[...] = jnp.concatenate(outs[:N_HEADS], axis=1)
    ob_ref[...] = jnp.concatenate(outs[N_HEADS:], axis=1)
    for n, (d, h) in enumerate(chains):
        state_s[d, h] = states[n]

    @pl.when(i == pl.num_programs(1) - 1)
    def _():
        sfin_ref[0] = state_s[...]


def gdn_params(a_log, dt_bias):
    return _small_vec(a_log, SM_DECAY), _small_vec(dt_bias, SM_DECAY)


def gdn_mixer(qkv, sm, s0, bsz, alog, dtb):
    t, c = qkv.shape
    s = t // bsz
    q = min(GDN_TILE, s)
    nc = s // q
    xf, xb = _chunk_specs(nc, q, c)
    sf, sb = _chunk_specs(nc, q, LANES)
    of, ob = _chunk_specs(nc, q, GROUP_W)
    st = pl.BlockSpec((1,) + s0.shape[1:], lambda b, i: (b, 0, 0, 0, 0))
    o_shape = jax.ShapeDtypeStruct((t, GROUP_W), F32)
    return pl.pallas_call(
        _gdn_kernel,
        grid=(bsz, nc),
        in_specs=[xf, xb, sf, sb, st, _full(alog.shape), _full(dtb.shape)],
        out_specs=[of, ob, st],
        out_shape=[o_shape, o_shape, jax.ShapeDtypeStruct(s0.shape, F32)],
        scratch_shapes=[pltpu.VMEM(s0.shape[1:], F32)],
        compiler_params=_cp("parallel", "arbitrary"),
        name="gdn_mixer",
    )(qkv, qkv, sm, sm, s0, alog, dtb)


def _split_hi_lo(a):
    hi = _mx(a)
    return hi, _mx(a - hi.astype(F32))


def _outproj_kernel(x_ref, ahf_ref, ahb_ref, ag_ref, bo_ref, cyf_ref, cyb_ref, cxc_ref, cz_ref,
                    dof_ref, dob_ref, dz_ref, wout_ref, gpost_ref, ga1_ref, gpre_ref, sc2_ref, sh2_ref,
                    dskip_ref, cnorm_ref, dnorm_ref, rhi_ref, rlo_ref, xo_ref, hp_ref, lg_ref):
    m_a = (ahf_ref[...] + ahb_ref[...]) * _gelu_tanh(ag_ref[...])
    y_c = (cyf_ref[...] + cyb_ref[...] + cxc_ref[...] * dskip_ref[...]) * _silu(cz_ref[...])
    m_c = _rms(y_c, cnorm_ref[...])
    o_d = dof_ref[...] + dob_ref[...]
    m_d = o_d * lax.rsqrt(_head_sums(o_d * o_d) * (1.0 / HEAD_DIM) + EPS) * dnorm_ref[...] * _silu(dz_ref[...])
    mix = jnp.concatenate([_mx(m_a), _mx(bo_ref[...]), _mx(m_c), _mx(m_d)], axis=1)
    ml = jnp.dot(mix, wout_ref[...], preferred_element_type=F32)
    x_new = x_ref[...] + ga1_ref[0] * _rms(ml, gpost_ref[...])
    xo_ref[...] = x_new
    h2 = _rms(x_new, gpre_ref[...]) * (1.0 + sc2_ref[0]) + sh2_ref[0]
    hi, lo = _split_hi_lo(h2)
    hp_ref[...] = _pack_pairs(h2)
    rhi = rhi_ref[...]
    lg_ref[...] = (jnp.dot(hi, rhi, preferred_element_type=F32) + jnp.dot(lo, rhi, preferred_element_type=F32)
                   + jnp.dot(hi, rlo_ref[...], preferred_element_type=F32))


def out_projection(x, mixers, w_out, gpost, ga1, gpre, sc2, sh2, dskip, cnorm, dnorm, router_w, tiles_per_group):
    t, d = x.shape
    tm = min(TOKEN_TILE, t)
    vec = lambda i: (i // tiles_per_group, 0, 0)
    row = lambda w: pl.BlockSpec((tm, w), lambda i: (i, 0))
    ne = LANES
    rhi, rlo = _split_hi_lo(jnp.pad(router_w.astype(F32), ((0, 0), (0, ne - router_w.shape[1]))))
    return pl.pallas_call(
        _outproj_kernel,
        grid=(t // tm,),
        in_specs=[row(d)] + [row(GROUP_W)] * 11
                 + [_full(w_out.shape), _full((1, d)), pl.BlockSpec((1, 1, d), vec), _full((1, d)),
                    pl.BlockSpec((1, 1, d), vec), pl.BlockSpec((1, 1, d), vec),
                    _full((1, GROUP_W)), _full((1, GROUP_W)), _full((1, GROUP_W)), _full(rhi.shape), _full(rlo.shape)],
        out_specs=[row(d), row(d // 2), row(ne)],
        out_shape=[jax.ShapeDtypeStruct((t, d), F32), jax.ShapeDtypeStruct((t, d // 2), jnp.uint32),
                   jax.ShapeDtypeStruct((t, ne), F32)],
        compiler_params=_cp("parallel"),
        name="out_projection",
    )(x, *mixers, w_out, gpost, ga1, gpre, sc2, sh2, dskip, cnorm, dnorm, rhi, rlo)


def _rank_before(vals, idx, count, stride):
    rank = jnp.zeros(vals.shape, jnp.int32)
    for j in range(count):
        other = vals[j * stride:j * stride + 1, :]
        ahead = (other > vals) | ((other == vals) & (idx > j))
        rank = rank + ahead.astype(jnp.int32)
    return rank


def _xor_partner(x, row, s):
    n = x.shape[0]
    return jnp.where((row & s) == 0, pltpu.roll(x, n - s, 0), pltpu.roll(x, s, 0))


def _route(logits, router_b):
    ne = N_EXPERTS
    gsz = ne // N_EXPERT_GROUPS
    scores = _sigmoid(logits.T[:ne, :])
    tm = scores.shape[1]
    biased = scores + router_b
    row = lax.broadcasted_iota(jnp.int32, (ne, tm), 0)
    m1, m2 = biased, jnp.full((ne, tm), -jnp.inf, F32)
    s = 1
    while s < gsz:
        o1, o2 = _xor_partner(m1, row, s), _xor_partner(m2, row, s)
        m2 = jnp.maximum(jnp.minimum(m1, o1), jnp.maximum(m2, o2))
        m1 = jnp.maximum(m1, o1)
        s *= 2
    gidx = row >> (gsz.bit_length() - 1)
    group_ok = _rank_before(m1 + m2, gidx, N_EXPERT_GROUPS, gsz) < TOPK_GROUPS
    choice = jnp.where(group_ok, biased, -jnp.inf)
    rank = _rank_before(choice, row, ne, 1)
    gate = jnp.where(rank < TOP_K, scores, 0.0)
    gate = gate / jnp.sum(gate, axis=0, keepdims=True) * ROUTED_SCALE
    return gate, rank, row


def _to_token_major(x):
    n, tm = x.shape
    return jnp.concatenate([x, jnp.zeros((LANES - n, tm), x.dtype)], axis=0).T


def _router_kernel(lg_ref, rb_ref, gate_ref):
    gate, _, _ = _route(lg_ref[...], rb_ref[...])
    gate_ref[...] = _to_token_major(gate)


def _router_dispatch_kernel(lg_ref, rb_ref, gk_ref, ek_ref, pk_ref, cnt_ref, carry_s):
    i = pl.program_id(0)

    @pl.when(i == 0)
    def _():
        carry_s[...] = jnp.zeros(carry_s.shape, F32)

    gate, rank, row = _route(lg_ref[...], rb_ref[...])
    tm = gate.shape[1]
    picked = (rank < TOP_K).astype(F32)
    before = lax.broadcasted_iota(jnp.int32, (tm, tm), 0) < lax.broadcasted_iota(jnp.int32, (tm, tm), 1)
    pos = _dot(picked, before.astype(F32)) + carry_s[:, 0:1]
    carry_s[...] = carry_s[...] + jnp.sum(picked, axis=1, keepdims=True)
    gk, ek, pk = [], [], []
    for k in range(TOP_K):
        sel = rank == k
        gk.append(jnp.sum(jnp.where(sel, gate, 0.0), axis=0, keepdims=True))
        ek.append(jnp.sum(jnp.where(sel, row, 0), axis=0, keepdims=True))
        pk.append(jnp.sum(jnp.where(sel, pos, 0.0), axis=0, keepdims=True))
    gk_ref[...] = _to_token_major(jnp.concatenate(gk, axis=0))
    ek_ref[...] = jnp.concatenate(ek, axis=0)
    pk_ref[...] = jnp.concatenate(pk, axis=0).astype(jnp.int32)

    @pl.when(i == pl.num_programs(0) - 1)
    def _():
        cnt_ref[...] = carry_s[...].astype(jnp.int32)


def router_dispatch(logits, router_b):
    t, w = logits.shape
    tm = min(TOKEN_TILE, t)
    return pl.pallas_call(
        _router_dispatch_kernel,
        grid=(t // tm,),
        in_specs=[pl.BlockSpec((tm, w), lambda i: (i, 0)), _full((N_EXPERTS, 1))],
        out_specs=[pl.BlockSpec((tm, w), lambda i: (i, 0)),
                   pl.BlockSpec((TOP_K, tm), lambda i: (0, i)),
                   pl.BlockSpec((TOP_K, tm), lambda i: (0, i)),
                   _full((N_EXPERTS, LANES))],
        out_shape=[jax.ShapeDtypeStruct((t, w), F32), jax.ShapeDtypeStruct((TOP_K, t), jnp.int32),
                   jax.ShapeDtypeStruct((TOP_K, t), jnp.int32), jax.ShapeDtypeStruct((N_EXPERTS, LANES), jnp.int32)],
        scratch_shapes=[pltpu.VMEM((N_EXPERTS, LANES), F32)],
        compiler_params=_cp("arbitrary"),
        name="router_dispatch",
    )(logits, router_b.reshape(N_EXPERTS, 1).astype(F32))


def router_gates(logits, router_b):
    t, w = logits.shape
    tm = min(TOKEN_TILE, t)
    return pl.pallas_call(
        _router_kernel,
        grid=(t // tm,),
        in_specs=[pl.BlockSpec((tm, w), lambda i: (i, 0)), _full((N_EXPERTS, 1))],
        out_specs=pl.BlockSpec((tm, w), lambda i: (i, 0)),
        out_shape=jax.ShapeDtypeStruct((t, w), F32),
        compiler_params=_cp("parallel"),
        name="router_gates",
    )(logits, router_b.reshape(N_EXPERTS, 1).astype(F32))


def _moe_kernel(h_ref, gate_ref, x_ref, wg_ref, wu_ref, wd_ref, sg_ref, su_ref, sd_ref, gpost_ref, ga2_ref,
                o_ref, acc_s):
    e = pl.program_id(1)
    h = _mx(_unpack_pairs(h_ref[...]))

    @pl.when(e == 0)
    def _():
        hs = _silu(jnp.dot(h, sg_ref[...], preferred_element_type=F32)) * jnp.dot(h, su_ref[...], preferred_element_type=F32)
        acc_s[...] = jnp.dot(_mx(hs), sd_ref[...], preferred_element_type=F32)

    gates = gate_ref[...]
    lane = lax.broadcasted_iota(jnp.int32, gates.shape, 1)
    hid = []
    for j in range(MOE_EB):
        gcol = jnp.sum(jnp.where(lane == e * MOE_EB + j, gates, 0.0), axis=1, keepdims=True)
        g = jnp.dot(h, _mx(wg_ref[j]), preferred_element_type=F32)
        u = jnp.dot(h, _mx(wu_ref[j]), preferred_element_type=F32)
        hid.append(_mx(_silu(g) * u * gcol))
    wd = _mx(wd_ref[...]).reshape(MOE_EB * D_EXPERT, -1)
    acc_s[...] += jnp.dot(jnp.concatenate(hid, axis=1), wd, preferred_element_type=F32)

    @pl.when(e == pl.num_programs(1) - 1)
    def _():
        o_ref[...] = x_ref[...] + ga2_ref[0] * _rms(acc_s[...], gpost_ref[...])


def moe_ffn(h, gates, x, layer, wg, wu, wd, sg, su, sd, gpost, ga2, tiles_per_group):
    t, d = x.shape
    tm = min(MOE_TILE, t)
    _, ne, _, f = wg.shape
    row = lambda w: pl.BlockSpec((tm, w), lambda i, e: (i, 0))
    return pl.pallas_call(
        _moe_kernel,
        grid=(t // tm, ne // MOE_EB),
        in_specs=[row(h.shape[1]), row(gates.shape[1]), row(d),
                  pl.BlockSpec((None, MOE_EB, d, f), lambda i, e: (layer, e, 0, 0)),
                  pl.BlockSpec((None, MOE_EB, d, f), lambda i, e: (layer, e, 0, 0)),
                  pl.BlockSpec((None, MOE_EB, f, d), lambda i, e: (layer, e, 0, 0)),
                  _full(sg.shape), _full(su.shape), _full(sd.shape), _full((1, d)),
                  pl.BlockSpec((1, 1, d), lambda i, e: (i // tiles_per_group, 0, 0))],
        out_specs=row(d),
        out_shape=jax.ShapeDtypeStruct((t, d), F32),
        scratch_shapes=[pltpu.VMEM((tm, d), F32)],
        compiler_params=_cp("parallel", "arbitrary"),
        name="moe_ffn",
    )(h, gates, x, wg, wu, wd, sg, su, sd, gpost, ga2)


def moe_plan(counts, n_tokens):
    n_blocks = (n_tokens * TOP_K + N_EXPERTS * (MOE_BLOCK - 1) + MOE_BLOCK - 1) // MOE_BLOCK
    cnt = counts[:, 0]
    padded = (cnt + MOE_BLOCK - 1) // MOE_BLOCK * MOE_BLOCK
    pad_end = jnp.cumsum(padded)
    off = pad_end - padded
    start = jnp.arange(n_blocks, dtype=jnp.int32) * MOE_BLOCK
    be = jnp.minimum(jnp.sum(pad_end[None, :] <= start[:, None], axis=1), N_EXPERTS - 1).astype(jnp.int32)
    mine = be[:, None] == jnp.arange(N_EXPERTS, dtype=jnp.int32)[None, :]
    end = jnp.sum(jnp.where(mine, (off + cnt)[None, :], 0), axis=1)
    nv = jnp.clip(end - start, 0, MOE_BLOCK).astype(jnp.int32)
    return off.astype(jnp.int32), be, nv


def _rows_kernel(off_ref, ek_ref, pk_ref, dest_ref):
    ek = ek_ref[...]
    dest = pk_ref[...]
    for e in range(N_EXPERTS):
        dest = dest + jnp.where(ek == e, off_ref[e], 0)
    dest_ref[...] = dest


def moe_rows(off, ek, pk):
    k, t = ek.shape
    tm = min(MOE_PLAN_TILE, t)
    spec = pl.BlockSpec((k, tm), lambda i, off: (0, i))
    return pl.pallas_call(
        _rows_kernel,
        grid_spec=pltpu.PrefetchScalarGridSpec(num_scalar_prefetch=1, grid=(t // tm,),
                                               in_specs=[spec, spec], out_specs=spec),
        out_shape=jax.ShapeDtypeStruct((k, t), jnp.int32),
        compiler_params=_cp("arbitrary"),
        name="moe_rows",
    )(off, ek, pk)


U32 = jnp.uint32
HIGH_HALF = 0xFFFF0000


def _pack_pairs(x):
    w = x.shape[1] // 2
    bits = lax.bitcast_convert_type(x.astype(jnp.bfloat16).astype(F32), U32)
    return (bits[:, w:] & jnp.uint32(HIGH_HALF)) | (bits[:, :w] >> 16)


def _unpack_pairs(p):
    lo = lax.bitcast_convert_type(p << 16, F32)
    hi = lax.bitcast_convert_type(p & jnp.uint32(HIGH_HALF), F32)
    return jnp.concatenate([lo, hi], axis=1)


def _sc_workers():
    info = plsc.get_sparse_core_info()
    return info.num_cores, info.num_cores * info.num_subcores


def sc_scatter_rows(src, idx, n_rows):
    k, t = idx.shape
    w = src.shape[1]
    n_cores, n_workers = _sc_workers()
    per_worker = t // n_workers
    mesh = plsc.VectorSubcoreMesh(core_axis_name="c", subcore_axis_name="s")

    @functools.partial(
        pl.kernel, mesh=mesh, out_type=jax.ShapeDtypeStruct((n_rows, w), src.dtype),
        scratch_types=[pltpu.VMEM((k, SC_WINDOW), jnp.int32), pltpu.VMEM((SC_WINDOW, w), src.dtype),
                       pltpu.SemaphoreType.DMA])
    def scatter(s_hbm, i_hbm, o_hbm, idx_v, rows_v, sem):
        base = (lax.axis_index("s") * n_cores + lax.axis_index("c")) * per_worker

        @pl.loop(0, per_worker // SC_WINDOW)
        def _(j):
            off = base + j * SC_WINDOW
            pltpu.sync_copy(i_hbm.at[:, pl.ds(off, SC_WINDOW)], idx_v)
            pltpu.sync_copy(s_hbm.at[pl.ds(off, SC_WINDOW)], rows_v)
            for kk in range(k):
                pltpu.async_copy(rows_v, o_hbm.at[idx_v.at[kk]], sem).wait()

    return scatter(src, idx)


def sc_gather_rows(table, idx):
    n = idx.shape[0]
    w = table.shape[1]
    n_cores, n_workers = _sc_workers()
    per_worker = n // n_workers
    mesh = plsc.VectorSubcoreMesh(core_axis_name="c", subcore_axis_name="s")

    @functools.partial(
        pl.kernel, mesh=mesh, out_type=jax.ShapeDtypeStruct((n, w), table.dtype),
        scratch_types=[pltpu.VMEM((SC_WINDOW,), jnp.int32), pltpu.VMEM((SC_WINDOW, w), table.dtype),
                       pltpu.SemaphoreType.DMA])
    def gather(t_hbm, i_hbm, o_hbm, idx_v, rows_v, sem):
        base = (lax.axis_index("s") * n_cores + lax.axis_index("c")) * per_worker

        @pl.loop(0, per_worker // SC_WINDOW)
        def _(j):
            off = base + j * SC_WINDOW
            pltpu.sync_copy(i_hbm.at[pl.ds(off, SC_WINDOW)], idx_v)
            pltpu.async_copy(t_hbm.at[idx_v], rows_v, sem).wait()
            pltpu.sync_copy(rows_v, o_hbm.at[pl.ds(off, SC_WINDOW)])

    return gather(table, idx)


def _expert_kernel(be_ref, nv_ref, xs_ref, wg_ref, wu_ref, wd_ref, ys_ref):
    nv = nv_ref[pl.program_id(0)]

    @pl.when(nv > 0)
    def _():
        x = _unpack_pairs(xs_ref[...])
        rows = lax.broadcasted_iota(jnp.int32, x.shape, 0)
        x = _mx(jnp.where(rows < nv, x, 0.0))
        hid = (_silu(jnp.dot(x, _mx(wg_ref[0]), preferred_element_type=F32))
               * jnp.dot(x, _mx(wu_ref[0]), preferred_element_type=F32))
        ys_ref[...] = _pack_pairs(jnp.dot(_mx(hid), _mx(wd_ref[0]), preferred_element_type=F32))

    @pl.when(nv == 0)
    def _():
        ys_ref[...] = jnp.zeros(ys_ref.shape, U32)


def moe_experts(xs, be, nv, layer, wg, wu, wd):
    n_rows, w = xs.shape
    _, _, d, f = wg.shape
    return pl.pallas_call(
        _expert_kernel,
        grid_spec=pltpu.PrefetchScalarGridSpec(
            num_scalar_prefetch=2,
            grid=(n_rows // MOE_BLOCK,),
            in_specs=[pl.BlockSpec((MOE_BLOCK, w), lambda b, be, nv: (b, 0)),
                      pl.BlockSpec((None, 1, d, f), lambda b, be, nv: (layer, be[b], 0, 0)),
                      pl.BlockSpec((None, 1, d, f), lambda b, be, nv: (layer, be[b], 0, 0)),
                      pl.BlockSpec((None, 1, f, d), lambda b, be, nv: (layer, be[b], 0, 0))],
            out_specs=pl.BlockSpec((MOE_BLOCK, w), lambda b, be, nv: (b, 0))),
        out_shape=jax.ShapeDtypeStruct((n_rows, w), U32),
        compiler_params=_cp("arbitrary"),
        name="moe_experts",
    )(be, nv, xs, wg, wu, wd)


def _combine_kernel(yg_ref, gk_ref, hp_ref, x_ref, sg_ref, su_ref, sd_ref, gpost_ref, ga2_ref, o_ref):
    h = _mx(_unpack_pairs(hp_ref[...]))
    hs = _silu(jnp.dot(h, sg_ref[...], preferred_element_type=F32)) * jnp.dot(h, su_ref[...], preferred_element_type=F32)
    f = jnp.dot(_mx(hs), sd_ref[...], preferred_element_type=F32)
    gk = gk_ref[...]
    for k in range(TOP_K):
        f = f + gk[:, k:k + 1] * _unpack_pairs(yg_ref[k])
    o_ref[...] = x_ref[...] + ga2_ref[0] * _rms(f, gpost_ref[...])


def moe_combine(yg, gk, hp, x, sg, su, sd, gpost, ga2, tiles_per_group):
    t, d = x.shape
    tm = min(MOE_ROW_TILE, t)
    w = hp.shape[1]
    row = lambda n: pl.BlockSpec((tm, n), lambda i: (i, 0))
    return pl.pallas_call(
        _combine_kernel,
        grid=(t // tm,),
        in_specs=[pl.BlockSpec((TOP_K, tm, w), lambda i: (0, i, 0)),
                  row(gk.shape[1]), row(w), row(d), _full(sg.shape), _full(su.shape), _full(sd.shape), _full((1, d)),
                  pl.BlockSpec((1, 1, d), lambda i: (i // tiles_per_group, 0, 0))],
        out_specs=row(d),
        out_shape=jax.ShapeDtypeStruct((t, d), F32),
        compiler_params=_cp("parallel"),
        name="moe_combine",
    )(yg, gk, hp, x, sg, su, sd, gpost, ga2)


def _reorder_w_in(w_in):
    c = np.cumsum((0,) + (GROUP_W, GROUP_W, GROUP_W, GROUP_W, GROUP_W, GROUP_W, 2 * SSD_STATE, 2 * SSD_STATE,
                          GROUP_W, 2 * N_HEADS, GROUP_W, GROUP_W, GROUP_W, GROUP_W, 2 * N_HEADS, 2 * N_HEADS))
    seg = lambda a, b: w_in[:, c[a]:c[b]]
    small = jnp.concatenate([seg(9, 10), seg(14, 15), seg(15, 16),
                             jnp.zeros((w_in.shape[0], LANES - 6 * N_HEADS), w_in.dtype)], axis=1)
    return jnp.concatenate([seg(0, 1), seg(5, 8), seg(10, 13), seg(1, 5), seg(8, 9), seg(13, 14), small], axis=1)


def kernel(x, c, ctx, c_ctx, w_mod, b_mod, g_pre_mix, g_post_mix, g_pre_ffn, g_post_ffn, w_in, w_out, lru_conv_w, lru_conv_b, lru_wa, lru_ba, lru_wx, lru_bx, lru_lambda, na_bias, ssd_conv_w, ssd_conv_b, ssd_a_log, ssd_dt_bias, ssd_d, ssd_norm, gdn_conv_w, gdn_a_log, gdn_dt_bias, gdn_norm, router_w, router_b, we_gate, we_up, we_down, ws_gate, ws_up, ws_down):
    bsz, seq, d = x.shape
    n_ctx = ctx.shape[1]
    depth = w_mod.shape[0]
    lat_tpg = seq // min(TOKEN_TILE, seq)
    ctx_tpg = max(bsz * n_ctx // TOKEN_TILE, 1)
    ctx_mpg = max(bsz * n_ctx // MOE_TILE, 1)

    cond = _pad_rows(jnp.concatenate([c, c_ctx[None, :]], axis=0))
    mod = modulation(cond, w_mod, b_mod).reshape(depth, SUBLANES, N_MOD, d)
    rope = rope_tables(seq)
    row = lambda v: v[None, :].astype(F32)

    xl = x.reshape(bsz * seq, d)
    xc = ctx.reshape(bsz * n_ctx, d)
    for l in range(depth):
        last = l == depth - 1
        m_lat = [mod[l, :bsz, k][:, None, :] for k in range(N_MOD)]
        m_ctx = [mod[l, bsz:bsz + 1, k][:, None, :] for k in range(N_MOD)]
        w_in_l = _reorder_w_in(w_in[l]).astype(MXU_DTYPE)
        conv = (_pad_rows(lru_conv_w[l]), row(lru_conv_b[l]), _pad_rows(ssd_conv_w[l]), row(ssd_conv_b[l]),
                _pad_rows(gdn_conv_w[l]))
        pc = in_projection(xc, row(g_pre_mix[l]), m_ctx[1], m_ctx[0], w_in_l, conv, None, n_ctx,
                           bsz * n_ctx // min(TOKEN_TILE, n_ctx))
        pl_ = in_projection(xl, row(g_pre_mix[l]), m_lat[1], m_lat[0], w_in_l, conv, rope, seq, lat_tpg)

        lru_p = lru_params(lru_wa[l], lru_ba[l], lru_wx[l], lru_bx[l], lru_lambda[l])
        a_cf, a_cb, a_st = lru_mixer(pc[P_AX], jnp.zeros((bsz, SUBLANES, GROUP_W), F32), bsz, *lru_p)
        a_lf, a_lb, _ = lru_mixer(pl_[P_AX], a_st, bsz, *lru_p)

        kc = pc[P_BK].reshape(bsz, n_ctx, GROUP_W)
        vc = pc[P_BV].reshape(bsz, n_ctx, GROUP_W)
        b_c = ctx_attention(pc[P_BQ].reshape(bsz, n_ctx, GROUP_W), kc, vc).reshape(bsz * n_ctx, GROUP_W)
        b_l = na_mixer(pl_[P_BQ], pl_[P_BK], pl_[P_BV], kc, vc, na_bias_slabs(na_bias[l]), bsz)

        ssd_p = ssd_params(ssd_a_log[l], ssd_dt_bias[l])
        c_cf, c_cb, c_st = ssd_mixer(pc[P_CX], pc[P_SM], jnp.zeros((bsz, 2, N_HEADS, SSD_STATE, HEAD_DIM), F32),
                                     bsz, *ssd_p)
        c_lf, c_lb, _ = ssd_mixer(pl_[P_CX], pl_[P_SM], c_st, bsz, *ssd_p)

        gdn_p = gdn_params(gdn_a_log[l], gdn_dt_bias[l])
        d_cf, d_cb, d_st = gdn_mixer(pc[P_DX], pc[P_SM], jnp.zeros((bsz, 2, N_HEADS, HEAD_DIM, HEAD_DIM), F32),
                                     bsz, *gdn_p)
        d_lf, d_lb, _ = gdn_mixer(pl_[P_DX], pl_[P_SM], d_st, bsz, *gdn_p)

        epi = (w_out[l].astype(MXU_DTYPE), row(g_post_mix[l]))
        epi_tail = (row(jnp.repeat(ssd_d[l], HEAD_DIM)), row(ssd_norm[l]), row(jnp.tile(gdn_norm[l], N_HEADS)), router_w[l])
        routed_w = (l, we_gate, we_up, we_down)
        shared_w = (ws_gate[l].astype(MXU_DTYPE), ws_up[l].astype(MXU_DTYPE), ws_down[l].astype(MXU_DTYPE),
                    row(g_post_ffn[l]))

        mix_l = (a_lf, a_lb, pl_[P_AG], b_l, c_lf, c_lb, pl_[P_CX], pl_[P_CZ], d_lf, d_lb, pl_[P_DZ])
        xl, hp, lg = out_projection(xl, mix_l, *epi, m_lat[2], row(g_pre_ffn[l]), m_lat[4], m_lat[3], *epi_tail, lat_tpg)
        gk, ek, pk, cnt = router_dispatch(lg, router_b[l])
        off, be, nv = moe_plan(cnt, bsz * seq)
        dest = moe_rows(off, ek, pk)
        xs = sc_scatter_rows(hp, dest, be.shape[0] * MOE_BLOCK)
        ys = moe_experts(xs, be, nv, *routed_w)
        yg = sc_gather_rows(ys, dest.reshape(-1)).reshape(TOP_K, bsz * seq, d // 2)
        xl = moe_combine(yg, gk, hp, xl, *shared_w, m_lat[5], seq // min(MOE_ROW_TILE, seq))
        if not last:
            mix_c = (a_cf, a_cb, pc[P_AG], b_c, c_cf, c_cb, pc[P_CX], pc[P_CZ], d_cf, d_cb, pc[P_DZ])
            xc, hp, lg = out_projection(xc, mix_c, *epi, m_ctx[2], row(g_pre_ffn[l]), m_ctx[4], m_ctx[3], *epi_tail, ctx_tpg)
            xc = moe_ffn(hp, router_gates(lg, router_b[l]), xc, *routed_w, *shared_w, m_ctx[5], ctx_mpg)
    return xl.reshape(bsz, seq, d)
```

```python
import functools
import math

import jax
import jax.numpy as jnp
import numpy as np
from jax import lax
from jax.experimental import pallas as pl
from jax.experimental.pallas import tpu as pltpu
from jax.experimental.pallas import tpu_sc as plsc

F32 = jnp.float32
MXU_DTYPE = jnp.bfloat16
HI = lax.Precision.HIGHEST

D_MODEL = 1024
GRID_W = 64
GROUP_W = 256
HEAD_DIM = 64
N_HEADS = 4
EPS = 1e-6
NEG_INF = -1e30
N_MOD = 6
LRU_C = 8.0
NA_WIN_ROWS = 8
NA_WIN_COLS = 16
SSD_STATE = 128
SSD_GROUPS = 2
ROPE_BASE = 10000.0
ROPE_AXIS_DIM = HEAD_DIM // 2
N_EXPERTS = 64
N_EXPERT_GROUPS = 8
TOPK_GROUPS = 4
TOP_K = 8
D_EXPERT = 256
ROUTED_SCALE = 2.5

LANES = 128
SUBLANES = 8
VMEM_LIMIT = 56 * 1024 * 1024

TOKEN_TILE = 512
LRU_CHUNK = 256
SSD_CHUNK = 128
GDN_CHUNK = 64
GDN_TILE = 256
GDN_SUB = 128
GDN_BASE = 16
MOE_TILE = 1024
MOE_EB = 4
MOE_BLOCK = 1024
MOE_ROW_TILE = 256
MOE_PLAN_TILE = 2048
SC_WINDOW = 128

P_WIDTHS = (256, 768, 768, 256, 256, 256, 256, 256, 256, 128)
(P_AX, P_CX, P_DX, P_AG, P_BQ, P_BK, P_BV, P_CZ, P_DZ, P_SM) = range(10)
P_CONV_GROUPS = 3
SM_DT, SM_BETA, SM_DECAY = 0, 8, 16


def _cp(*sem):
    return pltpu.CompilerParams(dimension_semantics=sem, vmem_limit_bytes=VMEM_LIMIT)


def _mx(x):
    return x.astype(MXU_DTYPE)


def _dot(a, b):
    return jnp.dot(_mx(a), _mx(b), preferred_element_type=F32)


def _dot_nt(a, b):
    return lax.dot_general(_mx(a), _mx(b), (((1,), (1,)), ((), ())), preferred_element_type=F32)


def _dot_tn(a, b):
    return lax.dot_general(_mx(a), _mx(b), (((0,), (0,)), ((), ())), preferred_element_type=F32)


def _dot_hi(a, b):
    return jnp.dot(a, b, preferred_element_type=F32, precision=HI)


def _sigmoid(x):
    return 1.0 / (1.0 + jnp.exp(-x))


def _silu(x):
    return x * _sigmoid(x)


def _softplus(x):
    return jnp.maximum(x, 0.0) + jnp.log1p(jnp.exp(-jnp.abs(x)))


def _gelu_tanh(x):
    return 0.5 * x * (1.0 + jnp.tanh(math.sqrt(2.0 / math.pi) * (x + 0.044715 * (x * x * x))))


def _rms(x, g):
    return x * lax.rsqrt(jnp.mean(x * x, axis=-1, keepdims=True) + EPS) * g


def _full(shape):
    n = len(shape)
    return pl.BlockSpec(shape, lambda *_: (0,) * n)


MOD_COLS = 1536


def _mod_kernel(c_ref, w_ref, b_ref, o_ref):
    o_ref[0] = _dot_hi(_silu(c_ref[...]), w_ref[0]) + b_ref[0]


def modulation(cond, w_mod, b_mod):
    depth, d, n = w_mod.shape
    return pl.pallas_call(
        _mod_kernel,
        grid=(depth, n // MOD_COLS),
        in_specs=[pl.BlockSpec((SUBLANES, d), lambda l, j: (0, 0)),
                  pl.BlockSpec((1, d, MOD_COLS), lambda l, j: (l, 0, j)),
                  pl.BlockSpec((1, 1, MOD_COLS), lambda l, j: (l, 0, j))],
        out_specs=pl.BlockSpec((1, SUBLANES, MOD_COLS), lambda l, j: (l, 0, j)),
        out_shape=jax.ShapeDtypeStruct((depth, SUBLANES, n), F32),
        compiler_params=_cp("parallel", "parallel"),
        name="modulation",
    )(cond, w_mod, b_mod.reshape(depth, 1, n))


def _inproj_kernel(*refs, tiles_per_seq, rope):
    (x_ref, xp_ref, xn_ref, g_ref, sc_ref, sh_ref, w_ref, lcw_ref, lcb_ref, scw_ref, scb_ref, gcw_ref) = refs[:12]
    cos_ref, sin_ref = (refs[12], refs[13]) if rope else (None, None)
    o_refs = refs[14:] if rope else refs[12:]
    i = pl.program_id(0)
    norm = lambda v: _rms(v, g_ref[...]) * (1.0 + sc_ref[0]) + sh_ref[0]
    p = _dot(norm(x_ref[...]), w_ref[...])
    n_conv = sum(P_WIDTHS[:P_CONV_GROUPS])
    ph = _dot(norm(jnp.concatenate([xp_ref[...], xn_ref[...]], axis=0)), w_ref[:, :n_conv])
    pos = i % tiles_per_seq
    prev = jnp.where(pos == 0, 0.0, ph[:SUBLANES])
    nxt = jnp.where(pos == tiles_per_seq - 1, 0.0, ph[SUBLANES:])
    row = lax.broadcasted_iota(jnp.int32, (SUBLANES, n_conv), 0)
    halo = jnp.where(row < 2, pltpu.roll(prev, 2, 0), jnp.where(row == 2, pltpu.roll(nxt, 2, 0), 0.0))
    c0, c1, c2 = GROUP_W, GROUP_W + 3 * GROUP_W, n_conv
    lru_u = _dwconv(p[:, :c0], halo[:, :c0], lcw_ref[...], lcb_ref[...])
    ssd_x = _silu(_dwconv(p[:, c0:c1], halo[:, c0:c1], scw_ref[...], scb_ref[...]))
    qkv = _silu(_dwconv(p[:, c1:c2], halo[:, c1:c2], gcw_ref[...]))
    qn = _l2norm_heads(qkv[:, :GROUP_W])
    kn = _l2norm_heads(qkv[:, GROUP_W:2 * GROUP_W])
    if rope:
        cos, sin = cos_ref[...], sin_ref[...]
        qn = qn * cos + _swap16(qn) * sin
        kn = kn * cos + _swap16(kn) * sin
    outs = [lru_u, ssd_x, jnp.concatenate([qn * (HEAD_DIM ** -0.5), kn, qkv[:, 2 * GROUP_W:]], axis=1)]
    off = n_conv
    for o_ref, w in zip(o_refs, P_WIDTHS):
        if outs:
            o_ref[...] = outs.pop(0)
        else:
            o_ref[...] = p[:, off:off + w]
            off += w


def in_projection(x, g, sc, sh, w, conv, rope, seq_len, tiles_per_group):
    t, d = x.shape
    tm = min(TOKEN_TILE, seq_len)
    tps = seq_len // tm
    hb = tm // SUBLANES
    vec = lambda i: (i // tiles_per_group, 0, 0)
    ins = [x, x, x, g, sc, sh, w, *conv]
    specs = [pl.BlockSpec((tm, d), lambda i: (i, 0)),
             pl.BlockSpec((SUBLANES, d), lambda i: (jnp.maximum(i * hb - 1, 0), 0)),
             pl.BlockSpec((SUBLANES, d), lambda i: (jnp.minimum((i + 1) * hb, t // SUBLANES - 1), 0)),
             _full((1, d)), pl.BlockSpec((1, 1, d), vec), pl.BlockSpec((1, 1, d), vec), _full(w.shape)]
    specs += [_full(a.shape) for a in conv]
    if rope is not None:
        ins += list(rope)
        specs += [pl.BlockSpec((tm, GROUP_W), lambda i: (i % tps, 0))] * 2
    return pl.pallas_call(
        functools.partial(_inproj_kernel, tiles_per_seq=tps, rope=rope is not None),
        grid=(t // tm,),
        in_specs=specs,
        out_specs=[pl.BlockSpec((tm, wd), lambda i: (i, 0)) for wd in P_WIDTHS],
        out_shape=[jax.ShapeDtypeStruct((t, wd), F32) for wd in P_WIDTHS],
        compiler_params=_cp("parallel"),
        name="in_projection",
    )(*ins)


def _dwconv(x, halo, w, b=None):
    q = x.shape[0]
    row = lax.broadcasted_iota(jnp.int32, (SUBLANES, x.shape[1]), 0)

    def shifted(s, keep_rolled, edge):
        r = pltpu.roll(x, s % q, 0)
        if s > 0:
            return jnp.concatenate([jnp.where(keep_rolled, r[:SUBLANES], edge), r[SUBLANES:]], axis=0)
        return jnp.concatenate([r[:q - SUBLANES], jnp.where(keep_rolled, r[q - SUBLANES:], edge)], axis=0)

    xm2 = shifted(2, row >= 2, halo)
    xm1 = shifted(1, row >= 1, pltpu.roll(halo, SUBLANES - 1, 0))
    xp1 = shifted(-1, row < SUBLANES - 1, pltpu.roll(halo, SUBLANES - 3, 0))
    y = w[0:1] * xm2 + w[1:2] * xm1 + w[2:3] * x + w[3:4] * xp1
    return y if b is None else y + b


def _pad_rows(a, rows=SUBLANES):
    return jnp.concatenate([a, jnp.zeros((rows - a.shape[0],) + a.shape[1:], a.dtype)], axis=0)


def _chunk_specs(nc, q, c):
    fwd = pl.BlockSpec((q, c), lambda b, i: (b * nc + i, 0))
    bwd = pl.BlockSpec((q, c), lambda b, i: (b * nc + nc - 1 - i, 0))
    return fwd, bwd


def _lru_kernel(xf_ref, xb_ref, h0_ref, wg_ref, bg_ref, lam_ref,
                yf_ref, yb_ref, hfin_ref, af_s, bf_s, ab_s, bb_s, carry_s):
    i = pl.program_id(1)
    q = xf_ref.shape[0]

    @pl.when(i == 0)
    def _():
        carry_s[...] = h0_ref[0]

    def coeffs(x_ref, d, a_s, b_s):
        u = x_ref[...]
        g = _dot(u, wg_ref[:, 2 * GROUP_W * d:2 * GROUP_W * (d + 1)]) + bg_ref[:, 2 * GROUP_W * d:2 * GROUP_W * (d + 1)]
        r = _sigmoid(g[:, :GROUP_W])
        gate_in = _sigmoid(g[:, GROUP_W:])
        log_a = -LRU_C * r * _softplus(-lam_ref[d:d + 1, :])
        a_s[...] = jnp.exp(log_a)
        b_s[...] = jnp.sqrt(1.0 - jnp.exp(2.0 * log_a)) * (gate_in * u)

    coeffs(xf_ref, 0, af_s, bf_s)
    coeffs(xb_ref, 1, ab_s, bb_s)

    ng = q // SUBLANES
    row = lax.broadcasted_iota(jnp.int32, (SUBLANES, GROUP_W), 0)

    def body(g, hs):
        h_f, h_b = hs
        i0 = pl.multiple_of(g * SUBLANES, SUBLANES)
        a = af_s[pl.ds(i0, SUBLANES), :]
        b = bf_s[pl.ds(i0, SUBLANES), :]
        for s in (1, 2, 4):
            m = row >= s
            b = jnp.where(m, a * pltpu.roll(b, s, 0) + b, b)
            a = jnp.where(m, a * pltpu.roll(a, s, 0), a)
        h = b + a * h_f
        yf_ref[pl.ds(i0, SUBLANES), :] = h
        h_f = h[SUBLANES - 1:SUBLANES, :]
        j0 = pl.multiple_of((ng - 1 - g) * SUBLANES, SUBLANES)
        a = ab_s[pl.ds(j0, SUBLANES), :]
        b = bb_s[pl.ds(j0, SUBLANES), :]
        for s in (1, 2, 4):
            m = row < SUBLANES - s
            b = jnp.where(m, a * pltpu.roll(b, SUBLANES - s, 0) + b, b)
            a = jnp.where(m, a * pltpu.roll(a, SUBLANES - s, 0), a)
        h = b + a * h_b
        yb_ref[pl.ds(j0, SUBLANES), :] = h
        return h_f, h[0:1, :]

    h_f, h_b = lax.fori_loop(0, ng, body, (carry_s[0:1, :], carry_s[1:2, :]))
    carry_s[0:1, :] = h_f
    carry_s[1:2, :] = h_b

    @pl.when(i == pl.num_programs(1) - 1)
    def _():
        hfin_ref[0] = carry_s[...]


def _block_diag(w):
    h, a, b = w.shape
    return jnp.einsum('hij,hg->higj', w, jnp.eye(h, dtype=w.dtype)).reshape(h * a, h * b)


def lru_params(wa, ba, wx, bx, lam):
    wg = jnp.concatenate([_block_diag(wa[0]), _block_diag(wx[0]), _block_diag(wa[1]), _block_diag(wx[1])], axis=1)
    bg = jnp.concatenate([ba[0], bx[0], ba[1], bx[1]])[None, :]
    return wg.astype(MXU_DTYPE), bg, _pad_rows(lam)


def lru_mixer(x, h0, bsz, wg, bg, lam):
    t, c = x.shape
    s = t // bsz
    q = min(LRU_CHUNK, s)
    nc = s // q
    xf, xb = _chunk_specs(nc, q, c)
    st = pl.BlockSpec((1, SUBLANES, c), lambda b, i: (b, 0, 0))
    return pl.pallas_call(
        _lru_kernel,
        grid=(bsz, nc),
        in_specs=[xf, xb, st, _full(wg.shape), _full(bg.shape), _full(lam.shape)],
        out_specs=[xf, xb, st],
        out_shape=[jax.ShapeDtypeStruct((t, c), F32), jax.ShapeDtypeStruct((t, c), F32),
                   jax.ShapeDtypeStruct((bsz, SUBLANES, c), F32)],
        scratch_shapes=[pltpu.VMEM((q, c), F32)] * 4 + [pltpu.VMEM((SUBLANES, c), F32)],
        compiler_params=_cp("parallel", "arbitrary"),
        name="lru_mixer",
    )(x, x, h0, wg, bg, lam)


NA_KEYS = NA_WIN_ROWS * GRID_W
NA_ROW_BLOCK = 8


def na_bias_slabs(table):
    qc = np.arange(GRID_W)[:, None]
    kc = np.arange(GRID_W)[None, :]
    win0 = np.clip(qc - NA_WIN_COLS // 2, 0, GRID_W - NA_WIN_COLS)
    ok = (kc >= win0) & (kc < win0 + NA_WIN_COLS)
    dc = np.clip(kc - qc + NA_WIN_COLS - 1, 0, 2 * NA_WIN_COLS - 2)
    dr = np.arange(NA_WIN_ROWS)[:, None] + np.arange(NA_WIN_ROWS)[None, :]
    b = table.astype(F32)[:, dr][:, :, :, dc]
    b = jnp.where(ok[None, None, None], b, NEG_INF)
    h = table.shape[0]
    return b.transpose(0, 1, 3, 2, 4).reshape(h, NA_WIN_ROWS, GRID_W, NA_KEYS)


def _na_span_start(j, rows):
    return jnp.clip(j * NA_ROW_BLOCK - NA_WIN_ROWS // 2, 0, rows - (NA_ROW_BLOCK + NA_WIN_ROWS - 1))


def _na_kernel(q_ref, kw_ref, vw_ref, kc_ref, vc_ref, slab_ref, o_ref, *, rows):
    j = pl.program_id(1)
    ustart = _na_span_start(j, rows)
    q = q_ref[...] * (HEAD_DIM ** -0.5)
    kc, vc = kc_ref[0], vc_ref[0]
    heads = [slice(h * HEAD_DIM, (h + 1) * HEAD_DIM) for h in range(N_HEADS)]
    qrows = [slice(i * GRID_W, (i + 1) * GRID_W) for i in range(NA_ROW_BLOCK)]
    kws, vws, offs = [], [], []
    for i in range(NA_ROW_BLOCK):
        r = j * NA_ROW_BLOCK + i
        r0 = jnp.clip(r - NA_WIN_ROWS // 2, 0, rows - NA_WIN_ROWS)
        start = pl.multiple_of((r0 - ustart) * GRID_W, GRID_W)
        kws.append(kw_ref[pl.ds(start, NA_KEYS), :])
        vws.append(vw_ref[pl.ds(start, NA_KEYS), :])
        offs.append(r0 - r + NA_WIN_ROWS - 1)
    s_ctx = [_dot_nt(q[:, sl], kc[:, sl]) for sl in heads]
    s_loc = [[_dot_nt(q[qr, sl], kws[i][:, sl]) + slab_ref[h, offs[i]] for h, sl in enumerate(heads)]
             for i, qr in enumerate(qrows)]
    m = [[jnp.maximum(jnp.max(s_loc[i][h], axis=-1, keepdims=True), jnp.max(s_ctx[h][qr], axis=-1, keepdims=True))
          for h in range(N_HEADS)] for i, qr in enumerate(qrows)]
    p_loc = [[jnp.exp(s_loc[i][h] - m[i][h]) for h in range(N_HEADS)] for i in range(NA_ROW_BLOCK)]
    p_ctx = [jnp.exp(s_ctx[h] - jnp.concatenate([m[i][h] for i in range(NA_ROW_BLOCK)], axis=0))
             for h in range(N_HEADS)]
    o_ctx = [_dot(p_ctx[h], vc[:, sl]) for h, sl in enumerate(heads)]
    rows_out = []
    for i, qr in enumerate(qrows):
        outs = []
        for h, sl in enumerate(heads):
            den = jnp.sum(p_loc[i][h], axis=-1, keepdims=True) + jnp.sum(p_ctx[h][qr], axis=-1, keepdims=True)
            outs.append((_dot(p_loc[i][h], vws[i][:, sl]) + o_ctx[h][qr]) / den)
        rows_out.append(jnp.concatenate(outs, axis=1))
    o_ref[...] = jnp.concatenate(rows_out, axis=0)


def na_mixer(q, k, v, kc, vc, slabs, bsz):
    t, c = q.shape
    s = t // bsz
    rows = s // GRID_W
    n_ctx = kc.shape[1]
    span = (NA_ROW_BLOCK + NA_WIN_ROWS - 1) * GRID_W

    def win(b, j):
        return ((b * rows + _na_span_start(j, rows)) * GRID_W, 0)

    wspec = pl.BlockSpec((pl.Element(span), pl.Element(c)), win)
    cspec = pl.BlockSpec((1, n_ctx, c), lambda b, j: (b, 0, 0))
    qspec = pl.BlockSpec((NA_ROW_BLOCK * GRID_W, c), lambda b, j: (b * (rows // NA_ROW_BLOCK) + j, 0))
    return pl.pallas_call(
        functools.partial(_na_kernel, rows=rows),
        grid=(bsz, rows // NA_ROW_BLOCK),
        in_specs=[qspec, wspec, wspec, cspec, cspec, _full(slabs.shape)],
        out_specs=qspec,
        out_shape=jax.ShapeDtypeStruct((t, c), F32),
        compiler_params=_cp("parallel", "arbitrary"),
        name="na_mixer",
    )(q, k, v, kc, vc, slabs)


def _ctx_attn_kernel(q_ref, k_ref, v_ref, o_ref):
    q = q_ref[0] * (HEAD_DIM ** -0.5)
    k, v = k_ref[0], v_ref[0]
    outs = []
    for h in range(N_HEADS):
        sl = slice(h * HEAD_DIM, (h + 1) * HEAD_DIM)
        s = _dot_nt(q[:, sl], k[:, sl])
        p = jnp.exp(s - jnp.max(s, axis=-1, keepdims=True))
        outs.append(_dot(p, v[:, sl]) / jnp.sum(p, axis=-1, keepdims=True))
    o_ref[0] = jnp.concatenate(outs, axis=1)


def ctx_attention(q, k, v):
    spec = pl.BlockSpec((1,) + q.shape[1:], lambda b: (b, 0, 0))
    return pl.pallas_call(
        _ctx_attn_kernel,
        grid=(q.shape[0],),
        in_specs=[spec, spec, spec],
        out_specs=spec,
        out_shape=jax.ShapeDtypeStruct(q.shape, F32),
        compiler_params=_cp("parallel"),
        name="ctx_attention",
    )(q, k, v)


def _small_vec(vals, off):
    v = jnp.zeros((LANES,), F32).at[off:off + 2 * N_HEADS].set(vals.reshape(-1).astype(F32))
    return v[None, :]


def _lane_mask(off):
    lane = lax.broadcasted_iota(jnp.int32, (1, LANES), 1)
    return (lane >= off) & (lane < off + 2 * N_HEADS)


def _tri_masks(q):
    rowi = lax.broadcasted_iota(jnp.int32, (q, q), 0)
    coli = lax.broadcasted_iota(jnp.int32, (q, q), 1)
    return rowi, coli


def _ssd_kernel(xf_ref, xb_ref, sf_ref, sb_ref, h0_ref, dtb_ref, alog_ref,
                yf_ref, yb_ref, hfin_ref, state_s):
    i = pl.program_id(1)
    q = xf_ref.shape[0]

    @pl.when(i == 0)
    def _():
        state_s[...] = h0_ref[0]

    rowi, coli = _tri_masks(q)
    a_neg = jnp.where(_lane_mask(SM_DT), -jnp.exp(alog_ref[...]), 0.0)

    chains = [(d, h) for d in range(2) for h in range(N_HEADS)]
    per_head = N_HEADS // SSD_GROUPS
    scores, xdt, c_in, b_out, e_last = [], [], [], [], []
    for d, (x_ref, sm_ref) in enumerate(((xf_ref, sf_ref), (xb_ref, sb_ref))):
        xbc = x_ref[...]
        dt = _softplus(sm_ref[...] + dtb_ref[...])
        keep = (rowi >= coli) if d == 0 else (rowi <= coli)
        acum = _dot_tri(keep, dt * a_neg)
        acum_t = acum.T
        last = acum[q - 1:q, :] if d == 0 else acum[0:1, :]
        dec_end = jnp.exp(last - acum)
        e_acum = jnp.exp(acum)
        e_end = jnp.exp(last)
        bgs = [xbc[:, GROUP_W + SSD_STATE * g:GROUP_W + SSD_STATE * (g + 1)] for g in range(SSD_GROUPS)]
        cgs = [xbc[:, GROUP_W + SSD_STATE * (SSD_GROUPS + g):GROUP_W + SSD_STATE * (SSD_GROUPS + g + 1)]
               for g in range(SSD_GROUPS)]
        cbt = [_dot_nt(cg, bg) for cg, bg in zip(cgs, bgs)]
        for h in range(N_HEADS):
            g = h // per_head
            ln = SM_DT + N_HEADS * d + h
            lmat = jnp.exp(jnp.where(keep, acum[:, ln:ln + 1] - acum_t[ln:ln + 1, :], NEG_INF))
            scores.append(cbt[g] * lmat)
            xdt.append(xbc[:, h * HEAD_DIM:(h + 1) * HEAD_DIM] * dt[:, ln:ln + 1])
            c_in.append(cgs[g] * e_acum[:, ln:ln + 1])
            b_out.append(bgs[g] * dec_end[:, ln:ln + 1])
            e_last.append(e_end[:, ln:ln + 1])
    states = [state_s[d, h] for d, h in chains]
    y_diag = [_dot(s, x) for s, x in zip(scores, xdt)]
    y_off = [_dot(c, st) for c, st in zip(c_in, states)]
    upd = [_dot_tn(b, x) for b, x in zip(b_out, xdt)]
    for n, (d, h) in enumerate(chains):
        state_s[d, h] = states[n] * e_last[n] + upd[n]
    ys = [a + b for a, b in zip(y_diag, y_off)]
    yf_ref[...] = jnp.concatenate(ys[:N_HEADS], axis=1)
    yb_ref[...] = jnp.concatenate(ys[N_HEADS:], axis=1)

    @pl.when(i == pl.num_programs(1) - 1)
    def _():
        hfin_ref[0] = state_s[...]


def ssd_params(a_log, dt_bias):
    return _small_vec(dt_bias, SM_DT), _small_vec(a_log, SM_DT)


def ssd_mixer(xbc, sm, h0, bsz, dtb, alog):
    t, c = xbc.shape
    s = t // bsz
    q = min(SSD_CHUNK, s)
    nc = s // q
    xf, xb = _chunk_specs(nc, q, c)
    sf, sb = _chunk_specs(nc, q, LANES)
    yf, yb = _chunk_specs(nc, q, GROUP_W)
    st = pl.BlockSpec((1,) + h0.shape[1:], lambda b, i: (b, 0, 0, 0, 0))
    y_shape = jax.ShapeDtypeStruct((t, GROUP_W), F32)
    return pl.pallas_call(
        _ssd_kernel,
        grid=(bsz, nc),
        in_specs=[xf, xb, sf, sb, st, _full(dtb.shape), _full(alog.shape)],
        out_specs=[yf, yb, st],
        out_shape=[y_shape, y_shape, jax.ShapeDtypeStruct(h0.shape, F32)],
        scratch_shapes=[pltpu.VMEM(h0.shape[1:], F32)],
        compiler_params=_cp("parallel", "arbitrary"),
        name="ssd_mixer",
    )(xbc, xbc, sm, sm, h0, dtb, alog)


def rope_tables(seq):
    t = jnp.arange(seq)
    row = (t // GRID_W).astype(F32)
    col = (t % GRID_W).astype(F32)
    inv = ROPE_BASE ** (-jnp.arange(0, ROPE_AXIS_DIM, 2, dtype=F32) / ROPE_AXIS_DIM)
    ar, ac = row[:, None] * inv, col[:, None] * inv
    cos = jnp.concatenate([jnp.cos(ar), jnp.cos(ar), jnp.cos(ac), jnp.cos(ac)], axis=1)
    sin = jnp.concatenate([-jnp.sin(ar), jnp.sin(ar), -jnp.sin(ac), jnp.sin(ac)], axis=1)
    return jnp.tile(cos, (1, N_HEADS)), jnp.tile(sin, (1, N_HEADS))


def _swap16(x):
    lane = lax.broadcasted_iota(jnp.int32, x.shape, 1)
    half = ROPE_AXIS_DIM // 2
    return jnp.where((lane & (ROPE_AXIS_DIM - 1)) < half,
                     pltpu.roll(x, x.shape[1] - half, 1), pltpu.roll(x, half, 1))


def _head_sums(sq):
    c = sq.shape[1]
    li = lax.broadcasted_iota(jnp.int32, (c, c), 0)
    lj = lax.broadcasted_iota(jnp.int32, (c, c), 1)
    sh = HEAD_DIM.bit_length() - 1
    ones = _mx(((li >> sh) == (lj >> sh)).astype(F32))
    hi = _mx(sq)
    lo = _mx(sq - hi.astype(F32))
    return jnp.dot(hi, ones, preferred_element_type=F32) + jnp.dot(lo, ones, preferred_element_type=F32)


def _l2norm_heads(x):
    return x * lax.rsqrt(_head_sums(x * x) + EPS)


def _dot_tri(mask, x):
    m = _mx(mask.astype(F32))
    x1 = _mx(x)
    r1 = x - x1.astype(F32)
    x2 = _mx(r1)
    x3 = _mx(r1 - x2.astype(F32))
    return (jnp.dot(m, x1, preferred_element_type=F32) + jnp.dot(m, x2, preferred_element_type=F32)
            + jnp.dot(m, x3, preferred_element_type=F32))


def _same_block(rowi, coli, n):
    sh = n.bit_length() - 1
    return (rowi >> sh) == (coli >> sh)


def _solve_unit_tri(a_list, rhs_list, rowi, coli, chunk):
    mm = lambda x, y: jnp.dot(x, y, preferred_element_type=F32)
    eye = (rowi == coli).astype(F32)
    in_base = _same_block(rowi, coli, GDN_BASE)
    base = [_mx(jnp.where(in_base, a, 0.0)) for a in a_list]
    ts = [jnp.where(in_base, eye - a, 0.0) for a in a_list]
    ps = [_mx(mm(b, b)) for b in base]
    ts = [t + mm(_mx(t), p) for t, p in zip(ts, ps)]
    n = 4
    while n < GDN_BASE:
        ps = [_mx(mm(p, p)) for p in ps]
        ts = [t + mm(_mx(t), p) for t, p in zip(ts, ps)]
        n *= 2
    n = GDN_BASE
    while 2 * n < chunk:
        inner = _same_block(rowi, coli, 2 * n) & jnp.logical_not(_same_block(rowi, coli, n))
        offs = [_mx(jnp.where(inner, a, 0.0)) for a in a_list]
        tb = [_mx(t) for t in ts]
        ms = [_mx(mm(t, off)) for t, off in zip(tb, offs)]
        ts = [t - mm(m, t_b) for t, m, t_b in zip(ts, ms, tb)]
        n *= 2
    outer = jnp.logical_not(_same_block(rowi, coli, n))
    offs = [_mx(jnp.where(outer, a, 0.0)) for a in a_list]
    tb = [_mx(t) for t in ts]
    ys = [mm(t, _mx(r)) for t, r in zip(tb, rhs_list)]
    zs = [_mx(mm(off, _mx(y))) for off, y in zip(offs, ys)]
    return [y - mm(t, z) for y, t, z in zip(ys, tb, zs)]


def _gdn_kernel(xf_ref, xb_ref, sf_ref, sb_ref, s0_ref, alog_ref, dtb_ref, of_ref, ob_ref, sfin_ref, state_s):
    i = pl.program_id(1)
    tq = xf_ref.shape[0]
    ck = min(GDN_CHUNK, tq)
    nck = tq // ck

    @pl.when(i == 0)
    def _():
        state_s[...] = s0_ref[0]

    sub = min(GDN_SUB, tq)
    nsub = tq // sub
    rowt, colt = _tri_masks(tq)
    in_chunk_t = _same_block(rowt, colt, ck)
    rowi, coli = _tri_masks(sub)
    in_chunk = _same_block(rowi, coli, ck)
    a_neg = jnp.where(_lane_mask(SM_DECAY), -jnp.exp(alog_ref[...]), 0.0)

    a_list, rhs_list, qkm, qg, kd, e_last = [], [], [], [], [], []
    for d, (x_ref, sm_ref) in enumerate(((xf_ref, sf_ref), (xb_ref, sb_ref))):
        qkv = x_ref[...]
        qn, kn, v = qkv[:, :GROUP_W], qkv[:, GROUP_W:2 * GROUP_W], qkv[:, 2 * GROUP_W:]
        sm = sm_ref[...]
        beta = _sigmoid(sm)
        keep_t = in_chunk_t & ((rowt >= colt) if d == 0 else (rowt <= colt))
        keep = in_chunk & ((rowi >= coli) if d == 0 else (rowi <= coli))
        strict = in_chunk & ((rowi > coli) if d == 0 else (rowi < coli))
        gc = _dot_tri(keep_t, _softplus(sm + dtb_ref[...]) * a_neg)
        gc_t = gc.T
        edge = ck - 1 if d == 0 else 0
        last = jnp.concatenate([jnp.broadcast_to(gc[c * ck + edge:c * ck + edge + 1, :], (ck, LANES))
                                for c in range(nck)], axis=0)
        e_gc = jnp.exp(gc)
        e_end = jnp.exp(last - gc)
        e_last.append(jnp.exp(last))
        for h in range(N_HEADS):
            sl = slice(h * HEAD_DIM, (h + 1) * HEAD_DIM)
            lg = SM_DECAY + N_HEADS * d + h
            lb = SM_BETA + N_HEADS * d + h
            qh, kh, bcol = qn[:, sl], kn[:, sl], beta[:, lb:lb + 1]
            kb = kh * bcol
            rhs = jnp.concatenate([v[:, sl] * bcol, kb * e_gc[:, lg:lg + 1]], axis=1)
            qg.append(qh * e_gc[:, lg:lg + 1])
            kd.append(kh * e_end[:, lg:lg + 1])
            qh_m, kh_m, kb_m = _mx(qh), _mx(kh), _mx(kb)
            for s in range(nsub):
                rs = slice(s * sub, (s + 1) * sub)
                decay = jnp.exp(jnp.where(keep, gc[rs, lg:lg + 1] - gc_t[lg:lg + 1, rs], NEG_INF))
                a_list.append(jnp.where(strict, _dot_nt(kb_m[rs], kh_m[rs]) * decay, 0.0))
                rhs_list.append(rhs[rs])
                qkm.append(_dot_nt(qh_m[rs], kh_m[rs]) * decay)
    sols = _solve_unit_tri(a_list, rhs_list, rowi, coli, ck)
    sols = [jnp.concatenate(sols[n * nsub:(n + 1) * nsub], axis=0) for n in range(2 * N_HEADS)]

    chains = [(d, h) for d in range(2) for h in range(N_HEADS)]
    states = [state_s[d, h] for d, h in chains]
    v_new = [[None] * nck for _ in chains]
    o_st = [[None] * nck for _ in chains]
    for step in range(nck):
        rows = [slice((step if d == 0 else nck - 1 - step) * ck, (step if d == 0 else nck - 1 - step) * ck + ck)
                for d, _ in chains]
        ms = [_dot(jnp.concatenate([sols[n][r, HEAD_DIM:], qg[n][r]], axis=0), states[n])
              for n, r in enumerate(rows)]
        for n, (d, _) in enumerate(chains):
            c = step if d == 0 else nck - 1 - step
            v_new[n][c] = sols[n][rows[n], :HEAD_DIM] - ms[n][:ck]
            o_st[n][c] = ms[n][ck:]
        ups = [_dot_tn(kd[n][r], v_new[n][step if chains[n][0] == 0 else nck - 1 - step])
               for n, r in enumerate(rows)]
        for n, (d, h) in enumerate(chains):
            lg = SM_DECAY + N_HEADS * d + h
            states[n] = states[n] * e_last[d][rows[n].start:rows[n].start + 1, lg:lg + 1] + ups[n]
    cps = sub // ck
    outs = [jnp.concatenate(o_st[n], axis=0)
            + jnp.concatenate([_dot(qkm[n * nsub + s], jnp.concatenate(v_new[n][s * cps:(s + 1) * cps], axis=0))
                               for s in range(nsub)], axis=0)
            for n in range(len(chains))]
    of_ref[...] = jnp.concatenate(outs[:N_HEADS], axis=1)
    ob_ref[...] = jnp.concatenate(outs[N_HEADS:], axis=1)
    for n, (d, h) in enumerate(chains):
        state_s[d, h] = states[n]

    @pl.when(i == pl.num_programs(1) - 1)
    def _():
        sfin_ref[0] = state_s[...]


def gdn_params(a_log, dt_bias):
    return _small_vec(a_log, SM_DECAY), _small_vec(dt_bias, SM_DECAY)


def gdn_mixer(qkv, sm, s0, bsz, alog, dtb):
    t, c = qkv.shape
    s = t // bsz
    q = min(GDN_TILE, s)
    nc = s // q
    xf, xb = _chunk_specs(nc, q, c)
    sf, sb = _chunk_specs(nc, q, LANES)
    of, ob = _chunk_specs(nc, q, GROUP_W)
    st = pl.BlockSpec((1,) + s0.shape[1:], lambda b, i: (b, 0, 0, 0, 0))
    o_shape = jax.ShapeDtypeStruct((t, GROUP_W), F32)
    return pl.pallas_call(
        _gdn_kernel,
        grid=(bsz, nc),
        in_specs=[xf, xb, sf, sb, st, _full(alog.shape), _full(dtb.shape)],
        out_specs=[of, ob, st],
        out_shape=[o_shape, o_shape, jax.ShapeDtypeStruct(s0.shape, F32)],
        scratch_shapes=[pltpu.VMEM(s0.shape[1:], F32)],
        compiler_params=_cp("parallel", "arbitrary"),
        name="gdn_mixer",
    )(qkv, qkv, sm, sm, s0, alog, dtb)


def _split_hi_lo(a):
    hi = _mx(a)
    return hi, _mx(a - hi.astype(F32))


def _outproj_kernel(x_ref, ahf_ref, ahb_ref, ag_ref, bo_ref, cyf_ref, cyb_ref, cxc_ref, cz_ref,
                    dof_ref, dob_ref, dz_ref, wout_ref, gpost_ref, ga1_ref, gpre_ref, sc2_ref, sh2_ref,
                    dskip_ref, cnorm_ref, dnorm_ref, rhi_ref, rlo_ref, xo_ref, hp_ref, lg_ref):
    m_a = (ahf_ref[...] + ahb_ref[...]) * _gelu_tanh(ag_ref[...])
    y_c = (cyf_ref[...] + cyb_ref[...] + cxc_ref[...] * dskip_ref[...]) * _silu(cz_ref[...])
    m_c = _rms(y_c, cnorm_ref[...])
    o_d = dof_ref[...] + dob_ref[...]
    m_d = o_d * lax.rsqrt(_head_sums(o_d * o_d) * (1.0 / HEAD_DIM) + EPS) * dnorm_ref[...] * _silu(dz_ref[...])
    mix = jnp.concatenate([_mx(m_a), _mx(bo_ref[...]), _mx(m_c), _mx(m_d)], axis=1)
    ml = jnp.dot(mix, wout_ref[...], preferred_element_type=F32)
    x_new = x_ref[...] + ga1_ref[0] * _rms(ml, gpost_ref[...])
    xo_ref[...] = x_new
    h2 = _rms(x_new, gpre_ref[...]) * (1.0 + sc2_ref[0]) + sh2_ref[0]
    hi, lo = _split_hi_lo(h2)
    hp_ref[...] = _pack_pairs(h2)
    rhi = rhi_ref[...]
    lg_ref[...] = (jnp.dot(hi, rhi, preferred_element_type=F32) + jnp.dot(lo, rhi, preferred_element_type=F32)
                   + jnp.dot(hi, rlo_ref[...], preferred_element_type=F32))


def out_projection(x, mixers, w_out, gpost, ga1, gpre, sc2, sh2, dskip, cnorm, dnorm, router_w, tiles_per_group):
    t, d = x.shape
    tm = min(TOKEN_TILE, t)
    vec = lambda i: (i // tiles_per_group, 0, 0)
    row = lambda w: pl.BlockSpec((tm, w), lambda i: (i, 0))
    ne = LANES
    rhi, rlo = _split_hi_lo(jnp.pad(router_w.astype(F32), ((0, 0), (0, ne - router_w.shape[1]))))
    return pl.pallas_call(
        _outproj_kernel,
        grid=(t // tm,),
        in_specs=[row(d)] + [row(GROUP_W)] * 11
                 + [_full(w_out.shape), _full((1, d)), pl.BlockSpec((1, 1, d), vec), _full((1, d)),
                    pl.BlockSpec((1, 1, d), vec), pl.BlockSpec((1, 1, d), vec),
                    _full((1, GROUP_W)), _full((1, GROUP_W)), _full((1, GROUP_W)), _full(rhi.shape), _full(rlo.shape)],
        out_specs=[row(d), row(d // 2), row(ne)],
        out_shape=[jax.ShapeDtypeStruct((t, d), F32), jax.ShapeDtypeStruct((t, d // 2), jnp.uint32),
                   jax.ShapeDtypeStruct((t, ne), F32)],
        compiler_params=_cp("parallel"),
        name="out_projection",
    )(x, *mixers, w_out, gpost, ga1, gpre, sc2, sh2, dskip, cnorm, dnorm, rhi, rlo)


def _rank_before(vals, idx, count, stride):
    rank = jnp.zeros(vals.shape, jnp.int32)
    for j in range(count):
        other = vals[j * stride:j * stride + 1, :]
        ahead = (other > vals) | ((other == vals) & (idx > j))
        rank = rank + ahead.astype(jnp.int32)
    return rank


def _xor_partner(x, row, s):
    n = x.shape[0]
    return jnp.where((row & s) == 0, pltpu.roll(x, n - s, 0), pltpu.roll(x, s, 0))


def _route(logits, router_b):
    ne = N_EXPERTS
    gsz = ne // N_EXPERT_GROUPS
    scores = _sigmoid(logits.T[:ne, :])
    tm = scores.shape[1]
    biased = scores + router_b
    row = lax.broadcasted_iota(jnp.int32, (ne, tm), 0)
    m1, m2 = biased, jnp.full((ne, tm), -jnp.inf, F32)
    s = 1
    while s < gsz:
        o1, o2 = _xor_partner(m1, row, s), _xor_partner(m2, row, s)
        m2 = jnp.maximum(jnp.minimum(m1, o1), jnp.maximum(m2, o2))
        m1 = jnp.maximum(m1, o1)
        s *= 2
    gidx = row >> (gsz.bit_length() - 1)
    group_ok = _rank_before(m1 + m2, gidx, N_EXPERT_GROUPS, gsz) < TOPK_GROUPS
    choice = jnp.where(group_ok, biased, -jnp.inf)
    rank = _rank_before(choice, row, ne, 1)
    gate = jnp.where(rank < TOP_K, scores, 0.0)
    gate = gate / jnp.sum(gate, axis=0, keepdims=True) * ROUTED_SCALE
    return gate, rank, row


def _to_token_major(x):
    n, tm = x.shape
    return jnp.concatenate([x, jnp.zeros((LANES - n, tm), x.dtype)], axis=0).T


def _router_kernel(lg_ref, rb_ref, gate_ref):
    gate, _, _ = _route(lg_ref[...], rb_ref[...])
    gate_ref[...] = _to_token_major(gate)


def _router_dispatch_kernel(lg_ref, rb_ref, gk_ref, ek_ref, pk_ref, cnt_ref, carry_s):
    i = pl.program_id(0)

    @pl.when(i == 0)
    def _():
        carry_s[...] = jnp.zeros(carry_s.shape, F32)

    gate, rank, row = _route(lg_ref[...], rb_ref[...])
    tm = gate.shape[1]
    picked = (rank < TOP_K).astype(F32)
    before = lax.broadcasted_iota(jnp.int32, (tm, tm), 0) < lax.broadcasted_iota(jnp.int32, (tm, tm), 1)
    pos = _dot(picked, before.astype(F32)) + carry_s[:, 0:1]
    carry_s[...] = carry_s[...] + jnp.sum(picked, axis=1, keepdims=True)
    gk, ek, pk = [], [], []
    for k in range(TOP_K):
        sel = rank == k
        gk.append(jnp.sum(jnp.where(sel, gate, 0.0), axis=0, keepdims=True))
        ek.append(jnp.sum(jnp.where(sel, row, 0), axis=0, keepdims=True))
        pk.append(jnp.sum(jnp.where(sel, pos, 0.0), axis=0, keepdims=True))
    gk_ref[...] = _to_token_major(jnp.concatenate(gk, axis=0))
    ek_ref[...] = jnp.concatenate(ek, axis=0)
    pk_ref[...] = jnp.concatenate(pk, axis=0).astype(jnp.int32)

    @pl.when(i == pl.num_programs(0) - 1)
    def _():
        cnt_ref[...] = carry_s[...].astype(jnp.int32)


def router_dispatch(logits, router_b):
    t, w = logits.shape
    tm = min(TOKEN_TILE, t)
    return pl.pallas_call(
        _router_dispatch_kernel,
        grid=(t // tm,),
        in_specs=[pl.BlockSpec((tm, w), lambda i: (i, 0)), _full((N_EXPERTS, 1))],
        out_specs=[pl.BlockSpec((tm, w), lambda i: (i, 0)),
                   pl.BlockSpec((TOP_K, tm), lambda i: (0, i)),
                   pl.BlockSpec((TOP_K, tm), lambda i: (0, i)),
                   _full((N_EXPERTS, LANES))],
        out_shape=[jax.ShapeDtypeStruct((t, w), F32), jax.ShapeDtypeStruct((TOP_K, t), jnp.int32),
                   jax.ShapeDtypeStruct((TOP_K, t), jnp.int32), jax.ShapeDtypeStruct((N_EXPERTS, LANES), jnp.int32)],
        scratch_shapes=[pltpu.VMEM((N_EXPERTS, LANES), F32)],
        compiler_params=_cp("arbitrary"),
        name="router_dispatch",
    )(logits, router_b.reshape(N_EXPERTS, 1).astype(F32))


def router_gates(logits, router_b):
    t, w = logits.shape
    tm = min(TOKEN_TILE, t)
    return pl.pallas_call(
        _router_kernel,
        grid=(t // tm,),
        in_specs=[pl.BlockSpec((tm, w), lambda i: (i, 0)), _full((N_EXPERTS, 1))],
        out_specs=pl.BlockSpec((tm, w), lambda i: (i, 0)),
        out_shape=jax.ShapeDtypeStruct((t, w), F32),
        compiler_params=_cp("parallel"),
        name="router_gates",
    )(logits, router_b.reshape(N_EXPERTS, 1).astype(F32))


def _moe_kernel(h_ref, gate_ref, x_ref, wg_ref, wu_ref, wd_ref, sg_ref, su_ref, sd_ref, gpost_ref, ga2_ref,
                o_ref, acc_s):
    e = pl.program_id(1)
    h = _mx(_unpack_pairs(h_ref[...]))

    @pl.when(e == 0)
    def _():
        hs = _silu(jnp.dot(h, sg_ref[...], preferred_element_type=F32)) * jnp.dot(h, su_ref[...], preferred_element_type=F32)
        acc_s[...] = jnp.dot(_mx(hs), sd_ref[...], preferred_element_type=F32)

    gates = gate_ref[...]
    lane = lax.broadcasted_iota(jnp.int32, gates.shape, 1)
    hid = []
    for j in range(MOE_EB):
        gcol = jnp.sum(jnp.where(lane == e * MOE_EB + j, gates, 0.0), axis=1, keepdims=True)
        g = jnp.dot(h, _mx(wg_ref[j]), preferred_element_type=F32)
        u = jnp.dot(h, _mx(wu_ref[j]), preferred_element_type=F32)
        hid.append(_mx(_silu(g) * u * gcol))
    wd = _mx(wd_ref[...]).reshape(MOE_EB * D_EXPERT, -1)
    acc_s[...] += jnp.dot(jnp.concatenate(hid, axis=1), wd, preferred_element_type=F32)

    @pl.when(e == pl.num_programs(1) - 1)
    def _():
        o_ref[...] = x_ref[...] + ga2_ref[0] * _rms(acc_s[...], gpost_ref[...])


def moe_ffn(h, gates, x, layer, wg, wu, wd, sg, su, sd, gpost, ga2, tiles_per_group):
    t, d = x.shape
    tm = min(MOE_TILE, t)
    _, ne, _, f = wg.shape
    row = lambda w: pl.BlockSpec((tm, w), lambda i, e: (i, 0))
    return pl.pallas_call(
        _moe_kernel,
        grid=(t // tm, ne // MOE_EB),
        in_specs=[row(h.shape[1]), row(gates.shape[1]), row(d),
                  pl.BlockSpec((None, MOE_EB, d, f), lambda i, e: (layer, e, 0, 0)),
                  pl.BlockSpec((None, MOE_EB, d, f), lambda i, e: (layer, e, 0, 0)),
                  pl.BlockSpec((None, MOE_EB, f, d), lambda i, e: (layer, e, 0, 0)),
                  _full(sg.shape), _full(su.shape), _full(sd.shape), _full((1, d)),
                  pl.BlockSpec((1, 1, d), lambda i, e: (i // tiles_per_group, 0, 0))],
        out_specs=row(d),
        out_shape=jax.ShapeDtypeStruct((t, d), F32),
        scratch_shapes=[pltpu.VMEM((tm, d), F32)],
        compiler_params=_cp("parallel", "arbitrary"),
        name="moe_ffn",
    )(h, gates, x, wg, wu, wd, sg, su, sd, gpost, ga2)


def moe_plan(counts, n_tokens):
    n_blocks = (n_tokens * TOP_K + N_EXPERTS * (MOE_BLOCK - 1) + MOE_BLOCK - 1) // MOE_BLOCK
    cnt = counts[:, 0]
    padded = (cnt + MOE_BLOCK - 1) // MOE_BLOCK * MOE_BLOCK
    pad_end = jnp.cumsum(padded)
    off = pad_end - padded
    start = jnp.arange(n_blocks, dtype=jnp.int32) * MOE_BLOCK
    be = jnp.minimum(jnp.sum(pad_end[None, :] <= start[:, None], axis=1), N_EXPERTS - 1).astype(jnp.int32)
    mine = be[:, None] == jnp.arange(N_EXPERTS, dtype=jnp.int32)[None, :]
    end = jnp.sum(jnp.where(mine, (off + cnt)[None, :], 0), axis=1)
    nv = jnp.clip(end - start, 0, MOE_BLOCK).astype(jnp.int32)
    return off.astype(jnp.int32), be, nv


def _rows_kernel(off_ref, ek_ref, pk_ref, dest_ref):
    ek = ek_ref[...]
    dest = pk_ref[...]
    for e in range(N_EXPERTS):
        dest = dest + jnp.where(ek == e, off_ref[e], 0)
    dest_ref[...] = dest


def moe_rows(off, ek, pk):
    k, t = ek.shape
    tm = min(MOE_PLAN_TILE, t)
    spec = pl.BlockSpec((k, tm), lambda i, off: (0, i))
    return pl.pallas_call(
        _rows_kernel,
        grid_spec=pltpu.PrefetchScalarGridSpec(num_scalar_prefetch=1, grid=(t // tm,),
                                               in_specs=[spec, spec], out_specs=spec),
        out_shape=jax.ShapeDtypeStruct((k, t), jnp.int32),
        compiler_params=_cp("arbitrary"),
        name="moe_rows",
    )(off, ek, pk)


U32 = jnp.uint32
HIGH_HALF = 0xFFFF0000


def _pack_pairs(x):
    w = x.shape[1] // 2
    bits = lax.bitcast_convert_type(x.astype(jnp.bfloat16).astype(F32), U32)
    return (bits[:, w:] & jnp.uint32(HIGH_HALF)) | (bits[:, :w] >> 16)


def _unpack_pairs(p):
    lo = lax.bitcast_convert_type(p << 16, F32)
    hi = lax.bitcast_convert_type(p & jnp.uint32(HIGH_HALF), F32)
    return jnp.concatenate([lo, hi], axis=1)


def _sc_workers():
    info = plsc.get_sparse_core_info()
    return info.num_cores, info.num_cores * info.num_subcores


def sc_scatter_rows(src, idx, n_rows):
    k, t = idx.shape
    w = src.shape[1]
    n_cores, n_workers = _sc_workers()
    per_worker = t // n_workers
    mesh = plsc.VectorSubcoreMesh(core_axis_name="c", subcore_axis_name="s")

    @functools.partial(
        pl.kernel, mesh=mesh, out_type=jax.ShapeDtypeStruct((n_rows, w), src.dtype),
        scratch_types=[pltpu.VMEM((k, SC_WINDOW), jnp.int32), pltpu.VMEM((SC_WINDOW, w), src.dtype),
                       pltpu.SemaphoreType.DMA])
    def scatter(s_hbm, i_hbm, o_hbm, idx_v, rows_v, sem):
        base = (lax.axis_index("s") * n_cores + lax.axis_index("c")) * per_worker

        @pl.loop(0, per_worker // SC_WINDOW)
        def _(j):
            off = base + j * SC_WINDOW
            pltpu.sync_copy(i_hbm.at[:, pl.ds(off, SC_WINDOW)], idx_v)
            pltpu.sync_copy(s_hbm.at[pl.ds(off, SC_WINDOW)], rows_v)
            for kk in range(k):
                pltpu.async_copy(rows_v, o_hbm.at[idx_v.at[kk]], sem).wait()

    return scatter(src, idx)


def sc_gather_rows(table, idx):
    n = idx.shape[0]
    w = table.shape[1]
    n_cores, n_workers = _sc_workers()
    per_worker = n // n_workers
    mesh = plsc.VectorSubcoreMesh(core_axis_name="c", subcore_axis_name="s")

    @functools.partial(
        pl.kernel, mesh=mesh, out_type=jax.ShapeDtypeStruct((n, w), table.dtype),
        scratch_types=[pltpu.VMEM((SC_WINDOW,), jnp.int32), pltpu.VMEM((SC_WINDOW, w), table.dtype),
                       pltpu.SemaphoreType.DMA])
    def gather(t_hbm, i_hbm, o_hbm, idx_v, rows_v, sem):
        base = (lax.axis_index("s") * n_cores + lax.axis_index("c")) * per_worker

        @pl.loop(0, per_worker // SC_WINDOW)
        def _(j):
            off = base + j * SC_WINDOW
            pltpu.sync_copy(i_hbm.at[pl.ds(off, SC_WINDOW)], idx_v)
            pltpu.async_copy(t_hbm.at[idx_v], rows_v, sem).wait()
            pltpu.sync_copy(rows_v, o_hbm.at[pl.ds(off, SC_WINDOW)])

    return gather(table, idx)


def _expert_kernel(be_ref, nv_ref, xs_ref, wg_ref, wu_ref, wd_ref, ys_ref):
    nv = nv_ref[pl.program_id(0)]

    def ffn(x):
        x = _mx(x)
        hid = (_silu(jnp.dot(x, _mx(wg_ref[0]), preferred_element_type=F32))
               * jnp.dot(x, _mx(wu_ref[0]), preferred_element_type=F32))
        ys_ref[...] = _pack_pairs(jnp.dot(_mx(hid), _mx(wd_ref[0]), preferred_element_type=F32))

    @pl.when(nv == MOE_BLOCK)
    def _():
        ffn(_unpack_pairs(xs_ref[...]))

    @pl.when((nv > 0) & (nv < MOE_BLOCK))
    def _():
        x = _unpack_pairs(xs_ref[...])
        rows = lax.broadcasted_iota(jnp.int32, x.shape, 0)
        ffn(jnp.where(rows < nv, x, 0.0))

    @pl.when(nv == 0)
    def _():
        ys_ref[...] = jnp.zeros(ys_ref.shape, U32)


def moe_experts(xs, be, nv, layer, wg, wu, wd):
    n_rows, w = xs.shape
    _, _, d, f = wg.shape
    return pl.pallas_call(
        _expert_kernel,
        grid_spec=pltpu.PrefetchScalarGridSpec(
            num_scalar_prefetch=2,
            grid=(n_rows // MOE_BLOCK,),
            in_specs=[pl.BlockSpec((MOE_BLOCK, w), lambda b, be, nv: (b, 0)),
                      pl.BlockSpec((None, 1, d, f), lambda b, be, nv: (layer, be[b], 0, 0)),
                      pl.BlockSpec((None, 1, d, f), lambda b, be, nv: (layer, be[b], 0, 0)),
                      pl.BlockSpec((None, 1, f, d), lambda b, be, nv: (layer, be[b], 0, 0))],
            out_specs=pl.BlockSpec((MOE_BLOCK, w), lambda b, be, nv: (b, 0))),
        out_shape=jax.ShapeDtypeStruct((n_rows, w), U32),
        compiler_params=_cp("arbitrary"),
        name="moe_experts",
    )(be, nv, xs, wg, wu, wd)


def _combine_kernel(yg_ref, gk_ref, hp_ref, x_ref, sg_ref, su_ref, sd_ref, gpost_ref, ga2_ref, o_ref):
    h = _mx(_unpack_pairs(hp_ref[...]))
    hs = _silu(jnp.dot(h, sg_ref[...], preferred_element_type=F32)) * jnp.dot(h, su_ref[...], preferred_element_type=F32)
    f = jnp.dot(_mx(hs), sd_ref[...], preferred_element_type=F32)
    gk = gk_ref[...]
    for k in range(TOP_K):
        f = f + gk[:, k:k + 1] * _unpack_pairs(yg_ref[k])
    o_ref[...] = x_ref[...] + ga2_ref[0] * _rms(f, gpost_ref[...])


def moe_combine(yg, gk, hp, x, sg, su, sd, gpost, ga2, tiles_per_group):
    t, d = x.shape
    tm = min(MOE_ROW_TILE, t)
    w = hp.shape[1]
    row = lambda n: pl.BlockSpec((tm, n), lambda i: (i, 0))
    return pl.pallas_call(
        _combine_kernel,
        grid=(t // tm,),
        in_specs=[pl.BlockSpec((TOP_K, tm, w), lambda i: (0, i, 0)),
                  row(gk.shape[1]), row(w), row(d), _full(sg.shape), _full(su.shape), _full(sd.shape), _full((1, d)),
                  pl.BlockSpec((1, 1, d), lambda i: (i // tiles_per_group, 0, 0))],
        out_specs=row(d),
        out_shape=jax.ShapeDtypeStruct((t, d), F32),
        compiler_params=_cp("parallel"),
        name="moe_combine",
    )(yg, gk, hp, x, sg, su, sd, gpost, ga2)


def _reorder_w_in(w_in):
    c = np.cumsum((0,) + (GROUP_W, GROUP_W, GROUP_W, GROUP_W, GROUP_W, GROUP_W, 2 * SSD_STATE, 2 * SSD_STATE,
                          GROUP_W, 2 * N_HEADS, GROUP_W, GROUP_W, GROUP_W, GROUP_W, 2 * N_HEADS, 2 * N_HEADS))
    seg = lambda a, b: w_in[:, c[a]:c[b]]
    small = jnp.concatenate([seg(9, 10), seg(14, 15), seg(15, 16),
                             jnp.zeros((w_in.shape[0], LANES - 6 * N_HEADS), w_in.dtype)], axis=1)
    return jnp.concatenate([seg(0, 1), seg(5, 8), seg(10, 13), seg(1, 5), seg(8, 9), seg(13, 14), small], axis=1)


def kernel(x, c, ctx, c_ctx, w_mod, b_mod, g_pre_mix, g_post_mix, g_pre_ffn, g_post_ffn, w_in, w_out, lru_conv_w, lru_conv_b, lru_wa, lru_ba, lru_wx, lru_bx, lru_lambda, na_bias, ssd_conv_w, ssd_conv_b, ssd_a_log, ssd_dt_bias, ssd_d, ssd_norm, gdn_conv_w, gdn_a_log, gdn_dt_bias, gdn_norm, router_w, router_b, we_gate, we_up, we_down, ws_gate, ws_up, ws_down):
    bsz, seq, d = x.shape
    n_ctx = ctx.shape[1]
    depth = w_mod.shape[0]
    lat_tpg = seq // min(TOKEN_TILE, seq)
    ctx_tpg = max(bsz * n_ctx // TOKEN_TILE, 1)
    ctx_mpg = max(bsz * n_ctx // MOE_TILE, 1)

    cond = _pad_rows(jnp.concatenate([c, c_ctx[None, :]], axis=0))
    mod = modulation(cond, w_mod, b_mod).reshape(depth, SUBLANES, N_MOD, d)
    rope = rope_tables(seq)
    row = lambda v: v[None, :].astype(F32)

    xl = x.reshape(bsz * seq, d)
    xc = ctx.reshape(bsz * n_ctx, d)
    for l in range(depth):
        last = l == depth - 1
        m_lat = [mod[l, :bsz, k][:, None, :] for k in range(N_MOD)]
        m_ctx = [mod[l, bsz:bsz + 1, k][:, None, :] for k in range(N_MOD)]
        w_in_l = _reorder_w_in(w_in[l]).astype(MXU_DTYPE)
        conv = (_pad_rows(lru_conv_w[l]), row(lru_conv_b[l]), _pad_rows(ssd_conv_w[l]), row(ssd_conv_b[l]),
                _pad_rows(gdn_conv_w[l]))
        pc = in_projection(xc, row(g_pre_mix[l]), m_ctx[1], m_ctx[0], w_in_l, conv, None, n_ctx,
                           bsz * n_ctx // min(TOKEN_TILE, n_ctx))
        pl_ = in_projection(xl, row(g_pre_mix[l]), m_lat[1], m_lat[0], w_in_l, conv, rope, seq, lat_tpg)

        lru_p = lru_params(lru_wa[l], lru_ba[l], lru_wx[l], lru_bx[l], lru_lambda[l])
        a_cf, a_cb, a_st = lru_mixer(pc[P_AX], jnp.zeros((bsz, SUBLANES, GROUP_W), F32), bsz, *lru_p)
        a_lf, a_lb, _ = lru_mixer(pl_[P_AX], a_st, bsz, *lru_p)

        kc = pc[P_BK].reshape(bsz, n_ctx, GROUP_W)
        vc = pc[P_BV].reshape(bsz, n_ctx, GROUP_W)
        b_c = ctx_attention(pc[P_BQ].reshape(bsz, n_ctx, GROUP_W), kc, vc).reshape(bsz * n_ctx, GROUP_W)
        b_l = na_mixer(pl_[P_BQ], pl_[P_BK], pl_[P_BV], kc, vc, na_bias_slabs(na_bias[l]), bsz)

        ssd_p = ssd_params(ssd_a_log[l], ssd_dt_bias[l])
        c_cf, c_cb, c_st = ssd_mixer(pc[P_CX], pc[P_SM], jnp.zeros((bsz, 2, N_HEADS, SSD_STATE, HEAD_DIM), F32),
                                     bsz, *ssd_p)
        c_lf, c_lb, _ = ssd_mixer(pl_[P_CX], pl_[P_SM], c_st, bsz, *ssd_p)

        gdn_p = gdn_params(gdn_a_log[l], gdn_dt_bias[l])
        d_cf, d_cb, d_st = gdn_mixer(pc[P_DX], pc[P_SM], jnp.zeros((bsz, 2, N_HEADS, HEAD_DIM, HEAD_DIM), F32),
                                     bsz, *gdn_p)
        d_lf, d_lb, _ = gdn_mixer(pl_[P_DX], pl_[P_SM], d_st, bsz, *gdn_p)

        epi = (w_out[l].astype(MXU_DTYPE), row(g_post_mix[l]))
        epi_tail = (row(jnp.repeat(ssd_d[l], HEAD_DIM)), row(ssd_norm[l]), row(jnp.tile(gdn_norm[l], N_HEADS)), router_w[l])
        routed_w = (l, we_gate, we_up, we_down)
        shared_w = (ws_gate[l].astype(MXU_DTYPE), ws_up[l].astype(MXU_DTYPE), ws_down[l].astype(MXU_DTYPE),
                    row(g_post_ffn[l]))

        mix_l = (a_lf, a_lb, pl_[P_AG], b_l, c_lf, c_lb, pl_[P_CX], pl_[P_CZ], d_lf, d_lb, pl_[P_DZ])
        xl, hp, lg = out_projection(xl, mix_l, *epi, m_lat[2], row(g_pre_ffn[l]), m_lat[4], m_lat[3], *epi_tail, lat_tpg)
        gk, ek, pk, cnt = router_dispatch(lg, router_b[l])
        off, be, nv = moe_plan(cnt, bsz * seq)
        dest = moe_rows(off, ek, pk)
        xs = sc_scatter_rows(hp, dest, be.shape[0] * MOE_BLOCK)
        ys = moe_experts(xs, be, nv, *routed_w)
        yg = sc_gather_rows(ys, dest.reshape(-1)).reshape(TOP_K, bsz * seq, d // 2)
        xl = moe_combine(yg, gk, hp, xl, *shared_w, m_lat[5], seq // min(MOE_ROW_TILE, seq))
        if not last:
            mix_c = (a_cf, a_cb, pc[P_AG], b_c, c_cf, c_cb, pc[P_CX], pc[P_CZ], d_cf, d_cb, pc[P_DZ])
            xc, hp, lg = out_projection(xc, mix_c, *epi, m_ctx[2], row(g_pre_ffn[l]), m_ctx[4], m_ctx[3], *epi_tail, ctx_tpg)
            xc = moe_ffn(hp, router_gates(lg, router_b[l]), xc, *routed_w, *shared_w, m_ctx[5], ctx_mpg)
    return xl.reshape(bsz, seq, d)
```

```python
import functools
import math

import jax
import jax.numpy as jnp
import numpy as np
from jax import lax
from jax.experimental import pallas as pl
from jax.experimental.pallas import tpu as pltpu
from jax.experimental.pallas import tpu_sc as plsc

F32 = jnp.float32
MXU_DTYPE = jnp.bfloat16
HI = lax.Precision.HIGHEST

D_MODEL = 1024
GRID_W = 64
GROUP_W = 256
HEAD_DIM = 64
N_HEADS = 4
EPS = 1e-6
NEG_INF = -1e30
N_MOD = 6
LRU_C = 8.0
NA_WIN_ROWS = 8
NA_WIN_COLS = 16
SSD_STATE = 128
SSD_GROUPS = 2
ROPE_BASE = 10000.0
ROPE_AXIS_DIM = HEAD_DIM // 2
N_EXPERTS = 64
N_EXPERT_GROUPS = 8
TOPK_GROUPS = 4
TOP_K = 8
D_EXPERT = 256
ROUTED_SCALE = 2.5

LANES = 128
SUBLANES = 8
VMEM_LIMIT = 56 * 1024 * 1024

TOKEN_TILE = 512
LRU_CHUNK = 256
SSD_CHUNK = 128
GDN_CHUNK = 64
GDN_TILE = 256
GDN_SUB = 128
GDN_BASE = 16
MOE_TILE = 1024
MOE_EB = 4
MOE_BLOCK = 1024
MOE_ROW_TILE = 256
MOE_PLAN_TILE = 2048
SC_WINDOW = 128

P_WIDTHS = (256, 768, 768, 256, 256, 256, 256, 256, 256, 128)
(P_AX, P_CX, P_DX, P_AG, P_BQ, P_BK, P_BV, P_CZ, P_DZ, P_SM) = range(10)
P_CONV_GROUPS = 3
SM_DT, SM_BETA, SM_DECAY = 0, 8, 16


def _cp(*sem):
    return pltpu.CompilerParams(dimension_semantics=sem, vmem_limit_bytes=VMEM_LIMIT)


def _mx(x):
    return x.astype(MXU_DTYPE)


def _dot(a, b):
    return jnp.dot(_mx(a), _mx(b), preferred_element_type=F32)


def _dot_nt(a, b):
    return lax.dot_general(_mx(a), _mx(b), (((1,), (1,)), ((), ())), preferred_element_type=F32)


def _dot_tn(a, b):
    return lax.dot_general(_mx(a), _mx(b), (((0,), (0,)), ((), ())), preferred_element_type=F32)


def _dot_hi(a, b):
    return jnp.dot(a, b, preferred_element_type=F32, precision=HI)


def _sigmoid(x):
    return 1.0 / (1.0 + jnp.exp(-x))


def _silu(x):
    return x * _sigmoid(x)


def _softplus(x):
    return jnp.maximum(x, 0.0) + jnp.log1p(jnp.exp(-jnp.abs(x)))


def _gelu_tanh(x):
    return 0.5 * x * (1.0 + jnp.tanh(math.sqrt(2.0 / math.pi) * (x + 0.044715 * (x * x * x))))


def _rms(x, g):
    return x * lax.rsqrt(jnp.mean(x * x, axis=-1, keepdims=True) + EPS) * g


def _full(shape):
    n = len(shape)
    return pl.BlockSpec(shape, lambda *_: (0,) * n)


MOD_COLS = 1536


def _mod_kernel(c_ref, w_ref, b_ref, o_ref):
    o_ref[0] = _dot_hi(_silu(c_ref[...]), w_ref[0]) + b_ref[0]


def modulation(cond, w_mod, b_mod):
    depth, d, n = w_mod.shape
    return pl.pallas_call(
        _mod_kernel,
        grid=(depth, n // MOD_COLS),
        in_specs=[pl.BlockSpec((SUBLANES, d), lambda l, j: (0, 0)),
                  pl.BlockSpec((1, d, MOD_COLS), lambda l, j: (l, 0, j)),
                  pl.BlockSpec((1, 1, MOD_COLS), lambda l, j: (l, 0, j))],
        out_specs=pl.BlockSpec((1, SUBLANES, MOD_COLS), lambda l, j: (l, 0, j)),
        out_shape=jax.ShapeDtypeStruct((depth, SUBLANES, n), F32),
        compiler_params=_cp("parallel", "parallel"),
        name="modulation",
    )(cond, w_mod, b_mod.reshape(depth, 1, n))


def _inproj_kernel(*refs, tiles_per_seq, rope):
    (x_ref, xp_ref, xn_ref, g_ref, sc_ref, sh_ref, w_ref, lcw_ref, lcb_ref, scw_ref, scb_ref, gcw_ref) = refs[:12]
    cos_ref, sin_ref = (refs[12], refs[13]) if rope else (None, None)
    o_refs = refs[14:] if rope else refs[12:]
    i = pl.program_id(0)
    norm = lambda v: _rms(v, g_ref[...]) * (1.0 + sc_ref[0]) + sh_ref[0]
    p = _dot(norm(x_ref[...]), w_ref[...])
    n_conv = sum(P_WIDTHS[:P_CONV_GROUPS])
    ph = _dot(norm(jnp.concatenate([xp_ref[...], xn_ref[...]], axis=0)), w_ref[:, :n_conv])
    pos = i % tiles_per_seq
    prev = jnp.where(pos == 0, 0.0, ph[:SUBLANES])
    nxt = jnp.where(pos == tiles_per_seq - 1, 0.0, ph[SUBLANES:])
    row = lax.broadcasted_iota(jnp.int32, (SUBLANES, n_conv), 0)
    halo = jnp.where(row < 2, pltpu.roll(prev, 2, 0), jnp.where(row == 2, pltpu.roll(nxt, 2, 0), 0.0))
    c0, c1, c2 = GROUP_W, GROUP_W + 3 * GROUP_W, n_conv
    lru_u = _dwconv(p[:, :c0], halo[:, :c0], lcw_ref[...], lcb_ref[...])
    ssd_x = _silu(_dwconv(p[:, c0:c1], halo[:, c0:c1], scw_ref[...], scb_ref[...]))
    qkv = _silu(_dwconv(p[:, c1:c2], halo[:, c1:c2], gcw_ref[...]))
    qn = _l2norm_heads(qkv[:, :GROUP_W])
    kn = _l2norm_heads(qkv[:, GROUP_W:2 * GROUP_W])
    if rope:
        cos, sin = cos_ref[...], sin_ref[...]
        qn = qn * cos + _swap16(qn) * sin
        kn = kn * cos + _swap16(kn) * sin
    outs = [lru_u, ssd_x, jnp.concatenate([qn * (HEAD_DIM ** -0.5), kn, qkv[:, 2 * GROUP_W:]], axis=1)]
    off = n_conv
    for o_ref, w in zip(o_refs, P_WIDTHS):
        if outs:
            o_ref[...] = outs.pop(0)
        else:
            o_ref[...] = p[:, off:off + w].astype(o_ref.dtype)
            off += w


def in_projection(x, g, sc, sh, w, conv, rope, seq_len, tiles_per_group):
    t, d = x.shape
    tm = min(TOKEN_TILE, seq_len)
    tps = seq_len // tm
    hb = tm // SUBLANES
    vec = lambda i: (i // tiles_per_group, 0, 0)
    ins = [x, x, x, g, sc, sh, w, *conv]
    specs = [pl.BlockSpec((tm, d), lambda i: (i, 0)),
             pl.BlockSpec((SUBLANES, d), lambda i: (jnp.maximum(i * hb - 1, 0), 0)),
             pl.BlockSpec((SUBLANES, d), lambda i: (jnp.minimum((i + 1) * hb, t // SUBLANES - 1), 0)),
             _full((1, d)), pl.BlockSpec((1, 1, d), vec), pl.BlockSpec((1, 1, d), vec), _full(w.shape)]
    specs += [_full(a.shape) for a in conv]
    if rope is not None:
        ins += list(rope)
        specs += [pl.BlockSpec((tm, GROUP_W), lambda i: (i % tps, 0))] * 2
    return pl.pallas_call(
        functools.partial(_inproj_kernel, tiles_per_seq=tps, rope=rope is not None),
        grid=(t // tm,),
        in_specs=specs,
        out_specs=[pl.BlockSpec((tm, wd), lambda i: (i, 0)) for wd in P_WIDTHS],
        out_shape=[jax.ShapeDtypeStruct((t, wd), MXU_DTYPE if k in (P_BQ, P_BK, P_BV) else F32)
                   for k, wd in enumerate(P_WIDTHS)],
        compiler_params=_cp("parallel"),
        name="in_projection",
    )(*ins)


def _dwconv(x, halo, w, b=None):
    q = x.shape[0]
    row = lax.broadcasted_iota(jnp.int32, (SUBLANES, x.shape[1]), 0)

    def shifted(s, keep_rolled, edge):
        r = pltpu.roll(x, s % q, 0)
        if s > 0:
            return jnp.concatenate([jnp.where(keep_rolled, r[:SUBLANES], edge), r[SUBLANES:]], axis=0)
        return jnp.concatenate([r[:q - SUBLANES], jnp.where(keep_rolled, r[q - SUBLANES:], edge)], axis=0)

    xm2 = shifted(2, row >= 2, halo)
    xm1 = shifted(1, row >= 1, pltpu.roll(halo, SUBLANES - 1, 0))
    xp1 = shifted(-1, row < SUBLANES - 1, pltpu.roll(halo, SUBLANES - 3, 0))
    y = w[0:1] * xm2 + w[1:2] * xm1 + w[2:3] * x + w[3:4] * xp1
    return y if b is None else y + b


def _pad_rows(a, rows=SUBLANES):
    return jnp.concatenate([a, jnp.zeros((rows - a.shape[0],) + a.shape[1:], a.dtype)], axis=0)


def _chunk_specs(nc, q, c):
    fwd = pl.BlockSpec((q, c), lambda b, i: (b * nc + i, 0))
    bwd = pl.BlockSpec((q, c), lambda b, i: (b * nc + nc - 1 - i, 0))
    return fwd, bwd


def _lru_kernel(xf_ref, xb_ref, h0_ref, wg_ref, bg_ref, lam_ref,
                yf_ref, yb_ref, hfin_ref, af_s, bf_s, ab_s, bb_s, carry_s):
    i = pl.program_id(1)
    q = xf_ref.shape[0]

    @pl.when(i == 0)
    def _():
        carry_s[...] = h0_ref[0]

    def coeffs(x_ref, d, a_s, b_s):
        u = x_ref[...]
        g = _dot(u, wg_ref[:, 2 * GROUP_W * d:2 * GROUP_W * (d + 1)]) + bg_ref[:, 2 * GROUP_W * d:2 * GROUP_W * (d + 1)]
        r = _sigmoid(g[:, :GROUP_W])
        gate_in = _sigmoid(g[:, GROUP_W:])
        log_a = -LRU_C * r * _softplus(-lam_ref[d:d + 1, :])
        a_s[...] = jnp.exp(log_a)
        b_s[...] = jnp.sqrt(1.0 - jnp.exp(2.0 * log_a)) * (gate_in * u)

    coeffs(xf_ref, 0, af_s, bf_s)
    coeffs(xb_ref, 1, ab_s, bb_s)

    ng = q // SUBLANES
    row = lax.broadcasted_iota(jnp.int32, (SUBLANES, GROUP_W), 0)

    def body(g, hs):
        h_f, h_b = hs
        i0 = pl.multiple_of(g * SUBLANES, SUBLANES)
        a = af_s[pl.ds(i0, SUBLANES), :]
        b = bf_s[pl.ds(i0, SUBLANES), :]
        for s in (1, 2, 4):
            m = row >= s
            b = jnp.where(m, a * pltpu.roll(b, s, 0) + b, b)
            a = jnp.where(m, a * pltpu.roll(a, s, 0), a)
        h = b + a * h_f
        yf_ref[pl.ds(i0, SUBLANES), :] = h
        h_f = h[SUBLANES - 1:SUBLANES, :]
        j0 = pl.multiple_of((ng - 1 - g) * SUBLANES, SUBLANES)
        a = ab_s[pl.ds(j0, SUBLANES), :]
        b = bb_s[pl.ds(j0, SUBLANES), :]
        for s in (1, 2, 4):
            m = row < SUBLANES - s
            b = jnp.where(m, a * pltpu.roll(b, SUBLANES - s, 0) + b, b)
            a = jnp.where(m, a * pltpu.roll(a, SUBLANES - s, 0), a)
        h = b + a * h_b
        yb_ref[pl.ds(j0, SUBLANES), :] = h
        return h_f, h[0:1, :]

    h_f, h_b = lax.fori_loop(0, ng, body, (carry_s[0:1, :], carry_s[1:2, :]))
    carry_s[0:1, :] = h_f
    carry_s[1:2, :] = h_b

    @pl.when(i == pl.num_programs(1) - 1)
    def _():
        hfin_ref[0] = carry_s[...]


def _block_diag(w):
    h, a, b = w.shape
    return jnp.einsum('hij,hg->higj', w, jnp.eye(h, dtype=w.dtype)).reshape(h * a, h * b)


def lru_params(wa, ba, wx, bx, lam):
    wg = jnp.concatenate([_block_diag(wa[0]), _block_diag(wx[0]), _block_diag(wa[1]), _block_diag(wx[1])], axis=1)
    bg = jnp.concatenate([ba[0], bx[0], ba[1], bx[1]])[None, :]
    return wg.astype(MXU_DTYPE), bg, _pad_rows(lam)


def lru_mixer(x, h0, bsz, wg, bg, lam):
    t, c = x.shape
    s = t // bsz
    q = min(LRU_CHUNK, s)
    nc = s // q
    xf, xb = _chunk_specs(nc, q, c)
    st = pl.BlockSpec((1, SUBLANES, c), lambda b, i: (b, 0, 0))
    return pl.pallas_call(
        _lru_kernel,
        grid=(bsz, nc),
        in_specs=[xf, xb, st, _full(wg.shape), _full(bg.shape), _full(lam.shape)],
        out_specs=[xf, xb, st],
        out_shape=[jax.ShapeDtypeStruct((t, c), F32), jax.ShapeDtypeStruct((t, c), F32),
                   jax.ShapeDtypeStruct((bsz, SUBLANES, c), F32)],
        scratch_shapes=[pltpu.VMEM((q, c), F32)] * 4 + [pltpu.VMEM((SUBLANES, c), F32)],
        compiler_params=_cp("parallel", "arbitrary"),
        name="lru_mixer",
    )(x, x, h0, wg, bg, lam)


NA_KEYS = NA_WIN_ROWS * GRID_W
NA_ROW_BLOCK = 8


def na_bias_slabs(table):
    qc = np.arange(GRID_W)[:, None]
    kc = np.arange(GRID_W)[None, :]
    win0 = np.clip(qc - NA_WIN_COLS // 2, 0, GRID_W - NA_WIN_COLS)
    ok = (kc >= win0) & (kc < win0 + NA_WIN_COLS)
    dc = np.clip(kc - qc + NA_WIN_COLS - 1, 0, 2 * NA_WIN_COLS - 2)
    dr = np.arange(NA_WIN_ROWS)[:, None] + np.arange(NA_WIN_ROWS)[None, :]
    b = table.astype(F32)[:, dr][:, :, :, dc]
    b = jnp.where(ok[None, None, None], b, NEG_INF)
    h = table.shape[0]
    return b.transpose(0, 1, 3, 2, 4).reshape(h, NA_WIN_ROWS, GRID_W, NA_KEYS)


def _na_span_start(j, rows):
    return jnp.clip(j * NA_ROW_BLOCK - NA_WIN_ROWS // 2, 0, rows - (NA_ROW_BLOCK + NA_WIN_ROWS - 1))


def _na_kernel(q_ref, kw_ref, vw_ref, kc_ref, vc_ref, slab_ref, o_ref, *, rows):
    j = pl.program_id(1)
    ustart = _na_span_start(j, rows)
    q = q_ref[...] * (HEAD_DIM ** -0.5)
    kc, vc = kc_ref[0], vc_ref[0]
    heads = [slice(h * HEAD_DIM, (h + 1) * HEAD_DIM) for h in range(N_HEADS)]
    qrows = [slice(i * GRID_W, (i + 1) * GRID_W) for i in range(NA_ROW_BLOCK)]
    kws, vws, offs = [], [], []
    for i in range(NA_ROW_BLOCK):
        r = j * NA_ROW_BLOCK + i
        r0 = jnp.clip(r - NA_WIN_ROWS // 2, 0, rows - NA_WIN_ROWS)
        start = pl.multiple_of((r0 - ustart) * GRID_W, GRID_W)
        kws.append(kw_ref[pl.ds(start, NA_KEYS), :])
        vws.append(vw_ref[pl.ds(start, NA_KEYS), :])
        offs.append(r0 - r + NA_WIN_ROWS - 1)
    s_ctx = [_dot_nt(q[:, sl], kc[:, sl]) for sl in heads]
    s_loc = [[_dot_nt(q[qr, sl], kws[i][:, sl]) + slab_ref[h, offs[i]] for h, sl in enumerate(heads)]
             for i, qr in enumerate(qrows)]
    m = [[jnp.maximum(jnp.max(s_loc[i][h], axis=-1, keepdims=True), jnp.max(s_ctx[h][qr], axis=-1, keepdims=True))
          for h in range(N_HEADS)] for i, qr in enumerate(qrows)]
    p_loc = [[jnp.exp(s_loc[i][h] - m[i][h]) for h in range(N_HEADS)] for i in range(NA_ROW_BLOCK)]
    p_ctx = [jnp.exp(s_ctx[h] - jnp.concatenate([m[i][h] for i in range(NA_ROW_BLOCK)], axis=0))
             for h in range(N_HEADS)]
    o_ctx = [_dot(p_ctx[h], vc[:, sl]) for h, sl in enumerate(heads)]
    rows_out = []
    for i, qr in enumerate(qrows):
        outs = []
        for h, sl in enumerate(heads):
            den = jnp.sum(p_loc[i][h], axis=-1, keepdims=True) + jnp.sum(p_ctx[h][qr], axis=-1, keepdims=True)
            outs.append((_dot(p_loc[i][h], vws[i][:, sl]) + o_ctx[h][qr]) / den)
        rows_out.append(jnp.concatenate(outs, axis=1))
    o_ref[...] = jnp.concatenate(rows_out, axis=0)


def na_mixer(q, k, v, kc, vc, slabs, bsz):
    t, c = q.shape
    s = t // bsz
    rows = s // GRID_W
    n_ctx = kc.shape[1]
    span = (NA_ROW_BLOCK + NA_WIN_ROWS - 1) * GRID_W

    def win(b, j):
        return ((b * rows + _na_span_start(j, rows)) * GRID_W, 0)

    wspec = pl.BlockSpec((pl.Element(span), pl.Element(c)), win)
    cspec = pl.BlockSpec((1, n_ctx, c), lambda b, j: (b, 0, 0))
    qspec = pl.BlockSpec((NA_ROW_BLOCK * GRID_W, c), lambda b, j: (b * (rows // NA_ROW_BLOCK) + j, 0))
    return pl.pallas_call(
        functools.partial(_na_kernel, rows=rows),
        grid=(bsz, rows // NA_ROW_BLOCK),
        in_specs=[qspec, wspec, wspec, cspec, cspec, _full(slabs.shape)],
        out_specs=qspec,
        out_shape=jax.ShapeDtypeStruct((t, c), F32),
        compiler_params=_cp("parallel", "arbitrary"),
        name="na_mixer",
    )(q, k, v, kc, vc, slabs)


def _ctx_attn_kernel(q_ref, k_ref, v_ref, o_ref):
    q = q_ref[0] * (HEAD_DIM ** -0.5)
    k, v = k_ref[0], v_ref[0]
    outs = []
    for h in range(N_HEADS):
        sl = slice(h * HEAD_DIM, (h + 1) * HEAD_DIM)
        s = _dot_nt(q[:, sl], k[:, sl])
        p = jnp.exp(s - jnp.max(s, axis=-1, keepdims=True))
        outs.append(_dot(p, v[:, sl]) / jnp.sum(p, axis=-1, keepdims=True))
    o_ref[0] = jnp.concatenate(outs, axis=1)


def ctx_attention(q, k, v):
    spec = pl.BlockSpec((1,) + q.shape[1:], lambda b: (b, 0, 0))
    return pl.pallas_call(
        _ctx_attn_kernel,
        grid=(q.shape[0],),
        in_specs=[spec, spec, spec],
        out_specs=spec,
        out_shape=jax.ShapeDtypeStruct(q.shape, F32),
        compiler_params=_cp("parallel"),
        name="ctx_attention",
    )(q, k, v)


def _small_vec(vals, off):
    v = jnp.zeros((LANES,), F32).at[off:off + 2 * N_HEADS].set(vals.reshape(-1).astype(F32))
    return v[None, :]


def _lane_mask(off):
    lane = lax.broadcasted_iota(jnp.int32, (1, LANES), 1)
    return (lane >= off) & (lane < off + 2 * N_HEADS)


def _tri_masks(q):
    rowi = lax.broadcasted_iota(jnp.int32, (q, q), 0)
    coli = lax.broadcasted_iota(jnp.int32, (q, q), 1)
    return rowi, coli


def _ssd_kernel(xf_ref, xb_ref, sf_ref, sb_ref, h0_ref, dtb_ref, alog_ref,
                yf_ref, yb_ref, hfin_ref, state_s):
    i = pl.program_id(1)
    q = xf_ref.shape[0]

    @pl.when(i == 0)
    def _():
        state_s[...] = h0_ref[0]

    rowi, coli = _tri_masks(q)
    a_neg = jnp.where(_lane_mask(SM_DT), -jnp.exp(alog_ref[...]), 0.0)

    chains = [(d, h) for d in range(2) for h in range(N_HEADS)]
    per_head = N_HEADS // SSD_GROUPS
    scores, xdt, c_in, b_out, e_last = [], [], [], [], []
    for d, (x_ref, sm_ref) in enumerate(((xf_ref, sf_ref), (xb_ref, sb_ref))):
        xbc = x_ref[...]
        dt = _softplus(sm_ref[...] + dtb_ref[...])
        keep = (rowi >= coli) if d == 0 else (rowi <= coli)
        acum = _dot_tri(keep, dt * a_neg)
        acum_t = acum.T
        last = acum[q - 1:q, :] if d == 0 else acum[0:1, :]
        dec_end = jnp.exp(last - acum)
        e_acum = jnp.exp(acum)
        e_end = jnp.exp(last)
        bgs = [xbc[:, GROUP_W + SSD_STATE * g:GROUP_W + SSD_STATE * (g + 1)] for g in range(SSD_GROUPS)]
        cgs = [xbc[:, GROUP_W + SSD_STATE * (SSD_GROUPS + g):GROUP_W + SSD_STATE * (SSD_GROUPS + g + 1)]
               for g in range(SSD_GROUPS)]
        cbt = [_dot_nt(cg, bg) for cg, bg in zip(cgs, bgs)]
        for h in range(N_HEADS):
            g = h // per_head
            ln = SM_DT + N_HEADS * d + h
            lmat = jnp.exp(jnp.where(keep, acum[:, ln:ln + 1] - acum_t[ln:ln + 1, :], NEG_INF))
            scores.append(cbt[g] * lmat)
            xdt.append(xbc[:, h * HEAD_DIM:(h + 1) * HEAD_DIM] * dt[:, ln:ln + 1])
            c_in.append(cgs[g] * e_acum[:, ln:ln + 1])
            b_out.append(bgs[g] * dec_end[:, ln:ln + 1])
            e_last.append(e_end[:, ln:ln + 1])
    states = [state_s[d, h] for d, h in chains]
    y_diag = [_dot(s, x) for s, x in zip(scores, xdt)]
    y_off = [_dot(c, st) for c, st in zip(c_in, states)]
    upd = [_dot_tn(b, x) for b, x in zip(b_out, xdt)]
    for n, (d, h) in enumerate(chains):
        state_s[d, h] = states[n] * e_last[n] + upd[n]
    ys = [a + b for a, b in zip(y_diag, y_off)]
    yf_ref[...] = jnp.concatenate(ys[:N_HEADS], axis=1)
    yb_ref[...] = jnp.concatenate(ys[N_HEADS:], axis=1)

    @pl.when(i == pl.num_programs(1) - 1)
    def _():
        hfin_ref[0] = state_s[...]


def ssd_params(a_log, dt_bias):
    return _small_vec(dt_bias, SM_DT), _small_vec(a_log, SM_DT)


def ssd_mixer(xbc, sm, h0, bsz, dtb, alog):
    t, c = xbc.shape
    s = t // bsz
    q = min(SSD_CHUNK, s)
    nc = s // q
    xf, xb = _chunk_specs(nc, q, c)
    sf, sb = _chunk_specs(nc, q, LANES)
    yf, yb = _chunk_specs(nc, q, GROUP_W)
    st = pl.BlockSpec((1,) + h0.shape[1:], lambda b, i: (b, 0, 0, 0, 0))
    y_shape = jax.ShapeDtypeStruct((t, GROUP_W), F32)
    return pl.pallas_call(
        _ssd_kernel,
        grid=(bsz, nc),
        in_specs=[xf, xb, sf, sb, st, _full(dtb.shape), _full(alog.shape)],
        out_specs=[yf, yb, st],
        out_shape=[y_shape, y_shape, jax.ShapeDtypeStruct(h0.shape, F32)],
        scratch_shapes=[pltpu.VMEM(h0.shape[1:], F32)],
        compiler_params=_cp("parallel", "arbitrary"),
        name="ssd_mixer",
    )(xbc, xbc, sm, sm, h0, dtb, alog)


def rope_tables(seq):
    t = jnp.arange(seq)
    row = (t // GRID_W).astype(F32)
    col = (t % GRID_W).astype(F32)
    inv = ROPE_BASE ** (-jnp.arange(0, ROPE_AXIS_DIM, 2, dtype=F32) / ROPE_AXIS_DIM)
    ar, ac = row[:, None] * inv, col[:, None] * inv
    cos = jnp.concatenate([jnp.cos(ar), jnp.cos(ar), jnp.cos(ac), jnp.cos(ac)], axis=1)
    sin = jnp.concatenate([-jnp.sin(ar), jnp.sin(ar), -jnp.sin(ac), jnp.sin(ac)], axis=1)
    return jnp.tile(cos, (1, N_HEADS)), jnp.tile(sin, (1, N_HEADS))


def _swap16(x):
    lane = lax.broadcasted_iota(jnp.int32, x.shape, 1)
    half = ROPE_AXIS_DIM // 2
    return jnp.where((lane & (ROPE_AXIS_DIM - 1)) < half,
                     pltpu.roll(x, x.shape[1] - half, 1), pltpu.roll(x, half, 1))


def _head_sums(sq):
    c = sq.shape[1]
    li = lax.broadcasted_iota(jnp.int32, (c, c), 0)
    lj = lax.broadcasted_iota(jnp.int32, (c, c), 1)
    sh = HEAD_DIM.bit_length() - 1
    ones = _mx(((li >> sh) == (lj >> sh)).astype(F32))
    hi = _mx(sq)
    lo = _mx(sq - hi.astype(F32))
    return jnp.dot(hi, ones, preferred_element_type=F32) + jnp.dot(lo, ones, preferred_element_type=F32)


def _l2norm_heads(x):
    return x * lax.rsqrt(_head_sums(x * x) + EPS)


def _head_columns(x, off):
    li = lax.broadcasted_iota(jnp.int32, (LANES, N_HEADS * HEAD_DIM), 0)
    lj = lax.broadcasted_iota(jnp.int32, (LANES, N_HEADS * HEAD_DIM), 1)
    pick = _mx((li == off + (lj >> (HEAD_DIM.bit_length() - 1))).astype(F32))
    hi = _mx(x)
    lo = _mx(x - hi.astype(F32))
    return jnp.dot(hi, pick, preferred_element_type=F32) + jnp.dot(lo, pick, preferred_element_type=F32)


def _dot_tri(mask, x):
    m = _mx(mask.astype(F32))
    x1 = _mx(x)
    r1 = x - x1.astype(F32)
    x2 = _mx(r1)
    x3 = _mx(r1 - x2.astype(F32))
    return (jnp.dot(m, x1, preferred_element_type=F32) + jnp.dot(m, x2, preferred_element_type=F32)
            + jnp.dot(m, x3, preferred_element_type=F32))


def _same_block(rowi, coli, n):
    sh = n.bit_length() - 1
    return (rowi >> sh) == (coli >> sh)


def _solve_unit_tri(a_list, rhs_list, rowi, coli, chunk):
    mm = lambda x, y: jnp.dot(x, y, preferred_element_type=F32)
    eye = (rowi == coli).astype(F32)
    in_base = _same_block(rowi, coli, GDN_BASE)
    base = [_mx(jnp.where(in_base, a, 0.0)) for a in a_list]
    ts = [jnp.where(in_base, eye - a, 0.0) for a in a_list]
    ps = [_mx(mm(b, b)) for b in base]
    ts = [t + mm(_mx(t), p) for t, p in zip(ts, ps)]
    n = 4
    while n < GDN_BASE:
        ps = [_mx(mm(p, p)) for p in ps]
        ts = [t + mm(_mx(t), p) for t, p in zip(ts, ps)]
        n *= 2
    n = GDN_BASE
    while 2 * n < chunk:
        inner = _same_block(rowi, coli, 2 * n) & jnp.logical_not(_same_block(rowi, coli, n))
        offs = [_mx(jnp.where(inner, a, 0.0)) for a in a_list]
        tb = [_mx(t) for t in ts]
        ms = [_mx(mm(t, off)) for t, off in zip(tb, offs)]
        ts = [t - mm(m, t_b) for t, m, t_b in zip(ts, ms, tb)]
        n *= 2
    outer = jnp.logical_not(_same_block(rowi, coli, n))
    offs = [_mx(jnp.where(outer, a, 0.0)) for a in a_list]
    tb = [_mx(t) for t in ts]
    ys = [mm(t, _mx(r)) for t, r in zip(tb, rhs_list)]
    zs = [_mx(mm(off, _mx(y))) for off, y in zip(offs, ys)]
    return [y - mm(t, z) for y, t, z in zip(ys, tb, zs)]


def _gdn_kernel(xf_ref, xb_ref, sf_ref, sb_ref, s0_ref, alog_ref, dtb_ref, of_ref, ob_ref, sfin_ref, state_s):
    i = pl.program_id(1)
    tq = xf_ref.shape[0]
    ck = min(GDN_CHUNK, tq)
    nck = tq // ck

    @pl.when(i == 0)
    def _():
        state_s[...] = s0_ref[0]

    sub = min(GDN_SUB, tq)
    nsub = tq // sub
    rowt, colt = _tri_masks(tq)
    in_chunk_t = _same_block(rowt, colt, ck)
    rowi, coli = _tri_masks(sub)
    in_chunk = _same_block(rowi, coli, ck)
    a_neg = jnp.where(_lane_mask(SM_DECAY), -jnp.exp(alog_ref[...]), 0.0)

    a_list, rhs_list, qkm, qg, kd, e_last = [], [], [], [], [], []
    for d, (x_ref, sm_ref) in enumerate(((xf_ref, sf_ref), (xb_ref, sb_ref))):
        qkv = x_ref[...]
        qn, kn, v = qkv[:, :GROUP_W], qkv[:, GROUP_W:2 * GROUP_W], qkv[:, 2 * GROUP_W:]
        sm = sm_ref[...]
        beta = _sigmoid(sm)
        keep_t = in_chunk_t & ((rowt >= colt) if d == 0 else (rowt <= colt))
        keep = in_chunk & ((rowi >= coli) if d == 0 else (rowi <= coli))
        strict = in_chunk & ((rowi > coli) if d == 0 else (rowi < coli))
        gc = _dot_tri(keep_t, _softplus(sm + dtb_ref[...]) * a_neg)
        gc_t = gc.T
        edge = ck - 1 if d == 0 else 0
        last = jnp.concatenate([jnp.broadcast_to(gc[c * ck + edge:c * ck + edge + 1, :], (ck, LANES))
                                for c in range(nck)], axis=0)
        e_last.append(jnp.exp(last))
        beta_w = _head_columns(beta, SM_BETA + N_HEADS * d)
        e_gc_w = _head_columns(jnp.exp(gc), SM_DECAY + N_HEADS * d)
        e_end_w = _head_columns(jnp.exp(last - gc), SM_DECAY + N_HEADS * d)
        kb_w = kn * beta_w
        vb_w = v * beta_w
        kbe_w = kb_w * e_gc_w
        qg_w = qn * e_gc_w
        kd_w = kn * e_end_w
        qn_m, kn_m, kb_wm = _mx(qn), _mx(kn), _mx(kb_w)
        for h in range(N_HEADS):
            sl = slice(h * HEAD_DIM, (h + 1) * HEAD_DIM)
            lg = SM_DECAY + N_HEADS * d + h
            rhs = jnp.concatenate([vb_w[:, sl], kbe_w[:, sl]], axis=1)
            qg.append(qg_w[:, sl])
            kd.append(kd_w[:, sl])
            qh_m, kh_m, kb_m = qn_m[:, sl], kn_m[:, sl], kb_wm[:, sl]
            for s in range(nsub):
                rs = slice(s * sub, (s + 1) * sub)
                decay = jnp.exp(jnp.where(keep, gc[rs, lg:lg + 1] - gc_t[lg:lg + 1, rs], NEG_INF))
                a_list.append(jnp.where(strict, _dot_nt(kb_m[rs], kh_m[rs]) * decay, 0.0))
                rhs_list.append(rhs[rs])
                qkm.append(_dot_nt(qh_m[rs], kh_m[rs]) * decay)
    sols = _solve_unit_tri(a_list, rhs_list, rowi, coli, ck)
    sols = [jnp.concatenate(sols[n * nsub:(n + 1) * nsub], axis=0) for n in range(2 * N_HEADS)]

    chains = [(d, h) for d in range(2) for h in range(N_HEADS)]
    states = [state_s[d, h] for d, h in chains]
    v_new = [[None] * nck for _ in chains]
    o_st = [[None] * nck for _ in chains]
    for step in range(nck):
        rows = [slice((step if d == 0 else nck - 1 - step) * ck, (step if d == 0 else nck - 1 - step) * ck + ck)
                for d, _ in chains]
        ms = [_dot(jnp.concatenate([sols[n][r, HEAD_DIM:], qg[n][r]], axis=0), states[n])
              for n, r in enumerate(rows)]
        for n, (d, _) in enumerate(chains):
            c = step if d == 0 else nck - 1 - step
            v_new[n][c] = sols[n][rows[n], :HEAD_DIM] - ms[n][:ck]
            o_st[n][c] = ms[n][ck:]
        ups = [_dot_tn(kd[n][r], v_new[n][step if chains[n][0] == 0 else nck - 1 - step])
               for n, r in enumerate(rows)]
        for n, (d, h) in enumerate(chains):
            lg = SM_DECAY + N_HEADS * d + h
            states[n] = states[n] * e_last[d][rows[n].start:rows[n].start + 1, lg:lg + 1] + ups[n]
    cps = sub // ck
    outs = [jnp.concatenate(o_st[n], axis=0)
            + jnp.concatenate([_dot(qkm[n * nsub + s], jnp.concatenate(v_new[n][s * cps:(s + 1) * cps], axis=0))
                               for s in range(nsub)], axis=0)
            for n in range(len(chains))]
    of_ref[...] = jnp.concatenate(outs[:N_HEADS], axis=1)
    ob_ref[...] = jnp.concatenate(outs[N_HEADS:], axis=1)
    for n, (d, h) in enumerate(chains):
        state_s[d, h] = states[n]

    @pl.when(i == pl.num_programs(1) - 1)
    def _():
        sfin_ref[0] = state_s[...]


def gdn_params(a_log, dt_bias):
    return _small_vec(a_log, SM_DECAY), _small_vec(dt_bias, SM_DECAY)


def gdn_mixer(qkv, sm, s0, bsz, alog, dtb):
    t, c = qkv.shape
    s = t // bsz
    q = min(GDN_TILE, s)
    nc = s // q
    xf, xb = _chunk_specs(nc, q, c)
    sf, sb = _chunk_specs(nc, q, LANES)
    of, ob = _chunk_specs(nc, q, GROUP_W)
    st = pl.BlockSpec((1,) + s0.shape[1:], lambda b, i: (b, 0, 0, 0, 0))
    o_shape = jax.ShapeDtypeStruct((t, GROUP_W), F32)
    return pl.pallas_call(
        _gdn_kernel,
        grid=(bsz, nc),
        in_specs=[xf, xb, sf, sb, st, _full(alog.shape), _full(dtb.shape)],
        out_specs=[of, ob, st],
        out_shape=[o_shape, o_shape, jax.ShapeDtypeStruct(s0.shape, F32)],
        scratch_shapes=[pltpu.VMEM(s0.shape[1:], F32)],
        compiler_params=_cp("parallel", "arbitrary"),
        name="gdn_mixer",
    )(qkv, qkv, sm, sm, s0, alog, dtb)


def _split_hi_lo(a):
    hi = _mx(a)
    return hi, _mx(a - hi.astype(F32))


def _outproj_kernel(x_ref, ahf_ref, ahb_ref, ag_ref, bo_ref, cyf_ref, cyb_ref, cxc_ref, cz_ref,
                    dof_ref, dob_ref, dz_ref, wout_ref, gpost_ref, ga1_ref, gpre_ref, sc2_ref, sh2_ref,
                    dskip_ref, cnorm_ref, dnorm_ref, rhi_ref, rlo_ref, xo_ref, hp_ref, lg_ref):
    m_a = (ahf_ref[...] + ahb_ref[...]) * _gelu_tanh(ag_ref[...])
    y_c = (cyf_ref[...] + cyb_ref[...] + cxc_ref[...] * dskip_ref[...]) * _silu(cz_ref[...])
    m_c = _rms(y_c, cnorm_ref[...])
    o_d = dof_ref[...] + dob_ref[...]
    m_d = o_d * lax.rsqrt(_head_sums(o_d * o_d) * (1.0 / HEAD_DIM) + EPS) * dnorm_ref[...] * _silu(dz_ref[...])
    mix = jnp.concatenate([_mx(m_a), _mx(bo_ref[...]), _mx(m_c), _mx(m_d)], axis=1)
    ml = jnp.dot(mix, wout_ref[...], preferred_element_type=F32)
    x_new = x_ref[...] + ga1_ref[0] * _rms(ml, gpost_ref[...])
    xo_ref[...] = x_new
    h2 = _rms(x_new, gpre_ref[...]) * (1.0 + sc2_ref[0]) + sh2_ref[0]
    hi, lo = _split_hi_lo(h2)
    hp_ref[...] = _pack_pairs(h2)
    rhi = rhi_ref[...]
    lg_ref[...] = (jnp.dot(hi, rhi, preferred_element_type=F32) + jnp.dot(lo, rhi, preferred_element_type=F32)
                   + jnp.dot(hi, rlo_ref[...], preferred_element_type=F32))


def out_projection(x, mixers, w_out, gpost, ga1, gpre, sc2, sh2, dskip, cnorm, dnorm, router_w, tiles_per_group):
    t, d = x.shape
    tm = min(TOKEN_TILE, t)
    vec = lambda i: (i // tiles_per_group, 0, 0)
    row = lambda w: pl.BlockSpec((tm, w), lambda i: (i, 0))
    ne = LANES
    rhi, rlo = _split_hi_lo(jnp.pad(router_w.astype(F32), ((0, 0), (0, ne - router_w.shape[1]))))
    return pl.pallas_call(
        _outproj_kernel,
        grid=(t // tm,),
        in_specs=[row(d)] + [row(GROUP_W)] * 11
                 + [_full(w_out.shape), _full((1, d)), pl.BlockSpec((1, 1, d), vec), _full((1, d)),
                    pl.BlockSpec((1, 1, d), vec), pl.BlockSpec((1, 1, d), vec),
                    _full((1, GROUP_W)), _full((1, GROUP_W)), _full((1, GROUP_W)), _full(rhi.shape), _full(rlo.shape)],
        out_specs=[row(d), row(d // 2), row(ne)],
        out_shape=[jax.ShapeDtypeStruct((t, d), F32), jax.ShapeDtypeStruct((t, d // 2), jnp.uint32),
                   jax.ShapeDtypeStruct((t, ne), F32)],
        compiler_params=_cp("parallel"),
        name="out_projection",
    )(x, *mixers, w_out, gpost, ga1, gpre, sc2, sh2, dskip, cnorm, dnorm, rhi, rlo)


def _rank_before(vals, idx, count, stride):
    rank = jnp.zeros(vals.shape, jnp.int32)
    for j in range(count):
        other = vals[j * stride:j * stride + 1, :]
        ahead = (other > vals) | ((other == vals) & (idx > j))
        rank = rank + ahead.astype(jnp.int32)
    return rank


def _xor_partner(x, row, s):
    n = x.shape[0]
    return jnp.where((row & s) == 0, pltpu.roll(x, n - s, 0), pltpu.roll(x, s, 0))


def _route(logits, router_b):
    ne = N_EXPERTS
    gsz = ne // N_EXPERT_GROUPS
    scores = _sigmoid(logits.T[:ne, :])
    tm = scores.shape[1]
    biased = scores + router_b
    row = lax.broadcasted_iota(jnp.int32, (ne, tm), 0)
    m1, m2 = biased, jnp.full((ne, tm), -jnp.inf, F32)
    s = 1
    while s < gsz:
        o1, o2 = _xor_partner(m1, row, s), _xor_partner(m2, row, s)
        m2 = jnp.maximum(jnp.minimum(m1, o1), jnp.maximum(m2, o2))
        m1 = jnp.maximum(m1, o1)
        s *= 2
    gidx = row >> (gsz.bit_length() - 1)
    group_ok = _rank_before(m1 + m2, gidx, N_EXPERT_GROUPS, gsz) < TOPK_GROUPS
    choice = jnp.where(group_ok, biased, -jnp.inf)
    rank = _rank_before(choice, row, ne, 1)
    gate = jnp.where(rank < TOP_K, scores, 0.0)
    gate = gate / jnp.sum(gate, axis=0, keepdims=True) * ROUTED_SCALE
    return gate, rank, row


def _to_token_major(x):
    n, tm = x.shape
    return jnp.concatenate([x, jnp.zeros((LANES - n, tm), x.dtype)], axis=0).T


def _router_kernel(lg_ref, rb_ref, gate_ref):
    gate, _, _ = _route(lg_ref[...], rb_ref[...])
    gate_ref[...] = _to_token_major(gate)


def _router_dispatch_kernel(lg_ref, rb_ref, gk_ref, ek_ref, pk_ref, cnt_ref, carry_s):
    i = pl.program_id(0)

    @pl.when(i == 0)
    def _():
        carry_s[...] = jnp.zeros(carry_s.shape, F32)

    gate, rank, row = _route(lg_ref[...], rb_ref[...])
    tm = gate.shape[1]
    picked = (rank < TOP_K).astype(F32)
    before = lax.broadcasted_iota(jnp.int32, (tm, tm), 0) < lax.broadcasted_iota(jnp.int32, (tm, tm), 1)
    pos = _dot(picked, before.astype(F32)) + carry_s[:, 0:1]
    carry_s[...] = carry_s[...] + jnp.sum(picked, axis=1, keepdims=True)
    gk, ek, pk = [], [], []
    for k in range(TOP_K):
        sel = rank == k
        gk.append(jnp.sum(jnp.where(sel, gate, 0.0), axis=0, keepdims=True))
        ek.append(jnp.sum(jnp.where(sel, row, 0), axis=0, keepdims=True))
        pk.append(jnp.sum(jnp.where(sel, pos, 0.0), axis=0, keepdims=True))
    gk_ref[...] = _to_token_major(jnp.concatenate(gk, axis=0))
    ek_ref[...] = jnp.concatenate(ek, axis=0)
    pk_ref[...] = jnp.concatenate(pk, axis=0).astype(jnp.int32)

    @pl.when(i == pl.num_programs(0) - 1)
    def _():
        cnt_ref[...] = carry_s[...].astype(jnp.int32)


def router_dispatch(logits, router_b):
    t, w = logits.shape
    tm = min(TOKEN_TILE, t)
    return pl.pallas_call(
        _router_dispatch_kernel,
        grid=(t // tm,),
        in_specs=[pl.BlockSpec((tm, w), lambda i: (i, 0)), _full((N_EXPERTS, 1))],
        out_specs=[pl.BlockSpec((tm, w), lambda i: (i, 0)),
                   pl.BlockSpec((TOP_K, tm), lambda i: (0, i)),
                   pl.BlockSpec((TOP_K, tm), lambda i: (0, i)),
                   _full((N_EXPERTS, LANES))],
        out_shape=[jax.ShapeDtypeStruct((t, w), F32), jax.ShapeDtypeStruct((TOP_K, t), jnp.int32),
                   jax.ShapeDtypeStruct((TOP_K, t), jnp.int32), jax.ShapeDtypeStruct((N_EXPERTS, LANES), jnp.int32)],
        scratch_shapes=[pltpu.VMEM((N_EXPERTS, LANES), F32)],
        compiler_params=_cp("arbitrary"),
        name="router_dispatch",
    )(logits, router_b.reshape(N_EXPERTS, 1).astype(F32))


def router_gates(logits, router_b):
    t, w = logits.shape
    tm = min(TOKEN_TILE, t)
    return pl.pallas_call(
        _router_kernel,
        grid=(t // tm,),
        in_specs=[pl.BlockSpec((tm, w), lambda i: (i, 0)), _full((N_EXPERTS, 1))],
        out_specs=pl.BlockSpec((tm, w), lambda i: (i, 0)),
        out_shape=jax.ShapeDtypeStruct((t, w), F32),
        compiler_params=_cp("parallel"),
        name="router_gates",
    )(logits, router_b.reshape(N_EXPERTS, 1).astype(F32))


def _moe_kernel(h_ref, gate_ref, x_ref, wg_ref, wu_ref, wd_ref, sg_ref, su_ref, sd_ref, gpost_ref, ga2_ref,
                o_ref, acc_s):
    e = pl.program_id(1)
    h = _mx(_unpack_pairs(h_ref[...]))

    @pl.when(e == 0)
    def _():
        hs = _silu(jnp.dot(h, sg_ref[...], preferred_element_type=F32)) * jnp.dot(h, su_ref[...], preferred_element_type=F32)
        acc_s[...] = jnp.dot(_mx(hs), sd_ref[...], preferred_element_type=F32)

    gates = gate_ref[...]
    lane = lax.broadcasted_iota(jnp.int32, gates.shape, 1)
    hid = []
    for j in range(MOE_EB):
        gcol = jnp.sum(jnp.where(lane == e * MOE_EB + j, gates, 0.0), axis=1, keepdims=True)
        g = jnp.dot(h, _mx(wg_ref[j]), preferred_element_type=F32)
        u = jnp.dot(h, _mx(wu_ref[j]), preferred_element_type=F32)
        hid.append(_mx(_silu(g) * u * gcol))
    wd = _mx(wd_ref[...]).reshape(MOE_EB * D_EXPERT, -1)
    acc_s[...] += jnp.dot(jnp.concatenate(hid, axis=1), wd, preferred_element_type=F32)

    @pl.when(e == pl.num_programs(1) - 1)
    def _():
        o_ref[...] = x_ref[...] + ga2_ref[0] * _rms(acc_s[...], gpost_ref[...])


def moe_ffn(h, gates, x, layer, wg, wu, wd, sg, su, sd, gpost, ga2, tiles_per_group):
    t, d = x.shape
    tm = min(MOE_TILE, t)
    _, ne, _, f = wg.shape
    row = lambda w: pl.BlockSpec((tm, w), lambda i, e: (i, 0))
    return pl.pallas_call(
        _moe_kernel,
        grid=(t // tm, ne // MOE_EB),
        in_specs=[row(h.shape[1]), row(gates.shape[1]), row(d),
                  pl.BlockSpec((None, MOE_EB, d, f), lambda i, e: (layer, e, 0, 0)),
                  pl.BlockSpec((None, MOE_EB, d, f), lambda i, e: (layer, e, 0, 0)),
                  pl.BlockSpec((None, MOE_EB, f, d), lambda i, e: (layer, e, 0, 0)),
                  _full(sg.shape), _full(su.shape), _full(sd.shape), _full((1, d)),
                  pl.BlockSpec((1, 1, d), lambda i, e: (i // tiles_per_group, 0, 0))],
        out_specs=row(d),
        out_shape=jax.ShapeDtypeStruct((t, d), F32),
        scratch_shapes=[pltpu.VMEM((tm, d), F32)],
        compiler_params=_cp("parallel", "arbitrary"),
        name="moe_ffn",
    )(h, gates, x, wg, wu, wd, sg, su, sd, gpost, ga2)


def moe_plan(counts, n_tokens):
    n_blocks = (n_tokens * TOP_K + N_EXPERTS * (MOE_BLOCK - 1) + MOE_BLOCK - 1) // MOE_BLOCK
    cnt = counts[:, 0]
    padded = (cnt + MOE_BLOCK - 1) // MOE_BLOCK * MOE_BLOCK
    pad_end = jnp.cumsum(padded)
    off = pad_end - padded
    start = jnp.arange(n_blocks, dtype=jnp.int32) * MOE_BLOCK
    be = jnp.minimum(jnp.sum(pad_end[None, :] <= start[:, None], axis=1), N_EXPERTS - 1).astype(jnp.int32)
    mine = be[:, None] == jnp.arange(N_EXPERTS, dtype=jnp.int32)[None, :]
    end = jnp.sum(jnp.where(mine, (off + cnt)[None, :], 0), axis=1)
    nv = jnp.clip(end - start, 0, MOE_BLOCK).astype(jnp.int32)
    return off.astype(jnp.int32), be, nv


def _rows_kernel(off_ref, ek_ref, pk_ref, dest_ref):
    ek = ek_ref[...]
    dest = pk_ref[...]
    for e in range(N_EXPERTS):
        dest = dest + jnp.where(ek == e, off_ref[e], 0)
    dest_ref[...] = dest


def moe_rows(off, ek, pk):
    k, t = ek.shape
    tm = min(MOE_PLAN_TILE, t)
    spec = pl.BlockSpec((k, tm), lambda i, off: (0, i))
    return pl.pallas_call(
        _rows_kernel,
        grid_spec=pltpu.PrefetchScalarGridSpec(num_scalar_prefetch=1, grid=(t // tm,),
                                               in_specs=[spec, spec], out_specs=spec),
        out_shape=jax.ShapeDtypeStruct((k, t), jnp.int32),
        compiler_params=_cp("arbitrary"),
        name="moe_rows",
    )(off, ek, pk)


U32 = jnp.uint32
HIGH_HALF = 0xFFFF0000


def _pack_pairs(x):
    w = x.shape[1] // 2
    bits = lax.bitcast_convert_type(x.astype(jnp.bfloat16).astype(F32), U32)
    return (bits[:, w:] & jnp.uint32(HIGH_HALF)) | (bits[:, :w] >> 16)


def _unpack_pairs(p):
    lo = lax.bitcast_convert_type(p << 16, F32)
    hi = lax.bitcast_convert_type(p & jnp.uint32(HIGH_HALF), F32)
    return jnp.concatenate([lo, hi], axis=1)


def _sc_workers():
    info = plsc.get_sparse_core_info()
    return info.num_cores, info.num_cores * info.num_subcores


def sc_scatter_rows(src, idx, n_rows):
    k, t = idx.shape
    w = src.shape[1]
    n_cores, n_workers = _sc_workers()
    per_worker = t // n_workers
    mesh = plsc.VectorSubcoreMesh(core_axis_name="c", subcore_axis_name="s")

    @functools.partial(
        pl.kernel, mesh=mesh, out_type=jax.ShapeDtypeStruct((n_rows, w), src.dtype),
        scratch_types=[pltpu.VMEM((k, SC_WINDOW), jnp.int32), pltpu.VMEM((SC_WINDOW, w), src.dtype),
                       pltpu.SemaphoreType.DMA])
    def scatter(s_hbm, i_hbm, o_hbm, idx_v, rows_v, sem):
        base = (lax.axis_index("s") * n_cores + lax.axis_index("c")) * per_worker

        @pl.loop(0, per_worker // SC_WINDOW)
        def _(j):
            off = base + j * SC_WINDOW
            pltpu.sync_copy(i_hbm.at[:, pl.ds(off, SC_WINDOW)], idx_v)
            pltpu.sync_copy(s_hbm.at[pl.ds(off, SC_WINDOW)], rows_v)
            for kk in range(k):
                pltpu.async_copy(rows_v, o_hbm.at[idx_v.at[kk]], sem).wait()

    return scatter(src, idx)


def sc_gather_rows(table, idx):
    n = idx.shape[0]
    w = table.shape[1]
    n_cores, n_workers = _sc_workers()
    per_worker = n // n_workers
    mesh = plsc.VectorSubcoreMesh(core_axis_name="c", subcore_axis_name="s")

    @functools.partial(
        pl.kernel, mesh=mesh, out_type=jax.ShapeDtypeStruct((n, w), table.dtype),
        scratch_types=[pltpu.VMEM((SC_WINDOW,), jnp.int32), pltpu.VMEM((SC_WINDOW, w), table.dtype),
                       pltpu.SemaphoreType.DMA])
    def gather(t_hbm, i_hbm, o_hbm, idx_v, rows_v, sem):
        base = (lax.axis_index("s") * n_cores + lax.axis_index("c")) * per_worker

        @pl.loop(0, per_worker // SC_WINDOW)
        def _(j):
            off = base + j * SC_WINDOW
            pltpu.sync_copy(i_hbm.at[pl.ds(off, SC_WINDOW)], idx_v)
            pltpu.async_copy(t_hbm.at[idx_v], rows_v, sem).wait()
            pltpu.sync_copy(rows_v, o_hbm.at[pl.ds(off, SC_WINDOW)])

    return gather(table, idx)


def _expert_kernel(be_ref, nv_ref, xs_ref, wg_ref, wu_ref, wd_ref, ys_ref):
    nv = nv_ref[pl.program_id(0)]

    def ffn(x):
        x = _mx(x)
        hid = (_silu(jnp.dot(x, _mx(wg_ref[0]), preferred_element_type=F32))
               * jnp.dot(x, _mx(wu_ref[0]), preferred_element_type=F32))
        ys_ref[...] = _pack_pairs(jnp.dot(_mx(hid), _mx(wd_ref[0]), preferred_element_type=F32))

    @pl.when(nv == MOE_BLOCK)
    def _():
        ffn(_unpack_pairs(xs_ref[...]))

    @pl.when((nv > 0) & (nv < MOE_BLOCK))
    def _():
        x = _unpack_pairs(xs_ref[...])
        rows = lax.broadcasted_iota(jnp.int32, x.shape, 0)
        ffn(jnp.where(rows < nv, x, 0.0))

    @pl.when(nv == 0)
    def _():
        ys_ref[...] = jnp.zeros(ys_ref.shape, U32)


def moe_experts(xs, be, nv, layer, wg, wu, wd):
    n_rows, w = xs.shape
    _, _, d, f = wg.shape
    return pl.pallas_call(
        _expert_kernel,
        grid_spec=pltpu.PrefetchScalarGridSpec(
            num_scalar_prefetch=2,
            grid=(n_rows // MOE_BLOCK,),
            in_specs=[pl.BlockSpec((MOE_BLOCK, w), lambda b, be, nv: (b, 0)),
                      pl.BlockSpec((None, 1, d, f), lambda b, be, nv: (layer, be[b], 0, 0)),
                      pl.BlockSpec((None, 1, d, f), lambda b, be, nv: (layer, be[b], 0, 0)),
                      pl.BlockSpec((None, 1, f, d), lambda b, be, nv: (layer, be[b], 0, 0))],
            out_specs=pl.BlockSpec((MOE_BLOCK, w), lambda b, be, nv: (b, 0))),
        out_shape=jax.ShapeDtypeStruct((n_rows, w), U32),
        compiler_params=_cp("arbitrary"),
        name="moe_experts",
    )(be, nv, xs, wg, wu, wd)


def _combine_kernel(yg_ref, gk_ref, hp_ref, x_ref, sg_ref, su_ref, sd_ref, gpost_ref, ga2_ref, o_ref):
    h = _mx(_unpack_pairs(hp_ref[...]))
    hs = _silu(jnp.dot(h, sg_ref[...], preferred_element_type=F32)) * jnp.dot(h, su_ref[...], preferred_element_type=F32)
    f = jnp.dot(_mx(hs), sd_ref[...], preferred_element_type=F32)
    gk = gk_ref[...]
    for k in range(TOP_K):
        f = f + gk[:, k:k + 1] * _unpack_pairs(yg_ref[k])
    o_ref[...] = x_ref[...] + ga2_ref[0] * _rms(f, gpost_ref[...])


def moe_combine(yg, gk, hp, x, sg, su, sd, gpost, ga2, tiles_per_group):
    t, d = x.shape
    tm = min(MOE_ROW_TILE, t)
    w = hp.shape[1]
    row = lambda n: pl.BlockSpec((tm, n), lambda i: (i, 0))
    return pl.pallas_call(
        _combine_kernel,
        grid=(t // tm,),
        in_specs=[pl.BlockSpec((TOP_K, tm, w), lambda i: (0, i, 0)),
                  row(gk.shape[1]), row(w), row(d), _full(sg.shape), _full(su.shape), _full(sd.shape), _full((1, d)),
                  pl.BlockSpec((1, 1, d), lambda i: (i // tiles_per_group, 0, 0))],
        out_specs=row(d),
        out_shape=jax.ShapeDtypeStruct((t, d), F32),
        compiler_params=_cp("parallel"),
        name="moe_combine",
    )(yg, gk, hp, x, sg, su, sd, gpost, ga2)


def _reorder_w_in(w_in):
    c = np.cumsum((0,) + (GROUP_W, GROUP_W, GROUP_W, GROUP_W, GROUP_W, GROUP_W, 2 * SSD_STATE, 2 * SSD_STATE,
                          GROUP_W, 2 * N_HEADS, GROUP_W, GROUP_W, GROUP_W, GROUP_W, 2 * N_HEADS, 2 * N_HEADS))
    seg = lambda a, b: w_in[:, c[a]:c[b]]
    small = jnp.concatenate([seg(9, 10), seg(14, 15), seg(15, 16),
                             jnp.zeros((w_in.shape[0], LANES - 6 * N_HEADS), w_in.dtype)], axis=1)
    return jnp.concatenate([seg(0, 1), seg(5, 8), seg(10, 13), seg(1, 5), seg(8, 9), seg(13, 14), small], axis=1)


def kernel(x, c, ctx, c_ctx, w_mod, b_mod, g_pre_mix, g_post_mix, g_pre_ffn, g_post_ffn, w_in, w_out, lru_conv_w, lru_conv_b, lru_wa, lru_ba, lru_wx, lru_bx, lru_lambda, na_bias, ssd_conv_w, ssd_conv_b, ssd_a_log, ssd_dt_bias, ssd_d, ssd_norm, gdn_conv_w, gdn_a_log, gdn_dt_bias, gdn_norm, router_w, router_b, we_gate, we_up, we_down, ws_gate, ws_up, ws_down):
    bsz, seq, d = x.shape
    n_ctx = ctx.shape[1]
    depth = w_mod.shape[0]
    lat_tpg = seq // min(TOKEN_TILE, seq)
    ctx_tpg = max(bsz * n_ctx // TOKEN_TILE, 1)
    ctx_mpg = max(bsz * n_ctx // MOE_TILE, 1)

    cond = _pad_rows(jnp.concatenate([c, c_ctx[None, :]], axis=0))
    mod = modulation(cond, w_mod, b_mod).reshape(depth, SUBLANES, N_MOD, d)
    rope = rope_tables(seq)
    row = lambda v: v[None, :].astype(F32)

    xl = x.reshape(bsz * seq, d)
    xc = ctx.reshape(bsz * n_ctx, d)
    for l in range(depth):
        last = l == depth - 1
        m_lat = [mod[l, :bsz, k][:, None, :] for k in range(N_MOD)]
        m_ctx = [mod[l, bsz:bsz + 1, k][:, None, :] for k in range(N_MOD)]
        w_in_l = _reorder_w_in(w_in[l]).astype(MXU_DTYPE)
        conv = (_pad_rows(lru_conv_w[l]), row(lru_conv_b[l]), _pad_rows(ssd_conv_w[l]), row(ssd_conv_b[l]),
                _pad_rows(gdn_conv_w[l]))
        pc = in_projection(xc, row(g_pre_mix[l]), m_ctx[1], m_ctx[0], w_in_l, conv, None, n_ctx,
                           bsz * n_ctx // min(TOKEN_TILE, n_ctx))
        pl_ = in_projection(xl, row(g_pre_mix[l]), m_lat[1], m_lat[0], w_in_l, conv, rope, seq, lat_tpg)

        lru_p = lru_params(lru_wa[l], lru_ba[l], lru_wx[l], lru_bx[l], lru_lambda[l])
        a_cf, a_cb, a_st = lru_mixer(pc[P_AX], jnp.zeros((bsz, SUBLANES, GROUP_W), F32), bsz, *lru_p)
        a_lf, a_lb, _ = lru_mixer(pl_[P_AX], a_st, bsz, *lru_p)

        kc = pc[P_BK].reshape(bsz, n_ctx, GROUP_W)
        vc = pc[P_BV].reshape(bsz, n_ctx, GROUP_W)
        b_c = ctx_attention(pc[P_BQ].reshape(bsz, n_ctx, GROUP_W), kc, vc).reshape(bsz * n_ctx, GROUP_W)
        b_l = na_mixer(pl_[P_BQ], pl_[P_BK], pl_[P_BV], kc, vc, na_bias_slabs(na_bias[l]), bsz)

        ssd_p = ssd_params(ssd_a_log[l], ssd_dt_bias[l])
        c_cf, c_cb, c_st = ssd_mixer(pc[P_CX], pc[P_SM], jnp.zeros((bsz, 2, N_HEADS, SSD_STATE, HEAD_DIM), F32),
                                     bsz, *ssd_p)
        c_lf, c_lb, _ = ssd_mixer(pl_[P_CX], pl_[P_SM], c_st, bsz, *ssd_p)

        gdn_p = gdn_params(gdn_a_log[l], gdn_dt_bias[l])
        d_cf, d_cb, d_st = gdn_mixer(pc[P_DX], pc[P_SM], jnp.zeros((bsz, 2, N_HEADS, HEAD_DIM, HEAD_DIM), F32),
                                     bsz, *gdn_p)
        d_lf, d_lb, _ = gdn_mixer(pl_[P_DX], pl_[P_SM], d_st, bsz, *gdn_p)

        epi = (w_out[l].astype(MXU_DTYPE), row(g_post_mix[l]))
        epi_tail = (row(jnp.repeat(ssd_d[l], HEAD_DIM)), row(ssd_norm[l]), row(jnp.tile(gdn_norm[l], N_HEADS)), router_w[l])
        routed_w = (l, we_gate, we_up, we_down)
        shared_w = (ws_gate[l].astype(MXU_DTYPE), ws_up[l].astype(MXU_DTYPE), ws_down[l].astype(MXU_DTYPE),
                    row(g_post_ffn[l]))

        mix_l = (a_lf, a_lb, pl_[P_AG], b_l, c_lf, c_lb, pl_[P_CX], pl_[P_CZ], d_lf, d_lb, pl_[P_DZ])
        xl, hp, lg = out_projection(xl, mix_l, *epi, m_lat[2], row(g_pre_ffn[l]), m_lat[4], m_lat[3], *epi_tail, lat_tpg)
        gk, ek, pk, cnt = router_dispatch(lg, router_b[l])
        off, be, nv = moe_plan(cnt, bsz * seq)
        dest = moe_rows(off, ek, pk)
        xs = sc_scatter_rows(hp, dest, be.shape[0] * MOE_BLOCK)
        ys = moe_experts(xs, be, nv, *routed_w)
        yg = sc_gather_rows(ys, dest.reshape(-1)).reshape(TOP_K, bsz * seq, d // 2)
        xl = moe_combine(yg, gk, hp, xl, *shared_w, m_lat[5], seq // min(MOE_ROW_TILE, seq))
        if not last:
            mix_c = (a_cf, a_cb, pc[P_AG], b_c, c_cf, c_cb, pc[P_CX], pc[P_CZ], d_cf, d_cb, pc[P_DZ])
            xc, hp, lg = out_projection(xc, mix_c, *epi, m_ctx[2], row(g_pre_ffn[l]), m_ctx[4], m_ctx[3], *epi_tail, ctx_tpg)
            xc = moe_ffn(hp, router_gates(lg, router_b[l]), xc, *routed_w, *shared_w, m_ctx[5], ctx_mpg)
    return xl.reshape(bsz, seq, d)
```

```python
import functools
import math

import jax
import jax.numpy as jnp
import numpy as np
from jax import lax
from jax.experimental import pallas as pl
from jax.experimental.pallas import tpu as pltpu
from jax.experimental.pallas import tpu_sc as plsc

F32 = jnp.float32
MXU_DTYPE = jnp.bfloat16
HI = lax.Precision.HIGHEST

D_MODEL = 1024
GRID_W = 64
GROUP_W = 256
HEAD_DIM = 64
N_HEADS = 4
EPS = 1e-6
NEG_INF = -1e30
N_MOD = 6
LRU_C = 8.0
NA_WIN_ROWS = 8
NA_WIN_COLS = 16
SSD_STATE = 128
SSD_GROUPS = 2
ROPE_BASE = 10000.0
ROPE_AXIS_DIM = HEAD_DIM // 2
N_EXPERTS = 64
N_EXPERT_GROUPS = 8
TOPK_GROUPS = 4
TOP_K = 8
D_EXPERT = 256
ROUTED_SCALE = 2.5

LANES = 128
SUBLANES = 8
VMEM_LIMIT = 56 * 1024 * 1024

TOKEN_TILE = 512
LRU_CHUNK = 256
SSD_CHUNK = 128
GDN_CHUNK = 64
GDN_TILE = 256
GDN_SUB = 128
GDN_BASE = 16
MOE_TILE = 1024
MOE_EB = 4
MOE_BLOCK = 1024
MOE_ROW_TILE = 256
MOE_PLAN_TILE = 2048
SC_WINDOW = 128

P_WIDTHS = (256, 768, 768, 256, 256, 256, 256, 256, 256, 128)
(P_AX, P_CX, P_DX, P_AG, P_BQ, P_BK, P_BV, P_CZ, P_DZ, P_SM) = range(10)
P_CONV_GROUPS = 3
SM_DT, SM_BETA, SM_DECAY = 0, 8, 16


def _cp(*sem):
    return pltpu.CompilerParams(dimension_semantics=sem, vmem_limit_bytes=VMEM_LIMIT)


def _mx(x):
    return x.astype(MXU_DTYPE)


def _dot(a, b):
    return jnp.dot(_mx(a), _mx(b), preferred_element_type=F32)


def _dot_nt(a, b):
    return lax.dot_general(_mx(a), _mx(b), (((1,), (1,)), ((), ())), preferred_element_type=F32)


def _dot_tn(a, b):
    return lax.dot_general(_mx(a), _mx(b), (((0,), (0,)), ((), ())), preferred_element_type=F32)


def _dot_hi(a, b):
    return jnp.dot(a, b, preferred_element_type=F32, precision=HI)


def _sigmoid(x):
    return 1.0 / (1.0 + jnp.exp(-x))


def _silu(x):
    return x * _sigmoid(x)


def _softplus(x):
    return jnp.maximum(x, 0.0) + jnp.log1p(jnp.exp(-jnp.abs(x)))


def _gelu_tanh(x):
    return 0.5 * x * (1.0 + jnp.tanh(math.sqrt(2.0 / math.pi) * (x + 0.044715 * (x * x * x))))


def _rms(x, g):
    return x * lax.rsqrt(jnp.mean(x * x, axis=-1, keepdims=True) + EPS) * g


def _full(shape):
    n = len(shape)
    return pl.BlockSpec(shape, lambda *_: (0,) * n)


MOD_COLS = 1536


def _mod_kernel(c_ref, w_ref, b_ref, o_ref):
    o_ref[0] = _dot_hi(_silu(c_ref[...]), w_ref[0]) + b_ref[0]


def modulation(cond, w_mod, b_mod):
    depth, d, n = w_mod.shape
    return pl.pallas_call(
        _mod_kernel,
        grid=(depth, n // MOD_COLS),
        in_specs=[pl.BlockSpec((SUBLANES, d), lambda l, j: (0, 0)),
                  pl.BlockSpec((1, d, MOD_COLS), lambda l, j: (l, 0, j)),
                  pl.BlockSpec((1, 1, MOD_COLS), lambda l, j: (l, 0, j))],
        out_specs=pl.BlockSpec((1, SUBLANES, MOD_COLS), lambda l, j: (l, 0, j)),
        out_shape=jax.ShapeDtypeStruct((depth, SUBLANES, n), F32),
        compiler_params=_cp("parallel", "parallel"),
        name="modulation",
    )(cond, w_mod, b_mod.reshape(depth, 1, n))


def _inproj_kernel(*refs, tiles_per_seq, rope):
    (x_ref, xp_ref, xn_ref, g_ref, sc_ref, sh_ref, w_ref, lcw_ref, lcb_ref, scw_ref, scb_ref, gcw_ref) = refs[:12]
    cos_ref, sin_ref = (refs[12], refs[13]) if rope else (None, None)
    o_refs = refs[14:] if rope else refs[12:]
    i = pl.program_id(0)
    norm = lambda v: _rms(v, g_ref[...]) * (1.0 + sc_ref[0]) + sh_ref[0]
    p = _dot(norm(x_ref[...]), w_ref[...])
    n_conv = sum(P_WIDTHS[:P_CONV_GROUPS])
    ph = _dot(norm(jnp.concatenate([xp_ref[...], xn_ref[...]], axis=0)), w_ref[:, :n_conv])
    pos = i % tiles_per_seq
    prev = jnp.where(pos == 0, 0.0, ph[:SUBLANES])
    nxt = jnp.where(pos == tiles_per_seq - 1, 0.0, ph[SUBLANES:])
    row = lax.broadcasted_iota(jnp.int32, (SUBLANES, n_conv), 0)
    halo = jnp.where(row < 2, pltpu.roll(prev, 2, 0), jnp.where(row == 2, pltpu.roll(nxt, 2, 0), 0.0))
    c0, c1, c2 = GROUP_W, GROUP_W + 3 * GROUP_W, n_conv
    lru_u = _dwconv(p[:, :c0], halo[:, :c0], lcw_ref[...], lcb_ref[...])
    ssd_x = _silu(_dwconv(p[:, c0:c1], halo[:, c0:c1], scw_ref[...], scb_ref[...]))
    qkv = _silu(_dwconv(p[:, c1:c2], halo[:, c1:c2], gcw_ref[...]))
    qn = _l2norm_heads(qkv[:, :GROUP_W])
    kn = _l2norm_heads(qkv[:, GROUP_W:2 * GROUP_W])
    if rope:
        cos, sin = cos_ref[...], sin_ref[...]
        qn = qn * cos + _swap16(qn) * sin
        kn = kn * cos + _swap16(kn) * sin
    outs = [lru_u, ssd_x, jnp.concatenate([qn * (HEAD_DIM ** -0.5), kn, qkv[:, 2 * GROUP_W:]], axis=1)]
    off = n_conv
    for o_ref, w in zip(o_refs, P_WIDTHS):
        if outs:
            o_ref[...] = outs.pop(0)
        else:
            o_ref[...] = p[:, off:off + w].astype(o_ref.dtype)
            off += w


def in_projection(x, g, sc, sh, w, conv, rope, seq_len, tiles_per_group):
    t, d = x.shape
    tm = min(TOKEN_TILE, seq_len)
    tps = seq_len // tm
    hb = tm // SUBLANES
    vec = lambda i: (i // tiles_per_group, 0, 0)
    ins = [x, x, x, g, sc, sh, w, *conv]
    specs = [pl.BlockSpec((tm, d), lambda i: (i, 0)),
             pl.BlockSpec((SUBLANES, d), lambda i: (jnp.maximum(i * hb - 1, 0), 0)),
             pl.BlockSpec((SUBLANES, d), lambda i: (jnp.minimum((i + 1) * hb, t // SUBLANES - 1), 0)),
             _full((1, d)), pl.BlockSpec((1, 1, d), vec), pl.BlockSpec((1, 1, d), vec), _full(w.shape)]
    specs += [_full(a.shape) for a in conv]
    if rope is not None:
        ins += list(rope)
        specs += [pl.BlockSpec((tm, GROUP_W), lambda i: (i % tps, 0))] * 2
    return pl.pallas_call(
        functools.partial(_inproj_kernel, tiles_per_seq=tps, rope=rope is not None),
        grid=(t // tm,),
        in_specs=specs,
        out_specs=[pl.BlockSpec((tm, wd), lambda i: (i, 0)) for wd in P_WIDTHS],
        out_shape=[jax.ShapeDtypeStruct((t, wd), MXU_DTYPE if k in (P_BQ, P_BK, P_BV) else F32)
                   for k, wd in enumerate(P_WIDTHS)],
        compiler_params=_cp("parallel"),
        name="in_projection",
    )(*ins)


def _dwconv(x, halo, w, b=None):
    q = x.shape[0]
    row = lax.broadcasted_iota(jnp.int32, (SUBLANES, x.shape[1]), 0)

    def shifted(s, keep_rolled, edge):
        r = pltpu.roll(x, s % q, 0)
        if s > 0:
            return jnp.concatenate([jnp.where(keep_rolled, r[:SUBLANES], edge), r[SUBLANES:]], axis=0)
        return jnp.concatenate([r[:q - SUBLANES], jnp.where(keep_rolled, r[q - SUBLANES:], edge)], axis=0)

    xm2 = shifted(2, row >= 2, halo)
    xm1 = shifted(1, row >= 1, pltpu.roll(halo, SUBLANES - 1, 0))
    xp1 = shifted(-1, row < SUBLANES - 1, pltpu.roll(halo, SUBLANES - 3, 0))
    y = w[0:1] * xm2 + w[1:2] * xm1 + w[2:3] * x + w[3:4] * xp1
    return y if b is None else y + b


def _pad_rows(a, rows=SUBLANES):
    return jnp.concatenate([a, jnp.zeros((rows - a.shape[0],) + a.shape[1:], a.dtype)], axis=0)


def _chunk_specs(nc, q, c):
    fwd = pl.BlockSpec((q, c), lambda b, i: (b * nc + i, 0))
    bwd = pl.BlockSpec((q, c), lambda b, i: (b * nc + nc - 1 - i, 0))
    return fwd, bwd


def _lru_kernel(xf_ref, xb_ref, h0_ref, wg_ref, bg_ref, lam_ref,
                yf_ref, yb_ref, hfin_ref, af_s, bf_s, ab_s, bb_s, carry_s):
    i = pl.program_id(1)
    q = xf_ref.shape[0]

    @pl.when(i == 0)
    def _():
        carry_s[...] = h0_ref[0]

    def coeffs(x_ref, d, a_s, b_s):
        u = x_ref[...]
        g = _dot(u, wg_ref[:, 2 * GROUP_W * d:2 * GROUP_W * (d + 1)]) + bg_ref[:, 2 * GROUP_W * d:2 * GROUP_W * (d + 1)]
        r = _sigmoid(g[:, :GROUP_W])
        gate_in = _sigmoid(g[:, GROUP_W:])
        log_a = -LRU_C * r * _softplus(-lam_ref[d:d + 1, :])
        a_s[...] = jnp.exp(log_a)
        b_s[...] = jnp.sqrt(1.0 - jnp.exp(2.0 * log_a)) * (gate_in * u)

    coeffs(xf_ref, 0, af_s, bf_s)
    coeffs(xb_ref, 1, ab_s, bb_s)

    ng = q // SUBLANES
    row = lax.broadcasted_iota(jnp.int32, (SUBLANES, GROUP_W), 0)

    def body(g, hs):
        h_f, h_b = hs
        i0 = pl.multiple_of(g * SUBLANES, SUBLANES)
        a = af_s[pl.ds(i0, SUBLANES), :]
        b = bf_s[pl.ds(i0, SUBLANES), :]
        for s in (1, 2, 4):
            m = row >= s
            b = jnp.where(m, a * pltpu.roll(b, s, 0) + b, b)
            a = jnp.where(m, a * pltpu.roll(a, s, 0), a)
        h = b + a * h_f
        yf_ref[pl.ds(i0, SUBLANES), :] = h
        h_f = h[SUBLANES - 1:SUBLANES, :]
        j0 = pl.multiple_of((ng - 1 - g) * SUBLANES, SUBLANES)
        a = ab_s[pl.ds(j0, SUBLANES), :]
        b = bb_s[pl.ds(j0, SUBLANES), :]
        for s in (1, 2, 4):
            m = row < SUBLANES - s
            b = jnp.where(m, a * pltpu.roll(b, SUBLANES - s, 0) + b, b)
            a = jnp.where(m, a * pltpu.roll(a, SUBLANES - s, 0), a)
        h = b + a * h_b
        yb_ref[pl.ds(j0, SUBLANES), :] = h
        return h_f, h[0:1, :]

    h_f, h_b = lax.fori_loop(0, ng, body, (carry_s[0:1, :], carry_s[1:2, :]))
    carry_s[0:1, :] = h_f
    carry_s[1:2, :] = h_b

    @pl.when(i == pl.num_programs(1) - 1)
    def _():
        hfin_ref[0] = carry_s[...]


def _block_diag(w):
    h, a, b = w.shape
    return jnp.einsum('hij,hg->higj', w, jnp.eye(h, dtype=w.dtype)).reshape(h * a, h * b)


def lru_params(wa, ba, wx, bx, lam):
    wg = jnp.concatenate([_block_diag(wa[0]), _block_diag(wx[0]), _block_diag(wa[1]), _block_diag(wx[1])], axis=1)
    bg = jnp.concatenate([ba[0], bx[0], ba[1], bx[1]])[None, :]
    return wg.astype(MXU_DTYPE), bg, _pad_rows(lam)


def lru_mixer(x, h0, bsz, wg, bg, lam):
    t, c = x.shape
    s = t // bsz
    q = min(LRU_CHUNK, s)
    nc = s // q
    xf, xb = _chunk_specs(nc, q, c)
    st = pl.BlockSpec((1, SUBLANES, c), lambda b, i: (b, 0, 0))
    return pl.pallas_call(
        _lru_kernel,
        grid=(bsz, nc),
        in_specs=[xf, xb, st, _full(wg.shape), _full(bg.shape), _full(lam.shape)],
        out_specs=[xf, xb, st],
        out_shape=[jax.ShapeDtypeStruct((t, c), F32), jax.ShapeDtypeStruct((t, c), F32),
                   jax.ShapeDtypeStruct((bsz, SUBLANES, c), F32)],
        scratch_shapes=[pltpu.VMEM((q, c), F32)] * 4 + [pltpu.VMEM((SUBLANES, c), F32)],
        compiler_params=_cp("parallel", "arbitrary"),
        name="lru_mixer",
    )(x, x, h0, wg, bg, lam)


NA_KEYS = NA_WIN_ROWS * GRID_W
NA_ROW_BLOCK = 8


def na_bias_slabs(table):
    qc = np.arange(GRID_W)[:, None]
    kc = np.arange(GRID_W)[None, :]
    win0 = np.clip(qc - NA_WIN_COLS // 2, 0, GRID_W - NA_WIN_COLS)
    ok = (kc >= win0) & (kc < win0 + NA_WIN_COLS)
    dc = np.clip(kc - qc + NA_WIN_COLS - 1, 0, 2 * NA_WIN_COLS - 2)
    dr = np.arange(NA_WIN_ROWS)[:, None] + np.arange(NA_WIN_ROWS)[None, :]
    b = table.astype(F32)[:, dr][:, :, :, dc]
    b = jnp.where(ok[None, None, None], b, NEG_INF)
    h = table.shape[0]
    return b.transpose(0, 1, 3, 2, 4).reshape(h, NA_WIN_ROWS, GRID_W, NA_KEYS)


def _na_span_start(j, rows):
    return jnp.clip(j * NA_ROW_BLOCK - NA_WIN_ROWS // 2, 0, rows - (NA_ROW_BLOCK + NA_WIN_ROWS - 1))


def _na_kernel(q_ref, kw_ref, vw_ref, kc_ref, vc_ref, slab_ref, o_ref, *, rows):
    j = pl.program_id(1)
    ustart = _na_span_start(j, rows)
    q = q_ref[...] * (HEAD_DIM ** -0.5)
    kc, vc = kc_ref[0], vc_ref[0]
    heads = [slice(h * HEAD_DIM, (h + 1) * HEAD_DIM) for h in range(N_HEADS)]
    qrows = [slice(i * GRID_W, (i + 1) * GRID_W) for i in range(NA_ROW_BLOCK)]
    kws, vws, offs = [], [], []
    for i in range(NA_ROW_BLOCK):
        r = j * NA_ROW_BLOCK + i
        r0 = jnp.clip(r - NA_WIN_ROWS // 2, 0, rows - NA_WIN_ROWS)
        start = pl.multiple_of((r0 - ustart) * GRID_W, GRID_W)
        kws.append(kw_ref[pl.ds(start, NA_KEYS), :])
        vws.append(vw_ref[pl.ds(start, NA_KEYS), :])
        offs.append(r0 - r + NA_WIN_ROWS - 1)
    s_ctx = [_dot_nt(q[:, sl], kc[:, sl]) for sl in heads]
    s_loc = [[_dot_nt(q[qr, sl], kws[i][:, sl]) + slab_ref[h, offs[i]] for h, sl in enumerate(heads)]
             for i, qr in enumerate(qrows)]
    m = [[jnp.maximum(jnp.max(s_loc[i][h], axis=-1, keepdims=True), jnp.max(s_ctx[h][qr], axis=-1, keepdims=True))
          for h in range(N_HEADS)] for i, qr in enumerate(qrows)]
    p_loc = [[jnp.exp(s_loc[i][h] - m[i][h]) for h in range(N_HEADS)] for i in range(NA_ROW_BLOCK)]
    p_ctx = [jnp.exp(s_ctx[h] - jnp.concatenate([m[i][h] for i in range(NA_ROW_BLOCK)], axis=0))
             for h in range(N_HEADS)]
    o_ctx = [_dot(p_ctx[h], vc[:, sl]) for h, sl in enumerate(heads)]
    rows_out = []
    for i, qr in enumerate(qrows):
        outs = []
        for h, sl in enumerate(heads):
            den = jnp.sum(p_loc[i][h], axis=-1, keepdims=True) + jnp.sum(p_ctx[h][qr], axis=-1, keepdims=True)
            outs.append((_dot(p_loc[i][h], vws[i][:, sl]) + o_ctx[h][qr]) / den)
        rows_out.append(jnp.concatenate(outs, axis=1))
    o_ref[...] = jnp.concatenate(rows_out, axis=0)


def na_mixer(q, k, v, kc, vc, slabs, bsz):
    t, c = q.shape
    s = t // bsz
    rows = s // GRID_W
    n_ctx = kc.shape[1]
    span = (NA_ROW_BLOCK + NA_WIN_ROWS - 1) * GRID_W

    def win(b, j):
        return ((b * rows + _na_span_start(j, rows)) * GRID_W, 0)

    wspec = pl.BlockSpec((pl.Element(span), pl.Element(c)), win)
    cspec = pl.BlockSpec((1, n_ctx, c), lambda b, j: (b, 0, 0))
    qspec = pl.BlockSpec((NA_ROW_BLOCK * GRID_W, c), lambda b, j: (b * (rows // NA_ROW_BLOCK) + j, 0))
    return pl.pallas_call(
        functools.partial(_na_kernel, rows=rows),
        grid=(bsz, rows // NA_ROW_BLOCK),
        in_specs=[qspec, wspec, wspec, cspec, cspec, _full(slabs.shape)],
        out_specs=qspec,
        out_shape=jax.ShapeDtypeStruct((t, c), F32),
        compiler_params=_cp("parallel", "arbitrary"),
        name="na_mixer",
    )(q, k, v, kc, vc, slabs)


def _ctx_attn_kernel(q_ref, k_ref, v_ref, o_ref):
    q = q_ref[0] * (HEAD_DIM ** -0.5)
    k, v = k_ref[0], v_ref[0]
    outs = []
    for h in range(N_HEADS):
        sl = slice(h * HEAD_DIM, (h + 1) * HEAD_DIM)
        s = _dot_nt(q[:, sl], k[:, sl])
        p = jnp.exp(s - jnp.max(s, axis=-1, keepdims=True))
        outs.append(_dot(p, v[:, sl]) / jnp.sum(p, axis=-1, keepdims=True))
    o_ref[0] = jnp.concatenate(outs, axis=1)


def ctx_attention(q, k, v):
    spec = pl.BlockSpec((1,) + q.shape[1:], lambda b: (b, 0, 0))
    return pl.pallas_call(
        _ctx_attn_kernel,
        grid=(q.shape[0],),
        in_specs=[spec, spec, spec],
        out_specs=spec,
        out_shape=jax.ShapeDtypeStruct(q.shape, F32),
        compiler_params=_cp("parallel"),
        name="ctx_attention",
    )(q, k, v)


def _small_vec(vals, off):
    v = jnp.zeros((LANES,), F32).at[off:off + 2 * N_HEADS].set(vals.reshape(-1).astype(F32))
    return v[None, :]


def _lane_mask(off):
    lane = lax.broadcasted_iota(jnp.int32, (1, LANES), 1)
    return (lane >= off) & (lane < off + 2 * N_HEADS)


def _tri_masks(q):
    rowi = lax.broadcasted_iota(jnp.int32, (q, q), 0)
    coli = lax.broadcasted_iota(jnp.int32, (q, q), 1)
    return rowi, coli


def _ssd_kernel(xf_ref, xb_ref, sf_ref, sb_ref, h0_ref, dtb_ref, alog_ref,
                yf_ref, yb_ref, hfin_ref, state_s):
    i = pl.program_id(1)
    q = xf_ref.shape[0]

    @pl.when(i == 0)
    def _():
        state_s[...] = h0_ref[0]

    rowi, coli = _tri_masks(q)
    a_neg = jnp.where(_lane_mask(SM_DT), -jnp.exp(alog_ref[...]), 0.0)

    chains = [(d, h) for d in range(2) for h in range(N_HEADS)]
    per_head = N_HEADS // SSD_GROUPS
    scores, xdt, c_in, b_out, e_last = [], [], [], [], []
    for d, (x_ref, sm_ref) in enumerate(((xf_ref, sf_ref), (xb_ref, sb_ref))):
        xbc = x_ref[...]
        dt = _softplus(sm_ref[...] + dtb_ref[...])
        keep = (rowi >= coli) if d == 0 else (rowi <= coli)
        acum = _dot_tri(keep, dt * a_neg)
        acum_t = acum.T
        last = acum[q - 1:q, :] if d == 0 else acum[0:1, :]
        dec_end = jnp.exp(last - acum)
        e_acum = jnp.exp(acum)
        e_end = jnp.exp(last)
        bgs = [xbc[:, GROUP_W + SSD_STATE * g:GROUP_W + SSD_STATE * (g + 1)] for g in range(SSD_GROUPS)]
        cgs = [xbc[:, GROUP_W + SSD_STATE * (SSD_GROUPS + g):GROUP_W + SSD_STATE * (SSD_GROUPS + g + 1)]
               for g in range(SSD_GROUPS)]
        cbt = [_dot_nt(cg, bg) for cg, bg in zip(cgs, bgs)]
        for h in range(N_HEADS):
            g = h // per_head
            ln = SM_DT + N_HEADS * d + h
            lmat = jnp.exp(jnp.where(keep, acum[:, ln:ln + 1] - acum_t[ln:ln + 1, :], NEG_INF))
            scores.append(cbt[g] * lmat)
            xdt.append(xbc[:, h * HEAD_DIM:(h + 1) * HEAD_DIM] * dt[:, ln:ln + 1])
            c_in.append(cgs[g] * e_acum[:, ln:ln + 1])
            b_out.append(bgs[g] * dec_end[:, ln:ln + 1])
            e_last.append(e_end[:, ln:ln + 1])
    states = [state_s[d, h] for d, h in chains]
    y_diag = [_dot(s, x) for s, x in zip(scores, xdt)]
    y_off = [_dot(c, st) for c, st in zip(c_in, states)]
    upd = [_dot_tn(b, x) for b, x in zip(b_out, xdt)]
    for n, (d, h) in enumerate(chains):
        state_s[d, h] = states[n] * e_last[n] + upd[n]
    ys = [a + b for a, b in zip(y_diag, y_off)]
    yf_ref[...] = jnp.concatenate(ys[:N_HEADS], axis=1)
    yb_ref[...] = jnp.concatenate(ys[N_HEADS:], axis=1)

    @pl.when(i == pl.num_programs(1) - 1)
    def _():
        hfin_ref[0] = state_s[...]


def ssd_params(a_log, dt_bias):
    return _small_vec(dt_bias, SM_DT), _small_vec(a_log, SM_DT)


def ssd_mixer(xbc, sm, h0, bsz, dtb, alog):
    t, c = xbc.shape
    s = t // bsz
    q = min(SSD_CHUNK, s)
    nc = s // q
    xf, xb = _chunk_specs(nc, q, c)
    sf, sb = _chunk_specs(nc, q, LANES)
    yf, yb = _chunk_specs(nc, q, GROUP_W)
    st = pl.BlockSpec((1,) + h0.shape[1:], lambda b, i: (b, 0, 0, 0, 0))
    y_shape = jax.ShapeDtypeStruct((t, GROUP_W), F32)
    return pl.pallas_call(
        _ssd_kernel,
        grid=(bsz, nc),
        in_specs=[xf, xb, sf, sb, st, _full(dtb.shape), _full(alog.shape)],
        out_specs=[yf, yb, st],
        out_shape=[y_shape, y_shape, jax.ShapeDtypeStruct(h0.shape, F32)],
        scratch_shapes=[pltpu.VMEM(h0.shape[1:], F32)],
        compiler_params=_cp("parallel", "arbitrary"),
        name="ssd_mixer",
    )(xbc, xbc, sm, sm, h0, dtb, alog)


def rope_tables(seq):
    t = jnp.arange(seq)
    row = (t // GRID_W).astype(F32)
    col = (t % GRID_W).astype(F32)
    inv = ROPE_BASE ** (-jnp.arange(0, ROPE_AXIS_DIM, 2, dtype=F32) / ROPE_AXIS_DIM)
    ar, ac = row[:, None] * inv, col[:, None] * inv
    cos = jnp.concatenate([jnp.cos(ar), jnp.cos(ar), jnp.cos(ac), jnp.cos(ac)], axis=1)
    sin = jnp.concatenate([-jnp.sin(ar), jnp.sin(ar), -jnp.sin(ac), jnp.sin(ac)], axis=1)
    return jnp.tile(cos, (1, N_HEADS)), jnp.tile(sin, (1, N_HEADS))


def _swap16(x):
    lane = lax.broadcasted_iota(jnp.int32, x.shape, 1)
    half = ROPE_AXIS_DIM // 2
    return jnp.where((lane & (ROPE_AXIS_DIM - 1)) < half,
                     pltpu.roll(x, x.shape[1] - half, 1), pltpu.roll(x, half, 1))


def _head_sums(sq):
    c = sq.shape[1]
    li = lax.broadcasted_iota(jnp.int32, (c, c), 0)
    lj = lax.broadcasted_iota(jnp.int32, (c, c), 1)
    sh = HEAD_DIM.bit_length() - 1
    ones = _mx(((li >> sh) == (lj >> sh)).astype(F32))
    hi = _mx(sq)
    lo = _mx(sq - hi.astype(F32))
    return jnp.dot(hi, ones, preferred_element_type=F32) + jnp.dot(lo, ones, preferred_element_type=F32)


def _l2norm_heads(x):
    return x * lax.rsqrt(_head_sums(x * x) + EPS)


def _head_columns(x, off):
    li = lax.broadcasted_iota(jnp.int32, (LANES, N_HEADS * HEAD_DIM), 0)
    lj = lax.broadcasted_iota(jnp.int32, (LANES, N_HEADS * HEAD_DIM), 1)
    pick = _mx((li == off + (lj >> (HEAD_DIM.bit_length() - 1))).astype(F32))
    hi = _mx(x)
    lo = _mx(x - hi.astype(F32))
    return jnp.dot(hi, pick, preferred_element_type=F32) + jnp.dot(lo, pick, preferred_element_type=F32)


def _dot_tri(mask, x):
    m = _mx(mask.astype(F32))
    x1 = _mx(x)
    r1 = x - x1.astype(F32)
    x2 = _mx(r1)
    x3 = _mx(r1 - x2.astype(F32))
    return (jnp.dot(m, x1, preferred_element_type=F32) + jnp.dot(m, x2, preferred_element_type=F32)
            + jnp.dot(m, x3, preferred_element_type=F32))


def _same_block(rowi, coli, n):
    sh = n.bit_length() - 1
    return (rowi >> sh) == (coli >> sh)


def _solve_unit_tri(a_list, rhs_list, rowi, coli, chunk):
    mm = lambda x, y: jnp.dot(x, y, preferred_element_type=F32)
    eye = (rowi == coli).astype(F32)
    in_base = _same_block(rowi, coli, GDN_BASE)
    base = [_mx(jnp.where(in_base, a, 0.0)) for a in a_list]
    ts = [jnp.where(in_base, eye - a, 0.0) for a in a_list]
    ps = [_mx(mm(b, b)) for b in base]
    ts = [t + mm(_mx(t), p) for t, p in zip(ts, ps)]
    n = 4
    while n < GDN_BASE:
        ps = [_mx(mm(p, p)) for p in ps]
        ts = [t + mm(_mx(t), p) for t, p in zip(ts, ps)]
        n *= 2
    n = GDN_BASE
    while 2 * n < chunk:
        inner = _same_block(rowi, coli, 2 * n) & jnp.logical_not(_same_block(rowi, coli, n))
        offs = [_mx(jnp.where(inner, a, 0.0)) for a in a_list]
        tb = [_mx(t) for t in ts]
        ms = [_mx(mm(t, off)) for t, off in zip(tb, offs)]
        ts = [t - mm(m, t_b) for t, m, t_b in zip(ts, ms, tb)]
        n *= 2
    outer = jnp.logical_not(_same_block(rowi, coli, n))
    offs = [_mx(jnp.where(outer, a, 0.0)) for a in a_list]
    tb = [_mx(t) for t in ts]
    ys = [mm(t, _mx(r)) for t, r in zip(tb, rhs_list)]
    zs = [_mx(mm(off, _mx(y))) for off, y in zip(offs, ys)]
    return [y - mm(t, z) for y, t, z in zip(ys, tb, zs)]


def _gdn_kernel(xf_ref, xb_ref, sf_ref, sb_ref, s0_ref, alog_ref, dtb_ref, of_ref, ob_ref, sfin_ref, state_s):
    i = pl.program_id(1)
    tq = xf_ref.shape[0]
    ck = min(GDN_CHUNK, tq)
    nck = tq // ck

    @pl.when(i == 0)
    def _():
        state_s[...] = s0_ref[0]

    sub = min(GDN_SUB, tq)
    nsub = tq // sub
    rowt, colt = _tri_masks(tq)
    in_chunk_t = _same_block(rowt, colt, ck)
    rowi, coli = _tri_masks(sub)
    in_chunk = _same_block(rowi, coli, ck)
    a_neg = jnp.where(_lane_mask(SM_DECAY), -jnp.exp(alog_ref[...]), 0.0)

    a_list, rhs_list, qkm, qg, kd, e_last = [], [], [], [], [], []
    for d, (x_ref, sm_ref) in enumerate(((xf_ref, sf_ref), (xb_ref, sb_ref))):
        qkv = x_ref[...]
        qn, kn, v = qkv[:, :GROUP_W], qkv[:, GROUP_W:2 * GROUP_W], qkv[:, 2 * GROUP_W:]
        sm = sm_ref[...]
        beta = _sigmoid(sm)
        keep_t = in_chunk_t & ((rowt >= colt) if d == 0 else (rowt <= colt))
        keep = in_chunk & ((rowi >= coli) if d == 0 else (rowi <= coli))
        strict = in_chunk & ((rowi > coli) if d == 0 else (rowi < coli))
        gc = _dot_tri(keep_t, _softplus(sm + dtb_ref[...]) * a_neg)
        gc_t = gc.T
        edge = ck - 1 if d == 0 else 0
        last = jnp.concatenate([jnp.broadcast_to(gc[c * ck + edge:c * ck + edge + 1, :], (ck, LANES))
                                for c in range(nck)], axis=0)
        e_last.append(jnp.exp(last))
        beta_w = _head_columns(beta, SM_BETA + N_HEADS * d)
        e_gc_w = _head_columns(jnp.exp(gc), SM_DECAY + N_HEADS * d)
        e_end_w = _head_columns(jnp.exp(last - gc), SM_DECAY + N_HEADS * d)
        kb_w = kn * beta_w
        vb_w = v * beta_w
        kbe_w = kb_w * e_gc_w
        qg_w = qn * e_gc_w
        kd_w = kn * e_end_w
        qn_m, kn_m, kb_wm = _mx(qn), _mx(kn), _mx(kb_w)
        for h in range(N_HEADS):
            sl = slice(h * HEAD_DIM, (h + 1) * HEAD_DIM)
            lg = SM_DECAY + N_HEADS * d + h
            rhs = jnp.concatenate([vb_w[:, sl], kbe_w[:, sl]], axis=1)
            qg.append(qg_w[:, sl])
            kd.append(kd_w[:, sl])
            qh_m, kh_m, kb_m = qn_m[:, sl], kn_m[:, sl], kb_wm[:, sl]
            for s in range(nsub):
                rs = slice(s * sub, (s + 1) * sub)
                decay = jnp.exp(jnp.where(keep, gc[rs, lg:lg + 1] - gc_t[lg:lg + 1, rs], NEG_INF))
                a_list.append(jnp.where(strict, _dot_nt(kb_m[rs], kh_m[rs]) * decay, 0.0))
                rhs_list.append(rhs[rs])
                qkm.append(_dot_nt(qh_m[rs], kh_m[rs]) * decay)
    sols = _solve_unit_tri(a_list, rhs_list, rowi, coli, ck)
    sols = [jnp.concatenate(sols[n * nsub:(n + 1) * nsub], axis=0) for n in range(2 * N_HEADS)]

    chains = [(d, h) for d in range(2) for h in range(N_HEADS)]
    states = [state_s[d, h] for d, h in chains]
    v_new = [[None] * nck for _ in chains]
    o_st = [[None] * nck for _ in chains]
    for step in range(nck):
        rows = [slice((step if d == 0 else nck - 1 - step) * ck, (step if d == 0 else nck - 1 - step) * ck + ck)
                for d, _ in chains]
        ms = [_dot(jnp.concatenate([sols[n][r, HEAD_DIM:], qg[n][r]], axis=0), states[n])
              for n, r in enumerate(rows)]
        for n, (d, _) in enumerate(chains):
            c = step if d == 0 else nck - 1 - step
            v_new[n][c] = sols[n][rows[n], :HEAD_DIM] - ms[n][:ck]
            o_st[n][c] = ms[n][ck:]
        ups = [_dot_tn(kd[n][r], v_new[n][step if chains[n][0] == 0 else nck - 1 - step])
               for n, r in enumerate(rows)]
        for n, (d, h) in enumerate(chains):
            lg = SM_DECAY + N_HEADS * d + h
            states[n] = states[n] * e_last[d][rows[n].start:rows[n].start + 1, lg:lg + 1] + ups[n]
    cps = sub // ck
    outs = [jnp.concatenate(o_st[n], axis=0)
            + jnp.concatenate([_dot(qkm[n * nsub + s], jnp.concatenate(v_new[n][s * cps:(s + 1) * cps], axis=0))
                               for s in range(nsub)], axis=0)
            for n in range(len(chains))]
    of_ref[...] = jnp.concatenate(outs[:N_HEADS], axis=1)
    ob_ref[...] = jnp.concatenate(outs[N_HEADS:], axis=1)
    for n, (d, h) in enumerate(chains):
        state_s[d, h] = states[n]

    @pl.when(i == pl.num_programs(1) - 1)
    def _():
        sfin_ref[0] = state_s[...]


def gdn_params(a_log, dt_bias):
    return _small_vec(a_log, SM_DECAY), _small_vec(dt_bias, SM_DECAY)


def gdn_mixer(qkv, sm, s0, bsz, alog, dtb):
    t, c = qkv.shape
    s = t // bsz
    q = min(GDN_TILE, s)
    nc = s // q
    xf, xb = _chunk_specs(nc, q, c)
    sf, sb = _chunk_specs(nc, q, LANES)
    of, ob = _chunk_specs(nc, q, GROUP_W)
    st = pl.BlockSpec((1,) + s0.shape[1:], lambda b, i: (b, 0, 0, 0, 0))
    o_shape = jax.ShapeDtypeStruct((t, GROUP_W), F32)
    return pl.pallas_call(
        _gdn_kernel,
        grid=(bsz, nc),
        in_specs=[xf, xb, sf, sb, st, _full(alog.shape), _full(dtb.shape)],
        out_specs=[of, ob, st],
        out_shape=[o_shape, o_shape, jax.ShapeDtypeStruct(s0.shape, F32)],
        scratch_shapes=[pltpu.VMEM(s0.shape[1:], F32)],
        compiler_params=_cp("parallel", "arbitrary"),
        name="gdn_mixer",
    )(qkv, qkv, sm, sm, s0, alog, dtb)


def _split_hi_lo(a):
    hi = _mx(a)
    return hi, _mx(a - hi.astype(F32))


def _outproj_kernel(x_ref, ahf_ref, ahb_ref, ag_ref, bo_ref, cyf_ref, cyb_ref, cxc_ref, cz_ref,
                    dof_ref, dob_ref, dz_ref, wout_ref, gpost_ref, ga1_ref, gpre_ref, sc2_ref, sh2_ref,
                    dskip_ref, cnorm_ref, dnorm_ref, rhi_ref, rlo_ref, xo_ref, hp_ref, lg_ref):
    m_a = (ahf_ref[...] + ahb_ref[...]) * _gelu_tanh(ag_ref[...])
    y_c = (cyf_ref[...] + cyb_ref[...] + cxc_ref[...] * dskip_ref[...]) * _silu(cz_ref[...])
    m_c = _rms(y_c, cnorm_ref[...])
    o_d = dof_ref[...] + dob_ref[...]
    m_d = o_d * lax.rsqrt(_head_sums(o_d * o_d) * (1.0 / HEAD_DIM) + EPS) * dnorm_ref[...] * _silu(dz_ref[...])
    mix = jnp.concatenate([_mx(m_a), _mx(bo_ref[...]), _mx(m_c), _mx(m_d)], axis=1)
    ml = jnp.dot(mix, wout_ref[...], preferred_element_type=F32)
    x_new = x_ref[...] + ga1_ref[0] * _rms(ml, gpost_ref[...])
    xo_ref[...] = x_new
    h2 = _rms(x_new, gpre_ref[...]) * (1.0 + sc2_ref[0]) + sh2_ref[0]
    hi, lo = _split_hi_lo(h2)
    hp_ref[...] = _pack_pairs(h2)
    rhi = rhi_ref[...]
    lg_ref[...] = (jnp.dot(hi, rhi, preferred_element_type=F32) + jnp.dot(lo, rhi, preferred_element_type=F32)
                   + jnp.dot(hi, rlo_ref[...], preferred_element_type=F32))


def out_projection(x, mixers, w_out, gpost, ga1, gpre, sc2, sh2, dskip, cnorm, dnorm, router_w, tiles_per_group):
    t, d = x.shape
    tm = min(TOKEN_TILE, t)
    vec = lambda i: (i // tiles_per_group, 0, 0)
    row = lambda w: pl.BlockSpec((tm, w), lambda i: (i, 0))
    ne = LANES
    rhi, rlo = _split_hi_lo(jnp.pad(router_w.astype(F32), ((0, 0), (0, ne - router_w.shape[1]))))
    return pl.pallas_call(
        _outproj_kernel,
        grid=(t // tm,),
        in_specs=[row(d)] + [row(GROUP_W)] * 11
                 + [_full(w_out.shape), _full((1, d)), pl.BlockSpec((1, 1, d), vec), _full((1, d)),
                    pl.BlockSpec((1, 1, d), vec), pl.BlockSpec((1, 1, d), vec),
                    _full((1, GROUP_W)), _full((1, GROUP_W)), _full((1, GROUP_W)), _full(rhi.shape), _full(rlo.shape)],
        out_specs=[row(d), row(d // 2), row(ne)],
        out_shape=[jax.ShapeDtypeStruct((t, d), F32), jax.ShapeDtypeStruct((t, d // 2), jnp.uint32),
                   jax.ShapeDtypeStruct((t, ne), F32)],
        compiler_params=_cp("parallel"),
        name="out_projection",
    )(x, *mixers, w_out, gpost, ga1, gpre, sc2, sh2, dskip, cnorm, dnorm, rhi, rlo)


def _rank_before(vals, idx, count, stride):
    rank = jnp.zeros(vals.shape, jnp.int32)
    for j in range(count):
        other = vals[j * stride:j * stride + 1, :]
        ahead = (other > vals) | ((other == vals) & (idx > j))
        rank = rank + ahead.astype(jnp.int32)
    return rank


def _xor_partner(x, row, s):
    n = x.shape[0]
    return jnp.where((row & s) == 0, pltpu.roll(x, n - s, 0), pltpu.roll(x, s, 0))


def _route(logits, router_b):
    ne = N_EXPERTS
    gsz = ne // N_EXPERT_GROUPS
    scores = _sigmoid(logits.T[:ne, :])
    tm = scores.shape[1]
    biased = scores + router_b
    row = lax.broadcasted_iota(jnp.int32, (ne, tm), 0)
    m1, m2 = biased, jnp.full((ne, tm), -jnp.inf, F32)
    s = 1
    while s < gsz:
        o1, o2 = _xor_partner(m1, row, s), _xor_partner(m2, row, s)
        m2 = jnp.maximum(jnp.minimum(m1, o1), jnp.maximum(m2, o2))
        m1 = jnp.maximum(m1, o1)
        s *= 2
    gidx = row >> (gsz.bit_length() - 1)
    group_ok = _rank_before(m1 + m2, gidx, N_EXPERT_GROUPS, gsz) < TOPK_GROUPS
    choice = jnp.where(group_ok, biased, -jnp.inf)
    rank = _rank_before(choice, row, ne, 1)
    gate = jnp.where(rank < TOP_K, scores, 0.0)
    gate = gate / jnp.sum(gate, axis=0, keepdims=True) * ROUTED_SCALE
    return gate, rank, row


def _to_token_major(x):
    n, tm = x.shape
    return jnp.concatenate([x, jnp.zeros((LANES - n, tm), x.dtype)], axis=0).T


def _router_kernel(lg_ref, rb_ref, gate_ref):
    gate, _, _ = _route(lg_ref[...], rb_ref[...])
    gate_ref[...] = _to_token_major(gate)


def _router_dispatch_kernel(lg_ref, rb_ref, gk_ref, ek_ref, pk_ref, cnt_ref, carry_s):
    i = pl.program_id(0)

    @pl.when(i == 0)
    def _():
        carry_s[...] = jnp.zeros(carry_s.shape, F32)

    gate, rank, row = _route(lg_ref[...], rb_ref[...])
    tm = gate.shape[1]
    picked = (rank < TOP_K).astype(F32)
    before = lax.broadcasted_iota(jnp.int32, (tm, tm), 0) < lax.broadcasted_iota(jnp.int32, (tm, tm), 1)
    pos = _dot(picked, before.astype(F32)) + carry_s[:, 0:1]
    carry_s[...] = carry_s[...] + jnp.sum(picked, axis=1, keepdims=True)
    gk, ek, pk = [], [], []
    for k in range(TOP_K):
        sel = rank == k
        gk.append(jnp.sum(jnp.where(sel, gate, 0.0), axis=0, keepdims=True))
        ek.append(jnp.sum(jnp.where(sel, row, 0), axis=0, keepdims=True))
        pk.append(jnp.sum(jnp.where(sel, pos, 0.0), axis=0, keepdims=True))
    gk_ref[...] = _to_token_major(jnp.concatenate(gk, axis=0))
    ek_ref[...] = jnp.concatenate(ek, axis=0)
    pk_ref[...] = jnp.concatenate(pk, axis=0).astype(jnp.int32)

    @pl.when(i == pl.num_programs(0) - 1)
    def _():
        cnt_ref[...] = carry_s[...].astype(jnp.int32)


def router_dispatch(logits, router_b):
    t, w = logits.shape
    tm = min(TOKEN_TILE, t)
    return pl.pallas_call(
        _router_dispatch_kernel,
        grid=(t // tm,),
        in_specs=[pl.BlockSpec((tm, w), lambda i: (i, 0)), _full((N_EXPERTS, 1))],
        out_specs=[pl.BlockSpec((tm, w), lambda i: (i, 0)),
                   pl.BlockSpec((TOP_K, tm), lambda i: (0, i)),
                   pl.BlockSpec((TOP_K, tm), lambda i: (0, i)),
                   _full((N_EXPERTS, LANES))],
        out_shape=[jax.ShapeDtypeStruct((t, w), F32), jax.ShapeDtypeStruct((TOP_K, t), jnp.int32),
                   jax.ShapeDtypeStruct((TOP_K, t), jnp.int32), jax.ShapeDtypeStruct((N_EXPERTS, LANES), jnp.int32)],
        scratch_shapes=[pltpu.VMEM((N_EXPERTS, LANES), F32)],
        compiler_params=_cp("arbitrary"),
        name="router_dispatch",
    )(logits, router_b.reshape(N_EXPERTS, 1).astype(F32))


def router_gates(logits, router_b):
    t, w = logits.shape
    tm = min(TOKEN_TILE, t)
    return pl.pallas_call(
        _router_kernel,
        grid=(t // tm,),
        in_specs=[pl.BlockSpec((tm, w), lambda i: (i, 0)), _full((N_EXPERTS, 1))],
        out_specs=pl.BlockSpec((tm, w), lambda i: (i, 0)),
        out_shape=jax.ShapeDtypeStruct((t, w), F32),
        compiler_params=_cp("parallel"),
        name="router_gates",
    )(logits, router_b.reshape(N_EXPERTS, 1).astype(F32))


def _moe_kernel(h_ref, gate_ref, x_ref, wg_ref, wu_ref, wd_ref, sg_ref, su_ref, sd_ref, gpost_ref, ga2_ref,
                o_ref, acc_s):
    e = pl.program_id(1)
    h = _mx(_unpack_pairs(h_ref[...]))

    @pl.when(e == 0)
    def _():
        hs = _silu(jnp.dot(h, sg_ref[...], preferred_element_type=F32)) * jnp.dot(h, su_ref[...], preferred_element_type=F32)
        acc_s[...] = jnp.dot(_mx(hs), sd_ref[...], preferred_element_type=F32)

    gates = gate_ref[...]
    lane = lax.broadcasted_iota(jnp.int32, gates.shape, 1)
    hid = []
    for j in range(MOE_EB):
        gcol = jnp.sum(jnp.where(lane == e * MOE_EB + j, gates, 0.0), axis=1, keepdims=True)
        g = jnp.dot(h, _mx(wg_ref[j]), preferred_element_type=F32)
        u = jnp.dot(h, _mx(wu_ref[j]), preferred_element_type=F32)
        hid.append(_mx(_silu(g) * u * gcol))
    wd = _mx(wd_ref[...]).reshape(MOE_EB * D_EXPERT, -1)
    acc_s[...] += jnp.dot(jnp.concatenate(hid, axis=1), wd, preferred_element_type=F32)

    @pl.when(e == pl.num_programs(1) - 1)
    def _():
        o_ref[...] = x_ref[...] + ga2_ref[0] * _rms(acc_s[...], gpost_ref[...])


def moe_ffn(h, gates, x, layer, wg, wu, wd, sg, su, sd, gpost, ga2, tiles_per_group):
    t, d = x.shape
    tm = min(MOE_TILE, t)
    _, ne, _, f = wg.shape
    row = lambda w: pl.BlockSpec((tm, w), lambda i, e: (i, 0))
    return pl.pallas_call(
        _moe_kernel,
        grid=(t // tm, ne // MOE_EB),
        in_specs=[row(h.shape[1]), row(gates.shape[1]), row(d),
                  pl.BlockSpec((None, MOE_EB, d, f), lambda i, e: (layer, e, 0, 0)),
                  pl.BlockSpec((None, MOE_EB, d, f), lambda i, e: (layer, e, 0, 0)),
                  pl.BlockSpec((None, MOE_EB, f, d), lambda i, e: (layer, e, 0, 0)),
                  _full(sg.shape), _full(su.shape), _full(sd.shape), _full((1, d)),
                  pl.BlockSpec((1, 1, d), lambda i, e: (i // tiles_per_group, 0, 0))],
        out_specs=row(d),
        out_shape=jax.ShapeDtypeStruct((t, d), F32),
        scratch_shapes=[pltpu.VMEM((tm, d), F32)],
        compiler_params=_cp("parallel", "arbitrary"),
        name="moe_ffn",
    )(h, gates, x, wg, wu, wd, sg, su, sd, gpost, ga2)


def moe_plan(counts, n_tokens):
    n_blocks = (n_tokens * TOP_K + N_EXPERTS * (MOE_BLOCK - 1) + MOE_BLOCK - 1) // MOE_BLOCK
    cnt = counts[:, 0]
    padded = (cnt + MOE_BLOCK - 1) // MOE_BLOCK * MOE_BLOCK
    pad_end = jnp.cumsum(padded)
    off = pad_end - padded
    start = jnp.arange(n_blocks, dtype=jnp.int32) * MOE_BLOCK
    be = jnp.minimum(jnp.sum(pad_end[None, :] <= start[:, None], axis=1), N_EXPERTS - 1).astype(jnp.int32)
    mine = be[:, None] == jnp.arange(N_EXPERTS, dtype=jnp.int32)[None, :]
    end = jnp.sum(jnp.where(mine, (off + cnt)[None, :], 0), axis=1)
    nv = jnp.clip(end - start, 0, MOE_BLOCK).astype(jnp.int32)
    return off.astype(jnp.int32), be, nv


def _rows_kernel(off_ref, ek_ref, pk_ref, dest_ref):
    ek = ek_ref[...]
    dest = pk_ref[...]
    for e in range(N_EXPERTS):
        dest = dest + jnp.where(ek == e, off_ref[e], 0)
    dest_ref[...] = dest


def moe_rows(off, ek, pk):
    k, t = ek.shape
    tm = min(MOE_PLAN_TILE, t)
    spec = pl.BlockSpec((k, tm), lambda i, off: (0, i))
    return pl.pallas_call(
        _rows_kernel,
        grid_spec=pltpu.PrefetchScalarGridSpec(num_scalar_prefetch=1, grid=(t // tm,),
                                               in_specs=[spec, spec], out_specs=spec),
        out_shape=jax.ShapeDtypeStruct((k, t), jnp.int32),
        compiler_params=_cp("arbitrary"),
        name="moe_rows",
    )(off, ek, pk)


U32 = jnp.uint32
HIGH_HALF = 0xFFFF0000


def _pack_pairs(x):
    w = x.shape[1] // 2
    bits = lax.bitcast_convert_type(x.astype(jnp.bfloat16).astype(F32), U32)
    return (bits[:, w:] & jnp.uint32(HIGH_HALF)) | (bits[:, :w] >> 16)


def _unpack_pairs(p):
    lo = lax.bitcast_convert_type(p << 16, F32)
    hi = lax.bitcast_convert_type(p & jnp.uint32(HIGH_HALF), F32)
    return jnp.concatenate([lo, hi], axis=1)


def _sc_workers():
    info = plsc.get_sparse_core_info()
    return info.num_cores, info.num_cores * info.num_subcores


def sc_scatter_rows(src, idx, n_rows):
    k, t = idx.shape
    w = src.shape[1]
    n_cores, n_workers = _sc_workers()
    per_worker = t // n_workers
    mesh = plsc.VectorSubcoreMesh(core_axis_name="c", subcore_axis_name="s")

    @functools.partial(
        pl.kernel, mesh=mesh, out_type=jax.ShapeDtypeStruct((n_rows, w), src.dtype),
        scratch_types=[pltpu.VMEM((k, SC_WINDOW), jnp.int32), pltpu.VMEM((SC_WINDOW, w), src.dtype),
                       pltpu.SemaphoreType.DMA])
    def scatter(s_hbm, i_hbm, o_hbm, idx_v, rows_v, sem):
        base = (lax.axis_index("s") * n_cores + lax.axis_index("c")) * per_worker

        @pl.loop(0, per_worker // SC_WINDOW)
        def _(j):
            off = base + j * SC_WINDOW
            pltpu.sync_copy(i_hbm.at[:, pl.ds(off, SC_WINDOW)], idx_v)
            pltpu.sync_copy(s_hbm.at[pl.ds(off, SC_WINDOW)], rows_v)
            for kk in range(k):
                pltpu.async_copy(rows_v, o_hbm.at[idx_v.at[kk]], sem).wait()

    return scatter(src, idx)


def sc_gather_rows(table, idx):
    n = idx.shape[0]
    w = table.shape[1]
    n_cores, n_workers = _sc_workers()
    per_worker = n // n_workers
    mesh = plsc.VectorSubcoreMesh(core_axis_name="c", subcore_axis_name="s")

    @functools.partial(
        pl.kernel, mesh=mesh, out_type=jax.ShapeDtypeStruct((n, w), table.dtype),
        scratch_types=[pltpu.VMEM((SC_WINDOW,), jnp.int32), pltpu.VMEM((SC_WINDOW, w), table.dtype),
                       pltpu.SemaphoreType.DMA])
    def gather(t_hbm, i_hbm, o_hbm, idx_v, rows_v, sem):
        base = (lax.axis_index("s") * n_cores + lax.axis_index("c")) * per_worker

        @pl.loop(0, per_worker // SC_WINDOW)
        def _(j):
            off = base + j * SC_WINDOW
            pltpu.sync_copy(i_hbm.at[pl.ds(off, SC_WINDOW)], idx_v)
            pltpu.async_copy(t_hbm.at[idx_v], rows_v, sem).wait()
            pltpu.sync_copy(rows_v, o_hbm.at[pl.ds(off, SC_WINDOW)])

    return gather(table, idx)


def _expert_kernel(be_ref, nv_ref, xs_ref, wg_ref, wu_ref, wd_ref, ys_ref):
    nv = nv_ref[pl.program_id(0)]

    def ffn(x):
        x = _mx(x)
        hid = (_silu(jnp.dot(x, _mx(wg_ref[0]), preferred_element_type=F32))
               * jnp.dot(x, _mx(wu_ref[0]), preferred_element_type=F32))
        ys_ref[...] = _pack_pairs(jnp.dot(_mx(hid), _mx(wd_ref[0]), preferred_element_type=F32))

    @pl.when(nv == MOE_BLOCK)
    def _():
        ffn(_unpack_pairs(xs_ref[...]))

    @pl.when((nv > 0) & (nv < MOE_BLOCK))
    def _():
        x = _unpack_pairs(xs_ref[...])
        rows = lax.broadcasted_iota(jnp.int32, x.shape, 0)
        ffn(jnp.where(rows < nv, x, 0.0))

    @pl.when(nv == 0)
    def _():
        ys_ref[...] = jnp.zeros(ys_ref.shape, U32)


def moe_experts(xs, be, nv, layer, wg, wu, wd):
    n_rows, w = xs.shape
    _, _, d, f = wg.shape
    return pl.pallas_call(
        _expert_kernel,
        grid_spec=pltpu.PrefetchScalarGridSpec(
            num_scalar_prefetch=2,
            grid=(n_rows // MOE_BLOCK,),
            in_specs=[pl.BlockSpec((MOE_BLOCK, w), lambda b, be, nv: (b, 0)),
                      pl.BlockSpec((None, 1, d, f), lambda b, be, nv: (layer, be[b], 0, 0)),
                      pl.BlockSpec((None, 1, d, f), lambda b, be, nv: (layer, be[b], 0, 0)),
                      pl.BlockSpec((None, 1, f, d), lambda b, be, nv: (layer, be[b], 0, 0))],
            out_specs=pl.BlockSpec((MOE_BLOCK, w), lambda b, be, nv: (b, 0))),
        out_shape=jax.ShapeDtypeStruct((n_rows, w), U32),
        compiler_params=_cp("arbitrary"),
        name="moe_experts",
    )(be, nv, xs, wg, wu, wd)


def _combine_kernel(yg_ref, gk_ref, hp_ref, x_ref, sg_ref, su_ref, sd_ref, gpost_ref, ga2_ref, o_ref):
    h = _mx(_unpack_pairs(hp_ref[...]))
    hs = _silu(jnp.dot(h, sg_ref[...], preferred_element_type=F32)) * jnp.dot(h, su_ref[...], preferred_element_type=F32)
    f = jnp.dot(_mx(hs), sd_ref[...], preferred_element_type=F32)
    gk = gk_ref[...]
    for k in range(TOP_K):
        f = f + gk[:, k:k + 1] * _unpack_pairs(yg_ref[k])
    o_ref[...] = x_ref[...] + ga2_ref[0] * _rms(f, gpost_ref[...])


def moe_combine(yg, gk, hp, x, sg, su, sd, gpost, ga2, tiles_per_group):
    t, d = x.shape
    tm = min(MOE_ROW_TILE, t)
    w = hp.shape[1]
    row = lambda n: pl.BlockSpec((tm, n), lambda i: (i, 0))
    return pl.pallas_call(
        _combine_kernel,
        grid=(t // tm,),
        in_specs=[pl.BlockSpec((TOP_K, tm, w), lambda i: (0, i, 0)),
                  row(gk.shape[1]), row(w), row(d), _full(sg.shape), _full(su.shape), _full(sd.shape), _full((1, d)),
                  pl.BlockSpec((1, 1, d), lambda i: (i // tiles_per_group, 0, 0))],
        out_specs=row(d),
        out_shape=jax.ShapeDtypeStruct((t, d), F32),
        compiler_params=_cp("parallel"),
        name="moe_combine",
    )(yg, gk, hp, x, sg, su, sd, gpost, ga2)


def _reorder_w_in(w_in):
    c = np.cumsum((0,) + (GROUP_W, GROUP_W, GROUP_W, GROUP_W, GROUP_W, GROUP_W, 2 * SSD_STATE, 2 * SSD_STATE,
                          GROUP_W, 2 * N_HEADS, GROUP_W, GROUP_W, GROUP_W, GROUP_W, 2 * N_HEADS, 2 * N_HEADS))
    seg = lambda a, b: w_in[:, c[a]:c[b]]
    small = jnp.concatenate([seg(9, 10), seg(14, 15), seg(15, 16),
                             jnp.zeros((w_in.shape[0], LANES - 6 * N_HEADS), w_in.dtype)], axis=1)
    return jnp.concatenate([seg(0, 1), seg(5, 8), seg(10, 13), seg(1, 5), seg(8, 9), seg(13, 14), small], axis=1)


def kernel(x, c, ctx, c_ctx, w_mod, b_mod, g_pre_mix, g_post_mix, g_pre_ffn, g_post_ffn, w_in, w_out, lru_conv_w, lru_conv_b, lru_wa, lru_ba, lru_wx, lru_bx, lru_lambda, na_bias, ssd_conv_w, ssd_conv_b, ssd_a_log, ssd_dt_bias, ssd_d, ssd_norm, gdn_conv_w, gdn_a_log, gdn_dt_bias, gdn_norm, router_w, router_b, we_gate, we_up, we_down, ws_gate, ws_up, ws_down):
    bsz, seq, d = x.shape
    n_ctx = ctx.shape[1]
    depth = w_mod.shape[0]
    lat_tpg = seq // min(TOKEN_TILE, seq)
    ctx_tpg = max(bsz * n_ctx // TOKEN_TILE, 1)
    ctx_mpg = max(bsz * n_ctx // MOE_TILE, 1)

    cond = _pad_rows(jnp.concatenate([c, c_ctx[None, :]], axis=0))
    mod = modulation(cond, w_mod, b_mod).reshape(depth, SUBLANES, N_MOD, d)
    rope = rope_tables(seq)
    row = lambda v: v[None, :].astype(F32)

    def layer_params(l):
        m_lat = [mod[l, :bsz, k][:, None, :] for k in range(N_MOD)]
        m_ctx = [mod[l, bsz:bsz + 1, k][:, None, :] for k in range(N_MOD)]
        w_in_l = _reorder_w_in(w_in[l]).astype(MXU_DTYPE)
        conv = (_pad_rows(lru_conv_w[l]), row(lru_conv_b[l]), _pad_rows(ssd_conv_w[l]), row(ssd_conv_b[l]),
                _pad_rows(gdn_conv_w[l]))
        return dict(
            m_lat=m_lat, m_ctx=m_ctx, w_in=w_in_l, conv=conv,
            lru=lru_params(lru_wa[l], lru_ba[l], lru_wx[l], lru_bx[l], lru_lambda[l]),
            ssd=ssd_params(ssd_a_log[l], ssd_dt_bias[l]), gdn=gdn_params(gdn_a_log[l], gdn_dt_bias[l]),
            epi=(w_out[l].astype(MXU_DTYPE), row(g_post_mix[l])),
            epi_tail=(row(jnp.repeat(ssd_d[l], HEAD_DIM)), row(ssd_norm[l]), row(jnp.tile(gdn_norm[l], N_HEADS)),
                      router_w[l]),
            routed=(l, we_gate, we_up, we_down),
            shared=(ws_gate[l].astype(MXU_DTYPE), ws_up[l].astype(MXU_DTYPE), ws_down[l].astype(MXU_DTYPE),
                    row(g_post_ffn[l])))

    def context_mixers(l, p, xc):
        pc = in_projection(xc, row(g_pre_mix[l]), p['m_ctx'][1], p['m_ctx'][0], p['w_in'], p['conv'], None, n_ctx,
                           bsz * n_ctx // min(TOKEN_TILE, n_ctx))
        a_f, a_b, a_st = lru_mixer(pc[P_AX], jnp.zeros((bsz, SUBLANES, GROUP_W), F32), bsz, *p['lru'])
        kc = pc[P_BK].reshape(bsz, n_ctx, GROUP_W)
        vc = pc[P_BV].reshape(bsz, n_ctx, GROUP_W)
        b_o = ctx_attention(pc[P_BQ].reshape(bsz, n_ctx, GROUP_W), kc, vc).reshape(bsz * n_ctx, GROUP_W)
        c_f, c_b, c_st = ssd_mixer(pc[P_CX], pc[P_SM], jnp.zeros((bsz, 2, N_HEADS, SSD_STATE, HEAD_DIM), F32),
                                   bsz, *p['ssd'])
        d_f, d_b, d_st = gdn_mixer(pc[P_DX], pc[P_SM], jnp.zeros((bsz, 2, N_HEADS, HEAD_DIM, HEAD_DIM), F32),
                                   bsz, *p['gdn'])
        mix = (a_f, a_b, pc[P_AG], b_o, c_f, c_b, pc[P_CX], pc[P_CZ], d_f, d_b, pc[P_DZ])
        return dict(mix=mix, a_st=a_st, kc=kc, vc=vc, c_st=c_st, d_st=d_st)

    def context_ffn(l, p, cm, xc):
        m = p['m_ctx']
        xc, hp, lg = out_projection(xc, cm['mix'], *p['epi'], m[2], row(g_pre_ffn[l]), m[4], m[3], *p['epi_tail'], ctx_tpg)
        return moe_ffn(hp, router_gates(lg, router_b[l]), xc, *p['routed'], *p['shared'], m[5], ctx_mpg)

    xl = x.reshape(bsz * seq, d)
    xc = ctx.reshape(bsz * n_ctx, d)
    p = layer_params(0)
    cm = context_mixers(0, p, xc)
    for l in range(depth):
        last = l == depth - 1
        m = p['m_lat']
        pl_ = in_projection(xl, row(g_pre_mix[l]), m[1], m[0], p['w_in'], p['conv'], rope, seq, lat_tpg)
        a_f, a_b, _ = lru_mixer(pl_[P_AX], cm['a_st'], bsz, *p['lru'])
        b_o = na_mixer(pl_[P_BQ], pl_[P_BK], pl_[P_BV], cm['kc'], cm['vc'], na_bias_slabs(na_bias[l]), bsz)
        c_f, c_b, _ = ssd_mixer(pl_[P_CX], pl_[P_SM], cm['c_st'], bsz, *p['ssd'])
        d_f, d_b, _ = gdn_mixer(pl_[P_DX], pl_[P_SM], cm['d_st'], bsz, *p['gdn'])
        mix_l = (a_f, a_b, pl_[P_AG], b_o, c_f, c_b, pl_[P_CX], pl_[P_CZ], d_f, d_b, pl_[P_DZ])
        xl, hp, lg = out_projection(xl, mix_l, *p['epi'], m[2], row(g_pre_ffn[l]), m[4], m[3], *p['epi_tail'], lat_tpg)
        gk, ek, pk, cnt = router_dispatch(lg, router_b[l])
        off, be, nv = moe_plan(cnt, bsz * seq)
        dest = moe_rows(off, ek, pk)
        xs = sc_scatter_rows(hp, dest, be.shape[0] * MOE_BLOCK)
        if not last:
            xc = context_ffn(l, p, cm, xc)
        ys = moe_experts(xs, be, nv, *p['routed'])
        yg = sc_gather_rows(ys, dest.reshape(-1)).reshape(TOP_K, bsz * seq, d // 2)
        if not last:
            p_next = layer_params(l + 1)
            cm = context_mixers(l + 1, p_next, xc)
        xl = moe_combine(yg, gk, hp, xl, *p['shared'], m[5], seq // min(MOE_ROW_TILE, seq))
        if not last:
            p = p_next
    return xl.reshape(bsz, seq, d)
```

```python
import functools
import math

import jax
import jax.numpy as jnp
import numpy as np
from jax import lax
from jax.experimental import pallas as pl
from jax.experimental.pallas import tpu as pltpu
from jax.experimental.pallas import tpu_sc as plsc

F32 = jnp.float32
MXU_DTYPE = jnp.bfloat16
HI = lax.Precision.HIGHEST

D_MODEL = 1024
GRID_W = 64
GROUP_W = 256
HEAD_DIM = 64
N_HEADS = 4
EPS = 1e-6
NEG_INF = -1e30
N_MOD = 6
LRU_C = 8.0
NA_WIN_ROWS = 8
NA_WIN_COLS = 16
SSD_STATE = 128
SSD_GROUPS = 2
ROPE_BASE = 10000.0
ROPE_AXIS_DIM = HEAD_DIM // 2
N_EXPERTS = 64
N_EXPERT_GROUPS = 8
TOPK_GROUPS = 4
TOP_K = 8
D_EXPERT = 256
ROUTED_SCALE = 2.5

LANES = 128
SUBLANES = 8
VMEM_LIMIT = 56 * 1024 * 1024

TOKEN_TILE = 512
LRU_CHUNK = 256
SSD_CHUNK = 128
GDN_CHUNK = 64
GDN_TILE = 256
GDN_SUB = 128
GDN_BASE = 16
MOE_TILE = 1024
MOE_EB = 4
MOE_BLOCK = 1024
MOE_ROW_TILE = 256
MOE_PLAN_TILE = 2048
SC_WINDOW = 128

P_WIDTHS = (256, 768, 768, 256, 256, 256, 256, 256, 256, 128)
(P_AX, P_CX, P_DX, P_AG, P_BQ, P_BK, P_BV, P_CZ, P_DZ, P_SM) = range(10)
P_CONV_GROUPS = 3
SM_DT, SM_BETA, SM_DECAY = 0, 8, 16


def _cp(*sem):
    return pltpu.CompilerParams(dimension_semantics=sem, vmem_limit_bytes=VMEM_LIMIT)


def _mx(x):
    return x.astype(MXU_DTYPE)


def _dot(a, b):
    return jnp.dot(_mx(a), _mx(b), preferred_element_type=F32)


def _dot_nt(a, b):
    return lax.dot_general(_mx(a), _mx(b), (((1,), (1,)), ((), ())), preferred_element_type=F32)


def _dot_tn(a, b):
    return lax.dot_general(_mx(a), _mx(b), (((0,), (0,)), ((), ())), preferred_element_type=F32)


def _dot_hi(a, b):
    return jnp.dot(a, b, preferred_element_type=F32, precision=HI)


def _sigmoid(x):
    return 1.0 / (1.0 + jnp.exp(-x))


def _silu(x):
    return x * _sigmoid(x)


def _softplus(x):
    return jnp.maximum(x, 0.0) + jnp.log1p(jnp.exp(-jnp.abs(x)))


def _gelu_tanh(x):
    return 0.5 * x * (1.0 + jnp.tanh(math.sqrt(2.0 / math.pi) * (x + 0.044715 * (x * x * x))))


def _rms(x, g):
    return x * lax.rsqrt(jnp.mean(x * x, axis=-1, keepdims=True) + EPS) * g


def _full(shape):
    n = len(shape)
    return pl.BlockSpec(shape, lambda *_: (0,) * n)


MOD_COLS = 1536


def _mod_kernel(c_ref, w_ref, b_ref, o_ref):
    o_ref[0] = _dot_hi(_silu(c_ref[...]), w_ref[0]) + b_ref[0]


def modulation(cond, w_mod, b_mod):
    depth, d, n = w_mod.shape
    return pl.pallas_call(
        _mod_kernel,
        grid=(depth, n // MOD_COLS),
        in_specs=[pl.BlockSpec((SUBLANES, d), lambda l, j: (0, 0)),
                  pl.BlockSpec((1, d, MOD_COLS), lambda l, j: (l, 0, j)),
                  pl.BlockSpec((1, 1, MOD_COLS), lambda l, j: (l, 0, j))],
        out_specs=pl.BlockSpec((1, SUBLANES, MOD_COLS), lambda l, j: (l, 0, j)),
        out_shape=jax.ShapeDtypeStruct((depth, SUBLANES, n), F32),
        compiler_params=_cp("parallel", "parallel"),
        name="modulation",
    )(cond, w_mod, b_mod.reshape(depth, 1, n))


def _inproj_kernel(*refs, tiles_per_seq, rope):
    (x_ref, xp_ref, xn_ref, g_ref, sc_ref, sh_ref, w_ref, lcw_ref, lcb_ref, scw_ref, scb_ref, gcw_ref) = refs[:12]
    cos_ref, sin_ref = (refs[12], refs[13]) if rope else (None, None)
    o_refs = refs[14:] if rope else refs[12:]
    i = pl.program_id(0)
    norm = lambda v: _rms(v, g_ref[...]) * (1.0 + sc_ref[0]) + sh_ref[0]
    p = _dot(norm(x_ref[...]), w_ref[...])
    n_conv = sum(P_WIDTHS[:P_CONV_GROUPS])
    ph = _dot(norm(jnp.concatenate([xp_ref[...], xn_ref[...]], axis=0)), w_ref[:, :n_conv])
    pos = i % tiles_per_seq
    prev = jnp.where(pos == 0, 0.0, ph[:SUBLANES])
    nxt = jnp.where(pos == tiles_per_seq - 1, 0.0, ph[SUBLANES:])
    row = lax.broadcasted_iota(jnp.int32, (SUBLANES, n_conv), 0)
    halo = jnp.where(row < 2, pltpu.roll(prev, 2, 0), jnp.where(row == 2, pltpu.roll(nxt, 2, 0), 0.0))
    c0, c1, c2 = GROUP_W, GROUP_W + 3 * GROUP_W, n_conv
    lru_u = _dwconv(p[:, :c0], halo[:, :c0], lcw_ref[...], lcb_ref[...])
    ssd_x = _silu(_dwconv(p[:, c0:c1], halo[:, c0:c1], scw_ref[...], scb_ref[...]))
    qkv = _silu(_dwconv(p[:, c1:c2], halo[:, c1:c2], gcw_ref[...]))
    qn = _l2norm_heads(qkv[:, :GROUP_W])
    kn = _l2norm_heads(qkv[:, GROUP_W:2 * GROUP_W])
    if rope:
        cos, sin = cos_ref[...], sin_ref[...]
        qn = qn * cos + _swap16(qn) * sin
        kn = kn * cos + _swap16(kn) * sin
    outs = [lru_u, ssd_x, jnp.concatenate([qn * (HEAD_DIM ** -0.5), kn, qkv[:, 2 * GROUP_W:]], axis=1)]
    off = n_conv
    for o_ref, w in zip(o_refs, P_WIDTHS):
        if outs:
            o_ref[...] = outs.pop(0)
        else:
            o_ref[...] = p[:, off:off + w].astype(o_ref.dtype)
            off += w


def in_projection(x, g, sc, sh, w, conv, rope, seq_len, tiles_per_group):
    t, d = x.shape
    tm = min(TOKEN_TILE, seq_len)
    tps = seq_len // tm
    hb = tm // SUBLANES
    vec = lambda i: (i // tiles_per_group, 0, 0)
    ins = [x, x, x, g, sc, sh, w, *conv]
    specs = [pl.BlockSpec((tm, d), lambda i: (i, 0)),
             pl.BlockSpec((SUBLANES, d), lambda i: (jnp.maximum(i * hb - 1, 0), 0)),
             pl.BlockSpec((SUBLANES, d), lambda i: (jnp.minimum((i + 1) * hb, t // SUBLANES - 1), 0)),
             _full((1, d)), pl.BlockSpec((1, 1, d), vec), pl.BlockSpec((1, 1, d), vec), _full(w.shape)]
    specs += [_full(a.shape) for a in conv]
    if rope is not None:
        ins += list(rope)
        specs += [pl.BlockSpec((tm, GROUP_W), lambda i: (i % tps, 0))] * 2
    return pl.pallas_call(
        functools.partial(_inproj_kernel, tiles_per_seq=tps, rope=rope is not None),
        grid=(t // tm,),
        in_specs=specs,
        out_specs=[pl.BlockSpec((tm, wd), lambda i: (i, 0)) for wd in P_WIDTHS],
        out_shape=[jax.ShapeDtypeStruct((t, wd), MXU_DTYPE if k in (P_BQ, P_BK, P_BV) else F32)
                   for k, wd in enumerate(P_WIDTHS)],
        compiler_params=_cp("parallel"),
        name="in_projection",
    )(*ins)


def _dwconv(x, halo, w, b=None):
    q = x.shape[0]
    row = lax.broadcasted_iota(jnp.int32, (SUBLANES, x.shape[1]), 0)

    def shifted(s, keep_rolled, edge):
        r = pltpu.roll(x, s % q, 0)
        if s > 0:
            return jnp.concatenate([jnp.where(keep_rolled, r[:SUBLANES], edge), r[SUBLANES:]], axis=0)
        return jnp.concatenate([r[:q - SUBLANES], jnp.where(keep_rolled, r[q - SUBLANES:], edge)], axis=0)

    xm2 = shifted(2, row >= 2, halo)
    xm1 = shifted(1, row >= 1, pltpu.roll(halo, SUBLANES - 1, 0))
    xp1 = shifted(-1, row < SUBLANES - 1, pltpu.roll(halo, SUBLANES - 3, 0))
    y = w[0:1] * xm2 + w[1:2] * xm1 + w[2:3] * x + w[3:4] * xp1
    return y if b is None else y + b


def _pad_rows(a, rows=SUBLANES):
    return jnp.concatenate([a, jnp.zeros((rows - a.shape[0],) + a.shape[1:], a.dtype)], axis=0)


def _chunk_specs(nc, q, c):
    fwd = pl.BlockSpec((q, c), lambda b, i: (b * nc + i, 0))
    bwd = pl.BlockSpec((q, c), lambda b, i: (b * nc + nc - 1 - i, 0))
    return fwd, bwd


def _lru_kernel(xf_ref, xb_ref, h0_ref, wg_ref, bg_ref, lam_ref,
                yf_ref, yb_ref, hfin_ref, af_s, bf_s, ab_s, bb_s, carry_s):
    i = pl.program_id(1)
    q = xf_ref.shape[0]

    @pl.when(i == 0)
    def _():
        carry_s[...] = h0_ref[0]

    def coeffs(x_ref, d, a_s, b_s):
        u = x_ref[...]
        g = _dot(u, wg_ref[:, 2 * GROUP_W * d:2 * GROUP_W * (d + 1)]) + bg_ref[:, 2 * GROUP_W * d:2 * GROUP_W * (d + 1)]
        r = _sigmoid(g[:, :GROUP_W])
        gate_in = _sigmoid(g[:, GROUP_W:])
        log_a = -LRU_C * r * _softplus(-lam_ref[d:d + 1, :])
        a_s[...] = jnp.exp(log_a)
        b_s[...] = jnp.sqrt(1.0 - jnp.exp(2.0 * log_a)) * (gate_in * u)

    coeffs(xf_ref, 0, af_s, bf_s)
    coeffs(xb_ref, 1, ab_s, bb_s)

    ng = q // SUBLANES
    row = lax.broadcasted_iota(jnp.int32, (SUBLANES, GROUP_W), 0)

    def body(g, hs):
        h_f, h_b = hs
        i0 = pl.multiple_of(g * SUBLANES, SUBLANES)
        a = af_s[pl.ds(i0, SUBLANES), :]
        b = bf_s[pl.ds(i0, SUBLANES), :]
        for s in (1, 2, 4):
            m = row >= s
            b = jnp.where(m, a * pltpu.roll(b, s, 0) + b, b)
            a = jnp.where(m, a * pltpu.roll(a, s, 0), a)
        h = b + a * h_f
        yf_ref[pl.ds(i0, SUBLANES), :] = h
        h_f = h[SUBLANES - 1:SUBLANES, :]
        j0 = pl.multiple_of((ng - 1 - g) * SUBLANES, SUBLANES)
        a = ab_s[pl.ds(j0, SUBLANES), :]
        b = bb_s[pl.ds(j0, SUBLANES), :]
        for s in (1, 2, 4):
            m = row < SUBLANES - s
            b = jnp.where(m, a * pltpu.roll(b, SUBLANES - s, 0) + b, b)
            a = jnp.where(m, a * pltpu.roll(a, SUBLANES - s, 0), a)
        h = b + a * h_b
        yb_ref[pl.ds(j0, SUBLANES), :] = h
        return h_f, h[0:1, :]

    h_f, h_b = lax.fori_loop(0, ng, body, (carry_s[0:1, :], carry_s[1:2, :]))
    carry_s[0:1, :] = h_f
    carry_s[1:2, :] = h_b

    @pl.when(i == pl.num_programs(1) - 1)
    def _():
        hfin_ref[0] = carry_s[...]


def _block_diag(w):
    h, a, b = w.shape
    return jnp.einsum('hij,hg->higj', w, jnp.eye(h, dtype=w.dtype)).reshape(h * a, h * b)


def lru_params(wa, ba, wx, bx, lam):
    wg = jnp.concatenate([_block_diag(wa[0]), _block_diag(wx[0]), _block_diag(wa[1]), _block_diag(wx[1])], axis=1)
    bg = jnp.concatenate([ba[0], bx[0], ba[1], bx[1]])[None, :]
    return wg.astype(MXU_DTYPE), bg, _pad_rows(lam)


def lru_mixer(x, h0, bsz, wg, bg, lam):
    t, c = x.shape
    s = t // bsz
    q = min(LRU_CHUNK, s)
    nc = s // q
    xf, xb = _chunk_specs(nc, q, c)
    st = pl.BlockSpec((1, SUBLANES, c), lambda b, i: (b, 0, 0))
    return pl.pallas_call(
        _lru_kernel,
        grid=(bsz, nc),
        in_specs=[xf, xb, st, _full(wg.shape), _full(bg.shape), _full(lam.shape)],
        out_specs=[xf, xb, st],
        out_shape=[jax.ShapeDtypeStruct((t, c), F32), jax.ShapeDtypeStruct((t, c), F32),
                   jax.ShapeDtypeStruct((bsz, SUBLANES, c), F32)],
        scratch_shapes=[pltpu.VMEM((q, c), F32)] * 4 + [pltpu.VMEM((SUBLANES, c), F32)],
        compiler_params=_cp("parallel", "arbitrary"),
        name="lru_mixer",
    )(x, x, h0, wg, bg, lam)


NA_KEYS = NA_WIN_ROWS * GRID_W
NA_ROW_BLOCK = 8


def na_bias_slabs(table):
    qc = np.arange(GRID_W)[:, None]
    kc = np.arange(GRID_W)[None, :]
    win0 = np.clip(qc - NA_WIN_COLS // 2, 0, GRID_W - NA_WIN_COLS)
    ok = (kc >= win0) & (kc < win0 + NA_WIN_COLS)
    dc = np.clip(kc - qc + NA_WIN_COLS - 1, 0, 2 * NA_WIN_COLS - 2)
    dr = np.arange(NA_WIN_ROWS)[:, None] + np.arange(NA_WIN_ROWS)[None, :]
    b = table.astype(F32)[:, dr][:, :, :, dc]
    b = jnp.where(ok[None, None, None], b, NEG_INF)
    h = table.shape[0]
    return b.transpose(0, 1, 3, 2, 4).reshape(h, NA_WIN_ROWS, GRID_W, NA_KEYS)


def _na_span_start(j, rows):
    return jnp.clip(j * NA_ROW_BLOCK - NA_WIN_ROWS // 2, 0, rows - (NA_ROW_BLOCK + NA_WIN_ROWS - 1))


def _na_kernel(q_ref, kw_ref, vw_ref, kc_ref, vc_ref, slab_ref, o_ref, *, rows):
    j = pl.program_id(1)
    ustart = _na_span_start(j, rows)
    q = q_ref[...] * (HEAD_DIM ** -0.5)
    kc, vc = kc_ref[0], vc_ref[0]
    heads = [slice(h * HEAD_DIM, (h + 1) * HEAD_DIM) for h in range(N_HEADS)]
    qrows = [slice(i * GRID_W, (i + 1) * GRID_W) for i in range(NA_ROW_BLOCK)]
    kws, vws, offs = [], [], []
    for i in range(NA_ROW_BLOCK):
        r = j * NA_ROW_BLOCK + i
        r0 = jnp.clip(r - NA_WIN_ROWS // 2, 0, rows - NA_WIN_ROWS)
        start = pl.multiple_of((r0 - ustart) * GRID_W, GRID_W)
        kws.append(kw_ref[pl.ds(start, NA_KEYS), :])
        vws.append(vw_ref[pl.ds(start, NA_KEYS), :])
        offs.append(r0 - r + NA_WIN_ROWS - 1)
    s_ctx = [_dot_nt(q[:, sl], kc[:, sl]) for sl in heads]
    s_loc = [[_dot_nt(q[qr, sl], kws[i][:, sl]) + slab_ref[h, offs[i]] for h, sl in enumerate(heads)]
             for i, qr in enumerate(qrows)]
    m = [[jnp.maximum(jnp.max(s_loc[i][h], axis=-1, keepdims=True), jnp.max(s_ctx[h][qr], axis=-1, keepdims=True))
          for h in range(N_HEADS)] for i, qr in enumerate(qrows)]
    p_loc = [[jnp.exp(s_loc[i][h] - m[i][h]) for h in range(N_HEADS)] for i in range(NA_ROW_BLOCK)]
    p_ctx = [jnp.exp(s_ctx[h] - jnp.concatenate([m[i][h] for i in range(NA_ROW_BLOCK)], axis=0))
             for h in range(N_HEADS)]
    o_ctx = [_dot(p_ctx[h], vc[:, sl]) for h, sl in enumerate(heads)]
    rows_out = []
    for i, qr in enumerate(qrows):
        outs = []
        for h, sl in enumerate(heads):
            den = jnp.sum(p_loc[i][h], axis=-1, keepdims=True) + jnp.sum(p_ctx[h][qr], axis=-1, keepdims=True)
            outs.append((_dot(p_loc[i][h], vws[i][:, sl]) + o_ctx[h][qr]) / den)
        rows_out.append(jnp.concatenate(outs, axis=1))
    o_ref[...] = jnp.concatenate(rows_out, axis=0)


def na_mixer(q, k, v, kc, vc, slabs, bsz):
    t, c = q.shape
    s = t // bsz
    rows = s // GRID_W
    n_ctx = kc.shape[1]
    span = (NA_ROW_BLOCK + NA_WIN_ROWS - 1) * GRID_W

    def win(b, j):
        return ((b * rows + _na_span_start(j, rows)) * GRID_W, 0)

    wspec = pl.BlockSpec((pl.Element(span), pl.Element(c)), win)
    cspec = pl.BlockSpec((1, n_ctx, c), lambda b, j: (b, 0, 0))
    qspec = pl.BlockSpec((NA_ROW_BLOCK * GRID_W, c), lambda b, j: (b * (rows // NA_ROW_BLOCK) + j, 0))
    return pl.pallas_call(
        functools.partial(_na_kernel, rows=rows),
        grid=(bsz, rows // NA_ROW_BLOCK),
        in_specs=[qspec, wspec, wspec, cspec, cspec, _full(slabs.shape)],
        out_specs=qspec,
        out_shape=jax.ShapeDtypeStruct((t, c), F32),
        compiler_params=_cp("parallel", "arbitrary"),
        name="na_mixer",
    )(q, k, v, kc, vc, slabs)


def _ctx_attn_kernel(q_ref, k_ref, v_ref, o_ref):
    q = q_ref[0] * (HEAD_DIM ** -0.5)
    k, v = k_ref[0], v_ref[0]
    outs = []
    for h in range(N_HEADS):
        sl = slice(h * HEAD_DIM, (h + 1) * HEAD_DIM)
        s = _dot_nt(q[:, sl], k[:, sl])
        p = jnp.exp(s - jnp.max(s, axis=-1, keepdims=True))
        outs.append(_dot(p, v[:, sl]) / jnp.sum(p, axis=-1, keepdims=True))
    o_ref[0] = jnp.concatenate(outs, axis=1)


def ctx_attention(q, k, v):
    spec = pl.BlockSpec((1,) + q.shape[1:], lambda b: (b, 0, 0))
    return pl.pallas_call(
        _ctx_attn_kernel,
        grid=(q.shape[0],),
        in_specs=[spec, spec, spec],
        out_specs=spec,
        out_shape=jax.ShapeDtypeStruct(q.shape, F32),
        compiler_params=_cp("parallel"),
        name="ctx_attention",
    )(q, k, v)


def _small_vec(vals, off):
    v = jnp.zeros((LANES,), F32).at[off:off + 2 * N_HEADS].set(vals.reshape(-1).astype(F32))
    return v[None, :]


def _lane_mask(off):
    lane = lax.broadcasted_iota(jnp.int32, (1, LANES), 1)
    return (lane >= off) & (lane < off + 2 * N_HEADS)


def _tri_masks(q):
    rowi = lax.broadcasted_iota(jnp.int32, (q, q), 0)
    coli = lax.broadcasted_iota(jnp.int32, (q, q), 1)
    return rowi, coli


def _ssd_kernel(xf_ref, xb_ref, sf_ref, sb_ref, h0_ref, dtb_ref, alog_ref,
                yf_ref, yb_ref, hfin_ref, state_s):
    i = pl.program_id(1)
    q = xf_ref.shape[0]

    @pl.when(i == 0)
    def _():
        state_s[...] = h0_ref[0]

    rowi, coli = _tri_masks(q)
    a_neg = jnp.where(_lane_mask(SM_DT), -jnp.exp(alog_ref[...]), 0.0)

    chains = [(d, h) for d in range(2) for h in range(N_HEADS)]
    per_head = N_HEADS // SSD_GROUPS
    scores, xdt, c_in, b_out, e_last = [], [], [], [], []
    for d, (x_ref, sm_ref) in enumerate(((xf_ref, sf_ref), (xb_ref, sb_ref))):
        xbc = x_ref[...]
        dt = _softplus(sm_ref[...] + dtb_ref[...])
        keep = (rowi >= coli) if d == 0 else (rowi <= coli)
        acum = _dot_tri(keep, dt * a_neg)
        acum_t = acum.T
        last = acum[q - 1:q, :] if d == 0 else acum[0:1, :]
        dec_end = jnp.exp(last - acum)
        e_acum = jnp.exp(acum)
        e_end = jnp.exp(last)
        bgs = [xbc[:, GROUP_W + SSD_STATE * g:GROUP_W + SSD_STATE * (g + 1)] for g in range(SSD_GROUPS)]
        cgs = [xbc[:, GROUP_W + SSD_STATE * (SSD_GROUPS + g):GROUP_W + SSD_STATE * (SSD_GROUPS + g + 1)]
               for g in range(SSD_GROUPS)]
        cbt = [_dot_nt(cg, bg) for cg, bg in zip(cgs, bgs)]
        for h in range(N_HEADS):
            g = h // per_head
            ln = SM_DT + N_HEADS * d + h
            lmat = jnp.exp(jnp.where(keep, acum[:, ln:ln + 1] - acum_t[ln:ln + 1, :], NEG_INF))
            scores.append(cbt[g] * lmat)
            xdt.append(xbc[:, h * HEAD_DIM:(h + 1) * HEAD_DIM] * dt[:, ln:ln + 1])
            c_in.append(cgs[g] * e_acum[:, ln:ln + 1])
            b_out.append(bgs[g] * dec_end[:, ln:ln + 1])
            e_last.append(e_end[:, ln:ln + 1])
    states = [state_s[d, h] for d, h in chains]
    y_diag = [_dot(s, x) for s, x in zip(scores, xdt)]
    y_off = [_dot(c, st) for c, st in zip(c_in, states)]
    upd = [_dot_tn(b, x) for b, x in zip(b_out, xdt)]
    for n, (d, h) in enumerate(chains):
        state_s[d, h] = states[n] * e_last[n] + upd[n]
    ys = [a + b for a, b in zip(y_diag, y_off)]
    yf_ref[...] = jnp.concatenate(ys[:N_HEADS], axis=1)
    yb_ref[...] = jnp.concatenate(ys[N_HEADS:], axis=1)

    @pl.when(i == pl.num_programs(1) - 1)
    def _():
        hfin_ref[0] = state_s[...]


def ssd_params(a_log, dt_bias):
    return _small_vec(dt_bias, SM_DT), _small_vec(a_log, SM_DT)


def ssd_mixer(xbc, sm, h0, bsz, dtb, alog):
    t, c = xbc.shape
    s = t // bsz
    q = min(SSD_CHUNK, s)
    nc = s // q
    xf, xb = _chunk_specs(nc, q, c)
    sf, sb = _chunk_specs(nc, q, LANES)
    yf, yb = _chunk_specs(nc, q, GROUP_W)
    st = pl.BlockSpec((1,) + h0.shape[1:], lambda b, i: (b, 0, 0, 0, 0))
    y_shape = jax.ShapeDtypeStruct((t, GROUP_W), F32)
    return pl.pallas_call(
        _ssd_kernel,
        grid=(bsz, nc),
        in_specs=[xf, xb, sf, sb, st, _full(dtb.shape), _full(alog.shape)],
        out_specs=[yf, yb, st],
        out_shape=[y_shape, y_shape, jax.ShapeDtypeStruct(h0.shape, F32)],
        scratch_shapes=[pltpu.VMEM(h0.shape[1:], F32)],
        compiler_params=_cp("parallel", "arbitrary"),
        name="ssd_mixer",
    )(xbc, xbc, sm, sm, h0, dtb, alog)


def rope_tables(seq):
    t = jnp.arange(seq)
    row = (t // GRID_W).astype(F32)
    col = (t % GRID_W).astype(F32)
    inv = ROPE_BASE ** (-jnp.arange(0, ROPE_AXIS_DIM, 2, dtype=F32) / ROPE_AXIS_DIM)
    ar, ac = row[:, None] * inv, col[:, None] * inv
    cos = jnp.concatenate([jnp.cos(ar), jnp.cos(ar), jnp.cos(ac), jnp.cos(ac)], axis=1)
    sin = jnp.concatenate([-jnp.sin(ar), jnp.sin(ar), -jnp.sin(ac), jnp.sin(ac)], axis=1)
    return jnp.tile(cos, (1, N_HEADS)), jnp.tile(sin, (1, N_HEADS))


def _swap16(x):
    lane = lax.broadcasted_iota(jnp.int32, x.shape, 1)
    half = ROPE_AXIS_DIM // 2
    return jnp.where((lane & (ROPE_AXIS_DIM - 1)) < half,
                     pltpu.roll(x, x.shape[1] - half, 1), pltpu.roll(x, half, 1))


def _head_sums(sq):
    c = sq.shape[1]
    li = lax.broadcasted_iota(jnp.int32, (c, c), 0)
    lj = lax.broadcasted_iota(jnp.int32, (c, c), 1)
    sh = HEAD_DIM.bit_length() - 1
    ones = _mx(((li >> sh) == (lj >> sh)).astype(F32))
    hi = _mx(sq)
    lo = _mx(sq - hi.astype(F32))
    return jnp.dot(hi, ones, preferred_element_type=F32) + jnp.dot(lo, ones, preferred_element_type=F32)


def _l2norm_heads(x):
    return x * lax.rsqrt(_head_sums(x * x) + EPS)


def _head_columns(x, off):
    li = lax.broadcasted_iota(jnp.int32, (LANES, N_HEADS * HEAD_DIM), 0)
    lj = lax.broadcasted_iota(jnp.int32, (LANES, N_HEADS * HEAD_DIM), 1)
    pick = _mx((li == off + (lj >> (HEAD_DIM.bit_length() - 1))).astype(F32))
    hi = _mx(x)
    lo = _mx(x - hi.astype(F32))
    return jnp.dot(hi, pick, preferred_element_type=F32) + jnp.dot(lo, pick, preferred_element_type=F32)


def _dot_tri(mask, x):
    m = _mx(mask.astype(F32))
    x1 = _mx(x)
    r1 = x - x1.astype(F32)
    x2 = _mx(r1)
    x3 = _mx(r1 - x2.astype(F32))
    return (jnp.dot(m, x1, preferred_element_type=F32) + jnp.dot(m, x2, preferred_element_type=F32)
            + jnp.dot(m, x3, preferred_element_type=F32))


def _same_block(rowi, coli, n):
    sh = n.bit_length() - 1
    return (rowi >> sh) == (coli >> sh)


def _solve_unit_tri(a_list, rhs_list, rowi, coli, chunk):
    mm = lambda x, y: jnp.dot(x, y, preferred_element_type=F32)
    eye = (rowi == coli).astype(F32)
    in_base = _same_block(rowi, coli, GDN_BASE)
    base = [_mx(jnp.where(in_base, a, 0.0)) for a in a_list]
    ts = [jnp.where(in_base, eye - a, 0.0) for a in a_list]
    ps = [_mx(mm(b, b)) for b in base]
    ts = [t + mm(_mx(t), p) for t, p in zip(ts, ps)]
    n = 4
    while n < GDN_BASE:
        ps = [_mx(mm(p, p)) for p in ps]
        ts = [t + mm(_mx(t), p) for t, p in zip(ts, ps)]
        n *= 2
    n = GDN_BASE
    while 2 * n < chunk:
        inner = _same_block(rowi, coli, 2 * n) & jnp.logical_not(_same_block(rowi, coli, n))
        offs = [_mx(jnp.where(inner, a, 0.0)) for a in a_list]
        tb = [_mx(t) for t in ts]
        ms = [_mx(mm(t, off)) for t, off in zip(tb, offs)]
        ts = [t - mm(m, t_b) for t, m, t_b in zip(ts, ms, tb)]
        n *= 2
    outer = jnp.logical_not(_same_block(rowi, coli, n))
    offs = [_mx(jnp.where(outer, a, 0.0)) for a in a_list]
    tb = [_mx(t) for t in ts]
    ys = [mm(t, _mx(r)) for t, r in zip(tb, rhs_list)]
    zs = [_mx(mm(off, _mx(y))) for off, y in zip(offs, ys)]
    return [y - mm(t, z) for y, t, z in zip(ys, tb, zs)]


def _gdn_kernel(xf_ref, xb_ref, sf_ref, sb_ref, s0_ref, alog_ref, dtb_ref, of_ref, ob_ref, sfin_ref, state_s):
    i = pl.program_id(1)
    tq = xf_ref.shape[0]
    ck = min(GDN_CHUNK, tq)
    nck = tq // ck

    @pl.when(i == 0)
    def _():
        state_s[...] = s0_ref[0]

    sub = min(GDN_SUB, tq)
    nsub = tq // sub
    rowt, colt = _tri_masks(tq)
    in_chunk_t = _same_block(rowt, colt, ck)
    rowi, coli = _tri_masks(sub)
    in_chunk = _same_block(rowi, coli, ck)
    a_neg = jnp.where(_lane_mask(SM_DECAY), -jnp.exp(alog_ref[...]), 0.0)

    a_list, rhs_list, qkm, qg, kd, e_last = [], [], [], [], [], []
    for d, (x_ref, sm_ref) in enumerate(((xf_ref, sf_ref), (xb_ref, sb_ref))):
        qkv = x_ref[...]
        qn, kn, v = qkv[:, :GROUP_W], qkv[:, GROUP_W:2 * GROUP_W], qkv[:, 2 * GROUP_W:]
        sm = sm_ref[...]
        beta = _sigmoid(sm)
        keep_t = in_chunk_t & ((rowt >= colt) if d == 0 else (rowt <= colt))
        keep = in_chunk & ((rowi >= coli) if d == 0 else (rowi <= coli))
        strict = in_chunk & ((rowi > coli) if d == 0 else (rowi < coli))
        gc = _dot_tri(keep_t, _softplus(sm + dtb_ref[...]) * a_neg)
        gc_t = gc.T
        edge = ck - 1 if d == 0 else 0
        last = jnp.concatenate([jnp.broadcast_to(gc[c * ck + edge:c * ck + edge + 1, :], (ck, LANES))
                                for c in range(nck)], axis=0)
        e_last.append(jnp.exp(last))
        beta_w = _head_columns(beta, SM_BETA + N_HEADS * d)
        e_gc_w = _head_columns(jnp.exp(gc), SM_DECAY + N_HEADS * d)
        e_end_w = _head_columns(jnp.exp(last - gc), SM_DECAY + N_HEADS * d)
        kb_w = kn * beta_w
        vb_w = v * beta_w
        kbe_w = kb_w * e_gc_w
        qg_w = qn * e_gc_w
        kd_w = kn * e_end_w
        qn_m, kn_m, kb_wm = _mx(qn), _mx(kn), _mx(kb_w)
        for h in range(N_HEADS):
            sl = slice(h * HEAD_DIM, (h + 1) * HEAD_DIM)
            lg = SM_DECAY + N_HEADS * d + h
            rhs = jnp.concatenate([vb_w[:, sl], kbe_w[:, sl]], axis=1)
            qg.append(qg_w[:, sl])
            kd.append(kd_w[:, sl])
            qh_m, kh_m, kb_m = qn_m[:, sl], kn_m[:, sl], kb_wm[:, sl]
            for s in range(nsub):
                rs = slice(s * sub, (s + 1) * sub)
                decay = jnp.exp(jnp.where(keep, gc[rs, lg:lg + 1] - gc_t[lg:lg + 1, rs], NEG_INF))
                a_list.append(jnp.where(strict, _dot_nt(kb_m[rs], kh_m[rs]) * decay, 0.0))
                rhs_list.append(rhs[rs])
                qkm.append(_dot_nt(qh_m[rs], kh_m[rs]) * decay)
    sols = _solve_unit_tri(a_list, rhs_list, rowi, coli, ck)
    sols = [jnp.concatenate(sols[n * nsub:(n + 1) * nsub], axis=0) for n in range(2 * N_HEADS)]

    chains = [(d, h) for d in range(2) for h in range(N_HEADS)]
    states = [state_s[d, h] for d, h in chains]
    v_new = [[None] * nck for _ in chains]
    o_st = [[None] * nck for _ in chains]
    for step in range(nck):
        rows = [slice((step if d == 0 else nck - 1 - step) * ck, (step if d == 0 else nck - 1 - step) * ck + ck)
                for d, _ in chains]
        ms = [_dot(jnp.concatenate([sols[n][r, HEAD_DIM:], qg[n][r]], axis=0), states[n])
              for n, r in enumerate(rows)]
        for n, (d, _) in enumerate(chains):
            c = step if d == 0 else nck - 1 - step
            v_new[n][c] = sols[n][rows[n], :HEAD_DIM] - ms[n][:ck]
            o_st[n][c] = ms[n][ck:]
        ups = [_dot_tn(kd[n][r], v_new[n][step if chains[n][0] == 0 else nck - 1 - step])
               for n, r in enumerate(rows)]
        for n, (d, h) in enumerate(chains):
            lg = SM_DECAY + N_HEADS * d + h
            states[n] = states[n] * e_last[d][rows[n].start:rows[n].start + 1, lg:lg + 1] + ups[n]
    cps = sub // ck
    outs = [jnp.concatenate(o_st[n], axis=0)
            + jnp.concatenate([_dot(qkm[n * nsub + s], jnp.concatenate(v_new[n][s * cps:(s + 1) * cps], axis=0))
                               for s in range(nsub)], axis=0)
            for n in range(len(chains))]
    of_ref[...] = jnp.concatenate(outs[:N_HEADS], axis=1)
    ob_ref[...] = jnp.concatenate(outs[N_HEADS:], axis=1)
    for n, (d, h) in enumerate(chains):
        state_s[d, h] = states[n]

    @pl.when(i == pl.num_programs(1) - 1)
    def _():
        sfin_ref[0] = state_s[...]


def gdn_params(a_log, dt_bias):
    return _small_vec(a_log, SM_DECAY), _small_vec(dt_bias, SM_DECAY)


def gdn_mixer(qkv, sm, s0, bsz, alog, dtb):
    t, c = qkv.shape
    s = t // bsz
    q = min(GDN_TILE, s)
    nc = s // q
    xf, xb = _chunk_specs(nc, q, c)
    sf, sb = _chunk_specs(nc, q, LANES)
    of, ob = _chunk_specs(nc, q, GROUP_W)
    st = pl.BlockSpec((1,) + s0.shape[1:], lambda b, i: (b, 0, 0, 0, 0))
    o_shape = jax.ShapeDtypeStruct((t, GROUP_W), F32)
    return pl.pallas_call(
        _gdn_kernel,
        grid=(bsz, nc),
        in_specs=[xf, xb, sf, sb, st, _full(alog.shape), _full(dtb.shape)],
        out_specs=[of, ob, st],
        out_shape=[o_shape, o_shape, jax.ShapeDtypeStruct(s0.shape, F32)],
        scratch_shapes=[pltpu.VMEM(s0.shape[1:], F32)],
        compiler_params=_cp("parallel", "arbitrary"),
        name="gdn_mixer",
    )(qkv, qkv, sm, sm, s0, alog, dtb)


def _split_hi_lo(a):
    hi = _mx(a)
    return hi, _mx(a - hi.astype(F32))


def _outproj_kernel(x_ref, ahf_ref, ahb_ref, ag_ref, bo_ref, cyf_ref, cyb_ref, cxc_ref, cz_ref,
                    dof_ref, dob_ref, dz_ref, wout_ref, gpost_ref, ga1_ref, gpre_ref, sc2_ref, sh2_ref,
                    dskip_ref, cnorm_ref, dnorm_ref, rhi_ref, rlo_ref, xo_ref, hp_ref, lg_ref):
    m_a = (ahf_ref[...] + ahb_ref[...]) * _gelu_tanh(ag_ref[...])
    y_c = (cyf_ref[...] + cyb_ref[...] + cxc_ref[...] * dskip_ref[...]) * _silu(cz_ref[...])
    m_c = _rms(y_c, cnorm_ref[...])
    o_d = dof_ref[...] + dob_ref[...]
    m_d = o_d * lax.rsqrt(_head_sums(o_d * o_d) * (1.0 / HEAD_DIM) + EPS) * dnorm_ref[...] * _silu(dz_ref[...])
    mix = jnp.concatenate([_mx(m_a), _mx(bo_ref[...]), _mx(m_c), _mx(m_d)], axis=1)
    ml = jnp.dot(mix, wout_ref[...], preferred_element_type=F32)
    x_new = x_ref[...] + ga1_ref[0] * _rms(ml, gpost_ref[...])
    xo_ref[...] = x_new
    h2 = _rms(x_new, gpre_ref[...]) * (1.0 + sc2_ref[0]) + sh2_ref[0]
    hi, lo = _split_hi_lo(h2)
    hp_ref[...] = _pack_pairs(h2)
    rhi = rhi_ref[...]
    lg_ref[...] = (jnp.dot(hi, rhi, preferred_element_type=F32) + jnp.dot(lo, rhi, preferred_element_type=F32)
                   + jnp.dot(hi, rlo_ref[...], preferred_element_type=F32))


def out_projection(x, mixers, w_out, gpost, ga1, gpre, sc2, sh2, dskip, cnorm, dnorm, router_w, tiles_per_group):
    t, d = x.shape
    tm = min(TOKEN_TILE, t)
    vec = lambda i: (i // tiles_per_group, 0, 0)
    row = lambda w: pl.BlockSpec((tm, w), lambda i: (i, 0))
    ne = LANES
    rhi, rlo = _split_hi_lo(jnp.pad(router_w.astype(F32), ((0, 0), (0, ne - router_w.shape[1]))))
    return pl.pallas_call(
        _outproj_kernel,
        grid=(t // tm,),
        in_specs=[row(d)] + [row(GROUP_W)] * 11
                 + [_full(w_out.shape), _full((1, d)), pl.BlockSpec((1, 1, d), vec), _full((1, d)),
                    pl.BlockSpec((1, 1, d), vec), pl.BlockSpec((1, 1, d), vec),
                    _full((1, GROUP_W)), _full((1, GROUP_W)), _full((1, GROUP_W)), _full(rhi.shape), _full(rlo.shape)],
        out_specs=[row(d), row(d // 2), row(ne)],
        out_shape=[jax.ShapeDtypeStruct((t, d), F32), jax.ShapeDtypeStruct((t, d // 2), jnp.uint32),
                   jax.ShapeDtypeStruct((t, ne), F32)],
        compiler_params=_cp("parallel"),
        name="out_projection",
    )(x, *mixers, w_out, gpost, ga1, gpre, sc2, sh2, dskip, cnorm, dnorm, rhi, rlo)


def _rank_before(vals, idx, count, stride):
    rank = jnp.zeros(vals.shape, jnp.int32)
    for j in range(count):
        other = vals[j * stride:j * stride + 1, :]
        ahead = (other > vals) | ((other == vals) & (idx > j))
        rank = rank + ahead.astype(jnp.int32)
    return rank


def _xor_partner(x, row, s):
    n = x.shape[0]
    return jnp.where((row & s) == 0, pltpu.roll(x, n - s, 0), pltpu.roll(x, s, 0))


def _route(logits, router_b):
    ne = N_EXPERTS
    gsz = ne // N_EXPERT_GROUPS
    scores = _sigmoid(logits.T[:ne, :])
    tm = scores.shape[1]
    biased = scores + router_b
    row = lax.broadcasted_iota(jnp.int32, (ne, tm), 0)
    m1, m2 = biased, jnp.full((ne, tm), -jnp.inf, F32)
    s = 1
    while s < gsz:
        o1, o2 = _xor_partner(m1, row, s), _xor_partner(m2, row, s)
        m2 = jnp.maximum(jnp.minimum(m1, o1), jnp.maximum(m2, o2))
        m1 = jnp.maximum(m1, o1)
        s *= 2
    gidx = row >> (gsz.bit_length() - 1)
    group_ok = _rank_before(m1 + m2, gidx, N_EXPERT_GROUPS, gsz) < TOPK_GROUPS
    choice = jnp.where(group_ok, biased, -jnp.inf)
    rank = jnp.full((ne, tm), TOP_K, jnp.int32)
    rest = choice
    for k in range(TOP_K):
        top = jnp.max(rest, axis=0, keepdims=True)
        first = jnp.min(jnp.where(rest == top, row, ne), axis=0, keepdims=True)
        hit = row == first
        rank = jnp.where(hit, k, rank)
        rest = jnp.where(hit, -jnp.inf, rest)
    gate = jnp.where(rank < TOP_K, scores, 0.0)
    gate = gate / jnp.sum(gate, axis=0, keepdims=True) * ROUTED_SCALE
    return gate, rank, row


def _to_token_major(x):
    n, tm = x.shape
    return jnp.concatenate([x, jnp.zeros((LANES - n, tm), x.dtype)], axis=0).T


def _router_kernel(lg_ref, rb_ref, gate_ref):
    gate, _, _ = _route(lg_ref[...], rb_ref[...])
    gate_ref[...] = _to_token_major(gate)


def _router_dispatch_kernel(lg_ref, rb_ref, gk_ref, ek_ref, pk_ref, cnt_ref, carry_s):
    i = pl.program_id(0)

    @pl.when(i == 0)
    def _():
        carry_s[...] = jnp.zeros(carry_s.shape, F32)

    gate, rank, row = _route(lg_ref[...], rb_ref[...])
    tm = gate.shape[1]
    picked = (rank < TOP_K).astype(F32)
    before = lax.broadcasted_iota(jnp.int32, (tm, tm), 0) < lax.broadcasted_iota(jnp.int32, (tm, tm), 1)
    pos = _dot(picked, before.astype(F32)) + carry_s[:, 0:1]
    carry_s[...] = carry_s[...] + jnp.sum(picked, axis=1, keepdims=True)
    gk, ek, pk = [], [], []
    for k in range(TOP_K):
        sel = rank == k
        gk.append(jnp.sum(jnp.where(sel, gate, 0.0), axis=0, keepdims=True))
        ek.append(jnp.sum(jnp.where(sel, row, 0), axis=0, keepdims=True))
        pk.append(jnp.sum(jnp.where(sel, pos, 0.0), axis=0, keepdims=True))
    gk_ref[...] = _to_token_major(jnp.concatenate(gk, axis=0))
    ek_ref[...] = jnp.concatenate(ek, axis=0)
    pk_ref[...] = jnp.concatenate(pk, axis=0).astype(jnp.int32)

    @pl.when(i == pl.num_programs(0) - 1)
    def _():
        cnt_ref[...] = carry_s[...].astype(jnp.int32)


def router_dispatch(logits, router_b):
    t, w = logits.shape
    tm = min(TOKEN_TILE, t)
    return pl.pallas_call(
        _router_dispatch_kernel,
        grid=(t // tm,),
        in_specs=[pl.BlockSpec((tm, w), lambda i: (i, 0)), _full((N_EXPERTS, 1))],
        out_specs=[pl.BlockSpec((tm, w), lambda i: (i, 0)),
                   pl.BlockSpec((TOP_K, tm), lambda i: (0, i)),
                   pl.BlockSpec((TOP_K, tm), lambda i: (0, i)),
                   _full((N_EXPERTS, LANES))],
        out_shape=[jax.ShapeDtypeStruct((t, w), F32), jax.ShapeDtypeStruct((TOP_K, t), jnp.int32),
                   jax.ShapeDtypeStruct((TOP_K, t), jnp.int32), jax.ShapeDtypeStruct((N_EXPERTS, LANES), jnp.int32)],
        scratch_shapes=[pltpu.VMEM((N_EXPERTS, LANES), F32)],
        compiler_params=_cp("arbitrary"),
        name="router_dispatch",
    )(logits, router_b.reshape(N_EXPERTS, 1).astype(F32))


def router_gates(logits, router_b):
    t, w = logits.shape
    tm = min(TOKEN_TILE, t)
    return pl.pallas_call(
        _router_kernel,
        grid=(t // tm,),
        in_specs=[pl.BlockSpec((tm, w), lambda i: (i, 0)), _full((N_EXPERTS, 1))],
        out_specs=pl.BlockSpec((tm, w), lambda i: (i, 0)),
        out_shape=jax.ShapeDtypeStruct((t, w), F32),
        compiler_params=_cp("parallel"),
        name="router_gates",
    )(logits, router_b.reshape(N_EXPERTS, 1).astype(F32))


def _moe_kernel(h_ref, gate_ref, x_ref, wg_ref, wu_ref, wd_ref, sg_ref, su_ref, sd_ref, gpost_ref, ga2_ref,
                o_ref, acc_s):
    e = pl.program_id(1)
    h = _mx(_unpack_pairs(h_ref[...]))

    @pl.when(e == 0)
    def _():
        hs = _silu(jnp.dot(h, sg_ref[...], preferred_element_type=F32)) * jnp.dot(h, su_ref[...], preferred_element_type=F32)
        acc_s[...] = jnp.dot(_mx(hs), sd_ref[...], preferred_element_type=F32)

    gates = gate_ref[...]
    lane = lax.broadcasted_iota(jnp.int32, gates.shape, 1)
    hid = []
    for j in range(MOE_EB):
        gcol = jnp.sum(jnp.where(lane == e * MOE_EB + j, gates, 0.0), axis=1, keepdims=True)
        g = jnp.dot(h, _mx(wg_ref[j]), preferred_element_type=F32)
        u = jnp.dot(h, _mx(wu_ref[j]), preferred_element_type=F32)
        hid.append(_mx(_silu(g) * u * gcol))
    wd = _mx(wd_ref[...]).reshape(MOE_EB * D_EXPERT, -1)
    acc_s[...] += jnp.dot(jnp.concatenate(hid, axis=1), wd, preferred_element_type=F32)

    @pl.when(e == pl.num_programs(1) - 1)
    def _():
        o_ref[...] = x_ref[...] + ga2_ref[0] * _rms(acc_s[...], gpost_ref[...])


def moe_ffn(h, gates, x, layer, wg, wu, wd, sg, su, sd, gpost, ga2, tiles_per_group):
    t, d = x.shape
    tm = min(MOE_TILE, t)
    _, ne, _, f = wg.shape
    row = lambda w: pl.BlockSpec((tm, w), lambda i, e: (i, 0))
    return pl.pallas_call(
        _moe_kernel,
        grid=(t // tm, ne // MOE_EB),
        in_specs=[row(h.shape[1]), row(gates.shape[1]), row(d),
                  pl.BlockSpec((None, MOE_EB, d, f), lambda i, e: (layer, e, 0, 0)),
                  pl.BlockSpec((None, MOE_EB, d, f), lambda i, e: (layer, e, 0, 0)),
                  pl.BlockSpec((None, MOE_EB, f, d), lambda i, e: (layer, e, 0, 0)),
                  _full(sg.shape), _full(su.shape), _full(sd.shape), _full((1, d)),
                  pl.BlockSpec((1, 1, d), lambda i, e: (i // tiles_per_group, 0, 0))],
        out_specs=row(d),
        out_shape=jax.ShapeDtypeStruct((t, d), F32),
        scratch_shapes=[pltpu.VMEM((tm, d), F32)],
        compiler_params=_cp("parallel", "arbitrary"),
        name="moe_ffn",
    )(h, gates, x, wg, wu, wd, sg, su, sd, gpost, ga2)


def moe_plan(counts, n_tokens):
    n_blocks = (n_tokens * TOP_K + N_EXPERTS * (MOE_BLOCK - 1) + MOE_BLOCK - 1) // MOE_BLOCK
    cnt = counts[:, 0]
    padded = (cnt + MOE_BLOCK - 1) // MOE_BLOCK * MOE_BLOCK
    pad_end = jnp.cumsum(padded)
    off = pad_end - padded
    start = jnp.arange(n_blocks, dtype=jnp.int32) * MOE_BLOCK
    be = jnp.minimum(jnp.sum(pad_end[None, :] <= start[:, None], axis=1), N_EXPERTS - 1).astype(jnp.int32)
    mine = be[:, None] == jnp.arange(N_EXPERTS, dtype=jnp.int32)[None, :]
    end = jnp.sum(jnp.where(mine, (off + cnt)[None, :], 0), axis=1)
    nv = jnp.clip(end - start, 0, MOE_BLOCK).astype(jnp.int32)
    return off.astype(jnp.int32), be, nv


def _rows_kernel(off_ref, ek_ref, pk_ref, dest_ref):
    ek = ek_ref[...]
    dest = pk_ref[...]
    for e in range(N_EXPERTS):
        dest = dest + jnp.where(ek == e, off_ref[e], 0)
    dest_ref[...] = dest


def moe_rows(off, ek, pk):
    k, t = ek.shape
    tm = min(MOE_PLAN_TILE, t)
    spec = pl.BlockSpec((k, tm), lambda i, off: (0, i))
    return pl.pallas_call(
        _rows_kernel,
        grid_spec=pltpu.PrefetchScalarGridSpec(num_scalar_prefetch=1, grid=(t // tm,),
                                               in_specs=[spec, spec], out_specs=spec),
        out_shape=jax.ShapeDtypeStruct((k, t), jnp.int32),
        compiler_params=_cp("arbitrary"),
        name="moe_rows",
    )(off, ek, pk)


U32 = jnp.uint32
HIGH_HALF = 0xFFFF0000


def _pack_pairs(x):
    w = x.shape[1] // 2
    bits = lax.bitcast_convert_type(x.astype(jnp.bfloat16).astype(F32), U32)
    return (bits[:, w:] & jnp.uint32(HIGH_HALF)) | (bits[:, :w] >> 16)


def _unpack_pairs(p):
    lo = lax.bitcast_convert_type(p << 16, F32)
    hi = lax.bitcast_convert_type(p & jnp.uint32(HIGH_HALF), F32)
    return jnp.concatenate([lo, hi], axis=1)


def _sc_workers():
    info = plsc.get_sparse_core_info()
    return info.num_cores, info.num_cores * info.num_subcores


def sc_scatter_rows(src, idx, n_rows):
    k, t = idx.shape
    w = src.shape[1]
    n_cores, n_workers = _sc_workers()
    per_worker = t // n_workers
    mesh = plsc.VectorSubcoreMesh(core_axis_name="c", subcore_axis_name="s")

    @functools.partial(
        pl.kernel, mesh=mesh, out_type=jax.ShapeDtypeStruct((n_rows, w), src.dtype),
        scratch_types=[pltpu.VMEM((k, SC_WINDOW), jnp.int32), pltpu.VMEM((SC_WINDOW, w), src.dtype),
                       pltpu.SemaphoreType.DMA])
    def scatter(s_hbm, i_hbm, o_hbm, idx_v, rows_v, sem):
        base = (lax.axis_index("s") * n_cores + lax.axis_index("c")) * per_worker

        @pl.loop(0, per_worker // SC_WINDOW)
        def _(j):
            off = base + j * SC_WINDOW
            pltpu.sync_copy(i_hbm.at[:, pl.ds(off, SC_WINDOW)], idx_v)
            pltpu.sync_copy(s_hbm.at[pl.ds(off, SC_WINDOW)], rows_v)
            for kk in range(k):
                pltpu.async_copy(rows_v, o_hbm.at[idx_v.at[kk]], sem).wait()

    return scatter(src, idx)


def sc_gather_rows(table, idx):
    n = idx.shape[0]
    w = table.shape[1]
    n_cores, n_workers = _sc_workers()
    per_worker = n // n_workers
    mesh = plsc.VectorSubcoreMesh(core_axis_name="c", subcore_axis_name="s")

    @functools.partial(
        pl.kernel, mesh=mesh, out_type=jax.ShapeDtypeStruct((n, w), table.dtype),
        scratch_types=[pltpu.VMEM((SC_WINDOW,), jnp.int32), pltpu.VMEM((SC_WINDOW, w), table.dtype),
                       pltpu.SemaphoreType.DMA])
    def gather(t_hbm, i_hbm, o_hbm, idx_v, rows_v, sem):
        base = (lax.axis_index("s") * n_cores + lax.axis_index("c")) * per_worker

        @pl.loop(0, per_worker // SC_WINDOW)
        def _(j):
            off = base + j * SC_WINDOW
            pltpu.sync_copy(i_hbm.at[pl.ds(off, SC_WINDOW)], idx_v)
            pltpu.async_copy(t_hbm.at[idx_v], rows_v, sem).wait()
            pltpu.sync_copy(rows_v, o_hbm.at[pl.ds(off, SC_WINDOW)])

    return gather(table, idx)


def _expert_kernel(be_ref, nv_ref, xs_ref, wg_ref, wu_ref, wd_ref, ys_ref):
    nv = nv_ref[pl.program_id(0)]

    def ffn(x):
        x = _mx(x)
        hid = (_silu(jnp.dot(x, _mx(wg_ref[0]), preferred_element_type=F32))
               * jnp.dot(x, _mx(wu_ref[0]), preferred_element_type=F32))
        ys_ref[...] = _pack_pairs(jnp.dot(_mx(hid), _mx(wd_ref[0]), preferred_element_type=F32))

    @pl.when(nv == MOE_BLOCK)
    def _():
        ffn(_unpack_pairs(xs_ref[...]))

    @pl.when((nv > 0) & (nv < MOE_BLOCK))
    def _():
        x = _unpack_pairs(xs_ref[...])
        rows = lax.broadcasted_iota(jnp.int32, x.shape, 0)
        ffn(jnp.where(rows < nv, x, 0.0))

    @pl.when(nv == 0)
    def _():
        ys_ref[...] = jnp.zeros(ys_ref.shape, U32)


def moe_experts(xs, be, nv, layer, wg, wu, wd):
    n_rows, w = xs.shape
    _, _, d, f = wg.shape
    return pl.pallas_call(
        _expert_kernel,
        grid_spec=pltpu.PrefetchScalarGridSpec(
            num_scalar_prefetch=2,
            grid=(n_rows // MOE_BLOCK,),
            in_specs=[pl.BlockSpec((MOE_BLOCK, w), lambda b, be, nv: (b, 0)),
                      pl.BlockSpec((None, 1, d, f), lambda b, be, nv: (layer, be[b], 0, 0)),
                      pl.BlockSpec((None, 1, d, f), lambda b, be, nv: (layer, be[b], 0, 0)),
                      pl.BlockSpec((None, 1, f, d), lambda b, be, nv: (layer, be[b], 0, 0))],
            out_specs=pl.BlockSpec((MOE_BLOCK, w), lambda b, be, nv: (b, 0))),
        out_shape=jax.ShapeDtypeStruct((n_rows, w), U32),
        compiler_params=_cp("arbitrary"),
        name="moe_experts",
    )(be, nv, xs, wg, wu, wd)


def _combine_kernel(yg_ref, gk_ref, hp_ref, x_ref, sg_ref, su_ref, sd_ref, gpost_ref, ga2_ref, o_ref):
    h = _mx(_unpack_pairs(hp_ref[...]))
    hs = _silu(jnp.dot(h, sg_ref[...], preferred_element_type=F32)) * jnp.dot(h, su_ref[...], preferred_element_type=F32)
    f = jnp.dot(_mx(hs), sd_ref[...], preferred_element_type=F32)
    gk = gk_ref[...]
    for k in range(TOP_K):
        f = f + gk[:, k:k + 1] * _unpack_pairs(yg_ref[k])
    o_ref[...] = x_ref[...] + ga2_ref[0] * _rms(f, gpost_ref[...])


def moe_combine(yg, gk, hp, x, sg, su, sd, gpost, ga2, tiles_per_group):
    t, d = x.shape
    tm = min(MOE_ROW_TILE, t)
    w = hp.shape[1]
    row = lambda n: pl.BlockSpec((tm, n), lambda i: (i, 0))
    return pl.pallas_call(
        _combine_kernel,
        grid=(t // tm,),
        in_specs=[pl.BlockSpec((TOP_K, tm, w), lambda i: (0, i, 0)),
                  row(gk.shape[1]), row(w), row(d), _full(sg.shape), _full(su.shape), _full(sd.shape), _full((1, d)),
                  pl.BlockSpec((1, 1, d), lambda i: (i // tiles_per_group, 0, 0))],
        out_specs=row(d),
        out_shape=jax.ShapeDtypeStruct((t, d), F32),
        compiler_params=_cp("parallel"),
        name="moe_combine",
    )(yg, gk, hp, x, sg, su, sd, gpost, ga2)


def _reorder_w_in(w_in):
    c = np.cumsum((0,) + (GROUP_W, GROUP_W, GROUP_W, GROUP_W, GROUP_W, GROUP_W, 2 * SSD_STATE, 2 * SSD_STATE,
                          GROUP_W, 2 * N_HEADS, GROUP_W, GROUP_W, GROUP_W, GROUP_W, 2 * N_HEADS, 2 * N_HEADS))
    seg = lambda a, b: w_in[:, c[a]:c[b]]
    small = jnp.concatenate([seg(9, 10), seg(14, 15), seg(15, 16),
                             jnp.zeros((w_in.shape[0], LANES - 6 * N_HEADS), w_in.dtype)], axis=1)
    return jnp.concatenate([seg(0, 1), seg(5, 8), seg(10, 13), seg(1, 5), seg(8, 9), seg(13, 14), small], axis=1)


def kernel(x, c, ctx, c_ctx, w_mod, b_mod, g_pre_mix, g_post_mix, g_pre_ffn, g_post_ffn, w_in, w_out, lru_conv_w, lru_conv_b, lru_wa, lru_ba, lru_wx, lru_bx, lru_lambda, na_bias, ssd_conv_w, ssd_conv_b, ssd_a_log, ssd_dt_bias, ssd_d, ssd_norm, gdn_conv_w, gdn_a_log, gdn_dt_bias, gdn_norm, router_w, router_b, we_gate, we_up, we_down, ws_gate, ws_up, ws_down):
    bsz, seq, d = x.shape
    n_ctx = ctx.shape[1]
    depth = w_mod.shape[0]
    lat_tpg = seq // min(TOKEN_TILE, seq)
    ctx_tpg = max(bsz * n_ctx // TOKEN_TILE, 1)
    ctx_mpg = max(bsz * n_ctx // MOE_TILE, 1)

    cond = _pad_rows(jnp.concatenate([c, c_ctx[None, :]], axis=0))
    mod = modulation(cond, w_mod, b_mod).reshape(depth, SUBLANES, N_MOD, d)
    rope = rope_tables(seq)
    row = lambda v: v[None, :].astype(F32)

    def layer_params(l):
        m_lat = [mod[l, :bsz, k][:, None, :] for k in range(N_MOD)]
        m_ctx = [mod[l, bsz:bsz + 1, k][:, None, :] for k in range(N_MOD)]
        w_in_l = _reorder_w_in(w_in[l]).astype(MXU_DTYPE)
        conv = (_pad_rows(lru_conv_w[l]), row(lru_conv_b[l]), _pad_rows(ssd_conv_w[l]), row(ssd_conv_b[l]),
                _pad_rows(gdn_conv_w[l]))
        return dict(
            m_lat=m_lat, m_ctx=m_ctx, w_in=w_in_l, conv=conv,
            lru=lru_params(lru_wa[l], lru_ba[l], lru_wx[l], lru_bx[l], lru_lambda[l]),
            ssd=ssd_params(ssd_a_log[l], ssd_dt_bias[l]), gdn=gdn_params(gdn_a_log[l], gdn_dt_bias[l]),
            epi=(w_out[l].astype(MXU_DTYPE), row(g_post_mix[l])),
            epi_tail=(row(jnp.repeat(ssd_d[l], HEAD_DIM)), row(ssd_norm[l]), row(jnp.tile(gdn_norm[l], N_HEADS)),
                      router_w[l]),
            routed=(l, we_gate, we_up, we_down),
            shared=(ws_gate[l].astype(MXU_DTYPE), ws_up[l].astype(MXU_DTYPE), ws_down[l].astype(MXU_DTYPE),
                    row(g_post_ffn[l])))

    def context_mixers(l, p, xc):
        pc = in_projection(xc, row(g_pre_mix[l]), p['m_ctx'][1], p['m_ctx'][0], p['w_in'], p['conv'], None, n_ctx,
                           bsz * n_ctx // min(TOKEN_TILE, n_ctx))
        a_f, a_b, a_st = lru_mixer(pc[P_AX], jnp.zeros((bsz, SUBLANES, GROUP_W), F32), bsz, *p['lru'])
        kc = pc[P_BK].reshape(bsz, n_ctx, GROUP_W)
        vc = pc[P_BV].reshape(bsz, n_ctx, GROUP_W)
        b_o = ctx_attention(pc[P_BQ].reshape(bsz, n_ctx, GROUP_W), kc, vc).reshape(bsz * n_ctx, GROUP_W)
        c_f, c_b, c_st = ssd_mixer(pc[P_CX], pc[P_SM], jnp.zeros((bsz, 2, N_HEADS, SSD_STATE, HEAD_DIM), F32),
                                   bsz, *p['ssd'])
        d_f, d_b, d_st = gdn_mixer(pc[P_DX], pc[P_SM], jnp.zeros((bsz, 2, N_HEADS, HEAD_DIM, HEAD_DIM), F32),
                                   bsz, *p['gdn'])
        mix = (a_f, a_b, pc[P_AG], b_o, c_f, c_b, pc[P_CX], pc[P_CZ], d_f, d_b, pc[P_DZ])
        return dict(mix=mix, a_st=a_st, kc=kc, vc=vc, c_st=c_st, d_st=d_st)

    def context_ffn(l, p, cm, xc):
        m = p['m_ctx']
        xc, hp, lg = out_projection(xc, cm['mix'], *p['epi'], m[2], row(g_pre_ffn[l]), m[4], m[3], *p['epi_tail'], ctx_tpg)
        return moe_ffn(hp, router_gates(lg, router_b[l]), xc, *p['routed'], *p['shared'], m[5], ctx_mpg)

    xl = x.reshape(bsz * seq, d)
    xc = ctx.reshape(bsz * n_ctx, d)
    p = layer_params(0)
    cm = context_mixers(0, p, xc)
    for l in range(depth):
        last = l == depth - 1
        m = p['m_lat']
        pl_ = in_projection(xl, row(g_pre_mix[l]), m[1], m[0], p['w_in'], p['conv'], rope, seq, lat_tpg)
        a_f, a_b, _ = lru_mixer(pl_[P_AX], cm['a_st'], bsz, *p['lru'])
        b_o = na_mixer(pl_[P_BQ], pl_[P_BK], pl_[P_BV], cm['kc'], cm['vc'], na_bias_slabs(na_bias[l]), bsz)
        c_f, c_b, _ = ssd_mixer(pl_[P_CX], pl_[P_SM], cm['c_st'], bsz, *p['ssd'])
        d_f, d_b, _ = gdn_mixer(pl_[P_DX], pl_[P_SM], cm['d_st'], bsz, *p['gdn'])
        mix_l = (a_f, a_b, pl_[P_AG], b_o, c_f, c_b, pl_[P_CX], pl_[P_CZ], d_f, d_b, pl_[P_DZ])
        xl, hp, lg = out_projection(xl, mix_l, *p['epi'], m[2], row(g_pre_ffn[l]), m[4], m[3], *p['epi_tail'], lat_tpg)
        gk, ek, pk, cnt = router_dispatch(lg, router_b[l])
        off, be, nv = moe_plan(cnt, bsz * seq)
        dest = moe_rows(off, ek, pk)
        xs = sc_scatter_rows(hp, dest, be.shape[0] * MOE_BLOCK)
        if not last:
            xc = context_ffn(l, p, cm, xc)
        ys = moe_experts(xs, be, nv, *p['routed'])
        yg = sc_gather_rows(ys, dest.reshape(-1)).reshape(TOP_K, bsz * seq, d // 2)
        if not last:
            p_next = layer_params(l + 1)
            cm = context_mixers(l + 1, p_next, xc)
        xl = moe_combine(yg, gk, hp, xl, *p['shared'], m[5], seq // min(MOE_ROW_TILE, seq))
        if not last:
            p = p_next
    return xl.reshape(bsz, seq, d)
```

```python
import functools
import math

import jax
import jax.numpy as jnp
import numpy as np
from jax import lax
from jax.experimental import pallas as pl
from jax.experimental.pallas import tpu as pltpu
from jax.experimental.pallas import tpu_sc as plsc

F32 = jnp.float32
MXU_DTYPE = jnp.bfloat16
HI = lax.Precision.HIGHEST

D_MODEL = 1024
GRID_W = 64
GROUP_W = 256
HEAD_DIM = 64
N_HEADS = 4
EPS = 1e-6
NEG_INF = -1e30
N_MOD = 6
LRU_C = 8.0
NA_WIN_ROWS = 8
NA_WIN_COLS = 16
SSD_STATE = 128
SSD_GROUPS = 2
ROPE_BASE = 10000.0
ROPE_AXIS_DIM = HEAD_DIM // 2
N_EXPERTS = 64
N_EXPERT_GROUPS = 8
TOPK_GROUPS = 4
TOP_K = 8
D_EXPERT = 256
ROUTED_SCALE = 2.5

LANES = 128
SUBLANES = 8
VMEM_LIMIT = 56 * 1024 * 1024

TOKEN_TILE = 512
LRU_CHUNK = 256
SSD_CHUNK = 128
GDN_CHUNK = 64
GDN_TILE = 256
GDN_SUB = 128
GDN_BASE = 16
MOE_TILE = 1024
MOE_EB = 4
MOE_BLOCK = 1024
MOE_ROW_TILE = 256
LRU_UNROLL = 4
MOE_PLAN_TILE = 2048
SC_WINDOW = 128

P_WIDTHS = (256, 768, 768, 256, 256, 256, 256, 256, 256, 128)
(P_AX, P_CX, P_DX, P_AG, P_BQ, P_BK, P_BV, P_CZ, P_DZ, P_SM) = range(10)
P_CONV_GROUPS = 3
SM_DT, SM_BETA, SM_DECAY = 0, 8, 16


def _cp(*sem):
    return pltpu.CompilerParams(dimension_semantics=sem, vmem_limit_bytes=VMEM_LIMIT)


def _mx(x):
    return x.astype(MXU_DTYPE)


def _dot(a, b):
    return jnp.dot(_mx(a), _mx(b), preferred_element_type=F32)


def _dot_nt(a, b):
    return lax.dot_general(_mx(a), _mx(b), (((1,), (1,)), ((), ())), preferred_element_type=F32)


def _dot_tn(a, b):
    return lax.dot_general(_mx(a), _mx(b), (((0,), (0,)), ((), ())), preferred_element_type=F32)


def _dot_hi(a, b):
    return jnp.dot(a, b, preferred_element_type=F32, precision=HI)


def _sigmoid(x):
    return 1.0 / (1.0 + jnp.exp(-x))


def _silu(x):
    return x * _sigmoid(x)


def _softplus(x):
    return jnp.maximum(x, 0.0) + jnp.log1p(jnp.exp(-jnp.abs(x)))


def _gelu_tanh(x):
    return 0.5 * x * (1.0 + jnp.tanh(math.sqrt(2.0 / math.pi) * (x + 0.044715 * (x * x * x))))


def _rms(x, g):
    return x * lax.rsqrt(jnp.mean(x * x, axis=-1, keepdims=True) + EPS) * g


def _full(shape):
    n = len(shape)
    return pl.BlockSpec(shape, lambda *_: (0,) * n)


MOD_COLS = 1536


def _mod_kernel(c_ref, w_ref, b_ref, o_ref):
    o_ref[0] = _dot_hi(_silu(c_ref[...]), w_ref[0]) + b_ref[0]


def modulation(cond, w_mod, b_mod):
    depth, d, n = w_mod.shape
    return pl.pallas_call(
        _mod_kernel,
        grid=(depth, n // MOD_COLS),
        in_specs=[pl.BlockSpec((SUBLANES, d), lambda l, j: (0, 0)),
                  pl.BlockSpec((1, d, MOD_COLS), lambda l, j: (l, 0, j)),
                  pl.BlockSpec((1, 1, MOD_COLS), lambda l, j: (l, 0, j))],
        out_specs=pl.BlockSpec((1, SUBLANES, MOD_COLS), lambda l, j: (l, 0, j)),
        out_shape=jax.ShapeDtypeStruct((depth, SUBLANES, n), F32),
        compiler_params=_cp("parallel", "parallel"),
        name="modulation",
    )(cond, w_mod, b_mod.reshape(depth, 1, n))


def _inproj_kernel(*refs, tiles_per_seq, rope):
    (x_ref, xp_ref, xn_ref, g_ref, sc_ref, sh_ref, w_ref, lcw_ref, lcb_ref, scw_ref, scb_ref, gcw_ref) = refs[:12]
    cos_ref, sin_ref = (refs[12], refs[13]) if rope else (None, None)
    o_refs = refs[14:] if rope else refs[12:]
    i = pl.program_id(0)
    norm = lambda v: _rms(v, g_ref[...]) * (1.0 + sc_ref[0]) + sh_ref[0]
    p = _dot(norm(x_ref[...]), w_ref[...])
    n_conv = sum(P_WIDTHS[:P_CONV_GROUPS])
    ph = _dot(norm(jnp.concatenate([xp_ref[...], xn_ref[...]], axis=0)), w_ref[:, :n_conv])
    pos = i % tiles_per_seq
    prev = jnp.where(pos == 0, 0.0, ph[:SUBLANES])
    nxt = jnp.where(pos == tiles_per_seq - 1, 0.0, ph[SUBLANES:])
    row = lax.broadcasted_iota(jnp.int32, (SUBLANES, n_conv), 0)
    halo = jnp.where(row < 2, pltpu.roll(prev, 2, 0), jnp.where(row == 2, pltpu.roll(nxt, 2, 0), 0.0))
    c0, c1, c2 = GROUP_W, GROUP_W + 3 * GROUP_W, n_conv
    lru_u = _dwconv(p[:, :c0], halo[:, :c0], lcw_ref[...], lcb_ref[...])
    ssd_x = _silu(_dwconv(p[:, c0:c1], halo[:, c0:c1], scw_ref[...], scb_ref[...]))
    qkv = _silu(_dwconv(p[:, c1:c2], halo[:, c1:c2], gcw_ref[...]))
    qn = _l2norm_heads(qkv[:, :GROUP_W])
    kn = _l2norm_heads(qkv[:, GROUP_W:2 * GROUP_W])
    if rope:
        cos, sin = cos_ref[...], sin_ref[...]
        qn = qn * cos + _swap16(qn) * sin
        kn = kn * cos + _swap16(kn) * sin
    outs = [lru_u, ssd_x, jnp.concatenate([qn * (HEAD_DIM ** -0.5), kn, qkv[:, 2 * GROUP_W:]], axis=1)]
    off = n_conv
    for o_ref, w in zip(o_refs, P_WIDTHS):
        if outs:
            o_ref[...] = outs.pop(0)
        else:
            o_ref[...] = p[:, off:off + w].astype(o_ref.dtype)
            off += w


def in_projection(x, g, sc, sh, w, conv, rope, seq_len, tiles_per_group):
    t, d = x.shape
    tm = min(TOKEN_TILE, seq_len)
    tps = seq_len // tm
    hb = tm // SUBLANES
    vec = lambda i: (i // tiles_per_group, 0, 0)
    ins = [x, x, x, g, sc, sh, w, *conv]
    specs = [pl.BlockSpec((tm, d), lambda i: (i, 0)),
             pl.BlockSpec((SUBLANES, d), lambda i: (jnp.maximum(i * hb - 1, 0), 0)),
             pl.BlockSpec((SUBLANES, d), lambda i: (jnp.minimum((i + 1) * hb, t // SUBLANES - 1), 0)),
             _full((1, d)), pl.BlockSpec((1, 1, d), vec), pl.BlockSpec((1, 1, d), vec), _full(w.shape)]
    specs += [_full(a.shape) for a in conv]
    if rope is not None:
        ins += list(rope)
        specs += [pl.BlockSpec((tm, GROUP_W), lambda i: (i % tps, 0))] * 2
    return pl.pallas_call(
        functools.partial(_inproj_kernel, tiles_per_seq=tps, rope=rope is not None),
        grid=(t // tm,),
        in_specs=specs,
        out_specs=[pl.BlockSpec((tm, wd), lambda i: (i, 0)) for wd in P_WIDTHS],
        out_shape=[jax.ShapeDtypeStruct((t, wd), MXU_DTYPE if k in (P_BQ, P_BK, P_BV) else F32)
                   for k, wd in enumerate(P_WIDTHS)],
        compiler_params=_cp("parallel"),
        name="in_projection",
    )(*ins)


def _dwconv(x, halo, w, b=None):
    q = x.shape[0]
    row = lax.broadcasted_iota(jnp.int32, (SUBLANES, x.shape[1]), 0)

    def shifted(s, keep_rolled, edge):
        r = pltpu.roll(x, s % q, 0)
        if s > 0:
            return jnp.concatenate([jnp.where(keep_rolled, r[:SUBLANES], edge), r[SUBLANES:]], axis=0)
        return jnp.concatenate([r[:q - SUBLANES], jnp.where(keep_rolled, r[q - SUBLANES:], edge)], axis=0)

    xm2 = shifted(2, row >= 2, halo)
    xm1 = shifted(1, row >= 1, pltpu.roll(halo, SUBLANES - 1, 0))
    xp1 = shifted(-1, row < SUBLANES - 1, pltpu.roll(halo, SUBLANES - 3, 0))
    y = w[0:1] * xm2 + w[1:2] * xm1 + w[2:3] * x + w[3:4] * xp1
    return y if b is None else y + b


def _pad_rows(a, rows=SUBLANES):
    return jnp.concatenate([a, jnp.zeros((rows - a.shape[0],) + a.shape[1:], a.dtype)], axis=0)


def _chunk_specs(nc, q, c):
    fwd = pl.BlockSpec((q, c), lambda b, i: (b * nc + i, 0))
    bwd = pl.BlockSpec((q, c), lambda b, i: (b * nc + nc - 1 - i, 0))
    return fwd, bwd


def _lru_kernel(xf_ref, xb_ref, h0_ref, wg_ref, bg_ref, lam_ref,
                yf_ref, yb_ref, hfin_ref, af_s, bf_s, ab_s, bb_s, carry_s):
    i = pl.program_id(1)
    q = xf_ref.shape[0]

    @pl.when(i == 0)
    def _():
        carry_s[...] = h0_ref[0]

    def coeffs(x_ref, d, a_s, b_s):
        u = x_ref[...]
        g = _dot(u, wg_ref[:, 2 * GROUP_W * d:2 * GROUP_W * (d + 1)]) + bg_ref[:, 2 * GROUP_W * d:2 * GROUP_W * (d + 1)]
        r = _sigmoid(g[:, :GROUP_W])
        gate_in = _sigmoid(g[:, GROUP_W:])
        log_a = -LRU_C * r * _softplus(-lam_ref[d:d + 1, :])
        a_s[...] = jnp.exp(log_a)
        b_s[...] = jnp.sqrt(1.0 - jnp.exp(2.0 * log_a)) * (gate_in * u)

    coeffs(xf_ref, 0, af_s, bf_s)
    coeffs(xb_ref, 1, ab_s, bb_s)

    ng = q // SUBLANES
    row = lax.broadcasted_iota(jnp.int32, (SUBLANES, GROUP_W), 0)

    def body(g, hs):
        h_f, h_b = hs
        i0 = pl.multiple_of(g * SUBLANES, SUBLANES)
        a = af_s[pl.ds(i0, SUBLANES), :]
        b = bf_s[pl.ds(i0, SUBLANES), :]
        for s in (1, 2, 4):
            m = row >= s
            b = jnp.where(m, a * pltpu.roll(b, s, 0) + b, b)
            a = jnp.where(m, a * pltpu.roll(a, s, 0), a)
        h = b + a * h_f
        yf_ref[pl.ds(i0, SUBLANES), :] = h
        h_f = h[SUBLANES - 1:SUBLANES, :]
        j0 = pl.multiple_of((ng - 1 - g) * SUBLANES, SUBLANES)
        a = ab_s[pl.ds(j0, SUBLANES), :]
        b = bb_s[pl.ds(j0, SUBLANES), :]
        for s in (1, 2, 4):
            m = row < SUBLANES - s
            b = jnp.where(m, a * pltpu.roll(b, SUBLANES - s, 0) + b, b)
            a = jnp.where(m, a * pltpu.roll(a, SUBLANES - s, 0), a)
        h = b + a * h_b
        yb_ref[pl.ds(j0, SUBLANES), :] = h
        return h_f, h[0:1, :]

    h_f, h_b = lax.fori_loop(0, ng, body, (carry_s[0:1, :], carry_s[1:2, :]), unroll=LRU_UNROLL)
    carry_s[0:1, :] = h_f
    carry_s[1:2, :] = h_b

    @pl.when(i == pl.num_programs(1) - 1)
    def _():
        hfin_ref[0] = carry_s[...]


def _block_diag(w):
    h, a, b = w.shape
    return jnp.einsum('hij,hg->higj', w, jnp.eye(h, dtype=w.dtype)).reshape(h * a, h * b)


def lru_params(wa, ba, wx, bx, lam):
    wg = jnp.concatenate([_block_diag(wa[0]), _block_diag(wx[0]), _block_diag(wa[1]), _block_diag(wx[1])], axis=1)
    bg = jnp.concatenate([ba[0], bx[0], ba[1], bx[1]])[None, :]
    return wg.astype(MXU_DTYPE), bg, _pad_rows(lam)


def lru_mixer(x, h0, bsz, wg, bg, lam):
    t, c = x.shape
    s = t // bsz
    q = min(LRU_CHUNK, s)
    nc = s // q
    xf, xb = _chunk_specs(nc, q, c)
    st = pl.BlockSpec((1, SUBLANES, c), lambda b, i: (b, 0, 0))
    return pl.pallas_call(
        _lru_kernel,
        grid=(bsz, nc),
        in_specs=[xf, xb, st, _full(wg.shape), _full(bg.shape), _full(lam.shape)],
        out_specs=[xf, xb, st],
        out_shape=[jax.ShapeDtypeStruct((t, c), F32), jax.ShapeDtypeStruct((t, c), F32),
                   jax.ShapeDtypeStruct((bsz, SUBLANES, c), F32)],
        scratch_shapes=[pltpu.VMEM((q, c), F32)] * 4 + [pltpu.VMEM((SUBLANES, c), F32)],
        compiler_params=_cp("parallel", "arbitrary"),
        name="lru_mixer",
    )(x, x, h0, wg, bg, lam)


NA_KEYS = NA_WIN_ROWS * GRID_W
NA_ROW_BLOCK = 8


def na_bias_slabs(table):
    qc = np.arange(GRID_W)[:, None]
    kc = np.arange(GRID_W)[None, :]
    win0 = np.clip(qc - NA_WIN_COLS // 2, 0, GRID_W - NA_WIN_COLS)
    ok = (kc >= win0) & (kc < win0 + NA_WIN_COLS)
    dc = np.clip(kc - qc + NA_WIN_COLS - 1, 0, 2 * NA_WIN_COLS - 2)
    dr = np.arange(NA_WIN_ROWS)[:, None] + np.arange(NA_WIN_ROWS)[None, :]
    b = table.astype(F32)[:, dr][:, :, :, dc]
    b = jnp.where(ok[None, None, None], b, NEG_INF)
    h = table.shape[0]
    return b.transpose(0, 1, 3, 2, 4).reshape(h, NA_WIN_ROWS, GRID_W, NA_KEYS)


def _na_span_start(j, rows):
    return jnp.clip(j * NA_ROW_BLOCK - NA_WIN_ROWS // 2, 0, rows - (NA_ROW_BLOCK + NA_WIN_ROWS - 1))


def _na_kernel(q_ref, kw_ref, vw_ref, kc_ref, vc_ref, slab_ref, o_ref, *, rows):
    j = pl.program_id(1)
    ustart = _na_span_start(j, rows)
    q = q_ref[...] * (HEAD_DIM ** -0.5)
    kc, vc = kc_ref[0], vc_ref[0]
    heads = [slice(h * HEAD_DIM, (h + 1) * HEAD_DIM) for h in range(N_HEADS)]
    qrows = [slice(i * GRID_W, (i + 1) * GRID_W) for i in range(NA_ROW_BLOCK)]
    kws, vws, offs = [], [], []
    for i in range(NA_ROW_BLOCK):
        r = j * NA_ROW_BLOCK + i
        r0 = jnp.clip(r - NA_WIN_ROWS // 2, 0, rows - NA_WIN_ROWS)
        start = pl.multiple_of((r0 - ustart) * GRID_W, GRID_W)
        kws.append(kw_ref[pl.ds(start, NA_KEYS), :])
        vws.append(vw_ref[pl.ds(start, NA_KEYS), :])
        offs.append(r0 - r + NA_WIN_ROWS - 1)
    s_ctx = [_dot_nt(q[:, sl], kc[:, sl]) for sl in heads]
    s_loc = [[_dot_nt(q[qr, sl], kws[i][:, sl]) + slab_ref[h, offs[i]] for h, sl in enumerate(heads)]
             for i, qr in enumerate(qrows)]
    m = [[jnp.maximum(jnp.max(s_loc[i][h], axis=-1, keepdims=True), jnp.max(s_ctx[h][qr], axis=-1, keepdims=True))
          for h in range(N_HEADS)] for i, qr in enumerate(qrows)]
    p_loc = [[jnp.exp(s_loc[i][h] - m[i][h]) for h in range(N_HEADS)] for i in range(NA_ROW_BLOCK)]
    p_ctx = [jnp.exp(s_ctx[h] - jnp.concatenate([m[i][h] for i in range(NA_ROW_BLOCK)], axis=0))
             for h in range(N_HEADS)]
    o_ctx = [_dot(p_ctx[h], vc[:, sl]) for h, sl in enumerate(heads)]
    rows_out = []
    for i, qr in enumerate(qrows):
        outs = []
        for h, sl in enumerate(heads):
            den = jnp.sum(p_loc[i][h], axis=-1, keepdims=True) + jnp.sum(p_ctx[h][qr], axis=-1, keepdims=True)
            outs.append((_dot(p_loc[i][h], vws[i][:, sl]) + o_ctx[h][qr]) / den)
        rows_out.append(jnp.concatenate(outs, axis=1))
    o_ref[...] = jnp.concatenate(rows_out, axis=0)


def na_mixer(q, k, v, kc, vc, slabs, bsz):
    t, c = q.shape
    s = t // bsz
    rows = s // GRID_W
    n_ctx = kc.shape[1]
    span = (NA_ROW_BLOCK + NA_WIN_ROWS - 1) * GRID_W

    def win(b, j):
        return ((b * rows + _na_span_start(j, rows)) * GRID_W, 0)

    wspec = pl.BlockSpec((pl.Element(span), pl.Element(c)), win)
    cspec = pl.BlockSpec((1, n_ctx, c), lambda b, j: (b, 0, 0))
    qspec = pl.BlockSpec((NA_ROW_BLOCK * GRID_W, c), lambda b, j: (b * (rows // NA_ROW_BLOCK) + j, 0))
    return pl.pallas_call(
        functools.partial(_na_kernel, rows=rows),
        grid=(bsz, rows // NA_ROW_BLOCK),
        in_specs=[qspec, wspec, wspec, cspec, cspec, _full(slabs.shape)],
        out_specs=qspec,
        out_shape=jax.ShapeDtypeStruct((t, c), F32),
        compiler_params=_cp("parallel", "arbitrary"),
        name="na_mixer",
    )(q, k, v, kc, vc, slabs)


def _ctx_attn_kernel(q_ref, k_ref, v_ref, o_ref):
    q = q_ref[0] * (HEAD_DIM ** -0.5)
    k, v = k_ref[0], v_ref[0]
    outs = []
    for h in range(N_HEADS):
        sl = slice(h * HEAD_DIM, (h + 1) * HEAD_DIM)
        s = _dot_nt(q[:, sl], k[:, sl])
        p = jnp.exp(s - jnp.max(s, axis=-1, keepdims=True))
        outs.append(_dot(p, v[:, sl]) / jnp.sum(p, axis=-1, keepdims=True))
    o_ref[0] = jnp.concatenate(outs, axis=1)


def ctx_attention(q, k, v):
    spec = pl.BlockSpec((1,) + q.shape[1:], lambda b: (b, 0, 0))
    return pl.pallas_call(
        _ctx_attn_kernel,
        grid=(q.shape[0],),
        in_specs=[spec, spec, spec],
        out_specs=spec,
        out_shape=jax.ShapeDtypeStruct(q.shape, F32),
        compiler_params=_cp("parallel"),
        name="ctx_attention",
    )(q, k, v)


def _small_vec(vals, off):
    v = jnp.zeros((LANES,), F32).at[off:off + 2 * N_HEADS].set(vals.reshape(-1).astype(F32))
    return v[None, :]


def _lane_mask(off):
    lane = lax.broadcasted_iota(jnp.int32, (1, LANES), 1)
    return (lane >= off) & (lane < off + 2 * N_HEADS)


def _tri_masks(q):
    rowi = lax.broadcasted_iota(jnp.int32, (q, q), 0)
    coli = lax.broadcasted_iota(jnp.int32, (q, q), 1)
    return rowi, coli


def _ssd_kernel(xf_ref, xb_ref, sf_ref, sb_ref, h0_ref, dtb_ref, alog_ref,
                yf_ref, yb_ref, hfin_ref, state_s):
    i = pl.program_id(1)
    q = xf_ref.shape[0]

    @pl.when(i == 0)
    def _():
        state_s[...] = h0_ref[0]

    rowi, coli = _tri_masks(q)
    a_neg = jnp.where(_lane_mask(SM_DT), -jnp.exp(alog_ref[...]), 0.0)

    chains = [(d, h) for d in range(2) for h in range(N_HEADS)]
    per_head = N_HEADS // SSD_GROUPS
    scores, xdt, c_in, b_out, e_last = [], [], [], [], []
    for d, (x_ref, sm_ref) in enumerate(((xf_ref, sf_ref), (xb_ref, sb_ref))):
        xbc = x_ref[...]
        dt = _softplus(sm_ref[...] + dtb_ref[...])
        keep = (rowi >= coli) if d == 0 else (rowi <= coli)
        acum = _dot_tri(keep, dt * a_neg)
        acum_t = acum.T
        last = acum[q - 1:q, :] if d == 0 else acum[0:1, :]
        dec_end = jnp.exp(last - acum)
        e_acum = jnp.exp(acum)
        e_end = jnp.exp(last)
        bgs = [xbc[:, GROUP_W + SSD_STATE * g:GROUP_W + SSD_STATE * (g + 1)] for g in range(SSD_GROUPS)]
        cgs = [xbc[:, GROUP_W + SSD_STATE * (SSD_GROUPS + g):GROUP_W + SSD_STATE * (SSD_GROUPS + g + 1)]
               for g in range(SSD_GROUPS)]
        cbt = [_dot_nt(cg, bg) for cg, bg in zip(cgs, bgs)]
        for h in range(N_HEADS):
            g = h // per_head
            ln = SM_DT + N_HEADS * d + h
            lmat = jnp.exp(jnp.where(keep, acum[:, ln:ln + 1] - acum_t[ln:ln + 1, :], NEG_INF))
            scores.append(cbt[g] * lmat)
            xdt.append(xbc[:, h * HEAD_DIM:(h + 1) * HEAD_DIM] * dt[:, ln:ln + 1])
            c_in.append(cgs[g] * e_acum[:, ln:ln + 1])
            b_out.append(bgs[g] * dec_end[:, ln:ln + 1])
            e_last.append(e_end[:, ln:ln + 1])
    states = [state_s[d, h] for d, h in chains]
    y_diag = [_dot(s, x) for s, x in zip(scores, xdt)]
    y_off = [_dot(c, st) for c, st in zip(c_in, states)]
    upd = [_dot_tn(b, x) for b, x in zip(b_out, xdt)]
    for n, (d, h) in enumerate(chains):
        state_s[d, h] = states[n] * e_last[n] + upd[n]
    ys = [a + b for a, b in zip(y_diag, y_off)]
    yf_ref[...] = jnp.concatenate(ys[:N_HEADS], axis=1)
    yb_ref[...] = jnp.concatenate(ys[N_HEADS:], axis=1)

    @pl.when(i == pl.num_programs(1) - 1)
    def _():
        hfin_ref[0] = state_s[...]


def ssd_params(a_log, dt_bias):
    return _small_vec(dt_bias, SM_DT), _small_vec(a_log, SM_DT)


def ssd_mixer(xbc, sm, h0, bsz, dtb, alog):
    t, c = xbc.shape
    s = t // bsz
    q = min(SSD_CHUNK, s)
    nc = s // q
    xf, xb = _chunk_specs(nc, q, c)
    sf, sb = _chunk_specs(nc, q, LANES)
    yf, yb = _chunk_specs(nc, q, GROUP_W)
    st = pl.BlockSpec((1,) + h0.shape[1:], lambda b, i: (b, 0, 0, 0, 0))
    y_shape = jax.ShapeDtypeStruct((t, GROUP_W), F32)
    return pl.pallas_call(
        _ssd_kernel,
        grid=(bsz, nc),
        in_specs=[xf, xb, sf, sb, st, _full(dtb.shape), _full(alog.shape)],
        out_specs=[yf, yb, st],
        out_shape=[y_shape, y_shape, jax.ShapeDtypeStruct(h0.shape, F32)],
        scratch_shapes=[pltpu.VMEM(h0.shape[1:], F32)],
        compiler_params=_cp("parallel", "arbitrary"),
        name="ssd_mixer",
    )(xbc, xbc, sm, sm, h0, dtb, alog)


def rope_tables(seq):
    t = jnp.arange(seq)
    row = (t // GRID_W).astype(F32)
    col = (t % GRID_W).astype(F32)
    inv = ROPE_BASE ** (-jnp.arange(0, ROPE_AXIS_DIM, 2, dtype=F32) / ROPE_AXIS_DIM)
    ar, ac = row[:, None] * inv, col[:, None] * inv
    cos = jnp.concatenate([jnp.cos(ar), jnp.cos(ar), jnp.cos(ac), jnp.cos(ac)], axis=1)
    sin = jnp.concatenate([-jnp.sin(ar), jnp.sin(ar), -jnp.sin(ac), jnp.sin(ac)], axis=1)
    return jnp.tile(cos, (1, N_HEADS)), jnp.tile(sin, (1, N_HEADS))


def _swap16(x):
    lane = lax.broadcasted_iota(jnp.int32, x.shape, 1)
    half = ROPE_AXIS_DIM // 2
    return jnp.where((lane & (ROPE_AXIS_DIM - 1)) < half,
                     pltpu.roll(x, x.shape[1] - half, 1), pltpu.roll(x, half, 1))


def _head_sums(sq):
    c = sq.shape[1]
    li = lax.broadcasted_iota(jnp.int32, (c, c), 0)
    lj = lax.broadcasted_iota(jnp.int32, (c, c), 1)
    sh = HEAD_DIM.bit_length() - 1
    ones = _mx(((li >> sh) == (lj >> sh)).astype(F32))
    hi = _mx(sq)
    lo = _mx(sq - hi.astype(F32))
    return jnp.dot(hi, ones, preferred_element_type=F32) + jnp.dot(lo, ones, preferred_element_type=F32)


def _l2norm_heads(x):
    return x * lax.rsqrt(_head_sums(x * x) + EPS)


def _head_columns(x, off):
    li = lax.broadcasted_iota(jnp.int32, (LANES, N_HEADS * HEAD_DIM), 0)
    lj = lax.broadcasted_iota(jnp.int32, (LANES, N_HEADS * HEAD_DIM), 1)
    pick = _mx((li == off + (lj >> (HEAD_DIM.bit_length() - 1))).astype(F32))
    hi = _mx(x)
    lo = _mx(x - hi.astype(F32))
    return jnp.dot(hi, pick, preferred_element_type=F32) + jnp.dot(lo, pick, preferred_element_type=F32)


def _dot_tri(mask, x):
    m = _mx(mask.astype(F32))
    x1 = _mx(x)
    r1 = x - x1.astype(F32)
    x2 = _mx(r1)
    x3 = _mx(r1 - x2.astype(F32))
    return (jnp.dot(m, x1, preferred_element_type=F32) + jnp.dot(m, x2, preferred_element_type=F32)
            + jnp.dot(m, x3, preferred_element_type=F32))


def _same_block(rowi, coli, n):
    sh = n.bit_length() - 1
    return (rowi >> sh) == (coli >> sh)


def _solve_unit_tri(a_list, rhs_list, rowi, coli, chunk):
    mm = lambda x, y: jnp.dot(x, y, preferred_element_type=F32)
    eye = (rowi == coli).astype(F32)
    in_base = _same_block(rowi, coli, GDN_BASE)
    base = [_mx(jnp.where(in_base, a, 0.0)) for a in a_list]
    ts = [jnp.where(in_base, eye - a, 0.0) for a in a_list]
    ps = [_mx(mm(b, b)) for b in base]
    ts = [t + mm(_mx(t), p) for t, p in zip(ts, ps)]
    n = 4
    while n < GDN_BASE:
        ps = [_mx(mm(p, p)) for p in ps]
        ts = [t + mm(_mx(t), p) for t, p in zip(ts, ps)]
        n *= 2
    n = GDN_BASE
    while 2 * n < chunk:
        inner = _same_block(rowi, coli, 2 * n) & jnp.logical_not(_same_block(rowi, coli, n))
        offs = [_mx(jnp.where(inner, a, 0.0)) for a in a_list]
        tb = [_mx(t) for t in ts]
        ms = [_mx(mm(t, off)) for t, off in zip(tb, offs)]
        ts = [t - mm(m, t_b) for t, m, t_b in zip(ts, ms, tb)]
        n *= 2
    outer = jnp.logical_not(_same_block(rowi, coli, n))
    offs = [_mx(jnp.where(outer, a, 0.0)) for a in a_list]
    tb = [_mx(t) for t in ts]
    ys = [mm(t, _mx(r)) for t, r in zip(tb, rhs_list)]
    zs = [_mx(mm(off, _mx(y))) for off, y in zip(offs, ys)]
    return [y - mm(t, z) for y, t, z in zip(ys, tb, zs)]


def _gdn_kernel(xf_ref, xb_ref, sf_ref, sb_ref, s0_ref, alog_ref, dtb_ref, of_ref, ob_ref, sfin_ref, state_s):
    i = pl.program_id(1)
    tq = xf_ref.shape[0]
    ck = min(GDN_CHUNK, tq)
    nck = tq // ck

    @pl.when(i == 0)
    def _():
        state_s[...] = s0_ref[0]

    sub = min(GDN_SUB, tq)
    nsub = tq // sub
    rowt, colt = _tri_masks(tq)
    in_chunk_t = _same_block(rowt, colt, ck)
    rowi, coli = _tri_masks(sub)
    in_chunk = _same_block(rowi, coli, ck)
    a_neg = jnp.where(_lane_mask(SM_DECAY), -jnp.exp(alog_ref[...]), 0.0)

    a_list, rhs_list, qkm, qg, kd, e_last = [], [], [], [], [], []
    for d, (x_ref, sm_ref) in enumerate(((xf_ref, sf_ref), (xb_ref, sb_ref))):
        qkv = x_ref[...]
        qn, kn, v = qkv[:, :GROUP_W], qkv[:, GROUP_W:2 * GROUP_W], qkv[:, 2 * GROUP_W:]
        sm = sm_ref[...]
        beta = _sigmoid(sm)
        keep_t = in_chunk_t & ((rowt >= colt) if d == 0 else (rowt <= colt))
        keep = in_chunk & ((rowi >= coli) if d == 0 else (rowi <= coli))
        strict = in_chunk & ((rowi > coli) if d == 0 else (rowi < coli))
        gc = _dot_tri(keep_t, _softplus(sm + dtb_ref[...]) * a_neg)
        gc_t = gc.T
        edge = ck - 1 if d == 0 else 0
        last = jnp.concatenate([jnp.broadcast_to(gc[c * ck + edge:c * ck + edge + 1, :], (ck, LANES))
                                for c in range(nck)], axis=0)
        e_last.append(jnp.exp(last))
        beta_w = _head_columns(beta, SM_BETA + N_HEADS * d)
        e_gc_w = _head_columns(jnp.exp(gc), SM_DECAY + N_HEADS * d)
        e_end_w = _head_columns(jnp.exp(last - gc), SM_DECAY + N_HEADS * d)
        kb_w = kn * beta_w
        vb_w = v * beta_w
        kbe_w = kb_w * e_gc_w
        qg_w = qn * e_gc_w
        kd_w = kn * e_end_w
        qn_m, kn_m, kb_wm = _mx(qn), _mx(kn), _mx(kb_w)
        for h in range(N_HEADS):
            sl = slice(h * HEAD_DIM, (h + 1) * HEAD_DIM)
            lg = SM_DECAY + N_HEADS * d + h
            rhs = jnp.concatenate([vb_w[:, sl], kbe_w[:, sl]], axis=1)
            qg.append(qg_w[:, sl])
            kd.append(kd_w[:, sl])
            qh_m, kh_m, kb_m = qn_m[:, sl], kn_m[:, sl], kb_wm[:, sl]
            for s in range(nsub):
                rs = slice(s * sub, (s + 1) * sub)
                decay = jnp.exp(jnp.where(keep, gc[rs, lg:lg + 1] - gc_t[lg:lg + 1, rs], NEG_INF))
                a_list.append(jnp.where(strict, _dot_nt(kb_m[rs], kh_m[rs]) * decay, 0.0))
                rhs_list.append(rhs[rs])
                qkm.append(_dot_nt(qh_m[rs], kh_m[rs]) * decay)
    sols = _solve_unit_tri(a_list, rhs_list, rowi, coli, ck)
    sols = [jnp.concatenate(sols[n * nsub:(n + 1) * nsub], axis=0) for n in range(2 * N_HEADS)]

    chains = [(d, h) for d in range(2) for h in range(N_HEADS)]
    states = [state_s[d, h] for d, h in chains]
    v_new = [[None] * nck for _ in chains]
    o_st = [[None] * nck for _ in chains]
    for step in range(nck):
        rows = [slice((step if d == 0 else nck - 1 - step) * ck, (step if d == 0 else nck - 1 - step) * ck + ck)
                for d, _ in chains]
        ms = [_dot(jnp.concatenate([sols[n][r, HEAD_DIM:], qg[n][r]], axis=0), states[n])
              for n, r in enumerate(rows)]
        for n, (d, _) in enumerate(chains):
            c = step if d == 0 else nck - 1 - step
            v_new[n][c] = sols[n][rows[n], :HEAD_DIM] - ms[n][:ck]
            o_st[n][c] = ms[n][ck:]
        ups = [_dot_tn(kd[n][r], v_new[n][step if chains[n][0] == 0 else nck - 1 - step])
               for n, r in enumerate(rows)]
        for n, (d, h) in enumerate(chains):
            lg = SM_DECAY + N_HEADS * d + h
            states[n] = states[n] * e_last[d][rows[n].start:rows[n].start + 1, lg:lg + 1] + ups[n]
    cps = sub // ck
    outs = [jnp.concatenate(o_st[n], axis=0)
            + jnp.concatenate([_dot(qkm[n * nsub + s], jnp.concatenate(v_new[n][s * cps:(s + 1) * cps], axis=0))
                               for s in range(nsub)], axis=0)
            for n in range(len(chains))]
    of_ref[...] = jnp.concatenate(outs[:N_HEADS], axis=1)
    ob_ref[...] = jnp.concatenate(outs[N_HEADS:], axis=1)
    for n, (d, h) in enumerate(chains):
        state_s[d, h] = states[n]

    @pl.when(i == pl.num_programs(1) - 1)
    def _():
        sfin_ref[0] = state_s[...]


def gdn_params(a_log, dt_bias):
    return _small_vec(a_log, SM_DECAY), _small_vec(dt_bias, SM_DECAY)


def gdn_mixer(qkv, sm, s0, bsz, alog, dtb):
    t, c = qkv.shape
    s = t // bsz
    q = min(GDN_TILE, s)
    nc = s // q
    xf, xb = _chunk_specs(nc, q, c)
    sf, sb = _chunk_specs(nc, q, LANES)
    of, ob = _chunk_specs(nc, q, GROUP_W)
    st = pl.BlockSpec((1,) + s0.shape[1:], lambda b, i: (b, 0, 0, 0, 0))
    o_shape = jax.ShapeDtypeStruct((t, GROUP_W), F32)
    return pl.pallas_call(
        _gdn_kernel,
        grid=(bsz, nc),
        in_specs=[xf, xb, sf, sb, st, _full(alog.shape), _full(dtb.shape)],
        out_specs=[of, ob, st],
        out_shape=[o_shape, o_shape, jax.ShapeDtypeStruct(s0.shape, F32)],
        scratch_shapes=[pltpu.VMEM(s0.shape[1:], F32)],
        compiler_params=_cp("parallel", "arbitrary"),
        name="gdn_mixer",
    )(qkv, qkv, sm, sm, s0, alog, dtb)


def _split_hi_lo(a):
    hi = _mx(a)
    return hi, _mx(a - hi.astype(F32))


def _outproj_kernel(x_ref, ahf_ref, ahb_ref, ag_ref, bo_ref, cyf_ref, cyb_ref, cxc_ref, cz_ref,
                    dof_ref, dob_ref, dz_ref, wout_ref, gpost_ref, ga1_ref, gpre_ref, sc2_ref, sh2_ref,
                    dskip_ref, cnorm_ref, dnorm_ref, rhi_ref, rlo_ref, xo_ref, hp_ref, lg_ref):
    m_a = (ahf_ref[...] + ahb_ref[...]) * _gelu_tanh(ag_ref[...])
    y_c = (cyf_ref[...] + cyb_ref[...] + cxc_ref[...] * dskip_ref[...]) * _silu(cz_ref[...])
    m_c = _rms(y_c, cnorm_ref[...])
    o_d = dof_ref[...] + dob_ref[...]
    m_d = o_d * lax.rsqrt(_head_sums(o_d * o_d) * (1.0 / HEAD_DIM) + EPS) * dnorm_ref[...] * _silu(dz_ref[...])
    mix = jnp.concatenate([_mx(m_a), _mx(bo_ref[...]), _mx(m_c), _mx(m_d)], axis=1)
    ml = jnp.dot(mix, wout_ref[...], preferred_element_type=F32)
    x_new = x_ref[...] + ga1_ref[0] * _rms(ml, gpost_ref[...])
    xo_ref[...] = x_new
    h2 = _rms(x_new, gpre_ref[...]) * (1.0 + sc2_ref[0]) + sh2_ref[0]
    hi, lo = _split_hi_lo(h2)
    hp_ref[...] = _pack_pairs(h2)
    rhi = rhi_ref[...]
    lg_ref[...] = (jnp.dot(hi, rhi, preferred_element_type=F32) + jnp.dot(lo, rhi, preferred_element_type=F32)
                   + jnp.dot(hi, rlo_ref[...], preferred_element_type=F32))


def out_projection(x, mixers, w_out, gpost, ga1, gpre, sc2, sh2, dskip, cnorm, dnorm, router_w, tiles_per_group):
    t, d = x.shape
    tm = min(TOKEN_TILE, t)
    vec = lambda i: (i // tiles_per_group, 0, 0)
    row = lambda w: pl.BlockSpec((tm, w), lambda i: (i, 0))
    ne = LANES
    rhi, rlo = _split_hi_lo(jnp.pad(router_w.astype(F32), ((0, 0), (0, ne - router_w.shape[1]))))
    return pl.pallas_call(
        _outproj_kernel,
        grid=(t // tm,),
        in_specs=[row(d)] + [row(GROUP_W)] * 11
                 + [_full(w_out.shape), _full((1, d)), pl.BlockSpec((1, 1, d), vec), _full((1, d)),
                    pl.BlockSpec((1, 1, d), vec), pl.BlockSpec((1, 1, d), vec),
                    _full((1, GROUP_W)), _full((1, GROUP_W)), _full((1, GROUP_W)), _full(rhi.shape), _full(rlo.shape)],
        out_specs=[row(d), row(d // 2), row(ne)],
        out_shape=[jax.ShapeDtypeStruct((t, d), F32), jax.ShapeDtypeStruct((t, d // 2), jnp.uint32),
                   jax.ShapeDtypeStruct((t, ne), F32)],
        compiler_params=_cp("parallel"),
        name="out_projection",
    )(x, *mixers, w_out, gpost, ga1, gpre, sc2, sh2, dskip, cnorm, dnorm, rhi, rlo)


def _rank_before(vals, idx, count, stride):
    rank = jnp.zeros(vals.shape, jnp.int32)
    for j in range(count):
        other = vals[j * stride:j * stride + 1, :]
        ahead = (other > vals) | ((other == vals) & (idx > j))
        rank = rank + ahead.astype(jnp.int32)
    return rank


def _xor_partner(x, row, s):
    n = x.shape[0]
    return jnp.where((row & s) == 0, pltpu.roll(x, n - s, 0), pltpu.roll(x, s, 0))


def _route(logits, router_b):
    ne = N_EXPERTS
    gsz = ne // N_EXPERT_GROUPS
    scores = _sigmoid(logits.T[:ne, :])
    tm = scores.shape[1]
    biased = scores + router_b
    row = lax.broadcasted_iota(jnp.int32, (ne, tm), 0)
    m1, m2 = biased, jnp.full((ne, tm), -jnp.inf, F32)
    s = 1
    while s < gsz:
        o1, o2 = _xor_partner(m1, row, s), _xor_partner(m2, row, s)
        m2 = jnp.maximum(jnp.minimum(m1, o1), jnp.maximum(m2, o2))
        m1 = jnp.maximum(m1, o1)
        s *= 2
    gidx = row >> (gsz.bit_length() - 1)
    group_ok = _rank_before(m1 + m2, gidx, N_EXPERT_GROUPS, gsz) < TOPK_GROUPS
    choice = jnp.where(group_ok, biased, -jnp.inf)
    rank = jnp.full((ne, tm), TOP_K, jnp.int32)
    rest = choice
    for k in range(TOP_K):
        top = jnp.max(rest, axis=0, keepdims=True)
        first = jnp.min(jnp.where(rest == top, row, ne), axis=0, keepdims=True)
        hit = row == first
        rank = jnp.where(hit, k, rank)
        rest = jnp.where(hit, -jnp.inf, rest)
    gate = jnp.where(rank < TOP_K, scores, 0.0)
    gate = gate / jnp.sum(gate, axis=0, keepdims=True) * ROUTED_SCALE
    return gate, rank, row


def _to_token_major(x):
    n, tm = x.shape
    return jnp.concatenate([x, jnp.zeros((LANES - n, tm), x.dtype)], axis=0).T


def _router_kernel(lg_ref, rb_ref, gate_ref):
    gate, _, _ = _route(lg_ref[...], rb_ref[...])
    gate_ref[...] = _to_token_major(gate)


def _router_dispatch_kernel(lg_ref, rb_ref, gk_ref, ek_ref, pk_ref, cnt_ref, carry_s):
    i = pl.program_id(0)

    @pl.when(i == 0)
    def _():
        carry_s[...] = jnp.zeros(carry_s.shape, F32)

    gate, rank, row = _route(lg_ref[...], rb_ref[...])
    tm = gate.shape[1]
    picked = (rank < TOP_K).astype(F32)
    before = lax.broadcasted_iota(jnp.int32, (tm, tm), 0) < lax.broadcasted_iota(jnp.int32, (tm, tm), 1)
    pos = _dot(picked, before.astype(F32)) + carry_s[:, 0:1]
    carry_s[...] = carry_s[...] + jnp.sum(picked, axis=1, keepdims=True)
    gk, ek, pk = [], [], []
    for k in range(TOP_K):
        sel = rank == k
        gk.append(jnp.sum(jnp.where(sel, gate, 0.0), axis=0, keepdims=True))
        ek.append(jnp.sum(jnp.where(sel, row, 0), axis=0, keepdims=True))
        pk.append(jnp.sum(jnp.where(sel, pos, 0.0), axis=0, keepdims=True))
    gk_ref[...] = _to_token_major(jnp.concatenate(gk, axis=0))
    ek_ref[...] = jnp.concatenate(ek, axis=0)
    pk_ref[...] = jnp.concatenate(pk, axis=0).astype(jnp.int32)

    @pl.when(i == pl.num_programs(0) - 1)
    def _():
        cnt_ref[...] = carry_s[...].astype(jnp.int32)


def router_dispatch(logits, router_b):
    t, w = logits.shape
    tm = min(TOKEN_TILE, t)
    return pl.pallas_call(
        _router_dispatch_kernel,
        grid=(t // tm,),
        in_specs=[pl.BlockSpec((tm, w), lambda i: (i, 0)), _full((N_EXPERTS, 1))],
        out_specs=[pl.BlockSpec((tm, w), lambda i: (i, 0)),
                   pl.BlockSpec((TOP_K, tm), lambda i: (0, i)),
                   pl.BlockSpec((TOP_K, tm), lambda i: (0, i)),
                   _full((N_EXPERTS, LANES))],
        out_shape=[jax.ShapeDtypeStruct((t, w), F32), jax.ShapeDtypeStruct((TOP_K, t), jnp.int32),
                   jax.ShapeDtypeStruct((TOP_K, t), jnp.int32), jax.ShapeDtypeStruct((N_EXPERTS, LANES), jnp.int32)],
        scratch_shapes=[pltpu.VMEM((N_EXPERTS, LANES), F32)],
        compiler_params=_cp("arbitrary"),
        name="router_dispatch",
    )(logits, router_b.reshape(N_EXPERTS, 1).astype(F32))


def router_gates(logits, router_b):
    t, w = logits.shape
    tm = min(TOKEN_TILE, t)
    return pl.pallas_call(
        _router_kernel,
        grid=(t // tm,),
        in_specs=[pl.BlockSpec((tm, w), lambda i: (i, 0)), _full((N_EXPERTS, 1))],
        out_specs=pl.BlockSpec((tm, w), lambda i: (i, 0)),
        out_shape=jax.ShapeDtypeStruct((t, w), F32),
        compiler_params=_cp("parallel"),
        name="router_gates",
    )(logits, router_b.reshape(N_EXPERTS, 1).astype(F32))


def _moe_kernel(h_ref, gate_ref, x_ref, wg_ref, wu_ref, wd_ref, sg_ref, su_ref, sd_ref, gpost_ref, ga2_ref,
                o_ref, acc_s):
    e = pl.program_id(1)
    h = _mx(_unpack_pairs(h_ref[...]))

    @pl.when(e == 0)
    def _():
        hs = _silu(jnp.dot(h, sg_ref[...], preferred_element_type=F32)) * jnp.dot(h, su_ref[...], preferred_element_type=F32)
        acc_s[...] = jnp.dot(_mx(hs), sd_ref[...], preferred_element_type=F32)

    gates = gate_ref[...]
    lane = lax.broadcasted_iota(jnp.int32, gates.shape, 1)
    hid = []
    for j in range(MOE_EB):
        gcol = jnp.sum(jnp.where(lane == e * MOE_EB + j, gates, 0.0), axis=1, keepdims=True)
        g = jnp.dot(h, _mx(wg_ref[j]), preferred_element_type=F32)
        u = jnp.dot(h, _mx(wu_ref[j]), preferred_element_type=F32)
        hid.append(_mx(_silu(g) * u * gcol))
    wd = _mx(wd_ref[...]).reshape(MOE_EB * D_EXPERT, -1)
    acc_s[...] += jnp.dot(jnp.concatenate(hid, axis=1), wd, preferred_element_type=F32)

    @pl.when(e == pl.num_programs(1) - 1)
    def _():
        o_ref[...] = x_ref[...] + ga2_ref[0] * _rms(acc_s[...], gpost_ref[...])


def moe_ffn(h, gates, x, layer, wg, wu, wd, sg, su, sd, gpost, ga2, tiles_per_group):
    t, d = x.shape
    tm = min(MOE_TILE, t)
    _, ne, _, f = wg.shape
    row = lambda w: pl.BlockSpec((tm, w), lambda i, e: (i, 0))
    return pl.pallas_call(
        _moe_kernel,
        grid=(t // tm, ne // MOE_EB),
        in_specs=[row(h.shape[1]), row(gates.shape[1]), row(d),
                  pl.BlockSpec((None, MOE_EB, d, f), lambda i, e: (layer, e, 0, 0)),
                  pl.BlockSpec((None, MOE_EB, d, f), lambda i, e: (layer, e, 0, 0)),
                  pl.BlockSpec((None, MOE_EB, f, d), lambda i, e: (layer, e, 0, 0)),
                  _full(sg.shape), _full(su.shape), _full(sd.shape), _full((1, d)),
                  pl.BlockSpec((1, 1, d), lambda i, e: (i // tiles_per_group, 0, 0))],
        out_specs=row(d),
        out_shape=jax.ShapeDtypeStruct((t, d), F32),
        scratch_shapes=[pltpu.VMEM((tm, d), F32)],
        compiler_params=_cp("parallel", "arbitrary"),
        name="moe_ffn",
    )(h, gates, x, wg, wu, wd, sg, su, sd, gpost, ga2)


def moe_plan(counts, n_tokens):
    n_blocks = (n_tokens * TOP_K + N_EXPERTS * (MOE_BLOCK - 1) + MOE_BLOCK - 1) // MOE_BLOCK
    cnt = counts[:, 0]
    padded = (cnt + MOE_BLOCK - 1) // MOE_BLOCK * MOE_BLOCK
    pad_end = jnp.cumsum(padded)
    off = pad_end - padded
    start = jnp.arange(n_blocks, dtype=jnp.int32) * MOE_BLOCK
    be = jnp.minimum(jnp.sum(pad_end[None, :] <= start[:, None], axis=1), N_EXPERTS - 1).astype(jnp.int32)
    mine = be[:, None] == jnp.arange(N_EXPERTS, dtype=jnp.int32)[None, :]
    end = jnp.sum(jnp.where(mine, (off + cnt)[None, :], 0), axis=1)
    nv = jnp.clip(end - start, 0, MOE_BLOCK).astype(jnp.int32)
    return off.astype(jnp.int32), be, nv


def _rows_kernel(off_ref, ek_ref, pk_ref, dest_ref):
    ek = ek_ref[...]
    dest = pk_ref[...]
    for e in range(N_EXPERTS):
        dest = dest + jnp.where(ek == e, off_ref[e], 0)
    dest_ref[...] = dest


def moe_rows(off, ek, pk):
    k, t = ek.shape
    tm = min(MOE_PLAN_TILE, t)
    spec = pl.BlockSpec((k, tm), lambda i, off: (0, i))
    return pl.pallas_call(
        _rows_kernel,
        grid_spec=pltpu.PrefetchScalarGridSpec(num_scalar_prefetch=1, grid=(t // tm,),
                                               in_specs=[spec, spec], out_specs=spec),
        out_shape=jax.ShapeDtypeStruct((k, t), jnp.int32),
        compiler_params=_cp("arbitrary"),
        name="moe_rows",
    )(off, ek, pk)


U32 = jnp.uint32
HIGH_HALF = 0xFFFF0000


def _pack_pairs(x):
    w = x.shape[1] // 2
    bits = lax.bitcast_convert_type(x.astype(jnp.bfloat16).astype(F32), U32)
    return (bits[:, w:] & jnp.uint32(HIGH_HALF)) | (bits[:, :w] >> 16)


def _unpack_pairs(p):
    lo = lax.bitcast_convert_type(p << 16, F32)
    hi = lax.bitcast_convert_type(p & jnp.uint32(HIGH_HALF), F32)
    return jnp.concatenate([lo, hi], axis=1)


def _sc_workers():
    info = plsc.get_sparse_core_info()
    return info.num_cores, info.num_cores * info.num_subcores


def sc_scatter_rows(src, idx, n_rows):
    k, t = idx.shape
    w = src.shape[1]
    n_cores, n_workers = _sc_workers()
    per_worker = t // n_workers
    mesh = plsc.VectorSubcoreMesh(core_axis_name="c", subcore_axis_name="s")

    @functools.partial(
        pl.kernel, mesh=mesh, out_type=jax.ShapeDtypeStruct((n_rows, w), src.dtype),
        scratch_types=[pltpu.VMEM((k, SC_WINDOW), jnp.int32), pltpu.VMEM((SC_WINDOW, w), src.dtype),
                       pltpu.SemaphoreType.DMA])
    def scatter(s_hbm, i_hbm, o_hbm, idx_v, rows_v, sem):
        base = (lax.axis_index("s") * n_cores + lax.axis_index("c")) * per_worker

        @pl.loop(0, per_worker // SC_WINDOW)
        def _(j):
            off = base + j * SC_WINDOW
            pltpu.sync_copy(i_hbm.at[:, pl.ds(off, SC_WINDOW)], idx_v)
            pltpu.sync_copy(s_hbm.at[pl.ds(off, SC_WINDOW)], rows_v)
            for kk in range(k):
                pltpu.async_copy(rows_v, o_hbm.at[idx_v.at[kk]], sem).wait()

    return scatter(src, idx)


def sc_gather_rows(table, idx):
    n = idx.shape[0]
    w = table.shape[1]
    n_cores, n_workers = _sc_workers()
    per_worker = n // n_workers
    mesh = plsc.VectorSubcoreMesh(core_axis_name="c", subcore_axis_name="s")

    @functools.partial(
        pl.kernel, mesh=mesh, out_type=jax.ShapeDtypeStruct((n, w), table.dtype),
        scratch_types=[pltpu.VMEM((SC_WINDOW,), jnp.int32), pltpu.VMEM((SC_WINDOW, w), table.dtype),
                       pltpu.SemaphoreType.DMA])
    def gather(t_hbm, i_hbm, o_hbm, idx_v, rows_v, sem):
        base = (lax.axis_index("s") * n_cores + lax.axis_index("c")) * per_worker

        @pl.loop(0, per_worker // SC_WINDOW)
        def _(j):
            off = base + j * SC_WINDOW
            pltpu.sync_copy(i_hbm.at[pl.ds(off, SC_WINDOW)], idx_v)
            pltpu.async_copy(t_hbm.at[idx_v], rows_v, sem).wait()
            pltpu.sync_copy(rows_v, o_hbm.at[pl.ds(off, SC_WINDOW)])

    return gather(table, idx)


def _expert_kernel(be_ref, nv_ref, xs_ref, wg_ref, wu_ref, wd_ref, ys_ref):
    nv = nv_ref[pl.program_id(0)]

    def ffn(x):
        x = _mx(x)
        hid = (_silu(jnp.dot(x, _mx(wg_ref[0]), preferred_element_type=F32))
               * jnp.dot(x, _mx(wu_ref[0]), preferred_element_type=F32))
        ys_ref[...] = _pack_pairs(jnp.dot(_mx(hid), _mx(wd_ref[0]), preferred_element_type=F32))

    @pl.when(nv == MOE_BLOCK)
    def _():
        ffn(_unpack_pairs(xs_ref[...]))

    @pl.when((nv > 0) & (nv < MOE_BLOCK))
    def _():
        x = _unpack_pairs(xs_ref[...])
        rows = lax.broadcasted_iota(jnp.int32, x.shape, 0)
        ffn(jnp.where(rows < nv, x, 0.0))

    @pl.when(nv == 0)
    def _():
        ys_ref[...] = jnp.zeros(ys_ref.shape, U32)


def moe_experts(xs, be, nv, layer, wg, wu, wd):
    n_rows, w = xs.shape
    _, _, d, f = wg.shape
    return pl.pallas_call(
        _expert_kernel,
        grid_spec=pltpu.PrefetchScalarGridSpec(
            num_scalar_prefetch=2,
            grid=(n_rows // MOE_BLOCK,),
            in_specs=[pl.BlockSpec((MOE_BLOCK, w), lambda b, be, nv: (b, 0)),
                      pl.BlockSpec((None, 1, d, f), lambda b, be, nv: (layer, be[b], 0, 0)),
                      pl.BlockSpec((None, 1, d, f), lambda b, be, nv: (layer, be[b], 0, 0)),
                      pl.BlockSpec((None, 1, f, d), lambda b, be, nv: (layer, be[b], 0, 0))],
            out_specs=pl.BlockSpec((MOE_BLOCK, w), lambda b, be, nv: (b, 0))),
        out_shape=jax.ShapeDtypeStruct((n_rows, w), U32),
        compiler_params=_cp("arbitrary"),
        name="moe_experts",
    )(be, nv, xs, wg, wu, wd)


def _combine_kernel(yg_ref, gk_ref, hp_ref, x_ref, sg_ref, su_ref, sd_ref, gpost_ref, ga2_ref, o_ref):
    h = _mx(_unpack_pairs(hp_ref[...]))
    hs = _silu(jnp.dot(h, sg_ref[...], preferred_element_type=F32)) * jnp.dot(h, su_ref[...], preferred_element_type=F32)
    f = jnp.dot(_mx(hs), sd_ref[...], preferred_element_type=F32)
    gk = gk_ref[...]
    for k in range(TOP_K):
        f = f + gk[:, k:k + 1] * _unpack_pairs(yg_ref[k])
    o_ref[...] = x_ref[...] + ga2_ref[0] * _rms(f, gpost_ref[...])


def moe_combine(yg, gk, hp, x, sg, su, sd, gpost, ga2, tiles_per_group):
    t, d = x.shape
    tm = min(MOE_ROW_TILE, t)
    w = hp.shape[1]
    row = lambda n: pl.BlockSpec((tm, n), lambda i: (i, 0))
    return pl.pallas_call(
        _combine_kernel,
        grid=(t // tm,),
        in_specs=[pl.BlockSpec((TOP_K, tm, w), lambda i: (0, i, 0)),
                  row(gk.shape[1]), row(w), row(d), _full(sg.shape), _full(su.shape), _full(sd.shape), _full((1, d)),
                  pl.BlockSpec((1, 1, d), lambda i: (i // tiles_per_group, 0, 0))],
        out_specs=row(d),
        out_shape=jax.ShapeDtypeStruct((t, d), F32),
        compiler_params=_cp("parallel"),
        name="moe_combine",
    )(yg, gk, hp, x, sg, su, sd, gpost, ga2)


def _reorder_w_in(w_in):
    c = np.cumsum((0,) + (GROUP_W, GROUP_W, GROUP_W, GROUP_W, GROUP_W, GROUP_W, 2 * SSD_STATE, 2 * SSD_STATE,
                          GROUP_W, 2 * N_HEADS, GROUP_W, GROUP_W, GROUP_W, GROUP_W, 2 * N_HEADS, 2 * N_HEADS))
    seg = lambda a, b: w_in[:, c[a]:c[b]]
    small = jnp.concatenate([seg(9, 10), seg(14, 15), seg(15, 16),
                             jnp.zeros((w_in.shape[0], LANES - 6 * N_HEADS), w_in.dtype)], axis=1)
    return jnp.concatenate([seg(0, 1), seg(5, 8), seg(10, 13), seg(1, 5), seg(8, 9), seg(13, 14), small], axis=1)


def kernel(x, c, ctx, c_ctx, w_mod, b_mod, g_pre_mix, g_post_mix, g_pre_ffn, g_post_ffn, w_in, w_out, lru_conv_w, lru_conv_b, lru_wa, lru_ba, lru_wx, lru_bx, lru_lambda, na_bias, ssd_conv_w, ssd_conv_b, ssd_a_log, ssd_dt_bias, ssd_d, ssd_norm, gdn_conv_w, gdn_a_log, gdn_dt_bias, gdn_norm, router_w, router_b, we_gate, we_up, we_down, ws_gate, ws_up, ws_down):
    bsz, seq, d = x.shape
    n_ctx = ctx.shape[1]
    depth = w_mod.shape[0]
    lat_tpg = seq // min(TOKEN_TILE, seq)
    ctx_tpg = max(bsz * n_ctx // TOKEN_TILE, 1)
    ctx_mpg = max(bsz * n_ctx // MOE_TILE, 1)

    cond = _pad_rows(jnp.concatenate([c, c_ctx[None, :]], axis=0))
    mod = modulation(cond, w_mod, b_mod).reshape(depth, SUBLANES, N_MOD, d)
    rope = rope_tables(seq)
    row = lambda v: v[None, :].astype(F32)

    def layer_params(l):
        m_lat = [mod[l, :bsz, k][:, None, :] for k in range(N_MOD)]
        m_ctx = [mod[l, bsz:bsz + 1, k][:, None, :] for k in range(N_MOD)]
        w_in_l = _reorder_w_in(w_in[l]).astype(MXU_DTYPE)
        conv = (_pad_rows(lru_conv_w[l]), row(lru_conv_b[l]), _pad_rows(ssd_conv_w[l]), row(ssd_conv_b[l]),
                _pad_rows(gdn_conv_w[l]))
        return dict(
            m_lat=m_lat, m_ctx=m_ctx, w_in=w_in_l, conv=conv,
            lru=lru_params(lru_wa[l], lru_ba[l], lru_wx[l], lru_bx[l], lru_lambda[l]),
            ssd=ssd_params(ssd_a_log[l], ssd_dt_bias[l]), gdn=gdn_params(gdn_a_log[l], gdn_dt_bias[l]),
            epi=(w_out[l].astype(MXU_DTYPE), row(g_post_mix[l])),
            epi_tail=(row(jnp.repeat(ssd_d[l], HEAD_DIM)), row(ssd_norm[l]), row(jnp.tile(gdn_norm[l], N_HEADS)),
                      router_w[l]),
            routed=(l, we_gate, we_up, we_down),
            shared=(ws_gate[l].astype(MXU_DTYPE), ws_up[l].astype(MXU_DTYPE), ws_down[l].astype(MXU_DTYPE),
                    row(g_post_ffn[l])))

    def context_mixers(l, p, xc):
        pc = in_projection(xc, row(g_pre_mix[l]), p['m_ctx'][1], p['m_ctx'][0], p['w_in'], p['conv'], None, n_ctx,
                           bsz * n_ctx // min(TOKEN_TILE, n_ctx))
        a_f, a_b, a_st = lru_mixer(pc[P_AX], jnp.zeros((bsz, SUBLANES, GROUP_W), F32), bsz, *p['lru'])
        kc = pc[P_BK].reshape(bsz, n_ctx, GROUP_W)
        vc = pc[P_BV].reshape(bsz, n_ctx, GROUP_W)
        b_o = ctx_attention(pc[P_BQ].reshape(bsz, n_ctx, GROUP_W), kc, vc).reshape(bsz * n_ctx, GROUP_W)
        c_f, c_b, c_st = ssd_mixer(pc[P_CX], pc[P_SM], jnp.zeros((bsz, 2, N_HEADS, SSD_STATE, HEAD_DIM), F32),
                                   bsz, *p['ssd'])
        d_f, d_b, d_st = gdn_mixer(pc[P_DX], pc[P_SM], jnp.zeros((bsz, 2, N_HEADS, HEAD_DIM, HEAD_DIM), F32),
                                   bsz, *p['gdn'])
        mix = (a_f, a_b, pc[P_AG], b_o, c_f, c_b, pc[P_CX], pc[P_CZ], d_f, d_b, pc[P_DZ])
        return dict(mix=mix, a_st=a_st, kc=kc, vc=vc, c_st=c_st, d_st=d_st)

    def context_ffn(l, p, cm, xc):
        m = p['m_ctx']
        xc, hp, lg = out_projection(xc, cm['mix'], *p['epi'], m[2], row(g_pre_ffn[l]), m[4], m[3], *p['epi_tail'], ctx_tpg)
        return moe_ffn(hp, router_gates(lg, router_b[l]), xc, *p['routed'], *p['shared'], m[5], ctx_mpg)

    xl = x.reshape(bsz * seq, d)
    xc = ctx.reshape(bsz * n_ctx, d)
    p = layer_params(0)
    cm = context_mixers(0, p, xc)
    for l in range(depth):
        last = l == depth - 1
        m = p['m_lat']
        pl_ = in_projection(xl, row(g_pre_mix[l]), m[1], m[0], p['w_in'], p['conv'], rope, seq, lat_tpg)
        a_f, a_b, _ = lru_mixer(pl_[P_AX], cm['a_st'], bsz, *p['lru'])
        b_o = na_mixer(pl_[P_BQ], pl_[P_BK], pl_[P_BV], cm['kc'], cm['vc'], na_bias_slabs(na_bias[l]), bsz)
        c_f, c_b, _ = ssd_mixer(pl_[P_CX], pl_[P_SM], cm['c_st'], bsz, *p['ssd'])
        d_f, d_b, _ = gdn_mixer(pl_[P_DX], pl_[P_SM], cm['d_st'], bsz, *p['gdn'])
        mix_l = (a_f, a_b, pl_[P_AG], b_o, c_f, c_b, pl_[P_CX], pl_[P_CZ], d_f, d_b, pl_[P_DZ])
        xl, hp, lg = out_projection(xl, mix_l, *p['epi'], m[2], row(g_pre_ffn[l]), m[4], m[3], *p['epi_tail'], lat_tpg)
        gk, ek, pk, cnt = router_dispatch(lg, router_b[l])
        off, be, nv = moe_plan(cnt, bsz * seq)
        dest = moe_rows(off, ek, pk)
        xs = sc_scatter_rows(hp, dest, be.shape[0] * MOE_BLOCK)
        if not last:
            xc = context_ffn(l, p, cm, xc)
        ys = moe_experts(xs, be, nv, *p['routed'])
        yg = sc_gather_rows(ys, dest.reshape(-1)).reshape(TOP_K, bsz * seq, d // 2)
        if not last:
            p_next = layer_params(l + 1)
            cm = context_mixers(l + 1, p_next, xc)
        xl = moe_combine(yg, gk, hp, xl, *p['shared'], m[5], seq // min(MOE_ROW_TILE, seq))
        if not last:
            p = p_next
    return xl.reshape(bsz, seq, d)
```

```python
import functools
import math

import jax
import jax.numpy as jnp
import numpy as np
from jax import lax
from jax.experimental import pallas as pl
from jax.experimental.pallas import tpu as pltpu
from jax.experimental.pallas import tpu_sc as plsc

F32 = jnp.float32
MXU_DTYPE = jnp.bfloat16
HI = lax.Precision.HIGHEST

D_MODEL = 1024
GRID_W = 64
GROUP_W = 256
HEAD_DIM = 64
N_HEADS = 4
EPS = 1e-6
NEG_INF = -1e30
N_MOD = 6
LRU_C = 8.0
NA_WIN_ROWS = 8
NA_WIN_COLS = 16
SSD_STATE = 128
SSD_GROUPS = 2
ROPE_BASE = 10000.0
ROPE_AXIS_DIM = HEAD_DIM // 2
N_EXPERTS = 64
N_EXPERT_GROUPS = 8
TOPK_GROUPS = 4
TOP_K = 8
D_EXPERT = 256
ROUTED_SCALE = 2.5

LANES = 128
SUBLANES = 8
VMEM_LIMIT = 56 * 1024 * 1024

TOKEN_TILE = 512
LRU_CHUNK = 256
SSD_CHUNK = 128
GDN_CHUNK = 64
GDN_TILE = 256
GDN_SUB = 128
GDN_BASE = 16
MOE_TILE = 1024
MOE_EB = 4
MOE_BLOCK = 1024
MOE_ROW_TILE = 512
LRU_UNROLL = 8
MOE_PLAN_TILE = 2048
SC_WINDOW = 128

P_WIDTHS = (256, 768, 768, 256, 256, 256, 256, 256, 256, 128)
(P_AX, P_CX, P_DX, P_AG, P_BQ, P_BK, P_BV, P_CZ, P_DZ, P_SM) = range(10)
P_CONV_GROUPS = 3
SM_DT, SM_BETA, SM_DECAY = 0, 8, 16


def _cp(*sem):
    return pltpu.CompilerParams(dimension_semantics=sem, vmem_limit_bytes=VMEM_LIMIT)


def _mx(x):
    return x.astype(MXU_DTYPE)


def _dot(a, b):
    return jnp.dot(_mx(a), _mx(b), preferred_element_type=F32)


def _dot_nt(a, b):
    return lax.dot_general(_mx(a), _mx(b), (((1,), (1,)), ((), ())), preferred_element_type=F32)


def _dot_tn(a, b):
    return lax.dot_general(_mx(a), _mx(b), (((0,), (0,)), ((), ())), preferred_element_type=F32)


def _dot_hi(a, b):
    return jnp.dot(a, b, preferred_element_type=F32, precision=HI)


def _sigmoid(x):
    return 1.0 / (1.0 + jnp.exp(-x))


def _silu(x):
    return x * _sigmoid(x)


def _softplus(x):
    return jnp.maximum(x, 0.0) + jnp.log1p(jnp.exp(-jnp.abs(x)))


def _gelu_tanh(x):
    return 0.5 * x * (1.0 + jnp.tanh(math.sqrt(2.0 / math.pi) * (x + 0.044715 * (x * x * x))))


def _rms(x, g):
    return x * lax.rsqrt(jnp.mean(x * x, axis=-1, keepdims=True) + EPS) * g


def _full(shape):
    n = len(shape)
    return pl.BlockSpec(shape, lambda *_: (0,) * n)


MOD_COLS = 1536


def _mod_kernel(c_ref, w_ref, b_ref, o_ref):
    o_ref[0] = _dot_hi(_silu(c_ref[...]), w_ref[0]) + b_ref[0]


def modulation(cond, w_mod, b_mod):
    depth, d, n = w_mod.shape
    return pl.pallas_call(
        _mod_kernel,
        grid=(depth, n // MOD_COLS),
        in_specs=[pl.BlockSpec((SUBLANES, d), lambda l, j: (0, 0)),
                  pl.BlockSpec((1, d, MOD_COLS), lambda l, j: (l, 0, j)),
                  pl.BlockSpec((1, 1, MOD_COLS), lambda l, j: (l, 0, j))],
        out_specs=pl.BlockSpec((1, SUBLANES, MOD_COLS), lambda l, j: (l, 0, j)),
        out_shape=jax.ShapeDtypeStruct((depth, SUBLANES, n), F32),
        compiler_params=_cp("parallel", "parallel"),
        name="modulation",
    )(cond, w_mod, b_mod.reshape(depth, 1, n))


def _inproj_kernel(*refs, tiles_per_seq, rope):
    (x_ref, xp_ref, xn_ref, g_ref, sc_ref, sh_ref, w_ref, lcw_ref, lcb_ref, scw_ref, scb_ref, gcw_ref) = refs[:12]
    cos_ref, sin_ref = (refs[12], refs[13]) if rope else (None, None)
    o_refs = refs[14:] if rope else refs[12:]
    i = pl.program_id(0)
    norm = lambda v: _rms(v, g_ref[...]) * (1.0 + sc_ref[0]) + sh_ref[0]
    p = _dot(norm(x_ref[...]), w_ref[...])
    n_conv = sum(P_WIDTHS[:P_CONV_GROUPS])
    ph = _dot(norm(jnp.concatenate([xp_ref[...], xn_ref[...]], axis=0)), w_ref[:, :n_conv])
    pos = i % tiles_per_seq
    prev = jnp.where(pos == 0, 0.0, ph[:SUBLANES])
    nxt = jnp.where(pos == tiles_per_seq - 1, 0.0, ph[SUBLANES:])
    row = lax.broadcasted_iota(jnp.int32, (SUBLANES, n_conv), 0)
    halo = jnp.where(row < 2, pltpu.roll(prev, 2, 0), jnp.where(row == 2, pltpu.roll(nxt, 2, 0), 0.0))
    c0, c1, c2 = GROUP_W, GROUP_W + 3 * GROUP_W, n_conv
    lru_u = _dwconv(p[:, :c0], halo[:, :c0], lcw_ref[...], lcb_ref[...])
    ssd_x = _silu(_dwconv(p[:, c0:c1], halo[:, c0:c1], scw_ref[...], scb_ref[...]))
    qkv = _silu(_dwconv(p[:, c1:c2], halo[:, c1:c2], gcw_ref[...]))
    qn = _l2norm_heads(qkv[:, :GROUP_W])
    kn = _l2norm_heads(qkv[:, GROUP_W:2 * GROUP_W])
    if rope:
        cos, sin = cos_ref[...], sin_ref[...]
        qn = qn * cos + _swap16(qn) * sin
        kn = kn * cos + _swap16(kn) * sin
    outs = [lru_u, ssd_x, jnp.concatenate([qn * (HEAD_DIM ** -0.5), kn, qkv[:, 2 * GROUP_W:]], axis=1)]
    off = n_conv
    for o_ref, w in zip(o_refs, P_WIDTHS):
        if outs:
            o_ref[...] = outs.pop(0)
        else:
            o_ref[...] = p[:, off:off + w].astype(o_ref.dtype)
            off += w


def in_projection(x, g, sc, sh, w, conv, rope, seq_len, tiles_per_group):
    t, d = x.shape
    tm = min(TOKEN_TILE, seq_len)
    tps = seq_len // tm
    hb = tm // SUBLANES
    vec = lambda i: (i // tiles_per_group, 0, 0)
    ins = [x, x, x, g, sc, sh, w, *conv]
    specs = [pl.BlockSpec((tm, d), lambda i: (i, 0)),
             pl.BlockSpec((SUBLANES, d), lambda i: (jnp.maximum(i * hb - 1, 0), 0)),
             pl.BlockSpec((SUBLANES, d), lambda i: (jnp.minimum((i + 1) * hb, t // SUBLANES - 1), 0)),
             _full((1, d)), pl.BlockSpec((1, 1, d), vec), pl.BlockSpec((1, 1, d), vec), _full(w.shape)]
    specs += [_full(a.shape) for a in conv]
    if rope is not None:
        ins += list(rope)
        specs += [pl.BlockSpec((tm, GROUP_W), lambda i: (i % tps, 0))] * 2
    return pl.pallas_call(
        functools.partial(_inproj_kernel, tiles_per_seq=tps, rope=rope is not None),
        grid=(t // tm,),
        in_specs=specs,
        out_specs=[pl.BlockSpec((tm, wd), lambda i: (i, 0)) for wd in P_WIDTHS],
        out_shape=[jax.ShapeDtypeStruct((t, wd), MXU_DTYPE if k in (P_BQ, P_BK, P_BV) else F32)
                   for k, wd in enumerate(P_WIDTHS)],
        compiler_params=_cp("parallel"),
        name="in_projection",
    )(*ins)


def _dwconv(x, halo, w, b=None):
    q = x.shape[0]
    row = lax.broadcasted_iota(jnp.int32, (SUBLANES, x.shape[1]), 0)

    def shifted(s, keep_rolled, edge):
        r = pltpu.roll(x, s % q, 0)
        if s > 0:
            return jnp.concatenate([jnp.where(keep_rolled, r[:SUBLANES], edge), r[SUBLANES:]], axis=0)
        return jnp.concatenate([r[:q - SUBLANES], jnp.where(keep_rolled, r[q - SUBLANES:], edge)], axis=0)

    xm2 = shifted(2, row >= 2, halo)
    xm1 = shifted(1, row >= 1, pltpu.roll(halo, SUBLANES - 1, 0))
    xp1 = shifted(-1, row < SUBLANES - 1, pltpu.roll(halo, SUBLANES - 3, 0))
    y = w[0:1] * xm2 + w[1:2] * xm1 + w[2:3] * x + w[3:4] * xp1
    return y if b is None else y + b


def _pad_rows(a, rows=SUBLANES):
    return jnp.concatenate([a, jnp.zeros((rows - a.shape[0],) + a.shape[1:], a.dtype)], axis=0)


def _chunk_specs(nc, q, c):
    fwd = pl.BlockSpec((q, c), lambda b, i: (b * nc + i, 0))
    bwd = pl.BlockSpec((q, c), lambda b, i: (b * nc + nc - 1 - i, 0))
    return fwd, bwd


def _lru_kernel(xf_ref, xb_ref, h0_ref, wg_ref, bg_ref, lam_ref,
                yf_ref, yb_ref, hfin_ref, af_s, bf_s, ab_s, bb_s, carry_s):
    i = pl.program_id(1)
    q = xf_ref.shape[0]

    @pl.when(i == 0)
    def _():
        carry_s[...] = h0_ref[0]

    def coeffs(x_ref, d, a_s, b_s):
        u = x_ref[...]
        g = _dot(u, wg_ref[:, 2 * GROUP_W * d:2 * GROUP_W * (d + 1)]) + bg_ref[:, 2 * GROUP_W * d:2 * GROUP_W * (d + 1)]
        r = _sigmoid(g[:, :GROUP_W])
        gate_in = _sigmoid(g[:, GROUP_W:])
        log_a = -LRU_C * r * _softplus(-lam_ref[d:d + 1, :])
        a_s[...] = jnp.exp(log_a)
        b_s[...] = jnp.sqrt(1.0 - jnp.exp(2.0 * log_a)) * (gate_in * u)

    coeffs(xf_ref, 0, af_s, bf_s)
    coeffs(xb_ref, 1, ab_s, bb_s)

    ng = q // SUBLANES
    row = lax.broadcasted_iota(jnp.int32, (SUBLANES, GROUP_W), 0)

    def body(g, hs):
        h_f, h_b = hs
        i0 = pl.multiple_of(g * SUBLANES, SUBLANES)
        a = af_s[pl.ds(i0, SUBLANES), :]
        b = bf_s[pl.ds(i0, SUBLANES), :]
        for s in (1, 2, 4):
            m = row >= s
            b = jnp.where(m, a * pltpu.roll(b, s, 0) + b, b)
            a = jnp.where(m, a * pltpu.roll(a, s, 0), a)
        h = b + a * h_f
        yf_ref[pl.ds(i0, SUBLANES), :] = h
        h_f = h[SUBLANES - 1:SUBLANES, :]
        j0 = pl.multiple_of((ng - 1 - g) * SUBLANES, SUBLANES)
        a = ab_s[pl.ds(j0, SUBLANES), :]
        b = bb_s[pl.ds(j0, SUBLANES), :]
        for s in (1, 2, 4):
            m = row < SUBLANES - s
            b = jnp.where(m, a * pltpu.roll(b, SUBLANES - s, 0) + b, b)
            a = jnp.where(m, a * pltpu.roll(a, SUBLANES - s, 0), a)
        h = b + a * h_b
        yb_ref[pl.ds(j0, SUBLANES), :] = h
        return h_f, h[0:1, :]

    h_f, h_b = lax.fori_loop(0, ng, body, (carry_s[0:1, :], carry_s[1:2, :]), unroll=LRU_UNROLL)
    carry_s[0:1, :] = h_f
    carry_s[1:2, :] = h_b

    @pl.when(i == pl.num_programs(1) - 1)
    def _():
        hfin_ref[0] = carry_s[...]


def _block_diag(w):
    h, a, b = w.shape
    return jnp.einsum('hij,hg->higj', w, jnp.eye(h, dtype=w.dtype)).reshape(h * a, h * b)


def lru_params(wa, ba, wx, bx, lam):
    wg = jnp.concatenate([_block_diag(wa[0]), _block_diag(wx[0]), _block_diag(wa[1]), _block_diag(wx[1])], axis=1)
    bg = jnp.concatenate([ba[0], bx[0], ba[1], bx[1]])[None, :]
    return wg.astype(MXU_DTYPE), bg, _pad_rows(lam)


def lru_mixer(x, h0, bsz, wg, bg, lam):
    t, c = x.shape
    s = t // bsz
    q = min(LRU_CHUNK, s)
    nc = s // q
    xf, xb = _chunk_specs(nc, q, c)
    st = pl.BlockSpec((1, SUBLANES, c), lambda b, i: (b, 0, 0))
    return pl.pallas_call(
        _lru_kernel,
        grid=(bsz, nc),
        in_specs=[xf, xb, st, _full(wg.shape), _full(bg.shape), _full(lam.shape)],
        out_specs=[xf, xb, st],
        out_shape=[jax.ShapeDtypeStruct((t, c), F32), jax.ShapeDtypeStruct((t, c), F32),
                   jax.ShapeDtypeStruct((bsz, SUBLANES, c), F32)],
        scratch_shapes=[pltpu.VMEM((q, c), F32)] * 4 + [pltpu.VMEM((SUBLANES, c), F32)],
        compiler_params=_cp("parallel", "arbitrary"),
        name="lru_mixer",
    )(x, x, h0, wg, bg, lam)


NA_KEYS = NA_WIN_ROWS * GRID_W
NA_ROW_BLOCK = 8


def na_bias_slabs(table):
    qc = np.arange(GRID_W)[:, None]
    kc = np.arange(GRID_W)[None, :]
    win0 = np.clip(qc - NA_WIN_COLS // 2, 0, GRID_W - NA_WIN_COLS)
    ok = (kc >= win0) & (kc < win0 + NA_WIN_COLS)
    dc = np.clip(kc - qc + NA_WIN_COLS - 1, 0, 2 * NA_WIN_COLS - 2)
    dr = np.arange(NA_WIN_ROWS)[:, None] + np.arange(NA_WIN_ROWS)[None, :]
    b = table.astype(F32)[:, dr][:, :, :, dc]
    b = jnp.where(ok[None, None, None], b, NEG_INF)
    h = table.shape[0]
    return b.transpose(0, 1, 3, 2, 4).reshape(h, NA_WIN_ROWS, GRID_W, NA_KEYS)


def _na_span_start(j, rows):
    return jnp.clip(j * NA_ROW_BLOCK - NA_WIN_ROWS // 2, 0, rows - (NA_ROW_BLOCK + NA_WIN_ROWS - 1))


def _na_kernel(q_ref, kw_ref, vw_ref, kc_ref, vc_ref, slab_ref, o_ref, *, rows):
    j = pl.program_id(1)
    ustart = _na_span_start(j, rows)
    q = q_ref[...] * (HEAD_DIM ** -0.5)
    kc, vc = kc_ref[0], vc_ref[0]
    heads = [slice(h * HEAD_DIM, (h + 1) * HEAD_DIM) for h in range(N_HEADS)]
    qrows = [slice(i * GRID_W, (i + 1) * GRID_W) for i in range(NA_ROW_BLOCK)]
    kws, vws, offs = [], [], []
    for i in range(NA_ROW_BLOCK):
        r = j * NA_ROW_BLOCK + i
        r0 = jnp.clip(r - NA_WIN_ROWS // 2, 0, rows - NA_WIN_ROWS)
        start = pl.multiple_of((r0 - ustart) * GRID_W, GRID_W)
        kws.append(kw_ref[pl.ds(start, NA_KEYS), :])
        vws.append(vw_ref[pl.ds(start, NA_KEYS), :])
        offs.append(r0 - r + NA_WIN_ROWS - 1)
    s_ctx = [_dot_nt(q[:, sl], kc[:, sl]) for sl in heads]
    s_loc = [[_dot_nt(q[qr, sl], kws[i][:, sl]) + slab_ref[h, offs[i]] for h, sl in enumerate(heads)]
             for i, qr in enumerate(qrows)]
    m = [[jnp.maximum(jnp.max(s_loc[i][h], axis=-1, keepdims=True), jnp.max(s_ctx[h][qr], axis=-1, keepdims=True))
          for h in range(N_HEADS)] for i, qr in enumerate(qrows)]
    p_loc = [[jnp.exp(s_loc[i][h] - m[i][h]) for h in range(N_HEADS)] for i in range(NA_ROW_BLOCK)]
    p_ctx = [jnp.exp(s_ctx[h] - jnp.concatenate([m[i][h] for i in range(NA_ROW_BLOCK)], axis=0))
             for h in range(N_HEADS)]
    o_ctx = [_dot(p_ctx[h], vc[:, sl]) for h, sl in enumerate(heads)]
    rows_out = []
    for i, qr in enumerate(qrows):
        outs = []
        for h, sl in enumerate(heads):
            den = jnp.sum(p_loc[i][h], axis=-1, keepdims=True) + jnp.sum(p_ctx[h][qr], axis=-1, keepdims=True)
            outs.append((_dot(p_loc[i][h], vws[i][:, sl]) + o_ctx[h][qr]) / den)
        rows_out.append(jnp.concatenate(outs, axis=1))
    o_ref[...] = jnp.concatenate(rows_out, axis=0)


def na_mixer(q, k, v, kc, vc, slabs, bsz):
    t, c = q.shape
    s = t // bsz
    rows = s // GRID_W
    n_ctx = kc.shape[1]
    span = (NA_ROW_BLOCK + NA_WIN_ROWS - 1) * GRID_W

    def win(b, j):
        return ((b * rows + _na_span_start(j, rows)) * GRID_W, 0)

    wspec = pl.BlockSpec((pl.Element(span), pl.Element(c)), win)
    cspec = pl.BlockSpec((1, n_ctx, c), lambda b, j: (b, 0, 0))
    qspec = pl.BlockSpec((NA_ROW_BLOCK * GRID_W, c), lambda b, j: (b * (rows // NA_ROW_BLOCK) + j, 0))
    return pl.pallas_call(
        functools.partial(_na_kernel, rows=rows),
        grid=(bsz, rows // NA_ROW_BLOCK),
        in_specs=[qspec, wspec, wspec, cspec, cspec, _full(slabs.shape)],
        out_specs=qspec,
        out_shape=jax.ShapeDtypeStruct((t, c), F32),
        compiler_params=_cp("parallel", "arbitrary"),
        name="na_mixer",
    )(q, k, v, kc, vc, slabs)


def _ctx_attn_kernel(q_ref, k_ref, v_ref, o_ref):
    q = q_ref[0] * (HEAD_DIM ** -0.5)
    k, v = k_ref[0], v_ref[0]
    outs = []
    for h in range(N_HEADS):
        sl = slice(h * HEAD_DIM, (h + 1) * HEAD_DIM)
        s = _dot_nt(q[:, sl], k[:, sl])
        p = jnp.exp(s - jnp.max(s, axis=-1, keepdims=True))
        outs.append(_dot(p, v[:, sl]) / jnp.sum(p, axis=-1, keepdims=True))
    o_ref[0] = jnp.concatenate(outs, axis=1)


def ctx_attention(q, k, v):
    spec = pl.BlockSpec((1,) + q.shape[1:], lambda b: (b, 0, 0))
    return pl.pallas_call(
        _ctx_attn_kernel,
        grid=(q.shape[0],),
        in_specs=[spec, spec, spec],
        out_specs=spec,
        out_shape=jax.ShapeDtypeStruct(q.shape, F32),
        compiler_params=_cp("parallel"),
        name="ctx_attention",
    )(q, k, v)


def _small_vec(vals, off):
    v = jnp.zeros((LANES,), F32).at[off:off + 2 * N_HEADS].set(vals.reshape(-1).astype(F32))
    return v[None, :]


def _lane_mask(off):
    lane = lax.broadcasted_iota(jnp.int32, (1, LANES), 1)
    return (lane >= off) & (lane < off + 2 * N_HEADS)


def _tri_masks(q):
    rowi = lax.broadcasted_iota(jnp.int32, (q, q), 0)
    coli = lax.broadcasted_iota(jnp.int32, (q, q), 1)
    return rowi, coli


def _ssd_kernel(xf_ref, xb_ref, sf_ref, sb_ref, h0_ref, dtb_ref, alog_ref,
                yf_ref, yb_ref, hfin_ref, state_s):
    i = pl.program_id(1)
    q = xf_ref.shape[0]

    @pl.when(i == 0)
    def _():
        state_s[...] = h0_ref[0]

    rowi, coli = _tri_masks(q)
    a_neg = jnp.where(_lane_mask(SM_DT), -jnp.exp(alog_ref[...]), 0.0)

    chains = [(d, h) for d in range(2) for h in range(N_HEADS)]
    per_head = N_HEADS // SSD_GROUPS
    scores, xdt, c_in, b_out, e_last = [], [], [], [], []
    for d, (x_ref, sm_ref) in enumerate(((xf_ref, sf_ref), (xb_ref, sb_ref))):
        xbc = x_ref[...]
        dt = _softplus(sm_ref[...] + dtb_ref[...])
        keep = (rowi >= coli) if d == 0 else (rowi <= coli)
        acum = _dot_tri(keep, dt * a_neg)
        acum_t = acum.T
        last = acum[q - 1:q, :] if d == 0 else acum[0:1, :]
        dec_end = jnp.exp(last - acum)
        e_acum = jnp.exp(acum)
        e_end = jnp.exp(last)
        bgs = [xbc[:, GROUP_W + SSD_STATE * g:GROUP_W + SSD_STATE * (g + 1)] for g in range(SSD_GROUPS)]
        cgs = [xbc[:, GROUP_W + SSD_STATE * (SSD_GROUPS + g):GROUP_W + SSD_STATE * (SSD_GROUPS + g + 1)]
               for g in range(SSD_GROUPS)]
        cbt = [_dot_nt(cg, bg) for cg, bg in zip(cgs, bgs)]
        for h in range(N_HEADS):
            g = h // per_head
            ln = SM_DT + N_HEADS * d + h
            lmat = jnp.exp(jnp.where(keep, acum[:, ln:ln + 1] - acum_t[ln:ln + 1, :], NEG_INF))
            scores.append(cbt[g] * lmat)
            xdt.append(xbc[:, h * HEAD_DIM:(h + 1) * HEAD_DIM] * dt[:, ln:ln + 1])
            c_in.append(cgs[g] * e_acum[:, ln:ln + 1])
            b_out.append(bgs[g] * dec_end[:, ln:ln + 1])
            e_last.append(e_end[:, ln:ln + 1])
    states = [state_s[d, h] for d, h in chains]
    y_diag = [_dot(s, x) for s, x in zip(scores, xdt)]
    y_off = [_dot(c, st) for c, st in zip(c_in, states)]
    upd = [_dot_tn(b, x) for b, x in zip(b_out, xdt)]
    for n, (d, h) in enumerate(chains):
        state_s[d, h] = states[n] * e_last[n] + upd[n]
    ys = [a + b for a, b in zip(y_diag, y_off)]
    yf_ref[...] = jnp.concatenate(ys[:N_HEADS], axis=1)
    yb_ref[...] = jnp.concatenate(ys[N_HEADS:], axis=1)

    @pl.when(i == pl.num_programs(1) - 1)
    def _():
        hfin_ref[0] = state_s[...]


def ssd_params(a_log, dt_bias):
    return _small_vec(dt_bias, SM_DT), _small_vec(a_log, SM_DT)


def ssd_mixer(xbc, sm, h0, bsz, dtb, alog):
    t, c = xbc.shape
    s = t // bsz
    q = min(SSD_CHUNK, s)
    nc = s // q
    xf, xb = _chunk_specs(nc, q, c)
    sf, sb = _chunk_specs(nc, q, LANES)
    yf, yb = _chunk_specs(nc, q, GROUP_W)
    st = pl.BlockSpec((1,) + h0.shape[1:], lambda b, i: (b, 0, 0, 0, 0))
    y_shape = jax.ShapeDtypeStruct((t, GROUP_W), F32)
    return pl.pallas_call(
        _ssd_kernel,
        grid=(bsz, nc),
        in_specs=[xf, xb, sf, sb, st, _full(dtb.shape), _full(alog.shape)],
        out_specs=[yf, yb, st],
        out_shape=[y_shape, y_shape, jax.ShapeDtypeStruct(h0.shape, F32)],
        scratch_shapes=[pltpu.VMEM(h0.shape[1:], F32)],
        compiler_params=_cp("parallel", "arbitrary"),
        name="ssd_mixer",
    )(xbc, xbc, sm, sm, h0, dtb, alog)


def rope_tables(seq):
    t = jnp.arange(seq)
    row = (t // GRID_W).astype(F32)
    col = (t % GRID_W).astype(F32)
    inv = ROPE_BASE ** (-jnp.arange(0, ROPE_AXIS_DIM, 2, dtype=F32) / ROPE_AXIS_DIM)
    ar, ac = row[:, None] * inv, col[:, None] * inv
    cos = jnp.concatenate([jnp.cos(ar), jnp.cos(ar), jnp.cos(ac), jnp.cos(ac)], axis=1)
    sin = jnp.concatenate([-jnp.sin(ar), jnp.sin(ar), -jnp.sin(ac), jnp.sin(ac)], axis=1)
    return jnp.tile(cos, (1, N_HEADS)), jnp.tile(sin, (1, N_HEADS))


def _swap16(x):
    lane = lax.broadcasted_iota(jnp.int32, x.shape, 1)
    half = ROPE_AXIS_DIM // 2
    return jnp.where((lane & (ROPE_AXIS_DIM - 1)) < half,
                     pltpu.roll(x, x.shape[1] - half, 1), pltpu.roll(x, half, 1))


def _head_sums(sq):
    c = sq.shape[1]
    li = lax.broadcasted_iota(jnp.int32, (c, c), 0)
    lj = lax.broadcasted_iota(jnp.int32, (c, c), 1)
    sh = HEAD_DIM.bit_length() - 1
    ones = _mx(((li >> sh) == (lj >> sh)).astype(F32))
    hi = _mx(sq)
    lo = _mx(sq - hi.astype(F32))
    return jnp.dot(hi, ones, preferred_element_type=F32) + jnp.dot(lo, ones, preferred_element_type=F32)


def _l2norm_heads(x):
    return x * lax.rsqrt(_head_sums(x * x) + EPS)


def _head_columns(x, off):
    li = lax.broadcasted_iota(jnp.int32, (LANES, N_HEADS * HEAD_DIM), 0)
    lj = lax.broadcasted_iota(jnp.int32, (LANES, N_HEADS * HEAD_DIM), 1)
    pick = _mx((li == off + (lj >> (HEAD_DIM.bit_length() - 1))).astype(F32))
    hi = _mx(x)
    lo = _mx(x - hi.astype(F32))
    return jnp.dot(hi, pick, preferred_element_type=F32) + jnp.dot(lo, pick, preferred_element_type=F32)


def _dot_tri(mask, x):
    m = _mx(mask.astype(F32))
    x1 = _mx(x)
    r1 = x - x1.astype(F32)
    x2 = _mx(r1)
    x3 = _mx(r1 - x2.astype(F32))
    return (jnp.dot(m, x1, preferred_element_type=F32) + jnp.dot(m, x2, preferred_element_type=F32)
            + jnp.dot(m, x3, preferred_element_type=F32))


def _same_block(rowi, coli, n):
    sh = n.bit_length() - 1
    return (rowi >> sh) == (coli >> sh)


def _solve_unit_tri(a_list, rhs_list, rowi, coli, chunk):
    mm = lambda x, y: jnp.dot(x, y, preferred_element_type=F32)
    eye = (rowi == coli).astype(F32)
    in_base = _same_block(rowi, coli, GDN_BASE)
    base = [_mx(jnp.where(in_base, a, 0.0)) for a in a_list]
    ts = [jnp.where(in_base, eye - a, 0.0) for a in a_list]
    ps = [_mx(mm(b, b)) for b in base]
    ts = [t + mm(_mx(t), p) for t, p in zip(ts, ps)]
    n = 4
    while n < GDN_BASE:
        ps = [_mx(mm(p, p)) for p in ps]
        ts = [t + mm(_mx(t), p) for t, p in zip(ts, ps)]
        n *= 2
    n = GDN_BASE
    while 2 * n < chunk:
        inner = _same_block(rowi, coli, 2 * n) & jnp.logical_not(_same_block(rowi, coli, n))
        offs = [_mx(jnp.where(inner, a, 0.0)) for a in a_list]
        tb = [_mx(t) for t in ts]
        ms = [_mx(mm(t, off)) for t, off in zip(tb, offs)]
        ts = [t - mm(m, t_b) for t, m, t_b in zip(ts, ms, tb)]
        n *= 2
    outer = jnp.logical_not(_same_block(rowi, coli, n))
    offs = [_mx(jnp.where(outer, a, 0.0)) for a in a_list]
    tb = [_mx(t) for t in ts]
    ys = [mm(t, _mx(r)) for t, r in zip(tb, rhs_list)]
    zs = [_mx(mm(off, _mx(y))) for off, y in zip(offs, ys)]
    return [y - mm(t, z) for y, t, z in zip(ys, tb, zs)]


def _gdn_kernel(xf_ref, xb_ref, sf_ref, sb_ref, s0_ref, alog_ref, dtb_ref, of_ref, ob_ref, sfin_ref, state_s):
    i = pl.program_id(1)
    tq = xf_ref.shape[0]
    ck = min(GDN_CHUNK, tq)
    nck = tq // ck

    @pl.when(i == 0)
    def _():
        state_s[...] = s0_ref[0]

    sub = min(GDN_SUB, tq)
    nsub = tq // sub
    rowt, colt = _tri_masks(tq)
    in_chunk_t = _same_block(rowt, colt, ck)
    rowi, coli = _tri_masks(sub)
    in_chunk = _same_block(rowi, coli, ck)
    a_neg = jnp.where(_lane_mask(SM_DECAY), -jnp.exp(alog_ref[...]), 0.0)

    a_list, rhs_list, qkm, qg, kd, e_last = [], [], [], [], [], []
    for d, (x_ref, sm_ref) in enumerate(((xf_ref, sf_ref), (xb_ref, sb_ref))):
        qkv = x_ref[...]
        qn, kn, v = qkv[:, :GROUP_W], qkv[:, GROUP_W:2 * GROUP_W], qkv[:, 2 * GROUP_W:]
        sm = sm_ref[...]
        beta = _sigmoid(sm)
        keep_t = in_chunk_t & ((rowt >= colt) if d == 0 else (rowt <= colt))
        keep = in_chunk & ((rowi >= coli) if d == 0 else (rowi <= coli))
        strict = in_chunk & ((rowi > coli) if d == 0 else (rowi < coli))
        gc = _dot_tri(keep_t, _softplus(sm + dtb_ref[...]) * a_neg)
        gc_t = gc.T
        edge = ck - 1 if d == 0 else 0
        last = jnp.concatenate([jnp.broadcast_to(gc[c * ck + edge:c * ck + edge + 1, :], (ck, LANES))
                                for c in range(nck)], axis=0)
        e_last.append(jnp.exp(last))
        beta_w = _head_columns(beta, SM_BETA + N_HEADS * d)
        e_gc_w = _head_columns(jnp.exp(gc), SM_DECAY + N_HEADS * d)
        e_end_w = _head_columns(jnp.exp(last - gc), SM_DECAY + N_HEADS * d)
        kb_w = kn * beta_w
        vb_w = v * beta_w
        kbe_w = kb_w * e_gc_w
        qg_w = qn * e_gc_w
        kd_w = kn * e_end_w
        qn_m, kn_m, kb_wm = _mx(qn), _mx(kn), _mx(kb_w)
        for h in range(N_HEADS):
            sl = slice(h * HEAD_DIM, (h + 1) * HEAD_DIM)
            lg = SM_DECAY + N_HEADS * d + h
            rhs = jnp.concatenate([vb_w[:, sl], kbe_w[:, sl]], axis=1)
            qg.append(qg_w[:, sl])
            kd.append(kd_w[:, sl])
            qh_m, kh_m, kb_m = qn_m[:, sl], kn_m[:, sl], kb_wm[:, sl]
            for s in range(nsub):
                rs = slice(s * sub, (s + 1) * sub)
                decay = jnp.exp(jnp.where(keep, gc[rs, lg:lg + 1] - gc_t[lg:lg + 1, rs], NEG_INF))
                a_list.append(jnp.where(strict, _dot_nt(kb_m[rs], kh_m[rs]) * decay, 0.0))
                rhs_list.append(rhs[rs])
                qkm.append(_dot_nt(qh_m[rs], kh_m[rs]) * decay)
    sols = _solve_unit_tri(a_list, rhs_list, rowi, coli, ck)
    sols = [jnp.concatenate(sols[n * nsub:(n + 1) * nsub], axis=0) for n in range(2 * N_HEADS)]

    chains = [(d, h) for d in range(2) for h in range(N_HEADS)]
    states = [state_s[d, h] for d, h in chains]
    v_new = [[None] * nck for _ in chains]
    o_st = [[None] * nck for _ in chains]
    for step in range(nck):
        rows = [slice((step if d == 0 else nck - 1 - step) * ck, (step if d == 0 else nck - 1 - step) * ck + ck)
                for d, _ in chains]
        ms = [_dot(jnp.concatenate([sols[n][r, HEAD_DIM:], qg[n][r]], axis=0), states[n])
              for n, r in enumerate(rows)]
        for n, (d, _) in enumerate(chains):
            c = step if d == 0 else nck - 1 - step
            v_new[n][c] = sols[n][rows[n], :HEAD_DIM] - ms[n][:ck]
            o_st[n][c] = ms[n][ck:]
        ups = [_dot_tn(kd[n][r], v_new[n][step if chains[n][0] == 0 else nck - 1 - step])
               for n, r in enumerate(rows)]
        for n, (d, h) in enumerate(chains):
            lg = SM_DECAY + N_HEADS * d + h
            states[n] = states[n] * e_last[d][rows[n].start:rows[n].start + 1, lg:lg + 1] + ups[n]
    cps = sub // ck
    outs = [jnp.concatenate(o_st[n], axis=0)
            + jnp.concatenate([_dot(qkm[n * nsub + s], jnp.concatenate(v_new[n][s * cps:(s + 1) * cps], axis=0))
                               for s in range(nsub)], axis=0)
            for n in range(len(chains))]
    of_ref[...] = jnp.concatenate(outs[:N_HEADS], axis=1)
    ob_ref[...] = jnp.concatenate(outs[N_HEADS:], axis=1)
    for n, (d, h) in enumerate(chains):
        state_s[d, h] = states[n]

    @pl.when(i == pl.num_programs(1) - 1)
    def _():
        sfin_ref[0] = state_s[...]


def gdn_params(a_log, dt_bias):
    return _small_vec(a_log, SM_DECAY), _small_vec(dt_bias, SM_DECAY)


def gdn_mixer(qkv, sm, s0, bsz, alog, dtb):
    t, c = qkv.shape
    s = t // bsz
    q = min(GDN_TILE, s)
    nc = s // q
    xf, xb = _chunk_specs(nc, q, c)
    sf, sb = _chunk_specs(nc, q, LANES)
    of, ob = _chunk_specs(nc, q, GROUP_W)
    st = pl.BlockSpec((1,) + s0.shape[1:], lambda b, i: (b, 0, 0, 0, 0))
    o_shape = jax.ShapeDtypeStruct((t, GROUP_W), F32)
    return pl.pallas_call(
        _gdn_kernel,
        grid=(bsz, nc),
        in_specs=[xf, xb, sf, sb, st, _full(alog.shape), _full(dtb.shape)],
        out_specs=[of, ob, st],
        out_shape=[o_shape, o_shape, jax.ShapeDtypeStruct(s0.shape, F32)],
        scratch_shapes=[pltpu.VMEM(s0.shape[1:], F32)],
        compiler_params=_cp("parallel", "arbitrary"),
        name="gdn_mixer",
    )(qkv, qkv, sm, sm, s0, alog, dtb)


def _split_hi_lo(a):
    hi = _mx(a)
    return hi, _mx(a - hi.astype(F32))


def _outproj_kernel(x_ref, ahf_ref, ahb_ref, ag_ref, bo_ref, cyf_ref, cyb_ref, cxc_ref, cz_ref,
                    dof_ref, dob_ref, dz_ref, wout_ref, gpost_ref, ga1_ref, gpre_ref, sc2_ref, sh2_ref,
                    dskip_ref, cnorm_ref, dnorm_ref, rhi_ref, rlo_ref, xo_ref, hp_ref, lg_ref):
    m_a = (ahf_ref[...] + ahb_ref[...]) * _gelu_tanh(ag_ref[...])
    y_c = (cyf_ref[...] + cyb_ref[...] + cxc_ref[...] * dskip_ref[...]) * _silu(cz_ref[...])
    m_c = _rms(y_c, cnorm_ref[...])
    o_d = dof_ref[...] + dob_ref[...]
    m_d = o_d * lax.rsqrt(_head_sums(o_d * o_d) * (1.0 / HEAD_DIM) + EPS) * dnorm_ref[...] * _silu(dz_ref[...])
    mix = jnp.concatenate([_mx(m_a), _mx(bo_ref[...]), _mx(m_c), _mx(m_d)], axis=1)
    ml = jnp.dot(mix, wout_ref[...], preferred_element_type=F32)
    x_new = x_ref[...] + ga1_ref[0] * _rms(ml, gpost_ref[...])
    xo_ref[...] = x_new
    h2 = _rms(x_new, gpre_ref[...]) * (1.0 + sc2_ref[0]) + sh2_ref[0]
    hi, lo = _split_hi_lo(h2)
    hp_ref[...] = _pack_pairs(h2)
    rhi = rhi_ref[...]
    lg_ref[...] = (jnp.dot(hi, rhi, preferred_element_type=F32) + jnp.dot(lo, rhi, preferred_element_type=F32)
                   + jnp.dot(hi, rlo_ref[...], preferred_element_type=F32))


def out_projection(x, mixers, w_out, gpost, ga1, gpre, sc2, sh2, dskip, cnorm, dnorm, router_w, tiles_per_group):
    t, d = x.shape
    tm = min(TOKEN_TILE, t)
    vec = lambda i: (i // tiles_per_group, 0, 0)
    row = lambda w: pl.BlockSpec((tm, w), lambda i: (i, 0))
    ne = LANES
    rhi, rlo = _split_hi_lo(jnp.pad(router_w.astype(F32), ((0, 0), (0, ne - router_w.shape[1]))))
    return pl.pallas_call(
        _outproj_kernel,
        grid=(t // tm,),
        in_specs=[row(d)] + [row(GROUP_W)] * 11
                 + [_full(w_out.shape), _full((1, d)), pl.BlockSpec((1, 1, d), vec), _full((1, d)),
                    pl.BlockSpec((1, 1, d), vec), pl.BlockSpec((1, 1, d), vec),
                    _full((1, GROUP_W)), _full((1, GROUP_W)), _full((1, GROUP_W)), _full(rhi.shape), _full(rlo.shape)],
        out_specs=[row(d), row(d // 2), row(ne)],
        out_shape=[jax.ShapeDtypeStruct((t, d), F32), jax.ShapeDtypeStruct((t, d // 2), jnp.uint32),
                   jax.ShapeDtypeStruct((t, ne), F32)],
        compiler_params=_cp("parallel"),
        name="out_projection",
    )(x, *mixers, w_out, gpost, ga1, gpre, sc2, sh2, dskip, cnorm, dnorm, rhi, rlo)


def _rank_before(vals, idx, count, stride):
    rank = jnp.zeros(vals.shape, jnp.int32)
    for j in range(count):
        other = vals[j * stride:j * stride + 1, :]
        ahead = (other > vals) | ((other == vals) & (idx > j))
        rank = rank + ahead.astype(jnp.int32)
    return rank


def _xor_partner(x, row, s):
    n = x.shape[0]
    return jnp.where((row & s) == 0, pltpu.roll(x, n - s, 0), pltpu.roll(x, s, 0))


def _route(logits, router_b):
    ne = N_EXPERTS
    gsz = ne // N_EXPERT_GROUPS
    scores = _sigmoid(logits.T[:ne, :])
    tm = scores.shape[1]
    biased = scores + router_b
    row = lax.broadcasted_iota(jnp.int32, (ne, tm), 0)
    m1, m2 = biased, jnp.full((ne, tm), -jnp.inf, F32)
    s = 1
    while s < gsz:
        o1, o2 = _xor_partner(m1, row, s), _xor_partner(m2, row, s)
        m2 = jnp.maximum(jnp.minimum(m1, o1), jnp.maximum(m2, o2))
        m1 = jnp.maximum(m1, o1)
        s *= 2
    gidx = row >> (gsz.bit_length() - 1)
    group_ok = _rank_before(m1 + m2, gidx, N_EXPERT_GROUPS, gsz) < TOPK_GROUPS
    choice = jnp.where(group_ok, biased, -jnp.inf)
    rank = jnp.full((ne, tm), TOP_K, jnp.int32)
    rest = choice
    for k in range(TOP_K):
        top = jnp.max(rest, axis=0, keepdims=True)
        first = jnp.min(jnp.where(rest == top, row, ne), axis=0, keepdims=True)
        hit = row == first
        rank = jnp.where(hit, k, rank)
        rest = jnp.where(hit, -jnp.inf, rest)
    gate = jnp.where(rank < TOP_K, scores, 0.0)
    gate = gate / jnp.sum(gate, axis=0, keepdims=True) * ROUTED_SCALE
    return gate, rank, row


def _to_token_major(x):
    n, tm = x.shape
    return jnp.concatenate([x, jnp.zeros((LANES - n, tm), x.dtype)], axis=0).T


def _router_kernel(lg_ref, rb_ref, gate_ref):
    gate, _, _ = _route(lg_ref[...], rb_ref[...])
    gate_ref[...] = _to_token_major(gate)


def _router_dispatch_kernel(lg_ref, rb_ref, gk_ref, ek_ref, pk_ref, cnt_ref, carry_s):
    i = pl.program_id(0)

    @pl.when(i == 0)
    def _():
        carry_s[...] = jnp.zeros(carry_s.shape, F32)

    gate, rank, row = _route(lg_ref[...], rb_ref[...])
    tm = gate.shape[1]
    picked = (rank < TOP_K).astype(F32)
    before = lax.broadcasted_iota(jnp.int32, (tm, tm), 0) < lax.broadcasted_iota(jnp.int32, (tm, tm), 1)
    pos = _dot(picked, before.astype(F32)) + carry_s[:, 0:1]
    carry_s[...] = carry_s[...] + jnp.sum(picked, axis=1, keepdims=True)
    gk, ek, pk = [], [], []
    for k in range(TOP_K):
        sel = rank == k
        gk.append(jnp.sum(jnp.where(sel, gate, 0.0), axis=0, keepdims=True))
        ek.append(jnp.sum(jnp.where(sel, row, 0), axis=0, keepdims=True))
        pk.append(jnp.sum(jnp.where(sel, pos, 0.0), axis=0, keepdims=True))
    gk_ref[...] = _to_token_major(jnp.concatenate(gk, axis=0))
    ek_ref[...] = jnp.concatenate(ek, axis=0)
    pk_ref[...] = jnp.concatenate(pk, axis=0).astype(jnp.int32)

    @pl.when(i == pl.num_programs(0) - 1)
    def _():
        cnt_ref[...] = carry_s[...].astype(jnp.int32)


def router_dispatch(logits, router_b):
    t, w = logits.shape
    tm = min(TOKEN_TILE, t)
    return pl.pallas_call(
        _router_dispatch_kernel,
        grid=(t // tm,),
        in_specs=[pl.BlockSpec((tm, w), lambda i: (i, 0)), _full((N_EXPERTS, 1))],
        out_specs=[pl.BlockSpec((tm, w), lambda i: (i, 0)),
                   pl.BlockSpec((TOP_K, tm), lambda i: (0, i)),
                   pl.BlockSpec((TOP_K, tm), lambda i: (0, i)),
                   _full((N_EXPERTS, LANES))],
        out_shape=[jax.ShapeDtypeStruct((t, w), F32), jax.ShapeDtypeStruct((TOP_K, t), jnp.int32),
                   jax.ShapeDtypeStruct((TOP_K, t), jnp.int32), jax.ShapeDtypeStruct((N_EXPERTS, LANES), jnp.int32)],
        scratch_shapes=[pltpu.VMEM((N_EXPERTS, LANES), F32)],
        compiler_params=_cp("arbitrary"),
        name="router_dispatch",
    )(logits, router_b.reshape(N_EXPERTS, 1).astype(F32))


def router_gates(logits, router_b):
    t, w = logits.shape
    tm = min(TOKEN_TILE, t)
    return pl.pallas_call(
        _router_kernel,
        grid=(t // tm,),
        in_specs=[pl.BlockSpec((tm, w), lambda i: (i, 0)), _full((N_EXPERTS, 1))],
        out_specs=pl.BlockSpec((tm, w), lambda i: (i, 0)),
        out_shape=jax.ShapeDtypeStruct((t, w), F32),
        compiler_params=_cp("parallel"),
        name="router_gates",
    )(logits, router_b.reshape(N_EXPERTS, 1).astype(F32))


def _moe_kernel(h_ref, gate_ref, x_ref, wg_ref, wu_ref, wd_ref, sg_ref, su_ref, sd_ref, gpost_ref, ga2_ref,
                o_ref, acc_s):
    e = pl.program_id(1)
    h = _mx(_unpack_pairs(h_ref[...]))

    @pl.when(e == 0)
    def _():
        hs = _silu(jnp.dot(h, sg_ref[...], preferred_element_type=F32)) * jnp.dot(h, su_ref[...], preferred_element_type=F32)
        acc_s[...] = jnp.dot(_mx(hs), sd_ref[...], preferred_element_type=F32)

    gates = gate_ref[...]
    lane = lax.broadcasted_iota(jnp.int32, gates.shape, 1)
    hid = []
    for j in range(MOE_EB):
        gcol = jnp.sum(jnp.where(lane == e * MOE_EB + j, gates, 0.0), axis=1, keepdims=True)
        g = jnp.dot(h, _mx(wg_ref[j]), preferred_element_type=F32)
        u = jnp.dot(h, _mx(wu_ref[j]), preferred_element_type=F32)
        hid.append(_mx(_silu(g) * u * gcol))
    wd = _mx(wd_ref[...]).reshape(MOE_EB * D_EXPERT, -1)
    acc_s[...] += jnp.dot(jnp.concatenate(hid, axis=1), wd, preferred_element_type=F32)

    @pl.when(e == pl.num_programs(1) - 1)
    def _():
        o_ref[...] = x_ref[...] + ga2_ref[0] * _rms(acc_s[...], gpost_ref[...])


def moe_ffn(h, gates, x, layer, wg, wu, wd, sg, su, sd, gpost, ga2, tiles_per_group):
    t, d = x.shape
    tm = min(MOE_TILE, t)
    _, ne, _, f = wg.shape
    row = lambda w: pl.BlockSpec((tm, w), lambda i, e: (i, 0))
    return pl.pallas_call(
        _moe_kernel,
        grid=(t // tm, ne // MOE_EB),
        in_specs=[row(h.shape[1]), row(gates.shape[1]), row(d),
                  pl.BlockSpec((None, MOE_EB, d, f), lambda i, e: (layer, e, 0, 0)),
                  pl.BlockSpec((None, MOE_EB, d, f), lambda i, e: (layer, e, 0, 0)),
                  pl.BlockSpec((None, MOE_EB, f, d), lambda i, e: (layer, e, 0, 0)),
                  _full(sg.shape), _full(su.shape), _full(sd.shape), _full((1, d)),
                  pl.BlockSpec((1, 1, d), lambda i, e: (i // tiles_per_group, 0, 0))],
        out_specs=row(d),
        out_shape=jax.ShapeDtypeStruct((t, d), F32),
        scratch_shapes=[pltpu.VMEM((tm, d), F32)],
        compiler_params=_cp("parallel", "arbitrary"),
        name="moe_ffn",
    )(h, gates, x, wg, wu, wd, sg, su, sd, gpost, ga2)


def moe_plan(counts, n_tokens):
    n_blocks = (n_tokens * TOP_K + N_EXPERTS * (MOE_BLOCK - 1) + MOE_BLOCK - 1) // MOE_BLOCK
    cnt = counts[:, 0]
    padded = (cnt + MOE_BLOCK - 1) // MOE_BLOCK * MOE_BLOCK
    pad_end = jnp.cumsum(padded)
    off = pad_end - padded
    start = jnp.arange(n_blocks, dtype=jnp.int32) * MOE_BLOCK
    be = jnp.minimum(jnp.sum(pad_end[None, :] <= start[:, None], axis=1), N_EXPERTS - 1).astype(jnp.int32)
    mine = be[:, None] == jnp.arange(N_EXPERTS, dtype=jnp.int32)[None, :]
    end = jnp.sum(jnp.where(mine, (off + cnt)[None, :], 0), axis=1)
    nv = jnp.clip(end - start, 0, MOE_BLOCK).astype(jnp.int32)
    return off.astype(jnp.int32), be, nv


def _rows_kernel(off_ref, ek_ref, pk_ref, dest_ref):
    ek = ek_ref[...]
    dest = pk_ref[...]
    for e in range(N_EXPERTS):
        dest = dest + jnp.where(ek == e, off_ref[e], 0)
    dest_ref[...] = dest


def moe_rows(off, ek, pk):
    k, t = ek.shape
    tm = min(MOE_PLAN_TILE, t)
    spec = pl.BlockSpec((k, tm), lambda i, off: (0, i))
    return pl.pallas_call(
        _rows_kernel,
        grid_spec=pltpu.PrefetchScalarGridSpec(num_scalar_prefetch=1, grid=(t // tm,),
                                               in_specs=[spec, spec], out_specs=spec),
        out_shape=jax.ShapeDtypeStruct((k, t), jnp.int32),
        compiler_params=_cp("arbitrary"),
        name="moe_rows",
    )(off, ek, pk)


U32 = jnp.uint32
HIGH_HALF = 0xFFFF0000


def _pack_pairs(x):
    w = x.shape[1] // 2
    bits = lax.bitcast_convert_type(x.astype(jnp.bfloat16).astype(F32), U32)
    return (bits[:, w:] & jnp.uint32(HIGH_HALF)) | (bits[:, :w] >> 16)


def _unpack_pairs(p):
    lo = lax.bitcast_convert_type(p << 16, F32)
    hi = lax.bitcast_convert_type(p & jnp.uint32(HIGH_HALF), F32)
    return jnp.concatenate([lo, hi], axis=1)


def _sc_workers():
    info = plsc.get_sparse_core_info()
    return info.num_cores, info.num_cores * info.num_subcores


def sc_scatter_rows(src, idx, n_rows):
    k, t = idx.shape
    w = src.shape[1]
    n_cores, n_workers = _sc_workers()
    per_worker = t // n_workers
    mesh = plsc.VectorSubcoreMesh(core_axis_name="c", subcore_axis_name="s")

    @functools.partial(
        pl.kernel, mesh=mesh, out_type=jax.ShapeDtypeStruct((n_rows, w), src.dtype),
        scratch_types=[pltpu.VMEM((k, SC_WINDOW), jnp.int32), pltpu.VMEM((SC_WINDOW, w), src.dtype),
                       pltpu.SemaphoreType.DMA])
    def scatter(s_hbm, i_hbm, o_hbm, idx_v, rows_v, sem):
        base = (lax.axis_index("s") * n_cores + lax.axis_index("c")) * per_worker

        @pl.loop(0, per_worker // SC_WINDOW)
        def _(j):
            off = base + j * SC_WINDOW
            pltpu.sync_copy(i_hbm.at[:, pl.ds(off, SC_WINDOW)], idx_v)
            pltpu.sync_copy(s_hbm.at[pl.ds(off, SC_WINDOW)], rows_v)
            for kk in range(k):
                pltpu.async_copy(rows_v, o_hbm.at[idx_v.at[kk]], sem).wait()

    return scatter(src, idx)


def sc_gather_rows(table, idx):
    n = idx.shape[0]
    w = table.shape[1]
    n_cores, n_workers = _sc_workers()
    per_worker = n // n_workers
    mesh = plsc.VectorSubcoreMesh(core_axis_name="c", subcore_axis_name="s")

    @functools.partial(
        pl.kernel, mesh=mesh, out_type=jax.ShapeDtypeStruct((n, w), table.dtype),
        scratch_types=[pltpu.VMEM((SC_WINDOW,), jnp.int32), pltpu.VMEM((SC_WINDOW, w), table.dtype),
                       pltpu.SemaphoreType.DMA])
    def gather(t_hbm, i_hbm, o_hbm, idx_v, rows_v, sem):
        base = (lax.axis_index("s") * n_cores + lax.axis_index("c")) * per_worker

        @pl.loop(0, per_worker // SC_WINDOW)
        def _(j):
            off = base + j * SC_WINDOW
            pltpu.sync_copy(i_hbm.at[pl.ds(off, SC_WINDOW)], idx_v)
            pltpu.async_copy(t_hbm.at[idx_v], rows_v, sem).wait()
            pltpu.sync_copy(rows_v, o_hbm.at[pl.ds(off, SC_WINDOW)])

    return gather(table, idx)


def _expert_kernel(be_ref, nv_ref, xs_ref, wg_ref, wu_ref, wd_ref, ys_ref):
    nv = nv_ref[pl.program_id(0)]

    def ffn(x):
        x = _mx(x)
        hid = (_silu(jnp.dot(x, _mx(wg_ref[0]), preferred_element_type=F32))
               * jnp.dot(x, _mx(wu_ref[0]), preferred_element_type=F32))
        ys_ref[...] = _pack_pairs(jnp.dot(_mx(hid), _mx(wd_ref[0]), preferred_element_type=F32))

    @pl.when(nv == MOE_BLOCK)
    def _():
        ffn(_unpack_pairs(xs_ref[...]))

    @pl.when((nv > 0) & (nv < MOE_BLOCK))
    def _():
        x = _unpack_pairs(xs_ref[...])
        rows = lax.broadcasted_iota(jnp.int32, x.shape, 0)
        ffn(jnp.where(rows < nv, x, 0.0))

    @pl.when(nv == 0)
    def _():
        ys_ref[...] = jnp.zeros(ys_ref.shape, U32)


def moe_experts(xs, be, nv, layer, wg, wu, wd):
    n_rows, w = xs.shape
    _, _, d, f = wg.shape
    return pl.pallas_call(
        _expert_kernel,
        grid_spec=pltpu.PrefetchScalarGridSpec(
            num_scalar_prefetch=2,
            grid=(n_rows // MOE_BLOCK,),
            in_specs=[pl.BlockSpec((MOE_BLOCK, w), lambda b, be, nv: (b, 0)),
                      pl.BlockSpec((None, 1, d, f), lambda b, be, nv: (layer, be[b], 0, 0)),
                      pl.BlockSpec((None, 1, d, f), lambda b, be, nv: (layer, be[b], 0, 0)),
                      pl.BlockSpec((None, 1, f, d), lambda b, be, nv: (layer, be[b], 0, 0))],
            out_specs=pl.BlockSpec((MOE_BLOCK, w), lambda b, be, nv: (b, 0))),
        out_shape=jax.ShapeDtypeStruct((n_rows, w), U32),
        compiler_params=_cp("arbitrary"),
        name="moe_experts",
    )(be, nv, xs, wg, wu, wd)


def _combine_kernel(yg_ref, gk_ref, hp_ref, x_ref, sg_ref, su_ref, sd_ref, gpost_ref, ga2_ref, o_ref):
    h = _mx(_unpack_pairs(hp_ref[...]))
    hs = _silu(jnp.dot(h, sg_ref[...], preferred_element_type=F32)) * jnp.dot(h, su_ref[...], preferred_element_type=F32)
    f = jnp.dot(_mx(hs), sd_ref[...], preferred_element_type=F32)
    gk = gk_ref[...]
    for k in range(TOP_K):
        f = f + gk[:, k:k + 1] * _unpack_pairs(yg_ref[k])
    o_ref[...] = x_ref[...] + ga2_ref[0] * _rms(f, gpost_ref[...])


def moe_combine(yg, gk, hp, x, sg, su, sd, gpost, ga2, tiles_per_group):
    t, d = x.shape
    tm = min(MOE_ROW_TILE, t)
    w = hp.shape[1]
    row = lambda n: pl.BlockSpec((tm, n), lambda i: (i, 0))
    return pl.pallas_call(
        _combine_kernel,
        grid=(t // tm,),
        in_specs=[pl.BlockSpec((TOP_K, tm, w), lambda i: (0, i, 0)),
                  row(gk.shape[1]), row(w), row(d), _full(sg.shape), _full(su.shape), _full(sd.shape), _full((1, d)),
                  pl.BlockSpec((1, 1, d), lambda i: (i // tiles_per_group, 0, 0))],
        out_specs=row(d),
        out_shape=jax.ShapeDtypeStruct((t, d), F32),
        compiler_params=_cp("parallel"),
        name="moe_combine",
    )(yg, gk, hp, x, sg, su, sd, gpost, ga2)


def _reorder_w_in(w_in):
    c = np.cumsum((0,) + (GROUP_W, GROUP_W, GROUP_W, GROUP_W, GROUP_W, GROUP_W, 2 * SSD_STATE, 2 * SSD_STATE,
                          GROUP_W, 2 * N_HEADS, GROUP_W, GROUP_W, GROUP_W, GROUP_W, 2 * N_HEADS, 2 * N_HEADS))
    seg = lambda a, b: w_in[:, c[a]:c[b]]
    small = jnp.concatenate([seg(9, 10), seg(14, 15), seg(15, 16),
                             jnp.zeros((w_in.shape[0], LANES - 6 * N_HEADS), w_in.dtype)], axis=1)
    return jnp.concatenate([seg(0, 1), seg(5, 8), seg(10, 13), seg(1, 5), seg(8, 9), seg(13, 14), small], axis=1)


def kernel(x, c, ctx, c_ctx, w_mod, b_mod, g_pre_mix, g_post_mix, g_pre_ffn, g_post_ffn, w_in, w_out, lru_conv_w, lru_conv_b, lru_wa, lru_ba, lru_wx, lru_bx, lru_lambda, na_bias, ssd_conv_w, ssd_conv_b, ssd_a_log, ssd_dt_bias, ssd_d, ssd_norm, gdn_conv_w, gdn_a_log, gdn_dt_bias, gdn_norm, router_w, router_b, we_gate, we_up, we_down, ws_gate, ws_up, ws_down):
    bsz, seq, d = x.shape
    n_ctx = ctx.shape[1]
    depth = w_mod.shape[0]
    lat_tpg = seq // min(TOKEN_TILE, seq)
    ctx_tpg = max(bsz * n_ctx // TOKEN_TILE, 1)
    ctx_mpg = max(bsz * n_ctx // MOE_TILE, 1)

    cond = _pad_rows(jnp.concatenate([c, c_ctx[None, :]], axis=0))
    mod = modulation(cond, w_mod, b_mod).reshape(depth, SUBLANES, N_MOD, d)
    rope = rope_tables(seq)
    row = lambda v: v[None, :].astype(F32)

    def layer_params(l):
        m_lat = [mod[l, :bsz, k][:, None, :] for k in range(N_MOD)]
        m_ctx = [mod[l, bsz:bsz + 1, k][:, None, :] for k in range(N_MOD)]
        w_in_l = _reorder_w_in(w_in[l]).astype(MXU_DTYPE)
        conv = (_pad_rows(lru_conv_w[l]), row(lru_conv_b[l]), _pad_rows(ssd_conv_w[l]), row(ssd_conv_b[l]),
                _pad_rows(gdn_conv_w[l]))
        return dict(
            m_lat=m_lat, m_ctx=m_ctx, w_in=w_in_l, conv=conv,
            lru=lru_params(lru_wa[l], lru_ba[l], lru_wx[l], lru_bx[l], lru_lambda[l]),
            ssd=ssd_params(ssd_a_log[l], ssd_dt_bias[l]), gdn=gdn_params(gdn_a_log[l], gdn_dt_bias[l]),
            epi=(w_out[l].astype(MXU_DTYPE), row(g_post_mix[l])),
            epi_tail=(row(jnp.repeat(ssd_d[l], HEAD_DIM)), row(ssd_norm[l]), row(jnp.tile(gdn_norm[l], N_HEADS)),
                      router_w[l]),
            routed=(l, we_gate, we_up, we_down),
            shared=(ws_gate[l].astype(MXU_DTYPE), ws_up[l].astype(MXU_DTYPE), ws_down[l].astype(MXU_DTYPE),
                    row(g_post_ffn[l])))

    def context_mixers(l, p, xc):
        pc = in_projection(xc, row(g_pre_mix[l]), p['m_ctx'][1], p['m_ctx'][0], p['w_in'], p['conv'], None, n_ctx,
                           bsz * n_ctx // min(TOKEN_TILE, n_ctx))
        a_f, a_b, a_st = lru_mixer(pc[P_AX], jnp.zeros((bsz, SUBLANES, GROUP_W), F32), bsz, *p['lru'])
        kc = pc[P_BK].reshape(bsz, n_ctx, GROUP_W)
        vc = pc[P_BV].reshape(bsz, n_ctx, GROUP_W)
        b_o = ctx_attention(pc[P_BQ].reshape(bsz, n_ctx, GROUP_W), kc, vc).reshape(bsz * n_ctx, GROUP_W)
        c_f, c_b, c_st = ssd_mixer(pc[P_CX], pc[P_SM], jnp.zeros((bsz, 2, N_HEADS, SSD_STATE, HEAD_DIM), F32),
                                   bsz, *p['ssd'])
        d_f, d_b, d_st = gdn_mixer(pc[P_DX], pc[P_SM], jnp.zeros((bsz, 2, N_HEADS, HEAD_DIM, HEAD_DIM), F32),
                                   bsz, *p['gdn'])
        mix = (a_f, a_b, pc[P_AG], b_o, c_f, c_b, pc[P_CX], pc[P_CZ], d_f, d_b, pc[P_DZ])
        return dict(mix=mix, a_st=a_st, kc=kc, vc=vc, c_st=c_st, d_st=d_st)

    def context_ffn(l, p, cm, xc):
        m = p['m_ctx']
        xc, hp, lg = out_projection(xc, cm['mix'], *p['epi'], m[2], row(g_pre_ffn[l]), m[4], m[3], *p['epi_tail'], ctx_tpg)
        return moe_ffn(hp, router_gates(lg, router_b[l]), xc, *p['routed'], *p['shared'], m[5], ctx_mpg)

    xl = x.reshape(bsz * seq, d)
    xc = ctx.reshape(bsz * n_ctx, d)
    p = layer_params(0)
    cm = context_mixers(0, p, xc)
    for l in range(depth):
        last = l == depth - 1
        m = p['m_lat']
        pl_ = in_projection(xl, row(g_pre_mix[l]), m[1], m[0], p['w_in'], p['conv'], rope, seq, lat_tpg)
        a_f, a_b, _ = lru_mixer(pl_[P_AX], cm['a_st'], bsz, *p['lru'])
        b_o = na_mixer(pl_[P_BQ], pl_[P_BK], pl_[P_BV], cm['kc'], cm['vc'], na_bias_slabs(na_bias[l]), bsz)
        c_f, c_b, _ = ssd_mixer(pl_[P_CX], pl_[P_SM], cm['c_st'], bsz, *p['ssd'])
        d_f, d_b, _ = gdn_mixer(pl_[P_DX], pl_[P_SM], cm['d_st'], bsz, *p['gdn'])
        mix_l = (a_f, a_b, pl_[P_AG], b_o, c_f, c_b, pl_[P_CX], pl_[P_CZ], d_f, d_b, pl_[P_DZ])
        xl, hp, lg = out_projection(xl, mix_l, *p['epi'], m[2], row(g_pre_ffn[l]), m[4], m[3], *p['epi_tail'], lat_tpg)
        gk, ek, pk, cnt = router_dispatch(lg, router_b[l])
        off, be, nv = moe_plan(cnt, bsz * seq)
        dest = moe_rows(off, ek, pk)
        xs = sc_scatter_rows(hp, dest, be.shape[0] * MOE_BLOCK)
        if not last:
            xc = context_ffn(l, p, cm, xc)
        ys = moe_experts(xs, be, nv, *p['routed'])
        yg = sc_gather_rows(ys, dest.reshape(-1)).reshape(TOP_K, bsz * seq, d // 2)
        if not last:
            p_next = layer_params(l + 1)
            cm = context_mixers(l + 1, p_next, xc)
        xl = moe_combine(yg, gk, hp, xl, *p['shared'], m[5], seq // min(MOE_ROW_TILE, seq))
        if not last:
            p = p_next
    return xl.reshape(bsz, seq, d)
```

```python
import functools
import math

import jax
import jax.numpy as jnp
import numpy as np
from jax import lax
from jax.experimental import pallas as pl
from jax.experimental.pallas import tpu as pltpu
from jax.experimental.pallas import tpu_sc as plsc

F32 = jnp.float32
MXU_DTYPE = jnp.bfloat16
HI = lax.Precision.HIGHEST

D_MODEL = 1024
GRID_W = 64
GROUP_W = 256
HEAD_DIM = 64
N_HEADS = 4
EPS = 1e-6
NEG_INF = -1e30
N_MOD = 6
LRU_C = 8.0
NA_WIN_ROWS = 8
NA_WIN_COLS = 16
SSD_STATE = 128
SSD_GROUPS = 2
ROPE_BASE = 10000.0
ROPE_AXIS_DIM = HEAD_DIM // 2
N_EXPERTS = 64
N_EXPERT_GROUPS = 8
TOPK_GROUPS = 4
TOP_K = 8
D_EXPERT = 256
ROUTED_SCALE = 2.5

LANES = 128
SUBLANES = 8
VMEM_LIMIT = 56 * 1024 * 1024

TOKEN_TILE = 512
LRU_CHUNK = 256
SSD_CHUNK = 256
GDN_CHUNK = 64
GDN_TILE = 256
GDN_SUB = 128
GDN_BASE = 16
MOE_TILE = 1024
MOE_EB = 4
MOE_BLOCK = 1024
MOE_ROW_TILE = 512
LRU_UNROLL = 8
MOE_PLAN_TILE = 2048
SC_WINDOW = 128

P_WIDTHS = (256, 768, 768, 256, 256, 256, 256, 256, 256, 128)
(P_AX, P_CX, P_DX, P_AG, P_BQ, P_BK, P_BV, P_CZ, P_DZ, P_SM) = range(10)
P_CONV_GROUPS = 3
SM_DT, SM_BETA, SM_DECAY = 0, 8, 16


def _cp(*sem):
    return pltpu.CompilerParams(dimension_semantics=sem, vmem_limit_bytes=VMEM_LIMIT)


def _mx(x):
    return x.astype(MXU_DTYPE)


def _dot(a, b):
    return jnp.dot(_mx(a), _mx(b), preferred_element_type=F32)


def _dot_nt(a, b):
    return lax.dot_general(_mx(a), _mx(b), (((1,), (1,)), ((), ())), preferred_element_type=F32)


def _dot_tn(a, b):
    return lax.dot_general(_mx(a), _mx(b), (((0,), (0,)), ((), ())), preferred_element_type=F32)


def _dot_hi(a, b):
    return jnp.dot(a, b, preferred_element_type=F32, precision=HI)


def _sigmoid(x):
    return 1.0 / (1.0 + jnp.exp(-x))


def _silu(x):
    return x * _sigmoid(x)


def _softplus(x):
    return jnp.maximum(x, 0.0) + jnp.log1p(jnp.exp(-jnp.abs(x)))


def _gelu_tanh(x):
    return 0.5 * x * (1.0 + jnp.tanh(math.sqrt(2.0 / math.pi) * (x + 0.044715 * (x * x * x))))


def _rms(x, g):
    return x * lax.rsqrt(jnp.mean(x * x, axis=-1, keepdims=True) + EPS) * g


def _full(shape):
    n = len(shape)
    return pl.BlockSpec(shape, lambda *_: (0,) * n)


MOD_COLS = 1536


def _mod_kernel(c_ref, w_ref, b_ref, o_ref):
    o_ref[0] = _dot_hi(_silu(c_ref[...]), w_ref[0]) + b_ref[0]


def modulation(cond, w_mod, b_mod):
    depth, d, n = w_mod.shape
    return pl.pallas_call(
        _mod_kernel,
        grid=(depth, n // MOD_COLS),
        in_specs=[pl.BlockSpec((SUBLANES, d), lambda l, j: (0, 0)),
                  pl.BlockSpec((1, d, MOD_COLS), lambda l, j: (l, 0, j)),
                  pl.BlockSpec((1, 1, MOD_COLS), lambda l, j: (l, 0, j))],
        out_specs=pl.BlockSpec((1, SUBLANES, MOD_COLS), lambda l, j: (l, 0, j)),
        out_shape=jax.ShapeDtypeStruct((depth, SUBLANES, n), F32),
        compiler_params=_cp("parallel", "parallel"),
        name="modulation",
    )(cond, w_mod, b_mod.reshape(depth, 1, n))


def _inproj_kernel(*refs, tiles_per_seq, rope):
    (x_ref, xp_ref, xn_ref, g_ref, sc_ref, sh_ref, w_ref, lcw_ref, lcb_ref, scw_ref, scb_ref, gcw_ref) = refs[:12]
    cos_ref, sin_ref = (refs[12], refs[13]) if rope else (None, None)
    o_refs = refs[14:] if rope else refs[12:]
    i = pl.program_id(0)
    norm = lambda v: _rms(v, g_ref[...]) * (1.0 + sc_ref[0]) + sh_ref[0]
    p = _dot(norm(x_ref[...]), w_ref[...])
    n_conv = sum(P_WIDTHS[:P_CONV_GROUPS])
    ph = _dot(norm(jnp.concatenate([xp_ref[...], xn_ref[...]], axis=0)), w_ref[:, :n_conv])
    pos = i % tiles_per_seq
    prev = jnp.where(pos == 0, 0.0, ph[:SUBLANES])
    nxt = jnp.where(pos == tiles_per_seq - 1, 0.0, ph[SUBLANES:])
    row = lax.broadcasted_iota(jnp.int32, (SUBLANES, n_conv), 0)
    halo = jnp.where(row < 2, pltpu.roll(prev, 2, 0), jnp.where(row == 2, pltpu.roll(nxt, 2, 0), 0.0))
    c0, c1, c2 = GROUP_W, GROUP_W + 3 * GROUP_W, n_conv
    lru_u = _dwconv(p[:, :c0], halo[:, :c0], lcw_ref[...], lcb_ref[...])
    ssd_x = _silu(_dwconv(p[:, c0:c1], halo[:, c0:c1], scw_ref[...], scb_ref[...]))
    qkv = _silu(_dwconv(p[:, c1:c2], halo[:, c1:c2], gcw_ref[...]))
    qn = _l2norm_heads(qkv[:, :GROUP_W])
    kn = _l2norm_heads(qkv[:, GROUP_W:2 * GROUP_W])
    if rope:
        cos, sin = cos_ref[...], sin_ref[...]
        qn = qn * cos + _swap16(qn) * sin
        kn = kn * cos + _swap16(kn) * sin
    outs = [lru_u, ssd_x, jnp.concatenate([qn * (HEAD_DIM ** -0.5), kn, qkv[:, 2 * GROUP_W:]], axis=1)]
    off = n_conv
    for o_ref, w in zip(o_refs, P_WIDTHS):
        if outs:
            o_ref[...] = outs.pop(0)
        else:
            o_ref[...] = p[:, off:off + w].astype(o_ref.dtype)
            off += w


def in_projection(x, g, sc, sh, w, conv, rope, seq_len, tiles_per_group):
    t, d = x.shape
    tm = min(TOKEN_TILE, seq_len)
    tps = seq_len // tm
    hb = tm // SUBLANES
    vec = lambda i: (i // tiles_per_group, 0, 0)
    ins = [x, x, x, g, sc, sh, w, *conv]
    specs = [pl.BlockSpec((tm, d), lambda i: (i, 0)),
             pl.BlockSpec((SUBLANES, d), lambda i: (jnp.maximum(i * hb - 1, 0), 0)),
             pl.BlockSpec((SUBLANES, d), lambda i: (jnp.minimum((i + 1) * hb, t // SUBLANES - 1), 0)),
             _full((1, d)), pl.BlockSpec((1, 1, d), vec), pl.BlockSpec((1, 1, d), vec), _full(w.shape)]
    specs += [_full(a.shape) for a in conv]
    if rope is not None:
        ins += list(rope)
        specs += [pl.BlockSpec((tm, GROUP_W), lambda i: (i % tps, 0))] * 2
    return pl.pallas_call(
        functools.partial(_inproj_kernel, tiles_per_seq=tps, rope=rope is not None),
        grid=(t // tm,),
        in_specs=specs,
        out_specs=[pl.BlockSpec((tm, wd), lambda i: (i, 0)) for wd in P_WIDTHS],
        out_shape=[jax.ShapeDtypeStruct((t, wd), MXU_DTYPE if k in (P_BQ, P_BK, P_BV) else F32)
                   for k, wd in enumerate(P_WIDTHS)],
        compiler_params=_cp("parallel"),
        name="in_projection",
    )(*ins)


def _dwconv(x, halo, w, b=None):
    q = x.shape[0]
    row = lax.broadcasted_iota(jnp.int32, (SUBLANES, x.shape[1]), 0)

    def shifted(s, keep_rolled, edge):
        r = pltpu.roll(x, s % q, 0)
        if s > 0:
            return jnp.concatenate([jnp.where(keep_rolled, r[:SUBLANES], edge), r[SUBLANES:]], axis=0)
        return jnp.concatenate([r[:q - SUBLANES], jnp.where(keep_rolled, r[q - SUBLANES:], edge)], axis=0)

    xm2 = shifted(2, row >= 2, halo)
    xm1 = shifted(1, row >= 1, pltpu.roll(halo, SUBLANES - 1, 0))
    xp1 = shifted(-1, row < SUBLANES - 1, pltpu.roll(halo, SUBLANES - 3, 0))
    y = w[0:1] * xm2 + w[1:2] * xm1 + w[2:3] * x + w[3:4] * xp1
    return y if b is None else y + b


def _pad_rows(a, rows=SUBLANES):
    return jnp.concatenate([a, jnp.zeros((rows - a.shape[0],) + a.shape[1:], a.dtype)], axis=0)


def _chunk_specs(nc, q, c):
    fwd = pl.BlockSpec((q, c), lambda b, i: (b * nc + i, 0))
    bwd = pl.BlockSpec((q, c), lambda b, i: (b * nc + nc - 1 - i, 0))
    return fwd, bwd


def _lru_kernel(xf_ref, xb_ref, h0_ref, wg_ref, bg_ref, lam_ref,
                yf_ref, yb_ref, hfin_ref, af_s, bf_s, ab_s, bb_s, carry_s):
    i = pl.program_id(1)
    q = xf_ref.shape[0]

    @pl.when(i == 0)
    def _():
        carry_s[...] = h0_ref[0]

    def coeffs(x_ref, d, a_s, b_s):
        u = x_ref[...]
        g = _dot(u, wg_ref[:, 2 * GROUP_W * d:2 * GROUP_W * (d + 1)]) + bg_ref[:, 2 * GROUP_W * d:2 * GROUP_W * (d + 1)]
        r = _sigmoid(g[:, :GROUP_W])
        gate_in = _sigmoid(g[:, GROUP_W:])
        log_a = -LRU_C * r * _softplus(-lam_ref[d:d + 1, :])
        a_s[...] = jnp.exp(log_a)
        b_s[...] = jnp.sqrt(1.0 - jnp.exp(2.0 * log_a)) * (gate_in * u)

    coeffs(xf_ref, 0, af_s, bf_s)
    coeffs(xb_ref, 1, ab_s, bb_s)

    ng = q // SUBLANES
    row = lax.broadcasted_iota(jnp.int32, (SUBLANES, GROUP_W), 0)

    def body(g, hs):
        h_f, h_b = hs
        i0 = pl.multiple_of(g * SUBLANES, SUBLANES)
        a = af_s[pl.ds(i0, SUBLANES), :]
        b = bf_s[pl.ds(i0, SUBLANES), :]
        for s in (1, 2, 4):
            m = row >= s
            b = jnp.where(m, a * pltpu.roll(b, s, 0) + b, b)
            a = jnp.where(m, a * pltpu.roll(a, s, 0), a)
        h = b + a * h_f
        yf_ref[pl.ds(i0, SUBLANES), :] = h
        h_f = h[SUBLANES - 1:SUBLANES, :]
        j0 = pl.multiple_of((ng - 1 - g) * SUBLANES, SUBLANES)
        a = ab_s[pl.ds(j0, SUBLANES), :]
        b = bb_s[pl.ds(j0, SUBLANES), :]
        for s in (1, 2, 4):
            m = row < SUBLANES - s
            b = jnp.where(m, a * pltpu.roll(b, SUBLANES - s, 0) + b, b)
            a = jnp.where(m, a * pltpu.roll(a, SUBLANES - s, 0), a)
        h = b + a * h_b
        yb_ref[pl.ds(j0, SUBLANES), :] = h
        return h_f, h[0:1, :]

    h_f, h_b = lax.fori_loop(0, ng, body, (carry_s[0:1, :], carry_s[1:2, :]), unroll=LRU_UNROLL)
    carry_s[0:1, :] = h_f
    carry_s[1:2, :] = h_b

    @pl.when(i == pl.num_programs(1) - 1)
    def _():
        hfin_ref[0] = carry_s[...]


def _block_diag(w):
    h, a, b = w.shape
    return jnp.einsum('hij,hg->higj', w, jnp.eye(h, dtype=w.dtype)).reshape(h * a, h * b)


def lru_params(wa, ba, wx, bx, lam):
    wg = jnp.concatenate([_block_diag(wa[0]), _block_diag(wx[0]), _block_diag(wa[1]), _block_diag(wx[1])], axis=1)
    bg = jnp.concatenate([ba[0], bx[0], ba[1], bx[1]])[None, :]
    return wg.astype(MXU_DTYPE), bg, _pad_rows(lam)


def lru_mixer(x, h0, bsz, wg, bg, lam):
    t, c = x.shape
    s = t // bsz
    q = min(LRU_CHUNK, s)
    nc = s // q
    xf, xb = _chunk_specs(nc, q, c)
    st = pl.BlockSpec((1, SUBLANES, c), lambda b, i: (b, 0, 0))
    return pl.pallas_call(
        _lru_kernel,
        grid=(bsz, nc),
        in_specs=[xf, xb, st, _full(wg.shape), _full(bg.shape), _full(lam.shape)],
        out_specs=[xf, xb, st],
        out_shape=[jax.ShapeDtypeStruct((t, c), F32), jax.ShapeDtypeStruct((t, c), F32),
                   jax.ShapeDtypeStruct((bsz, SUBLANES, c), F32)],
        scratch_shapes=[pltpu.VMEM((q, c), F32)] * 4 + [pltpu.VMEM((SUBLANES, c), F32)],
        compiler_params=_cp("parallel", "arbitrary"),
        name="lru_mixer",
    )(x, x, h0, wg, bg, lam)


NA_KEYS = NA_WIN_ROWS * GRID_W
NA_ROW_BLOCK = 8


def na_bias_slabs(table):
    qc = np.arange(GRID_W)[:, None]
    kc = np.arange(GRID_W)[None, :]
    win0 = np.clip(qc - NA_WIN_COLS // 2, 0, GRID_W - NA_WIN_COLS)
    ok = (kc >= win0) & (kc < win0 + NA_WIN_COLS)
    dc = np.clip(kc - qc + NA_WIN_COLS - 1, 0, 2 * NA_WIN_COLS - 2)
    dr = np.arange(NA_WIN_ROWS)[:, None] + np.arange(NA_WIN_ROWS)[None, :]
    b = table.astype(F32)[:, dr][:, :, :, dc]
    b = jnp.where(ok[None, None, None], b, NEG_INF)
    h = table.shape[0]
    return b.transpose(0, 1, 3, 2, 4).reshape(h, NA_WIN_ROWS, GRID_W, NA_KEYS)


def _na_span_start(j, rows):
    return jnp.clip(j * NA_ROW_BLOCK - NA_WIN_ROWS // 2, 0, rows - (NA_ROW_BLOCK + NA_WIN_ROWS - 1))


def _na_kernel(q_ref, kw_ref, vw_ref, kc_ref, vc_ref, slab_ref, o_ref, *, rows):
    j = pl.program_id(1)
    ustart = _na_span_start(j, rows)
    q = q_ref[...] * (HEAD_DIM ** -0.5)
    kc, vc = kc_ref[0], vc_ref[0]
    heads = [slice(h * HEAD_DIM, (h + 1) * HEAD_DIM) for h in range(N_HEADS)]
    qrows = [slice(i * GRID_W, (i + 1) * GRID_W) for i in range(NA_ROW_BLOCK)]
    kws, vws, offs = [], [], []
    for i in range(NA_ROW_BLOCK):
        r = j * NA_ROW_BLOCK + i
        r0 = jnp.clip(r - NA_WIN_ROWS // 2, 0, rows - NA_WIN_ROWS)
        start = pl.multiple_of((r0 - ustart) * GRID_W, GRID_W)
        kws.append(kw_ref[pl.ds(start, NA_KEYS), :])
        vws.append(vw_ref[pl.ds(start, NA_KEYS), :])
        offs.append(r0 - r + NA_WIN_ROWS - 1)
    s_ctx = [_dot_nt(q[:, sl], kc[:, sl]) for sl in heads]
    s_loc = [[_dot_nt(q[qr, sl], kws[i][:, sl]) + slab_ref[h, offs[i]] for h, sl in enumerate(heads)]
             for i, qr in enumerate(qrows)]
    m = [[jnp.maximum(jnp.max(s_loc[i][h], axis=-1, keepdims=True), jnp.max(s_ctx[h][qr], axis=-1, keepdims=True))
          for h in range(N_HEADS)] for i, qr in enumerate(qrows)]
    p_loc = [[jnp.exp(s_loc[i][h] - m[i][h]) for h in range(N_HEADS)] for i in range(NA_ROW_BLOCK)]
    p_ctx = [jnp.exp(s_ctx[h] - jnp.concatenate([m[i][h] for i in range(NA_ROW_BLOCK)], axis=0))
             for h in range(N_HEADS)]
    o_ctx = [_dot(p_ctx[h], vc[:, sl]) for h, sl in enumerate(heads)]
    rows_out = []
    for i, qr in enumerate(qrows):
        outs = []
        for h, sl in enumerate(heads):
            den = jnp.sum(p_loc[i][h], axis=-1, keepdims=True) + jnp.sum(p_ctx[h][qr], axis=-1, keepdims=True)
            outs.append((_dot(p_loc[i][h], vws[i][:, sl]) + o_ctx[h][qr]) / den)
        rows_out.append(jnp.concatenate(outs, axis=1))
    o_ref[...] = jnp.concatenate(rows_out, axis=0)


def na_mixer(q, k, v, kc, vc, slabs, bsz):
    t, c = q.shape
    s = t // bsz
    rows = s // GRID_W
    n_ctx = kc.shape[1]
    span = (NA_ROW_BLOCK + NA_WIN_ROWS - 1) * GRID_W

    def win(b, j):
        return ((b * rows + _na_span_start(j, rows)) * GRID_W, 0)

    wspec = pl.BlockSpec((pl.Element(span), pl.Element(c)), win)
    cspec = pl.BlockSpec((1, n_ctx, c), lambda b, j: (b, 0, 0))
    qspec = pl.BlockSpec((NA_ROW_BLOCK * GRID_W, c), lambda b, j: (b * (rows // NA_ROW_BLOCK) + j, 0))
    return pl.pallas_call(
        functools.partial(_na_kernel, rows=rows),
        grid=(bsz, rows // NA_ROW_BLOCK),
        in_specs=[qspec, wspec, wspec, cspec, cspec, _full(slabs.shape)],
        out_specs=qspec,
        out_shape=jax.ShapeDtypeStruct((t, c), F32),
        compiler_params=_cp("parallel", "arbitrary"),
        name="na_mixer",
    )(q, k, v, kc, vc, slabs)


def _ctx_attn_kernel(q_ref, k_ref, v_ref, o_ref):
    q = q_ref[0] * (HEAD_DIM ** -0.5)
    k, v = k_ref[0], v_ref[0]
    outs = []
    for h in range(N_HEADS):
        sl = slice(h * HEAD_DIM, (h + 1) * HEAD_DIM)
        s = _dot_nt(q[:, sl], k[:, sl])
        p = jnp.exp(s - jnp.max(s, axis=-1, keepdims=True))
        outs.append(_dot(p, v[:, sl]) / jnp.sum(p, axis=-1, keepdims=True))
    o_ref[0] = jnp.concatenate(outs, axis=1)


def ctx_attention(q, k, v):
    spec = pl.BlockSpec((1,) + q.shape[1:], lambda b: (b, 0, 0))
    return pl.pallas_call(
        _ctx_attn_kernel,
        grid=(q.shape[0],),
        in_specs=[spec, spec, spec],
        out_specs=spec,
        out_shape=jax.ShapeDtypeStruct(q.shape, F32),
        compiler_params=_cp("parallel"),
        name="ctx_attention",
    )(q, k, v)


def _small_vec(vals, off):
    v = jnp.zeros((LANES,), F32).at[off:off + 2 * N_HEADS].set(vals.reshape(-1).astype(F32))
    return v[None, :]


def _lane_mask(off):
    lane = lax.broadcasted_iota(jnp.int32, (1, LANES), 1)
    return (lane >= off) & (lane < off + 2 * N_HEADS)


def _tri_masks(q):
    rowi = lax.broadcasted_iota(jnp.int32, (q, q), 0)
    coli = lax.broadcasted_iota(jnp.int32, (q, q), 1)
    return rowi, coli


def _ssd_kernel(xf_ref, xb_ref, sf_ref, sb_ref, h0_ref, dtb_ref, alog_ref,
                yf_ref, yb_ref, hfin_ref, state_s):
    i = pl.program_id(1)
    q = xf_ref.shape[0]

    @pl.when(i == 0)
    def _():
        state_s[...] = h0_ref[0]

    rowi, coli = _tri_masks(q)
    a_neg = jnp.where(_lane_mask(SM_DT), -jnp.exp(alog_ref[...]), 0.0)

    chains = [(d, h) for d in range(2) for h in range(N_HEADS)]
    per_head = N_HEADS // SSD_GROUPS
    scores, xdt, c_in, b_out, e_last = [], [], [], [], []
    for d, (x_ref, sm_ref) in enumerate(((xf_ref, sf_ref), (xb_ref, sb_ref))):
        xbc = x_ref[...]
        dt = _softplus(sm_ref[...] + dtb_ref[...])
        keep = (rowi >= coli) if d == 0 else (rowi <= coli)
        acum = _dot_tri(keep, dt * a_neg)
        acum_t = acum.T
        last = acum[q - 1:q, :] if d == 0 else acum[0:1, :]
        dec_end = jnp.exp(last - acum)
        e_acum = jnp.exp(acum)
        e_end = jnp.exp(last)
        bgs = [xbc[:, GROUP_W + SSD_STATE * g:GROUP_W + SSD_STATE * (g + 1)] for g in range(SSD_GROUPS)]
        cgs = [xbc[:, GROUP_W + SSD_STATE * (SSD_GROUPS + g):GROUP_W + SSD_STATE * (SSD_GROUPS + g + 1)]
               for g in range(SSD_GROUPS)]
        cbt = [_dot_nt(cg, bg) for cg, bg in zip(cgs, bgs)]
        for h in range(N_HEADS):
            g = h // per_head
            ln = SM_DT + N_HEADS * d + h
            lmat = jnp.exp(jnp.where(keep, acum[:, ln:ln + 1] - acum_t[ln:ln + 1, :], NEG_INF))
            scores.append(cbt[g] * lmat)
            xdt.append(xbc[:, h * HEAD_DIM:(h + 1) * HEAD_DIM] * dt[:, ln:ln + 1])
            c_in.append(cgs[g] * e_acum[:, ln:ln + 1])
            b_out.append(bgs[g] * dec_end[:, ln:ln + 1])
            e_last.append(e_end[:, ln:ln + 1])
    states = [state_s[d, h] for d, h in chains]
    y_diag = [_dot(s, x) for s, x in zip(scores, xdt)]
    y_off = [_dot(c, st) for c, st in zip(c_in, states)]
    upd = [_dot_tn(b, x) for b, x in zip(b_out, xdt)]
    for n, (d, h) in enumerate(chains):
        state_s[d, h] = states[n] * e_last[n] + upd[n]
    ys = [a + b for a, b in zip(y_diag, y_off)]
    yf_ref[...] = jnp.concatenate(ys[:N_HEADS], axis=1)
    yb_ref[...] = jnp.concatenate(ys[N_HEADS:], axis=1)

    @pl.when(i == pl.num_programs(1) - 1)
    def _():
        hfin_ref[0] = state_s[...]


def ssd_params(a_log, dt_bias):
    return _small_vec(dt_bias, SM_DT), _small_vec(a_log, SM_DT)


def ssd_mixer(xbc, sm, h0, bsz, dtb, alog):
    t, c = xbc.shape
    s = t // bsz
    q = min(SSD_CHUNK, s)
    nc = s // q
    xf, xb = _chunk_specs(nc, q, c)
    sf, sb = _chunk_specs(nc, q, LANES)
    yf, yb = _chunk_specs(nc, q, GROUP_W)
    st = pl.BlockSpec((1,) + h0.shape[1:], lambda b, i: (b, 0, 0, 0, 0))
    y_shape = jax.ShapeDtypeStruct((t, GROUP_W), F32)
    return pl.pallas_call(
        _ssd_kernel,
        grid=(bsz, nc),
        in_specs=[xf, xb, sf, sb, st, _full(dtb.shape), _full(alog.shape)],
        out_specs=[yf, yb, st],
        out_shape=[y_shape, y_shape, jax.ShapeDtypeStruct(h0.shape, F32)],
        scratch_shapes=[pltpu.VMEM(h0.shape[1:], F32)],
        compiler_params=_cp("parallel", "arbitrary"),
        name="ssd_mixer",
    )(xbc, xbc, sm, sm, h0, dtb, alog)


def rope_tables(seq):
    t = jnp.arange(seq)
    row = (t // GRID_W).astype(F32)
    col = (t % GRID_W).astype(F32)
    inv = ROPE_BASE ** (-jnp.arange(0, ROPE_AXIS_DIM, 2, dtype=F32) / ROPE_AXIS_DIM)
    ar, ac = row[:, None] * inv, col[:, None] * inv
    cos = jnp.concatenate([jnp.cos(ar), jnp.cos(ar), jnp.cos(ac), jnp.cos(ac)], axis=1)
    sin = jnp.concatenate([-jnp.sin(ar), jnp.sin(ar), -jnp.sin(ac), jnp.sin(ac)], axis=1)
    return jnp.tile(cos, (1, N_HEADS)), jnp.tile(sin, (1, N_HEADS))


def _swap16(x):
    lane = lax.broadcasted_iota(jnp.int32, x.shape, 1)
    half = ROPE_AXIS_DIM // 2
    return jnp.where((lane & (ROPE_AXIS_DIM - 1)) < half,
                     pltpu.roll(x, x.shape[1] - half, 1), pltpu.roll(x, half, 1))


def _head_sums(sq):
    c = sq.shape[1]
    li = lax.broadcasted_iota(jnp.int32, (c, c), 0)
    lj = lax.broadcasted_iota(jnp.int32, (c, c), 1)
    sh = HEAD_DIM.bit_length() - 1
    ones = _mx(((li >> sh) == (lj >> sh)).astype(F32))
    hi = _mx(sq)
    lo = _mx(sq - hi.astype(F32))
    return jnp.dot(hi, ones, preferred_element_type=F32) + jnp.dot(lo, ones, preferred_element_type=F32)


def _l2norm_heads(x):
    return x * lax.rsqrt(_head_sums(x * x) + EPS)


def _head_columns(x, off):
    li = lax.broadcasted_iota(jnp.int32, (LANES, N_HEADS * HEAD_DIM), 0)
    lj = lax.broadcasted_iota(jnp.int32, (LANES, N_HEADS * HEAD_DIM), 1)
    pick = _mx((li == off + (lj >> (HEAD_DIM.bit_length() - 1))).astype(F32))
    hi = _mx(x)
    lo = _mx(x - hi.astype(F32))
    return jnp.dot(hi, pick, preferred_element_type=F32) + jnp.dot(lo, pick, preferred_element_type=F32)


def _dot_tri(mask, x):
    m = _mx(mask.astype(F32))
    x1 = _mx(x)
    r1 = x - x1.astype(F32)
    x2 = _mx(r1)
    x3 = _mx(r1 - x2.astype(F32))
    return (jnp.dot(m, x1, preferred_element_type=F32) + jnp.dot(m, x2, preferred_element_type=F32)
            + jnp.dot(m, x3, preferred_element_type=F32))


def _same_block(rowi, coli, n):
    sh = n.bit_length() - 1
    return (rowi >> sh) == (coli >> sh)


def _solve_unit_tri(a_list, rhs_list, rowi, coli, chunk):
    mm = lambda x, y: jnp.dot(x, y, preferred_element_type=F32)
    eye = (rowi == coli).astype(F32)
    in_base = _same_block(rowi, coli, GDN_BASE)
    base = [_mx(jnp.where(in_base, a, 0.0)) for a in a_list]
    ts = [jnp.where(in_base, eye - a, 0.0) for a in a_list]
    ps = [_mx(mm(b, b)) for b in base]
    ts = [t + mm(_mx(t), p) for t, p in zip(ts, ps)]
    n = 4
    while n < GDN_BASE:
        ps = [_mx(mm(p, p)) for p in ps]
        ts = [t + mm(_mx(t), p) for t, p in zip(ts, ps)]
        n *= 2
    n = GDN_BASE
    while 2 * n < chunk:
        inner = _same_block(rowi, coli, 2 * n) & jnp.logical_not(_same_block(rowi, coli, n))
        offs = [_mx(jnp.where(inner, a, 0.0)) for a in a_list]
        tb = [_mx(t) for t in ts]
        ms = [_mx(mm(t, off)) for t, off in zip(tb, offs)]
        ts = [t - mm(m, t_b) for t, m, t_b in zip(ts, ms, tb)]
        n *= 2
    outer = jnp.logical_not(_same_block(rowi, coli, n))
    offs = [_mx(jnp.where(outer, a, 0.0)) for a in a_list]
    tb = [_mx(t) for t in ts]
    ys = [mm(t, _mx(r)) for t, r in zip(tb, rhs_list)]
    zs = [_mx(mm(off, _mx(y))) for off, y in zip(offs, ys)]
    return [y - mm(t, z) for y, t, z in zip(ys, tb, zs)]


def _gdn_kernel(xf_ref, xb_ref, sf_ref, sb_ref, s0_ref, alog_ref, dtb_ref, of_ref, ob_ref, sfin_ref, state_s):
    i = pl.program_id(1)
    tq = xf_ref.shape[0]
    ck = min(GDN_CHUNK, tq)
    nck = tq // ck

    @pl.when(i == 0)
    def _():
        state_s[...] = s0_ref[0]

    sub = min(GDN_SUB, tq)
    nsub = tq // sub
    rowt, colt = _tri_masks(tq)
    in_chunk_t = _same_block(rowt, colt, ck)
    rowi, coli = _tri_masks(sub)
    in_chunk = _same_block(rowi, coli, ck)
    a_neg = jnp.where(_lane_mask(SM_DECAY), -jnp.exp(alog_ref[...]), 0.0)

    a_list, rhs_list, qkm, qg, kd, e_last = [], [], [], [], [], []
    for d, (x_ref, sm_ref) in enumerate(((xf_ref, sf_ref), (xb_ref, sb_ref))):
        qkv = x_ref[...]
        qn, kn, v = qkv[:, :GROUP_W], qkv[:, GROUP_W:2 * GROUP_W], qkv[:, 2 * GROUP_W:]
        sm = sm_ref[...]
        beta = _sigmoid(sm)
        keep_t = in_chunk_t & ((rowt >= colt) if d == 0 else (rowt <= colt))
        keep = in_chunk & ((rowi >= coli) if d == 0 else (rowi <= coli))
        strict = in_chunk & ((rowi > coli) if d == 0 else (rowi < coli))
        gc = _dot_tri(keep_t, _softplus(sm + dtb_ref[...]) * a_neg)
        gc_t = gc.T
        edge = ck - 1 if d == 0 else 0
        last = jnp.concatenate([jnp.broadcast_to(gc[c * ck + edge:c * ck + edge + 1, :], (ck, LANES))
                                for c in range(nck)], axis=0)
        e_last.append(jnp.exp(last))
        beta_w = _head_columns(beta, SM_BETA + N_HEADS * d)
        e_gc_w = _head_columns(jnp.exp(gc), SM_DECAY + N_HEADS * d)
        e_end_w = _head_columns(jnp.exp(last - gc), SM_DECAY + N_HEADS * d)
        kb_w = kn * beta_w
        vb_w = v * beta_w
        kbe_w = kb_w * e_gc_w
        qg_w = qn * e_gc_w
        kd_w = kn * e_end_w
        qn_m, kn_m, kb_wm = _mx(qn), _mx(kn), _mx(kb_w)
        for h in range(N_HEADS):
            sl = slice(h * HEAD_DIM, (h + 1) * HEAD_DIM)
            lg = SM_DECAY + N_HEADS * d + h
            rhs = jnp.concatenate([vb_w[:, sl], kbe_w[:, sl]], axis=1)
            qg.append(qg_w[:, sl])
            kd.append(kd_w[:, sl])
            qh_m, kh_m, kb_m = qn_m[:, sl], kn_m[:, sl], kb_wm[:, sl]
            for s in range(nsub):
                rs = slice(s * sub, (s + 1) * sub)
                decay = jnp.exp(jnp.where(keep, gc[rs, lg:lg + 1] - gc_t[lg:lg + 1, rs], NEG_INF))
                a_list.append(jnp.where(strict, _dot_nt(kb_m[rs], kh_m[rs]) * decay, 0.0))
                rhs_list.append(rhs[rs])
                qkm.append(_dot_nt(qh_m[rs], kh_m[rs]) * decay)
    sols = _solve_unit_tri(a_list, rhs_list, rowi, coli, ck)
    sols = [jnp.concatenate(sols[n * nsub:(n + 1) * nsub], axis=0) for n in range(2 * N_HEADS)]

    chains = [(d, h) for d in range(2) for h in range(N_HEADS)]
    states = [state_s[d, h] for d, h in chains]
    v_new = [[None] * nck for _ in chains]
    o_st = [[None] * nck for _ in chains]
    for step in range(nck):
        rows = [slice((step if d == 0 else nck - 1 - step) * ck, (step if d == 0 else nck - 1 - step) * ck + ck)
                for d, _ in chains]
        ms = [_dot(jnp.concatenate([sols[n][r, HEAD_DIM:], qg[n][r]], axis=0), states[n])
              for n, r in enumerate(rows)]
        for n, (d, _) in enumerate(chains):
            c = step if d == 0 else nck - 1 - step
            v_new[n][c] = sols[n][rows[n], :HEAD_DIM] - ms[n][:ck]
            o_st[n][c] = ms[n][ck:]
        ups = [_dot_tn(kd[n][r], v_new[n][step if chains[n][0] == 0 else nck - 1 - step])
               for n, r in enumerate(rows)]
        for n, (d, h) in enumerate(chains):
            lg = SM_DECAY + N_HEADS * d + h
            states[n] = states[n] * e_last[d][rows[n].start:rows[n].start + 1, lg:lg + 1] + ups[n]
    cps = sub // ck
    outs = [jnp.concatenate(o_st[n], axis=0)
            + jnp.concatenate([_dot(qkm[n * nsub + s], jnp.concatenate(v_new[n][s * cps:(s + 1) * cps], axis=0))
                               for s in range(nsub)], axis=0)
            for n in range(len(chains))]
    of_ref[...] = jnp.concatenate(outs[:N_HEADS], axis=1)
    ob_ref[...] = jnp.concatenate(outs[N_HEADS:], axis=1)
    for n, (d, h) in enumerate(chains):
        state_s[d, h] = states[n]

    @pl.when(i == pl.num_programs(1) - 1)
    def _():
        sfin_ref[0] = state_s[...]


def gdn_params(a_log, dt_bias):
    return _small_vec(a_log, SM_DECAY), _small_vec(dt_bias, SM_DECAY)


def gdn_mixer(qkv, sm, s0, bsz, alog, dtb):
    t, c = qkv.shape
    s = t // bsz
    q = min(GDN_TILE, s)
    nc = s // q
    xf, xb = _chunk_specs(nc, q, c)
    sf, sb = _chunk_specs(nc, q, LANES)
    of, ob = _chunk_specs(nc, q, GROUP_W)
    st = pl.BlockSpec((1,) + s0.shape[1:], lambda b, i: (b, 0, 0, 0, 0))
    o_shape = jax.ShapeDtypeStruct((t, GROUP_W), F32)
    return pl.pallas_call(
        _gdn_kernel,
        grid=(bsz, nc),
        in_specs=[xf, xb, sf, sb, st, _full(alog.shape), _full(dtb.shape)],
        out_specs=[of, ob, st],
        out_shape=[o_shape, o_shape, jax.ShapeDtypeStruct(s0.shape, F32)],
        scratch_shapes=[pltpu.VMEM(s0.shape[1:], F32)],
        compiler_params=_cp("parallel", "arbitrary"),
        name="gdn_mixer",
    )(qkv, qkv, sm, sm, s0, alog, dtb)


def _split_hi_lo(a):
    hi = _mx(a)
    return hi, _mx(a - hi.astype(F32))


def _outproj_kernel(x_ref, ahf_ref, ahb_ref, ag_ref, bo_ref, cyf_ref, cyb_ref, cxc_ref, cz_ref,
                    dof_ref, dob_ref, dz_ref, wout_ref, gpost_ref, ga1_ref, gpre_ref, sc2_ref, sh2_ref,
                    dskip_ref, cnorm_ref, dnorm_ref, rhi_ref, rlo_ref, xo_ref, hp_ref, lg_ref):
    m_a = (ahf_ref[...] + ahb_ref[...]) * _gelu_tanh(ag_ref[...])
    y_c = (cyf_ref[...] + cyb_ref[...] + cxc_ref[...] * dskip_ref[...]) * _silu(cz_ref[...])
    m_c = _rms(y_c, cnorm_ref[...])
    o_d = dof_ref[...] + dob_ref[...]
    m_d = o_d * lax.rsqrt(_head_sums(o_d * o_d) * (1.0 / HEAD_DIM) + EPS) * dnorm_ref[...] * _silu(dz_ref[...])
    mix = jnp.concatenate([_mx(m_a), _mx(bo_ref[...]), _mx(m_c), _mx(m_d)], axis=1)
    ml = jnp.dot(mix, wout_ref[...], preferred_element_type=F32)
    x_new = x_ref[...] + ga1_ref[0] * _rms(ml, gpost_ref[...])
    xo_ref[...] = x_new
    h2 = _rms(x_new, gpre_ref[...]) * (1.0 + sc2_ref[0]) + sh2_ref[0]
    hi, lo = _split_hi_lo(h2)
    hp_ref[...] = _pack_pairs(h2)
    rhi = rhi_ref[...]
    lg_ref[...] = (jnp.dot(hi, rhi, preferred_element_type=F32) + jnp.dot(lo, rhi, preferred_element_type=F32)
                   + jnp.dot(hi, rlo_ref[...], preferred_element_type=F32))


def out_projection(x, mixers, w_out, gpost, ga1, gpre, sc2, sh2, dskip, cnorm, dnorm, router_w, tiles_per_group):
    t, d = x.shape
    tm = min(TOKEN_TILE, t)
    vec = lambda i: (i // tiles_per_group, 0, 0)
    row = lambda w: pl.BlockSpec((tm, w), lambda i: (i, 0))
    ne = LANES
    rhi, rlo = _split_hi_lo(jnp.pad(router_w.astype(F32), ((0, 0), (0, ne - router_w.shape[1]))))
    return pl.pallas_call(
        _outproj_kernel,
        grid=(t // tm,),
        in_specs=[row(d)] + [row(GROUP_W)] * 11
                 + [_full(w_out.shape), _full((1, d)), pl.BlockSpec((1, 1, d), vec), _full((1, d)),
                    pl.BlockSpec((1, 1, d), vec), pl.BlockSpec((1, 1, d), vec),
                    _full((1, GROUP_W)), _full((1, GROUP_W)), _full((1, GROUP_W)), _full(rhi.shape), _full(rlo.shape)],
        out_specs=[row(d), row(d // 2), row(ne)],
        out_shape=[jax.ShapeDtypeStruct((t, d), F32), jax.ShapeDtypeStruct((t, d // 2), jnp.uint32),
                   jax.ShapeDtypeStruct((t, ne), F32)],
        compiler_params=_cp("parallel"),
        name="out_projection",
    )(x, *mixers, w_out, gpost, ga1, gpre, sc2, sh2, dskip, cnorm, dnorm, rhi, rlo)


def _rank_before(vals, idx, count, stride):
    rank = jnp.zeros(vals.shape, jnp.int32)
    for j in range(count):
        other = vals[j * stride:j * stride + 1, :]
        ahead = (other > vals) | ((other == vals) & (idx > j))
        rank = rank + ahead.astype(jnp.int32)
    return rank


def _xor_partner(x, row, s):
    n = x.shape[0]
    return jnp.where((row & s) == 0, pltpu.roll(x, n - s, 0), pltpu.roll(x, s, 0))


def _route(logits, router_b):
    ne = N_EXPERTS
    gsz = ne // N_EXPERT_GROUPS
    scores = _sigmoid(logits.T[:ne, :])
    tm = scores.shape[1]
    biased = scores + router_b
    row = lax.broadcasted_iota(jnp.int32, (ne, tm), 0)
    m1, m2 = biased, jnp.full((ne, tm), -jnp.inf, F32)
    s = 1
    while s < gsz:
        o1, o2 = _xor_partner(m1, row, s), _xor_partner(m2, row, s)
        m2 = jnp.maximum(jnp.minimum(m1, o1), jnp.maximum(m2, o2))
        m1 = jnp.maximum(m1, o1)
        s *= 2
    gidx = row >> (gsz.bit_length() - 1)
    group_ok = _rank_before(m1 + m2, gidx, N_EXPERT_GROUPS, gsz) < TOPK_GROUPS
    choice = jnp.where(group_ok, biased, -jnp.inf)
    rank = jnp.full((ne, tm), TOP_K, jnp.int32)
    rest = choice
    for k in range(TOP_K):
        top = jnp.max(rest, axis=0, keepdims=True)
        first = jnp.min(jnp.where(rest == top, row, ne), axis=0, keepdims=True)
        hit = row == first
        rank = jnp.where(hit, k, rank)
        rest = jnp.where(hit, -jnp.inf, rest)
    gate = jnp.where(rank < TOP_K, scores, 0.0)
    gate = gate / jnp.sum(gate, axis=0, keepdims=True) * ROUTED_SCALE
    return gate, rank, row


def _to_token_major(x):
    n, tm = x.shape
    return jnp.concatenate([x, jnp.zeros((LANES - n, tm), x.dtype)], axis=0).T


def _router_kernel(lg_ref, rb_ref, gate_ref):
    gate, _, _ = _route(lg_ref[...], rb_ref[...])
    gate_ref[...] = _to_token_major(gate)


def _router_dispatch_kernel(lg_ref, rb_ref, gk_ref, ek_ref, pk_ref, cnt_ref, carry_s):
    i = pl.program_id(0)

    @pl.when(i == 0)
    def _():
        carry_s[...] = jnp.zeros(carry_s.shape, F32)

    gate, rank, row = _route(lg_ref[...], rb_ref[...])
    tm = gate.shape[1]
    picked = (rank < TOP_K).astype(F32)
    before = lax.broadcasted_iota(jnp.int32, (tm, tm), 0) < lax.broadcasted_iota(jnp.int32, (tm, tm), 1)
    pos = _dot(picked, before.astype(F32)) + carry_s[:, 0:1]
    carry_s[...] = carry_s[...] + jnp.sum(picked, axis=1, keepdims=True)
    gk, ek, pk = [], [], []
    for k in range(TOP_K):
        sel = rank == k
        gk.append(jnp.sum(jnp.where(sel, gate, 0.0), axis=0, keepdims=True))
        ek.append(jnp.sum(jnp.where(sel, row, 0), axis=0, keepdims=True))
        pk.append(jnp.sum(jnp.where(sel, pos, 0.0), axis=0, keepdims=True))
    gk_ref[...] = _to_token_major(jnp.concatenate(gk, axis=0))
    ek_ref[...] = jnp.concatenate(ek, axis=0)
    pk_ref[...] = jnp.concatenate(pk, axis=0).astype(jnp.int32)

    @pl.when(i == pl.num_programs(0) - 1)
    def _():
        cnt_ref[...] = carry_s[...].astype(jnp.int32)


def router_dispatch(logits, router_b):
    t, w = logits.shape
    tm = min(TOKEN_TILE, t)
    return pl.pallas_call(
        _router_dispatch_kernel,
        grid=(t // tm,),
        in_specs=[pl.BlockSpec((tm, w), lambda i: (i, 0)), _full((N_EXPERTS, 1))],
        out_specs=[pl.BlockSpec((tm, w), lambda i: (i, 0)),
                   pl.BlockSpec((TOP_K, tm), lambda i: (0, i)),
                   pl.BlockSpec((TOP_K, tm), lambda i: (0, i)),
                   _full((N_EXPERTS, LANES))],
        out_shape=[jax.ShapeDtypeStruct((t, w), F32), jax.ShapeDtypeStruct((TOP_K, t), jnp.int32),
                   jax.ShapeDtypeStruct((TOP_K, t), jnp.int32), jax.ShapeDtypeStruct((N_EXPERTS, LANES), jnp.int32)],
        scratch_shapes=[pltpu.VMEM((N_EXPERTS, LANES), F32)],
        compiler_params=_cp("arbitrary"),
        name="router_dispatch",
    )(logits, router_b.reshape(N_EXPERTS, 1).astype(F32))


def router_gates(logits, router_b):
    t, w = logits.shape
    tm = min(TOKEN_TILE, t)
    return pl.pallas_call(
        _router_kernel,
        grid=(t // tm,),
        in_specs=[pl.BlockSpec((tm, w), lambda i: (i, 0)), _full((N_EXPERTS, 1))],
        out_specs=pl.BlockSpec((tm, w), lambda i: (i, 0)),
        out_shape=jax.ShapeDtypeStruct((t, w), F32),
        compiler_params=_cp("parallel"),
        name="router_gates",
    )(logits, router_b.reshape(N_EXPERTS, 1).astype(F32))


def _moe_kernel(h_ref, gate_ref, x_ref, wg_ref, wu_ref, wd_ref, sg_ref, su_ref, sd_ref, gpost_ref, ga2_ref,
                o_ref, acc_s):
    e = pl.program_id(1)
    h = _mx(_unpack_pairs(h_ref[...]))

    @pl.when(e == 0)
    def _():
        hs = _silu(jnp.dot(h, sg_ref[...], preferred_element_type=F32)) * jnp.dot(h, su_ref[...], preferred_element_type=F32)
        acc_s[...] = jnp.dot(_mx(hs), sd_ref[...], preferred_element_type=F32)

    gates = gate_ref[...]
    lane = lax.broadcasted_iota(jnp.int32, gates.shape, 1)
    hid = []
    for j in range(MOE_EB):
        gcol = jnp.sum(jnp.where(lane == e * MOE_EB + j, gates, 0.0), axis=1, keepdims=True)
        g = jnp.dot(h, _mx(wg_ref[j]), preferred_element_type=F32)
        u = jnp.dot(h, _mx(wu_ref[j]), preferred_element_type=F32)
        hid.append(_mx(_silu(g) * u * gcol))
    wd = _mx(wd_ref[...]).reshape(MOE_EB * D_EXPERT, -1)
    acc_s[...] += jnp.dot(jnp.concatenate(hid, axis=1), wd, preferred_element_type=F32)

    @pl.when(e == pl.num_programs(1) - 1)
    def _():
        o_ref[...] = x_ref[...] + ga2_ref[0] * _rms(acc_s[...], gpost_ref[...])


def moe_ffn(h, gates, x, layer, wg, wu, wd, sg, su, sd, gpost, ga2, tiles_per_group):
    t, d = x.shape
    tm = min(MOE_TILE, t)
    _, ne, _, f = wg.shape
    row = lambda w: pl.BlockSpec((tm, w), lambda i, e: (i, 0))
    return pl.pallas_call(
        _moe_kernel,
        grid=(t // tm, ne // MOE_EB),
        in_specs=[row(h.shape[1]), row(gates.shape[1]), row(d),
                  pl.BlockSpec((None, MOE_EB, d, f), lambda i, e: (layer, e, 0, 0)),
                  pl.BlockSpec((None, MOE_EB, d, f), lambda i, e: (layer, e, 0, 0)),
                  pl.BlockSpec((None, MOE_EB, f, d), lambda i, e: (layer, e, 0, 0)),
                  _full(sg.shape), _full(su.shape), _full(sd.shape), _full((1, d)),
                  pl.BlockSpec((1, 1, d), lambda i, e: (i // tiles_per_group, 0, 0))],
        out_specs=row(d),
        out_shape=jax.ShapeDtypeStruct((t, d), F32),
        scratch_shapes=[pltpu.VMEM((tm, d), F32)],
        compiler_params=_cp("parallel", "arbitrary"),
        name="moe_ffn",
    )(h, gates, x, wg, wu, wd, sg, su, sd, gpost, ga2)


def moe_plan(counts, n_tokens):
    n_blocks = (n_tokens * TOP_K + N_EXPERTS * (MOE_BLOCK - 1) + MOE_BLOCK - 1) // MOE_BLOCK
    cnt = counts[:, 0]
    padded = (cnt + MOE_BLOCK - 1) // MOE_BLOCK * MOE_BLOCK
    pad_end = jnp.cumsum(padded)
    off = pad_end - padded
    start = jnp.arange(n_blocks, dtype=jnp.int32) * MOE_BLOCK
    be = jnp.minimum(jnp.sum(pad_end[None, :] <= start[:, None], axis=1), N_EXPERTS - 1).astype(jnp.int32)
    mine = be[:, None] == jnp.arange(N_EXPERTS, dtype=jnp.int32)[None, :]
    end = jnp.sum(jnp.where(mine, (off + cnt)[None, :], 0), axis=1)
    nv = jnp.clip(end - start, 0, MOE_BLOCK).astype(jnp.int32)
    return off.astype(jnp.int32), be, nv


def _rows_kernel(off_ref, ek_ref, pk_ref, dest_ref):
    ek = ek_ref[...]
    dest = pk_ref[...]
    for e in range(N_EXPERTS):
        dest = dest + jnp.where(ek == e, off_ref[e], 0)
    dest_ref[...] = dest


def moe_rows(off, ek, pk):
    k, t = ek.shape
    tm = min(MOE_PLAN_TILE, t)
    spec = pl.BlockSpec((k, tm), lambda i, off: (0, i))
    return pl.pallas_call(
        _rows_kernel,
        grid_spec=pltpu.PrefetchScalarGridSpec(num_scalar_prefetch=1, grid=(t // tm,),
                                               in_specs=[spec, spec], out_specs=spec),
        out_shape=jax.ShapeDtypeStruct((k, t), jnp.int32),
        compiler_params=_cp("arbitrary"),
        name="moe_rows",
    )(off, ek, pk)


U32 = jnp.uint32
HIGH_HALF = 0xFFFF0000


def _pack_pairs(x):
    w = x.shape[1] // 2
    bits = lax.bitcast_convert_type(x.astype(jnp.bfloat16).astype(F32), U32)
    return (bits[:, w:] & jnp.uint32(HIGH_HALF)) | (bits[:, :w] >> 16)


def _unpack_pairs(p):
    lo = lax.bitcast_convert_type(p << 16, F32)
    hi = lax.bitcast_convert_type(p & jnp.uint32(HIGH_HALF), F32)
    return jnp.concatenate([lo, hi], axis=1)


def _sc_workers():
    info = plsc.get_sparse_core_info()
    return info.num_cores, info.num_cores * info.num_subcores


def sc_scatter_rows(src, idx, n_rows):
    k, t = idx.shape
    w = src.shape[1]
    n_cores, n_workers = _sc_workers()
    per_worker = t // n_workers
    mesh = plsc.VectorSubcoreMesh(core_axis_name="c", subcore_axis_name="s")

    @functools.partial(
        pl.kernel, mesh=mesh, out_type=jax.ShapeDtypeStruct((n_rows, w), src.dtype),
        scratch_types=[pltpu.VMEM((k, SC_WINDOW), jnp.int32), pltpu.VMEM((SC_WINDOW, w), src.dtype),
                       pltpu.SemaphoreType.DMA])
    def scatter(s_hbm, i_hbm, o_hbm, idx_v, rows_v, sem):
        base = (lax.axis_index("s") * n_cores + lax.axis_index("c")) * per_worker

        @pl.loop(0, per_worker // SC_WINDOW)
        def _(j):
            off = base + j * SC_WINDOW
            pltpu.sync_copy(i_hbm.at[:, pl.ds(off, SC_WINDOW)], idx_v)
            pltpu.sync_copy(s_hbm.at[pl.ds(off, SC_WINDOW)], rows_v)
            for kk in range(k):
                pltpu.async_copy(rows_v, o_hbm.at[idx_v.at[kk]], sem).wait()

    return scatter(src, idx)


def sc_gather_rows(table, idx):
    n = idx.shape[0]
    w = table.shape[1]
    n_cores, n_workers = _sc_workers()
    per_worker = n // n_workers
    mesh = plsc.VectorSubcoreMesh(core_axis_name="c", subcore_axis_name="s")

    @functools.partial(
        pl.kernel, mesh=mesh, out_type=jax.ShapeDtypeStruct((n, w), table.dtype),
        scratch_types=[pltpu.VMEM((SC_WINDOW,), jnp.int32), pltpu.VMEM((SC_WINDOW, w), table.dtype),
                       pltpu.SemaphoreType.DMA])
    def gather(t_hbm, i_hbm, o_hbm, idx_v, rows_v, sem):
        base = (lax.axis_index("s") * n_cores + lax.axis_index("c")) * per_worker

        @pl.loop(0, per_worker // SC_WINDOW)
        def _(j):
            off = base + j * SC_WINDOW
            pltpu.sync_copy(i_hbm.at[pl.ds(off, SC_WINDOW)], idx_v)
            pltpu.async_copy(t_hbm.at[idx_v], rows_v, sem).wait()
            pltpu.sync_copy(rows_v, o_hbm.at[pl.ds(off, SC_WINDOW)])

    return gather(table, idx)


def _expert_kernel(be_ref, nv_ref, xs_ref, wg_ref, wu_ref, wd_ref, ys_ref):
    nv = nv_ref[pl.program_id(0)]

    def ffn(x):
        x = _mx(x)
        hid = (_silu(jnp.dot(x, _mx(wg_ref[0]), preferred_element_type=F32))
               * jnp.dot(x, _mx(wu_ref[0]), preferred_element_type=F32))
        ys_ref[...] = _pack_pairs(jnp.dot(_mx(hid), _mx(wd_ref[0]), preferred_element_type=F32))

    @pl.when(nv == MOE_BLOCK)
    def _():
        ffn(_unpack_pairs(xs_ref[...]))

    @pl.when((nv > 0) & (nv < MOE_BLOCK))
    def _():
        x = _unpack_pairs(xs_ref[...])
        rows = lax.broadcasted_iota(jnp.int32, x.shape, 0)
        ffn(jnp.where(rows < nv, x, 0.0))

    @pl.when(nv == 0)
    def _():
        ys_ref[...] = jnp.zeros(ys_ref.shape, U32)


def moe_experts(xs, be, nv, layer, wg, wu, wd):
    n_rows, w = xs.shape
    _, _, d, f = wg.shape
    return pl.pallas_call(
        _expert_kernel,
        grid_spec=pltpu.PrefetchScalarGridSpec(
            num_scalar_prefetch=2,
            grid=(n_rows // MOE_BLOCK,),
            in_specs=[pl.BlockSpec((MOE_BLOCK, w), lambda b, be, nv: (b, 0)),
                      pl.BlockSpec((None, 1, d, f), lambda b, be, nv: (layer, be[b], 0, 0)),
                      pl.BlockSpec((None, 1, d, f), lambda b, be, nv: (layer, be[b], 0, 0)),
                      pl.BlockSpec((None, 1, f, d), lambda b, be, nv: (layer, be[b], 0, 0))],
            out_specs=pl.BlockSpec((MOE_BLOCK, w), lambda b, be, nv: (b, 0))),
        out_shape=jax.ShapeDtypeStruct((n_rows, w), U32),
        compiler_params=_cp("arbitrary"),
        name="moe_experts",
    )(be, nv, xs, wg, wu, wd)


def _combine_kernel(yg_ref, gk_ref, hp_ref, x_ref, sg_ref, su_ref, sd_ref, gpost_ref, ga2_ref, o_ref):
    h = _mx(_unpack_pairs(hp_ref[...]))
    hs = _silu(jnp.dot(h, sg_ref[...], preferred_element_type=F32)) * jnp.dot(h, su_ref[...], preferred_element_type=F32)
    f = jnp.dot(_mx(hs), sd_ref[...], preferred_element_type=F32)
    gk = gk_ref[...]
    for k in range(TOP_K):
        f = f + gk[:, k:k + 1] * _unpack_pairs(yg_ref[k])
    o_ref[...] = x_ref[...] + ga2_ref[0] * _rms(f, gpost_ref[...])


def moe_combine(yg, gk, hp, x, sg, su, sd, gpost, ga2, tiles_per_group):
    t, d = x.shape
    tm = min(MOE_ROW_TILE, t)
    w = hp.shape[1]
    row = lambda n: pl.BlockSpec((tm, n), lambda i: (i, 0))
    return pl.pallas_call(
        _combine_kernel,
        grid=(t // tm,),
        in_specs=[pl.BlockSpec((TOP_K, tm, w), lambda i: (0, i, 0)),
                  row(gk.shape[1]), row(w), row(d), _full(sg.shape), _full(su.shape), _full(sd.shape), _full((1, d)),
                  pl.BlockSpec((1, 1, d), lambda i: (i // tiles_per_group, 0, 0))],
        out_specs=row(d),
        out_shape=jax.ShapeDtypeStruct((t, d), F32),
        compiler_params=_cp("parallel"),
        name="moe_combine",
    )(yg, gk, hp, x, sg, su, sd, gpost, ga2)


def _reorder_w_in(w_in):
    c = np.cumsum((0,) + (GROUP_W, GROUP_W, GROUP_W, GROUP_W, GROUP_W, GROUP_W, 2 * SSD_STATE, 2 * SSD_STATE,
                          GROUP_W, 2 * N_HEADS, GROUP_W, GROUP_W, GROUP_W, GROUP_W, 2 * N_HEADS, 2 * N_HEADS))
    seg = lambda a, b: w_in[:, c[a]:c[b]]
    small = jnp.concatenate([seg(9, 10), seg(14, 15), seg(15, 16),
                             jnp.zeros((w_in.shape[0], LANES - 6 * N_HEADS), w_in.dtype)], axis=1)
    return jnp.concatenate([seg(0, 1), seg(5, 8), seg(10, 13), seg(1, 5), seg(8, 9), seg(13, 14), small], axis=1)


def kernel(x, c, ctx, c_ctx, w_mod, b_mod, g_pre_mix, g_post_mix, g_pre_ffn, g_post_ffn, w_in, w_out, lru_conv_w, lru_conv_b, lru_wa, lru_ba, lru_wx, lru_bx, lru_lambda, na_bias, ssd_conv_w, ssd_conv_b, ssd_a_log, ssd_dt_bias, ssd_d, ssd_norm, gdn_conv_w, gdn_a_log, gdn_dt_bias, gdn_norm, router_w, router_b, we_gate, we_up, we_down, ws_gate, ws_up, ws_down):
    bsz, seq, d = x.shape
    n_ctx = ctx.shape[1]
    depth = w_mod.shape[0]
    lat_tpg = seq // min(TOKEN_TILE, seq)
    ctx_tpg = max(bsz * n_ctx // TOKEN_TILE, 1)
    ctx_mpg = max(bsz * n_ctx // MOE_TILE, 1)

    cond = _pad_rows(jnp.concatenate([c, c_ctx[None, :]], axis=0))
    mod = modulation(cond, w_mod, b_mod).reshape(depth, SUBLANES, N_MOD, d)
    rope = rope_tables(seq)
    row = lambda v: v[None, :].astype(F32)

    def layer_params(l):
        m_lat = [mod[l, :bsz, k][:, None, :] for k in range(N_MOD)]
        m_ctx = [mod[l, bsz:bsz + 1, k][:, None, :] for k in range(N_MOD)]
        w_in_l = _reorder_w_in(w_in[l]).astype(MXU_DTYPE)
        conv = (_pad_rows(lru_conv_w[l]), row(lru_conv_b[l]), _pad_rows(ssd_conv_w[l]), row(ssd_conv_b[l]),
                _pad_rows(gdn_conv_w[l]))
        return dict(
            m_lat=m_lat, m_ctx=m_ctx, w_in=w_in_l, conv=conv,
            lru=lru_params(lru_wa[l], lru_ba[l], lru_wx[l], lru_bx[l], lru_lambda[l]),
            ssd=ssd_params(ssd_a_log[l], ssd_dt_bias[l]), gdn=gdn_params(gdn_a_log[l], gdn_dt_bias[l]),
            epi=(w_out[l].astype(MXU_DTYPE), row(g_post_mix[l])),
            epi_tail=(row(jnp.repeat(ssd_d[l], HEAD_DIM)), row(ssd_norm[l]), row(jnp.tile(gdn_norm[l], N_HEADS)),
                      router_w[l]),
            routed=(l, we_gate, we_up, we_down),
            shared=(ws_gate[l].astype(MXU_DTYPE), ws_up[l].astype(MXU_DTYPE), ws_down[l].astype(MXU_DTYPE),
                    row(g_post_ffn[l])))

    def context_mixers(l, p, xc):
        pc = in_projection(xc, row(g_pre_mix[l]), p['m_ctx'][1], p['m_ctx'][0], p['w_in'], p['conv'], None, n_ctx,
                           bsz * n_ctx // min(TOKEN_TILE, n_ctx))
        a_f, a_b, a_st = lru_mixer(pc[P_AX], jnp.zeros((bsz, SUBLANES, GROUP_W), F32), bsz, *p['lru'])
        kc = pc[P_BK].reshape(bsz, n_ctx, GROUP_W)
        vc = pc[P_BV].reshape(bsz, n_ctx, GROUP_W)
        b_o = ctx_attention(pc[P_BQ].reshape(bsz, n_ctx, GROUP_W), kc, vc).reshape(bsz * n_ctx, GROUP_W)
        c_f, c_b, c_st = ssd_mixer(pc[P_CX], pc[P_SM], jnp.zeros((bsz, 2, N_HEADS, SSD_STATE, HEAD_DIM), F32),
                                   bsz, *p['ssd'])
        d_f, d_b, d_st = gdn_mixer(pc[P_DX], pc[P_SM], jnp.zeros((bsz, 2, N_HEADS, HEAD_DIM, HEAD_DIM), F32),
                                   bsz, *p['gdn'])
        mix = (a_f, a_b, pc[P_AG], b_o, c_f, c_b, pc[P_CX], pc[P_CZ], d_f, d_b, pc[P_DZ])
        return dict(mix=mix, a_st=a_st, kc=kc, vc=vc, c_st=c_st, d_st=d_st)

    def context_ffn(l, p, cm, xc):
        m = p['m_ctx']
        xc, hp, lg = out_projection(xc, cm['mix'], *p['epi'], m[2], row(g_pre_ffn[l]), m[4], m[3], *p['epi_tail'], ctx_tpg)
        return moe_ffn(hp, router_gates(lg, router_b[l]), xc, *p['routed'], *p['shared'], m[5], ctx_mpg)

    xl = x.reshape(bsz * seq, d)
    xc = ctx.reshape(bsz * n_ctx, d)
    p = layer_params(0)
    cm = context_mixers(0, p, xc)
    for l in range(depth):
        last = l == depth - 1
        m = p['m_lat']
        pl_ = in_projection(xl, row(g_pre_mix[l]), m[1], m[0], p['w_in'], p['conv'], rope, seq, lat_tpg)
        a_f, a_b, _ = lru_mixer(pl_[P_AX], cm['a_st'], bsz, *p['lru'])
        b_o = na_mixer(pl_[P_BQ], pl_[P_BK], pl_[P_BV], cm['kc'], cm['vc'], na_bias_slabs(na_bias[l]), bsz)
        c_f, c_b, _ = ssd_mixer(pl_[P_CX], pl_[P_SM], cm['c_st'], bsz, *p['ssd'])
        d_f, d_b, _ = gdn_mixer(pl_[P_DX], pl_[P_SM], cm['d_st'], bsz, *p['gdn'])
        mix_l = (a_f, a_b, pl_[P_AG], b_o, c_f, c_b, pl_[P_CX], pl_[P_CZ], d_f, d_b, pl_[P_DZ])
        xl, hp, lg = out_projection(xl, mix_l, *p['epi'], m[2], row(g_pre_ffn[l]), m[4], m[3], *p['epi_tail'], lat_tpg)
        gk, ek, pk, cnt = router_dispatch(lg, router_b[l])
        off, be, nv = moe_plan(cnt, bsz * seq)
        dest = moe_rows(off, ek, pk)
        xs = sc_scatter_rows(hp, dest, be.shape[0] * MOE_BLOCK)
        if not last:
            xc = context_ffn(l, p, cm, xc)
        ys = moe_experts(xs, be, nv, *p['routed'])
        yg = sc_gather_rows(ys, dest.reshape(-1)).reshape(TOP_K, bsz * seq, d // 2)
        if not last:
            p_next = layer_params(l + 1)
            cm = context_mixers(l + 1, p_next, xc)
        xl = moe_combine(yg, gk, hp, xl, *p['shared'], m[5], seq // min(MOE_ROW_TILE, seq))
        if not last:
            p = p_next
    return xl.reshape(bsz, seq, d)
```

```python
import functools
import math

import jax
import jax.numpy as jnp
import numpy as np
from jax import lax
from jax.experimental import pallas as pl
from jax.experimental.pallas import tpu as pltpu
from jax.experimental.pallas import tpu_sc as plsc

F32 = jnp.float32
MXU_DTYPE = jnp.bfloat16
HI = lax.Precision.HIGHEST

D_MODEL = 1024
GRID_W = 64
GROUP_W = 256
HEAD_DIM = 64
N_HEADS = 4
EPS = 1e-6
NEG_INF = -1e30
N_MOD = 6
LRU_C = 8.0
NA_WIN_ROWS = 8
NA_WIN_COLS = 16
SSD_STATE = 128
SSD_GROUPS = 2
ROPE_BASE = 10000.0
ROPE_AXIS_DIM = HEAD_DIM // 2
N_EXPERTS = 64
N_EXPERT_GROUPS = 8
TOPK_GROUPS = 4
TOP_K = 8
D_EXPERT = 256
ROUTED_SCALE = 2.5

LANES = 128
SUBLANES = 8
VMEM_LIMIT = 56 * 1024 * 1024

TOKEN_TILE = 512
LRU_CHUNK = 512
SSD_CHUNK = 256
GDN_CHUNK = 64
GDN_TILE = 256
GDN_SUB = 128
GDN_BASE = 16
MOE_TILE = 1024
MOE_EB = 4
MOE_BLOCK = 1024
MOE_ROW_TILE = 512
LRU_UNROLL = 8
MOE_PLAN_TILE = 2048
SC_WINDOW = 128

P_WIDTHS = (256, 768, 768, 256, 256, 256, 256, 256, 256, 128)
(P_AX, P_CX, P_DX, P_AG, P_BQ, P_BK, P_BV, P_CZ, P_DZ, P_SM) = range(10)
P_CONV_GROUPS = 3
SM_DT, SM_BETA, SM_DECAY = 0, 8, 16


def _cp(*sem):
    return pltpu.CompilerParams(dimension_semantics=sem, vmem_limit_bytes=VMEM_LIMIT)


def _mx(x):
    return x.astype(MXU_DTYPE)


def _dot(a, b):
    return jnp.dot(_mx(a), _mx(b), preferred_element_type=F32)


def _dot_nt(a, b):
    return lax.dot_general(_mx(a), _mx(b), (((1,), (1,)), ((), ())), preferred_element_type=F32)


def _dot_tn(a, b):
    return lax.dot_general(_mx(a), _mx(b), (((0,), (0,)), ((), ())), preferred_element_type=F32)


def _dot_hi(a, b):
    return jnp.dot(a, b, preferred_element_type=F32, precision=HI)


def _sigmoid(x):
    return 1.0 / (1.0 + jnp.exp(-x))


def _silu(x):
    return x * _sigmoid(x)


def _softplus(x):
    return jnp.maximum(x, 0.0) + jnp.log1p(jnp.exp(-jnp.abs(x)))


def _gelu_tanh(x):
    return 0.5 * x * (1.0 + jnp.tanh(math.sqrt(2.0 / math.pi) * (x + 0.044715 * (x * x * x))))


def _rms(x, g):
    return x * lax.rsqrt(jnp.mean(x * x, axis=-1, keepdims=True) + EPS) * g


def _full(shape):
    n = len(shape)
    return pl.BlockSpec(shape, lambda *_: (0,) * n)


MOD_COLS = 1536


def _mod_kernel(c_ref, w_ref, b_ref, o_ref):
    o_ref[0] = _dot_hi(_silu(c_ref[...]), w_ref[0]) + b_ref[0]


def modulation(cond, w_mod, b_mod):
    depth, d, n = w_mod.shape
    return pl.pallas_call(
        _mod_kernel,
        grid=(depth, n // MOD_COLS),
        in_specs=[pl.BlockSpec((SUBLANES, d), lambda l, j: (0, 0)),
                  pl.BlockSpec((1, d, MOD_COLS), lambda l, j: (l, 0, j)),
                  pl.BlockSpec((1, 1, MOD_COLS), lambda l, j: (l, 0, j))],
        out_specs=pl.BlockSpec((1, SUBLANES, MOD_COLS), lambda l, j: (l, 0, j)),
        out_shape=jax.ShapeDtypeStruct((depth, SUBLANES, n), F32),
        compiler_params=_cp("parallel", "parallel"),
        name="modulation",
    )(cond, w_mod, b_mod.reshape(depth, 1, n))


def _inproj_kernel(*refs, tiles_per_seq, rope):
    (x_ref, xp_ref, xn_ref, g_ref, sc_ref, sh_ref, w_ref, lcw_ref, lcb_ref, scw_ref, scb_ref, gcw_ref) = refs[:12]
    cos_ref, sin_ref = (refs[12], refs[13]) if rope else (None, None)
    o_refs = refs[14:] if rope else refs[12:]
    i = pl.program_id(0)
    norm = lambda v: _rms(v, g_ref[...]) * (1.0 + sc_ref[0]) + sh_ref[0]
    p = _dot(norm(x_ref[...]), w_ref[...])
    n_conv = sum(P_WIDTHS[:P_CONV_GROUPS])
    ph = _dot(norm(jnp.concatenate([xp_ref[...], xn_ref[...]], axis=0)), w_ref[:, :n_conv])
    pos = i % tiles_per_seq
    prev = jnp.where(pos == 0, 0.0, ph[:SUBLANES])
    nxt = jnp.where(pos == tiles_per_seq - 1, 0.0, ph[SUBLANES:])
    row = lax.broadcasted_iota(jnp.int32, (SUBLANES, n_conv), 0)
    halo = jnp.where(row < 2, pltpu.roll(prev, 2, 0), jnp.where(row == 2, pltpu.roll(nxt, 2, 0), 0.0))
    c0, c1, c2 = GROUP_W, GROUP_W + 3 * GROUP_W, n_conv
    lru_u = _dwconv(p[:, :c0], halo[:, :c0], lcw_ref[...], lcb_ref[...])
    ssd_x = _silu(_dwconv(p[:, c0:c1], halo[:, c0:c1], scw_ref[...], scb_ref[...]))
    qkv = _silu(_dwconv(p[:, c1:c2], halo[:, c1:c2], gcw_ref[...]))
    qn = _l2norm_heads(qkv[:, :GROUP_W])
    kn = _l2norm_heads(qkv[:, GROUP_W:2 * GROUP_W])
    if rope:
        cos, sin = cos_ref[...], sin_ref[...]
        qn = qn * cos + _swap16(qn) * sin
        kn = kn * cos + _swap16(kn) * sin
    outs = [lru_u, ssd_x, jnp.concatenate([qn * (HEAD_DIM ** -0.5), kn, qkv[:, 2 * GROUP_W:]], axis=1)]
    off = n_conv
    for o_ref, w in zip(o_refs, P_WIDTHS):
        if outs:
            o_ref[...] = outs.pop(0)
        else:
            o_ref[...] = p[:, off:off + w].astype(o_ref.dtype)
            off += w


def in_projection(x, g, sc, sh, w, conv, rope, seq_len, tiles_per_group):
    t, d = x.shape
    tm = min(TOKEN_TILE, seq_len)
    tps = seq_len // tm
    hb = tm // SUBLANES
    vec = lambda i: (i // tiles_per_group, 0, 0)
    ins = [x, x, x, g, sc, sh, w, *conv]
    specs = [pl.BlockSpec((tm, d), lambda i: (i, 0)),
             pl.BlockSpec((SUBLANES, d), lambda i: (jnp.maximum(i * hb - 1, 0), 0)),
             pl.BlockSpec((SUBLANES, d), lambda i: (jnp.minimum((i + 1) * hb, t // SUBLANES - 1), 0)),
             _full((1, d)), pl.BlockSpec((1, 1, d), vec), pl.BlockSpec((1, 1, d), vec), _full(w.shape)]
    specs += [_full(a.shape) for a in conv]
    if rope is not None:
        ins += list(rope)
        specs += [pl.BlockSpec((tm, GROUP_W), lambda i: (i % tps, 0))] * 2
    return pl.pallas_call(
        functools.partial(_inproj_kernel, tiles_per_seq=tps, rope=rope is not None),
        grid=(t // tm,),
        in_specs=specs,
        out_specs=[pl.BlockSpec((tm, wd), lambda i: (i, 0)) for wd in P_WIDTHS],
        out_shape=[jax.ShapeDtypeStruct((t, wd), MXU_DTYPE if k in (P_BQ, P_BK, P_BV) else F32)
                   for k, wd in enumerate(P_WIDTHS)],
        compiler_params=_cp("parallel"),
        name="in_projection",
    )(*ins)


def _dwconv(x, halo, w, b=None):
    q = x.shape[0]
    row = lax.broadcasted_iota(jnp.int32, (SUBLANES, x.shape[1]), 0)

    def shifted(s, keep_rolled, edge):
        r = pltpu.roll(x, s % q, 0)
        if s > 0:
            return jnp.concatenate([jnp.where(keep_rolled, r[:SUBLANES], edge), r[SUBLANES:]], axis=0)
        return jnp.concatenate([r[:q - SUBLANES], jnp.where(keep_rolled, r[q - SUBLANES:], edge)], axis=0)

    xm2 = shifted(2, row >= 2, halo)
    xm1 = shifted(1, row >= 1, pltpu.roll(halo, SUBLANES - 1, 0))
    xp1 = shifted(-1, row < SUBLANES - 1, pltpu.roll(halo, SUBLANES - 3, 0))
    y = w[0:1] * xm2 + w[1:2] * xm1 + w[2:3] * x + w[3:4] * xp1
    return y if b is None else y + b


def _pad_rows(a, rows=SUBLANES):
    return jnp.concatenate([a, jnp.zeros((rows - a.shape[0],) + a.shape[1:], a.dtype)], axis=0)


def _chunk_specs(nc, q, c):
    fwd = pl.BlockSpec((q, c), lambda b, i: (b * nc + i, 0))
    bwd = pl.BlockSpec((q, c), lambda b, i: (b * nc + nc - 1 - i, 0))
    return fwd, bwd


def _lru_kernel(xf_ref, xb_ref, h0_ref, wg_ref, bg_ref, lam_ref,
                yf_ref, yb_ref, hfin_ref, af_s, bf_s, ab_s, bb_s, carry_s):
    i = pl.program_id(1)
    q = xf_ref.shape[0]

    @pl.when(i == 0)
    def _():
        carry_s[...] = h0_ref[0]

    def coeffs(x_ref, d, a_s, b_s):
        u = x_ref[...]
        g = _dot(u, wg_ref[:, 2 * GROUP_W * d:2 * GROUP_W * (d + 1)]) + bg_ref[:, 2 * GROUP_W * d:2 * GROUP_W * (d + 1)]
        r = _sigmoid(g[:, :GROUP_W])
        gate_in = _sigmoid(g[:, GROUP_W:])
        log_a = -LRU_C * r * _softplus(-lam_ref[d:d + 1, :])
        a_s[...] = jnp.exp(log_a)
        b_s[...] = jnp.sqrt(1.0 - jnp.exp(2.0 * log_a)) * (gate_in * u)

    coeffs(xf_ref, 0, af_s, bf_s)
    coeffs(xb_ref, 1, ab_s, bb_s)

    ng = q // SUBLANES
    row = lax.broadcasted_iota(jnp.int32, (SUBLANES, GROUP_W), 0)

    def body(g, hs):
        h_f, h_b = hs
        i0 = pl.multiple_of(g * SUBLANES, SUBLANES)
        a = af_s[pl.ds(i0, SUBLANES), :]
        b = bf_s[pl.ds(i0, SUBLANES), :]
        for s in (1, 2, 4):
            m = row >= s
            b = jnp.where(m, a * pltpu.roll(b, s, 0) + b, b)
            a = jnp.where(m, a * pltpu.roll(a, s, 0), a)
        h = b + a * h_f
        yf_ref[pl.ds(i0, SUBLANES), :] = h
        h_f = h[SUBLANES - 1:SUBLANES, :]
        j0 = pl.multiple_of((ng - 1 - g) * SUBLANES, SUBLANES)
        a = ab_s[pl.ds(j0, SUBLANES), :]
        b = bb_s[pl.ds(j0, SUBLANES), :]
        for s in (1, 2, 4):
            m = row < SUBLANES - s
            b = jnp.where(m, a * pltpu.roll(b, SUBLANES - s, 0) + b, b)
            a = jnp.where(m, a * pltpu.roll(a, SUBLANES - s, 0), a)
        h = b + a * h_b
        yb_ref[pl.ds(j0, SUBLANES), :] = h
        return h_f, h[0:1, :]

    h_f, h_b = lax.fori_loop(0, ng, body, (carry_s[0:1, :], carry_s[1:2, :]), unroll=LRU_UNROLL)
    carry_s[0:1, :] = h_f
    carry_s[1:2, :] = h_b

    @pl.when(i == pl.num_programs(1) - 1)
    def _():
        hfin_ref[0] = carry_s[...]


def _block_diag(w):
    h, a, b = w.shape
    return jnp.einsum('hij,hg->higj', w, jnp.eye(h, dtype=w.dtype)).reshape(h * a, h * b)


def lru_params(wa, ba, wx, bx, lam):
    wg = jnp.concatenate([_block_diag(wa[0]), _block_diag(wx[0]), _block_diag(wa[1]), _block_diag(wx[1])], axis=1)
    bg = jnp.concatenate([ba[0], bx[0], ba[1], bx[1]])[None, :]
    return wg.astype(MXU_DTYPE), bg, _pad_rows(lam)


def lru_mixer(x, h0, bsz, wg, bg, lam):
    t, c = x.shape
    s = t // bsz
    q = min(LRU_CHUNK, s)
    nc = s // q
    xf, xb = _chunk_specs(nc, q, c)
    st = pl.BlockSpec((1, SUBLANES, c), lambda b, i: (b, 0, 0))
    return pl.pallas_call(
        _lru_kernel,
        grid=(bsz, nc),
        in_specs=[xf, xb, st, _full(wg.shape), _full(bg.shape), _full(lam.shape)],
        out_specs=[xf, xb, st],
        out_shape=[jax.ShapeDtypeStruct((t, c), F32), jax.ShapeDtypeStruct((t, c), F32),
                   jax.ShapeDtypeStruct((bsz, SUBLANES, c), F32)],
        scratch_shapes=[pltpu.VMEM((q, c), F32)] * 4 + [pltpu.VMEM((SUBLANES, c), F32)],
        compiler_params=_cp("parallel", "arbitrary"),
        name="lru_mixer",
    )(x, x, h0, wg, bg, lam)


NA_KEYS = NA_WIN_ROWS * GRID_W
NA_ROW_BLOCK = 8


def na_bias_slabs(table):
    qc = np.arange(GRID_W)[:, None]
    kc = np.arange(GRID_W)[None, :]
    win0 = np.clip(qc - NA_WIN_COLS // 2, 0, GRID_W - NA_WIN_COLS)
    ok = (kc >= win0) & (kc < win0 + NA_WIN_COLS)
    dc = np.clip(kc - qc + NA_WIN_COLS - 1, 0, 2 * NA_WIN_COLS - 2)
    dr = np.arange(NA_WIN_ROWS)[:, None] + np.arange(NA_WIN_ROWS)[None, :]
    b = table.astype(F32)[:, dr][:, :, :, dc]
    b = jnp.where(ok[None, None, None], b, NEG_INF)
    h = table.shape[0]
    return b.transpose(0, 1, 3, 2, 4).reshape(h, NA_WIN_ROWS, GRID_W, NA_KEYS)


def _na_span_start(j, rows):
    return jnp.clip(j * NA_ROW_BLOCK - NA_WIN_ROWS // 2, 0, rows - (NA_ROW_BLOCK + NA_WIN_ROWS - 1))


def _na_kernel(q_ref, kw_ref, vw_ref, kc_ref, vc_ref, slab_ref, o_ref, *, rows):
    j = pl.program_id(1)
    ustart = _na_span_start(j, rows)
    q = q_ref[...] * (HEAD_DIM ** -0.5)
    kc, vc = kc_ref[0], vc_ref[0]
    heads = [slice(h * HEAD_DIM, (h + 1) * HEAD_DIM) for h in range(N_HEADS)]
    qrows = [slice(i * GRID_W, (i + 1) * GRID_W) for i in range(NA_ROW_BLOCK)]
    kws, vws, offs = [], [], []
    for i in range(NA_ROW_BLOCK):
        r = j * NA_ROW_BLOCK + i
        r0 = jnp.clip(r - NA_WIN_ROWS // 2, 0, rows - NA_WIN_ROWS)
        start = pl.multiple_of((r0 - ustart) * GRID_W, GRID_W)
        kws.append(kw_ref[pl.ds(start, NA_KEYS), :])
        vws.append(vw_ref[pl.ds(start, NA_KEYS), :])
        offs.append(r0 - r + NA_WIN_ROWS - 1)
    s_ctx = [_dot_nt(q[:, sl], kc[:, sl]) for sl in heads]
    s_loc = [[_dot_nt(q[qr, sl], kws[i][:, sl]) + slab_ref[h, offs[i]] for h, sl in enumerate(heads)]
             for i, qr in enumerate(qrows)]
    m = [[jnp.maximum(jnp.max(s_loc[i][h], axis=-1, keepdims=True), jnp.max(s_ctx[h][qr], axis=-1, keepdims=True))
          for h in range(N_HEADS)] for i, qr in enumerate(qrows)]
    p_loc = [[jnp.exp(s_loc[i][h] - m[i][h]) for h in range(N_HEADS)] for i in range(NA_ROW_BLOCK)]
    p_ctx = [jnp.exp(s_ctx[h] - jnp.concatenate([m[i][h] for i in range(NA_ROW_BLOCK)], axis=0))
             for h in range(N_HEADS)]
    o_ctx = [_dot(p_ctx[h], vc[:, sl]) for h, sl in enumerate(heads)]
    rows_out = []
    for i, qr in enumerate(qrows):
        outs = []
        for h, sl in enumerate(heads):
            den = jnp.sum(p_loc[i][h], axis=-1, keepdims=True) + jnp.sum(p_ctx[h][qr], axis=-1, keepdims=True)
            outs.append((_dot(p_loc[i][h], vws[i][:, sl]) + o_ctx[h][qr]) / den)
        rows_out.append(jnp.concatenate(outs, axis=1))
    o_ref[...] = jnp.concatenate(rows_out, axis=0)


def na_mixer(q, k, v, kc, vc, slabs, bsz):
    t, c = q.shape
    s = t // bsz
    rows = s // GRID_W
    n_ctx = kc.shape[1]
    span = (NA_ROW_BLOCK + NA_WIN_ROWS - 1) * GRID_W

    def win(b, j):
        return ((b * rows + _na_span_start(j, rows)) * GRID_W, 0)

    wspec = pl.BlockSpec((pl.Element(span), pl.Element(c)), win)
    cspec = pl.BlockSpec((1, n_ctx, c), lambda b, j: (b, 0, 0))
    qspec = pl.BlockSpec((NA_ROW_BLOCK * GRID_W, c), lambda b, j: (b * (rows // NA_ROW_BLOCK) + j, 0))
    return pl.pallas_call(
        functools.partial(_na_kernel, rows=rows),
        grid=(bsz, rows // NA_ROW_BLOCK),
        in_specs=[qspec, wspec, wspec, cspec, cspec, _full(slabs.shape)],
        out_specs=qspec,
        out_shape=jax.ShapeDtypeStruct((t, c), F32),
        compiler_params=_cp("parallel", "arbitrary"),
        name="na_mixer",
    )(q, k, v, kc, vc, slabs)


def _ctx_attn_kernel(q_ref, k_ref, v_ref, o_ref):
    q = q_ref[0] * (HEAD_DIM ** -0.5)
    k, v = k_ref[0], v_ref[0]
    outs = []
    for h in range(N_HEADS):
        sl = slice(h * HEAD_DIM, (h + 1) * HEAD_DIM)
        s = _dot_nt(q[:, sl], k[:, sl])
        p = jnp.exp(s - jnp.max(s, axis=-1, keepdims=True))
        outs.append(_dot(p, v[:, sl]) / jnp.sum(p, axis=-1, keepdims=True))
    o_ref[0] = jnp.concatenate(outs, axis=1)


def ctx_attention(q, k, v):
    spec = pl.BlockSpec((1,) + q.shape[1:], lambda b: (b, 0, 0))
    return pl.pallas_call(
        _ctx_attn_kernel,
        grid=(q.shape[0],),
        in_specs=[spec, spec, spec],
        out_specs=spec,
        out_shape=jax.ShapeDtypeStruct(q.shape, F32),
        compiler_params=_cp("parallel"),
        name="ctx_attention",
    )(q, k, v)


def _small_vec(vals, off):
    v = jnp.zeros((LANES,), F32).at[off:off + 2 * N_HEADS].set(vals.reshape(-1).astype(F32))
    return v[None, :]


def _lane_mask(off):
    lane = lax.broadcasted_iota(jnp.int32, (1, LANES), 1)
    return (lane >= off) & (lane < off + 2 * N_HEADS)


def _tri_masks(q):
    rowi = lax.broadcasted_iota(jnp.int32, (q, q), 0)
    coli = lax.broadcasted_iota(jnp.int32, (q, q), 1)
    return rowi, coli


def _ssd_kernel(xf_ref, xb_ref, sf_ref, sb_ref, h0_ref, dtb_ref, alog_ref,
                yf_ref, yb_ref, hfin_ref, state_s):
    i = pl.program_id(1)
    q = xf_ref.shape[0]

    @pl.when(i == 0)
    def _():
        state_s[...] = h0_ref[0]

    rowi, coli = _tri_masks(q)
    a_neg = jnp.where(_lane_mask(SM_DT), -jnp.exp(alog_ref[...]), 0.0)

    chains = [(d, h) for d in range(2) for h in range(N_HEADS)]
    per_head = N_HEADS // SSD_GROUPS
    scores, xdt, c_in, b_out, e_last = [], [], [], [], []
    for d, (x_ref, sm_ref) in enumerate(((xf_ref, sf_ref), (xb_ref, sb_ref))):
        xbc = x_ref[...]
        dt = _softplus(sm_ref[...] + dtb_ref[...])
        keep = (rowi >= coli) if d == 0 else (rowi <= coli)
        acum = _dot_tri(keep, dt * a_neg)
        acum_t = acum.T
        last = acum[q - 1:q, :] if d == 0 else acum[0:1, :]
        dec_end = jnp.exp(last - acum)
        e_acum = jnp.exp(acum)
        e_end = jnp.exp(last)
        bgs = [xbc[:, GROUP_W + SSD_STATE * g:GROUP_W + SSD_STATE * (g + 1)] for g in range(SSD_GROUPS)]
        cgs = [xbc[:, GROUP_W + SSD_STATE * (SSD_GROUPS + g):GROUP_W + SSD_STATE * (SSD_GROUPS + g + 1)]
               for g in range(SSD_GROUPS)]
        cbt = [_dot_nt(cg, bg) for cg, bg in zip(cgs, bgs)]
        for h in range(N_HEADS):
            g = h // per_head
            ln = SM_DT + N_HEADS * d + h
            lmat = jnp.exp(jnp.where(keep, acum[:, ln:ln + 1] - acum_t[ln:ln + 1, :], NEG_INF))
            scores.append(cbt[g] * lmat)
            xdt.append(xbc[:, h * HEAD_DIM:(h + 1) * HEAD_DIM] * dt[:, ln:ln + 1])
            c_in.append(cgs[g] * e_acum[:, ln:ln + 1])
            b_out.append(bgs[g] * dec_end[:, ln:ln + 1])
            e_last.append(e_end[:, ln:ln + 1])
    states = [state_s[d, h] for d, h in chains]
    y_diag = [_dot(s, x) for s, x in zip(scores, xdt)]
    y_off = [_dot(c, st) for c, st in zip(c_in, states)]
    upd = [_dot_tn(b, x) for b, x in zip(b_out, xdt)]
    for n, (d, h) in enumerate(chains):
        state_s[d, h] = states[n] * e_last[n] + upd[n]
    ys = [a + b for a, b in zip(y_diag, y_off)]
    yf_ref[...] = jnp.concatenate(ys[:N_HEADS], axis=1)
    yb_ref[...] = jnp.concatenate(ys[N_HEADS:], axis=1)

    @pl.when(i == pl.num_programs(1) - 1)
    def _():
        hfin_ref[0] = state_s[...]


def ssd_params(a_log, dt_bias):
    return _small_vec(dt_bias, SM_DT), _small_vec(a_log, SM_DT)


def ssd_mixer(xbc, sm, h0, bsz, dtb, alog):
    t, c = xbc.shape
    s = t // bsz
    q = min(SSD_CHUNK, s)
    nc = s // q
    xf, xb = _chunk_specs(nc, q, c)
    sf, sb = _chunk_specs(nc, q, LANES)
    yf, yb = _chunk_specs(nc, q, GROUP_W)
    st = pl.BlockSpec((1,) + h0.shape[1:], lambda b, i: (b, 0, 0, 0, 0))
    y_shape = jax.ShapeDtypeStruct((t, GROUP_W), F32)
    return pl.pallas_call(
        _ssd_kernel,
        grid=(bsz, nc),
        in_specs=[xf, xb, sf, sb, st, _full(dtb.shape), _full(alog.shape)],
        out_specs=[yf, yb, st],
        out_shape=[y_shape, y_shape, jax.ShapeDtypeStruct(h0.shape, F32)],
        scratch_shapes=[pltpu.VMEM(h0.shape[1:], F32)],
        compiler_params=_cp("parallel", "arbitrary"),
        name="ssd_mixer",
    )(xbc, xbc, sm, sm, h0, dtb, alog)


def rope_tables(seq):
    t = jnp.arange(seq)
    row = (t // GRID_W).astype(F32)
    col = (t % GRID_W).astype(F32)
    inv = ROPE_BASE ** (-jnp.arange(0, ROPE_AXIS_DIM, 2, dtype=F32) / ROPE_AXIS_DIM)
    ar, ac = row[:, None] * inv, col[:, None] * inv
    cos = jnp.concatenate([jnp.cos(ar), jnp.cos(ar), jnp.cos(ac), jnp.cos(ac)], axis=1)
    sin = jnp.concatenate([-jnp.sin(ar), jnp.sin(ar), -jnp.sin(ac), jnp.sin(ac)], axis=1)
    return jnp.tile(cos, (1, N_HEADS)), jnp.tile(sin, (1, N_HEADS))


def _swap16(x):
    lane = lax.broadcasted_iota(jnp.int32, x.shape, 1)
    half = ROPE_AXIS_DIM // 2
    return jnp.where((lane & (ROPE_AXIS_DIM - 1)) < half,
                     pltpu.roll(x, x.shape[1] - half, 1), pltpu.roll(x, half, 1))


def _head_sums(sq):
    c = sq.shape[1]
    li = lax.broadcasted_iota(jnp.int32, (c, c), 0)
    lj = lax.broadcasted_iota(jnp.int32, (c, c), 1)
    sh = HEAD_DIM.bit_length() - 1
    ones = _mx(((li >> sh) == (lj >> sh)).astype(F32))
    hi = _mx(sq)
    lo = _mx(sq - hi.astype(F32))
    return jnp.dot(hi, ones, preferred_element_type=F32) + jnp.dot(lo, ones, preferred_element_type=F32)


def _l2norm_heads(x):
    return x * lax.rsqrt(_head_sums(x * x) + EPS)


def _head_columns(x, off):
    li = lax.broadcasted_iota(jnp.int32, (LANES, N_HEADS * HEAD_DIM), 0)
    lj = lax.broadcasted_iota(jnp.int32, (LANES, N_HEADS * HEAD_DIM), 1)
    pick = _mx((li == off + (lj >> (HEAD_DIM.bit_length() - 1))).astype(F32))
    hi = _mx(x)
    lo = _mx(x - hi.astype(F32))
    return jnp.dot(hi, pick, preferred_element_type=F32) + jnp.dot(lo, pick, preferred_element_type=F32)


def _dot_tri(mask, x):
    m = _mx(mask.astype(F32))
    x1 = _mx(x)
    r1 = x - x1.astype(F32)
    x2 = _mx(r1)
    x3 = _mx(r1 - x2.astype(F32))
    return (jnp.dot(m, x1, preferred_element_type=F32) + jnp.dot(m, x2, preferred_element_type=F32)
            + jnp.dot(m, x3, preferred_element_type=F32))


def _same_block(rowi, coli, n):
    sh = n.bit_length() - 1
    return (rowi >> sh) == (coli >> sh)


def _solve_unit_tri(a_list, rhs_list, rowi, coli, chunk):
    mm = lambda x, y: jnp.dot(x, y, preferred_element_type=F32)
    eye = (rowi == coli).astype(F32)
    in_base = _same_block(rowi, coli, GDN_BASE)
    base = [_mx(jnp.where(in_base, a, 0.0)) for a in a_list]
    ts = [jnp.where(in_base, eye - a, 0.0) for a in a_list]
    ps = [_mx(mm(b, b)) for b in base]
    ts = [t + mm(_mx(t), p) for t, p in zip(ts, ps)]
    n = 4
    while n < GDN_BASE:
        ps = [_mx(mm(p, p)) for p in ps]
        ts = [t + mm(_mx(t), p) for t, p in zip(ts, ps)]
        n *= 2
    n = GDN_BASE
    while 2 * n < chunk:
        inner = _same_block(rowi, coli, 2 * n) & jnp.logical_not(_same_block(rowi, coli, n))
        offs = [_mx(jnp.where(inner, a, 0.0)) for a in a_list]
        tb = [_mx(t) for t in ts]
        ms = [_mx(mm(t, off)) for t, off in zip(tb, offs)]
        ts = [t - mm(m, t_b) for t, m, t_b in zip(ts, ms, tb)]
        n *= 2
    outer = jnp.logical_not(_same_block(rowi, coli, n))
    offs = [_mx(jnp.where(outer, a, 0.0)) for a in a_list]
    tb = [_mx(t) for t in ts]
    ys = [mm(t, _mx(r)) for t, r in zip(tb, rhs_list)]
    zs = [_mx(mm(off, _mx(y))) for off, y in zip(offs, ys)]
    return [y - mm(t, z) for y, t, z in zip(ys, tb, zs)]


def _gdn_kernel(xf_ref, xb_ref, sf_ref, sb_ref, s0_ref, alog_ref, dtb_ref, of_ref, ob_ref, sfin_ref, state_s):
    i = pl.program_id(1)
    tq = xf_ref.shape[0]
    ck = min(GDN_CHUNK, tq)
    nck = tq // ck

    @pl.when(i == 0)
    def _():
        state_s[...] = s0_ref[0]

    sub = min(GDN_SUB, tq)
    nsub = tq // sub
    rowt, colt = _tri_masks(tq)
    in_chunk_t = _same_block(rowt, colt, ck)
    rowi, coli = _tri_masks(sub)
    in_chunk = _same_block(rowi, coli, ck)
    a_neg = jnp.where(_lane_mask(SM_DECAY), -jnp.exp(alog_ref[...]), 0.0)

    a_list, rhs_list, qkm, qg, kd, e_last = [], [], [], [], [], []
    for d, (x_ref, sm_ref) in enumerate(((xf_ref, sf_ref), (xb_ref, sb_ref))):
        qkv = x_ref[...]
        qn, kn, v = qkv[:, :GROUP_W], qkv[:, GROUP_W:2 * GROUP_W], qkv[:, 2 * GROUP_W:]
        sm = sm_ref[...]
        beta = _sigmoid(sm)
        keep_t = in_chunk_t & ((rowt >= colt) if d == 0 else (rowt <= colt))
        keep = in_chunk & ((rowi >= coli) if d == 0 else (rowi <= coli))
        strict = in_chunk & ((rowi > coli) if d == 0 else (rowi < coli))
        gc = _dot_tri(keep_t, _softplus(sm + dtb_ref[...]) * a_neg)
        gc_t = gc.T
        edge = ck - 1 if d == 0 else 0
        last = jnp.concatenate([jnp.broadcast_to(gc[c * ck + edge:c * ck + edge + 1, :], (ck, LANES))
                                for c in range(nck)], axis=0)
        e_last.append(jnp.exp(last))
        beta_w = _head_columns(beta, SM_BETA + N_HEADS * d)
        e_gc_w = _head_columns(jnp.exp(gc), SM_DECAY + N_HEADS * d)
        e_end_w = _head_columns(jnp.exp(last - gc), SM_DECAY + N_HEADS * d)
        kb_w = kn * beta_w
        vb_w = v * beta_w
        kbe_w = kb_w * e_gc_w
        qg_w = qn * e_gc_w
        kd_w = kn * e_end_w
        qn_m, kn_m, kb_wm = _mx(qn), _mx(kn), _mx(kb_w)
        for h in range(N_HEADS):
            sl = slice(h * HEAD_DIM, (h + 1) * HEAD_DIM)
            lg = SM_DECAY + N_HEADS * d + h
            rhs = jnp.concatenate([vb_w[:, sl], kbe_w[:, sl]], axis=1)
            qg.append(qg_w[:, sl])
            kd.append(kd_w[:, sl])
            qh_m, kh_m, kb_m = qn_m[:, sl], kn_m[:, sl], kb_wm[:, sl]
            for s in range(nsub):
                rs = slice(s * sub, (s + 1) * sub)
                decay = jnp.exp(jnp.where(keep, gc[rs, lg:lg + 1] - gc_t[lg:lg + 1, rs], NEG_INF))
                a_list.append(jnp.where(strict, _dot_nt(kb_m[rs], kh_m[rs]) * decay, 0.0))
                rhs_list.append(rhs[rs])
                qkm.append(_dot_nt(qh_m[rs], kh_m[rs]) * decay)
    sols = _solve_unit_tri(a_list, rhs_list, rowi, coli, ck)
    sols = [jnp.concatenate(sols[n * nsub:(n + 1) * nsub], axis=0) for n in range(2 * N_HEADS)]

    chains = [(d, h) for d in range(2) for h in range(N_HEADS)]
    states = [state_s[d, h] for d, h in chains]
    v_new = [[None] * nck for _ in chains]
    o_st = [[None] * nck for _ in chains]
    for step in range(nck):
        rows = [slice((step if d == 0 else nck - 1 - step) * ck, (step if d == 0 else nck - 1 - step) * ck + ck)
                for d, _ in chains]
        ms = [_dot(jnp.concatenate([sols[n][r, HEAD_DIM:], qg[n][r]], axis=0), states[n])
              for n, r in enumerate(rows)]
        for n, (d, _) in enumerate(chains):
            c = step if d == 0 else nck - 1 - step
            v_new[n][c] = sols[n][rows[n], :HEAD_DIM] - ms[n][:ck]
            o_st[n][c] = ms[n][ck:]
        ups = [_dot_tn(kd[n][r], v_new[n][step if chains[n][0] == 0 else nck - 1 - step])
               for n, r in enumerate(rows)]
        for n, (d, h) in enumerate(chains):
            lg = SM_DECAY + N_HEADS * d + h
            states[n] = states[n] * e_last[d][rows[n].start:rows[n].start + 1, lg:lg + 1] + ups[n]
    cps = sub // ck
    outs = [jnp.concatenate(o_st[n], axis=0)
            + jnp.concatenate([_dot(qkm[n * nsub + s], jnp.concatenate(v_new[n][s * cps:(s + 1) * cps], axis=0))
                               for s in range(nsub)], axis=0)
            for n in range(len(chains))]
    of_ref[...] = jnp.concatenate(outs[:N_HEADS], axis=1)
    ob_ref[...] = jnp.concatenate(outs[N_HEADS:], axis=1)
    for n, (d, h) in enumerate(chains):
        state_s[d, h] = states[n]

    @pl.when(i == pl.num_programs(1) - 1)
    def _():
        sfin_ref[0] = state_s[...]


def gdn_params(a_log, dt_bias):
    return _small_vec(a_log, SM_DECAY), _small_vec(dt_bias, SM_DECAY)


def gdn_mixer(qkv, sm, s0, bsz, alog, dtb):
    t, c = qkv.shape
    s = t // bsz
    q = min(GDN_TILE, s)
    nc = s // q
    xf, xb = _chunk_specs(nc, q, c)
    sf, sb = _chunk_specs(nc, q, LANES)
    of, ob = _chunk_specs(nc, q, GROUP_W)
    st = pl.BlockSpec((1,) + s0.shape[1:], lambda b, i: (b, 0, 0, 0, 0))
    o_shape = jax.ShapeDtypeStruct((t, GROUP_W), F32)
    return pl.pallas_call(
        _gdn_kernel,
        grid=(bsz, nc),
        in_specs=[xf, xb, sf, sb, st, _full(alog.shape), _full(dtb.shape)],
        out_specs=[of, ob, st],
        out_shape=[o_shape, o_shape, jax.ShapeDtypeStruct(s0.shape, F32)],
        scratch_shapes=[pltpu.VMEM(s0.shape[1:], F32)],
        compiler_params=_cp("parallel", "arbitrary"),
        name="gdn_mixer",
    )(qkv, qkv, sm, sm, s0, alog, dtb)


def _split_hi_lo(a):
    hi = _mx(a)
    return hi, _mx(a - hi.astype(F32))


def _outproj_kernel(x_ref, ahf_ref, ahb_ref, ag_ref, bo_ref, cyf_ref, cyb_ref, cxc_ref, cz_ref,
                    dof_ref, dob_ref, dz_ref, wout_ref, gpost_ref, ga1_ref, gpre_ref, sc2_ref, sh2_ref,
                    dskip_ref, cnorm_ref, dnorm_ref, rhi_ref, rlo_ref, xo_ref, hp_ref, lg_ref):
    m_a = (ahf_ref[...] + ahb_ref[...]) * _gelu_tanh(ag_ref[...])
    y_c = (cyf_ref[...] + cyb_ref[...] + cxc_ref[...] * dskip_ref[...]) * _silu(cz_ref[...])
    m_c = _rms(y_c, cnorm_ref[...])
    o_d = dof_ref[...] + dob_ref[...]
    m_d = o_d * lax.rsqrt(_head_sums(o_d * o_d) * (1.0 / HEAD_DIM) + EPS) * dnorm_ref[...] * _silu(dz_ref[...])
    mix = jnp.concatenate([_mx(m_a), _mx(bo_ref[...]), _mx(m_c), _mx(m_d)], axis=1)
    ml = jnp.dot(mix, wout_ref[...], preferred_element_type=F32)
    x_new = x_ref[...] + ga1_ref[0] * _rms(ml, gpost_ref[...])
    xo_ref[...] = x_new
    h2 = _rms(x_new, gpre_ref[...]) * (1.0 + sc2_ref[0]) + sh2_ref[0]
    hi, lo = _split_hi_lo(h2)
    hp_ref[...] = _pack_pairs(h2)
    rhi = rhi_ref[...]
    lg_ref[...] = (jnp.dot(hi, rhi, preferred_element_type=F32) + jnp.dot(lo, rhi, preferred_element_type=F32)
                   + jnp.dot(hi, rlo_ref[...], preferred_element_type=F32))


def out_projection(x, mixers, w_out, gpost, ga1, gpre, sc2, sh2, dskip, cnorm, dnorm, router_w, tiles_per_group):
    t, d = x.shape
    tm = min(TOKEN_TILE, t)
    vec = lambda i: (i // tiles_per_group, 0, 0)
    row = lambda w: pl.BlockSpec((tm, w), lambda i: (i, 0))
    ne = LANES
    rhi, rlo = _split_hi_lo(jnp.pad(router_w.astype(F32), ((0, 0), (0, ne - router_w.shape[1]))))
    return pl.pallas_call(
        _outproj_kernel,
        grid=(t // tm,),
        in_specs=[row(d)] + [row(GROUP_W)] * 11
                 + [_full(w_out.shape), _full((1, d)), pl.BlockSpec((1, 1, d), vec), _full((1, d)),
                    pl.BlockSpec((1, 1, d), vec), pl.BlockSpec((1, 1, d), vec),
                    _full((1, GROUP_W)), _full((1, GROUP_W)), _full((1, GROUP_W)), _full(rhi.shape), _full(rlo.shape)],
        out_specs=[row(d), row(d // 2), row(ne)],
        out_shape=[jax.ShapeDtypeStruct((t, d), F32), jax.ShapeDtypeStruct((t, d // 2), jnp.uint32),
                   jax.ShapeDtypeStruct((t, ne), F32)],
        compiler_params=_cp("parallel"),
        name="out_projection",
    )(x, *mixers, w_out, gpost, ga1, gpre, sc2, sh2, dskip, cnorm, dnorm, rhi, rlo)


def _rank_before(vals, idx, count, stride):
    rank = jnp.zeros(vals.shape, jnp.int32)
    for j in range(count):
        other = vals[j * stride:j * stride + 1, :]
        ahead = (other > vals) | ((other == vals) & (idx > j))
        rank = rank + ahead.astype(jnp.int32)
    return rank


def _xor_partner(x, row, s):
    n = x.shape[0]
    return jnp.where((row & s) == 0, pltpu.roll(x, n - s, 0), pltpu.roll(x, s, 0))


def _route(logits, router_b):
    ne = N_EXPERTS
    gsz = ne // N_EXPERT_GROUPS
    scores = _sigmoid(logits.T[:ne, :])
    tm = scores.shape[1]
    biased = scores + router_b
    row = lax.broadcasted_iota(jnp.int32, (ne, tm), 0)
    m1, m2 = biased, jnp.full((ne, tm), -jnp.inf, F32)
    s = 1
    while s < gsz:
        o1, o2 = _xor_partner(m1, row, s), _xor_partner(m2, row, s)
        m2 = jnp.maximum(jnp.minimum(m1, o1), jnp.maximum(m2, o2))
        m1 = jnp.maximum(m1, o1)
        s *= 2
    gidx = row >> (gsz.bit_length() - 1)
    group_ok = _rank_before(m1 + m2, gidx, N_EXPERT_GROUPS, gsz) < TOPK_GROUPS
    choice = jnp.where(group_ok, biased, -jnp.inf)
    rank = jnp.full((ne, tm), TOP_K, jnp.int32)
    rest = choice
    for k in range(TOP_K):
        top = jnp.max(rest, axis=0, keepdims=True)
        first = jnp.min(jnp.where(rest == top, row, ne), axis=0, keepdims=True)
        hit = row == first
        rank = jnp.where(hit, k, rank)
        rest = jnp.where(hit, -jnp.inf, rest)
    gate = jnp.where(rank < TOP_K, scores, 0.0)
    gate = gate / jnp.sum(gate, axis=0, keepdims=True) * ROUTED_SCALE
    return gate, rank, row


def _to_token_major(x):
    n, tm = x.shape
    return jnp.concatenate([x, jnp.zeros((LANES - n, tm), x.dtype)], axis=0).T


def _router_kernel(lg_ref, rb_ref, gate_ref):
    gate, _, _ = _route(lg_ref[...], rb_ref[...])
    gate_ref[...] = _to_token_major(gate)


def _router_dispatch_kernel(lg_ref, rb_ref, gk_ref, ek_ref, pk_ref, cnt_ref, carry_s):
    i = pl.program_id(0)

    @pl.when(i == 0)
    def _():
        carry_s[...] = jnp.zeros(carry_s.shape, F32)

    gate, rank, row = _route(lg_ref[...], rb_ref[...])
    tm = gate.shape[1]
    picked = (rank < TOP_K).astype(F32)
    before = lax.broadcasted_iota(jnp.int32, (tm, tm), 0) < lax.broadcasted_iota(jnp.int32, (tm, tm), 1)
    pos = _dot(picked, before.astype(F32)) + carry_s[:, 0:1]
    carry_s[...] = carry_s[...] + jnp.sum(picked, axis=1, keepdims=True)
    gk, ek, pk = [], [], []
    for k in range(TOP_K):
        sel = rank == k
        gk.append(jnp.sum(jnp.where(sel, gate, 0.0), axis=0, keepdims=True))
        ek.append(jnp.sum(jnp.where(sel, row, 0), axis=0, keepdims=True))
        pk.append(jnp.sum(jnp.where(sel, pos, 0.0), axis=0, keepdims=True))
    gk_ref[...] = _to_token_major(jnp.concatenate(gk, axis=0))
    ek_ref[...] = jnp.concatenate(ek, axis=0)
    pk_ref[...] = jnp.concatenate(pk, axis=0).astype(jnp.int32)

    @pl.when(i == pl.num_programs(0) - 1)
    def _():
        cnt_ref[...] = carry_s[...].astype(jnp.int32)


def router_dispatch(logits, router_b):
    t, w = logits.shape
    tm = min(TOKEN_TILE, t)
    return pl.pallas_call(
        _router_dispatch_kernel,
        grid=(t // tm,),
        in_specs=[pl.BlockSpec((tm, w), lambda i: (i, 0)), _full((N_EXPERTS, 1))],
        out_specs=[pl.BlockSpec((tm, w), lambda i: (i, 0)),
                   pl.BlockSpec((TOP_K, tm), lambda i: (0, i)),
                   pl.BlockSpec((TOP_K, tm), lambda i: (0, i)),
                   _full((N_EXPERTS, LANES))],
        out_shape=[jax.ShapeDtypeStruct((t, w), F32), jax.ShapeDtypeStruct((TOP_K, t), jnp.int32),
                   jax.ShapeDtypeStruct((TOP_K, t), jnp.int32), jax.ShapeDtypeStruct((N_EXPERTS, LANES), jnp.int32)],
        scratch_shapes=[pltpu.VMEM((N_EXPERTS, LANES), F32)],
        compiler_params=_cp("arbitrary"),
        name="router_dispatch",
    )(logits, router_b.reshape(N_EXPERTS, 1).astype(F32))


def router_gates(logits, router_b):
    t, w = logits.shape
    tm = min(TOKEN_TILE, t)
    return pl.pallas_call(
        _router_kernel,
        grid=(t // tm,),
        in_specs=[pl.BlockSpec((tm, w), lambda i: (i, 0)), _full((N_EXPERTS, 1))],
        out_specs=pl.BlockSpec((tm, w), lambda i: (i, 0)),
        out_shape=jax.ShapeDtypeStruct((t, w), F32),
        compiler_params=_cp("parallel"),
        name="router_gates",
    )(logits, router_b.reshape(N_EXPERTS, 1).astype(F32))


def _moe_kernel(h_ref, gate_ref, x_ref, wg_ref, wu_ref, wd_ref, sg_ref, su_ref, sd_ref, gpost_ref, ga2_ref,
                o_ref, acc_s):
    e = pl.program_id(1)
    h = _mx(_unpack_pairs(h_ref[...]))

    @pl.when(e == 0)
    def _():
        hs = _silu(jnp.dot(h, sg_ref[...], preferred_element_type=F32)) * jnp.dot(h, su_ref[...], preferred_element_type=F32)
        acc_s[...] = jnp.dot(_mx(hs), sd_ref[...], preferred_element_type=F32)

    gates = gate_ref[...]
    lane = lax.broadcasted_iota(jnp.int32, gates.shape, 1)
    hid = []
    for j in range(MOE_EB):
        gcol = jnp.sum(jnp.where(lane == e * MOE_EB + j, gates, 0.0), axis=1, keepdims=True)
        g = jnp.dot(h, _mx(wg_ref[j]), preferred_element_type=F32)
        u = jnp.dot(h, _mx(wu_ref[j]), preferred_element_type=F32)
        hid.append(_mx(_silu(g) * u * gcol))
    wd = _mx(wd_ref[...]).reshape(MOE_EB * D_EXPERT, -1)
    acc_s[...] += jnp.dot(jnp.concatenate(hid, axis=1), wd, preferred_element_type=F32)

    @pl.when(e == pl.num_programs(1) - 1)
    def _():
        o_ref[...] = x_ref[...] + ga2_ref[0] * _rms(acc_s[...], gpost_ref[...])


def moe_ffn(h, gates, x, layer, wg, wu, wd, sg, su, sd, gpost, ga2, tiles_per_group):
    t, d = x.shape
    tm = min(MOE_TILE, t)
    _, ne, _, f = wg.shape
    row = lambda w: pl.BlockSpec((tm, w), lambda i, e: (i, 0))
    return pl.pallas_call(
        _moe_kernel,
        grid=(t // tm, ne // MOE_EB),
        in_specs=[row(h.shape[1]), row(gates.shape[1]), row(d),
                  pl.BlockSpec((None, MOE_EB, d, f), lambda i, e: (layer, e, 0, 0)),
                  pl.BlockSpec((None, MOE_EB, d, f), lambda i, e: (layer, e, 0, 0)),
                  pl.BlockSpec((None, MOE_EB, f, d), lambda i, e: (layer, e, 0, 0)),
                  _full(sg.shape), _full(su.shape), _full(sd.shape), _full((1, d)),
                  pl.BlockSpec((1, 1, d), lambda i, e: (i // tiles_per_group, 0, 0))],
        out_specs=row(d),
        out_shape=jax.ShapeDtypeStruct((t, d), F32),
        scratch_shapes=[pltpu.VMEM((tm, d), F32)],
        compiler_params=_cp("parallel", "arbitrary"),
        name="moe_ffn",
    )(h, gates, x, wg, wu, wd, sg, su, sd, gpost, ga2)


def moe_plan(counts, n_tokens):
    n_blocks = (n_tokens * TOP_K + N_EXPERTS * (MOE_BLOCK - 1) + MOE_BLOCK - 1) // MOE_BLOCK
    cnt = counts[:, 0]
    padded = (cnt + MOE_BLOCK - 1) // MOE_BLOCK * MOE_BLOCK
    pad_end = jnp.cumsum(padded)
    off = pad_end - padded
    start = jnp.arange(n_blocks, dtype=jnp.int32) * MOE_BLOCK
    be = jnp.minimum(jnp.sum(pad_end[None, :] <= start[:, None], axis=1), N_EXPERTS - 1).astype(jnp.int32)
    mine = be[:, None] == jnp.arange(N_EXPERTS, dtype=jnp.int32)[None, :]
    end = jnp.sum(jnp.where(mine, (off + cnt)[None, :], 0), axis=1)
    nv = jnp.clip(end - start, 0, MOE_BLOCK).astype(jnp.int32)
    return off.astype(jnp.int32), be, nv


def _rows_kernel(off_ref, ek_ref, pk_ref, dest_ref):
    ek = ek_ref[...]
    dest = pk_ref[...]
    for e in range(N_EXPERTS):
        dest = dest + jnp.where(ek == e, off_ref[e], 0)
    dest_ref[...] = dest


def moe_rows(off, ek, pk):
    k, t = ek.shape
    tm = min(MOE_PLAN_TILE, t)
    spec = pl.BlockSpec((k, tm), lambda i, off: (0, i))
    return pl.pallas_call(
        _rows_kernel,
        grid_spec=pltpu.PrefetchScalarGridSpec(num_scalar_prefetch=1, grid=(t // tm,),
                                               in_specs=[spec, spec], out_specs=spec),
        out_shape=jax.ShapeDtypeStruct((k, t), jnp.int32),
        compiler_params=_cp("arbitrary"),
        name="moe_rows",
    )(off, ek, pk)


U32 = jnp.uint32
HIGH_HALF = 0xFFFF0000


def _pack_pairs(x):
    w = x.shape[1] // 2
    bits = lax.bitcast_convert_type(x.astype(jnp.bfloat16).astype(F32), U32)
    return (bits[:, w:] & jnp.uint32(HIGH_HALF)) | (bits[:, :w] >> 16)


def _unpack_pairs(p):
    lo = lax.bitcast_convert_type(p << 16, F32)
    hi = lax.bitcast_convert_type(p & jnp.uint32(HIGH_HALF), F32)
    return jnp.concatenate([lo, hi], axis=1)


def _sc_workers():
    info = plsc.get_sparse_core_info()
    return info.num_cores, info.num_cores * info.num_subcores


def sc_scatter_rows(src, idx, n_rows):
    k, t = idx.shape
    w = src.shape[1]
    n_cores, n_workers = _sc_workers()
    per_worker = t // n_workers
    mesh = plsc.VectorSubcoreMesh(core_axis_name="c", subcore_axis_name="s")

    @functools.partial(
        pl.kernel, mesh=mesh, out_type=jax.ShapeDtypeStruct((n_rows, w), src.dtype),
        scratch_types=[pltpu.VMEM((k, SC_WINDOW), jnp.int32), pltpu.VMEM((SC_WINDOW, w), src.dtype),
                       pltpu.SemaphoreType.DMA])
    def scatter(s_hbm, i_hbm, o_hbm, idx_v, rows_v, sem):
        base = (lax.axis_index("s") * n_cores + lax.axis_index("c")) * per_worker

        @pl.loop(0, per_worker // SC_WINDOW)
        def _(j):
            off = base + j * SC_WINDOW
            pltpu.sync_copy(i_hbm.at[:, pl.ds(off, SC_WINDOW)], idx_v)
            pltpu.sync_copy(s_hbm.at[pl.ds(off, SC_WINDOW)], rows_v)
            for kk in range(k):
                pltpu.async_copy(rows_v, o_hbm.at[idx_v.at[kk]], sem).wait()

    return scatter(src, idx)


def sc_gather_rows(table, idx):
    n = idx.shape[0]
    w = table.shape[1]
    n_cores, n_workers = _sc_workers()
    per_worker = n // n_workers
    mesh = plsc.VectorSubcoreMesh(core_axis_name="c", subcore_axis_name="s")

    @functools.partial(
        pl.kernel, mesh=mesh, out_type=jax.ShapeDtypeStruct((n, w), table.dtype),
        scratch_types=[pltpu.VMEM((SC_WINDOW,), jnp.int32), pltpu.VMEM((SC_WINDOW, w), table.dtype),
                       pltpu.SemaphoreType.DMA])
    def gather(t_hbm, i_hbm, o_hbm, idx_v, rows_v, sem):
        base = (lax.axis_index("s") * n_cores + lax.axis_index("c")) * per_worker

        @pl.loop(0, per_worker // SC_WINDOW)
        def _(j):
            off = base + j * SC_WINDOW
            pltpu.sync_copy(i_hbm.at[pl.ds(off, SC_WINDOW)], idx_v)
            pltpu.async_copy(t_hbm.at[idx_v], rows_v, sem).wait()
            pltpu.sync_copy(rows_v, o_hbm.at[pl.ds(off, SC_WINDOW)])

    return gather(table, idx)


def _expert_kernel(be_ref, nv_ref, xs_ref, wg_ref, wu_ref, wd_ref, ys_ref):
    nv = nv_ref[pl.program_id(0)]

    def ffn(x):
        x = _mx(x)
        hid = (_silu(jnp.dot(x, _mx(wg_ref[0]), preferred_element_type=F32))
               * jnp.dot(x, _mx(wu_ref[0]), preferred_element_type=F32))
        ys_ref[...] = _pack_pairs(jnp.dot(_mx(hid), _mx(wd_ref[0]), preferred_element_type=F32))

    @pl.when(nv == MOE_BLOCK)
    def _():
        ffn(_unpack_pairs(xs_ref[...]))

    @pl.when((nv > 0) & (nv < MOE_BLOCK))
    def _():
        x = _unpack_pairs(xs_ref[...])
        rows = lax.broadcasted_iota(jnp.int32, x.shape, 0)
        ffn(jnp.where(rows < nv, x, 0.0))

    @pl.when(nv == 0)
    def _():
        ys_ref[...] = jnp.zeros(ys_ref.shape, U32)


def moe_experts(xs, be, nv, layer, wg, wu, wd):
    n_rows, w = xs.shape
    _, _, d, f = wg.shape
    return pl.pallas_call(
        _expert_kernel,
        grid_spec=pltpu.PrefetchScalarGridSpec(
            num_scalar_prefetch=2,
            grid=(n_rows // MOE_BLOCK,),
            in_specs=[pl.BlockSpec((MOE_BLOCK, w), lambda b, be, nv: (b, 0)),
                      pl.BlockSpec((None, 1, d, f), lambda b, be, nv: (layer, be[b], 0, 0)),
                      pl.BlockSpec((None, 1, d, f), lambda b, be, nv: (layer, be[b], 0, 0)),
                      pl.BlockSpec((None, 1, f, d), lambda b, be, nv: (layer, be[b], 0, 0))],
            out_specs=pl.BlockSpec((MOE_BLOCK, w), lambda b, be, nv: (b, 0))),
        out_shape=jax.ShapeDtypeStruct((n_rows, w), U32),
        compiler_params=_cp("arbitrary"),
        name="moe_experts",
    )(be, nv, xs, wg, wu, wd)


def _combine_kernel(yg_ref, gk_ref, hp_ref, x_ref, sg_ref, su_ref, sd_ref, gpost_ref, ga2_ref, o_ref):
    h = _mx(_unpack_pairs(hp_ref[...]))
    hs = _silu(jnp.dot(h, sg_ref[...], preferred_element_type=F32)) * jnp.dot(h, su_ref[...], preferred_element_type=F32)
    f = jnp.dot(_mx(hs), sd_ref[...], preferred_element_type=F32)
    gk = gk_ref[...]
    for k in range(TOP_K):
        f = f + gk[:, k:k + 1] * _unpack_pairs(yg_ref[k])
    o_ref[...] = x_ref[...] + ga2_ref[0] * _rms(f, gpost_ref[...])


def moe_combine(yg, gk, hp, x, sg, su, sd, gpost, ga2, tiles_per_group):
    t, d = x.shape
    tm = min(MOE_ROW_TILE, t)
    w = hp.shape[1]
    row = lambda n: pl.BlockSpec((tm, n), lambda i: (i, 0))
    return pl.pallas_call(
        _combine_kernel,
        grid=(t // tm,),
        in_specs=[pl.BlockSpec((TOP_K, tm, w), lambda i: (0, i, 0)),
                  row(gk.shape[1]), row(w), row(d), _full(sg.shape), _full(su.shape), _full(sd.shape), _full((1, d)),
                  pl.BlockSpec((1, 1, d), lambda i: (i // tiles_per_group, 0, 0))],
        out_specs=row(d),
        out_shape=jax.ShapeDtypeStruct((t, d), F32),
        compiler_params=_cp("parallel"),
        name="moe_combine",
    )(yg, gk, hp, x, sg, su, sd, gpost, ga2)


def _reorder_w_in(w_in):
    c = np.cumsum((0,) + (GROUP_W, GROUP_W, GROUP_W, GROUP_W, GROUP_W, GROUP_W, 2 * SSD_STATE, 2 * SSD_STATE,
                          GROUP_W, 2 * N_HEADS, GROUP_W, GROUP_W, GROUP_W, GROUP_W, 2 * N_HEADS, 2 * N_HEADS))
    seg = lambda a, b: w_in[:, c[a]:c[b]]
    small = jnp.concatenate([seg(9, 10), seg(14, 15), seg(15, 16),
                             jnp.zeros((w_in.shape[0], LANES - 6 * N_HEADS), w_in.dtype)], axis=1)
    return jnp.concatenate([seg(0, 1), seg(5, 8), seg(10, 13), seg(1, 5), seg(8, 9), seg(13, 14), small], axis=1)


def kernel(x, c, ctx, c_ctx, w_mod, b_mod, g_pre_mix, g_post_mix, g_pre_ffn, g_post_ffn, w_in, w_out, lru_conv_w, lru_conv_b, lru_wa, lru_ba, lru_wx, lru_bx, lru_lambda, na_bias, ssd_conv_w, ssd_conv_b, ssd_a_log, ssd_dt_bias, ssd_d, ssd_norm, gdn_conv_w, gdn_a_log, gdn_dt_bias, gdn_norm, router_w, router_b, we_gate, we_up, we_down, ws_gate, ws_up, ws_down):
    bsz, seq, d = x.shape
    n_ctx = ctx.shape[1]
    depth = w_mod.shape[0]
    lat_tpg = seq // min(TOKEN_TILE, seq)
    ctx_tpg = max(bsz * n_ctx // TOKEN_TILE, 1)
    ctx_mpg = max(bsz * n_ctx // MOE_TILE, 1)

    cond = _pad_rows(jnp.concatenate([c, c_ctx[None, :]], axis=0))
    mod = modulation(cond, w_mod, b_mod).reshape(depth, SUBLANES, N_MOD, d)
    rope = rope_tables(seq)
    row = lambda v: v[None, :].astype(F32)

    def layer_params(l):
        m_lat = [mod[l, :bsz, k][:, None, :] for k in range(N_MOD)]
        m_ctx = [mod[l, bsz:bsz + 1, k][:, None, :] for k in range(N_MOD)]
        w_in_l = _reorder_w_in(w_in[l]).astype(MXU_DTYPE)
        conv = (_pad_rows(lru_conv_w[l]), row(lru_conv_b[l]), _pad_rows(ssd_conv_w[l]), row(ssd_conv_b[l]),
                _pad_rows(gdn_conv_w[l]))
        return dict(
            m_lat=m_lat, m_ctx=m_ctx, w_in=w_in_l, conv=conv,
            lru=lru_params(lru_wa[l], lru_ba[l], lru_wx[l], lru_bx[l], lru_lambda[l]),
            ssd=ssd_params(ssd_a_log[l], ssd_dt_bias[l]), gdn=gdn_params(gdn_a_log[l], gdn_dt_bias[l]),
            epi=(w_out[l].astype(MXU_DTYPE), row(g_post_mix[l])),
            epi_tail=(row(jnp.repeat(ssd_d[l], HEAD_DIM)), row(ssd_norm[l]), row(jnp.tile(gdn_norm[l], N_HEADS)),
                      router_w[l]),
            routed=(l, we_gate, we_up, we_down),
            shared=(ws_gate[l].astype(MXU_DTYPE), ws_up[l].astype(MXU_DTYPE), ws_down[l].astype(MXU_DTYPE),
                    row(g_post_ffn[l])))

    def context_mixers(l, p, xc):
        pc = in_projection(xc, row(g_pre_mix[l]), p['m_ctx'][1], p['m_ctx'][0], p['w_in'], p['conv'], None, n_ctx,
                           bsz * n_ctx // min(TOKEN_TILE, n_ctx))
        a_f, a_b, a_st = lru_mixer(pc[P_AX], jnp.zeros((bsz, SUBLANES, GROUP_W), F32), bsz, *p['lru'])
        kc = pc[P_BK].reshape(bsz, n_ctx, GROUP_W)
        vc = pc[P_BV].reshape(bsz, n_ctx, GROUP_W)
        b_o = ctx_attention(pc[P_BQ].reshape(bsz, n_ctx, GROUP_W), kc, vc).reshape(bsz * n_ctx, GROUP_W)
        c_f, c_b, c_st = ssd_mixer(pc[P_CX], pc[P_SM], jnp.zeros((bsz, 2, N_HEADS, SSD_STATE, HEAD_DIM), F32),
                                   bsz, *p['ssd'])
        d_f, d_b, d_st = gdn_mixer(pc[P_DX], pc[P_SM], jnp.zeros((bsz, 2, N_HEADS, HEAD_DIM, HEAD_DIM), F32),
                                   bsz, *p['gdn'])
        mix = (a_f, a_b, pc[P_AG], b_o, c_f, c_b, pc[P_CX], pc[P_CZ], d_f, d_b, pc[P_DZ])
        return dict(mix=mix, a_st=a_st, kc=kc, vc=vc, c_st=c_st, d_st=d_st)

    def context_ffn(l, p, cm, xc):
        m = p['m_ctx']
        xc, hp, lg = out_projection(xc, cm['mix'], *p['epi'], m[2], row(g_pre_ffn[l]), m[4], m[3], *p['epi_tail'], ctx_tpg)
        return moe_ffn(hp, router_gates(lg, router_b[l]), xc, *p['routed'], *p['shared'], m[5], ctx_mpg)

    xl = x.reshape(bsz * seq, d)
    xc = ctx.reshape(bsz * n_ctx, d)
    p = layer_params(0)
    cm = context_mixers(0, p, xc)
    for l in range(depth):
        last = l == depth - 1
        m = p['m_lat']
        pl_ = in_projection(xl, row(g_pre_mix[l]), m[1], m[0], p['w_in'], p['conv'], rope, seq, lat_tpg)
        a_f, a_b, _ = lru_mixer(pl_[P_AX], cm['a_st'], bsz, *p['lru'])
        b_o = na_mixer(pl_[P_BQ], pl_[P_BK], pl_[P_BV], cm['kc'], cm['vc'], na_bias_slabs(na_bias[l]), bsz)
        c_f, c_b, _ = ssd_mixer(pl_[P_CX], pl_[P_SM], cm['c_st'], bsz, *p['ssd'])
        d_f, d_b, _ = gdn_mixer(pl_[P_DX], pl_[P_SM], cm['d_st'], bsz, *p['gdn'])
        mix_l = (a_f, a_b, pl_[P_AG], b_o, c_f, c_b, pl_[P_CX], pl_[P_CZ], d_f, d_b, pl_[P_DZ])
        xl, hp, lg = out_projection(xl, mix_l, *p['epi'], m[2], row(g_pre_ffn[l]), m[4], m[3], *p['epi_tail'], lat_tpg)
        gk, ek, pk, cnt = router_dispatch(lg, router_b[l])
        off, be, nv = moe_plan(cnt, bsz * seq)
        dest = moe_rows(off, ek, pk)
        xs = sc_scatter_rows(hp, dest, be.shape[0] * MOE_BLOCK)
        if not last:
            xc = context_ffn(l, p, cm, xc)
        ys = moe_experts(xs, be, nv, *p['routed'])
        yg = sc_gather_rows(ys, dest.reshape(-1)).reshape(TOP_K, bsz * seq, d // 2)
        if not last:
            p_next = layer_params(l + 1)
            cm = context_mixers(l + 1, p_next, xc)
        xl = moe_combine(yg, gk, hp, xl, *p['shared'], m[5], seq // min(MOE_ROW_TILE, seq))
        if not last:
            p = p_next
    return xl.reshape(bsz, seq, d)
```
